```python
import jax
import jax.numpy as jnp
from jax import lax
import numpy as np

D_MODEL = 1024
BATCH = 8
SEQ = 16384
DEPTH = 1

GRID_W = 64
CTX_LEN = 256
HEAD_DIM = 64
A_HEADS = 8
A_KV_HEADS = 2
A_GROUPS = A_HEADS // A_KV_HEADS
B_HEADS = 8
B_KV_HEADS = 2
B_GROUPS = B_HEADS // B_KV_HEADS
WINDOW = 128
BLOCK = 128
ROPE_THETA = 10000.0
D_FF = 2816
CONV_WIDTH = 3
LN_EPS = 1e-5
QK_EPS = 1e-6
N_MOD = 6
DEEPNORM_ALPHA = (2.0 * DEPTH) ** 0.25
DEEPNORM_BETA = (8.0 * DEPTH) ** -0.25

OFF_QA = 0
OFF_KA = OFF_QA + A_HEADS * HEAD_DIM
OFF_VA = OFF_KA + A_KV_HEADS * HEAD_DIM
OFF_QB = OFF_VA + A_KV_HEADS * HEAD_DIM
OFF_KB = OFF_QB + B_HEADS * HEAD_DIM
OFF_VB = OFF_KB + B_KV_HEADS * HEAD_DIM
OFF_GA = OFF_VB + B_KV_HEADS * HEAD_DIM
OFF_GB = OFF_GA + D_MODEL
IN_COLS = OFF_GB + D_MODEL

kernel_name = "hybrid_window_axial_gqa_convffn_dit_layer"


def _layer_norm(x, g, b):
    xf = x.astype(jnp.float32)
    mu = jnp.mean(xf, axis=-1, keepdims=True)
    var = jnp.mean(jnp.square(xf - mu), axis=-1, keepdims=True)
    return ((xf - mu) * lax.rsqrt(var + LN_EPS) * g + b).astype(x.dtype)


def _qk_rms(t, g):
    tf = t.astype(jnp.float32)
    return (tf * lax.rsqrt(jnp.mean(tf * tf, axis=-1, keepdims=True) + QK_EPS) * g).astype(t.dtype)


def _axial_rope_tables(n_tok, dtype):
    pos = jnp.arange(n_tok, dtype=jnp.int32)
    rows = (pos // GRID_W).astype(jnp.float32)
    cols = (pos % GRID_W).astype(jnp.float32)
    n_freq = HEAD_DIM // 4
    inv_freq = ROPE_THETA ** (-jnp.arange(n_freq, dtype=jnp.float32) / n_freq)
    ang_r = rows[:, None, None] * inv_freq
    ang_c = cols[:, None, None] * inv_freq
    return (jnp.cos(ang_r).astype(dtype), jnp.sin(ang_r).astype(dtype),
            jnp.cos(ang_c).astype(dtype), jnp.sin(ang_c).astype(dtype))


def _rotate_half(t, cos, sin):
    t1, t2 = jnp.split(t, 2, axis=-1)
    return jnp.concatenate([t1 * cos - t2 * sin, t2 * cos + t1 * sin], axis=-1)


def _rope_2d(t, tables):
    cos_r, sin_r, cos_c, sin_c = tables
    t_row, t_col = jnp.split(t, 2, axis=-1)
    return jnp.concatenate([_rotate_half(t_row, cos_r, sin_r), _rotate_half(t_col, cos_c, sin_c)], axis=-1)


def _gqa_scores(q, k):
    return jnp.einsum("bqhgd,bkhd->bhgqk", q, k).astype(jnp.float32) * (HEAD_DIM ** -0.5)


def _gqa_values(p, v):
    return jnp.einsum("bhgqk,bkhd->bqhgd", p.astype(v.dtype), v)


def _sink_softmax(scores, sink):
    sink = sink.astype(jnp.float32)[:, :, None, None]
    m = jnp.maximum(jnp.max(scores, axis=-1, keepdims=True), sink)
    e = jnp.exp(scores - m)
    return e / (jnp.sum(e, axis=-1, keepdims=True) + jnp.exp(sink - m))


def _windowed_attention(q, k, v, k_ctx, v_ctx, sink):
    bsz, n_tok = q.shape[:2]
    pad = ((0, 0), (BLOCK, BLOCK), (0, 0), (0, 0))
    k_pad = jnp.pad(k, pad)
    v_pad = jnp.pad(v, pad)
    ctx_mask = jnp.ones((BLOCK, k_ctx.shape[1]), dtype=bool)

    def one_block(i):
        start = i * BLOCK
        q_blk = lax.dynamic_slice_in_dim(q, start, BLOCK, axis=1)
        k_blk = jnp.concatenate([lax.dynamic_slice_in_dim(k_pad, start, 3 * BLOCK, axis=1), k_ctx], axis=1)
        v_blk = jnp.concatenate([lax.dynamic_slice_in_dim(v_pad, start, 3 * BLOCK, axis=1), v_ctx], axis=1)
        q_pos = start + jnp.arange(BLOCK)
        k_pos = start - BLOCK + jnp.arange(3 * BLOCK)
        band = ((jnp.abs(q_pos[:, None] - k_pos[None, :]) <= WINDOW)
                & (k_pos[None, :] >= 0) & (k_pos[None, :] < n_tok))
        mask = jnp.concatenate([band, ctx_mask], axis=1)
        scores = jnp.where(mask, _gqa_scores(q_blk, k_blk), -jnp.inf)
        return _gqa_values(_sink_softmax(scores, sink), v_blk)

    out = lax.map(one_block, jnp.arange(n_tok // BLOCK))
    return jnp.moveaxis(out, 0, 1).reshape(bsz, n_tok, -1)


def _global_attention(q, k_all, v_all):
    bsz, n_tok = q.shape[:2]

    def one_block(i):
        q_blk = lax.dynamic_slice_in_dim(q, i * BLOCK, BLOCK, axis=1)
        p = jax.nn.softmax(_gqa_scores(q_blk, k_all), axis=-1)
        return _gqa_values(p, v_all)

    out = lax.map(one_block, jnp.arange(n_tok // BLOCK))
    return jnp.moveaxis(out, 0, 1).reshape(bsz, n_tok, -1)


def _merge_branches(o_a, o_b, gate_logits, w_branch_a, w_branch_b, w_out):
    g_a, g_b = jnp.split(jax.nn.sigmoid(gate_logits), 2, axis=-1)
    return (g_a * (o_a @ w_branch_a) + g_b * (o_b @ w_branch_b)) @ w_out


def _context_kv(h_c, w_in, b_in, k_norm_g):
    bsz, n_ctx, _ = h_c.shape
    kv_a = (h_c @ w_in[:, OFF_KA:OFF_QB] + b_in[OFF_KA:OFF_QB]).reshape(bsz, n_ctx, 2 * A_KV_HEADS, HEAD_DIM)
    kv_b = (h_c @ w_in[:, OFF_KB:OFF_GA] + b_in[OFF_KB:OFF_GA]).reshape(bsz, n_ctx, 2 * B_KV_HEADS, HEAD_DIM)
    k_a, v_a = jnp.split(kv_a, 2, axis=2)
    k_b, v_b = jnp.split(kv_b, 2, axis=2)
    return (k_a, v_a, _qk_rms(k_b, k_norm_g), v_b)


def _latent_token_mixer(h, kv_ctx, w_in, b_in, sink, q_norm_g, k_norm_g,
                        w_branch_a, w_branch_b, w_out, rope):
    bsz, n_tok, _ = h.shape
    k_a_c, v_a_c, k_b_c, v_b_c = kv_ctx
    proj = h @ w_in + b_in

    def heads(lo, hi, n):
        return proj[..., lo:hi].reshape(bsz, n_tok, n, HEAD_DIM)

    q_a = _rope_2d(heads(OFF_QA, OFF_KA, A_HEADS), rope).reshape(bsz, n_tok, A_KV_HEADS, A_GROUPS, HEAD_DIM)
    k_a = _rope_2d(heads(OFF_KA, OFF_VA, A_KV_HEADS), rope)
    v_a = heads(OFF_VA, OFF_QB, A_KV_HEADS)
    q_b = _rope_2d(_qk_rms(heads(OFF_QB, OFF_KB, B_HEADS), q_norm_g), rope).reshape(
        bsz, n_tok, B_KV_HEADS, B_GROUPS, HEAD_DIM)
    k_b = _rope_2d(_qk_rms(heads(OFF_KB, OFF_VB, B_KV_HEADS), k_norm_g), rope)
    v_b = heads(OFF_VB, OFF_GA, B_KV_HEADS)
    o_a = _windowed_attention(q_a, k_a, v_a, k_a_c, v_a_c, sink.reshape(A_KV_HEADS, A_GROUPS))
    o_b = _global_attention(q_b, jnp.concatenate([k_b, k_b_c], axis=1), jnp.concatenate([v_b, v_b_c], axis=1))
    return _merge_branches(o_a, o_b, proj[..., OFF_GA:], w_branch_a, w_branch_b, w_out)


def _context_token_mixer(h_c, kv_ctx, w_in, b_in, sink, q_norm_g, w_branch_a, w_branch_b, w_out):
    bsz, n_ctx, _ = h_c.shape
    k_a_c, v_a_c, k_b_c, v_b_c = kv_ctx
    q_a = (h_c @ w_in[:, OFF_QA:OFF_KA] + b_in[OFF_QA:OFF_KA]).reshape(bsz, n_ctx, A_KV_HEADS, A_GROUPS, HEAD_DIM)
    q_b = _qk_rms((h_c @ w_in[:, OFF_QB:OFF_KB] + b_in[OFF_QB:OFF_KB]).reshape(bsz, n_ctx, B_HEADS, HEAD_DIM),
                  q_norm_g).reshape(bsz, n_ctx, B_KV_HEADS, B_GROUPS, HEAD_DIM)
    gate_logits = h_c @ w_in[:, OFF_GA:] + b_in[OFF_GA:]
    o_a = _gqa_values(_sink_softmax(_gqa_scores(q_a, k_a_c), sink.reshape(A_KV_HEADS, A_GROUPS)), v_a_c)
    o_b = _gqa_values(jax.nn.softmax(_gqa_scores(q_b, k_b_c), axis=-1), v_b_c)
    return _merge_branches(o_a.reshape(bsz, n_ctx, -1), o_b.reshape(bsz, n_ctx, -1), gate_logits,
                           w_branch_a, w_branch_b, w_out)


def _conv_ffn(h, w_up, conv_w, conv_b, w_down):
    u = h @ w_up
    half = CONV_WIDTH // 2
    u = lax.conv_general_dilated(u, conv_w[:, None, :], window_strides=(1,), padding=((half, half),),
                                 dimension_numbers=("NWC", "WIO", "NWC"),
                                 feature_group_count=u.shape[-1]) + conv_b
    gate, val = jnp.split(u, 2, axis=-1)
    return (jax.nn.silu(gate) * val) @ w_down


def _fwd_setup_inputs(seed: int = 0) -> dict:
    key = jax.random.key(seed)
    ks = jax.random.split(key, 24)
    f32 = jnp.float32

    def normal(k, shape, scale):
        return jax.random.normal(k, shape, f32) * scale

    L = DEPTH
    return {
        "x": normal(ks[0], (BATCH, SEQ, D_MODEL), 1.0),
        "c": normal(ks[1], (BATCH, D_MODEL), 1.0),
        "ctx": normal(ks[2], (BATCH, CTX_LEN, D_MODEL), 1.0),
        "c_ctx": normal(ks[3], (D_MODEL,), 1.0),
        "w_mod": normal(ks[4], (L, D_MODEL, N_MOD * D_MODEL), 0.5 * D_MODEL ** -0.5),
        "b_mod": normal(ks[5], (L, N_MOD * D_MODEL), 0.02),
        "w_in": normal(ks[6], (L, D_MODEL, IN_COLS), D_MODEL ** -0.5),
        "b_in": normal(ks[7], (L, IN_COLS), 0.02),
        "attn_sink": normal(ks[8], (L, A_HEADS), 0.5),
        "q_norm_g": 1.0 + normal(ks[9], (L, HEAD_DIM), 0.05),
        "k_norm_g": 1.0 + normal(ks[10], (L, HEAD_DIM), 0.05),
        "w_branch_a": normal(ks[11], (L, A_HEADS * HEAD_DIM, D_MODEL), (A_HEADS * HEAD_DIM) ** -0.5),
        "w_branch_b": normal(ks[12], (L, B_HEADS * HEAD_DIM, D_MODEL), (B_HEADS * HEAD_DIM) ** -0.5),
        "w_out": normal(ks[13], (L, D_MODEL, D_MODEL), DEEPNORM_BETA * D_MODEL ** -0.5),
        "ln1_g": 1.0 + normal(ks[14], (L, D_MODEL), 0.05),
        "ln1_b": normal(ks[15], (L, D_MODEL), 0.02),
        "w_up": normal(ks[16], (L, D_MODEL, 2 * D_FF), D_MODEL ** -0.5),
        "conv_w": normal(ks[17], (L, CONV_WIDTH, 2 * D_FF), CONV_WIDTH ** -0.5),
        "conv_b": normal(ks[18], (L, 2 * D_FF), 0.02),
        "w_down": normal(ks[19], (L, D_FF, D_MODEL), DEEPNORM_BETA * D_FF ** -0.5),
        "ln2_g": 1.0 + normal(ks[20], (L, D_MODEL), 0.05),
        "ln2_b": normal(ks[21], (L, D_MODEL), 0.02),
    }


def _fwd_reference(x, c, ctx, c_ctx, w_mod, b_mod, w_in, b_in, attn_sink, q_norm_g, k_norm_g,
              w_branch_a, w_branch_b, w_out, ln1_g, ln1_b, w_up, conv_w, conv_b, w_down,
              ln2_g, ln2_b):
    n_tok = x.shape[1]
    rope = _axial_rope_tables(n_tok, x.dtype)
    for l in range(DEPTH):
        last = l == DEPTH - 1
        mod = jax.nn.silu(c) @ w_mod[l] + b_mod[l]
        shift1, scale1, gate1, shift2, scale2, gate2 = jnp.split(mod[:, None, :], N_MOD, axis=-1)
        n_mod_c = 2 if last else N_MOD
        mod_c = jax.nn.silu(c_ctx) @ w_mod[l, :, :n_mod_c * D_MODEL] + b_mod[l, :n_mod_c * D_MODEL]
        mods_c = jnp.split(mod_c, n_mod_c)
        h_c = ctx * (1.0 + mods_c[1]) + mods_c[0]
        kv_ctx = _context_kv(h_c, w_in[l], b_in[l], k_norm_g[l])

        h = x * (1.0 + scale1) + shift1
        y = _latent_token_mixer(h, kv_ctx, w_in[l], b_in[l], attn_sink[l], q_norm_g[l], k_norm_g[l],
                                w_branch_a[l], w_branch_b[l], w_out[l], rope)
        x = _layer_norm(DEEPNORM_ALPHA * x + gate1 * y, ln1_g[l], ln1_b[l])

        h = x * (1.0 + scale2) + shift2
        y = _conv_ffn(h, w_up[l], conv_w[l], conv_b[l], w_down[l])
        x = _layer_norm(DEEPNORM_ALPHA * x + gate2 * y, ln2_g[l], ln2_b[l])

        if not last:
            y_c = _context_token_mixer(h_c, kv_ctx, w_in[l], b_in[l], attn_sink[l], q_norm_g[l],
                                       w_branch_a[l], w_branch_b[l], w_out[l])
            ctx = _layer_norm(DEEPNORM_ALPHA * ctx + mods_c[2] * y_c, ln1_g[l], ln1_b[l])
            h_c = ctx * (1.0 + mods_c[4]) + mods_c[3]
            y_c = _conv_ffn(h_c, w_up[l], conv_w[l], conv_b[l], w_down[l])
            ctx = _layer_norm(DEEPNORM_ALPHA * ctx + mods_c[5] * y_c, ln2_g[l], ln2_b[l])
    return x


import jax as _jax
import jax.numpy as _jnp

TWIN_FORMAT = 'train_step'
FWD_PARAMS = ['x', 'c', 'ctx', 'c_ctx', 'w_mod', 'b_mod', 'w_in', 'b_in', 'attn_sink', 'q_norm_g', 'k_norm_g', 'w_branch_a', 'w_branch_b', 'w_out', 'ln1_g', 'ln1_b', 'w_up', 'conv_w', 'conv_b', 'w_down', 'ln2_g', 'ln2_b']
TWIN_WEIGHTS = ['c_ctx', 'w_mod', 'b_mod', 'w_in', 'b_in', 'attn_sink', 'q_norm_g', 'k_norm_g', 'w_branch_a', 'w_branch_b', 'w_out', 'ln1_g', 'ln1_b', 'w_up', 'conv_w', 'conv_b', 'w_down', 'ln2_g', 'ln2_b']
TWIN_DIFF_INPUT = 'x'
TWIN_INPUTS = ['x', 'c', 'ctx', 'c_ctx', 'w_mod', 'b_mod', 'w_in', 'b_in', 'attn_sink', 'q_norm_g', 'k_norm_g', 'w_branch_a', 'w_branch_b', 'w_out', 'ln1_g', 'ln1_b', 'w_up', 'conv_w', 'conv_b', 'w_down', 'ln2_g', 'ln2_b', 'loss_target', 'm_c_ctx', 'm_w_mod', 'm_b_mod', 'm_w_in', 'm_b_in', 'm_attn_sink', 'm_q_norm_g', 'm_k_norm_g', 'm_w_branch_a', 'm_w_branch_b', 'm_w_out', 'm_ln1_g', 'm_ln1_b', 'm_w_up', 'm_conv_w', 'm_conv_b', 'm_w_down', 'm_ln2_g', 'm_ln2_b', 'v_c_ctx', 'v_w_mod', 'v_b_mod', 'v_w_in', 'v_b_in', 'v_attn_sink', 'v_q_norm_g', 'v_k_norm_g', 'v_w_branch_a', 'v_w_branch_b', 'v_w_out', 'v_ln1_g', 'v_ln1_b', 'v_w_up', 'v_conv_w', 'v_conv_b', 'v_w_down', 'v_ln2_g', 'v_ln2_b']
TWIN_OUTPUTS = ['loss', 'grad_x', 'grad_c_ctx', 'grad_w_mod', 'grad_b_mod', 'grad_w_in', 'grad_b_in', 'grad_attn_sink', 'grad_q_norm_g', 'grad_k_norm_g', 'grad_w_branch_a', 'grad_w_branch_b', 'grad_w_out', 'grad_ln1_g', 'grad_ln1_b', 'grad_w_up', 'grad_conv_w', 'grad_conv_b', 'grad_w_down', 'grad_ln2_g', 'grad_ln2_b', 'delta_c_ctx', 'delta_w_mod', 'delta_b_mod', 'delta_w_in', 'delta_b_in', 'delta_attn_sink', 'delta_q_norm_g', 'delta_k_norm_g', 'delta_w_branch_a', 'delta_w_branch_b', 'delta_w_out', 'delta_ln1_g', 'delta_ln1_b', 'delta_w_up', 'delta_conv_w', 'delta_conv_b', 'delta_w_down', 'delta_ln2_g', 'delta_ln2_b', 'new_m_c_ctx', 'new_m_w_mod', 'new_m_b_mod', 'new_m_w_in', 'new_m_b_in', 'new_m_attn_sink', 'new_m_q_norm_g', 'new_m_k_norm_g', 'new_m_w_branch_a', 'new_m_w_branch_b', 'new_m_w_out', 'new_m_ln1_g', 'new_m_ln1_b', 'new_m_w_up', 'new_m_conv_w', 'new_m_conv_b', 'new_m_w_down', 'new_m_ln2_g', 'new_m_ln2_b', 'new_v_c_ctx', 'new_v_w_mod', 'new_v_b_mod', 'new_v_w_in', 'new_v_b_in', 'new_v_attn_sink', 'new_v_q_norm_g', 'new_v_k_norm_g', 'new_v_w_branch_a', 'new_v_w_branch_b', 'new_v_w_out', 'new_v_ln1_g', 'new_v_ln1_b', 'new_v_w_up', 'new_v_conv_w', 'new_v_conv_b', 'new_v_w_down', 'new_v_ln2_g', 'new_v_ln2_b']
TWIN_LEAF_KINDS = {'loss': 'loss', 'grad_x': 'grad_x', 'grad_c_ctx': 'grad_w', 'grad_w_mod': 'grad_w', 'grad_b_mod': 'grad_w', 'grad_w_in': 'grad_w', 'grad_b_in': 'grad_w', 'grad_attn_sink': 'grad_w', 'grad_q_norm_g': 'grad_w', 'grad_k_norm_g': 'grad_w', 'grad_w_branch_a': 'grad_w', 'grad_w_branch_b': 'grad_w', 'grad_w_out': 'grad_w', 'grad_ln1_g': 'grad_w', 'grad_ln1_b': 'grad_w', 'grad_w_up': 'grad_w', 'grad_conv_w': 'grad_w', 'grad_conv_b': 'grad_w', 'grad_w_down': 'grad_w', 'grad_ln2_g': 'grad_w', 'grad_ln2_b': 'grad_w', 'delta_c_ctx': 'delta_w', 'delta_w_mod': 'delta_w', 'delta_b_mod': 'delta_w', 'delta_w_in': 'delta_w', 'delta_b_in': 'delta_w', 'delta_attn_sink': 'delta_w', 'delta_q_norm_g': 'delta_w', 'delta_k_norm_g': 'delta_w', 'delta_w_branch_a': 'delta_w', 'delta_w_branch_b': 'delta_w', 'delta_w_out': 'delta_w', 'delta_ln1_g': 'delta_w', 'delta_ln1_b': 'delta_w', 'delta_w_up': 'delta_w', 'delta_conv_w': 'delta_w', 'delta_conv_b': 'delta_w', 'delta_w_down': 'delta_w', 'delta_ln2_g': 'delta_w', 'delta_ln2_b': 'delta_w', 'new_m_c_ctx': 'new_m', 'new_m_w_mod': 'new_m', 'new_m_b_mod': 'new_m', 'new_m_w_in': 'new_m', 'new_m_b_in': 'new_m', 'new_m_attn_sink': 'new_m', 'new_m_q_norm_g': 'new_m', 'new_m_k_norm_g': 'new_m', 'new_m_w_branch_a': 'new_m', 'new_m_w_branch_b': 'new_m', 'new_m_w_out': 'new_m', 'new_m_ln1_g': 'new_m', 'new_m_ln1_b': 'new_m', 'new_m_w_up': 'new_m', 'new_m_conv_w': 'new_m', 'new_m_conv_b': 'new_m', 'new_m_w_down': 'new_m', 'new_m_ln2_g': 'new_m', 'new_m_ln2_b': 'new_m', 'new_v_c_ctx': 'new_v', 'new_v_w_mod': 'new_v', 'new_v_b_mod': 'new_v', 'new_v_w_in': 'new_v', 'new_v_b_in': 'new_v', 'new_v_attn_sink': 'new_v', 'new_v_q_norm_g': 'new_v', 'new_v_k_norm_g': 'new_v', 'new_v_w_branch_a': 'new_v', 'new_v_w_branch_b': 'new_v', 'new_v_w_out': 'new_v', 'new_v_ln1_g': 'new_v', 'new_v_ln1_b': 'new_v', 'new_v_w_up': 'new_v', 'new_v_conv_w': 'new_v', 'new_v_conv_b': 'new_v', 'new_v_w_down': 'new_v', 'new_v_ln2_g': 'new_v', 'new_v_ln2_b': 'new_v'}


def _forward(args):
    return _fwd_reference(*[args[k] for k in FWD_PARAMS])


def _output_shape():
    def fwd():
        inp = _fwd_setup_inputs(0)
        return _fwd_reference(*[inp[k] for k in FWD_PARAMS])
    out = _jax.eval_shape(fwd)
    return out.shape, out.dtype

N_MICROBATCH = 1
ADAM_LR = 0.001
ADAM_B1 = 0.9
ADAM_B2 = 0.999
ADAM_EPS = 1e-08
ADAM_WD = 0.01
ADAM_STEP = 10
PER_EXAMPLE_BATCH_AXIS = {'x': 0, 'c': 0, 'ctx': 0, 'loss_target': 0}
SHARED_INPUTS = []
_WEIGHT_DTYPES = {'c_ctx': _jnp.float32, 'w_mod': _jnp.float32, 'b_mod': _jnp.float32, 'w_in': _jnp.float32, 'b_in': _jnp.float32, 'attn_sink': _jnp.float32, 'q_norm_g': _jnp.float32, 'k_norm_g': _jnp.float32, 'w_branch_a': _jnp.float32, 'w_branch_b': _jnp.float32, 'w_out': _jnp.float32, 'ln1_g': _jnp.float32, 'ln1_b': _jnp.float32, 'w_up': _jnp.float32, 'conv_w': _jnp.float32, 'conv_b': _jnp.float32, 'w_down': _jnp.float32, 'ln2_g': _jnp.float32, 'ln2_b': _jnp.float32}
MOMENT_SCALE = {'c_ctx': 7.212045e-03, 'w_mod': 4.864107e-02, 'b_mod': 9.798368e-02, 'w_in': 9.536086e-03, 'b_in': 4.561500e-02, 'attn_sink': 8.082513e-05, 'q_norm_g': 9.969686e-03, 'k_norm_g': 1.017068e-02, 'w_branch_a': 1.079923e-02, 'w_branch_b': 1.362672e-02, 'w_out': 2.951194e-02, 'ln1_g': 1.170431e+01, 'ln1_b': 2.003594e+00, 'w_up': 2.452091e-02, 'conv_w': 2.446929e-02, 'conv_b': 2.465918e-02, 'w_down': 6.767492e-02, 'ln2_g': 1.289519e+02, 'ln2_b': 2.880447e+00}


def _to_microbatches(a, axis):
    t = _jnp.moveaxis(a, axis, 0)
    t = t.reshape((N_MICROBATCH, t.shape[0] // N_MICROBATCH) + t.shape[1:])
    return _jnp.moveaxis(t, 1, axis + 1)


def setup_inputs(seed: int = 0) -> dict:
    inp = _fwd_setup_inputs(seed)
    key = _jax.random.fold_in(_jax.random.key(seed), 7919)
    shape, _ = _output_shape()
    out = dict(inp)
    out["loss_target"] = _jax.random.normal(_jax.random.fold_in(key, 0), shape, _jnp.float32)
    for i, name in enumerate(TWIN_WEIGHTS):
        w = inp[name].astype(_jnp.float32)
        if MOMENT_SCALE is None:
            s = _jnp.sqrt(_jnp.mean(_jnp.square(w)) + 1e-30)
        else:
            s = MOMENT_SCALE[name]
        km, kv = _jax.random.split(_jax.random.fold_in(key, i + 1))
        out[name] = w
        out["m_" + name] = s * _jax.random.normal(km, w.shape, _jnp.float32)
        out["v_" + name] = (s * s) * _jax.random.uniform(kv, w.shape, _jnp.float32, 0.5, 1.5)
    if N_MICROBATCH > 1:
        for name, axis in PER_EXAMPLE_BATCH_AXIS.items():
            out[name] = _to_microbatches(out[name], axis)
    return {'x': out['x'], 'c': out['c'], 'ctx': out['ctx'], 'c_ctx': out['c_ctx'], 'w_mod': out['w_mod'], 'b_mod': out['b_mod'], 'w_in': out['w_in'], 'b_in': out['b_in'], 'attn_sink': out['attn_sink'], 'q_norm_g': out['q_norm_g'], 'k_norm_g': out['k_norm_g'], 'w_branch_a': out['w_branch_a'], 'w_branch_b': out['w_branch_b'], 'w_out': out['w_out'], 'ln1_g': out['ln1_g'], 'ln1_b': out['ln1_b'], 'w_up': out['w_up'], 'conv_w': out['conv_w'], 'conv_b': out['conv_b'], 'w_down': out['w_down'], 'ln2_g': out['ln2_g'], 'ln2_b': out['ln2_b'], 'loss_target': out['loss_target'], 'm_c_ctx': out['m_c_ctx'], 'm_w_mod': out['m_w_mod'], 'm_b_mod': out['m_b_mod'], 'm_w_in': out['m_w_in'], 'm_b_in': out['m_b_in'], 'm_attn_sink': out['m_attn_sink'], 'm_q_norm_g': out['m_q_norm_g'], 'm_k_norm_g': out['m_k_norm_g'], 'm_w_branch_a': out['m_w_branch_a'], 'm_w_branch_b': out['m_w_branch_b'], 'm_w_out': out['m_w_out'], 'm_ln1_g': out['m_ln1_g'], 'm_ln1_b': out['m_ln1_b'], 'm_w_up': out['m_w_up'], 'm_conv_w': out['m_conv_w'], 'm_conv_b': out['m_conv_b'], 'm_w_down': out['m_w_down'], 'm_ln2_g': out['m_ln2_g'], 'm_ln2_b': out['m_ln2_b'], 'v_c_ctx': out['v_c_ctx'], 'v_w_mod': out['v_w_mod'], 'v_b_mod': out['v_b_mod'], 'v_w_in': out['v_w_in'], 'v_b_in': out['v_b_in'], 'v_attn_sink': out['v_attn_sink'], 'v_q_norm_g': out['v_q_norm_g'], 'v_k_norm_g': out['v_k_norm_g'], 'v_w_branch_a': out['v_w_branch_a'], 'v_w_branch_b': out['v_w_branch_b'], 'v_w_out': out['v_w_out'], 'v_ln1_g': out['v_ln1_g'], 'v_ln1_b': out['v_ln1_b'], 'v_w_up': out['v_w_up'], 'v_conv_w': out['v_conv_w'], 'v_conv_b': out['v_conv_b'], 'v_w_down': out['v_w_down'], 'v_ln2_g': out['v_ln2_g'], 'v_ln2_b': out['v_ln2_b']}


def _loss(weights, diff, rest, loss_target):
    with _jax.named_scope("forward"):
        args = {**rest, TWIN_DIFF_INPUT: diff, **{k: w.astype(_WEIGHT_DTYPES[k]) for k, w in weights.items()}}
        y = _forward(args)
    with _jax.named_scope("loss_head"):
        err = _jnp.square(y.astype(_jnp.float32) - loss_target)
        return 0.5 * _jnp.sum(_jnp.mean(err, axis=-1)) if err.ndim else 0.5 * err


def _adamw(w, g, m, v):
    m = ADAM_B1 * m + (1.0 - ADAM_B1) * g
    v = ADAM_B2 * v + (1.0 - ADAM_B2) * _jnp.square(g)
    m_hat = m / (1.0 - ADAM_B1 ** ADAM_STEP)
    v_hat = v / (1.0 - ADAM_B2 ** ADAM_STEP)
    delta = -ADAM_LR * (m_hat / (_jnp.sqrt(v_hat) + ADAM_EPS) + ADAM_WD * w)
    return delta, m, v


def reference(x, c, ctx, c_ctx, w_mod, b_mod, w_in, b_in, attn_sink, q_norm_g, k_norm_g, w_branch_a, w_branch_b, w_out, ln1_g, ln1_b, w_up, conv_w, conv_b, w_down, ln2_g, ln2_b, loss_target, m_c_ctx, m_w_mod, m_b_mod, m_w_in, m_b_in, m_attn_sink, m_q_norm_g, m_k_norm_g, m_w_branch_a, m_w_branch_b, m_w_out, m_ln1_g, m_ln1_b, m_w_up, m_conv_w, m_conv_b, m_w_down, m_ln2_g, m_ln2_b, v_c_ctx, v_w_mod, v_b_mod, v_w_in, v_b_in, v_attn_sink, v_q_norm_g, v_k_norm_g, v_w_branch_a, v_w_branch_b, v_w_out, v_ln1_g, v_ln1_b, v_w_up, v_conv_w, v_conv_b, v_w_down, v_ln2_g, v_ln2_b):
    given = dict(x=x, c=c, ctx=ctx, c_ctx=c_ctx, w_mod=w_mod, b_mod=b_mod, w_in=w_in, b_in=b_in, attn_sink=attn_sink, q_norm_g=q_norm_g, k_norm_g=k_norm_g, w_branch_a=w_branch_a, w_branch_b=w_branch_b, w_out=w_out, ln1_g=ln1_g, ln1_b=ln1_b, w_up=w_up, conv_w=conv_w, conv_b=conv_b, w_down=w_down, ln2_g=ln2_g, ln2_b=ln2_b, loss_target=loss_target, m_c_ctx=m_c_ctx, m_w_mod=m_w_mod, m_b_mod=m_b_mod, m_w_in=m_w_in, m_b_in=m_b_in, m_attn_sink=m_attn_sink, m_q_norm_g=m_q_norm_g, m_k_norm_g=m_k_norm_g, m_w_branch_a=m_w_branch_a, m_w_branch_b=m_w_branch_b, m_w_out=m_w_out, m_ln1_g=m_ln1_g, m_ln1_b=m_ln1_b, m_w_up=m_w_up, m_conv_w=m_conv_w, m_conv_b=m_conv_b, m_w_down=m_w_down, m_ln2_g=m_ln2_g, m_ln2_b=m_ln2_b, v_c_ctx=v_c_ctx, v_w_mod=v_w_mod, v_b_mod=v_b_mod, v_w_in=v_w_in, v_b_in=v_b_in, v_attn_sink=v_attn_sink, v_q_norm_g=v_q_norm_g, v_k_norm_g=v_k_norm_g, v_w_branch_a=v_w_branch_a, v_w_branch_b=v_w_branch_b, v_w_out=v_w_out, v_ln1_g=v_ln1_g, v_ln1_b=v_ln1_b, v_w_up=v_w_up, v_conv_w=v_conv_w, v_conv_b=v_conv_b, v_w_down=v_w_down, v_ln2_g=v_ln2_g, v_ln2_b=v_ln2_b)
    weights = {n: given[n] for n in TWIN_WEIGHTS}
    shared = {n: given[n] for n in SHARED_INPUTS}
    per_example = {n: given[n] for n in ['x', 'c', 'ctx']}
    grad_fn = _jax.value_and_grad(_loss, argnums=(0, 1))

    def one_microbatch(ex, loss_target):
        ex = dict(ex)
        diff = ex.pop(TWIN_DIFF_INPUT)
        return grad_fn(weights, diff, {**shared, **ex}, loss_target)

    if N_MICROBATCH == 1:
        loss, (grad_w, grad_x) = one_microbatch(per_example, given["loss_target"])
    else:
        def body(carry, xs):
            loss_sum, grad_sum = carry
            l_k, (gw_k, gx_k) = one_microbatch(xs[0], xs[1])
            with _jax.named_scope("update"):
                return (loss_sum + l_k, _jax.tree.map(_jnp.add, grad_sum, gw_k)), gx_k

        init = (_jnp.zeros((), _jnp.float32), _jax.tree.map(_jnp.zeros_like, weights))
        (loss, grad_w), grad_x = _jax.lax.scan(body, init, (per_example, given["loss_target"]))
    with _jax.named_scope("update"):
        delta_w, new_m, new_v = {}, {}, {}
        for n in TWIN_WEIGHTS:
            delta_w[n], new_m[n], new_v[n] = _adamw(weights[n], grad_w[n], given["m_" + n], given["v_" + n])
    return (loss, grad_x, *[grad_w[n] for n in TWIN_WEIGHTS], *[delta_w[n] for n in TWIN_WEIGHTS],
            *[new_m[n] for n in TWIN_WEIGHTS], *[new_v[n] for n in TWIN_WEIGHTS])
```

```python
import functools

import jax
import jax.numpy as jnp
import numpy as np
from jax import lax
from jax.experimental import pallas as pl
from jax.experimental.pallas import tpu as pltpu

F32 = jnp.float32
BF16 = jnp.bfloat16
MXU_DTYPE = BF16

AXES = ("x", "y", "c")
N_DEV = 8
D = 1024
HEAD_DIM = 64
N_HEADS = 8
N_KV = 2
GROUPS = 4
KV_W = GROUPS * HEAD_DIM
Q_W = N_HEADS * HEAD_DIM
GRID_W = 64
WIN = 128
ROPE_THETA = 10000.0
D_FF = 2816
LN_EPS = 1e-5
QK_EPS = 1e-6
N_MOD = 6
ALPHA = 2.0 ** 0.25
Q_SCALE = HEAD_DIM ** -0.5
IN_COLS = 3584
OFF_KA, OFF_VA, OFF_QB, OFF_KB, OFF_VB, OFF_GA = 512, 640, 768, 1280, 1408, 1536
EXT_COLS = 6 * Q_W + 2 * D
X_QA, X_KA, X_VA, X_QB, X_KB, X_VB, X_GL = 0, 512, 1024, 1536, 2048, 2560, 3072
ADAM_LR, ADAM_B1, ADAM_B2, ADAM_EPS, ADAM_WD, ADAM_STEP = 0.001, 0.9, 0.999, 1e-08, 0.01, 10
LANES = 128
TM = 256
VMEM_LIMIT = 56 * 1024 * 1024
ELEMENTWISE_BLOCK_BYTES = 1 << 20

ANY = pl.BlockSpec(memory_space=pl.ANY)
SDS = jax.ShapeDtypeStruct


def _pick(n, candidates):
    for t in candidates:
        if n % t == 0:
            return t
    raise ValueError(f"no tile for {n}")


def _full(a):
    nd = a.ndim
    return pl.BlockSpec(a.shape, lambda *_: (0,) * nd)


def _rows(tm, w, fn=lambda t: t):
    return pl.BlockSpec((tm, w), lambda i: (fn(i), 0))


def _dot(a, b):
    return jnp.dot(a.astype(MXU_DTYPE), b.astype(MXU_DTYPE), preferred_element_type=F32)


def _dot_nt(a, b):
    return lax.dot_general(a.astype(MXU_DTYPE), b.astype(MXU_DTYPE), (((1,), (1,)), ((), ())), preferred_element_type=F32)


def _dot_tn(a, b):
    return lax.dot_general(a.astype(MXU_DTYPE), b.astype(MXU_DTYPE), (((0,), (0,)), ((), ())), preferred_element_type=F32)


def _cparams(sem):
    return pltpu.CompilerParams(dimension_semantics=sem, vmem_limit_bytes=VMEM_LIMIT)


def all_gather(name, v):
    r, w = v.shape

    def body(x_ref, out_ref, send_sems, recv_sems, local_sem):
        x, y, c = (lax.axis_index(a) for a in AXES)
        me, sibling = (x, y, c), (x, y, 1 - c)
        chips = [(1 - x, y), (x, 1 - y), (1 - x, 1 - y)]

        def rows(px, py, pc):
            return out_ref.at[4 * px + 2 * py + pc]

        def copy(k, block, to, src=None):
            return pltpu.make_async_remote_copy(
                src_ref=rows(*block) if src is None else src, dst_ref=rows(*block),
                send_sem=send_sems.at[k], recv_sem=recv_sems.at[k],
                device_id=to, device_id_type=pl.DeviceIdType.MESH)

        mine = pltpu.make_async_copy(x_ref, rows(*me), local_sem)
        mine.start()
        first = [copy(0, me, sibling, src=x_ref)]
        first += [copy(1 + j, me, (*chip, c), src=x_ref) for j, chip in enumerate(chips)]
        for cp in first:
            cp.start()
        passed = [copy(4 + j, (*chip, c), sibling) for j, chip in enumerate(chips)]
        for j, chip in enumerate(chips):
            copy(1 + j, (*chip, c), me).wait_recv()
            passed[j].start()
        copy(0, sibling, me).wait_recv()
        for j, chip in enumerate(chips):
            copy(4 + j, (*chip, 1 - c), me).wait_recv()
        for cp in first + passed:
            cp.wait_send()
        mine.wait()

    return pl.pallas_call(
        body, name=name, out_shape=SDS((N_DEV, r, w), v.dtype), in_specs=[ANY], out_specs=ANY,
        scratch_shapes=[pltpu.SemaphoreType.DMA((7,)), pltpu.SemaphoreType.DMA((7,)), pltpu.SemaphoreType.DMA],
    )(v)


def exchange(name, outbox, to_chips):
    k = outbox.shape[0]
    assert k == (3 if to_chips else 1)

    def body(out_ref, in_ref, send_sems, recv_sems):
        x, y, c = (lax.axis_index(a) for a in AXES)
        peers = [(x, 1 - y, c), (1 - x, y, c), (1 - x, 1 - y, c)] if to_chips else [(x, y, 1 - c)]
        copies = [
            pltpu.make_async_remote_copy(
                src_ref=out_ref.at[m], dst_ref=in_ref.at[m], send_sem=send_sems.at[m], recv_sem=recv_sems.at[m],
                device_id=peer, device_id_type=pl.DeviceIdType.MESH)
            for m, peer in enumerate(peers)
        ]
        for cp in copies:
            cp.start()
        for cp in copies:
            cp.wait_recv()
        for cp in copies:
            cp.wait_send()

    return pl.pallas_call(
        body, name=name, out_shape=SDS(outbox.shape, outbox.dtype), in_specs=[ANY], out_specs=ANY,
        scratch_shapes=[pltpu.SemaphoreType.DMA((k,)), pltpu.SemaphoreType.DMA((k,))],
    )(outbox)


def rowwise(name, body, *, ntiles, tile_off=0, tiled, full, outs, accs=()):
    nt, nf, no = len(tiled), len(full), len(outs)

    def kern(*refs):
        i = pl.program_id(0)
        out_vals, incs = body(i + tile_off, refs[:nt], refs[nt:nt + nf])
        for r, v in zip(refs[nt + nf:nt + nf + no], out_vals, strict=True):
            r[...] = v.astype(r.dtype)
        acc_refs = refs[nt + nf + no:]

        @pl.when(i == 0)
        def _():
            for r in acc_refs:
                r[...] = jnp.zeros_like(r)

        for r, v in zip(acc_refs, incs, strict=True):
            r[...] += v

    res = pl.pallas_call(
        kern, name=name, grid=(ntiles,),
        in_specs=[s for _, s in tiled] + [_full(a) for a in full],
        out_specs=[s for _, _, s in outs] + [pl.BlockSpec(s, lambda i, n=len(s): (0,) * n) for s in accs],
        out_shape=[SDS(s, d) for s, d, _ in outs] + [SDS(s, F32) for s in accs],
        compiler_params=_cparams(("arbitrary",) if accs else ("parallel",)),
    )(*[a for a, _ in tiled], *full)
    return res[:no], res[no:]


def mm_tn(name, a, b, rows):
    ka, nb = a.shape[1], b.shape[1]
    tr = _pick(rows, (1280, 1024, 768, 512, 256))
    tn = _pick(nb, (512, 256, 128))

    def kern(a_ref, b_ref, o_ref):
        @pl.when(pl.program_id(1) == 0)
        def _():
            o_ref[...] = jnp.zeros_like(o_ref)

        o_ref[...] += _dot_tn(a_ref[...], b_ref[...])

    return pl.pallas_call(
        kern, name=name, grid=(nb // tn, rows // tr),
        in_specs=[pl.BlockSpec((tr, ka), lambda n, r: (r, 0)), pl.BlockSpec((tr, tn), lambda n, r: (r, n))],
        out_specs=pl.BlockSpec((ka, tn), lambda n, r: (0, n)), out_shape=SDS((ka, nb), F32),
        compiler_params=_cparams(("parallel", "arbitrary")),
    )(a, b)


def _swap16(t):
    w = t.shape[1]
    lane = lax.broadcasted_iota(jnp.int32, t.shape, 1)
    return jnp.where((lane & 16) == 0, pltpu.roll(t, w - 16, 1), pltpu.roll(t, 16, 1))


def _rope(t, cos, sin):
    return t * cos + _swap16(t) * sin


def _rope_t(d, cos, sin):
    return d * cos - _swap16(d) * sin


def _seg_sum64(a, bd_ref):
    bd = bd_ref[...]
    hi = a.astype(BF16)
    lo = (a - hi.astype(F32)).astype(BF16)
    return jnp.dot(hi, bd, preferred_element_type=F32) + jnp.dot(lo, bd, preferred_element_type=F32)


def _lane_block(shape):
    return jnp.right_shift(lax.broadcasted_iota(jnp.int32, shape, 1), 6)


def _stack_groups(t, dtype):
    blk = _lane_block(t.shape)
    return jnp.concatenate([jnp.where(blk == g, t, jnp.zeros_like(t)).astype(dtype) for g in range(GROUPS)], axis=0)


def _fold_groups(ts, tq):
    blk = _lane_block((tq, KV_W))
    out = jnp.zeros((tq, KV_W), ts.dtype)
    for g in range(GROUPS):
        out = jnp.where(blk == g, ts[g * tq:(g + 1) * tq], out)
    return out


def _layer_norm_bwd(dxh, xhat, rstd):
    m1 = jnp.mean(dxh, axis=1, keepdims=True)
    m2 = jnp.mean(dxh * xhat, axis=1, keepdims=True)
    return rstd * (dxh - m1 - xhat * m2)


def _colsum(a):
    return jnp.sum(a, axis=0, keepdims=True)


def _shifted_rows(t, prev_row, next_row):
    n = t.shape[0]
    row = lax.broadcasted_iota(jnp.int32, t.shape, 0)
    up = jnp.where(row == 0, prev_row, pltpu.roll(t, 1, 0))
    dn = jnp.where(row == n - 1, next_row, pltpu.roll(t, n - 1, 0))
    return up, dn


def mod_fwd(cs, w_sh, b_sh):
    def kern(c_ref, w_ref, b_ref, o_ref):
        o_ref[...] = _dot(jax.nn.silu(c_ref[...]), w_ref[...]) + b_ref[...]

    return pl.pallas_call(kern, name="mod_fwd", out_shape=SDS((16, w_sh.shape[1]), F32),
                          compiler_params=pltpu.CompilerParams(vmem_limit_bytes=VMEM_LIMIT))(cs, w_sh, b_sh)


def mod_bwd(cs, w_sh, dm_sh, dm_all):
    hp = lax.Precision.HIGHEST

    def kern(c_ref, w_ref, dm_ref, da_ref, dw_ref, dc_ref, db_ref):
        c = c_ref[...]
        sg = jax.nn.sigmoid(c)
        sc = c * sg
        dm = dm_ref[...]
        dmc = dm_ref[8:9, :]
        for i in range(9, 16):
            dmc = dmc + dm_ref[i:i + 1, :]
        row = lax.broadcasted_iota(jnp.int32, dm.shape, 0)
        a = jnp.where(row < 8, dm, jnp.where(row == 8, dmc, 0.0))
        dw_ref[...] = lax.dot_general(sc, a, (((0,), (0,)), ((), ())), precision=hp, preferred_element_type=F32)
        dsc = lax.dot_general(a, w_ref[...], (((1,), (1,)), ((), ())), precision=hp, preferred_element_type=F32)
        dc_ref[...] = dsc * (sg * (1.0 + c * (1.0 - sg)))
        db = da_ref[0:1, :]
        for i in range(1, 16):
            db = db + da_ref[i:i + 1, :]
        db_ref[...] = db

    return pl.pallas_call(
        kern, name="mod_bwd",
        out_shape=[SDS(w_sh.shape, F32), SDS((16, D), F32), SDS((1, dm_all.shape[1]), F32)],
        compiler_params=pltpu.CompilerParams(vmem_limit_bytes=VMEM_LIMIT))(cs, w_sh, dm_sh, dm_all)


M_SHIFT1, M_SCALE1, M_SHIFTC, M_SCALEC, M_GATE1, M_SHIFT2, M_SCALE2, M_GATE2 = range(8)


def _mrow(ref, k):
    return ref[k:k + 1, :]


def inproj_fwd(xa, cos, sin, modrows, w_ext, b_ext, qg, kg, bd, n_lat_tiles):
    n = xa.shape[0]

    def body(t, vals, fr):
        x, cs, sn = (v[...] for v in vals)
        mod, w, b, qg_r, kg_r, bd_r = fr
        is_ctx = t >= n_lat_tiles
        shift = jnp.where(is_ctx, _mrow(mod, M_SHIFTC), _mrow(mod, M_SHIFT1))
        scale = jnp.where(is_ctx, _mrow(mod, M_SCALEC), _mrow(mod, M_SCALE1))
        hb = (x * (1.0 + scale) + shift).astype(MXU_DTYPE)
        proj = jnp.dot(hb, w[...], preferred_element_type=F32) + b[...]
        cos4 = jnp.concatenate([cs] * 4, axis=1)
        sin4 = jnp.concatenate([sn] * 4, axis=1)
        qa = _rope(proj[:, X_QA:X_QA + Q_W], cos4, sin4) * Q_SCALE
        ka = _rope(proj[:, X_KA:X_KA + Q_W], cos4, sin4)
        va = proj[:, X_VA:X_VA + Q_W]
        tq = proj[:, X_QB:X_QB + Q_W]
        rq = lax.rsqrt(_seg_sum64(tq * tq, bd_r) * (1.0 / HEAD_DIM) + QK_EPS)
        qb = _rope(tq * rq * qg_r[...], cos4, sin4) * Q_SCALE
        tk = proj[:, X_KB:X_KB + Q_W]
        rk = lax.rsqrt(_seg_sum64(tk * tk, bd_r) * (1.0 / HEAD_DIM) + QK_EPS)
        kb = _rope(tk * rk * kg_r[...], cos4, sin4)
        vb = proj[:, X_VB:X_VB + Q_W]
        gl = proj[:, X_GL:]
        return [hb, qa, ka, va, qb, kb, vb, tq, rq, tk, rk, gl], []

    mx = MXU_DTYPE
    outs = [((n, D), mx, _rows(TM, D))] + [((n, Q_W), mx, _rows(TM, Q_W))] * 6 + \
           [((n, Q_W), F32, _rows(TM, Q_W))] * 4 + [((n, 2 * D), F32, _rows(TM, 2 * D))]
    res, _ = rowwise("inproj_fwd", body, ntiles=n // TM,
                     tiled=[(xa, _rows(TM, D)), (cos, _rows(TM, LANES)), (sin, _rows(TM, LANES))],
                     full=[modrows, w_ext, b_ext, qg, kg, bd], outs=outs)
    return res


def merge_fwd(oa, ob, gl, x, modrows, wba, wbb, w_out, s_rows):
    def body(t, vals, fr):
        oa_, ob_, gl_, x_ = (v[...] for v in vals)
        mod, wa, wb, wo = fr
        ya = _dot(oa_, wa[...])
        yb = _dot(ob_, wb[...])
        ga = jax.nn.sigmoid(gl_[:, :D])
        gb = jax.nn.sigmoid(gl_[:, D:])
        mrg = ga * ya + gb * yb
        y = _dot(mrg, wo[...])
        r1 = ALPHA * x_ + _mrow(mod, M_GATE1) * y
        mu = jnp.mean(r1, axis=1, keepdims=True)
        xc = r1 - mu
        var = jnp.mean(xc * xc, axis=1, keepdims=True)
        rstd = lax.rsqrt(var + LN_EPS)
        xhat = xc * rstd
        return [ya, yb, mrg, y, xhat, rstd], []

    outs = [((s_rows, D), F32, _rows(TM, D))] * 2 + [((s_rows, D), MXU_DTYPE, _rows(TM, D))] + \
           [((s_rows, D), F32, _rows(TM, D))] * 2 + [((s_rows, 1), F32, _rows(TM, 1))]
    res, _ = rowwise("merge_fwd", body, ntiles=s_rows // TM,
                     tiled=[(oa, _rows(TM, Q_W)), (ob, _rows(TM, Q_W)), (gl, _rows(TM, 2 * D)), (x, _rows(TM, D))],
                     full=[modrows, wba, wbb, w_out], outs=outs)
    return res


def ffn_up_fwd(xhat1, modrows, ln_g, ln_b, w_up, s_rows):
    def body(t, vals, fr):
        xh = vals[0][...]
        mod, g_r, b_r, w = fr
        x1 = xh * g_r[...] + b_r[...]
        h2 = (x1 * (1.0 + _mrow(mod, M_SCALE2)) + _mrow(mod, M_SHIFT2)).astype(MXU_DTYPE)
        return [h2, jnp.dot(h2, w[...], preferred_element_type=F32)], []

    res, _ = rowwise("ffn_up_fwd", body, ntiles=s_rows // TM, tiled=[(xhat1, _rows(TM, D))],
                     full=[modrows, ln_g, ln_b, w_up],
                     outs=[((s_rows, D), MXU_DTYPE, _rows(TM, D)), ((s_rows, 2 * D_FF), F32, _rows(TM, 2 * D_FF))])
    return res


TC = 128


def _halo_specs(tm, w, s_rows):
    per = tm // 8
    last = s_rows // 8 - 1
    return (pl.BlockSpec((8, w), lambda i: (jnp.maximum(i * per - 1, 0), 0)),
            pl.BlockSpec((8, w), lambda i: (jnp.minimum((i + 1) * per, last), 0)))


def _halo_rows(t, ntiles, prev_ref, next_ref):
    prev_row = jnp.where(t == 0, 0.0, prev_ref[7:8, :].astype(F32))
    next_row = jnp.where(t == ntiles - 1, 0.0, next_ref[0:1, :].astype(F32))
    return prev_row, next_row


def conv_swiglu_fwd(u0, conv_w8, conv_b, s_rows):
    w2 = 2 * D_FF
    nt = s_rows // TC

    def body(t, vals, fr):
        u_ref, pv, nx = vals
        cw, cb = fr
        u = u_ref[...]
        up, dn = _shifted_rows(u, *_halo_rows(t, nt, pv, nx))
        uc = cw[0:1, :] * up + cw[1:2, :] * u + cw[2:3, :] * dn + cb[...]
        gate, val = uc[:, :D_FF], uc[:, D_FF:]
        return [gate * jax.nn.sigmoid(gate) * val], []

    hp, hn = _halo_specs(TC, w2, s_rows)
    res, _ = rowwise("conv_swiglu_fwd", body, ntiles=nt,
                     tiled=[(u0, _rows(TC, w2)), (u0, hp), (u0, hn)], full=[conv_w8, conv_b],
                     outs=[((s_rows, D_FF), MXU_DTYPE, _rows(TC, D_FF))])
    return res[0]


def ffn_down_loss(a, xhat1, target, modrows, ln1_g, ln1_b, ln2_g, ln2_b, w_down, s_rows):
    def body(t, vals, fr):
        a_, xh1, tgt = (v[...] for v in vals)
        mod, g1, b1, g2, b2, wd = fr
        y2 = jnp.dot(a_, wd[...], preferred_element_type=F32)
        x1 = xh1 * g1[...] + b1[...]
        gate2 = _mrow(mod, M_GATE2)
        r2 = ALPHA * x1 + gate2 * y2
        mu = jnp.mean(r2, axis=1, keepdims=True)
        xc = r2 - mu
        var = jnp.mean(xc * xc, axis=1, keepdims=True)
        rstd = lax.rsqrt(var + LN_EPS)
        xhat = xc * rstd
        out = xhat * g2[...] + b2[...]
        diff = out - tgt
        loss = 0.5 * jnp.sum(jnp.mean(diff * diff, axis=1, keepdims=True), axis=0, keepdims=True)
        dout = diff * (1.0 / D)
        dr2 = _layer_norm_bwd(dout * g2[...], xhat, rstd)
        incs = [loss, _colsum(dout * xhat), _colsum(dout), _colsum(dr2 * y2)]
        return [dr2, dr2 * gate2], incs

    res, accs = rowwise("ffn_down_loss", body, ntiles=s_rows // TM,
                        tiled=[(a, _rows(TM, D_FF)), (xhat1, _rows(TM, D)), (target, _rows(TM, D))],
                        full=[modrows, ln1_g, ln1_b, ln2_g, ln2_b, w_down],
                        outs=[((s_rows, D), F32, _rows(TM, D)), ((s_rows, D), MXU_DTYPE, _rows(TM, D))],
                        accs=[(1, 1), (1, D), (1, D), (1, D)])
    return res, accs


def ffn_down_bwd(dy2, w_down_t, s_rows):
    def body(t, vals, fr):
        return [jnp.dot(vals[0][...], fr[0][...], preferred_element_type=F32)], []

    res, _ = rowwise("ffn_down_bwd", body, ntiles=s_rows // TM, tiled=[(dy2, _rows(TM, D))], full=[w_down_t],
                     outs=[((s_rows, D_FF), F32, _rows(TM, D_FF))])
    return res[0]


def swiglu_conv_bwd(u0, da, conv_w8, conv_b, s_rows):
    w2 = 2 * D_FF
    nt = s_rows // TC

    def body(t, vals, fr):
        u_ref, pv, nx, da_ref = vals
        cw, cb = fr
        u, da_ = u_ref[...], da_ref[...]
        up, dn = _shifted_rows(u, *_halo_rows(t, nt, pv, nx))
        uc = cw[0:1, :] * up + cw[1:2, :] * u + cw[2:3, :] * dn + cb[...]
        gate, val = uc[:, :D_FF], uc[:, D_FF:]
        sg = jax.nn.sigmoid(gate)
        dgate = da_ * val * (sg * (1.0 + gate * (1.0 - sg)))
        dval = da_ * (gate * sg)
        du = jnp.concatenate([dgate, dval], axis=1)
        return [du], [_colsum(du), _colsum(up * du), _colsum(u * du), _colsum(dn * du)]

    hp, hn = _halo_specs(TC, w2, s_rows)
    res, accs = rowwise("swiglu_conv_bwd", body, ntiles=nt,
                        tiled=[(u0, _rows(TC, w2)), (u0, hp), (u0, hn), (da, _rows(TC, D_FF))],
                        full=[conv_w8, conv_b], outs=[((s_rows, w2), F32, _rows(TC, w2))], accs=[(1, w2)] * 4)
    return res[0], accs


def conv_bwd_input(du, conv_w8, s_rows):
    w2 = 2 * D_FF
    nt = s_rows // TC

    def body(t, vals, fr):
        d_ref, pv, nx = vals
        (cw,) = fr
        d = d_ref[...]
        up, dn = _shifted_rows(d, *_halo_rows(t, nt, pv, nx))
        return [cw[0:1, :] * dn + cw[1:2, :] * d + cw[2:3, :] * up], []

    hp, hn = _halo_specs(TC, w2, s_rows)
    res, _ = rowwise("conv_bwd_input", body, ntiles=nt, tiled=[(du, _rows(TC, w2)), (du, hp), (du, hn)],
                     full=[conv_w8], outs=[((s_rows, w2), MXU_DTYPE, _rows(TC, w2))])
    return res[0]


def ffn_up_ln1_bwd(du0, dr2, xhat1, y, rstd1, modrows, ln_g, ln_b, w_up_t, s_rows):
    def body(t, vals, fr):
        du0_, dr2_, xh, y_, rstd = (v[...] for v in vals)
        mod, g_r, b_r, wt = fr
        dh2 = jnp.dot(du0_, wt[...], preferred_element_type=F32)
        x1 = xh * g_r[...] + b_r[...]
        dx1 = ALPHA * dr2_ + dh2 * (1.0 + _mrow(mod, M_SCALE2))
        dr1 = _layer_norm_bwd(dx1 * g_r[...], xh, rstd)
        incs = [_colsum(dh2 * x1), _colsum(dh2), _colsum(dx1 * xh), _colsum(dx1), _colsum(dr1 * y_)]
        return [dr1 * _mrow(mod, M_GATE1), ALPHA * dr1], incs

    res, accs = rowwise("ffn_up_ln1_bwd", body, ntiles=s_rows // TM,
                        tiled=[(du0, _rows(TM, 2 * D_FF)), (dr2, _rows(TM, D)), (xhat1, _rows(TM, D)), (y, _rows(TM, D)),
                               (rstd1, _rows(TM, 1))],
                        full=[modrows, ln_g, ln_b, w_up_t],
                        outs=[((s_rows, D), MXU_DTYPE, _rows(TM, D)), ((s_rows, D), F32, _rows(TM, D))],
                        accs=[(1, D)] * 5)
    return res, accs


def merge_bwd(dy, ya, yb, gl, w_out_t, wba_t, wbb_t, s_rows):
    def body(t, vals, fr):
        dy_, ya_, yb_, gl_ = (v[...] for v in vals)
        wot, wat, wbt = fr
        dmrg = jnp.dot(dy_, wot[...], preferred_element_type=F32)
        ga = jax.nn.sigmoid(gl_[:, :D])
        gb = jax.nn.sigmoid(gl_[:, D:])
        dya = dmrg * ga
        dyb = dmrg * gb
        dgl = jnp.concatenate([dmrg * ya_ * ga * (1.0 - ga), dmrg * yb_ * gb * (1.0 - gb)], axis=1)
        return [dya, dyb, dgl, _dot(dya, wat[...]), _dot(dyb, wbt[...])], []

    mx = MXU_DTYPE
    res, _ = rowwise("merge_bwd", body, ntiles=s_rows // TM,
                     tiled=[(dy, _rows(TM, D)), (ya, _rows(TM, D)), (yb, _rows(TM, D)), (gl, _rows(TM, 2 * D))],
                     full=[w_out_t, wba_t, wbb_t],
                     outs=[((s_rows, D), mx, _rows(TM, D))] * 2 + [((s_rows, 2 * D), F32, _rows(TM, 2 * D))] +
                          [((s_rows, Q_W), F32, _rows(TM, Q_W))] * 2)
    return res


def qk_bwd(dqa, dka, dkax, dva, dvax, dqb, dkb, dvb, dgl, tq, rq, tk, rk, cos, sin, qg, kg, bd, n_lat_tiles, n):
    def body(t, vals, fr):
        dqa_, dka_, dkax_, dva_, dvax_, dqb_, dkb_, dvb_, dgl_, tq_, rq_, tk_, rk_, cs, sn = (v[...] for v in vals)
        qg_r, kg_r, bd_r = fr
        is_ctx = t >= n_lat_tiles
        cos4 = jnp.concatenate([cs] * 4, axis=1)
        sin4 = jnp.concatenate([sn] * 4, axis=1)
        zero = jnp.zeros_like(dqa_)
        dpqa = jnp.where(is_ctx, zero, _rope_t(dqa_, cos4, sin4) * Q_SCALE)
        dpka = _rope_t(jnp.where(is_ctx, dkax_, dka_), cos4, sin4)
        dpva = jnp.where(is_ctx, dvax_, dva_)
        dnq = jnp.where(is_ctx, zero, _rope_t(dqb_, cos4, sin4) * Q_SCALE)
        gq = qg_r[...] * dnq
        dtq = rq_ * gq - tq_ * (rq_ * rq_ * rq_) * (_seg_sum64(gq * tq_, bd_r) * (1.0 / HEAD_DIM))
        dnk = _rope_t(dkb_, cos4, sin4)
        gk = kg_r[...] * dnk
        dtk = rk_ * gk - tk_ * (rk_ * rk_ * rk_) * (_seg_sum64(gk * tk_, bd_r) * (1.0 / HEAD_DIM))
        dgl32 = jnp.where(is_ctx, jnp.zeros_like(dgl_), dgl_)
        dproj = jnp.concatenate([dpqa, dpka, dpva, dtq, dtk, dvb_, dgl32], axis=1)
        return [dproj], [_colsum(dproj), _colsum(dnq * tq_ * rq_), _colsum(dnk * tk_ * rk_)]

    lat = lambda t: jnp.minimum(t, n_lat_tiles - 1)
    cx = lambda t: jnp.maximum(t - n_lat_tiles, 0)
    qs = _rows(TM, Q_W)
    res, accs = rowwise(
        "qk_bwd", body, ntiles=n // TM,
        tiled=[(dqa, _rows(TM, Q_W, lat)), (dka, _rows(TM, Q_W, lat)), (dkax, _rows(TM, Q_W, cx)),
               (dva, _rows(TM, Q_W, lat)), (dvax, _rows(TM, Q_W, cx)), (dqb, _rows(TM, Q_W, lat)),
               (dkb, qs), (dvb, qs), (dgl, _rows(TM, 2 * D, lat)), (tq, qs), (rq, qs), (tk, qs), (rk, qs),
               (cos, _rows(TM, LANES)), (sin, _rows(TM, LANES))],
        full=[qg, kg, bd], outs=[((n, EXT_COLS), MXU_DTYPE, _rows(TM, EXT_COLS))],
        accs=[(1, EXT_COLS), (1, Q_W), (1, Q_W)])
    return res[0], accs


def inproj_bwd(name, dproj, xa, dxp, modrows, w_ext_t, *, ntiles, tile_off, is_ctx, out_rows):
    kc = M_SCALEC if is_ctx else M_SCALE1

    def body(t, vals, fr):
        dp, x_ = vals[0][...], vals[1][...]
        mod, wt = fr
        dh = jnp.dot(dp, wt[...], preferred_element_type=F32)
        incs = [_colsum(dh * x_), _colsum(dh)]
        if is_ctx:
            return [], incs
        return [vals[2][...] + dh * (1.0 + _mrow(mod, kc))], incs

    tiled = [(dproj, _rows(TM, EXT_COLS, lambda i: i + tile_off)), (xa, _rows(TM, D, lambda i: i + tile_off))]
    outs = []
    if not is_ctx:
        tiled.append((dxp, _rows(TM, D)))
        outs = [((out_rows, D), F32, _rows(TM, D))]
    return rowwise(name, body, ntiles=ntiles, tiled=tiled, full=[modrows, w_ext_t], outs=outs, accs=[(1, D)] * 2)


def _attn_semantics():
    return _cparams(("arbitrary", "arbitrary", "arbitrary"))


def glob_fwd(q, kt, vt, s_rows):
    n = kt.shape[0]
    tq = TM
    tk = _pick(n, (640, 512, 384, 256, 128))
    nq, nk = s_rows // tq, n // tk

    def kern(q_ref, k_ref, v_ref, o_ref, lse_ref, qs, m_s, l_s, acc):
        j = pl.program_id(2)

        @pl.when(j == 0)
        def _():
            qs[...] = _stack_groups(q_ref[...], qs.dtype)
            m_s[...] = jnp.full_like(m_s, -jnp.inf)
            l_s[...] = jnp.zeros_like(l_s)
            acc[...] = jnp.zeros_like(acc)

        s = _dot_nt(qs[...], k_ref[...])
        m_prev = m_s[...]
        m_new = jnp.maximum(m_prev, jnp.max(s, axis=1, keepdims=True))
        alpha = jnp.exp(m_prev - m_new)
        p = jnp.exp(s - m_new)
        l_s[...] = alpha * l_s[...] + jnp.sum(p, axis=1, keepdims=True)
        acc[...] = alpha * acc[...] + _dot(p, v_ref[...])
        m_s[...] = m_new

        @pl.when(j == nk - 1)
        def _():
            o_ref[...] = _fold_groups(acc[...] / l_s[...], tq)
            lse_ref[0, 0] = m_s[...] + jnp.log(l_s[...])

    return pl.pallas_call(
        kern, name="glob_fwd", grid=(N_KV, nq, nk),
        in_specs=[pl.BlockSpec((tq, KV_W), lambda h, i, j: (i, h)), pl.BlockSpec((tk, KV_W), lambda h, i, j: (j, h)),
                  pl.BlockSpec((tk, KV_W), lambda h, i, j: (j, h))],
        out_specs=[pl.BlockSpec((tq, KV_W), lambda h, i, j: (i, h)),
                   pl.BlockSpec((1, 1, GROUPS * tq, 1), lambda h, i, j: (h, i, 0, 0))],
        out_shape=[SDS((s_rows, Q_W), F32), SDS((N_KV, nq, GROUPS * tq, 1), F32)],
        scratch_shapes=[pltpu.VMEM((GROUPS * tq, KV_W), MXU_DTYPE), pltpu.VMEM((GROUPS * tq, 1), F32),
                        pltpu.VMEM((GROUPS * tq, 1), F32), pltpu.VMEM((GROUPS * tq, KV_W), F32)],
        compiler_params=_attn_semantics(),
    )(q, kt, vt)


def glob_bwd_dq(q, kt, vt, o, do, lse, s_rows):
    n = kt.shape[0]
    tq = TM
    tk = _pick(n, (640, 512, 384, 256, 128))
    nq, nk = s_rows // tq, n // tk

    def kern(q_ref, k_ref, v_ref, o_ref, do_ref, lse_ref, dq_ref, delta_ref, qs, dos, dl, acc):
        j = pl.program_id(2)

        @pl.when(j == 0)
        def _():
            qs[...] = _stack_groups(q_ref[...], qs.dtype)
            do32 = _stack_groups(do_ref[...], F32)
            dl[...] = jnp.sum(do32 * _stack_groups(o_ref[...], F32), axis=1, keepdims=True)
            dos[...] = do32.astype(dos.dtype)
            acc[...] = jnp.zeros_like(acc)

        k = k_ref[...]
        p = jnp.exp(_dot_nt(qs[...], k) - lse_ref[0, 0])
        ds = p * (_dot_nt(dos[...], v_ref[...]) - dl[...])
        acc[...] += _dot(ds, k)

        @pl.when(j == nk - 1)
        def _():
            dq_ref[...] = _fold_groups(acc[...], tq)
            delta_ref[0, 0] = dl[...]

    col = pl.BlockSpec((1, 1, GROUPS * tq, 1), lambda h, i, j: (h, i, 0, 0))
    qspec = pl.BlockSpec((tq, KV_W), lambda h, i, j: (i, h))
    kspec = pl.BlockSpec((tk, KV_W), lambda h, i, j: (j, h))
    return pl.pallas_call(
        kern, name="glob_bwd_dq", grid=(N_KV, nq, nk),
        in_specs=[qspec, kspec, kspec, qspec, qspec, col], out_specs=[qspec, col],
        out_shape=[SDS((s_rows, Q_W), F32), SDS((N_KV, nq, GROUPS * tq, 1), F32)],
        scratch_shapes=[pltpu.VMEM((GROUPS * tq, KV_W), MXU_DTYPE), pltpu.VMEM((GROUPS * tq, KV_W), MXU_DTYPE),
                        pltpu.VMEM((GROUPS * tq, 1), F32), pltpu.VMEM((GROUPS * tq, KV_W), F32)],
        compiler_params=_attn_semantics(),
    )(q, kt, vt, o, do, lse)


def glob_bwd_dkv(q, kt, vt, do, lse_row, delta_row, s_rows):
    n = kt.shape[0]
    tq = TM
    tk = _pick(n, (640, 512, 384, 256, 128))
    nq, nk = s_rows // tq, n // tk

    def kern(q_ref, k_ref, v_ref, do_ref, lse_ref, delta_ref, dk_ref, dv_ref, dk_acc, dv_acc):
        i = pl.program_id(2)

        @pl.when(i == 0)
        def _():
            dk_acc[...] = jnp.zeros_like(dk_acc)
            dv_acc[...] = jnp.zeros_like(dv_acc)

        qs = _stack_groups(q_ref[...], MXU_DTYPE)
        dos = _stack_groups(do_ref[...], MXU_DTYPE)
        pt = jnp.exp(_dot_nt(k_ref[...], qs) - lse_ref[0, 0])
        dv_acc[...] += _dot(pt, dos)
        dst = pt * (_dot_nt(v_ref[...], dos) - delta_ref[0, 0])
        dk_acc[...] += _dot(dst, qs)

        @pl.when(i == nq - 1)
        def _():
            dk_ref[...] = dk_acc[...]
            dv_ref[...] = dv_acc[...]

    rowv = pl.BlockSpec((1, 1, 1, GROUPS * tq), lambda h, j, i: (h, i, 0, 0))
    qspec = pl.BlockSpec((tq, KV_W), lambda h, j, i: (i, h))
    kspec = pl.BlockSpec((tk, KV_W), lambda h, j, i: (j, h))
    return pl.pallas_call(
        kern, name="glob_bwd_dkv", grid=(N_KV, nk, nq),
        in_specs=[qspec, kspec, kspec, qspec, rowv, rowv], out_specs=[kspec, kspec],
        out_shape=[SDS((n, Q_W), F32), SDS((n, Q_W), F32)],
        scratch_shapes=[pltpu.VMEM((tk, KV_W), F32), pltpu.VMEM((tk, KV_W), F32)],
        compiler_params=_attn_semantics(),
    )(q, kt, vt, do, lse_row, delta_row)


WQ = GROUPS * WIN


def _win_kv_specs(s_rows, c_rows):
    nb = s_rows // WIN
    blk = lambda f: pl.BlockSpec((WIN, KV_W), lambda h, i: (f(i), h))
    return [blk(lambda i: jnp.maximum(i - 1, 0)), blk(lambda i: i), blk(lambda i: jnp.minimum(i + 1, nb - 1)),
            pl.BlockSpec((c_rows, KV_W), lambda h, i: (s_rows // c_rows, h))]


def _win_mask(i, s_rows, shape):
    row = lax.broadcasted_iota(jnp.int32, shape, 0)
    col = lax.broadcasted_iota(jnp.int32, shape, 1)
    qpos = i * WIN + (row & (WIN - 1))
    kpos = (i - 1) * WIN + col
    band = (jnp.abs(qpos - kpos) <= WIN) & (kpos >= 0) & (kpos < s_rows)
    return (col >= 3 * WIN) | band


def _win_cat(dst, parts):
    off = 0
    for p in parts:
        dst[off:off + p.shape[0], :] = p[...]
        off += p.shape[0]


def win_fwd(q, kt, vt, sinkcol, s_rows, c_rows):
    nb = s_rows // WIN
    nkeys = 3 * WIN + c_rows

    def kern(q_ref, kp, kc, kn, kx, vp, vc, vn, vx, sink_ref, o_ref, lse_ref, kcat, vcat):
        i = pl.program_id(1)
        _win_cat(kcat, (kp, kc, kn, kx))
        _win_cat(vcat, (vp, vc, vn, vx))
        qs = _stack_groups(q_ref[...], MXU_DTYPE)
        s = _dot_nt(qs, kcat[...])
        s = jnp.where(_win_mask(i, s_rows, s.shape), s, -jnp.inf)
        sink = sink_ref[0][:, 0:1]
        m = jnp.maximum(jnp.max(s, axis=1, keepdims=True), sink)
        e = jnp.exp(s - m)
        den = jnp.sum(e, axis=1, keepdims=True) + jnp.exp(sink - m)
        o_ref[...] = _fold_groups(_dot(e / den, vcat[...]), WIN)
        lse_ref[0, 0] = m + jnp.log(den)

    kv = _win_kv_specs(s_rows, c_rows)
    qspec = pl.BlockSpec((WIN, KV_W), lambda h, i: (i, h))
    col = pl.BlockSpec((1, 1, WQ, 1), lambda h, i: (h, i, 0, 0))
    return pl.pallas_call(
        kern, name="win_fwd", grid=(N_KV, nb),
        in_specs=[qspec] + kv + kv + [pl.BlockSpec((1, WQ, LANES), lambda h, i: (h, 0, 0))],
        out_specs=[qspec, col], out_shape=[SDS((s_rows, Q_W), F32), SDS((N_KV, nb, WQ, 1), F32)],
        scratch_shapes=[pltpu.VMEM((nkeys, KV_W), MXU_DTYPE), pltpu.VMEM((nkeys, KV_W), MXU_DTYPE)],
        compiler_params=_cparams(("arbitrary", "arbitrary")),
    )(q, kt, kt, kt, kt, vt, vt, vt, vt, sinkcol)


def win_bwd_dq(q, kt, vt, sinkcol, o, do, lse, s_rows, c_rows):
    nb = s_rows // WIN
    nkeys = 3 * WIN + c_rows

    def kern(q_ref, kp, kc, kn, kx, vp, vc, vn, vx, sink_ref, o_ref, do_ref, lse_ref,
             dq_ref, delta_ref, dkx_ref, dvx_ref, dsk_ref, kcat, vcat):
        i = pl.program_id(1)

        @pl.when(i == 0)
        def _():
            dkx_ref[...] = jnp.zeros_like(dkx_ref)
            dvx_ref[...] = jnp.zeros_like(dvx_ref)
            dsk_ref[...] = jnp.zeros_like(dsk_ref)

        _win_cat(kcat, (kp, kc, kn, kx))
        _win_cat(vcat, (vp, vc, vn, vx))
        qs = _stack_groups(q_ref[...], MXU_DTYPE)
        do32 = _stack_groups(do_ref[...], F32)
        delta = jnp.sum(do32 * _stack_groups(o_ref[...], F32), axis=1, keepdims=True)
        dos = do32.astype(MXU_DTYPE)
        lse_c = lse_ref[0, 0]
        s = _dot_nt(qs, kcat[...])
        s = jnp.where(_win_mask(i, s_rows, s.shape), s, -jnp.inf)
        p = jnp.exp(s - lse_c)
        ds = p * (_dot_nt(dos, vcat[...]) - delta)
        dq_ref[...] = _fold_groups(_dot(ds, kcat[...]), WIN)
        delta_ref[0, 0] = delta
        dkx_ref[0] += _dot_tn(ds[:, 3 * WIN:], qs)
        dvx_ref[0] += _dot_tn(p[:, 3 * WIN:], dos)
        dsk_ref[0] += -(jnp.exp(sink_ref[0][:, 0:1] - lse_c) * delta)

    kv = _win_kv_specs(s_rows, c_rows)
    qspec = pl.BlockSpec((WIN, KV_W), lambda h, i: (i, h))
    col = pl.BlockSpec((1, 1, WQ, 1), lambda h, i: (h, i, 0, 0))
    xspec = pl.BlockSpec((1, c_rows, KV_W), lambda h, i: (h, 0, 0))
    return pl.pallas_call(
        kern, name="win_bwd_dq", grid=(N_KV, nb),
        in_specs=[qspec] + kv + kv + [pl.BlockSpec((1, WQ, LANES), lambda h, i: (h, 0, 0)), qspec, qspec, col],
        out_specs=[qspec, col, xspec, xspec, pl.BlockSpec((1, WQ, 1), lambda h, i: (h, 0, 0))],
        out_shape=[SDS((s_rows, Q_W), F32), SDS((N_KV, nb, WQ, 1), F32), SDS((N_KV, c_rows, KV_W), F32),
                   SDS((N_KV, c_rows, KV_W), F32), SDS((N_KV, WQ, 1), F32)],
        scratch_shapes=[pltpu.VMEM((nkeys, KV_W), MXU_DTYPE), pltpu.VMEM((nkeys, KV_W), MXU_DTYPE)],
        compiler_params=_cparams(("arbitrary", "arbitrary")),
    )(q, kt, kt, kt, kt, vt, vt, vt, vt, sinkcol, o, do, lse)


def win_bwd_dkv(q, kt, vt, do, lse, delta, s_rows):
    nb = s_rows // WIN

    def kern(k_ref, v_ref, qp, qc, qn, dop, doc, don, lp, lc, ln, dp_, dc_, dn_, dk_ref, dv_ref):
        j = pl.program_id(1)
        k = k_ref[...]
        v = v_ref[...]
        dk = jnp.zeros((WIN, KV_W), F32)
        dv = jnp.zeros((WIN, KV_W), F32)
        for b, (q_r, do_r, l_r, d_r) in enumerate(((qp, dop, lp, dp_), (qc, doc, lc, dc_), (qn, don, ln, dn_))):
            ib = j - 1 + b
            qs = _stack_groups(q_r[...], MXU_DTYPE)
            dos = _stack_groups(do_r[...], MXU_DTYPE)
            s = _dot_nt(qs, k)
            row = lax.broadcasted_iota(jnp.int32, s.shape, 0)
            col = lax.broadcasted_iota(jnp.int32, s.shape, 1)
            near = jnp.abs(ib * WIN + (row & (WIN - 1)) - (j * WIN + col)) <= WIN
            ok = near & (ib >= 0) & (ib < nb)
            p = jnp.where(ok, jnp.exp(s - l_r[0, 0]), 0.0)
            dv = dv + _dot_tn(p, dos)
            ds = p * (_dot_nt(dos, v) - d_r[0, 0])
            dk = dk + _dot_tn(ds, qs)
        dk_ref[...] = dk
        dv_ref[...] = dv

    fs = (lambda j: jnp.maximum(j - 1, 0), lambda j: j, lambda j: jnp.minimum(j + 1, nb - 1))
    qspecs = [pl.BlockSpec((WIN, KV_W), lambda h, j, f=f: (f(j), h)) for f in fs]
    cols = [pl.BlockSpec((1, 1, WQ, 1), lambda h, j, f=f: (h, f(j), 0, 0)) for f in fs]
    kspec = pl.BlockSpec((WIN, KV_W), lambda h, j: (j, h))
    return pl.pallas_call(
        kern, name="win_bwd_dkv", grid=(N_KV, nb),
        in_specs=[kspec, kspec] + qspecs + qspecs + cols + cols, out_specs=[kspec, kspec],
        out_shape=[SDS((s_rows, Q_W), F32), SDS((s_rows, Q_W), F32)],
        compiler_params=_cparams(("arbitrary", "arbitrary")),
    )(kt, vt, q, q, q, do, do, do, lse, lse, lse, delta, delta, delta)


def adamw(name, w, m, v, grads):
    r, wd = w.shape
    tr = _pick(r, [t for t in (1408, 1024, 512, 256, 128, 64, 32, 16, 8) if t * wd * 4 <= ELEMENTWISE_BLOCK_BYTES])
    stacked = not isinstance(grads, (list, tuple))
    ng = grads.shape[0] if stacked else len(grads)

    def kern(*refs):
        w_ref, m_ref, v_ref = refs[:3]
        g_refs = refs[3:-4]
        g_out, d_out, m_out, v_out = refs[-4:]
        if stacked:
            g = g_refs[0][0]
            for k in range(1, ng):
                g = g + g_refs[0][k]
        else:
            g = g_refs[0][...]
            for gr in g_refs[1:]:
                g = g + gr[...]
        wv = w_ref[...]
        mn = ADAM_B1 * m_ref[...] + (1.0 - ADAM_B1) * g
        vn = ADAM_B2 * v_ref[...] + (1.0 - ADAM_B2) * (g * g)
        m_hat = mn / (1.0 - ADAM_B1 ** ADAM_STEP)
        v_hat = vn / (1.0 - ADAM_B2 ** ADAM_STEP)
        g_out[...] = g
        d_out[...] = -ADAM_LR * (m_hat / (jnp.sqrt(v_hat) + ADAM_EPS) + ADAM_WD * wv)
        m_out[...] = mn
        v_out[...] = vn

    spec = pl.BlockSpec((tr, wd), lambda i: (i, 0))
    gspecs = [pl.BlockSpec((ng, tr, wd), lambda i: (0, i, 0))] if stacked else [spec] * ng
    return pl.pallas_call(
        kern, name=name, grid=(r // tr,), in_specs=[spec] * 3 + gspecs, out_specs=[spec] * 4,
        out_shape=[SDS((r, wd), F32)] * 4, compiler_params=_cparams(("parallel",)),
    )(w, m, v, *([grads] if stacked else grads))


def add2(name, a, b):
    k, r, w = a.shape
    tr = _pick(r, (1408, 1024, 512, 256, 128, 64, 32, 16, 8))

    def kern(a_ref, b_ref, o_ref):
        o_ref[...] = a_ref[...] + b_ref[...]

    spec = pl.BlockSpec((1, tr, w), lambda s, i: (s, i, 0))
    return pl.pallas_call(kern, name=name, grid=(k, r // tr), in_specs=[spec, spec], out_specs=spec,
                          out_shape=SDS(a.shape, a.dtype), compiler_params=_cparams(("parallel", "parallel")))(a, b)


def _rep4(a, off):
    return jnp.concatenate([a[:, off + HEAD_DIM * h: off + HEAD_DIM * (h + 1)] for h in range(N_KV) for _ in range(GROUPS)], axis=1)


def _extend_cols(a):
    return jnp.concatenate([a[:, 0:OFF_KA], _rep4(a, OFF_KA), _rep4(a, OFF_VA), a[:, OFF_QB:OFF_KB],
                            _rep4(a, OFF_KB), _rep4(a, OFF_VB), a[:, OFF_GA:]], axis=1)


def _fold4(a, off):
    r = a.shape[0]
    return a[:, off:off + Q_W].reshape(r, N_KV, GROUPS, HEAD_DIM).sum(axis=2).reshape(r, N_KV * HEAD_DIM)


def _fold_cols(a):
    return jnp.concatenate([a[:, X_QA:X_QA + Q_W], _fold4(a, X_KA), _fold4(a, X_VA), a[:, X_QB:X_QB + Q_W],
                            _fold4(a, X_KB), _fold4(a, X_VB), a[:, X_GL:]], axis=1)


def _rope_tables(s_rows, c_rows):
    pos = jnp.arange(s_rows, dtype=jnp.int32)
    rows = (pos // GRID_W).astype(F32)
    cols = (pos % GRID_W).astype(F32)
    n_freq = HEAD_DIM // 4
    inv_freq = ROPE_THETA ** (-jnp.arange(n_freq, dtype=F32) / n_freq)
    ang_r = rows[:, None] * inv_freq
    ang_c = cols[:, None] * inv_freq
    cos = jnp.concatenate([jnp.cos(ang_r)] * 2 + [jnp.cos(ang_c)] * 2, axis=1)
    sin = jnp.concatenate([-jnp.sin(ang_r), jnp.sin(ang_r), -jnp.sin(ang_c), jnp.sin(ang_c)], axis=1)
    cos = jnp.concatenate([cos, jnp.ones((c_rows, HEAD_DIM), F32)], axis=0)
    sin = jnp.concatenate([sin, jnp.zeros((c_rows, HEAD_DIM), F32)], axis=0)
    return jnp.concatenate([cos, cos], axis=1), jnp.concatenate([sin, sin], axis=1)


BIG = (("w_in", (D, IN_COLS // N_DEV)), ("w_branch_a", (Q_W, D // N_DEV)), ("w_branch_b", (Q_W, D // N_DEV)),
       ("w_out", (D // N_DEV, D)), ("w_up", (D, 2 * D_FF // N_DEV)), ("w_down", (D_FF // N_DEV, D)))
BIG_SIZES = tuple(int(np.prod(s)) for _, s in BIG)
BIG_ROWS = sum(BIG_SIZES) // LANES


def _pack_big(parts):
    return jnp.concatenate([p.reshape(-1) for p in parts]).reshape(BIG_ROWS, LANES)


def _unpack_big(flat):
    lead = flat.shape[:-2]
    f = flat.reshape(*lead, BIG_ROWS * LANES)
    out, off = [], 0
    for (_, shp), sz in zip(BIG, BIG_SIZES, strict=True):
        out.append(f[..., off:off + sz].reshape(*lead, *shp))
        off += sz
    return out


def _cols_to_full(g):
    return jnp.transpose(g, (1, 0, 2)).reshape(g.shape[1], -1)


def _full_to_cols(a):
    r, c = a.shape
    return jnp.transpose(a.reshape(r, N_DEV, c // N_DEV), (1, 0, 2))


SMALL = (("c_ctx", D), ("b_mod", N_MOD * D), ("b_in", IN_COLS), ("attn_sink", N_HEADS), ("q_norm_g", HEAD_DIM),
         ("k_norm_g", HEAD_DIM), ("ln1_g", D), ("ln1_b", D), ("conv_w", 3 * 2 * D_FF // N_DEV), ("conv_b", 2 * D_FF),
         ("ln2_g", D), ("ln2_b", D))
SMALL_TOTAL = sum(n for _, n in SMALL)
SMALL_ROWS = -(-SMALL_TOTAL // (8 * LANES)) * 8


def _pack_small(parts):
    flat = jnp.concatenate([p.reshape(-1).astype(F32) for p in parts])
    return jnp.pad(flat, (0, SMALL_ROWS * LANES - flat.shape[0])).reshape(SMALL_ROWS, LANES)


def _unpack_small(packed):
    f = packed.reshape(-1)
    out, off = {}, 0
    for name, n in SMALL:
        out[name] = f[off:off + n]
        off += n
    return out


RED = (("c_ctx", D), ("b_in", IN_COLS), ("attn_sink", N_HEADS), ("q_norm_g", HEAD_DIM), ("k_norm_g", HEAD_DIM),
       ("ln1_g", D), ("ln1_b", D), ("conv_w", 3 * 2 * D_FF), ("conv_b", 2 * D_FF), ("ln2_g", D), ("ln2_b", D))
RED_TOTAL = sum(n for _, n in RED)
RED_ROWS = -(-RED_TOTAL // (8 * LANES)) * 8


def sum8(name, g):
    _, r, w = g.shape

    def kern(g_ref, o_ref):
        acc = g_ref[0]
        for k in range(1, N_DEV):
            acc = acc + g_ref[k]
        o_ref[...] = acc

    return pl.pallas_call(kern, name=name, out_shape=SDS((r, w), F32))(g)


def _local_step(x, ctx, target, modrows, weights, small):
    s_rows, c_rows = x.shape[0], ctx.shape[0]
    n = s_rows + c_rows
    nl = s_rows // TM
    w_in, wba, wbb, w_out, w_up, w_down = weights
    f = lambda a: a.reshape(1, -1).astype(F32)
    b_in, ln1_g, ln1_b, ln2_g, ln2_b, conv_b = (f(small[k]) for k in ("b_in", "ln1_g", "ln1_b", "ln2_g", "ln2_b", "conv_b"))
    conv_w8 = jnp.pad(small["conv_w_full"], ((0, 5), (0, 0)))
    qg = jnp.tile(small["q_norm_g"].reshape(1, HEAD_DIM), (1, N_HEADS))
    kg = jnp.tile(small["k_norm_g"].reshape(1, HEAD_DIM), (1, N_HEADS))
    sinkcol = jnp.broadcast_to(jnp.repeat(small["attn_sink"].reshape(N_KV, GROUPS), WIN, axis=1)[:, :, None], (N_KV, WQ, LANES))
    bd = jnp.kron(jnp.eye(N_HEADS, dtype=F32), jnp.ones((HEAD_DIM, HEAD_DIM), F32)).astype(BF16)
    cos, sin = _rope_tables(s_rows, c_rows)
    w_ext = _extend_cols(w_in)
    b_ext = _extend_cols(b_in)
    xa = jnp.concatenate([x, ctx], axis=0)

    hb, qa, kat, vat, qb, kbt, vbt, tq, rq, tk, rk, gl = inproj_fwd(xa, cos, sin, modrows, w_ext, b_ext, qg, kg, bd, nl)
    oa, lse_a = win_fwd(qa, kat, vat, sinkcol, s_rows, c_rows)
    ob, lse_b = glob_fwd(qb, kbt, vbt, s_rows)
    ya, yb, mrg, y, xhat1, rstd1 = merge_fwd(oa, ob, gl, x, modrows, wba, wbb, w_out, s_rows)
    h2, u0 = ffn_up_fwd(xhat1, modrows, ln1_g, ln1_b, w_up, s_rows)
    a = conv_swiglu_fwd(u0, conv_w8, conv_b, s_rows)
    (dr2, dy2), (loss, dln2_g, dln2_b, dgate2) = ffn_down_loss(a, xhat1, target, modrows, ln1_g, ln1_b, ln2_g, ln2_b, w_down, s_rows)

    da = ffn_down_bwd(dy2, w_down.T, s_rows)
    dw_down = mm_tn("dw_down", a, dy2, s_rows)
    du, (dconv_b, dcw0, dcw1, dcw2) = swiglu_conv_bwd(u0, da, conv_w8, conv_b, s_rows)
    du0 = conv_bwd_input(du, conv_w8, s_rows)
    dw_up = mm_tn("dw_up", h2, du0, s_rows)
    (dy, dxp), (dscale2, dshift2, dln1_g, dln1_b, dgate1) = ffn_up_ln1_bwd(du0, dr2, xhat1, y, rstd1, modrows, ln1_g, ln1_b, w_up.T, s_rows)
    dya, dyb, dgl, doa, dob = merge_bwd(dy, ya, yb, gl, w_out.T, wba.T, wbb.T, s_rows)
    dw_out = mm_tn("dw_out", mrg, dy, s_rows)
    dwba = mm_tn("dw_branch_a", oa, dya, s_rows)
    dwbb = mm_tn("dw_branch_b", ob, dyb, s_rows)

    dqa, delta_a, dkax, dvax, dsk = win_bwd_dq(qa, kat, vat, sinkcol, oa, doa, lse_a, s_rows, c_rows)
    dka, dva = win_bwd_dkv(qa, kat, vat, doa, lse_a, delta_a, s_rows)
    dqb, delta_b = glob_bwd_dq(qb, kbt, vbt, ob, dob, lse_b, s_rows)
    to_row = lambda t: t.reshape(t.shape[0], t.shape[1], 1, t.shape[2])
    dkb, dvb = glob_bwd_dkv(qb, kbt, vbt, dob, to_row(lse_b), to_row(delta_b), s_rows)
    ctx_cols = lambda t: jnp.transpose(t, (1, 0, 2)).reshape(c_rows, Q_W)
    dproj, (db_ext, dqg, dkg) = qk_bwd(dqa, dka, ctx_cols(dkax), dva, ctx_cols(dvax), dqb, dkb, dvb, dgl, tq, rq, tk, rk,
                                       cos, sin, qg, kg, bd, nl, n)
    w_ext_t = w_ext.T
    (grad_x,), (dscale1, dshift1) = inproj_bwd("inproj_bwd", dproj, xa, dxp, modrows, w_ext_t, ntiles=nl, tile_off=0,
                                               is_ctx=False, out_rows=s_rows)
    _, (dscale_c, dshift_c) = inproj_bwd("inproj_bwd_ctx", dproj, xa, None, modrows, w_ext_t, ntiles=c_rows // TM,
                                         tile_off=nl, is_ctx=True, out_rows=0)
    dw_in = _fold_cols(mm_tn("dw_in", hb, dproj, n))

    dmod = jnp.concatenate([dshift1, dscale1, dgate1, dshift2, dscale2, dgate2], axis=1)
    dmod_c = jnp.concatenate([dshift_c, dscale_c, jnp.zeros((1, (N_MOD - 2) * D), F32)], axis=1)
    fold_g = lambda t: t.reshape(N_HEADS, HEAD_DIM).sum(axis=0)
    red = {
        "b_in": _fold_cols(db_ext), "attn_sink": dsk.reshape(N_HEADS, WIN).sum(axis=1), "q_norm_g": fold_g(dqg),
        "k_norm_g": fold_g(dkg), "ln1_g": dln1_g, "ln1_b": dln1_b, "conv_w": jnp.concatenate([dcw0, dcw1, dcw2], axis=0),
        "conv_b": dconv_b, "ln2_g": dln2_g, "ln2_b": dln2_b,
    }
    return loss[0, 0], grad_x, (dw_in, dwba, dwbb, dw_out, dw_up, dw_down), dmod, dmod_c, red


def kernel(x, c, ctx, c_ctx, w_mod, b_mod, w_in, b_in, attn_sink, q_norm_g, k_norm_g, w_branch_a, w_branch_b, w_out, ln1_g, ln1_b, w_up, conv_w, conv_b, w_down, ln2_g, ln2_b, loss_target, m_c_ctx, m_w_mod, m_b_mod, m_w_in, m_b_in, m_attn_sink, m_q_norm_g, m_k_norm_g, m_w_branch_a, m_w_branch_b, m_w_out, m_ln1_g, m_ln1_b, m_w_up, m_conv_w, m_conv_b, m_w_down, m_ln2_g, m_ln2_b, v_c_ctx, v_w_mod, v_b_mod, v_w_in, v_b_in, v_attn_sink, v_q_norm_g, v_k_norm_g, v_w_branch_a, v_w_branch_b, v_w_out, v_ln1_g, v_ln1_b, v_w_up, v_conv_w, v_conv_b, v_w_down, v_ln2_g, v_ln2_b):
    ax, ay, ac = (lax.axis_index(a) for a in AXES)
    me = 4 * ax + 2 * ay + ac
    chip = 2 * ax + ay
    mod_w = N_MOD * D // N_DEV
    params = dict(c_ctx=c_ctx, w_mod=w_mod, b_mod=b_mod, w_in=w_in, b_in=b_in, attn_sink=attn_sink, q_norm_g=q_norm_g,
                  k_norm_g=k_norm_g, w_branch_a=w_branch_a, w_branch_b=w_branch_b, w_out=w_out, ln1_g=ln1_g, ln1_b=ln1_b,
                  w_up=w_up, conv_w=conv_w, conv_b=conv_b, w_down=w_down, ln2_g=ln2_g, ln2_b=ln2_b)
    mom_m = dict(c_ctx=m_c_ctx, w_mod=m_w_mod, b_mod=m_b_mod, w_in=m_w_in, b_in=m_b_in, attn_sink=m_attn_sink,
                 q_norm_g=m_q_norm_g, k_norm_g=m_k_norm_g, w_branch_a=m_w_branch_a, w_branch_b=m_w_branch_b, w_out=m_w_out,
                 ln1_g=m_ln1_g, ln1_b=m_ln1_b, w_up=m_w_up, conv_w=m_conv_w, conv_b=m_conv_b, w_down=m_w_down,
                 ln2_g=m_ln2_g, ln2_b=m_ln2_b)
    mom_v = dict(c_ctx=v_c_ctx, w_mod=v_w_mod, b_mod=v_b_mod, w_in=v_w_in, b_in=v_b_in, attn_sink=v_attn_sink,
                 q_norm_g=v_q_norm_g, k_norm_g=v_k_norm_g, w_branch_a=v_w_branch_a, w_branch_b=v_w_branch_b, w_out=v_w_out,
                 ln1_g=v_ln1_g, ln1_b=v_ln1_b, w_up=v_w_up, conv_w=v_conv_w, conv_b=v_conv_b, w_down=v_w_down,
                 ln2_g=v_ln2_g, ln2_b=v_ln2_b)
    big_names = [nm for nm, _ in BIG]

    wg = all_gather("ag_weights", _pack_big([params[nm][0].astype(MXU_DTYPE) for nm in big_names]))
    g_in, g_ba, g_bb, g_out, g_up, g_down = _unpack_big(wg)
    weights = (_cols_to_full(g_in), _cols_to_full(g_ba), _cols_to_full(g_bb), g_out.reshape(D, D), _cols_to_full(g_up),
               g_down.reshape(D_FF, D))

    c_all = all_gather("ag_c", c.reshape(8, LANES)).reshape(N_DEV, D)
    cs = jnp.concatenate([c_all, c_ctx.reshape(1, D), jnp.zeros((7, D), F32)], axis=0)
    w_mod_sh = w_mod[0]
    b_mod_sh = lax.dynamic_slice(b_mod, (0, me * mod_w), (1, mod_w))
    mod_part = mod_fwd(cs, w_mod_sh, b_mod_sh)
    mg = all_gather("ag_mod", mod_part.reshape(16 * mod_w // LANES, LANES)).reshape(N_DEV, 16, mod_w)
    mod = lax.dynamic_index_in_dim(mg, me, axis=1, keepdims=False).reshape(N_MOD, D)
    mod_c = mg[:, 8, :].reshape(N_MOD, D)
    modrows = jnp.stack([mod[0], mod[1], mod_c[0], mod_c[1], mod[2], mod[3], mod[4], mod[5]], axis=0)

    conv_w_full = all_gather("ag_conv_w", jnp.pad(conv_w[0], ((0, 5), (0, LANES * 6 - 2 * D_FF // N_DEV))))
    conv_w_full = _cols_to_full(conv_w_full[:, :3, :2 * D_FF // N_DEV])
    small = dict(b_in=b_in, ln1_g=ln1_g, ln1_b=ln1_b, ln2_g=ln2_g, ln2_b=ln2_b, conv_b=conv_b, conv_w_full=conv_w_full,
                 q_norm_g=q_norm_g, k_norm_g=k_norm_g, attn_sink=attn_sink)
    loss, grad_x, big_grads, dmod, dmod_c, red = _local_step(x[0], ctx[0], loss_target[0], modrows, weights, small)
    loss = lax.psum(loss, AXES)

    dm = all_gather("ag_dmod", jnp.concatenate([dmod, dmod_c], axis=0).reshape(2 * N_MOD * D // LANES, LANES))
    dm = dm.reshape(N_DEV, 2, N_MOD * D)
    dm_all = jnp.concatenate([dm[:, 0], dm[:, 1]], axis=0)
    dm_sh = lax.dynamic_slice(dm_all, (0, me * mod_w), (16, mod_w))
    dw_mod, dcc, db_mod = mod_bwd(cs, w_mod_sh, dm_sh, dm_all)
    red["c_ctx"] = dcc[8]

    red_vec = jnp.concatenate([red[nm].reshape(-1) for nm, _ in RED])
    red_vec = jnp.pad(red_vec, (0, RED_ROWS * LANES - RED_TOTAL)).reshape(RED_ROWS, LANES)
    red_sum = sum8("sum_small", all_gather("ag_small", red_vec)).reshape(-1)
    gsm, off = {}, 0
    for nm, k in RED:
        gsm[nm] = red_sum[off:off + k]
        off += k
    gsm["b_mod"] = db_mod.reshape(-1)
    cw_sh = 2 * D_FF // N_DEV
    gsm["conv_w"] = lax.dynamic_slice(gsm["conv_w"].reshape(3, 2 * D_FF), (0, me * cw_sh), (3, cw_sh))
    sm_names = [nm for nm, _ in SMALL]
    gs, ds, ms, vs = adamw("adamw_small", _pack_small([params[nm] for nm in sm_names]),
                           _pack_small([mom_m[nm] for nm in sm_names]), _pack_small([mom_v[nm] for nm in sm_names]),
                           [_pack_small([gsm[nm] for nm in sm_names])])
    sm_out = [_unpack_small(t) for t in (gs, ds, ms, vs)]

    dw_in, dwba, dwbb, dw_out, dw_up, dw_down = big_grads
    slabs = jnp.stack([_pack_big(parts) for parts in zip(
        _full_to_cols(dw_in), _full_to_cols(dwba), _full_to_cols(dwbb), dw_out.reshape(N_DEV, D // N_DEV, D),
        _full_to_cols(dw_up), dw_down.reshape(N_DEV, D_FF // N_DEV, D), strict=True)])
    by_core = slabs.reshape(4, 2, BIG_ROWS, LANES)
    keep = lax.dynamic_index_in_dim(by_core, ac, axis=1, keepdims=False)
    give = lax.dynamic_index_in_dim(by_core, 1 - ac, axis=1, keepdims=False)
    got = exchange("rs_sibling", give.reshape(1, 4 * BIG_ROWS, LANES), to_chips=False).reshape(4, BIG_ROWS, LANES)
    pair = add2("rs_pair_sum", keep, got)
    outbox = jnp.stack([lax.dynamic_index_in_dim(pair, jnp.bitwise_xor(chip, m), axis=0, keepdims=False) for m in (1, 2, 3)])
    inbox = exchange("rs_chips", outbox, to_chips=True)
    mine = lax.dynamic_index_in_dim(pair, chip, axis=0, keepdims=False)
    gb, db, mb, vb = adamw("adamw_big", _pack_big([params[nm][0] for nm in big_names]),
                           _pack_big([mom_m[nm][0] for nm in big_names]), _pack_big([mom_v[nm][0] for nm in big_names]),
                           [mine, inbox[0], inbox[1], inbox[2]])
    big_out = [dict(zip(big_names, _unpack_big(t), strict=True)) for t in (gb, db, mb, vb)]
    gm, dmo, mmo, vmo = adamw("adamw_mod", w_mod[0], m_w_mod[0], v_w_mod[0], [dw_mod])
    mod_out = (gm, dmo, mmo, vmo)

    order = ["c_ctx", "w_mod", "b_mod", "w_in", "b_in", "attn_sink", "q_norm_g", "k_norm_g", "w_branch_a", "w_branch_b",
             "w_out", "ln1_g", "ln1_b", "w_up", "conv_w", "conv_b", "w_down", "ln2_g", "ln2_b"]
    results = [loss, grad_x[None]]
    for kind in range(4):
        for nm in order:
            if nm == "w_mod":
                val = mod_out[kind]
            elif nm in big_out[kind]:
                val = big_out[kind][nm]
            else:
                val = sm_out[kind][nm]
            results.append(val.reshape(params[nm].shape))
    return tuple(results)
```

```python
import functools

import jax
import jax.numpy as jnp
import numpy as np
from jax import lax
from jax.experimental import pallas as pl
from jax.experimental.pallas import tpu as pltpu

F32 = jnp.float32
BF16 = jnp.bfloat16
MXU_DTYPE = BF16

AXES = ("x", "y", "c")
N_DEV = 8
D = 1024
HEAD_DIM = 64
N_HEADS = 8
N_KV = 2
GROUPS = 4
KV_W = GROUPS * HEAD_DIM
Q_W = N_HEADS * HEAD_DIM
GRID_W = 64
WIN = 128
ROPE_THETA = 10000.0
D_FF = 2816
LN_EPS = 1e-5
QK_EPS = 1e-6
N_MOD = 6
ALPHA = 2.0 ** 0.25
Q_SCALE = HEAD_DIM ** -0.5
IN_COLS = 3584
OFF_KA, OFF_VA, OFF_QB, OFF_KB, OFF_VB, OFF_GA = 512, 640, 768, 1280, 1408, 1536
EXT_COLS = 6 * Q_W + 2 * D
X_QA, X_KA, X_VA, X_QB, X_KB, X_VB, X_GL = 0, 512, 1024, 1536, 2048, 2560, 3072
ADAM_LR, ADAM_B1, ADAM_B2, ADAM_EPS, ADAM_WD, ADAM_STEP = 0.001, 0.9, 0.999, 1e-08, 0.01, 10
LANES = 128
TM = 256
VMEM_LIMIT = 56 * 1024 * 1024
ELEMENTWISE_BLOCK_BYTES = 1 << 20

ANY = pl.BlockSpec(memory_space=pl.ANY)
SDS = jax.ShapeDtypeStruct


def _pick(n, candidates):
    for t in candidates:
        if n % t == 0:
            return t
    raise ValueError(f"no tile for {n}")


def _full(a):
    nd = a.ndim
    return pl.BlockSpec(a.shape, lambda *_: (0,) * nd)


def _rows(tm, w, fn=lambda t: t):
    return pl.BlockSpec((tm, w), lambda i: (fn(i), 0))


def _dot(a, b):
    return jnp.dot(a.astype(MXU_DTYPE), b.astype(MXU_DTYPE), preferred_element_type=F32)


def _dot_nt(a, b):
    return lax.dot_general(a.astype(MXU_DTYPE), b.astype(MXU_DTYPE), (((1,), (1,)), ((), ())), preferred_element_type=F32)


def _dot_tn(a, b):
    return lax.dot_general(a.astype(MXU_DTYPE), b.astype(MXU_DTYPE), (((0,), (0,)), ((), ())), preferred_element_type=F32)


def _cparams(sem):
    return pltpu.CompilerParams(dimension_semantics=sem, vmem_limit_bytes=VMEM_LIMIT)


def all_gather(name, v):
    r, w = v.shape

    def body(x_ref, out_ref, send_sems, recv_sems, local_sem):
        x, y, c = (lax.axis_index(a) for a in AXES)
        me, sibling = (x, y, c), (x, y, 1 - c)
        chips = [(1 - x, y), (x, 1 - y), (1 - x, 1 - y)]

        def rows(px, py, pc):
            return out_ref.at[4 * px + 2 * py + pc]

        def copy(k, block, to, src=None):
            return pltpu.make_async_remote_copy(
                src_ref=rows(*block) if src is None else src, dst_ref=rows(*block),
                send_sem=send_sems.at[k], recv_sem=recv_sems.at[k],
                device_id=to, device_id_type=pl.DeviceIdType.MESH)

        mine = pltpu.make_async_copy(x_ref, rows(*me), local_sem)
        mine.start()
        first = [copy(0, me, sibling, src=x_ref)]
        first += [copy(1 + j, me, (*chip, c), src=x_ref) for j, chip in enumerate(chips)]
        for cp in first:
            cp.start()
        passed = [copy(4 + j, (*chip, c), sibling) for j, chip in enumerate(chips)]
        for j, chip in enumerate(chips):
            copy(1 + j, (*chip, c), me).wait_recv()
            passed[j].start()
        copy(0, sibling, me).wait_recv()
        for j, chip in enumerate(chips):
            copy(4 + j, (*chip, 1 - c), me).wait_recv()
        for cp in first + passed:
            cp.wait_send()
        mine.wait()

    return pl.pallas_call(
        body, name=name, out_shape=SDS((N_DEV, r, w), v.dtype), in_specs=[ANY], out_specs=ANY,
        scratch_shapes=[pltpu.SemaphoreType.DMA((7,)), pltpu.SemaphoreType.DMA((7,)), pltpu.SemaphoreType.DMA],
    )(v)


def exchange(name, outbox, to_chips):
    k = outbox.shape[0]
    assert k == (3 if to_chips else 1)

    def body(out_ref, in_ref, send_sems, recv_sems):
        x, y, c = (lax.axis_index(a) for a in AXES)
        peers = [(x, 1 - y, c), (1 - x, y, c), (1 - x, 1 - y, c)] if to_chips else [(x, y, 1 - c)]
        copies = [
            pltpu.make_async_remote_copy(
                src_ref=out_ref.at[m], dst_ref=in_ref.at[m], send_sem=send_sems.at[m], recv_sem=recv_sems.at[m],
                device_id=peer, device_id_type=pl.DeviceIdType.MESH)
            for m, peer in enumerate(peers)
        ]
        for cp in copies:
            cp.start()
        for cp in copies:
            cp.wait_recv()
        for cp in copies:
            cp.wait_send()

    return pl.pallas_call(
        body, name=name, out_shape=SDS(outbox.shape, outbox.dtype), in_specs=[ANY], out_specs=ANY,
        scratch_shapes=[pltpu.SemaphoreType.DMA((k,)), pltpu.SemaphoreType.DMA((k,))],
    )(outbox)


def rowwise(name, body, *, ntiles, tile_off=0, tiled, full, outs, accs=()):
    nt, nf, no = len(tiled), len(full), len(outs)

    def kern(*refs):
        i = pl.program_id(0)
        out_vals, incs = body(i + tile_off, refs[:nt], refs[nt:nt + nf])
        for r, v in zip(refs[nt + nf:nt + nf + no], out_vals, strict=True):
            r[...] = v.astype(r.dtype)
        acc_refs = refs[nt + nf + no:]

        @pl.when(i == 0)
        def _():
            for r in acc_refs:
                r[...] = jnp.zeros_like(r)

        for r, v in zip(acc_refs, incs, strict=True):
            r[...] += v

    res = pl.pallas_call(
        kern, name=name, grid=(ntiles,),
        in_specs=[s for _, s in tiled] + [_full(a) for a in full],
        out_specs=[s for _, _, s in outs] + [pl.BlockSpec(s, lambda i, n=len(s): (0,) * n) for s in accs],
        out_shape=[SDS(s, d) for s, d, _ in outs] + [SDS(s, F32) for s in accs],
        compiler_params=_cparams(("arbitrary",) if accs else ("parallel",)),
    )(*[a for a, _ in tiled], *full)
    return res[:no], res[no:]


def mm_tn(name, a, b, rows):
    ka, nb = a.shape[1], b.shape[1]
    tr = _pick(rows, (1280, 1024, 768, 512, 256))
    tn = _pick(nb, (512, 256, 128))

    def kern(a_ref, b_ref, o_ref):
        @pl.when(pl.program_id(1) == 0)
        def _():
            o_ref[...] = jnp.zeros_like(o_ref)

        o_ref[...] += _dot_tn(a_ref[...], b_ref[...])

    return pl.pallas_call(
        kern, name=name, grid=(nb // tn, rows // tr),
        in_specs=[pl.BlockSpec((tr, ka), lambda n, r: (r, 0)), pl.BlockSpec((tr, tn), lambda n, r: (r, n))],
        out_specs=pl.BlockSpec((ka, tn), lambda n, r: (0, n)), out_shape=SDS((ka, nb), F32),
        compiler_params=_cparams(("parallel", "arbitrary")),
    )(a, b)


def _swap16(t):
    w = t.shape[1]
    lane = lax.broadcasted_iota(jnp.int32, t.shape, 1)
    return jnp.where((lane & 16) == 0, pltpu.roll(t, w - 16, 1), pltpu.roll(t, 16, 1))


def _rope(t, cos, sin):
    return t * cos + _swap16(t) * sin


def _rope_t(d, cos, sin):
    return d * cos - _swap16(d) * sin


def _seg_sum64(a, bd_ref):
    bd = bd_ref[...]
    hi = a.astype(BF16)
    lo = (a - hi.astype(F32)).astype(BF16)
    return jnp.dot(hi, bd, preferred_element_type=F32) + jnp.dot(lo, bd, preferred_element_type=F32)


def _lane_block(shape):
    return jnp.right_shift(lax.broadcasted_iota(jnp.int32, shape, 1), 6)


def _stack_groups(t, dtype):
    blk = _lane_block(t.shape)
    return jnp.concatenate([jnp.where(blk == g, t, jnp.zeros_like(t)).astype(dtype) for g in range(GROUPS)], axis=0)


def _fold_groups(ts, tq):
    blk = _lane_block((tq, KV_W))
    out = jnp.zeros((tq, KV_W), ts.dtype)
    for g in range(GROUPS):
        out = jnp.where(blk == g, ts[g * tq:(g + 1) * tq], out)
    return out


def _layer_norm_bwd(dxh, xhat, rstd):
    m1 = jnp.mean(dxh, axis=1, keepdims=True)
    m2 = jnp.mean(dxh * xhat, axis=1, keepdims=True)
    return rstd * (dxh - m1 - xhat * m2)


def _colsum(a):
    return jnp.sum(a, axis=0, keepdims=True)


def _shifted_rows(t, prev_row, next_row):
    n = t.shape[0]
    row = lax.broadcasted_iota(jnp.int32, t.shape, 0)
    up = jnp.where(row == 0, prev_row, pltpu.roll(t, 1, 0))
    dn = jnp.where(row == n - 1, next_row, pltpu.roll(t, n - 1, 0))
    return up, dn


def mod_fwd(cs, w_sh, b_sh):
    def kern(c_ref, w_ref, b_ref, o_ref):
        o_ref[...] = _dot(jax.nn.silu(c_ref[...]), w_ref[...]) + b_ref[...]

    return pl.pallas_call(kern, name="mod_fwd", out_shape=SDS((16, w_sh.shape[1]), F32),
                          compiler_params=pltpu.CompilerParams(vmem_limit_bytes=VMEM_LIMIT))(cs, w_sh, b_sh)


def mod_bwd(cs, w_sh, dm_sh, dm_all):
    hp = lax.Precision.HIGHEST

    def kern(c_ref, w_ref, dm_ref, da_ref, dw_ref, dc_ref, db_ref):
        c = c_ref[...]
        sg = jax.nn.sigmoid(c)
        sc = c * sg
        dm = dm_ref[...]
        dmc = dm_ref[8:9, :]
        for i in range(9, 16):
            dmc = dmc + dm_ref[i:i + 1, :]
        row = lax.broadcasted_iota(jnp.int32, dm.shape, 0)
        a = jnp.where(row < 8, dm, jnp.where(row == 8, dmc, 0.0))
        dw_ref[...] = lax.dot_general(sc, a, (((0,), (0,)), ((), ())), precision=hp, preferred_element_type=F32)
        dsc = lax.dot_general(a, w_ref[...], (((1,), (1,)), ((), ())), precision=hp, preferred_element_type=F32)
        dc_ref[...] = dsc * (sg * (1.0 + c * (1.0 - sg)))
        db = da_ref[0:1, :]
        for i in range(1, 16):
            db = db + da_ref[i:i + 1, :]
        db_ref[...] = db

    return pl.pallas_call(
        kern, name="mod_bwd",
        out_shape=[SDS(w_sh.shape, F32), SDS((16, D), F32), SDS((1, dm_all.shape[1]), F32)],
        compiler_params=pltpu.CompilerParams(vmem_limit_bytes=VMEM_LIMIT))(cs, w_sh, dm_sh, dm_all)


M_SHIFT1, M_SCALE1, M_SHIFTC, M_SCALEC, M_GATE1, M_SHIFT2, M_SCALE2, M_GATE2 = range(8)


def _mrow(ref, k):
    return ref[k:k + 1, :]


def inproj_fwd(xa, cos, sin, modrows, w_ext, b_ext, qg, kg, bd, n_lat_tiles):
    n = xa.shape[0]

    def body(t, vals, fr):
        x, cs, sn = (v[...] for v in vals)
        mod, w, b, qg_r, kg_r, bd_r = fr
        is_ctx = t >= n_lat_tiles
        shift = jnp.where(is_ctx, _mrow(mod, M_SHIFTC), _mrow(mod, M_SHIFT1))
        scale = jnp.where(is_ctx, _mrow(mod, M_SCALEC), _mrow(mod, M_SCALE1))
        hb = (x * (1.0 + scale) + shift).astype(MXU_DTYPE)
        proj = jnp.dot(hb, w[...], preferred_element_type=F32) + b[...]
        cos4 = jnp.concatenate([cs] * 4, axis=1)
        sin4 = jnp.concatenate([sn] * 4, axis=1)
        qa = _rope(proj[:, X_QA:X_QA + Q_W], cos4, sin4) * Q_SCALE
        ka = _rope(proj[:, X_KA:X_KA + Q_W], cos4, sin4)
        va = proj[:, X_VA:X_VA + Q_W]
        tq = proj[:, X_QB:X_QB + Q_W]
        rq = lax.rsqrt(_seg_sum64(tq * tq, bd_r) * (1.0 / HEAD_DIM) + QK_EPS)
        qb = _rope(tq * rq * qg_r[...], cos4, sin4) * Q_SCALE
        tk = proj[:, X_KB:X_KB + Q_W]
        rk = lax.rsqrt(_seg_sum64(tk * tk, bd_r) * (1.0 / HEAD_DIM) + QK_EPS)
        kb = _rope(tk * rk * kg_r[...], cos4, sin4)
        vb = proj[:, X_VB:X_VB + Q_W]
        gl = proj[:, X_GL:]
        return [hb, qa, ka, va, qb, kb, vb, tq, rq, tk, rk, gl], []

    mx = MXU_DTYPE
    outs = [((n, D), mx, _rows(TM, D))] + [((n, Q_W), mx, _rows(TM, Q_W))] * 6 + \
           [((n, Q_W), F32, _rows(TM, Q_W))] * 4 + [((n, 2 * D), F32, _rows(TM, 2 * D))]
    res, _ = rowwise("inproj_fwd", body, ntiles=n // TM,
                     tiled=[(xa, _rows(TM, D)), (cos, _rows(TM, LANES)), (sin, _rows(TM, LANES))],
                     full=[modrows, w_ext, b_ext, qg, kg, bd], outs=outs)
    return res


def merge_fwd(oa, ob, gl, x, modrows, wba, wbb, w_out, s_rows):
    def body(t, vals, fr):
        oa_, ob_, gl_, x_ = (v[...] for v in vals)
        mod, wa, wb, wo = fr
        ya = _dot(oa_, wa[...])
        yb = _dot(ob_, wb[...])
        ga = jax.nn.sigmoid(gl_[:, :D])
        gb = jax.nn.sigmoid(gl_[:, D:])
        mrg = ga * ya + gb * yb
        y = _dot(mrg, wo[...])
        r1 = ALPHA * x_ + _mrow(mod, M_GATE1) * y
        mu = jnp.mean(r1, axis=1, keepdims=True)
        xc = r1 - mu
        var = jnp.mean(xc * xc, axis=1, keepdims=True)
        rstd = lax.rsqrt(var + LN_EPS)
        xhat = xc * rstd
        return [ya, yb, mrg, y, xhat, rstd], []

    outs = [((s_rows, D), F32, _rows(TM, D))] * 2 + [((s_rows, D), MXU_DTYPE, _rows(TM, D))] + \
           [((s_rows, D), F32, _rows(TM, D))] * 2 + [((s_rows, 1), F32, _rows(TM, 1))]
    res, _ = rowwise("merge_fwd", body, ntiles=s_rows // TM,
                     tiled=[(oa, _rows(TM, Q_W)), (ob, _rows(TM, Q_W)), (gl, _rows(TM, 2 * D)), (x, _rows(TM, D))],
                     full=[modrows, wba, wbb, w_out], outs=outs)
    return res


def ffn_up_fwd(xhat1, modrows, ln_g, ln_b, w_up, s_rows):
    def body(t, vals, fr):
        xh = vals[0][...]
        mod, g_r, b_r, w = fr
        x1 = xh * g_r[...] + b_r[...]
        h2 = (x1 * (1.0 + _mrow(mod, M_SCALE2)) + _mrow(mod, M_SHIFT2)).astype(MXU_DTYPE)
        return [h2, jnp.dot(h2, w[...], preferred_element_type=F32)], []

    res, _ = rowwise("ffn_up_fwd", body, ntiles=s_rows // TM, tiled=[(xhat1, _rows(TM, D))],
                     full=[modrows, ln_g, ln_b, w_up],
                     outs=[((s_rows, D), MXU_DTYPE, _rows(TM, D)), ((s_rows, 2 * D_FF), F32, _rows(TM, 2 * D_FF))])
    return res


TC = 128


def _halo_specs(tm, w, s_rows):
    per = tm // 8
    last = s_rows // 8 - 1
    return (pl.BlockSpec((8, w), lambda i: (jnp.maximum(i * per - 1, 0), 0)),
            pl.BlockSpec((8, w), lambda i: (jnp.minimum((i + 1) * per, last), 0)))


def _halo_rows(t, ntiles, prev_ref, next_ref):
    prev_row = jnp.where(t == 0, 0.0, prev_ref[7:8, :].astype(F32))
    next_row = jnp.where(t == ntiles - 1, 0.0, next_ref[0:1, :].astype(F32))
    return prev_row, next_row


def conv_swiglu_fwd(u0, conv_w8, conv_b, s_rows):
    w2 = 2 * D_FF
    nt = s_rows // TC

    def body(t, vals, fr):
        u_ref, pv, nx = vals
        cw, cb = fr
        u = u_ref[...]
        up, dn = _shifted_rows(u, *_halo_rows(t, nt, pv, nx))
        uc = cw[0:1, :] * up + cw[1:2, :] * u + cw[2:3, :] * dn + cb[...]
        gate, val = uc[:, :D_FF], uc[:, D_FF:]
        return [gate * jax.nn.sigmoid(gate) * val], []

    hp, hn = _halo_specs(TC, w2, s_rows)
    res, _ = rowwise("conv_swiglu_fwd", body, ntiles=nt,
                     tiled=[(u0, _rows(TC, w2)), (u0, hp), (u0, hn)], full=[conv_w8, conv_b],
                     outs=[((s_rows, D_FF), MXU_DTYPE, _rows(TC, D_FF))])
    return res[0]


def ffn_down_loss(a, xhat1, target, modrows, ln1_g, ln1_b, ln2_g, ln2_b, w_down, s_rows):
    def body(t, vals, fr):
        a_, xh1, tgt = (v[...] for v in vals)
        mod, g1, b1, g2, b2, wd = fr
        y2 = jnp.dot(a_, wd[...], preferred_element_type=F32)
        x1 = xh1 * g1[...] + b1[...]
        gate2 = _mrow(mod, M_GATE2)
        r2 = ALPHA * x1 + gate2 * y2
        mu = jnp.mean(r2, axis=1, keepdims=True)
        xc = r2 - mu
        var = jnp.mean(xc * xc, axis=1, keepdims=True)
        rstd = lax.rsqrt(var + LN_EPS)
        xhat = xc * rstd
        out = xhat * g2[...] + b2[...]
        diff = out - tgt
        loss = 0.5 * jnp.sum(jnp.mean(diff * diff, axis=1, keepdims=True), axis=0, keepdims=True)
        dout = diff * (1.0 / D)
        dr2 = _layer_norm_bwd(dout * g2[...], xhat, rstd)
        incs = [loss, _colsum(dout * xhat), _colsum(dout), _colsum(dr2 * y2)]
        return [dr2, dr2 * gate2], incs

    res, accs = rowwise("ffn_down_loss", body, ntiles=s_rows // TM,
                        tiled=[(a, _rows(TM, D_FF)), (xhat1, _rows(TM, D)), (target, _rows(TM, D))],
                        full=[modrows, ln1_g, ln1_b, ln2_g, ln2_b, w_down],
                        outs=[((s_rows, D), F32, _rows(TM, D)), ((s_rows, D), MXU_DTYPE, _rows(TM, D))],
                        accs=[(1, 1), (1, D), (1, D), (1, D)])
    return res, accs


def ffn_down_bwd(dy2, w_down_t, s_rows):
    def body(t, vals, fr):
        return [jnp.dot(vals[0][...], fr[0][...], preferred_element_type=F32)], []

    res, _ = rowwise("ffn_down_bwd", body, ntiles=s_rows // TM, tiled=[(dy2, _rows(TM, D))], full=[w_down_t],
                     outs=[((s_rows, D_FF), F32, _rows(TM, D_FF))])
    return res[0]


def swiglu_conv_bwd(u0, da, conv_w8, conv_b, s_rows):
    w2 = 2 * D_FF
    nt = s_rows // TC

    def body(t, vals, fr):
        u_ref, pv, nx, da_ref = vals
        cw, cb = fr
        u, da_ = u_ref[...], da_ref[...]
        up, dn = _shifted_rows(u, *_halo_rows(t, nt, pv, nx))
        uc = cw[0:1, :] * up + cw[1:2, :] * u + cw[2:3, :] * dn + cb[...]
        gate, val = uc[:, :D_FF], uc[:, D_FF:]
        sg = jax.nn.sigmoid(gate)
        dgate = da_ * val * (sg * (1.0 + gate * (1.0 - sg)))
        dval = da_ * (gate * sg)
        du = jnp.concatenate([dgate, dval], axis=1)
        return [du], [_colsum(du), _colsum(up * du), _colsum(u * du), _colsum(dn * du)]

    hp, hn = _halo_specs(TC, w2, s_rows)
    res, accs = rowwise("swiglu_conv_bwd", body, ntiles=nt,
                        tiled=[(u0, _rows(TC, w2)), (u0, hp), (u0, hn), (da, _rows(TC, D_FF))],
                        full=[conv_w8, conv_b], outs=[((s_rows, w2), F32, _rows(TC, w2))], accs=[(1, w2)] * 4)
    return res[0], accs


def conv_bwd_input(du, conv_w8, s_rows):
    w2 = 2 * D_FF
    nt = s_rows // TC

    def body(t, vals, fr):
        d_ref, pv, nx = vals
        (cw,) = fr
        d = d_ref[...]
        up, dn = _shifted_rows(d, *_halo_rows(t, nt, pv, nx))
        return [cw[0:1, :] * dn + cw[1:2, :] * d + cw[2:3, :] * up], []

    hp, hn = _halo_specs(TC, w2, s_rows)
    res, _ = rowwise("conv_bwd_input", body, ntiles=nt, tiled=[(du, _rows(TC, w2)), (du, hp), (du, hn)],
                     full=[conv_w8], outs=[((s_rows, w2), MXU_DTYPE, _rows(TC, w2))])
    return res[0]


def ffn_up_ln1_bwd(du0, dr2, xhat1, y, rstd1, modrows, ln_g, ln_b, w_up_t, s_rows):
    def body(t, vals, fr):
        du0_, dr2_, xh, y_, rstd = (v[...] for v in vals)
        mod, g_r, b_r, wt = fr
        dh2 = jnp.dot(du0_, wt[...], preferred_element_type=F32)
        x1 = xh * g_r[...] + b_r[...]
        dx1 = ALPHA * dr2_ + dh2 * (1.0 + _mrow(mod, M_SCALE2))
        dr1 = _layer_norm_bwd(dx1 * g_r[...], xh, rstd)
        incs = [_colsum(dh2 * x1), _colsum(dh2), _colsum(dx1 * xh), _colsum(dx1), _colsum(dr1 * y_)]
        return [dr1 * _mrow(mod, M_GATE1), ALPHA * dr1], incs

    res, accs = rowwise("ffn_up_ln1_bwd", body, ntiles=s_rows // TM,
                        tiled=[(du0, _rows(TM, 2 * D_FF)), (dr2, _rows(TM, D)), (xhat1, _rows(TM, D)), (y, _rows(TM, D)),
                               (rstd1, _rows(TM, 1))],
                        full=[modrows, ln_g, ln_b, w_up_t],
                        outs=[((s_rows, D), MXU_DTYPE, _rows(TM, D)), ((s_rows, D), F32, _rows(TM, D))],
                        accs=[(1, D)] * 5)
    return res, accs


def merge_bwd(dy, ya, yb, gl, w_out_t, wba_t, wbb_t, s_rows):
    def body(t, vals, fr):
        dy_, ya_, yb_, gl_ = (v[...] for v in vals)
        wot, wat, wbt = fr
        dmrg = jnp.dot(dy_, wot[...], preferred_element_type=F32)
        ga = jax.nn.sigmoid(gl_[:, :D])
        gb = jax.nn.sigmoid(gl_[:, D:])
        dya = dmrg * ga
        dyb = dmrg * gb
        dgl = jnp.concatenate([dmrg * ya_ * ga * (1.0 - ga), dmrg * yb_ * gb * (1.0 - gb)], axis=1)
        return [dya, dyb, dgl, _dot(dya, wat[...]), _dot(dyb, wbt[...])], []

    mx = MXU_DTYPE
    res, _ = rowwise("merge_bwd", body, ntiles=s_rows // TM,
                     tiled=[(dy, _rows(TM, D)), (ya, _rows(TM, D)), (yb, _rows(TM, D)), (gl, _rows(TM, 2 * D))],
                     full=[w_out_t, wba_t, wbb_t],
                     outs=[((s_rows, D), mx, _rows(TM, D))] * 2 + [((s_rows, 2 * D), F32, _rows(TM, 2 * D))] +
                          [((s_rows, Q_W), F32, _rows(TM, Q_W))] * 2)
    return res


def qk_bwd(dqa, dka, dkax, dva, dvax, dqb, dkb, dvb, dgl, tq, rq, tk, rk, cos, sin, qg, kg, bd, n_lat_tiles, n):
    def body(t, vals, fr):
        dqa_, dka_, dkax_, dva_, dvax_, dqb_, dkb_, dvb_, dgl_, tq_, rq_, tk_, rk_, cs, sn = (v[...] for v in vals)
        qg_r, kg_r, bd_r = fr
        is_ctx = t >= n_lat_tiles
        cos4 = jnp.concatenate([cs] * 4, axis=1)
        sin4 = jnp.concatenate([sn] * 4, axis=1)
        zero = jnp.zeros_like(dqa_)
        dpqa = jnp.where(is_ctx, zero, _rope_t(dqa_, cos4, sin4) * Q_SCALE)
        dpka = _rope_t(jnp.where(is_ctx, dkax_, dka_), cos4, sin4)
        dpva = jnp.where(is_ctx, dvax_, dva_)
        dnq = jnp.where(is_ctx, zero, _rope_t(dqb_, cos4, sin4) * Q_SCALE)
        gq = qg_r[...] * dnq
        dtq = rq_ * gq - tq_ * (rq_ * rq_ * rq_) * (_seg_sum64(gq * tq_, bd_r) * (1.0 / HEAD_DIM))
        dnk = _rope_t(dkb_, cos4, sin4)
        gk = kg_r[...] * dnk
        dtk = rk_ * gk - tk_ * (rk_ * rk_ * rk_) * (_seg_sum64(gk * tk_, bd_r) * (1.0 / HEAD_DIM))
        dgl32 = jnp.where(is_ctx, jnp.zeros_like(dgl_), dgl_)
        dproj = jnp.concatenate([dpqa, dpka, dpva, dtq, dtk, dvb_, dgl32], axis=1)
        return [dproj], [_colsum(dproj), _colsum(dnq * tq_ * rq_), _colsum(dnk * tk_ * rk_)]

    lat = lambda t: jnp.minimum(t, n_lat_tiles - 1)
    cx = lambda t: jnp.maximum(t - n_lat_tiles, 0)
    qs = _rows(TM, Q_W)
    res, accs = rowwise(
        "qk_bwd", body, ntiles=n // TM,
        tiled=[(dqa, _rows(TM, Q_W, lat)), (dka, _rows(TM, Q_W, lat)), (dkax, _rows(TM, Q_W, cx)),
               (dva, _rows(TM, Q_W, lat)), (dvax, _rows(TM, Q_W, cx)), (dqb, _rows(TM, Q_W, lat)),
               (dkb, qs), (dvb, qs), (dgl, _rows(TM, 2 * D, lat)), (tq, qs), (rq, qs), (tk, qs), (rk, qs),
               (cos, _rows(TM, LANES)), (sin, _rows(TM, LANES))],
        full=[qg, kg, bd], outs=[((n, EXT_COLS), MXU_DTYPE, _rows(TM, EXT_COLS))],
        accs=[(1, EXT_COLS), (1, Q_W), (1, Q_W)])
    return res[0], accs


def inproj_bwd(name, dproj, xa, dxp, modrows, w_ext_t, *, ntiles, tile_off, is_ctx, out_rows):
    kc = M_SCALEC if is_ctx else M_SCALE1

    def body(t, vals, fr):
        dp, x_ = vals[0][...], vals[1][...]
        mod, wt = fr
        dh = jnp.dot(dp, wt[...], preferred_element_type=F32)
        incs = [_colsum(dh * x_), _colsum(dh)]
        if is_ctx:
            return [], incs
        return [vals[2][...] + dh * (1.0 + _mrow(mod, kc))], incs

    tiled = [(dproj, _rows(TM, EXT_COLS, lambda i: i + tile_off)), (xa, _rows(TM, D, lambda i: i + tile_off))]
    outs = []
    if not is_ctx:
        tiled.append((dxp, _rows(TM, D)))
        outs = [((out_rows, D), F32, _rows(TM, D))]
    return rowwise(name, body, ntiles=ntiles, tiled=tiled, full=[modrows, w_ext_t], outs=outs, accs=[(1, D)] * 2)


def _attn_semantics():
    return _cparams(("arbitrary", "arbitrary", "arbitrary"))


GLOB_TK = (1280, 1024, 768, 512, 256)
KEY_CHUNK = 256


def glob_fwd(q, kt, vt, s_rows):
    n = kt.shape[0]
    tq = TM
    tk = _pick(n, GLOB_TK)
    nq, nk = s_rows // tq, n // tk
    r = GROUPS * tq
    nch = tk // KEY_CHUNK

    def produce(qs, k_ref, s_buf, c, mx):
        lo = c * KEY_CHUNK
        sn = _dot_nt(qs[...], k_ref[lo:lo + KEY_CHUNK, :])
        s_buf[:, lo:lo + KEY_CHUNK] = sn
        for t in range(KEY_CHUNK // LANES):
            mx = jnp.maximum(mx, sn[:, t * LANES:(t + 1) * LANES])
        return mx

    def kern(q_ref, k0_ref, kn_ref, v_ref, o_ref, lse_ref, qs, s_buf, mx_buf, p_buf, m_s, l_s, acc):
        j = pl.program_id(2)

        @pl.when(j == 0)
        def _():
            qs[...] = _stack_groups(q_ref[...], qs.dtype)
            mx = jnp.full((r, LANES), -jnp.inf, F32)
            for c in range(nch):
                mx = produce(qs, k0_ref, s_buf, c, mx)
            mx_buf[...] = mx
            m_s[...] = jnp.full_like(m_s, -jnp.inf)
            l_s[...] = jnp.zeros_like(l_s)
            acc[...] = jnp.zeros_like(acc)

        m_prev = m_s[...]
        m_new = jnp.maximum(m_prev, jnp.max(mx_buf[...], axis=1, keepdims=True))
        alpha = jnp.exp(m_prev - m_new)
        m_b = jnp.broadcast_to(m_new, (r, LANES))
        mx = jnp.full((r, LANES), -jnp.inf, F32)
        ls = jnp.zeros((r, LANES), F32)
        for c in range(nch):
            for t in range(KEY_CHUNK // LANES):
                lo = c * KEY_CHUNK + t * LANES
                pt = jnp.exp(s_buf[:, lo:lo + LANES] - m_b)
                ls = ls + pt
                p_buf[:, lo:lo + LANES] = pt.astype(p_buf.dtype)
            mx = produce(qs, kn_ref, s_buf, c, mx)
        mx_buf[...] = mx
        l_s[...] = alpha * l_s[...] + jnp.sum(ls, axis=1, keepdims=True)
        acc[...] = alpha * acc[...] + jnp.dot(p_buf[...], v_ref[...], preferred_element_type=F32)
        m_s[...] = m_new

        @pl.when(j == nk - 1)
        def _():
            o_ref[...] = _fold_groups(acc[...] / l_s[...], tq)
            lse_ref[0, 0] = m_s[...] + jnp.log(l_s[...])

    kspec = lambda f: pl.BlockSpec((tk, KV_W), lambda h, i, j: (f(j), h))
    return pl.pallas_call(
        kern, name="glob_fwd", grid=(N_KV, nq, nk),
        in_specs=[pl.BlockSpec((tq, KV_W), lambda h, i, j: (i, h)), kspec(lambda j: 0),
                  kspec(lambda j: jnp.minimum(j + 1, nk - 1)), kspec(lambda j: j)],
        out_specs=[pl.BlockSpec((tq, KV_W), lambda h, i, j: (i, h)),
                   pl.BlockSpec((1, 1, r, 1), lambda h, i, j: (h, i, 0, 0))],
        out_shape=[SDS((s_rows, Q_W), F32), SDS((N_KV, nq, r, 1), F32)],
        scratch_shapes=[pltpu.VMEM((r, KV_W), MXU_DTYPE), pltpu.VMEM((r, tk), F32), pltpu.VMEM((r, LANES), F32),
                        pltpu.VMEM((r, tk), MXU_DTYPE), pltpu.VMEM((r, 1), F32), pltpu.VMEM((r, 1), F32),
                        pltpu.VMEM((r, KV_W), F32)],
        compiler_params=_attn_semantics(),
    )(q, kt, kt, vt)


def glob_bwd_dq(q, kt, vt, o, do, lse, s_rows):
    n = kt.shape[0]
    tq = TM
    tk = _pick(n, GLOB_TK)
    nq, nk = s_rows // tq, n // tk

    def kern(q_ref, k_ref, v_ref, o_ref, do_ref, lse_ref, dq_ref, delta_ref, qs, dos, dl, acc):
        j = pl.program_id(2)

        @pl.when(j == 0)
        def _():
            qs[...] = _stack_groups(q_ref[...], qs.dtype)
            do32 = _stack_groups(do_ref[...], F32)
            dl[...] = jnp.sum(do32 * _stack_groups(o_ref[...], F32), axis=1, keepdims=True)
            dos[...] = do32.astype(dos.dtype)
            acc[...] = jnp.zeros_like(acc)

        k = k_ref[...]
        p = jnp.exp(_dot_nt(qs[...], k) - lse_ref[0, 0])
        ds = p * (_dot_nt(dos[...], v_ref[...]) - dl[...])
        acc[...] += _dot(ds, k)

        @pl.when(j == nk - 1)
        def _():
            dq_ref[...] = _fold_groups(acc[...], tq)
            delta_ref[0, 0] = dl[...]

    col = pl.BlockSpec((1, 1, GROUPS * tq, 1), lambda h, i, j: (h, i, 0, 0))
    qspec = pl.BlockSpec((tq, KV_W), lambda h, i, j: (i, h))
    kspec = pl.BlockSpec((tk, KV_W), lambda h, i, j: (j, h))
    return pl.pallas_call(
        kern, name="glob_bwd_dq", grid=(N_KV, nq, nk),
        in_specs=[qspec, kspec, kspec, qspec, qspec, col], out_specs=[qspec, col],
        out_shape=[SDS((s_rows, Q_W), F32), SDS((N_KV, nq, GROUPS * tq, 1), F32)],
        scratch_shapes=[pltpu.VMEM((GROUPS * tq, KV_W), MXU_DTYPE), pltpu.VMEM((GROUPS * tq, KV_W), MXU_DTYPE),
                        pltpu.VMEM((GROUPS * tq, 1), F32), pltpu.VMEM((GROUPS * tq, KV_W), F32)],
        compiler_params=_attn_semantics(),
    )(q, kt, vt, o, do, lse)


def glob_bwd_dkv(q, kt, vt, do, lse_row, delta_row, s_rows):
    n = kt.shape[0]
    tq = TM
    tk = _pick(n, GLOB_TK)
    nq, nk = s_rows // tq, n // tk

    def kern(q_ref, k_ref, v_ref, do_ref, lse_ref, delta_ref, dk_ref, dv_ref, dk_acc, dv_acc):
        i = pl.program_id(2)

        @pl.when(i == 0)
        def _():
            dk_acc[...] = jnp.zeros_like(dk_acc)
            dv_acc[...] = jnp.zeros_like(dv_acc)

        qs = _stack_groups(q_ref[...], MXU_DTYPE)
        dos = _stack_groups(do_ref[...], MXU_DTYPE)
        pt = jnp.exp(_dot_nt(k_ref[...], qs) - lse_ref[0, 0])
        dv_acc[...] += _dot(pt, dos)
        dst = pt * (_dot_nt(v_ref[...], dos) - delta_ref[0, 0])
        dk_acc[...] += _dot(dst, qs)

        @pl.when(i == nq - 1)
        def _():
            dk_ref[...] = dk_acc[...]
            dv_ref[...] = dv_acc[...]

    rowv = pl.BlockSpec((1, 1, 1, GROUPS * tq), lambda h, j, i: (h, i, 0, 0))
    qspec = pl.BlockSpec((tq, KV_W), lambda h, j, i: (i, h))
    kspec = pl.BlockSpec((tk, KV_W), lambda h, j, i: (j, h))
    return pl.pallas_call(
        kern, name="glob_bwd_dkv", grid=(N_KV, nk, nq),
        in_specs=[qspec, kspec, kspec, qspec, rowv, rowv], out_specs=[kspec, kspec],
        out_shape=[SDS((n, Q_W), F32), SDS((n, Q_W), F32)],
        scratch_shapes=[pltpu.VMEM((tk, KV_W), F32), pltpu.VMEM((tk, KV_W), F32)],
        compiler_params=_attn_semantics(),
    )(q, kt, vt, do, lse_row, delta_row)


WQ = GROUPS * WIN


def _win_kv_specs(s_rows, c_rows):
    nb = s_rows // WIN
    blk = lambda f: pl.BlockSpec((WIN, KV_W), lambda h, i: (f(i), h))
    return [blk(lambda i: jnp.maximum(i - 1, 0)), blk(lambda i: i), blk(lambda i: jnp.minimum(i + 1, nb - 1)),
            pl.BlockSpec((c_rows, KV_W), lambda h, i: (s_rows // c_rows, h))]


def _win_mask(i, s_rows, shape):
    row = lax.broadcasted_iota(jnp.int32, shape, 0)
    col = lax.broadcasted_iota(jnp.int32, shape, 1)
    qpos = i * WIN + (row & (WIN - 1))
    kpos = (i - 1) * WIN + col
    band = (jnp.abs(qpos - kpos) <= WIN) & (kpos >= 0) & (kpos < s_rows)
    return (col >= 3 * WIN) | band


def _win_cat(dst, parts):
    off = 0
    for p in parts:
        dst[off:off + p.shape[0], :] = p[...]
        off += p.shape[0]


def win_fwd(q, kt, vt, sinkcol, s_rows, c_rows):
    nb = s_rows // WIN
    nkeys = 3 * WIN + c_rows

    def kern(q_ref, kp, kc, kn, kx, vp, vc, vn, vx, sink_ref, o_ref, lse_ref, kcat, vcat):
        i = pl.program_id(1)
        _win_cat(kcat, (kp, kc, kn, kx))
        _win_cat(vcat, (vp, vc, vn, vx))
        qs = _stack_groups(q_ref[...], MXU_DTYPE)
        s = _dot_nt(qs, kcat[...])
        s = jnp.where(_win_mask(i, s_rows, s.shape), s, -jnp.inf)
        sink = sink_ref[0][:, 0:1]
        m = jnp.maximum(jnp.max(s, axis=1, keepdims=True), sink)
        e = jnp.exp(s - m)
        den = jnp.sum(e, axis=1, keepdims=True) + jnp.exp(sink - m)
        o_ref[...] = _fold_groups(_dot(e / den, vcat[...]), WIN)
        lse_ref[0, 0] = m + jnp.log(den)

    kv = _win_kv_specs(s_rows, c_rows)
    qspec = pl.BlockSpec((WIN, KV_W), lambda h, i: (i, h))
    col = pl.BlockSpec((1, 1, WQ, 1), lambda h, i: (h, i, 0, 0))
    return pl.pallas_call(
        kern, name="win_fwd", grid=(N_KV, nb),
        in_specs=[qspec] + kv + kv + [pl.BlockSpec((1, WQ, LANES), lambda h, i: (h, 0, 0))],
        out_specs=[qspec, col], out_shape=[SDS((s_rows, Q_W), F32), SDS((N_KV, nb, WQ, 1), F32)],
        scratch_shapes=[pltpu.VMEM((nkeys, KV_W), MXU_DTYPE), pltpu.VMEM((nkeys, KV_W), MXU_DTYPE)],
        compiler_params=_cparams(("arbitrary", "arbitrary")),
    )(q, kt, kt, kt, kt, vt, vt, vt, vt, sinkcol)


def win_bwd_dq(q, kt, vt, sinkcol, o, do, lse, s_rows, c_rows):
    nb = s_rows // WIN
    nkeys = 3 * WIN + c_rows

    def kern(q_ref, kp, kc, kn, kx, vp, vc, vn, vx, sink_ref, o_ref, do_ref, lse_ref,
             dq_ref, delta_ref, dkx_ref, dvx_ref, dsk_ref, kcat, vcat):
        i = pl.program_id(1)

        @pl.when(i == 0)
        def _():
            dkx_ref[...] = jnp.zeros_like(dkx_ref)
            dvx_ref[...] = jnp.zeros_like(dvx_ref)
            dsk_ref[...] = jnp.zeros_like(dsk_ref)

        _win_cat(kcat, (kp, kc, kn, kx))
        _win_cat(vcat, (vp, vc, vn, vx))
        qs = _stack_groups(q_ref[...], MXU_DTYPE)
        do32 = _stack_groups(do_ref[...], F32)
        delta = jnp.sum(do32 * _stack_groups(o_ref[...], F32), axis=1, keepdims=True)
        dos = do32.astype(MXU_DTYPE)
        lse_c = lse_ref[0, 0]
        s = _dot_nt(qs, kcat[...])
        s = jnp.where(_win_mask(i, s_rows, s.shape), s, -jnp.inf)
        p = jnp.exp(s - lse_c)
        ds = p * (_dot_nt(dos, vcat[...]) - delta)
        dq_ref[...] = _fold_groups(_dot(ds, kcat[...]), WIN)
        delta_ref[0, 0] = delta
        dkx_ref[0] += _dot_tn(ds[:, 3 * WIN:], qs)
        dvx_ref[0] += _dot_tn(p[:, 3 * WIN:], dos)
        dsk_ref[0] += -(jnp.exp(sink_ref[0][:, 0:1] - lse_c) * delta)

    kv = _win_kv_specs(s_rows, c_rows)
    qspec = pl.BlockSpec((WIN, KV_W), lambda h, i: (i, h))
    col = pl.BlockSpec((1, 1, WQ, 1), lambda h, i: (h, i, 0, 0))
    xspec = pl.BlockSpec((1, c_rows, KV_W), lambda h, i: (h, 0, 0))
    return pl.pallas_call(
        kern, name="win_bwd_dq", grid=(N_KV, nb),
        in_specs=[qspec] + kv + kv + [pl.BlockSpec((1, WQ, LANES), lambda h, i: (h, 0, 0)), qspec, qspec, col],
        out_specs=[qspec, col, xspec, xspec, pl.BlockSpec((1, WQ, 1), lambda h, i: (h, 0, 0))],
        out_shape=[SDS((s_rows, Q_W), F32), SDS((N_KV, nb, WQ, 1), F32), SDS((N_KV, c_rows, KV_W), F32),
                   SDS((N_KV, c_rows, KV_W), F32), SDS((N_KV, WQ, 1), F32)],
        scratch_shapes=[pltpu.VMEM((nkeys, KV_W), MXU_DTYPE), pltpu.VMEM((nkeys, KV_W), MXU_DTYPE)],
        compiler_params=_cparams(("arbitrary", "arbitrary")),
    )(q, kt, kt, kt, kt, vt, vt, vt, vt, sinkcol, o, do, lse)


def win_bwd_dkv(q, kt, vt, do, lse, delta, s_rows):
    nb = s_rows // WIN

    def kern(k_ref, v_ref, qp, qc, qn, dop, doc, don, lp, lc, ln, dp_, dc_, dn_, dk_ref, dv_ref):
        j = pl.program_id(1)
        k = k_ref[...]
        v = v_ref[...]
        dk = jnp.zeros((WIN, KV_W), F32)
        dv = jnp.zeros((WIN, KV_W), F32)
        for b, (q_r, do_r, l_r, d_r) in enumerate(((qp, dop, lp, dp_), (qc, doc, lc, dc_), (qn, don, ln, dn_))):
            ib = j - 1 + b
            qs = _stack_groups(q_r[...], MXU_DTYPE)
            dos = _stack_groups(do_r[...], MXU_DTYPE)
            s = _dot_nt(qs, k)
            row = lax.broadcasted_iota(jnp.int32, s.shape, 0)
            col = lax.broadcasted_iota(jnp.int32, s.shape, 1)
            near = jnp.abs(ib * WIN + (row & (WIN - 1)) - (j * WIN + col)) <= WIN
            ok = near & (ib >= 0) & (ib < nb)
            p = jnp.where(ok, jnp.exp(s - l_r[0, 0]), 0.0)
            dv = dv + _dot_tn(p, dos)
            ds = p * (_dot_nt(dos, v) - d_r[0, 0])
            dk = dk + _dot_tn(ds, qs)
        dk_ref[...] = dk
        dv_ref[...] = dv

    fs = (lambda j: jnp.maximum(j - 1, 0), lambda j: j, lambda j: jnp.minimum(j + 1, nb - 1))
    qspecs = [pl.BlockSpec((WIN, KV_W), lambda h, j, f=f: (f(j), h)) for f in fs]
    cols = [pl.BlockSpec((1, 1, WQ, 1), lambda h, j, f=f: (h, f(j), 0, 0)) for f in fs]
    kspec = pl.BlockSpec((WIN, KV_W), lambda h, j: (j, h))
    return pl.pallas_call(
        kern, name="win_bwd_dkv", grid=(N_KV, nb),
        in_specs=[kspec, kspec] + qspecs + qspecs + cols + cols, out_specs=[kspec, kspec],
        out_shape=[SDS((s_rows, Q_W), F32), SDS((s_rows, Q_W), F32)],
        compiler_params=_cparams(("arbitrary", "arbitrary")),
    )(kt, vt, q, q, q, do, do, do, lse, lse, lse, delta, delta, delta)


def adamw(name, w, m, v, grads):
    r, wd = w.shape
    tr = _pick(r, [t for t in (1408, 1024, 512, 256, 128, 64, 32, 16, 8) if t * wd * 4 <= ELEMENTWISE_BLOCK_BYTES])
    stacked = not isinstance(grads, (list, tuple))
    ng = grads.shape[0] if stacked else len(grads)

    def kern(*refs):
        w_ref, m_ref, v_ref = refs[:3]
        g_refs = refs[3:-4]
        g_out, d_out, m_out, v_out = refs[-4:]
        if stacked:
            g = g_refs[0][0]
            for k in range(1, ng):
                g = g + g_refs[0][k]
        else:
            g = g_refs[0][...]
            for gr in g_refs[1:]:
                g = g + gr[...]
        wv = w_ref[...]
        mn = ADAM_B1 * m_ref[...] + (1.0 - ADAM_B1) * g
        vn = ADAM_B2 * v_ref[...] + (1.0 - ADAM_B2) * (g * g)
        m_hat = mn / (1.0 - ADAM_B1 ** ADAM_STEP)
        v_hat = vn / (1.0 - ADAM_B2 ** ADAM_STEP)
        g_out[...] = g
        d_out[...] = -ADAM_LR * (m_hat / (jnp.sqrt(v_hat) + ADAM_EPS) + ADAM_WD * wv)
        m_out[...] = mn
        v_out[...] = vn

    spec = pl.BlockSpec((tr, wd), lambda i: (i, 0))
    gspecs = [pl.BlockSpec((ng, tr, wd), lambda i: (0, i, 0))] if stacked else [spec] * ng
    return pl.pallas_call(
        kern, name=name, grid=(r // tr,), in_specs=[spec] * 3 + gspecs, out_specs=[spec] * 4,
        out_shape=[SDS((r, wd), F32)] * 4, compiler_params=_cparams(("parallel",)),
    )(w, m, v, *([grads] if stacked else grads))


def add2(name, a, b):
    k, r, w = a.shape
    tr = _pick(r, (1408, 1024, 512, 256, 128, 64, 32, 16, 8))

    def kern(a_ref, b_ref, o_ref):
        o_ref[...] = a_ref[...] + b_ref[...]

    spec = pl.BlockSpec((1, tr, w), lambda s, i: (s, i, 0))
    return pl.pallas_call(kern, name=name, grid=(k, r // tr), in_specs=[spec, spec], out_specs=spec,
                          out_shape=SDS(a.shape, a.dtype), compiler_params=_cparams(("parallel", "parallel")))(a, b)


def _rep4(a, off):
    return jnp.concatenate([a[:, off + HEAD_DIM * h: off + HEAD_DIM * (h + 1)] for h in range(N_KV) for _ in range(GROUPS)], axis=1)


def _extend_cols(a):
    return jnp.concatenate([a[:, 0:OFF_KA], _rep4(a, OFF_KA), _rep4(a, OFF_VA), a[:, OFF_QB:OFF_KB],
                            _rep4(a, OFF_KB), _rep4(a, OFF_VB), a[:, OFF_GA:]], axis=1)


def _fold4(a, off):
    r = a.shape[0]
    return a[:, off:off + Q_W].reshape(r, N_KV, GROUPS, HEAD_DIM).sum(axis=2).reshape(r, N_KV * HEAD_DIM)


def _fold_cols(a):
    return jnp.concatenate([a[:, X_QA:X_QA + Q_W], _fold4(a, X_KA), _fold4(a, X_VA), a[:, X_QB:X_QB + Q_W],
                            _fold4(a, X_KB), _fold4(a, X_VB), a[:, X_GL:]], axis=1)


def _rope_tables(s_rows, c_rows):
    pos = jnp.arange(s_rows, dtype=jnp.int32)
    rows = (pos // GRID_W).astype(F32)
    cols = (pos % GRID_W).astype(F32)
    n_freq = HEAD_DIM // 4
    inv_freq = ROPE_THETA ** (-jnp.arange(n_freq, dtype=F32) / n_freq)
    ang_r = rows[:, None] * inv_freq
    ang_c = cols[:, None] * inv_freq
    cos = jnp.concatenate([jnp.cos(ang_r)] * 2 + [jnp.cos(ang_c)] * 2, axis=1)
    sin = jnp.concatenate([-jnp.sin(ang_r), jnp.sin(ang_r), -jnp.sin(ang_c), jnp.sin(ang_c)], axis=1)
    cos = jnp.concatenate([cos, jnp.ones((c_rows, HEAD_DIM), F32)], axis=0)
    sin = jnp.concatenate([sin, jnp.zeros((c_rows, HEAD_DIM), F32)], axis=0)
    return jnp.concatenate([cos, cos], axis=1), jnp.concatenate([sin, sin], axis=1)


BIG = (("w_in", (D, IN_COLS // N_DEV)), ("w_branch_a", (Q_W, D // N_DEV)), ("w_branch_b", (Q_W, D // N_DEV)),
       ("w_out", (D // N_DEV, D)), ("w_up", (D, 2 * D_FF // N_DEV)), ("w_down", (D_FF // N_DEV, D)))
BIG_SIZES = tuple(int(np.prod(s)) for _, s in BIG)
BIG_ROWS = sum(BIG_SIZES) // LANES


def _pack_big(parts):
    return jnp.concatenate([p.reshape(-1) for p in parts]).reshape(BIG_ROWS, LANES)


def _unpack_big(flat):
    lead = flat.shape[:-2]
    f = flat.reshape(*lead, BIG_ROWS * LANES)
    out, off = [], 0
    for (_, shp), sz in zip(BIG, BIG_SIZES, strict=True):
        out.append(f[..., off:off + sz].reshape(*lead, *shp))
        off += sz
    return out


def _cols_to_full(g):
    return jnp.transpose(g, (1, 0, 2)).reshape(g.shape[1], -1)


def _full_to_cols(a):
    r, c = a.shape
    return jnp.transpose(a.reshape(r, N_DEV, c // N_DEV), (1, 0, 2))


SMALL = (("c_ctx", D), ("b_mod", N_MOD * D), ("b_in", IN_COLS), ("attn_sink", N_HEADS), ("q_norm_g", HEAD_DIM),
         ("k_norm_g", HEAD_DIM), ("ln1_g", D), ("ln1_b", D), ("conv_w", 3 * 2 * D_FF // N_DEV), ("conv_b", 2 * D_FF),
         ("ln2_g", D), ("ln2_b", D))
SMALL_TOTAL = sum(n for _, n in SMALL)
SMALL_ROWS = -(-SMALL_TOTAL // (8 * LANES)) * 8


def _pack_small(parts):
    flat = jnp.concatenate([p.reshape(-1).astype(F32) for p in parts])
    return jnp.pad(flat, (0, SMALL_ROWS * LANES - flat.shape[0])).reshape(SMALL_ROWS, LANES)


def _unpack_small(packed):
    f = packed.reshape(-1)
    out, off = {}, 0
    for name, n in SMALL:
        out[name] = f[off:off + n]
        off += n
    return out


RED = (("c_ctx", D), ("b_in", IN_COLS), ("attn_sink", N_HEADS), ("q_norm_g", HEAD_DIM), ("k_norm_g", HEAD_DIM),
       ("ln1_g", D), ("ln1_b", D), ("conv_w", 3 * 2 * D_FF), ("conv_b", 2 * D_FF), ("ln2_g", D), ("ln2_b", D))
RED_TOTAL = sum(n for _, n in RED)
RED_ROWS = -(-RED_TOTAL // (8 * LANES)) * 8


def sum8(name, g):
    _, r, w = g.shape

    def kern(g_ref, o_ref):
        acc = g_ref[0]
        for k in range(1, N_DEV):
            acc = acc + g_ref[k]
        o_ref[...] = acc

    return pl.pallas_call(kern, name=name, out_shape=SDS((r, w), F32))(g)


def _local_step(x, ctx, target, modrows, weights, small):
    s_rows, c_rows = x.shape[0], ctx.shape[0]
    n = s_rows + c_rows
    nl = s_rows // TM
    w_in, wba, wbb, w_out, w_up, w_down = weights
    f = lambda a: a.reshape(1, -1).astype(F32)
    b_in, ln1_g, ln1_b, ln2_g, ln2_b, conv_b = (f(small[k]) for k in ("b_in", "ln1_g", "ln1_b", "ln2_g", "ln2_b", "conv_b"))
    conv_w8 = jnp.pad(small["conv_w_full"], ((0, 5), (0, 0)))
    qg = jnp.tile(small["q_norm_g"].reshape(1, HEAD_DIM), (1, N_HEADS))
    kg = jnp.tile(small["k_norm_g"].reshape(1, HEAD_DIM), (1, N_HEADS))
    sinkcol = jnp.broadcast_to(jnp.repeat(small["attn_sink"].reshape(N_KV, GROUPS), WIN, axis=1)[:, :, None], (N_KV, WQ, LANES))
    bd = jnp.kron(jnp.eye(N_HEADS, dtype=F32), jnp.ones((HEAD_DIM, HEAD_DIM), F32)).astype(BF16)
    cos, sin = _rope_tables(s_rows, c_rows)
    w_ext = _extend_cols(w_in)
    b_ext = _extend_cols(b_in)
    xa = jnp.concatenate([x, ctx], axis=0)

    hb, qa, kat, vat, qb, kbt, vbt, tq, rq, tk, rk, gl = inproj_fwd(xa, cos, sin, modrows, w_ext, b_ext, qg, kg, bd, nl)
    oa, lse_a = win_fwd(qa, kat, vat, sinkcol, s_rows, c_rows)
    ob, lse_b = glob_fwd(qb, kbt, vbt, s_rows)
    ya, yb, mrg, y, xhat1, rstd1 = merge_fwd(oa, ob, gl, x, modrows, wba, wbb, w_out, s_rows)
    h2, u0 = ffn_up_fwd(xhat1, modrows, ln1_g, ln1_b, w_up, s_rows)
    a = conv_swiglu_fwd(u0, conv_w8, conv_b, s_rows)
    (dr2, dy2), (loss, dln2_g, dln2_b, dgate2) = ffn_down_loss(a, xhat1, target, modrows, ln1_g, ln1_b, ln2_g, ln2_b, w_down, s_rows)

    da = ffn_down_bwd(dy2, w_down.T, s_rows)
    dw_down = mm_tn("dw_down", a, dy2, s_rows)
    du, (dconv_b, dcw0, dcw1, dcw2) = swiglu_conv_bwd(u0, da, conv_w8, conv_b, s_rows)
    du0 = conv_bwd_input(du, conv_w8, s_rows)
    dw_up = mm_tn("dw_up", h2, du0, s_rows)
    (dy, dxp), (dscale2, dshift2, dln1_g, dln1_b, dgate1) = ffn_up_ln1_bwd(du0, dr2, xhat1, y, rstd1, modrows, ln1_g, ln1_b, w_up.T, s_rows)
    dya, dyb, dgl, doa, dob = merge_bwd(dy, ya, yb, gl, w_out.T, wba.T, wbb.T, s_rows)
    dw_out = mm_tn("dw_out", mrg, dy, s_rows)
    dwba = mm_tn("dw_branch_a", oa, dya, s_rows)
    dwbb = mm_tn("dw_branch_b", ob, dyb, s_rows)

    dqa, delta_a, dkax, dvax, dsk = win_bwd_dq(qa, kat, vat, sinkcol, oa, doa, lse_a, s_rows, c_rows)
    dka, dva = win_bwd_dkv(qa, kat, vat, doa, lse_a, delta_a, s_rows)
    dqb, delta_b = glob_bwd_dq(qb, kbt, vbt, ob, dob, lse_b, s_rows)
    to_row = lambda t: t.reshape(t.shape[0], t.shape[1], 1, t.shape[2])
    dkb, dvb = glob_bwd_dkv(qb, kbt, vbt, dob, to_row(lse_b), to_row(delta_b), s_rows)
    ctx_cols = lambda t: jnp.transpose(t, (1, 0, 2)).reshape(c_rows, Q_W)
    dproj, (db_ext, dqg, dkg) = qk_bwd(dqa, dka, ctx_cols(dkax), dva, ctx_cols(dvax), dqb, dkb, dvb, dgl, tq, rq, tk, rk,
                                       cos, sin, qg, kg, bd, nl, n)
    w_ext_t = w_ext.T
    (grad_x,), (dscale1, dshift1) = inproj_bwd("inproj_bwd", dproj, xa, dxp, modrows, w_ext_t, ntiles=nl, tile_off=0,
                                               is_ctx=False, out_rows=s_rows)
    _, (dscale_c, dshift_c) = inproj_bwd("inproj_bwd_ctx", dproj, xa, None, modrows, w_ext_t, ntiles=c_rows // TM,
                                         tile_off=nl, is_ctx=True, out_rows=0)
    dw_in = _fold_cols(mm_tn("dw_in", hb, dproj, n))

    dmod = jnp.concatenate([dshift1, dscale1, dgate1, dshift2, dscale2, dgate2], axis=1)
    dmod_c = jnp.concatenate([dshift_c, dscale_c, jnp.zeros((1, (N_MOD - 2) * D), F32)], axis=1)
    fold_g = lambda t: t.reshape(N_HEADS, HEAD_DIM).sum(axis=0)
    red = {
        "b_in": _fold_cols(db_ext), "attn_sink": dsk.reshape(N_HEADS, WIN).sum(axis=1), "q_norm_g": fold_g(dqg),
        "k_norm_g": fold_g(dkg), "ln1_g": dln1_g, "ln1_b": dln1_b, "conv_w": jnp.concatenate([dcw0, dcw1, dcw2], axis=0),
        "conv_b": dconv_b, "ln2_g": dln2_g, "ln2_b": dln2_b,
    }
    return loss[0, 0], grad_x, (dw_in, dwba, dwbb, dw_out, dw_up, dw_down), dmod, dmod_c, red


def kernel(x, c, ctx, c_ctx, w_mod, b_mod, w_in, b_in, attn_sink, q_norm_g, k_norm_g, w_branch_a, w_branch_b, w_out, ln1_g, ln1_b, w_up, conv_w, conv_b, w_down, ln2_g, ln2_b, loss_target, m_c_ctx, m_w_mod, m_b_mod, m_w_in, m_b_in, m_attn_sink, m_q_norm_g, m_k_norm_g, m_w_branch_a, m_w_branch_b, m_w_out, m_ln1_g, m_ln1_b, m_w_up, m_conv_w, m_conv_b, m_w_down, m_ln2_g, m_ln2_b, v_c_ctx, v_w_mod, v_b_mod, v_w_in, v_b_in, v_attn_sink, v_q_norm_g, v_k_norm_g, v_w_branch_a, v_w_branch_b, v_w_out, v_ln1_g, v_ln1_b, v_w_up, v_conv_w, v_conv_b, v_w_down, v_ln2_g, v_ln2_b):
    ax, ay, ac = (lax.axis_index(a) for a in AXES)
    me = 4 * ax + 2 * ay + ac
    chip = 2 * ax + ay
    mod_w = N_MOD * D // N_DEV
    params = dict(c_ctx=c_ctx, w_mod=w_mod, b_mod=b_mod, w_in=w_in, b_in=b_in, attn_sink=attn_sink, q_norm_g=q_norm_g,
                  k_norm_g=k_norm_g, w_branch_a=w_branch_a, w_branch_b=w_branch_b, w_out=w_out, ln1_g=ln1_g, ln1_b=ln1_b,
                  w_up=w_up, conv_w=conv_w, conv_b=conv_b, w_down=w_down, ln2_g=ln2_g, ln2_b=ln2_b)
    mom_m = dict(c_ctx=m_c_ctx, w_mod=m_w_mod, b_mod=m_b_mod, w_in=m_w_in, b_in=m_b_in, attn_sink=m_attn_sink,
                 q_norm_g=m_q_norm_g, k_norm_g=m_k_norm_g, w_branch_a=m_w_branch_a, w_branch_b=m_w_branch_b, w_out=m_w_out,
                 ln1_g=m_ln1_g, ln1_b=m_ln1_b, w_up=m_w_up, conv_w=m_conv_w, conv_b=m_conv_b, w_down=m_w_down,
                 ln2_g=m_ln2_g, ln2_b=m_ln2_b)
    mom_v = dict(c_ctx=v_c_ctx, w_mod=v_w_mod, b_mod=v_b_mod, w_in=v_w_in, b_in=v_b_in, attn_sink=v_attn_sink,
                 q_norm_g=v_q_norm_g, k_norm_g=v_k_norm_g, w_branch_a=v_w_branch_a, w_branch_b=v_w_branch_b, w_out=v_w_out,
                 ln1_g=v_ln1_g, ln1_b=v_ln1_b, w_up=v_w_up, conv_w=v_conv_w, conv_b=v_conv_b, w_down=v_w_down,
                 ln2_g=v_ln2_g, ln2_b=v_ln2_b)
    big_names = [nm for nm, _ in BIG]

    wg = all_gather("ag_weights", _pack_big([params[nm][0].astype(MXU_DTYPE) for nm in big_names]))
    g_in, g_ba, g_bb, g_out, g_up, g_down = _unpack_big(wg)
    weights = (_cols_to_full(g_in), _cols_to_full(g_ba), _cols_to_full(g_bb), g_out.reshape(D, D), _cols_to_full(g_up),
               g_down.reshape(D_FF, D))

    c_all = all_gather("ag_c", c.reshape(8, LANES)).reshape(N_DEV, D)
    cs = jnp.concatenate([c_all, c_ctx.reshape(1, D), jnp.zeros((7, D), F32)], axis=0)
    w_mod_sh = w_mod[0]
    b_mod_sh = lax.dynamic_slice(b_mod, (0, me * mod_w), (1, mod_w))
    mod_part = mod_fwd(cs, w_mod_sh, b_mod_sh)
    mg = all_gather("ag_mod", mod_part.reshape(16 * mod_w // LANES, LANES)).reshape(N_DEV, 16, mod_w)
    mod = lax.dynamic_index_in_dim(mg, me, axis=1, keepdims=False).reshape(N_MOD, D)
    mod_c = mg[:, 8, :].reshape(N_MOD, D)
    modrows = jnp.stack([mod[0], mod[1], mod_c[0], mod_c[1], mod[2], mod[3], mod[4], mod[5]], axis=0)

    conv_w_full = all_gather("ag_conv_w", jnp.pad(conv_w[0], ((0, 5), (0, LANES * 6 - 2 * D_FF // N_DEV))))
    conv_w_full = _cols_to_full(conv_w_full[:, :3, :2 * D_FF // N_DEV])
    small = dict(b_in=b_in, ln1_g=ln1_g, ln1_b=ln1_b, ln2_g=ln2_g, ln2_b=ln2_b, conv_b=conv_b, conv_w_full=conv_w_full,
                 q_norm_g=q_norm_g, k_norm_g=k_norm_g, attn_sink=attn_sink)
    loss, grad_x, big_grads, dmod, dmod_c, red = _local_step(x[0], ctx[0], loss_target[0], modrows, weights, small)
    loss = lax.psum(loss, AXES)

    dm = all_gather("ag_dmod", jnp.concatenate([dmod, dmod_c], axis=0).reshape(2 * N_MOD * D // LANES, LANES))
    dm = dm.reshape(N_DEV, 2, N_MOD * D)
    dm_all = jnp.concatenate([dm[:, 0], dm[:, 1]], axis=0)
    dm_sh = lax.dynamic_slice(dm_all, (0, me * mod_w), (16, mod_w))
    dw_mod, dcc, db_mod = mod_bwd(cs, w_mod_sh, dm_sh, dm_all)
    red["c_ctx"] = dcc[8]

    red_vec = jnp.concatenate([red[nm].reshape(-1) for nm, _ in RED])
    red_vec = jnp.pad(red_vec, (0, RED_ROWS * LANES - RED_TOTAL)).reshape(RED_ROWS, LANES)
    red_sum = sum8("sum_small", all_gather("ag_small", red_vec)).reshape(-1)
    gsm, off = {}, 0
    for nm, k in RED:
        gsm[nm] = red_sum[off:off + k]
        off += k
    gsm["b_mod"] = db_mod.reshape(-1)
    cw_sh = 2 * D_FF // N_DEV
    gsm["conv_w"] = lax.dynamic_slice(gsm["conv_w"].reshape(3, 2 * D_FF), (0, me * cw_sh), (3, cw_sh))
    sm_names = [nm for nm, _ in SMALL]
    gs, ds, ms, vs = adamw("adamw_small", _pack_small([params[nm] for nm in sm_names]),
                           _pack_small([mom_m[nm] for nm in sm_names]), _pack_small([mom_v[nm] for nm in sm_names]),
                           [_pack_small([gsm[nm] for nm in sm_names])])
    sm_out = [_unpack_small(t) for t in (gs, ds, ms, vs)]

    dw_in, dwba, dwbb, dw_out, dw_up, dw_down = big_grads
    slabs = jnp.stack([_pack_big(parts) for parts in zip(
        _full_to_cols(dw_in), _full_to_cols(dwba), _full_to_cols(dwbb), dw_out.reshape(N_DEV, D // N_DEV, D),
        _full_to_cols(dw_up), dw_down.reshape(N_DEV, D_FF // N_DEV, D), strict=True)])
    by_core = slabs.reshape(4, 2, BIG_ROWS, LANES)
    keep = lax.dynamic_index_in_dim(by_core, ac, axis=1, keepdims=False)
    give = lax.dynamic_index_in_dim(by_core, 1 - ac, axis=1, keepdims=False)
    got = exchange("rs_sibling", give.reshape(1, 4 * BIG_ROWS, LANES), to_chips=False).reshape(4, BIG_ROWS, LANES)
    pair = add2("rs_pair_sum", keep, got)
    outbox = jnp.stack([lax.dynamic_index_in_dim(pair, jnp.bitwise_xor(chip, m), axis=0, keepdims=False) for m in (1, 2, 3)])
    inbox = exchange("rs_chips", outbox, to_chips=True)
    mine = lax.dynamic_index_in_dim(pair, chip, axis=0, keepdims=False)
    gb, db, mb, vb = adamw("adamw_big", _pack_big([params[nm][0] for nm in big_names]),
                           _pack_big([mom_m[nm][0] for nm in big_names]), _pack_big([mom_v[nm][0] for nm in big_names]),
                           [mine, inbox[0], inbox[1], inbox[2]])
    big_out = [dict(zip(big_names, _unpack_big(t), strict=True)) for t in (gb, db, mb, vb)]
    gm, dmo, mmo, vmo = adamw("adamw_mod", w_mod[0], m_w_mod[0], v_w_mod[0], [dw_mod])
    mod_out = (gm, dmo, mmo, vmo)

    order = ["c_ctx", "w_mod", "b_mod", "w_in", "b_in", "attn_sink", "q_norm_g", "k_norm_g", "w_branch_a", "w_branch_b",
             "w_out", "ln1_g", "ln1_b", "w_up", "conv_w", "conv_b", "w_down", "ln2_g", "ln2_b"]
    results = [loss, grad_x[None]]
    for kind in range(4):
        for nm in order:
            if nm == "w_mod":
                val = mod_out[kind]
            elif nm in big_out[kind]:
                val = big_out[kind][nm]
            else:
                val = sm_out[kind][nm]
            results.append(val.reshape(params[nm].shape))
    return tuple(results)
```

```python
import functools

import jax
import jax.numpy as jnp
import numpy as np
from jax import lax
from jax.experimental import pallas as pl
from jax.experimental.pallas import tpu as pltpu

F32 = jnp.float32
BF16 = jnp.bfloat16
MXU_DTYPE = BF16

AXES = ("x", "y", "c")
N_DEV = 8
D = 1024
HEAD_DIM = 64
N_HEADS = 8
N_KV = 2
GROUPS = 4
KV_W = GROUPS * HEAD_DIM
Q_W = N_HEADS * HEAD_DIM
GRID_W = 64
WIN = 128
ROPE_THETA = 10000.0
D_FF = 2816
LN_EPS = 1e-5
QK_EPS = 1e-6
N_MOD = 6
ALPHA = 2.0 ** 0.25
Q_SCALE = HEAD_DIM ** -0.5
IN_COLS = 3584
OFF_KA, OFF_VA, OFF_QB, OFF_KB, OFF_VB, OFF_GA = 512, 640, 768, 1280, 1408, 1536
EXT_COLS = 6 * Q_W + 2 * D
X_QA, X_KA, X_VA, X_QB, X_KB, X_VB, X_GL = 0, 512, 1024, 1536, 2048, 2560, 3072
ADAM_LR, ADAM_B1, ADAM_B2, ADAM_EPS, ADAM_WD, ADAM_STEP = 0.001, 0.9, 0.999, 1e-08, 0.01, 10
LANES = 128
TM = 256
VMEM_LIMIT = 56 * 1024 * 1024
ELEMENTWISE_BLOCK_BYTES = 1 << 20

ANY = pl.BlockSpec(memory_space=pl.ANY)
SDS = jax.ShapeDtypeStruct


def _pick(n, candidates):
    for t in candidates:
        if n % t == 0:
            return t
    raise ValueError(f"no tile for {n}")


def _full(a):
    nd = a.ndim
    return pl.BlockSpec(a.shape, lambda *_: (0,) * nd)


def _rows(tm, w, fn=lambda t: t):
    return pl.BlockSpec((tm, w), lambda i: (fn(i), 0))


def _dot(a, b):
    return jnp.dot(a.astype(MXU_DTYPE), b.astype(MXU_DTYPE), preferred_element_type=F32)


def _dot_nt(a, b):
    return lax.dot_general(a.astype(MXU_DTYPE), b.astype(MXU_DTYPE), (((1,), (1,)), ((), ())), preferred_element_type=F32)


def _dot_tn(a, b):
    return lax.dot_general(a.astype(MXU_DTYPE), b.astype(MXU_DTYPE), (((0,), (0,)), ((), ())), preferred_element_type=F32)


def _cparams(sem):
    return pltpu.CompilerParams(dimension_semantics=sem, vmem_limit_bytes=VMEM_LIMIT)


def all_gather(name, v):
    r, w = v.shape

    def body(x_ref, out_ref, send_sems, recv_sems, local_sem):
        x, y, c = (lax.axis_index(a) for a in AXES)
        me, sibling = (x, y, c), (x, y, 1 - c)
        chips = [(1 - x, y), (x, 1 - y), (1 - x, 1 - y)]

        def rows(px, py, pc):
            return out_ref.at[4 * px + 2 * py + pc]

        def copy(k, block, to, src=None):
            return pltpu.make_async_remote_copy(
                src_ref=rows(*block) if src is None else src, dst_ref=rows(*block),
                send_sem=send_sems.at[k], recv_sem=recv_sems.at[k],
                device_id=to, device_id_type=pl.DeviceIdType.MESH)

        mine = pltpu.make_async_copy(x_ref, rows(*me), local_sem)
        mine.start()
        first = [copy(0, me, sibling, src=x_ref)]
        first += [copy(1 + j, me, (*chip, c), src=x_ref) for j, chip in enumerate(chips)]
        for cp in first:
            cp.start()
        passed = [copy(4 + j, (*chip, c), sibling) for j, chip in enumerate(chips)]
        for j, chip in enumerate(chips):
            copy(1 + j, (*chip, c), me).wait_recv()
            passed[j].start()
        copy(0, sibling, me).wait_recv()
        for j, chip in enumerate(chips):
            copy(4 + j, (*chip, 1 - c), me).wait_recv()
        for cp in first + passed:
            cp.wait_send()
        mine.wait()

    return pl.pallas_call(
        body, name=name, out_shape=SDS((N_DEV, r, w), v.dtype), in_specs=[ANY], out_specs=ANY,
        scratch_shapes=[pltpu.SemaphoreType.DMA((7,)), pltpu.SemaphoreType.DMA((7,)), pltpu.SemaphoreType.DMA],
    )(v)


def exchange(name, outbox, to_chips):
    k = outbox.shape[0]
    assert k == (3 if to_chips else 1)

    def body(out_ref, in_ref, send_sems, recv_sems):
        x, y, c = (lax.axis_index(a) for a in AXES)
        peers = [(x, 1 - y, c), (1 - x, y, c), (1 - x, 1 - y, c)] if to_chips else [(x, y, 1 - c)]
        copies = [
            pltpu.make_async_remote_copy(
                src_ref=out_ref.at[m], dst_ref=in_ref.at[m], send_sem=send_sems.at[m], recv_sem=recv_sems.at[m],
                device_id=peer, device_id_type=pl.DeviceIdType.MESH)
            for m, peer in enumerate(peers)
        ]
        for cp in copies:
            cp.start()
        for cp in copies:
            cp.wait_recv()
        for cp in copies:
            cp.wait_send()

    return pl.pallas_call(
        body, name=name, out_shape=SDS(outbox.shape, outbox.dtype), in_specs=[ANY], out_specs=ANY,
        scratch_shapes=[pltpu.SemaphoreType.DMA((k,)), pltpu.SemaphoreType.DMA((k,))],
    )(outbox)


def rowwise(name, body, *, ntiles, tile_off=0, tiled, full, outs, accs=()):
    nt, nf, no = len(tiled), len(full), len(outs)

    def kern(*refs):
        i = pl.program_id(0)
        out_vals, incs = body(i + tile_off, refs[:nt], refs[nt:nt + nf])
        for r, v in zip(refs[nt + nf:nt + nf + no], out_vals, strict=True):
            r[...] = v.astype(r.dtype)
        acc_refs = refs[nt + nf + no:]

        @pl.when(i == 0)
        def _():
            for r in acc_refs:
                r[...] = jnp.zeros_like(r)

        for r, v in zip(acc_refs, incs, strict=True):
            r[...] += v

    res = pl.pallas_call(
        kern, name=name, grid=(ntiles,),
        in_specs=[s for _, s in tiled] + [_full(a) for a in full],
        out_specs=[s for _, _, s in outs] + [pl.BlockSpec(s, lambda i, n=len(s): (0,) * n) for s in accs],
        out_shape=[SDS(s, d) for s, d, _ in outs] + [SDS(s, F32) for s in accs],
        compiler_params=_cparams(("arbitrary",) if accs else ("parallel",)),
    )(*[a for a, _ in tiled], *full)
    return res[:no], res[no:]


def mm_tn(name, a, b, rows):
    ka, nb = a.shape[1], b.shape[1]
    tr = _pick(rows, (1280, 1024, 768, 512, 256))
    tn = _pick(nb, (512, 256, 128))

    def kern(a_ref, b_ref, o_ref):
        @pl.when(pl.program_id(1) == 0)
        def _():
            o_ref[...] = jnp.zeros_like(o_ref)

        o_ref[...] += _dot_tn(a_ref[...], b_ref[...])

    return pl.pallas_call(
        kern, name=name, grid=(nb // tn, rows // tr),
        in_specs=[pl.BlockSpec((tr, ka), lambda n, r: (r, 0)), pl.BlockSpec((tr, tn), lambda n, r: (r, n))],
        out_specs=pl.BlockSpec((ka, tn), lambda n, r: (0, n)), out_shape=SDS((ka, nb), F32),
        compiler_params=_cparams(("parallel", "arbitrary")),
    )(a, b)


def _swap16(t):
    w = t.shape[1]
    lane = lax.broadcasted_iota(jnp.int32, t.shape, 1)
    return jnp.where((lane & 16) == 0, pltpu.roll(t, w - 16, 1), pltpu.roll(t, 16, 1))


def _rope(t, cos, sin):
    return t * cos + _swap16(t) * sin


def _rope_t(d, cos, sin):
    return d * cos - _swap16(d) * sin


def _seg_sum64(a, bd_ref):
    bd = bd_ref[...]
    hi = a.astype(BF16)
    lo = (a - hi.astype(F32)).astype(BF16)
    return jnp.dot(hi, bd, preferred_element_type=F32) + jnp.dot(lo, bd, preferred_element_type=F32)


def _lane_block(shape):
    return jnp.right_shift(lax.broadcasted_iota(jnp.int32, shape, 1), 6)


def _stack_groups(t, dtype):
    blk = _lane_block(t.shape)
    return jnp.concatenate([jnp.where(blk == g, t, jnp.zeros_like(t)).astype(dtype) for g in range(GROUPS)], axis=0)


def _fold_groups(ts, tq):
    blk = _lane_block((tq, KV_W))
    out = jnp.zeros((tq, KV_W), ts.dtype)
    for g in range(GROUPS):
        out = jnp.where(blk == g, ts[g * tq:(g + 1) * tq], out)
    return out


def _layer_norm_bwd(dxh, xhat, rstd):
    m1 = jnp.mean(dxh, axis=1, keepdims=True)
    m2 = jnp.mean(dxh * xhat, axis=1, keepdims=True)
    return rstd * (dxh - m1 - xhat * m2)


def _colsum(a):
    return jnp.sum(a, axis=0, keepdims=True)


def _shifted_rows(t, prev_row, next_row):
    n = t.shape[0]
    row = lax.broadcasted_iota(jnp.int32, t.shape, 0)
    up = jnp.where(row == 0, prev_row, pltpu.roll(t, 1, 0))
    dn = jnp.where(row == n - 1, next_row, pltpu.roll(t, n - 1, 0))
    return up, dn


def mod_fwd(cs, w_sh, b_sh):
    def kern(c_ref, w_ref, b_ref, o_ref):
        o_ref[...] = _dot(jax.nn.silu(c_ref[...]), w_ref[...]) + b_ref[...]

    return pl.pallas_call(kern, name="mod_fwd", out_shape=SDS((16, w_sh.shape[1]), F32),
                          compiler_params=pltpu.CompilerParams(vmem_limit_bytes=VMEM_LIMIT))(cs, w_sh, b_sh)


def mod_bwd(cs, w_sh, dm_sh, dm_all):
    hp = lax.Precision.HIGHEST

    def kern(c_ref, w_ref, dm_ref, da_ref, dw_ref, dc_ref, db_ref):
        c = c_ref[...]
        sg = jax.nn.sigmoid(c)
        sc = c * sg
        dm = dm_ref[...]
        dmc = dm_ref[8:9, :]
        for i in range(9, 16):
            dmc = dmc + dm_ref[i:i + 1, :]
        row = lax.broadcasted_iota(jnp.int32, dm.shape, 0)
        a = jnp.where(row < 8, dm, jnp.where(row == 8, dmc, 0.0))
        dw_ref[...] = lax.dot_general(sc, a, (((0,), (0,)), ((), ())), precision=hp, preferred_element_type=F32)
        dsc = lax.dot_general(a, w_ref[...], (((1,), (1,)), ((), ())), precision=hp, preferred_element_type=F32)
        dc_ref[...] = dsc * (sg * (1.0 + c * (1.0 - sg)))
        db = da_ref[0:1, :]
        for i in range(1, 16):
            db = db + da_ref[i:i + 1, :]
        db_ref[...] = db

    return pl.pallas_call(
        kern, name="mod_bwd",
        out_shape=[SDS(w_sh.shape, F32), SDS((16, D), F32), SDS((1, dm_all.shape[1]), F32)],
        compiler_params=pltpu.CompilerParams(vmem_limit_bytes=VMEM_LIMIT))(cs, w_sh, dm_sh, dm_all)


M_SHIFT1, M_SCALE1, M_SHIFTC, M_SCALEC, M_GATE1, M_SHIFT2, M_SCALE2, M_GATE2 = range(8)


def _mrow(ref, k):
    return ref[k:k + 1, :]


def inproj_fwd(xa, cos, sin, modrows, w_ext, b_ext, qg, kg, bd, n_lat_tiles):
    n = xa.shape[0]

    def body(t, vals, fr):
        x, cs, sn = (v[...] for v in vals)
        mod, w, b, qg_r, kg_r, bd_r = fr
        is_ctx = t >= n_lat_tiles
        shift = jnp.where(is_ctx, _mrow(mod, M_SHIFTC), _mrow(mod, M_SHIFT1))
        scale = jnp.where(is_ctx, _mrow(mod, M_SCALEC), _mrow(mod, M_SCALE1))
        hb = (x * (1.0 + scale) + shift).astype(MXU_DTYPE)
        proj = jnp.dot(hb, w[...], preferred_element_type=F32) + b[...]
        cos4 = jnp.concatenate([cs] * 4, axis=1)
        sin4 = jnp.concatenate([sn] * 4, axis=1)
        qa = _rope(proj[:, X_QA:X_QA + Q_W], cos4, sin4) * Q_SCALE
        ka = _rope(proj[:, X_KA:X_KA + Q_W], cos4, sin4)
        va = proj[:, X_VA:X_VA + Q_W]
        tq = proj[:, X_QB:X_QB + Q_W]
        rq = lax.rsqrt(_seg_sum64(tq * tq, bd_r) * (1.0 / HEAD_DIM) + QK_EPS)
        qb = _rope(tq * rq * qg_r[...], cos4, sin4) * Q_SCALE
        tk = proj[:, X_KB:X_KB + Q_W]
        rk = lax.rsqrt(_seg_sum64(tk * tk, bd_r) * (1.0 / HEAD_DIM) + QK_EPS)
        kb = _rope(tk * rk * kg_r[...], cos4, sin4)
        vb = proj[:, X_VB:X_VB + Q_W]
        gl = proj[:, X_GL:]
        return [hb, qa, ka, va, qb, kb, vb, tq, rq, tk, rk, gl], []

    mx = MXU_DTYPE
    outs = [((n, D), mx, _rows(TM, D))] + [((n, Q_W), mx, _rows(TM, Q_W))] * 6 + \
           [((n, Q_W), F32, _rows(TM, Q_W))] * 4 + [((n, 2 * D), F32, _rows(TM, 2 * D))]
    res, _ = rowwise("inproj_fwd", body, ntiles=n // TM,
                     tiled=[(xa, _rows(TM, D)), (cos, _rows(TM, LANES)), (sin, _rows(TM, LANES))],
                     full=[modrows, w_ext, b_ext, qg, kg, bd], outs=outs)
    return res


def merge_fwd(oa, ob, gl, x, modrows, wba, wbb, w_out, s_rows):
    def body(t, vals, fr):
        oa_, ob_, gl_, x_ = (v[...] for v in vals)
        mod, wa, wb, wo = fr
        ya = _dot(oa_, wa[...])
        yb = _dot(ob_, wb[...])
        ga = jax.nn.sigmoid(gl_[:, :D])
        gb = jax.nn.sigmoid(gl_[:, D:])
        mrg = ga * ya + gb * yb
        y = _dot(mrg, wo[...])
        r1 = ALPHA * x_ + _mrow(mod, M_GATE1) * y
        mu = jnp.mean(r1, axis=1, keepdims=True)
        xc = r1 - mu
        var = jnp.mean(xc * xc, axis=1, keepdims=True)
        rstd = lax.rsqrt(var + LN_EPS)
        xhat = xc * rstd
        return [ya, yb, mrg, y, xhat, rstd], []

    outs = [((s_rows, D), F32, _rows(TM, D))] * 2 + [((s_rows, D), MXU_DTYPE, _rows(TM, D))] + \
           [((s_rows, D), F32, _rows(TM, D))] * 2 + [((s_rows, 1), F32, _rows(TM, 1))]
    res, _ = rowwise("merge_fwd", body, ntiles=s_rows // TM,
                     tiled=[(oa, _rows(TM, Q_W)), (ob, _rows(TM, Q_W)), (gl, _rows(TM, 2 * D)), (x, _rows(TM, D))],
                     full=[modrows, wba, wbb, w_out], outs=outs)
    return res


def ffn_up_fwd(xhat1, modrows, ln_g, ln_b, w_up, s_rows):
    def body(t, vals, fr):
        xh = vals[0][...]
        mod, g_r, b_r, w = fr
        x1 = xh * g_r[...] + b_r[...]
        h2 = (x1 * (1.0 + _mrow(mod, M_SCALE2)) + _mrow(mod, M_SHIFT2)).astype(MXU_DTYPE)
        return [h2, jnp.dot(h2, w[...], preferred_element_type=F32)], []

    res, _ = rowwise("ffn_up_fwd", body, ntiles=s_rows // TM, tiled=[(xhat1, _rows(TM, D))],
                     full=[modrows, ln_g, ln_b, w_up],
                     outs=[((s_rows, D), MXU_DTYPE, _rows(TM, D)), ((s_rows, 2 * D_FF), F32, _rows(TM, 2 * D_FF))])
    return res


TC = 128


def _halo_specs(tm, w, s_rows):
    per = tm // 8
    last = s_rows // 8 - 1
    return (pl.BlockSpec((8, w), lambda i: (jnp.maximum(i * per - 1, 0), 0)),
            pl.BlockSpec((8, w), lambda i: (jnp.minimum((i + 1) * per, last), 0)))


def _halo_rows(t, ntiles, prev_ref, next_ref):
    prev_row = jnp.where(t == 0, 0.0, prev_ref[7:8, :].astype(F32))
    next_row = jnp.where(t == ntiles - 1, 0.0, next_ref[0:1, :].astype(F32))
    return prev_row, next_row


def conv_swiglu_fwd(u0, conv_w8, conv_b, s_rows):
    w2 = 2 * D_FF
    nt = s_rows // TC

    def body(t, vals, fr):
        u_ref, pv, nx = vals
        cw, cb = fr
        u = u_ref[...]
        up, dn = _shifted_rows(u, *_halo_rows(t, nt, pv, nx))
        uc = cw[0:1, :] * up + cw[1:2, :] * u + cw[2:3, :] * dn + cb[...]
        gate, val = uc[:, :D_FF], uc[:, D_FF:]
        return [gate * jax.nn.sigmoid(gate) * val], []

    hp, hn = _halo_specs(TC, w2, s_rows)
    res, _ = rowwise("conv_swiglu_fwd", body, ntiles=nt,
                     tiled=[(u0, _rows(TC, w2)), (u0, hp), (u0, hn)], full=[conv_w8, conv_b],
                     outs=[((s_rows, D_FF), MXU_DTYPE, _rows(TC, D_FF))])
    return res[0]


def ffn_down_loss(a, xhat1, target, modrows, ln1_g, ln1_b, ln2_g, ln2_b, w_down, s_rows):
    def body(t, vals, fr):
        a_, xh1, tgt = (v[...] for v in vals)
        mod, g1, b1, g2, b2, wd = fr
        y2 = jnp.dot(a_, wd[...], preferred_element_type=F32)
        x1 = xh1 * g1[...] + b1[...]
        gate2 = _mrow(mod, M_GATE2)
        r2 = ALPHA * x1 + gate2 * y2
        mu = jnp.mean(r2, axis=1, keepdims=True)
        xc = r2 - mu
        var = jnp.mean(xc * xc, axis=1, keepdims=True)
        rstd = lax.rsqrt(var + LN_EPS)
        xhat = xc * rstd
        out = xhat * g2[...] + b2[...]
        diff = out - tgt
        loss = 0.5 * jnp.sum(jnp.mean(diff * diff, axis=1, keepdims=True), axis=0, keepdims=True)
        dout = diff * (1.0 / D)
        dr2 = _layer_norm_bwd(dout * g2[...], xhat, rstd)
        incs = [loss, _colsum(dout * xhat), _colsum(dout), _colsum(dr2 * y2)]
        return [dr2, dr2 * gate2], incs

    res, accs = rowwise("ffn_down_loss", body, ntiles=s_rows // TM,
                        tiled=[(a, _rows(TM, D_FF)), (xhat1, _rows(TM, D)), (target, _rows(TM, D))],
                        full=[modrows, ln1_g, ln1_b, ln2_g, ln2_b, w_down],
                        outs=[((s_rows, D), F32, _rows(TM, D)), ((s_rows, D), MXU_DTYPE, _rows(TM, D))],
                        accs=[(1, 1), (1, D), (1, D), (1, D)])
    return res, accs


def ffn_down_bwd(dy2, w_down_t, s_rows):
    def body(t, vals, fr):
        return [jnp.dot(vals[0][...], fr[0][...], preferred_element_type=F32)], []

    res, _ = rowwise("ffn_down_bwd", body, ntiles=s_rows // TM, tiled=[(dy2, _rows(TM, D))], full=[w_down_t],
                     outs=[((s_rows, D_FF), F32, _rows(TM, D_FF))])
    return res[0]


def swiglu_conv_bwd(u0, da, conv_w8, conv_b, s_rows):
    w2 = 2 * D_FF
    nt = s_rows // TC

    def body(t, vals, fr):
        u_ref, pv, nx, da_ref = vals
        cw, cb = fr
        u, da_ = u_ref[...], da_ref[...]
        up, dn = _shifted_rows(u, *_halo_rows(t, nt, pv, nx))
        uc = cw[0:1, :] * up + cw[1:2, :] * u + cw[2:3, :] * dn + cb[...]
        gate, val = uc[:, :D_FF], uc[:, D_FF:]
        sg = jax.nn.sigmoid(gate)
        dgate = da_ * val * (sg * (1.0 + gate * (1.0 - sg)))
        dval = da_ * (gate * sg)
        du = jnp.concatenate([dgate, dval], axis=1)
        return [du], [_colsum(du), _colsum(up * du), _colsum(u * du), _colsum(dn * du)]

    hp, hn = _halo_specs(TC, w2, s_rows)
    res, accs = rowwise("swiglu_conv_bwd", body, ntiles=nt,
                        tiled=[(u0, _rows(TC, w2)), (u0, hp), (u0, hn), (da, _rows(TC, D_FF))],
                        full=[conv_w8, conv_b], outs=[((s_rows, w2), F32, _rows(TC, w2))], accs=[(1, w2)] * 4)
    return res[0], accs


def conv_bwd_input(du, conv_w8, s_rows):
    w2 = 2 * D_FF
    nt = s_rows // TC

    def body(t, vals, fr):
        d_ref, pv, nx = vals
        (cw,) = fr
        d = d_ref[...]
        up, dn = _shifted_rows(d, *_halo_rows(t, nt, pv, nx))
        return [cw[0:1, :] * dn + cw[1:2, :] * d + cw[2:3, :] * up], []

    hp, hn = _halo_specs(TC, w2, s_rows)
    res, _ = rowwise("conv_bwd_input", body, ntiles=nt, tiled=[(du, _rows(TC, w2)), (du, hp), (du, hn)],
                     full=[conv_w8], outs=[((s_rows, w2), MXU_DTYPE, _rows(TC, w2))])
    return res[0]


def ffn_up_ln1_bwd(du0, dr2, xhat1, y, rstd1, modrows, ln_g, ln_b, w_up_t, s_rows):
    def body(t, vals, fr):
        du0_, dr2_, xh, y_, rstd = (v[...] for v in vals)
        mod, g_r, b_r, wt = fr
        dh2 = jnp.dot(du0_, wt[...], preferred_element_type=F32)
        x1 = xh * g_r[...] + b_r[...]
        dx1 = ALPHA * dr2_ + dh2 * (1.0 + _mrow(mod, M_SCALE2))
        dr1 = _layer_norm_bwd(dx1 * g_r[...], xh, rstd)
        incs = [_colsum(dh2 * x1), _colsum(dh2), _colsum(dx1 * xh), _colsum(dx1), _colsum(dr1 * y_)]
        return [dr1 * _mrow(mod, M_GATE1), ALPHA * dr1], incs

    res, accs = rowwise("ffn_up_ln1_bwd", body, ntiles=s_rows // TM,
                        tiled=[(du0, _rows(TM, 2 * D_FF)), (dr2, _rows(TM, D)), (xhat1, _rows(TM, D)), (y, _rows(TM, D)),
                               (rstd1, _rows(TM, 1))],
                        full=[modrows, ln_g, ln_b, w_up_t],
                        outs=[((s_rows, D), MXU_DTYPE, _rows(TM, D)), ((s_rows, D), F32, _rows(TM, D))],
                        accs=[(1, D)] * 5)
    return res, accs


def merge_bwd(dy, ya, yb, gl, w_out_t, wba_t, wbb_t, s_rows):
    def body(t, vals, fr):
        dy_, ya_, yb_, gl_ = (v[...] for v in vals)
        wot, wat, wbt = fr
        dmrg = jnp.dot(dy_, wot[...], preferred_element_type=F32)
        ga = jax.nn.sigmoid(gl_[:, :D])
        gb = jax.nn.sigmoid(gl_[:, D:])
        dya = dmrg * ga
        dyb = dmrg * gb
        dgl = jnp.concatenate([dmrg * ya_ * ga * (1.0 - ga), dmrg * yb_ * gb * (1.0 - gb)], axis=1)
        return [dya, dyb, dgl, _dot(dya, wat[...]), _dot(dyb, wbt[...])], []

    mx = MXU_DTYPE
    res, _ = rowwise("merge_bwd", body, ntiles=s_rows // TM,
                     tiled=[(dy, _rows(TM, D)), (ya, _rows(TM, D)), (yb, _rows(TM, D)), (gl, _rows(TM, 2 * D))],
                     full=[w_out_t, wba_t, wbb_t],
                     outs=[((s_rows, D), mx, _rows(TM, D))] * 2 + [((s_rows, 2 * D), F32, _rows(TM, 2 * D))] +
                          [((s_rows, Q_W), F32, _rows(TM, Q_W))] * 2)
    return res


def qk_bwd(dqa, dka, dkax, dva, dvax, dqb, dkb, dvb, dgl, tq, rq, tk, rk, cos, sin, qg, kg, bd, n_lat_tiles, n):
    def body(t, vals, fr):
        dqa_, dka_, dkax_, dva_, dvax_, dqb_, dkb_, dvb_, dgl_, tq_, rq_, tk_, rk_, cs, sn = (v[...] for v in vals)
        qg_r, kg_r, bd_r = fr
        is_ctx = t >= n_lat_tiles
        cos4 = jnp.concatenate([cs] * 4, axis=1)
        sin4 = jnp.concatenate([sn] * 4, axis=1)
        zero = jnp.zeros_like(dqa_)
        dpqa = jnp.where(is_ctx, zero, _rope_t(dqa_, cos4, sin4) * Q_SCALE)
        dpka = _rope_t(jnp.where(is_ctx, dkax_, dka_), cos4, sin4)
        dpva = jnp.where(is_ctx, dvax_, dva_)
        dnq = jnp.where(is_ctx, zero, _rope_t(dqb_, cos4, sin4) * Q_SCALE)
        gq = qg_r[...] * dnq
        dtq = rq_ * gq - tq_ * (rq_ * rq_ * rq_) * (_seg_sum64(gq * tq_, bd_r) * (1.0 / HEAD_DIM))
        dnk = _rope_t(dkb_, cos4, sin4)
        gk = kg_r[...] * dnk
        dtk = rk_ * gk - tk_ * (rk_ * rk_ * rk_) * (_seg_sum64(gk * tk_, bd_r) * (1.0 / HEAD_DIM))
        dgl32 = jnp.where(is_ctx, jnp.zeros_like(dgl_), dgl_)
        dproj = jnp.concatenate([dpqa, dpka, dpva, dtq, dtk, dvb_, dgl32], axis=1)
        return [dproj], [_colsum(dproj), _colsum(dnq * tq_ * rq_), _colsum(dnk * tk_ * rk_)]

    lat = lambda t: jnp.minimum(t, n_lat_tiles - 1)
    cx = lambda t: jnp.maximum(t - n_lat_tiles, 0)
    qs = _rows(TM, Q_W)
    res, accs = rowwise(
        "qk_bwd", body, ntiles=n // TM,
        tiled=[(dqa, _rows(TM, Q_W, lat)), (dka, _rows(TM, Q_W, lat)), (dkax, _rows(TM, Q_W, cx)),
               (dva, _rows(TM, Q_W, lat)), (dvax, _rows(TM, Q_W, cx)), (dqb, _rows(TM, Q_W, lat)),
               (dkb, qs), (dvb, qs), (dgl, _rows(TM, 2 * D, lat)), (tq, qs), (rq, qs), (tk, qs), (rk, qs),
               (cos, _rows(TM, LANES)), (sin, _rows(TM, LANES))],
        full=[qg, kg, bd], outs=[((n, EXT_COLS), MXU_DTYPE, _rows(TM, EXT_COLS))],
        accs=[(1, EXT_COLS), (1, Q_W), (1, Q_W)])
    return res[0], accs


def inproj_bwd(name, dproj, xa, dxp, modrows, w_ext_t, *, ntiles, tile_off, is_ctx, out_rows):
    kc = M_SCALEC if is_ctx else M_SCALE1

    def body(t, vals, fr):
        dp, x_ = vals[0][...], vals[1][...]
        mod, wt = fr
        dh = jnp.dot(dp, wt[...], preferred_element_type=F32)
        incs = [_colsum(dh * x_), _colsum(dh)]
        if is_ctx:
            return [], incs
        return [vals[2][...] + dh * (1.0 + _mrow(mod, kc))], incs

    tiled = [(dproj, _rows(TM, EXT_COLS, lambda i: i + tile_off)), (xa, _rows(TM, D, lambda i: i + tile_off))]
    outs = []
    if not is_ctx:
        tiled.append((dxp, _rows(TM, D)))
        outs = [((out_rows, D), F32, _rows(TM, D))]
    return rowwise(name, body, ntiles=ntiles, tiled=tiled, full=[modrows, w_ext_t], outs=outs, accs=[(1, D)] * 2)


def _attn_semantics():
    return _cparams(("arbitrary", "arbitrary", "arbitrary"))


GLOB_TK = (1280, 1024, 768, 512, 256)
KEY_CHUNK = 256


def glob_fwd(q, kt, vt, s_rows):
    n = kt.shape[0]
    tq = TM
    tk = _pick(n, GLOB_TK)
    nq, nk = s_rows // tq, n // tk
    r = GROUPS * tq
    nch = tk // KEY_CHUNK

    def produce(qs, k_ref, s_buf, c, mx):
        lo = c * KEY_CHUNK
        sn = _dot_nt(qs[...], k_ref[lo:lo + KEY_CHUNK, :])
        s_buf[:, lo:lo + KEY_CHUNK] = sn
        for t in range(KEY_CHUNK // LANES):
            mx = jnp.maximum(mx, sn[:, t * LANES:(t + 1) * LANES])
        return mx

    def kern(q_ref, k0_ref, kn_ref, v_ref, o_ref, lse_ref, qs, s_buf, mx_buf, p_buf, m_s, l_s, acc):
        j = pl.program_id(2)

        @pl.when(j == 0)
        def _():
            qs[...] = _stack_groups(q_ref[...], qs.dtype)
            mx = jnp.full((r, LANES), -jnp.inf, F32)
            for c in range(nch):
                mx = produce(qs, k0_ref, s_buf, c, mx)
            mx_buf[...] = mx
            m_s[...] = jnp.full_like(m_s, -jnp.inf)
            l_s[...] = jnp.zeros_like(l_s)
            acc[...] = jnp.zeros_like(acc)

        m_prev = m_s[...]
        m_new = jnp.maximum(m_prev, jnp.max(mx_buf[...], axis=1, keepdims=True))
        alpha = jnp.exp(m_prev - m_new)
        m_b = jnp.broadcast_to(m_new, (r, LANES))
        mx = jnp.full((r, LANES), -jnp.inf, F32)
        ls = jnp.zeros((r, LANES), F32)
        for c in range(nch):
            for t in range(KEY_CHUNK // LANES):
                lo = c * KEY_CHUNK + t * LANES
                pt = jnp.exp(s_buf[:, lo:lo + LANES] - m_b)
                ls = ls + pt
                p_buf[:, lo:lo + LANES] = pt.astype(p_buf.dtype)
            mx = produce(qs, kn_ref, s_buf, c, mx)
        mx_buf[...] = mx
        l_s[...] = alpha * l_s[...] + jnp.sum(ls, axis=1, keepdims=True)
        acc[...] = alpha * acc[...] + jnp.dot(p_buf[...], v_ref[...], preferred_element_type=F32)
        m_s[...] = m_new

        @pl.when(j == nk - 1)
        def _():
            o_ref[...] = _fold_groups(acc[...] / l_s[...], tq)
            lse_ref[0, 0] = m_s[...] + jnp.log(l_s[...])

    kspec = lambda f: pl.BlockSpec((tk, KV_W), lambda h, i, j: (f(j), h))
    return pl.pallas_call(
        kern, name="glob_fwd", grid=(N_KV, nq, nk),
        in_specs=[pl.BlockSpec((tq, KV_W), lambda h, i, j: (i, h)), kspec(lambda j: 0),
                  kspec(lambda j: jnp.minimum(j + 1, nk - 1)), kspec(lambda j: j)],
        out_specs=[pl.BlockSpec((tq, KV_W), lambda h, i, j: (i, h)),
                   pl.BlockSpec((1, 1, r, 1), lambda h, i, j: (h, i, 0, 0))],
        out_shape=[SDS((s_rows, Q_W), F32), SDS((N_KV, nq, r, 1), F32)],
        scratch_shapes=[pltpu.VMEM((r, KV_W), MXU_DTYPE), pltpu.VMEM((r, tk), F32), pltpu.VMEM((r, LANES), F32),
                        pltpu.VMEM((r, tk), MXU_DTYPE), pltpu.VMEM((r, 1), F32), pltpu.VMEM((r, 1), F32),
                        pltpu.VMEM((r, KV_W), F32)],
        compiler_params=_attn_semantics(),
    )(q, kt, kt, vt)


def attn_delta(o, do, s_rows):
    tq = TM
    nq = s_rows // tq
    r = GROUPS * tq

    def kern(o_ref, do_ref, d_ref):
        d_ref[0, 0] = jnp.sum(_stack_groups(do_ref[...], F32) * _stack_groups(o_ref[...], F32), axis=1, keepdims=True)

    qspec = pl.BlockSpec((tq, KV_W), lambda h, i: (i, h))
    return pl.pallas_call(
        kern, name="attn_delta", grid=(N_KV, nq), in_specs=[qspec, qspec],
        out_specs=pl.BlockSpec((1, 1, r, 1), lambda h, i: (h, i, 0, 0)), out_shape=SDS((N_KV, nq, r, 1), F32),
        compiler_params=_cparams(("parallel", "parallel")),
    )(o, do)


def _stack_groups_t(tt, dtype):
    blk = jnp.right_shift(lax.broadcasted_iota(jnp.int32, tt.shape, 0), 6)
    return jnp.concatenate([jnp.where(blk == g, tt, jnp.zeros_like(tt)).astype(dtype) for g in range(GROUPS)], axis=1)


def glob_bwd(q, q_t, kt, vt, do, do_t, lse, delta, h, s_rows):
    n = kt.shape[0]
    tq = TM
    tk = _pick(n, GLOB_TK)
    nq, nk = s_rows // tq, n // tk
    r = GROUPS * tq
    nch = tk // KEY_CHUNK

    def kern(q_ref, qt_ref, k_ref, v_ref, do_ref, dot_ref, lse_ref, dl_ref, dq_ref, dkt_ref, dvt_ref, p_buf, ds_buf):
        j = pl.program_id(0)
        i = pl.program_id(1)

        @pl.when(i == 0)
        def _():
            dkt_ref[...] = jnp.zeros_like(dkt_ref)
            dvt_ref[...] = jnp.zeros_like(dvt_ref)

        qs = _stack_groups(q_ref[...], MXU_DTYPE)
        dos = _stack_groups(do_ref[...], MXU_DTYPE)
        lse_b = jnp.broadcast_to(lse_ref[0], (r, LANES))
        dl_b = jnp.broadcast_to(dl_ref[0], (r, LANES))
        for c in range(nch):
            lo = c * KEY_CHUNK
            sc = _dot_nt(qs, k_ref[lo:lo + KEY_CHUNK, :])
            dpc = _dot_nt(dos, v_ref[lo:lo + KEY_CHUNK, :])
            for t in range(KEY_CHUNK // LANES):
                sl = slice(t * LANES, (t + 1) * LANES)
                pt = jnp.exp(sc[:, sl] - lse_b)
                p_buf[:, lo + t * LANES:lo + (t + 1) * LANES] = pt.astype(p_buf.dtype)
                ds_buf[:, lo + t * LANES:lo + (t + 1) * LANES] = (pt * (dpc[:, sl] - dl_b)).astype(ds_buf.dtype)
        dq_t = _fold_groups(jnp.dot(ds_buf[...], k_ref[...], preferred_element_type=F32), tq)
        rows = pl.ds(pl.multiple_of(i * tq, tq), tq)

        @pl.when(j == 0)
        def _():
            dq_ref[rows, :] = dq_t

        @pl.when(j > 0)
        def _():
            dq_ref[rows, :] += dq_t

        dvt_ref[...] += jnp.dot(_stack_groups_t(dot_ref[...], MXU_DTYPE), p_buf[...], preferred_element_type=F32)
        dkt_ref[...] += jnp.dot(_stack_groups_t(qt_ref[...], MXU_DTYPE), ds_buf[...], preferred_element_type=F32)

    col = pl.BlockSpec((1, r, 1), lambda j, i: (i, 0, 0))
    qspec = pl.BlockSpec((tq, KV_W), lambda j, i: (i, h))
    tspec = pl.BlockSpec((KV_W, tq), lambda j, i: (h, i))
    kspec = pl.BlockSpec((tk, KV_W), lambda j, i: (j, h))
    ospec = pl.BlockSpec((KV_W, tk), lambda j, i: (0, j))
    return pl.pallas_call(
        kern, name=f"glob_bwd_h{h}", grid=(nk, nq),
        in_specs=[qspec, tspec, kspec, kspec, qspec, tspec, col, col],
        out_specs=[pl.BlockSpec(memory_space=pltpu.VMEM), ospec, ospec],
        out_shape=[SDS((s_rows, KV_W), F32), SDS((KV_W, n), F32), SDS((KV_W, n), F32)],
        scratch_shapes=[pltpu.VMEM((r, tk), MXU_DTYPE), pltpu.VMEM((r, tk), MXU_DTYPE)],
        compiler_params=_cparams(("arbitrary", "arbitrary")),
    )(q, q_t, kt, vt, do, do_t, lse, delta)


WQ = GROUPS * WIN


def _win_kv_specs(s_rows, c_rows):
    nb = s_rows // WIN
    blk = lambda f: pl.BlockSpec((WIN, KV_W), lambda h, i: (f(i), h))
    return [blk(lambda i: jnp.maximum(i - 1, 0)), blk(lambda i: i), blk(lambda i: jnp.minimum(i + 1, nb - 1)),
            pl.BlockSpec((c_rows, KV_W), lambda h, i: (s_rows // c_rows, h))]


def _win_mask(i, s_rows, shape):
    row = lax.broadcasted_iota(jnp.int32, shape, 0)
    col = lax.broadcasted_iota(jnp.int32, shape, 1)
    qpos = i * WIN + (row & (WIN - 1))
    kpos = (i - 1) * WIN + col
    band = (jnp.abs(qpos - kpos) <= WIN) & (kpos >= 0) & (kpos < s_rows)
    return (col >= 3 * WIN) | band


def _win_cat(dst, parts):
    off = 0
    for p in parts:
        dst[off:off + p.shape[0], :] = p[...]
        off += p.shape[0]


def win_fwd(q, kt, vt, sinkcol, s_rows, c_rows):
    nb = s_rows // WIN
    nkeys = 3 * WIN + c_rows

    def kern(q_ref, kp, kc, kn, kx, vp, vc, vn, vx, sink_ref, o_ref, lse_ref, kcat, vcat):
        i = pl.program_id(1)
        _win_cat(kcat, (kp, kc, kn, kx))
        _win_cat(vcat, (vp, vc, vn, vx))
        qs = _stack_groups(q_ref[...], MXU_DTYPE)
        s = _dot_nt(qs, kcat[...])
        s = jnp.where(_win_mask(i, s_rows, s.shape), s, -jnp.inf)
        sink = sink_ref[0][:, 0:1]
        m = jnp.maximum(jnp.max(s, axis=1, keepdims=True), sink)
        e = jnp.exp(s - m)
        den = jnp.sum(e, axis=1, keepdims=True) + jnp.exp(sink - m)
        o_ref[...] = _fold_groups(_dot(e / den, vcat[...]), WIN)
        lse_ref[0, 0] = m + jnp.log(den)

    kv = _win_kv_specs(s_rows, c_rows)
    qspec = pl.BlockSpec((WIN, KV_W), lambda h, i: (i, h))
    col = pl.BlockSpec((1, 1, WQ, 1), lambda h, i: (h, i, 0, 0))
    return pl.pallas_call(
        kern, name="win_fwd", grid=(N_KV, nb),
        in_specs=[qspec] + kv + kv + [pl.BlockSpec((1, WQ, LANES), lambda h, i: (h, 0, 0))],
        out_specs=[qspec, col], out_shape=[SDS((s_rows, Q_W), F32), SDS((N_KV, nb, WQ, 1), F32)],
        scratch_shapes=[pltpu.VMEM((nkeys, KV_W), MXU_DTYPE), pltpu.VMEM((nkeys, KV_W), MXU_DTYPE)],
        compiler_params=_cparams(("arbitrary", "arbitrary")),
    )(q, kt, kt, kt, kt, vt, vt, vt, vt, sinkcol)


def win_bwd_dq(q, kt, vt, sinkcol, o, do, lse, s_rows, c_rows):
    nb = s_rows // WIN
    nkeys = 3 * WIN + c_rows

    def kern(q_ref, kp, kc, kn, kx, vp, vc, vn, vx, sink_ref, o_ref, do_ref, lse_ref,
             dq_ref, delta_ref, dkx_ref, dvx_ref, dsk_ref, kcat, vcat):
        i = pl.program_id(1)

        @pl.when(i == 0)
        def _():
            dkx_ref[...] = jnp.zeros_like(dkx_ref)
            dvx_ref[...] = jnp.zeros_like(dvx_ref)
            dsk_ref[...] = jnp.zeros_like(dsk_ref)

        _win_cat(kcat, (kp, kc, kn, kx))
        _win_cat(vcat, (vp, vc, vn, vx))
        qs = _stack_groups(q_ref[...], MXU_DTYPE)
        do32 = _stack_groups(do_ref[...], F32)
        delta = jnp.sum(do32 * _stack_groups(o_ref[...], F32), axis=1, keepdims=True)
        dos = do32.astype(MXU_DTYPE)
        lse_c = lse_ref[0, 0]
        s = _dot_nt(qs, kcat[...])
        s = jnp.where(_win_mask(i, s_rows, s.shape), s, -jnp.inf)
        p = jnp.exp(s - lse_c)
        ds = p * (_dot_nt(dos, vcat[...]) - delta)
        dq_ref[...] = _fold_groups(_dot(ds, kcat[...]), WIN)
        delta_ref[0, 0] = delta
        dkx_ref[0] += _dot_tn(ds[:, 3 * WIN:], qs)
        dvx_ref[0] += _dot_tn(p[:, 3 * WIN:], dos)
        dsk_ref[0] += -(jnp.exp(sink_ref[0][:, 0:1] - lse_c) * delta)

    kv = _win_kv_specs(s_rows, c_rows)
    qspec = pl.BlockSpec((WIN, KV_W), lambda h, i: (i, h))
    col = pl.BlockSpec((1, 1, WQ, 1), lambda h, i: (h, i, 0, 0))
    xspec = pl.BlockSpec((1, c_rows, KV_W), lambda h, i: (h, 0, 0))
    return pl.pallas_call(
        kern, name="win_bwd_dq", grid=(N_KV, nb),
        in_specs=[qspec] + kv + kv + [pl.BlockSpec((1, WQ, LANES), lambda h, i: (h, 0, 0)), qspec, qspec, col],
        out_specs=[qspec, col, xspec, xspec, pl.BlockSpec((1, WQ, 1), lambda h, i: (h, 0, 0))],
        out_shape=[SDS((s_rows, Q_W), F32), SDS((N_KV, nb, WQ, 1), F32), SDS((N_KV, c_rows, KV_W), F32),
                   SDS((N_KV, c_rows, KV_W), F32), SDS((N_KV, WQ, 1), F32)],
        scratch_shapes=[pltpu.VMEM((nkeys, KV_W), MXU_DTYPE), pltpu.VMEM((nkeys, KV_W), MXU_DTYPE)],
        compiler_params=_cparams(("arbitrary", "arbitrary")),
    )(q, kt, kt, kt, kt, vt, vt, vt, vt, sinkcol, o, do, lse)


def win_bwd_dkv(q, kt, vt, do, lse, delta, s_rows):
    nb = s_rows // WIN

    def kern(k_ref, v_ref, qp, qc, qn, dop, doc, don, lp, lc, ln, dp_, dc_, dn_, dk_ref, dv_ref):
        j = pl.program_id(1)
        k = k_ref[...]
        v = v_ref[...]
        dk = jnp.zeros((WIN, KV_W), F32)
        dv = jnp.zeros((WIN, KV_W), F32)
        for b, (q_r, do_r, l_r, d_r) in enumerate(((qp, dop, lp, dp_), (qc, doc, lc, dc_), (qn, don, ln, dn_))):
            ib = j - 1 + b
            qs = _stack_groups(q_r[...], MXU_DTYPE)
            dos = _stack_groups(do_r[...], MXU_DTYPE)
            s = _dot_nt(qs, k)
            row = lax.broadcasted_iota(jnp.int32, s.shape, 0)
            col = lax.broadcasted_iota(jnp.int32, s.shape, 1)
            near = jnp.abs(ib * WIN + (row & (WIN - 1)) - (j * WIN + col)) <= WIN
            ok = near & (ib >= 0) & (ib < nb)
            p = jnp.where(ok, jnp.exp(s - l_r[0, 0]), 0.0)
            dv = dv + _dot_tn(p, dos)
            ds = p * (_dot_nt(dos, v) - d_r[0, 0])
            dk = dk + _dot_tn(ds, qs)
        dk_ref[...] = dk
        dv_ref[...] = dv

    fs = (lambda j: jnp.maximum(j - 1, 0), lambda j: j, lambda j: jnp.minimum(j + 1, nb - 1))
    qspecs = [pl.BlockSpec((WIN, KV_W), lambda h, j, f=f: (f(j), h)) for f in fs]
    cols = [pl.BlockSpec((1, 1, WQ, 1), lambda h, j, f=f: (h, f(j), 0, 0)) for f in fs]
    kspec = pl.BlockSpec((WIN, KV_W), lambda h, j: (j, h))
    return pl.pallas_call(
        kern, name="win_bwd_dkv", grid=(N_KV, nb),
        in_specs=[kspec, kspec] + qspecs + qspecs + cols + cols, out_specs=[kspec, kspec],
        out_shape=[SDS((s_rows, Q_W), F32), SDS((s_rows, Q_W), F32)],
        compiler_params=_cparams(("arbitrary", "arbitrary")),
    )(kt, vt, q, q, q, do, do, do, lse, lse, lse, delta, delta, delta)


def adamw(name, w, m, v, grads):
    r, wd = w.shape
    tr = _pick(r, [t for t in (1408, 1024, 512, 256, 128, 64, 32, 16, 8) if t * wd * 4 <= ELEMENTWISE_BLOCK_BYTES])
    stacked = not isinstance(grads, (list, tuple))
    ng = grads.shape[0] if stacked else len(grads)

    def kern(*refs):
        w_ref, m_ref, v_ref = refs[:3]
        g_refs = refs[3:-4]
        g_out, d_out, m_out, v_out = refs[-4:]
        if stacked:
            g = g_refs[0][0]
            for k in range(1, ng):
                g = g + g_refs[0][k]
        else:
            g = g_refs[0][...]
            for gr in g_refs[1:]:
                g = g + gr[...]
        wv = w_ref[...]
        mn = ADAM_B1 * m_ref[...] + (1.0 - ADAM_B1) * g
        vn = ADAM_B2 * v_ref[...] + (1.0 - ADAM_B2) * (g * g)
        m_hat = mn / (1.0 - ADAM_B1 ** ADAM_STEP)
        v_hat = vn / (1.0 - ADAM_B2 ** ADAM_STEP)
        g_out[...] = g
        d_out[...] = -ADAM_LR * (m_hat / (jnp.sqrt(v_hat) + ADAM_EPS) + ADAM_WD * wv)
        m_out[...] = mn
        v_out[...] = vn

    spec = pl.BlockSpec((tr, wd), lambda i: (i, 0))
    gspecs = [pl.BlockSpec((ng, tr, wd), lambda i: (0, i, 0))] if stacked else [spec] * ng
    return pl.pallas_call(
        kern, name=name, grid=(r // tr,), in_specs=[spec] * 3 + gspecs, out_specs=[spec] * 4,
        out_shape=[SDS((r, wd), F32)] * 4, compiler_params=_cparams(("parallel",)),
    )(w, m, v, *([grads] if stacked else grads))


def add2(name, a, b):
    k, r, w = a.shape
    tr = _pick(r, (1408, 1024, 512, 256, 128, 64, 32, 16, 8))

    def kern(a_ref, b_ref, o_ref):
        o_ref[...] = a_ref[...] + b_ref[...]

    spec = pl.BlockSpec((1, tr, w), lambda s, i: (s, i, 0))
    return pl.pallas_call(kern, name=name, grid=(k, r // tr), in_specs=[spec, spec], out_specs=spec,
                          out_shape=SDS(a.shape, a.dtype), compiler_params=_cparams(("parallel", "parallel")))(a, b)


def _rep4(a, off):
    return jnp.concatenate([a[:, off + HEAD_DIM * h: off + HEAD_DIM * (h + 1)] for h in range(N_KV) for _ in range(GROUPS)], axis=1)


def _extend_cols(a):
    return jnp.concatenate([a[:, 0:OFF_KA], _rep4(a, OFF_KA), _rep4(a, OFF_VA), a[:, OFF_QB:OFF_KB],
                            _rep4(a, OFF_KB), _rep4(a, OFF_VB), a[:, OFF_GA:]], axis=1)


def _fold4(a, off):
    r = a.shape[0]
    return a[:, off:off + Q_W].reshape(r, N_KV, GROUPS, HEAD_DIM).sum(axis=2).reshape(r, N_KV * HEAD_DIM)


def _fold_cols(a):
    return jnp.concatenate([a[:, X_QA:X_QA + Q_W], _fold4(a, X_KA), _fold4(a, X_VA), a[:, X_QB:X_QB + Q_W],
                            _fold4(a, X_KB), _fold4(a, X_VB), a[:, X_GL:]], axis=1)


def _rope_tables(s_rows, c_rows):
    pos = jnp.arange(s_rows, dtype=jnp.int32)
    rows = (pos // GRID_W).astype(F32)
    cols = (pos % GRID_W).astype(F32)
    n_freq = HEAD_DIM // 4
    inv_freq = ROPE_THETA ** (-jnp.arange(n_freq, dtype=F32) / n_freq)
    ang_r = rows[:, None] * inv_freq
    ang_c = cols[:, None] * inv_freq
    cos = jnp.concatenate([jnp.cos(ang_r)] * 2 + [jnp.cos(ang_c)] * 2, axis=1)
    sin = jnp.concatenate([-jnp.sin(ang_r), jnp.sin(ang_r), -jnp.sin(ang_c), jnp.sin(ang_c)], axis=1)
    cos = jnp.concatenate([cos, jnp.ones((c_rows, HEAD_DIM), F32)], axis=0)
    sin = jnp.concatenate([sin, jnp.zeros((c_rows, HEAD_DIM), F32)], axis=0)
    return jnp.concatenate([cos, cos], axis=1), jnp.concatenate([sin, sin], axis=1)


BIG = (("w_in", (D, IN_COLS // N_DEV)), ("w_branch_a", (Q_W, D // N_DEV)), ("w_branch_b", (Q_W, D // N_DEV)),
       ("w_out", (D // N_DEV, D)), ("w_up", (D, 2 * D_FF // N_DEV)), ("w_down", (D_FF // N_DEV, D)))
BIG_SIZES = tuple(int(np.prod(s)) for _, s in BIG)
BIG_ROWS = sum(BIG_SIZES) // LANES


def _pack_big(parts):
    return jnp.concatenate([p.reshape(-1) for p in parts]).reshape(BIG_ROWS, LANES)


def _unpack_big(flat):
    lead = flat.shape[:-2]
    f = flat.reshape(*lead, BIG_ROWS * LANES)
    out, off = [], 0
    for (_, shp), sz in zip(BIG, BIG_SIZES, strict=True):
        out.append(f[..., off:off + sz].reshape(*lead, *shp))
        off += sz
    return out


def _cols_to_full(g):
    return jnp.transpose(g, (1, 0, 2)).reshape(g.shape[1], -1)


def _full_to_cols(a):
    r, c = a.shape
    return jnp.transpose(a.reshape(r, N_DEV, c // N_DEV), (1, 0, 2))


SMALL = (("c_ctx", D), ("b_mod", N_MOD * D), ("b_in", IN_COLS), ("attn_sink", N_HEADS), ("q_norm_g", HEAD_DIM),
         ("k_norm_g", HEAD_DIM), ("ln1_g", D), ("ln1_b", D), ("conv_w", 3 * 2 * D_FF // N_DEV), ("conv_b", 2 * D_FF),
         ("ln2_g", D), ("ln2_b", D))
SMALL_TOTAL = sum(n for _, n in SMALL)
SMALL_ROWS = -(-SMALL_TOTAL // (8 * LANES)) * 8


def _pack_small(parts):
    flat = jnp.concatenate([p.reshape(-1).astype(F32) for p in parts])
    return jnp.pad(flat, (0, SMALL_ROWS * LANES - flat.shape[0])).reshape(SMALL_ROWS, LANES)


def _unpack_small(packed):
    f = packed.reshape(-1)
    out, off = {}, 0
    for name, n in SMALL:
        out[name] = f[off:off + n]
        off += n
    return out


RED = (("c_ctx", D), ("b_in", IN_COLS), ("attn_sink", N_HEADS), ("q_norm_g", HEAD_DIM), ("k_norm_g", HEAD_DIM),
       ("ln1_g", D), ("ln1_b", D), ("conv_w", 3 * 2 * D_FF), ("conv_b", 2 * D_FF), ("ln2_g", D), ("ln2_b", D))
RED_TOTAL = sum(n for _, n in RED)
RED_ROWS = -(-RED_TOTAL // (8 * LANES)) * 8


def sum8(name, g):
    _, r, w = g.shape

    def kern(g_ref, o_ref):
        acc = g_ref[0]
        for k in range(1, N_DEV):
            acc = acc + g_ref[k]
        o_ref[...] = acc

    return pl.pallas_call(kern, name=name, out_shape=SDS((r, w), F32))(g)


def _local_step(x, ctx, target, modrows, weights, small):
    s_rows, c_rows = x.shape[0], ctx.shape[0]
    n = s_rows + c_rows
    nl = s_rows // TM
    w_in, wba, wbb, w_out, w_up, w_down = weights
    f = lambda a: a.reshape(1, -1).astype(F32)
    b_in, ln1_g, ln1_b, ln2_g, ln2_b, conv_b = (f(small[k]) for k in ("b_in", "ln1_g", "ln1_b", "ln2_g", "ln2_b", "conv_b"))
    conv_w8 = jnp.pad(small["conv_w_full"], ((0, 5), (0, 0)))
    qg = jnp.tile(small["q_norm_g"].reshape(1, HEAD_DIM), (1, N_HEADS))
    kg = jnp.tile(small["k_norm_g"].reshape(1, HEAD_DIM), (1, N_HEADS))
    sinkcol = jnp.broadcast_to(jnp.repeat(small["attn_sink"].reshape(N_KV, GROUPS), WIN, axis=1)[:, :, None], (N_KV, WQ, LANES))
    bd = jnp.kron(jnp.eye(N_HEADS, dtype=F32), jnp.ones((HEAD_DIM, HEAD_DIM), F32)).astype(BF16)
    cos, sin = _rope_tables(s_rows, c_rows)
    w_ext = _extend_cols(w_in)
    b_ext = _extend_cols(b_in)
    xa = jnp.concatenate([x, ctx], axis=0)

    hb, qa, kat, vat, qb, kbt, vbt, tq, rq, tk, rk, gl = inproj_fwd(xa, cos, sin, modrows, w_ext, b_ext, qg, kg, bd, nl)
    oa, lse_a = win_fwd(qa, kat, vat, sinkcol, s_rows, c_rows)
    ob, lse_b = glob_fwd(qb, kbt, vbt, s_rows)
    ya, yb, mrg, y, xhat1, rstd1 = merge_fwd(oa, ob, gl, x, modrows, wba, wbb, w_out, s_rows)
    h2, u0 = ffn_up_fwd(xhat1, modrows, ln1_g, ln1_b, w_up, s_rows)
    a = conv_swiglu_fwd(u0, conv_w8, conv_b, s_rows)
    (dr2, dy2), (loss, dln2_g, dln2_b, dgate2) = ffn_down_loss(a, xhat1, target, modrows, ln1_g, ln1_b, ln2_g, ln2_b, w_down, s_rows)

    da = ffn_down_bwd(dy2, w_down.T, s_rows)
    dw_down = mm_tn("dw_down", a, dy2, s_rows)
    du, (dconv_b, dcw0, dcw1, dcw2) = swiglu_conv_bwd(u0, da, conv_w8, conv_b, s_rows)
    du0 = conv_bwd_input(du, conv_w8, s_rows)
    dw_up = mm_tn("dw_up", h2, du0, s_rows)
    (dy, dxp), (dscale2, dshift2, dln1_g, dln1_b, dgate1) = ffn_up_ln1_bwd(du0, dr2, xhat1, y, rstd1, modrows, ln1_g, ln1_b, w_up.T, s_rows)
    dya, dyb, dgl, doa, dob = merge_bwd(dy, ya, yb, gl, w_out.T, wba.T, wbb.T, s_rows)
    dw_out = mm_tn("dw_out", mrg, dy, s_rows)
    dwba = mm_tn("dw_branch_a", oa, dya, s_rows)
    dwbb = mm_tn("dw_branch_b", ob, dyb, s_rows)

    dqa, delta_a, dkax, dvax, dsk = win_bwd_dq(qa, kat, vat, sinkcol, oa, doa, lse_a, s_rows, c_rows)
    dka, dva = win_bwd_dkv(qa, kat, vat, doa, lse_a, delta_a, s_rows)
    delta_b = attn_delta(ob, dob, s_rows)
    qb_t = qb[:s_rows].T
    dob_t = dob.astype(MXU_DTYPE).T
    heads = [glob_bwd(qb, qb_t, kbt, vbt, dob, dob_t, lse_b[h], delta_b[h], h, s_rows) for h in range(N_KV)]
    dqb = jnp.concatenate([hd[0] for hd in heads], axis=1)
    dkb = jnp.concatenate([hd[1] for hd in heads], axis=0).T
    dvb = jnp.concatenate([hd[2] for hd in heads], axis=0).T
    ctx_cols = lambda t: jnp.transpose(t, (1, 0, 2)).reshape(c_rows, Q_W)
    dproj, (db_ext, dqg, dkg) = qk_bwd(dqa, dka, ctx_cols(dkax), dva, ctx_cols(dvax), dqb, dkb, dvb, dgl, tq, rq, tk, rk,
                                       cos, sin, qg, kg, bd, nl, n)
    w_ext_t = w_ext.T
    (grad_x,), (dscale1, dshift1) = inproj_bwd("inproj_bwd", dproj, xa, dxp, modrows, w_ext_t, ntiles=nl, tile_off=0,
                                               is_ctx=False, out_rows=s_rows)
    _, (dscale_c, dshift_c) = inproj_bwd("inproj_bwd_ctx", dproj, xa, None, modrows, w_ext_t, ntiles=c_rows // TM,
                                         tile_off=nl, is_ctx=True, out_rows=0)
    dw_in = _fold_cols(mm_tn("dw_in", hb, dproj, n))

    dmod = jnp.concatenate([dshift1, dscale1, dgate1, dshift2, dscale2, dgate2], axis=1)
    dmod_c = jnp.concatenate([dshift_c, dscale_c, jnp.zeros((1, (N_MOD - 2) * D), F32)], axis=1)
    fold_g = lambda t: t.reshape(N_HEADS, HEAD_DIM).sum(axis=0)
    red = {
        "b_in": _fold_cols(db_ext), "attn_sink": dsk.reshape(N_HEADS, WIN).sum(axis=1), "q_norm_g": fold_g(dqg),
        "k_norm_g": fold_g(dkg), "ln1_g": dln1_g, "ln1_b": dln1_b, "conv_w": jnp.concatenate([dcw0, dcw1, dcw2], axis=0),
        "conv_b": dconv_b, "ln2_g": dln2_g, "ln2_b": dln2_b,
    }
    return loss[0, 0], grad_x, (dw_in, dwba, dwbb, dw_out, dw_up, dw_down), dmod, dmod_c, red


def kernel(x, c, ctx, c_ctx, w_mod, b_mod, w_in, b_in, attn_sink, q_norm_g, k_norm_g, w_branch_a, w_branch_b, w_out, ln1_g, ln1_b, w_up, conv_w, conv_b, w_down, ln2_g, ln2_b, loss_target, m_c_ctx, m_w_mod, m_b_mod, m_w_in, m_b_in, m_attn_sink, m_q_norm_g, m_k_norm_g, m_w_branch_a, m_w_branch_b, m_w_out, m_ln1_g, m_ln1_b, m_w_up, m_conv_w, m_conv_b, m_w_down, m_ln2_g, m_ln2_b, v_c_ctx, v_w_mod, v_b_mod, v_w_in, v_b_in, v_attn_sink, v_q_norm_g, v_k_norm_g, v_w_branch_a, v_w_branch_b, v_w_out, v_ln1_g, v_ln1_b, v_w_up, v_conv_w, v_conv_b, v_w_down, v_ln2_g, v_ln2_b):
    ax, ay, ac = (lax.axis_index(a) for a in AXES)
    me = 4 * ax + 2 * ay + ac
    chip = 2 * ax + ay
    mod_w = N_MOD * D // N_DEV
    params = dict(c_ctx=c_ctx, w_mod=w_mod, b_mod=b_mod, w_in=w_in, b_in=b_in, attn_sink=attn_sink, q_norm_g=q_norm_g,
                  k_norm_g=k_norm_g, w_branch_a=w_branch_a, w_branch_b=w_branch_b, w_out=w_out, ln1_g=ln1_g, ln1_b=ln1_b,
                  w_up=w_up, conv_w=conv_w, conv_b=conv_b, w_down=w_down, ln2_g=ln2_g, ln2_b=ln2_b)
    mom_m = dict(c_ctx=m_c_ctx, w_mod=m_w_mod, b_mod=m_b_mod, w_in=m_w_in, b_in=m_b_in, attn_sink=m_attn_sink,
                 q_norm_g=m_q_norm_g, k_norm_g=m_k_norm_g, w_branch_a=m_w_branch_a, w_branch_b=m_w_branch_b, w_out=m_w_out,
                 ln1_g=m_ln1_g, ln1_b=m_ln1_b, w_up=m_w_up, conv_w=m_conv_w, conv_b=m_conv_b, w_down=m_w_down,
                 ln2_g=m_ln2_g, ln2_b=m_ln2_b)
    mom_v = dict(c_ctx=v_c_ctx, w_mod=v_w_mod, b_mod=v_b_mod, w_in=v_w_in, b_in=v_b_in, attn_sink=v_attn_sink,
                 q_norm_g=v_q_norm_g, k_norm_g=v_k_norm_g, w_branch_a=v_w_branch_a, w_branch_b=v_w_branch_b, w_out=v_w_out,
                 ln1_g=v_ln1_g, ln1_b=v_ln1_b, w_up=v_w_up, conv_w=v_conv_w, conv_b=v_conv_b, w_down=v_w_down,
                 ln2_g=v_ln2_g, ln2_b=v_ln2_b)
    big_names = [nm for nm, _ in BIG]

    wg = all_gather("ag_weights", _pack_big([params[nm][0].astype(MXU_DTYPE) for nm in big_names]))
    g_in, g_ba, g_bb, g_out, g_up, g_down = _unpack_big(wg)
    weights = (_cols_to_full(g_in), _cols_to_full(g_ba), _cols_to_full(g_bb), g_out.reshape(D, D), _cols_to_full(g_up),
               g_down.reshape(D_FF, D))

    c_all = all_gather("ag_c", c.reshape(8, LANES)).reshape(N_DEV, D)
    cs = jnp.concatenate([c_all, c_ctx.reshape(1, D), jnp.zeros((7, D), F32)], axis=0)
    w_mod_sh = w_mod[0]
    b_mod_sh = lax.dynamic_slice(b_mod, (0, me * mod_w), (1, mod_w))
    mod_part = mod_fwd(cs, w_mod_sh, b_mod_sh)
    mg = all_gather("ag_mod", mod_part.reshape(16 * mod_w // LANES, LANES)).reshape(N_DEV, 16, mod_w)
    mod = lax.dynamic_index_in_dim(mg, me, axis=1, keepdims=False).reshape(N_MOD, D)
    mod_c = mg[:, 8, :].reshape(N_MOD, D)
    modrows = jnp.stack([mod[0], mod[1], mod_c[0], mod_c[1], mod[2], mod[3], mod[4], mod[5]], axis=0)

    conv_w_full = all_gather("ag_conv_w", jnp.pad(conv_w[0], ((0, 5), (0, LANES * 6 - 2 * D_FF // N_DEV))))
    conv_w_full = _cols_to_full(conv_w_full[:, :3, :2 * D_FF // N_DEV])
    small = dict(b_in=b_in, ln1_g=ln1_g, ln1_b=ln1_b, ln2_g=ln2_g, ln2_b=ln2_b, conv_b=conv_b, conv_w_full=conv_w_full,
                 q_norm_g=q_norm_g, k_norm_g=k_norm_g, attn_sink=attn_sink)
    loss, grad_x, big_grads, dmod, dmod_c, red = _local_step(x[0], ctx[0], loss_target[0], modrows, weights, small)
    loss = lax.psum(loss, AXES)

    dm = all_gather("ag_dmod", jnp.concatenate([dmod, dmod_c], axis=0).reshape(2 * N_MOD * D // LANES, LANES))
    dm = dm.reshape(N_DEV, 2, N_MOD * D)
    dm_all = jnp.concatenate([dm[:, 0], dm[:, 1]], axis=0)
    dm_sh = lax.dynamic_slice(dm_all, (0, me * mod_w), (16, mod_w))
    dw_mod, dcc, db_mod = mod_bwd(cs, w_mod_sh, dm_sh, dm_all)
    red["c_ctx"] = dcc[8]

    red_vec = jnp.concatenate([red[nm].reshape(-1) for nm, _ in RED])
    red_vec = jnp.pad(red_vec, (0, RED_ROWS * LANES - RED_TOTAL)).reshape(RED_ROWS, LANES)
    red_sum = sum8("sum_small", all_gather("ag_small", red_vec)).reshape(-1)
    gsm, off = {}, 0
    for nm, k in RED:
        gsm[nm] = red_sum[off:off + k]
        off += k
    gsm["b_mod"] = db_mod.reshape(-1)
    cw_sh = 2 * D_FF // N_DEV
    gsm["conv_w"] = lax.dynamic_slice(gsm["conv_w"].reshape(3, 2 * D_FF), (0, me * cw_sh), (3, cw_sh))
    sm_names = [nm for nm, _ in SMALL]
    gs, ds, ms, vs = adamw("adamw_small", _pack_small([params[nm] for nm in sm_names]),
                           _pack_small([mom_m[nm] for nm in sm_names]), _pack_small([mom_v[nm] for nm in sm_names]),
                           [_pack_small([gsm[nm] for nm in sm_names])])
    sm_out = [_unpack_small(t) for t in (gs, ds, ms, vs)]

    dw_in, dwba, dwbb, dw_out, dw_up, dw_down = big_grads
    slabs = jnp.concatenate([t.reshape(N_DEV, -1) for t in (
        _full_to_cols(dw_in), _full_to_cols(dwba), _full_to_cols(dwbb), dw_out, _full_to_cols(dw_up), dw_down)], axis=1)
    by_core = slabs.reshape(4, 2, BIG_ROWS, LANES)
    keep = lax.dynamic_index_in_dim(by_core, ac, axis=1, keepdims=False)
    give = lax.dynamic_index_in_dim(by_core, 1 - ac, axis=1, keepdims=False)
    got = exchange("rs_sibling", give.reshape(1, 4 * BIG_ROWS, LANES), to_chips=False).reshape(4, BIG_ROWS, LANES)
    pair = add2("rs_pair_sum", keep, got)
    outbox = jnp.stack([lax.dynamic_index_in_dim(pair, jnp.bitwise_xor(chip, m), axis=0, keepdims=False) for m in (1, 2, 3)])
    inbox = exchange("rs_chips", outbox, to_chips=True)
    mine = lax.dynamic_index_in_dim(pair, chip, axis=0, keepdims=False)
    gb, db, mb, vb = adamw("adamw_big", _pack_big([params[nm][0] for nm in big_names]),
                           _pack_big([mom_m[nm][0] for nm in big_names]), _pack_big([mom_v[nm][0] for nm in big_names]),
                           [mine, inbox[0], inbox[1], inbox[2]])
    big_out = [dict(zip(big_names, _unpack_big(t), strict=True)) for t in (gb, db, mb, vb)]
    gm, dmo, mmo, vmo = adamw("adamw_mod", w_mod[0], m_w_mod[0], v_w_mod[0], [dw_mod])
    mod_out = (gm, dmo, mmo, vmo)

    order = ["c_ctx", "w_mod", "b_mod", "w_in", "b_in", "attn_sink", "q_norm_g", "k_norm_g", "w_branch_a", "w_branch_b",
             "w_out", "ln1_g", "ln1_b", "w_up", "conv_w", "conv_b", "w_down", "ln2_g", "ln2_b"]
    results = [loss, grad_x[None]]
    for kind in range(4):
        for nm in order:
            if nm == "w_mod":
                val = mod_out[kind]
            elif nm in big_out[kind]:
                val = big_out[kind][nm]
            else:
                val = sm_out[kind][nm]
            results.append(val.reshape(params[nm].shape))
    return tuple(results)
```

```python
import functools

import jax
import jax.numpy as jnp
import numpy as np
from jax import lax
from jax.experimental import pallas as pl
from jax.experimental.pallas import tpu as pltpu

F32 = jnp.float32
BF16 = jnp.bfloat16
MXU_DTYPE = BF16

AXES = ("x", "y", "c")
N_DEV = 8
D = 1024
HEAD_DIM = 64
N_HEADS = 8
N_KV = 2
GROUPS = 4
KV_W = GROUPS * HEAD_DIM
Q_W = N_HEADS * HEAD_DIM
GRID_W = 64
WIN = 128
ROPE_THETA = 10000.0
D_FF = 2816
FF_SHARD = 2 * D_FF // N_DEV
FF_SHARD_PAD = 768
FF = N_DEV // 2 * FF_SHARD_PAD
LN_EPS = 1e-5
QK_EPS = 1e-6
N_MOD = 6
ALPHA = 2.0 ** 0.25
Q_SCALE = HEAD_DIM ** -0.5
IN_COLS = 3584
OFF_KA, OFF_VA, OFF_QB, OFF_KB, OFF_VB, OFF_GA = 512, 640, 768, 1280, 1408, 1536
EXT_COLS = 6 * Q_W + 2 * D
X_QA, X_KA, X_VA, X_QB, X_KB, X_VB, X_GL = 0, 512, 1024, 1536, 2048, 2560, 3072
ADAM_LR, ADAM_B1, ADAM_B2, ADAM_EPS, ADAM_WD, ADAM_STEP = 0.001, 0.9, 0.999, 1e-08, 0.01, 10
LANES = 128
TM = 256
VMEM_LIMIT = 56 * 1024 * 1024
ELEMENTWISE_BLOCK_BYTES = 1 << 20
ELEMENTWISE_ROWS = (1824, 1408, 1024, 512, 256, 128, 64, 32, 16, 8)

ANY = pl.BlockSpec(memory_space=pl.ANY)
SDS = jax.ShapeDtypeStruct


def _pick(n, candidates):
    for t in candidates:
        if n % t == 0:
            return t
    raise ValueError(f"no tile for {n}")


def _full(a):
    nd = a.ndim
    return pl.BlockSpec(a.shape, lambda *_: (0,) * nd)


def _rows(tm, w, fn=lambda t: t):
    return pl.BlockSpec((tm, w), lambda i: (fn(i), 0))


def _dot(a, b):
    return jnp.dot(a.astype(MXU_DTYPE), b.astype(MXU_DTYPE), preferred_element_type=F32)


def _dot_nt(a, b):
    return lax.dot_general(a.astype(MXU_DTYPE), b.astype(MXU_DTYPE), (((1,), (1,)), ((), ())), preferred_element_type=F32)


def _dot_tn(a, b):
    return lax.dot_general(a.astype(MXU_DTYPE), b.astype(MXU_DTYPE), (((0,), (0,)), ((), ())), preferred_element_type=F32)


def _cparams(sem):
    return pltpu.CompilerParams(dimension_semantics=sem, vmem_limit_bytes=VMEM_LIMIT)


def all_gather(name, v):
    r, w = v.shape

    def body(x_ref, out_ref, send_sems, recv_sems, local_sem):
        x, y, c = (lax.axis_index(a) for a in AXES)
        me, sibling = (x, y, c), (x, y, 1 - c)
        chips = [(1 - x, y), (x, 1 - y), (1 - x, 1 - y)]

        def rows(px, py, pc):
            return out_ref.at[4 * px + 2 * py + pc]

        def copy(k, block, to, src=None):
            return pltpu.make_async_remote_copy(
                src_ref=rows(*block) if src is None else src, dst_ref=rows(*block),
                send_sem=send_sems.at[k], recv_sem=recv_sems.at[k],
                device_id=to, device_id_type=pl.DeviceIdType.MESH)

        mine = pltpu.make_async_copy(x_ref, rows(*me), local_sem)
        mine.start()
        first = [copy(0, me, sibling, src=x_ref)]
        first += [copy(1 + j, me, (*chip, c), src=x_ref) for j, chip in enumerate(chips)]
        for cp in first:
            cp.start()
        passed = [copy(4 + j, (*chip, c), sibling) for j, chip in enumerate(chips)]
        for j, chip in enumerate(chips):
            copy(1 + j, (*chip, c), me).wait_recv()
            passed[j].start()
        copy(0, sibling, me).wait_recv()
        for j, chip in enumerate(chips):
            copy(4 + j, (*chip, 1 - c), me).wait_recv()
        for cp in first + passed:
            cp.wait_send()
        mine.wait()

    return pl.pallas_call(
        body, name=name, out_shape=SDS((N_DEV, r, w), v.dtype), in_specs=[ANY], out_specs=ANY,
        scratch_shapes=[pltpu.SemaphoreType.DMA((7,)), pltpu.SemaphoreType.DMA((7,)), pltpu.SemaphoreType.DMA],
    )(v)


def exchange(name, outbox, to_chips):
    k = outbox.shape[0]
    assert k == (3 if to_chips else 1)

    def body(out_ref, in_ref, send_sems, recv_sems):
        x, y, c = (lax.axis_index(a) for a in AXES)
        peers = [(x, 1 - y, c), (1 - x, y, c), (1 - x, 1 - y, c)] if to_chips else [(x, y, 1 - c)]
        copies = [
            pltpu.make_async_remote_copy(
                src_ref=out_ref.at[m], dst_ref=in_ref.at[m], send_sem=send_sems.at[m], recv_sem=recv_sems.at[m],
                device_id=peer, device_id_type=pl.DeviceIdType.MESH)
            for m, peer in enumerate(peers)
        ]
        for cp in copies:
            cp.start()
        for cp in copies:
            cp.wait_recv()
        for cp in copies:
            cp.wait_send()

    return pl.pallas_call(
        body, name=name, out_shape=SDS(outbox.shape, outbox.dtype), in_specs=[ANY], out_specs=ANY,
        scratch_shapes=[pltpu.SemaphoreType.DMA((k,)), pltpu.SemaphoreType.DMA((k,))],
    )(outbox)


def rowwise(name, body, *, ntiles, tile_off=0, tiled, full, outs, accs=()):
    nt, nf, no = len(tiled), len(full), len(outs)

    def kern(*refs):
        i = pl.program_id(0)
        out_vals, incs = body(i + tile_off, refs[:nt], refs[nt:nt + nf])
        for r, v in zip(refs[nt + nf:nt + nf + no], out_vals, strict=True):
            r[...] = v.astype(r.dtype)
        acc_refs = refs[nt + nf + no:]

        @pl.when(i == 0)
        def _():
            for r in acc_refs:
                r[...] = jnp.zeros_like(r)

        for r, v in zip(acc_refs, incs, strict=True):
            r[...] += v

    res = pl.pallas_call(
        kern, name=name, grid=(ntiles,),
        in_specs=[s for _, s in tiled] + [_full(a) for a in full],
        out_specs=[s for _, _, s in outs] + [pl.BlockSpec(s, lambda i, n=len(s): (0,) * n) for s in accs],
        out_shape=[SDS(s, d) for s, d, _ in outs] + [SDS(s, F32) for s in accs],
        compiler_params=_cparams(("arbitrary",) if accs else ("parallel",)),
    )(*[a for a, _ in tiled], *full)
    return res[:no], res[no:]


def mm_tn(name, a, b, rows):
    ka, nb = a.shape[1], b.shape[1]
    tr = _pick(rows, (1280, 1024, 768, 512, 256))
    tn = _pick(nb, (512, 256, 128))

    def kern(a_ref, b_ref, o_ref):
        @pl.when(pl.program_id(1) == 0)
        def _():
            o_ref[...] = jnp.zeros_like(o_ref)

        o_ref[...] += _dot_tn(a_ref[...], b_ref[...])

    return pl.pallas_call(
        kern, name=name, grid=(nb // tn, rows // tr),
        in_specs=[pl.BlockSpec((tr, ka), lambda n, r: (r, 0)), pl.BlockSpec((tr, tn), lambda n, r: (r, n))],
        out_specs=pl.BlockSpec((ka, tn), lambda n, r: (0, n)), out_shape=SDS((ka, nb), F32),
        compiler_params=_cparams(("parallel", "arbitrary")),
    )(a, b)


def _swap16(t):
    w = t.shape[1]
    lane = lax.broadcasted_iota(jnp.int32, t.shape, 1)
    return jnp.where((lane & 16) == 0, pltpu.roll(t, w - 16, 1), pltpu.roll(t, 16, 1))


def _rope(t, cos, sin):
    return t * cos + _swap16(t) * sin


def _rope_t(d, cos, sin):
    return d * cos - _swap16(d) * sin


def _seg_sum64(a, bd_ref):
    bd = bd_ref[...]
    hi = a.astype(BF16)
    lo = (a - hi.astype(F32)).astype(BF16)
    return jnp.dot(hi, bd, preferred_element_type=F32) + jnp.dot(lo, bd, preferred_element_type=F32)


def _lane_block(shape):
    return jnp.right_shift(lax.broadcasted_iota(jnp.int32, shape, 1), 6)


def _stack_groups(t, dtype):
    blk = _lane_block(t.shape)
    return jnp.concatenate([jnp.where(blk == g, t, jnp.zeros_like(t)).astype(dtype) for g in range(GROUPS)], axis=0)


def _fold_groups(ts, tq):
    blk = _lane_block((tq, KV_W))
    out = jnp.zeros((tq, KV_W), ts.dtype)
    for g in range(GROUPS):
        out = jnp.where(blk == g, ts[g * tq:(g + 1) * tq], out)
    return out


def _layer_norm_bwd(dxh, xhat, rstd):
    m1 = jnp.mean(dxh, axis=1, keepdims=True)
    m2 = jnp.mean(dxh * xhat, axis=1, keepdims=True)
    return rstd * (dxh - m1 - xhat * m2)


def _colsum(a):
    return jnp.sum(a, axis=0, keepdims=True)


def _shifted_rows(t, prev_row, next_row):
    n = t.shape[0]
    row = lax.broadcasted_iota(jnp.int32, t.shape, 0)
    up = jnp.where(row == 0, prev_row, pltpu.roll(t, 1, 0))
    dn = jnp.where(row == n - 1, next_row, pltpu.roll(t, n - 1, 0))
    return up, dn


def mod_fwd(cs, w_sh, b_sh):
    def kern(c_ref, w_ref, b_ref, o_ref):
        o_ref[...] = _dot(jax.nn.silu(c_ref[...]), w_ref[...]) + b_ref[...]

    return pl.pallas_call(kern, name="mod_fwd", out_shape=SDS((16, w_sh.shape[1]), F32),
                          compiler_params=pltpu.CompilerParams(vmem_limit_bytes=VMEM_LIMIT))(cs, w_sh, b_sh)


def mod_bwd(cs, w_sh, dm_sh, dm_all):
    hp = lax.Precision.HIGHEST

    def kern(c_ref, w_ref, dm_ref, da_ref, dw_ref, dc_ref, db_ref):
        c = c_ref[...]
        sg = jax.nn.sigmoid(c)
        sc = c * sg
        dm = dm_ref[...]
        dmc = dm_ref[8:9, :]
        for i in range(9, 16):
            dmc = dmc + dm_ref[i:i + 1, :]
        row = lax.broadcasted_iota(jnp.int32, dm.shape, 0)
        a = jnp.where(row < 8, dm, jnp.where(row == 8, dmc, 0.0))
        dw_ref[...] = lax.dot_general(sc, a, (((0,), (0,)), ((), ())), precision=hp, preferred_element_type=F32)
        dsc = lax.dot_general(a, w_ref[...], (((1,), (1,)), ((), ())), precision=hp, preferred_element_type=F32)
        dc_ref[...] = dsc * (sg * (1.0 + c * (1.0 - sg)))
        db = da_ref[0:1, :]
        for i in range(1, 16):
            db = db + da_ref[i:i + 1, :]
        db_ref[...] = db

    return pl.pallas_call(
        kern, name="mod_bwd",
        out_shape=[SDS(w_sh.shape, F32), SDS((16, D), F32), SDS((1, dm_all.shape[1]), F32)],
        compiler_params=pltpu.CompilerParams(vmem_limit_bytes=VMEM_LIMIT))(cs, w_sh, dm_sh, dm_all)


M_SHIFT1, M_SCALE1, M_SHIFTC, M_SCALEC, M_GATE1, M_SHIFT2, M_SCALE2, M_GATE2 = range(8)


def _mrow(ref, k):
    return ref[k:k + 1, :]


def inproj_fwd(xa, cos, sin, modrows, w_ext, b_ext, qg, kg, bd, n_lat_tiles):
    n = xa.shape[0]

    def body(t, vals, fr):
        x, cs, sn = (v[...] for v in vals)
        mod, w, b, qg_r, kg_r, bd_r = fr
        is_ctx = t >= n_lat_tiles
        shift = jnp.where(is_ctx, _mrow(mod, M_SHIFTC), _mrow(mod, M_SHIFT1))
        scale = jnp.where(is_ctx, _mrow(mod, M_SCALEC), _mrow(mod, M_SCALE1))
        hb = (x * (1.0 + scale) + shift).astype(MXU_DTYPE)
        proj = jnp.dot(hb, w[...], preferred_element_type=F32) + b[...]
        cos4 = jnp.concatenate([cs] * 4, axis=1)
        sin4 = jnp.concatenate([sn] * 4, axis=1)
        qa = _rope(proj[:, X_QA:X_QA + Q_W], cos4, sin4) * Q_SCALE
        ka = _rope(proj[:, X_KA:X_KA + Q_W], cos4, sin4)
        va = proj[:, X_VA:X_VA + Q_W]
        tq = proj[:, X_QB:X_QB + Q_W]
        rq = lax.rsqrt(_seg_sum64(tq * tq, bd_r) * (1.0 / HEAD_DIM) + QK_EPS)
        qb = _rope(tq * rq * qg_r[...], cos4, sin4) * Q_SCALE
        tk = proj[:, X_KB:X_KB + Q_W]
        rk = lax.rsqrt(_seg_sum64(tk * tk, bd_r) * (1.0 / HEAD_DIM) + QK_EPS)
        kb = _rope(tk * rk * kg_r[...], cos4, sin4)
        vb = proj[:, X_VB:X_VB + Q_W]
        gl = proj[:, X_GL:]
        return [hb, qa, ka, va, qb, kb, vb, tq, rq, tk, rk, gl], []

    mx = MXU_DTYPE
    outs = [((n, D), mx, _rows(TM, D))] + [((n, Q_W), mx, _rows(TM, Q_W))] * 6 + \
           [((n, Q_W), F32, _rows(TM, Q_W))] * 4 + [((n, 2 * D), F32, _rows(TM, 2 * D))]
    res, _ = rowwise("inproj_fwd", body, ntiles=n // TM,
                     tiled=[(xa, _rows(TM, D)), (cos, _rows(TM, LANES)), (sin, _rows(TM, LANES))],
                     full=[modrows, w_ext, b_ext, qg, kg, bd], outs=outs)
    return res


def merge_fwd(oa, ob, gl, x, modrows, wba, wbb, w_out, s_rows):
    def body(t, vals, fr):
        oa_, ob_, gl_, x_ = (v[...] for v in vals)
        mod, wa, wb, wo = fr
        ya = _dot(oa_, wa[...])
        yb = _dot(ob_, wb[...])
        ga = jax.nn.sigmoid(gl_[:, :D])
        gb = jax.nn.sigmoid(gl_[:, D:])
        mrg = ga * ya + gb * yb
        y = _dot(mrg, wo[...])
        r1 = ALPHA * x_ + _mrow(mod, M_GATE1) * y
        mu = jnp.mean(r1, axis=1, keepdims=True)
        xc = r1 - mu
        var = jnp.mean(xc * xc, axis=1, keepdims=True)
        rstd = lax.rsqrt(var + LN_EPS)
        xhat = xc * rstd
        return [ya, yb, mrg, y, xhat, rstd], []

    outs = [((s_rows, D), F32, _rows(TM, D))] * 2 + [((s_rows, D), MXU_DTYPE, _rows(TM, D))] + \
           [((s_rows, D), F32, _rows(TM, D))] * 2 + [((s_rows, 1), F32, _rows(TM, 1))]
    res, _ = rowwise("merge_fwd", body, ntiles=s_rows // TM,
                     tiled=[(oa, _rows(TM, Q_W)), (ob, _rows(TM, Q_W)), (gl, _rows(TM, 2 * D)), (x, _rows(TM, D))],
                     full=[modrows, wba, wbb, w_out], outs=outs)
    return res


def ffn_up_fwd(xhat1, modrows, ln_g, ln_b, w_up, s_rows):
    def body(t, vals, fr):
        xh = vals[0][...]
        mod, g_r, b_r, w = fr
        x1 = xh * g_r[...] + b_r[...]
        h2 = (x1 * (1.0 + _mrow(mod, M_SCALE2)) + _mrow(mod, M_SHIFT2)).astype(MXU_DTYPE)
        return [h2, jnp.dot(h2, w[...], preferred_element_type=F32)], []

    res, _ = rowwise("ffn_up_fwd", body, ntiles=s_rows // TM, tiled=[(xhat1, _rows(TM, D))],
                     full=[modrows, ln_g, ln_b, w_up],
                     outs=[((s_rows, D), MXU_DTYPE, _rows(TM, D)), ((s_rows, 2 * FF), F32, _rows(TM, 2 * FF))])
    return res


TC = 128


def _halo_specs(tm, w, s_rows):
    per = tm // 8
    last = s_rows // 8 - 1
    return (pl.BlockSpec((8, w), lambda i: (jnp.maximum(i * per - 1, 0), 0)),
            pl.BlockSpec((8, w), lambda i: (jnp.minimum((i + 1) * per, last), 0)))


def _halo_rows(t, ntiles, prev_ref, next_ref):
    prev_row = jnp.where(t == 0, 0.0, prev_ref[7:8, :].astype(F32))
    next_row = jnp.where(t == ntiles - 1, 0.0, next_ref[0:1, :].astype(F32))
    return prev_row, next_row


def conv_swiglu_fwd(u0, conv_w8, conv_b, s_rows):
    w2 = 2 * FF
    nt = s_rows // TC

    def body(t, vals, fr):
        u_ref, pv, nx = vals
        cw, cb = fr
        u = u_ref[...]
        up, dn = _shifted_rows(u, *_halo_rows(t, nt, pv, nx))
        uc = cw[0:1, :] * up + cw[1:2, :] * u + cw[2:3, :] * dn + cb[...]
        gate, val = uc[:, :FF], uc[:, FF:]
        return [gate * jax.nn.sigmoid(gate) * val], []

    hp, hn = _halo_specs(TC, w2, s_rows)
    res, _ = rowwise("conv_swiglu_fwd", body, ntiles=nt,
                     tiled=[(u0, _rows(TC, w2)), (u0, hp), (u0, hn)], full=[conv_w8, conv_b],
                     outs=[((s_rows, FF), MXU_DTYPE, _rows(TC, FF))])
    return res[0]


def ffn_down_loss(a, xhat1, target, modrows, ln1_g, ln1_b, ln2_g, ln2_b, w_down, s_rows):
    def body(t, vals, fr):
        a_, xh1, tgt = (v[...] for v in vals)
        mod, g1, b1, g2, b2, wd = fr
        y2 = jnp.dot(a_, wd[...], preferred_element_type=F32)
        x1 = xh1 * g1[...] + b1[...]
        gate2 = _mrow(mod, M_GATE2)
        r2 = ALPHA * x1 + gate2 * y2
        mu = jnp.mean(r2, axis=1, keepdims=True)
        xc = r2 - mu
        var = jnp.mean(xc * xc, axis=1, keepdims=True)
        rstd = lax.rsqrt(var + LN_EPS)
        xhat = xc * rstd
        out = xhat * g2[...] + b2[...]
        diff = out - tgt
        loss = 0.5 * jnp.sum(jnp.mean(diff * diff, axis=1, keepdims=True), axis=0, keepdims=True)
        dout = diff * (1.0 / D)
        dr2 = _layer_norm_bwd(dout * g2[...], xhat, rstd)
        incs = [loss, _colsum(dout * xhat), _colsum(dout), _colsum(dr2 * y2)]
        return [dr2, dr2 * gate2], incs

    res, accs = rowwise("ffn_down_loss", body, ntiles=s_rows // TM,
                        tiled=[(a, _rows(TM, FF)), (xhat1, _rows(TM, D)), (target, _rows(TM, D))],
                        full=[modrows, ln1_g, ln1_b, ln2_g, ln2_b, w_down],
                        outs=[((s_rows, D), F32, _rows(TM, D)), ((s_rows, D), MXU_DTYPE, _rows(TM, D))],
                        accs=[(1, 1), (1, D), (1, D), (1, D)])
    return res, accs


def ffn_down_bwd(dy2, w_down_t, s_rows):
    def body(t, vals, fr):
        return [jnp.dot(vals[0][...], fr[0][...], preferred_element_type=F32)], []

    res, _ = rowwise("ffn_down_bwd", body, ntiles=s_rows // TM, tiled=[(dy2, _rows(TM, D))], full=[w_down_t],
                     outs=[((s_rows, FF), F32, _rows(TM, FF))])
    return res[0]


def swiglu_conv_bwd(u0, da, conv_w8, conv_b, s_rows):
    w2 = 2 * FF
    nt = s_rows // TC

    def body(t, vals, fr):
        u_ref, pv, nx, da_ref = vals
        cw, cb = fr
        u, da_ = u_ref[...], da_ref[...]
        up, dn = _shifted_rows(u, *_halo_rows(t, nt, pv, nx))
        uc = cw[0:1, :] * up + cw[1:2, :] * u + cw[2:3, :] * dn + cb[...]
        gate, val = uc[:, :FF], uc[:, FF:]
        sg = jax.nn.sigmoid(gate)
        dgate = da_ * val * (sg * (1.0 + gate * (1.0 - sg)))
        dval = da_ * (gate * sg)
        du = jnp.concatenate([dgate, dval], axis=1)
        return [du], [_colsum(du), _colsum(up * du), _colsum(u * du), _colsum(dn * du)]

    hp, hn = _halo_specs(TC, w2, s_rows)
    res, accs = rowwise("swiglu_conv_bwd", body, ntiles=nt,
                        tiled=[(u0, _rows(TC, w2)), (u0, hp), (u0, hn), (da, _rows(TC, FF))],
                        full=[conv_w8, conv_b], outs=[((s_rows, w2), F32, _rows(TC, w2))], accs=[(1, w2)] * 4)
    return res[0], accs


def conv_bwd_input(du, conv_w8, s_rows):
    w2 = 2 * FF
    nt = s_rows // TC

    def body(t, vals, fr):
        d_ref, pv, nx = vals
        (cw,) = fr
        d = d_ref[...]
        up, dn = _shifted_rows(d, *_halo_rows(t, nt, pv, nx))
        return [cw[0:1, :] * dn + cw[1:2, :] * d + cw[2:3, :] * up], []

    hp, hn = _halo_specs(TC, w2, s_rows)
    res, _ = rowwise("conv_bwd_input", body, ntiles=nt, tiled=[(du, _rows(TC, w2)), (du, hp), (du, hn)],
                     full=[conv_w8], outs=[((s_rows, w2), MXU_DTYPE, _rows(TC, w2))])
    return res[0]


def ffn_up_ln1_bwd(du0, dr2, xhat1, y, rstd1, modrows, ln_g, ln_b, w_up_t, s_rows):
    def body(t, vals, fr):
        du0_, dr2_, xh, y_, rstd = (v[...] for v in vals)
        mod, g_r, b_r, wt = fr
        dh2 = jnp.dot(du0_, wt[...], preferred_element_type=F32)
        x1 = xh * g_r[...] + b_r[...]
        dx1 = ALPHA * dr2_ + dh2 * (1.0 + _mrow(mod, M_SCALE2))
        dr1 = _layer_norm_bwd(dx1 * g_r[...], xh, rstd)
        incs = [_colsum(dh2 * x1), _colsum(dh2), _colsum(dx1 * xh), _colsum(dx1), _colsum(dr1 * y_)]
        return [dr1 * _mrow(mod, M_GATE1), ALPHA * dr1], incs

    res, accs = rowwise("ffn_up_ln1_bwd", body, ntiles=s_rows // TM,
                        tiled=[(du0, _rows(TM, 2 * FF)), (dr2, _rows(TM, D)), (xhat1, _rows(TM, D)), (y, _rows(TM, D)),
                               (rstd1, _rows(TM, 1))],
                        full=[modrows, ln_g, ln_b, w_up_t],
                        outs=[((s_rows, D), MXU_DTYPE, _rows(TM, D)), ((s_rows, D), F32, _rows(TM, D))],
                        accs=[(1, D)] * 5)
    return res, accs


def merge_bwd(dy, ya, yb, gl, w_out_t, wba_t, wbb_t, s_rows):
    def body(t, vals, fr):
        dy_, ya_, yb_, gl_ = (v[...] for v in vals)
        wot, wat, wbt = fr
        dmrg = jnp.dot(dy_, wot[...], preferred_element_type=F32)
        ga = jax.nn.sigmoid(gl_[:, :D])
        gb = jax.nn.sigmoid(gl_[:, D:])
        dya = dmrg * ga
        dyb = dmrg * gb
        dgl = jnp.concatenate([dmrg * ya_ * ga * (1.0 - ga), dmrg * yb_ * gb * (1.0 - gb)], axis=1)
        return [dya, dyb, dgl, _dot(dya, wat[...]), _dot(dyb, wbt[...])], []

    mx = MXU_DTYPE
    res, _ = rowwise("merge_bwd", body, ntiles=s_rows // TM,
                     tiled=[(dy, _rows(TM, D)), (ya, _rows(TM, D)), (yb, _rows(TM, D)), (gl, _rows(TM, 2 * D))],
                     full=[w_out_t, wba_t, wbb_t],
                     outs=[((s_rows, D), mx, _rows(TM, D))] * 2 + [((s_rows, 2 * D), F32, _rows(TM, 2 * D))] +
                          [((s_rows, Q_W), F32, _rows(TM, Q_W))] * 2)
    return res


def qk_bwd(dqa, dka, dkax, dva, dvax, dqb, dkb, dvb, dgl, tq, rq, tk, rk, cos, sin, qg, kg, bd, n_lat_tiles, n):
    def body(t, vals, fr):
        dqa_, dka_, dkax_, dva_, dvax_, dqb_, dkb_, dvb_, dgl_, tq_, rq_, tk_, rk_, cs, sn = (v[...] for v in vals)
        qg_r, kg_r, bd_r = fr
        is_ctx = t >= n_lat_tiles
        cos4 = jnp.concatenate([cs] * 4, axis=1)
        sin4 = jnp.concatenate([sn] * 4, axis=1)
        zero = jnp.zeros_like(dqa_)
        dpqa = jnp.where(is_ctx, zero, _rope_t(dqa_, cos4, sin4) * Q_SCALE)
        dpka = _rope_t(jnp.where(is_ctx, dkax_, dka_), cos4, sin4)
        dpva = jnp.where(is_ctx, dvax_, dva_)
        dnq = jnp.where(is_ctx, zero, _rope_t(dqb_, cos4, sin4) * Q_SCALE)
        gq = qg_r[...] * dnq
        dtq = rq_ * gq - tq_ * (rq_ * rq_ * rq_) * (_seg_sum64(gq * tq_, bd_r) * (1.0 / HEAD_DIM))
        dnk = _rope_t(dkb_, cos4, sin4)
        gk = kg_r[...] * dnk
        dtk = rk_ * gk - tk_ * (rk_ * rk_ * rk_) * (_seg_sum64(gk * tk_, bd_r) * (1.0 / HEAD_DIM))
        dgl32 = jnp.where(is_ctx, jnp.zeros_like(dgl_), dgl_)
        dproj = jnp.concatenate([dpqa, dpka, dpva, dtq, dtk, dvb_, dgl32], axis=1)
        return [dproj], [_colsum(dproj), _colsum(dnq * tq_ * rq_), _colsum(dnk * tk_ * rk_)]

    lat = lambda t: jnp.minimum(t, n_lat_tiles - 1)
    cx = lambda t: jnp.maximum(t - n_lat_tiles, 0)
    qs = _rows(TM, Q_W)
    res, accs = rowwise(
        "qk_bwd", body, ntiles=n // TM,
        tiled=[(dqa, _rows(TM, Q_W, lat)), (dka, _rows(TM, Q_W, lat)), (dkax, _rows(TM, Q_W, cx)),
               (dva, _rows(TM, Q_W, lat)), (dvax, _rows(TM, Q_W, cx)), (dqb, _rows(TM, Q_W, lat)),
               (dkb, qs), (dvb, qs), (dgl, _rows(TM, 2 * D, lat)), (tq, qs), (rq, qs), (tk, qs), (rk, qs),
               (cos, _rows(TM, LANES)), (sin, _rows(TM, LANES))],
        full=[qg, kg, bd], outs=[((n, EXT_COLS), MXU_DTYPE, _rows(TM, EXT_COLS))],
        accs=[(1, EXT_COLS), (1, Q_W), (1, Q_W)])
    return res[0], accs


def inproj_bwd(name, dproj, xa, dxp, modrows, w_ext_t, *, ntiles, tile_off, is_ctx, out_rows):
    kc = M_SCALEC if is_ctx else M_SCALE1

    def body(t, vals, fr):
        dp, x_ = vals[0][...], vals[1][...]
        mod, wt = fr
        dh = jnp.dot(dp, wt[...], preferred_element_type=F32)
        incs = [_colsum(dh * x_), _colsum(dh)]
        if is_ctx:
            return [], incs
        return [vals[2][...] + dh * (1.0 + _mrow(mod, kc))], incs

    tiled = [(dproj, _rows(TM, EXT_COLS, lambda i: i + tile_off)), (xa, _rows(TM, D, lambda i: i + tile_off))]
    outs = []
    if not is_ctx:
        tiled.append((dxp, _rows(TM, D)))
        outs = [((out_rows, D), F32, _rows(TM, D))]
    return rowwise(name, body, ntiles=ntiles, tiled=tiled, full=[modrows, w_ext_t], outs=outs, accs=[(1, D)] * 2)


def _attn_semantics():
    return _cparams(("arbitrary", "arbitrary", "arbitrary"))


GLOB_TK = (1280, 1024, 768, 512, 256)
KEY_CHUNK = 256


def glob_fwd(q, kt, vt, s_rows):
    n = kt.shape[0]
    tq = TM
    tk = _pick(n, GLOB_TK)
    nq, nk = s_rows // tq, n // tk
    r = GROUPS * tq
    nch = tk // KEY_CHUNK

    def produce(qs, k_ref, s_buf, c, mx):
        lo = c * KEY_CHUNK
        sn = _dot_nt(qs[...], k_ref[lo:lo + KEY_CHUNK, :])
        s_buf[:, lo:lo + KEY_CHUNK] = sn
        for t in range(KEY_CHUNK // LANES):
            mx = jnp.maximum(mx, sn[:, t * LANES:(t + 1) * LANES])
        return mx

    def kern(q_ref, k0_ref, kn_ref, v_ref, o_ref, lse_ref, qs, s_buf, mx_buf, p_buf, m_s, l_s, acc):
        j = pl.program_id(2)

        @pl.when(j == 0)
        def _():
            qs[...] = _stack_groups(q_ref[...], qs.dtype)
            mx = jnp.full((r, LANES), -jnp.inf, F32)
            for c in range(nch):
                mx = produce(qs, k0_ref, s_buf, c, mx)
            mx_buf[...] = mx
            m_s[...] = jnp.full_like(m_s, -jnp.inf)
            l_s[...] = jnp.zeros_like(l_s)
            acc[...] = jnp.zeros_like(acc)

        m_prev = m_s[...]
        m_new = jnp.maximum(m_prev, jnp.max(mx_buf[...], axis=1, keepdims=True))
        alpha = jnp.exp(m_prev - m_new)
        m_b = jnp.broadcast_to(m_new, (r, LANES))
        mx = jnp.full((r, LANES), -jnp.inf, F32)
        ls = jnp.zeros((r, LANES), F32)
        for c in range(nch):
            for t in range(KEY_CHUNK // LANES):
                lo = c * KEY_CHUNK + t * LANES
                pt = jnp.exp(s_buf[:, lo:lo + LANES] - m_b)
                ls = ls + pt
                p_buf[:, lo:lo + LANES] = pt.astype(p_buf.dtype)
            mx = produce(qs, kn_ref, s_buf, c, mx)
        mx_buf[...] = mx
        l_s[...] = alpha * l_s[...] + jnp.sum(ls, axis=1, keepdims=True)
        acc[...] = alpha * acc[...] + jnp.dot(p_buf[...], v_ref[...], preferred_element_type=F32)
        m_s[...] = m_new

        @pl.when(j == nk - 1)
        def _():
            o_ref[...] = _fold_groups(acc[...] / l_s[...], tq)
            lse_ref[0, 0] = m_s[...] + jnp.log(l_s[...])

    kspec = lambda f: pl.BlockSpec((tk, KV_W), lambda h, i, j: (f(j), h))
    return pl.pallas_call(
        kern, name="glob_fwd", grid=(N_KV, nq, nk),
        in_specs=[pl.BlockSpec((tq, KV_W), lambda h, i, j: (i, h)), kspec(lambda j: 0),
                  kspec(lambda j: jnp.minimum(j + 1, nk - 1)), kspec(lambda j: j)],
        out_specs=[pl.BlockSpec((tq, KV_W), lambda h, i, j: (i, h)),
                   pl.BlockSpec((1, 1, r, 1), lambda h, i, j: (h, i, 0, 0))],
        out_shape=[SDS((s_rows, Q_W), F32), SDS((N_KV, nq, r, 1), F32)],
        scratch_shapes=[pltpu.VMEM((r, KV_W), MXU_DTYPE), pltpu.VMEM((r, tk), F32), pltpu.VMEM((r, LANES), F32),
                        pltpu.VMEM((r, tk), MXU_DTYPE), pltpu.VMEM((r, 1), F32), pltpu.VMEM((r, 1), F32),
                        pltpu.VMEM((r, KV_W), F32)],
        compiler_params=_attn_semantics(),
    )(q, kt, kt, vt)


def attn_delta(o, do, s_rows):
    tq = TM
    nq = s_rows // tq
    r = GROUPS * tq

    def kern(o_ref, do_ref, d_ref):
        d_ref[0, 0] = jnp.sum(_stack_groups(do_ref[...], F32) * _stack_groups(o_ref[...], F32), axis=1, keepdims=True)

    qspec = pl.BlockSpec((tq, KV_W), lambda h, i: (i, h))
    return pl.pallas_call(
        kern, name="attn_delta", grid=(N_KV, nq), in_specs=[qspec, qspec],
        out_specs=pl.BlockSpec((1, 1, r, 1), lambda h, i: (h, i, 0, 0)), out_shape=SDS((N_KV, nq, r, 1), F32),
        compiler_params=_cparams(("parallel", "parallel")),
    )(o, do)


def _compact_t(tt, dtype):
    return jnp.concatenate([tt[g * HEAD_DIM:(g + 1) * HEAD_DIM, :] for g in range(GROUPS)], axis=1).astype(dtype)


def glob_bwd(q, q_t, kt, vt, do, do_t, lse, delta, h, s_rows):
    n = kt.shape[0]
    tq = TM
    tk = _pick(n, GLOB_TK)
    nq, nk = s_rows // tq, n // tk
    r = GROUPS * tq
    nch = tk // KEY_CHUNK

    def kern(q_ref, qt_ref, k_ref, v_ref, do_ref, dot_ref, lse_ref, dl_ref, dq_ref, dkt_ref, dvt_ref, p_buf, ds_buf):
        j = pl.program_id(0)
        i = pl.program_id(1)

        @pl.when(i == 0)
        def _():
            dkt_ref[...] = jnp.zeros_like(dkt_ref)
            dvt_ref[...] = jnp.zeros_like(dvt_ref)

        qs = _stack_groups(q_ref[...], MXU_DTYPE)
        dos = _stack_groups(do_ref[...], MXU_DTYPE)
        lse_b = jnp.broadcast_to(lse_ref[0], (r, LANES))
        dl_b = jnp.broadcast_to(dl_ref[0], (r, LANES))
        for c in range(nch):
            lo = c * KEY_CHUNK
            sc = _dot_nt(qs, k_ref[lo:lo + KEY_CHUNK, :])
            dpc = _dot_nt(dos, v_ref[lo:lo + KEY_CHUNK, :])
            for t in range(KEY_CHUNK // LANES):
                sl = slice(t * LANES, (t + 1) * LANES)
                pt = jnp.exp(sc[:, sl] - lse_b)
                p_buf[:, lo + t * LANES:lo + (t + 1) * LANES] = pt.astype(p_buf.dtype)
                ds_buf[:, lo + t * LANES:lo + (t + 1) * LANES] = (pt * (dpc[:, sl] - dl_b)).astype(ds_buf.dtype)
        dq_t = _fold_groups(jnp.dot(ds_buf[...], k_ref[...], preferred_element_type=F32), tq)
        rows = pl.ds(pl.multiple_of(i * tq, tq), tq)

        @pl.when(j == 0)
        def _():
            dq_ref[rows, :] = dq_t

        @pl.when(j > 0)
        def _():
            dq_ref[rows, :] += dq_t

        dvt_ref[...] += jnp.dot(_compact_t(dot_ref[...], MXU_DTYPE), p_buf[...], preferred_element_type=F32)
        dkt_ref[...] += jnp.dot(_compact_t(qt_ref[...], MXU_DTYPE), ds_buf[...], preferred_element_type=F32)

    col = pl.BlockSpec((1, r, 1), lambda j, i: (i, 0, 0))
    qspec = pl.BlockSpec((tq, KV_W), lambda j, i: (i, h))
    tspec = pl.BlockSpec((KV_W, tq), lambda j, i: (h, i))
    kspec = pl.BlockSpec((tk, KV_W), lambda j, i: (j, h))
    ospec = pl.BlockSpec((HEAD_DIM, tk), lambda j, i: (0, j))
    return pl.pallas_call(
        kern, name=f"glob_bwd_h{h}", grid=(nk, nq),
        in_specs=[qspec, tspec, kspec, kspec, qspec, tspec, col, col],
        out_specs=[pl.BlockSpec(memory_space=pltpu.VMEM), ospec, ospec],
        out_shape=[SDS((s_rows, KV_W), F32), SDS((HEAD_DIM, n), F32), SDS((HEAD_DIM, n), F32)],
        scratch_shapes=[pltpu.VMEM((r, tk), MXU_DTYPE), pltpu.VMEM((r, tk), MXU_DTYPE)],
        compiler_params=_cparams(("arbitrary", "arbitrary")),
    )(q, q_t, kt, vt, do, do_t, lse, delta)


WQ = GROUPS * WIN


def _win_kv_specs(s_rows, c_rows):
    nb = s_rows // WIN
    blk = lambda f: pl.BlockSpec((WIN, KV_W), lambda h, i: (f(i), h))
    return [blk(lambda i: jnp.maximum(i - 1, 0)), blk(lambda i: i), blk(lambda i: jnp.minimum(i + 1, nb - 1)),
            pl.BlockSpec((c_rows, KV_W), lambda h, i: (s_rows // c_rows, h))]


def _win_mask(i, s_rows, shape):
    row = lax.broadcasted_iota(jnp.int32, shape, 0)
    col = lax.broadcasted_iota(jnp.int32, shape, 1)
    qpos = i * WIN + (row & (WIN - 1))
    kpos = (i - 1) * WIN + col
    band = (jnp.abs(qpos - kpos) <= WIN) & (kpos >= 0) & (kpos < s_rows)
    return (col >= 3 * WIN) | band


def _win_cat(dst, parts):
    off = 0
    for p in parts:
        dst[off:off + p.shape[0], :] = p[...]
        off += p.shape[0]


def win_fwd(q, kt, vt, sinkcol, s_rows, c_rows):
    nb = s_rows // WIN
    nkeys = 3 * WIN + c_rows

    def kern(q_ref, kp, kc, kn, kx, vp, vc, vn, vx, sink_ref, o_ref, lse_ref, kcat, vcat):
        i = pl.program_id(1)
        _win_cat(kcat, (kp, kc, kn, kx))
        _win_cat(vcat, (vp, vc, vn, vx))
        qs = _stack_groups(q_ref[...], MXU_DTYPE)
        s = _dot_nt(qs, kcat[...])
        s = jnp.where(_win_mask(i, s_rows, s.shape), s, -jnp.inf)
        sink = sink_ref[0][:, 0:1]
        m = jnp.maximum(jnp.max(s, axis=1, keepdims=True), sink)
        e = jnp.exp(s - m)
        den = jnp.sum(e, axis=1, keepdims=True) + jnp.exp(sink - m)
        o_ref[...] = _fold_groups(_dot(e / den, vcat[...]), WIN)
        lse_ref[0, 0] = m + jnp.log(den)

    kv = _win_kv_specs(s_rows, c_rows)
    qspec = pl.BlockSpec((WIN, KV_W), lambda h, i: (i, h))
    col = pl.BlockSpec((1, 1, WQ, 1), lambda h, i: (h, i, 0, 0))
    return pl.pallas_call(
        kern, name="win_fwd", grid=(N_KV, nb),
        in_specs=[qspec] + kv + kv + [pl.BlockSpec((1, WQ, LANES), lambda h, i: (h, 0, 0))],
        out_specs=[qspec, col], out_shape=[SDS((s_rows, Q_W), F32), SDS((N_KV, nb, WQ, 1), F32)],
        scratch_shapes=[pltpu.VMEM((nkeys, KV_W), MXU_DTYPE), pltpu.VMEM((nkeys, KV_W), MXU_DTYPE)],
        compiler_params=_cparams(("arbitrary", "arbitrary")),
    )(q, kt, kt, kt, kt, vt, vt, vt, vt, sinkcol)


def win_bwd_dq(q, kt, vt, sinkcol, o, do, lse, s_rows, c_rows):
    nb = s_rows // WIN
    nkeys = 3 * WIN + c_rows

    def kern(q_ref, kp, kc, kn, kx, vp, vc, vn, vx, sink_ref, o_ref, do_ref, lse_ref,
             dq_ref, delta_ref, dkx_ref, dvx_ref, dsk_ref, kcat, vcat):
        i = pl.program_id(1)

        @pl.when(i == 0)
        def _():
            dkx_ref[...] = jnp.zeros_like(dkx_ref)
            dvx_ref[...] = jnp.zeros_like(dvx_ref)
            dsk_ref[...] = jnp.zeros_like(dsk_ref)

        _win_cat(kcat, (kp, kc, kn, kx))
        _win_cat(vcat, (vp, vc, vn, vx))
        qs = _stack_groups(q_ref[...], MXU_DTYPE)
        do32 = _stack_groups(do_ref[...], F32)
        delta = jnp.sum(do32 * _stack_groups(o_ref[...], F32), axis=1, keepdims=True)
        dos = do32.astype(MXU_DTYPE)
        lse_c = lse_ref[0, 0]
        s = _dot_nt(qs, kcat[...])
        s = jnp.where(_win_mask(i, s_rows, s.shape), s, -jnp.inf)
        p = jnp.exp(s - lse_c)
        ds = p * (_dot_nt(dos, vcat[...]) - delta)
        dq_ref[...] = _fold_groups(_dot(ds, kcat[...]), WIN)
        delta_ref[0, 0] = delta
        dkx_ref[0] += _dot_tn(ds[:, 3 * WIN:], qs)
        dvx_ref[0] += _dot_tn(p[:, 3 * WIN:], dos)
        dsk_ref[0] += -(jnp.exp(sink_ref[0][:, 0:1] - lse_c) * delta)

    kv = _win_kv_specs(s_rows, c_rows)
    qspec = pl.BlockSpec((WIN, KV_W), lambda h, i: (i, h))
    col = pl.BlockSpec((1, 1, WQ, 1), lambda h, i: (h, i, 0, 0))
    xspec = pl.BlockSpec((1, c_rows, KV_W), lambda h, i: (h, 0, 0))
    return pl.pallas_call(
        kern, name="win_bwd_dq", grid=(N_KV, nb),
        in_specs=[qspec] + kv + kv + [pl.BlockSpec((1, WQ, LANES), lambda h, i: (h, 0, 0)), qspec, qspec, col],
        out_specs=[qspec, col, xspec, xspec, pl.BlockSpec((1, WQ, 1), lambda h, i: (h, 0, 0))],
        out_shape=[SDS((s_rows, Q_W), F32), SDS((N_KV, nb, WQ, 1), F32), SDS((N_KV, c_rows, KV_W), F32),
                   SDS((N_KV, c_rows, KV_W), F32), SDS((N_KV, WQ, 1), F32)],
        scratch_shapes=[pltpu.VMEM((nkeys, KV_W), MXU_DTYPE), pltpu.VMEM((nkeys, KV_W), MXU_DTYPE)],
        compiler_params=_cparams(("arbitrary", "arbitrary")),
    )(q, kt, kt, kt, kt, vt, vt, vt, vt, sinkcol, o, do, lse)


def win_bwd_dkv(q, kt, vt, do, lse, delta, s_rows):
    nb = s_rows // WIN

    def kern(k_ref, v_ref, qp, qc, qn, dop, doc, don, lp, lc, ln, dp_, dc_, dn_, dk_ref, dv_ref):
        j = pl.program_id(1)
        k = k_ref[...]
        v = v_ref[...]
        dk = jnp.zeros((WIN, KV_W), F32)
        dv = jnp.zeros((WIN, KV_W), F32)
        for b, (q_r, do_r, l_r, d_r) in enumerate(((qp, dop, lp, dp_), (qc, doc, lc, dc_), (qn, don, ln, dn_))):
            ib = j - 1 + b
            qs = _stack_groups(q_r[...], MXU_DTYPE)
            dos = _stack_groups(do_r[...], MXU_DTYPE)
            s = _dot_nt(qs, k)
            row = lax.broadcasted_iota(jnp.int32, s.shape, 0)
            col = lax.broadcasted_iota(jnp.int32, s.shape, 1)
            near = jnp.abs(ib * WIN + (row & (WIN - 1)) - (j * WIN + col)) <= WIN
            ok = near & (ib >= 0) & (ib < nb)
            p = jnp.where(ok, jnp.exp(s - l_r[0, 0]), 0.0)
            dv = dv + _dot_tn(p, dos)
            ds = p * (_dot_nt(dos, v) - d_r[0, 0])
            dk = dk + _dot_tn(ds, qs)
        dk_ref[...] = dk
        dv_ref[...] = dv

    fs = (lambda j: jnp.maximum(j - 1, 0), lambda j: j, lambda j: jnp.minimum(j + 1, nb - 1))
    qspecs = [pl.BlockSpec((WIN, KV_W), lambda h, j, f=f: (f(j), h)) for f in fs]
    cols = [pl.BlockSpec((1, 1, WQ, 1), lambda h, j, f=f: (h, f(j), 0, 0)) for f in fs]
    kspec = pl.BlockSpec((WIN, KV_W), lambda h, j: (j, h))
    return pl.pallas_call(
        kern, name="win_bwd_dkv", grid=(N_KV, nb),
        in_specs=[kspec, kspec] + qspecs + qspecs + cols + cols, out_specs=[kspec, kspec],
        out_shape=[SDS((s_rows, Q_W), F32), SDS((s_rows, Q_W), F32)],
        compiler_params=_cparams(("arbitrary", "arbitrary")),
    )(kt, vt, q, q, q, do, do, do, lse, lse, lse, delta, delta, delta)


def adamw(name, w, m, v, grads):
    r, wd = w.shape
    tr = _pick(r, [t for t in ELEMENTWISE_ROWS if t * wd * 4 <= ELEMENTWISE_BLOCK_BYTES])
    stacked = not isinstance(grads, (list, tuple))
    ng = grads.shape[0] if stacked else len(grads)

    def kern(*refs):
        w_ref, m_ref, v_ref = refs[:3]
        g_refs = refs[3:-4]
        g_out, d_out, m_out, v_out = refs[-4:]
        if stacked:
            g = g_refs[0][0]
            for k in range(1, ng):
                g = g + g_refs[0][k]
        else:
            g = g_refs[0][...]
            for gr in g_refs[1:]:
                g = g + gr[...]
        wv = w_ref[...]
        mn = ADAM_B1 * m_ref[...] + (1.0 - ADAM_B1) * g
        vn = ADAM_B2 * v_ref[...] + (1.0 - ADAM_B2) * (g * g)
        m_hat = mn / (1.0 - ADAM_B1 ** ADAM_STEP)
        v_hat = vn / (1.0 - ADAM_B2 ** ADAM_STEP)
        g_out[...] = g
        d_out[...] = -ADAM_LR * (m_hat / (jnp.sqrt(v_hat) + ADAM_EPS) + ADAM_WD * wv)
        m_out[...] = mn
        v_out[...] = vn

    spec = pl.BlockSpec((tr, wd), lambda i: (i, 0))
    gspecs = [pl.BlockSpec((ng, tr, wd), lambda i: (0, i, 0))] if stacked else [spec] * ng
    return pl.pallas_call(
        kern, name=name, grid=(r // tr,), in_specs=[spec] * 3 + gspecs, out_specs=[spec] * 4,
        out_shape=[SDS((r, wd), F32)] * 4, compiler_params=_cparams(("parallel",)),
    )(w, m, v, *([grads] if stacked else grads))


def add2(name, a, b):
    k, r, w = a.shape
    tr = _pick(r, [t for t in ELEMENTWISE_ROWS if t * w * 4 <= ELEMENTWISE_BLOCK_BYTES])

    def kern(a_ref, b_ref, o_ref):
        o_ref[...] = a_ref[...] + b_ref[...]

    spec = pl.BlockSpec((1, tr, w), lambda s, i: (s, i, 0))
    return pl.pallas_call(kern, name=name, grid=(k, r // tr), in_specs=[spec, spec], out_specs=spec,
                          out_shape=SDS(a.shape, a.dtype), compiler_params=_cparams(("parallel", "parallel")))(a, b)


def _rep4(a, off):
    return jnp.concatenate([a[:, off + HEAD_DIM * h: off + HEAD_DIM * (h + 1)] for h in range(N_KV) for _ in range(GROUPS)], axis=1)


def _extend_cols(a):
    return jnp.concatenate([a[:, 0:OFF_KA], _rep4(a, OFF_KA), _rep4(a, OFF_VA), a[:, OFF_QB:OFF_KB],
                            _rep4(a, OFF_KB), _rep4(a, OFF_VB), a[:, OFF_GA:]], axis=1)


def _fold4(a, off):
    r = a.shape[0]
    return a[:, off:off + Q_W].reshape(r, N_KV, GROUPS, HEAD_DIM).sum(axis=2).reshape(r, N_KV * HEAD_DIM)


def _fold_cols(a):
    return jnp.concatenate([a[:, X_QA:X_QA + Q_W], _fold4(a, X_KA), _fold4(a, X_VA), a[:, X_QB:X_QB + Q_W],
                            _fold4(a, X_KB), _fold4(a, X_VB), a[:, X_GL:]], axis=1)


def _rope_tables(s_rows, c_rows):
    pos = jnp.arange(s_rows, dtype=jnp.int32)
    rows = (pos // GRID_W).astype(F32)
    cols = (pos % GRID_W).astype(F32)
    n_freq = HEAD_DIM // 4
    inv_freq = ROPE_THETA ** (-jnp.arange(n_freq, dtype=F32) / n_freq)
    ang_r = rows[:, None] * inv_freq
    ang_c = cols[:, None] * inv_freq
    cos = jnp.concatenate([jnp.cos(ang_r)] * 2 + [jnp.cos(ang_c)] * 2, axis=1)
    sin = jnp.concatenate([-jnp.sin(ang_r), jnp.sin(ang_r), -jnp.sin(ang_c), jnp.sin(ang_c)], axis=1)
    cos = jnp.concatenate([cos, jnp.ones((c_rows, HEAD_DIM), F32)], axis=0)
    sin = jnp.concatenate([sin, jnp.zeros((c_rows, HEAD_DIM), F32)], axis=0)
    return jnp.concatenate([cos, cos], axis=1), jnp.concatenate([sin, sin], axis=1)


def _ff_pad_cols(a):
    r = a.shape[0]
    a = jnp.pad(a.reshape(r, N_DEV, FF_SHARD), ((0, 0), (0, 0), (0, FF_SHARD_PAD - FF_SHARD)))
    return a.reshape(r, 2 * FF)


def _ff_unpad_cols(a):
    r = a.shape[0]
    return a.reshape(r, N_DEV, FF_SHARD_PAD)[:, :, :FF_SHARD].reshape(r, 2 * D_FF)


def _ff_pad_rows(a):
    c = a.shape[1]
    a = jnp.pad(a.reshape(N_DEV // 2, FF_SHARD, c), ((0, 0), (0, FF_SHARD_PAD - FF_SHARD), (0, 0)))
    return a.reshape(FF, c)


def _ff_unpad_rows(a):
    c = a.shape[1]
    return a.reshape(N_DEV // 2, FF_SHARD_PAD, c)[:, :FF_SHARD].reshape(D_FF, c)


BIG = (("w_in", (D, IN_COLS // N_DEV)), ("w_branch_a", (Q_W, D // N_DEV)), ("w_branch_b", (Q_W, D // N_DEV)),
       ("w_out", (D // N_DEV, D)), ("w_up", (D, FF_SHARD_PAD)), ("w_down", (D_FF // N_DEV, D)))
BIG_SIZES = tuple(int(np.prod(s)) for _, s in BIG)
BIG_ROWS = sum(BIG_SIZES) // LANES


def _pack_big(parts):
    return jnp.concatenate([p.reshape(-1) for p in parts]).reshape(BIG_ROWS, LANES)


def _unpack_big(flat):
    lead = flat.shape[:-2]
    f = flat.reshape(*lead, BIG_ROWS * LANES)
    out, off = [], 0
    for (_, shp), sz in zip(BIG, BIG_SIZES, strict=True):
        out.append(f[..., off:off + sz].reshape(*lead, *shp))
        off += sz
    return out


def _cols_to_full(g):
    return jnp.transpose(g, (1, 0, 2)).reshape(g.shape[1], -1)


def _full_to_cols(a):
    r, c = a.shape
    return jnp.transpose(a.reshape(r, N_DEV, c // N_DEV), (1, 0, 2))


SMALL = (("c_ctx", D), ("b_mod", N_MOD * D), ("b_in", IN_COLS), ("attn_sink", N_HEADS), ("q_norm_g", HEAD_DIM),
         ("k_norm_g", HEAD_DIM), ("ln1_g", D), ("ln1_b", D), ("conv_w", 3 * 2 * D_FF // N_DEV), ("conv_b", 2 * D_FF),
         ("ln2_g", D), ("ln2_b", D))
SMALL_TOTAL = sum(n for _, n in SMALL)
SMALL_ROWS = -(-SMALL_TOTAL // (8 * LANES)) * 8


def _pack_small(parts):
    flat = jnp.concatenate([p.reshape(-1).astype(F32) for p in parts])
    return jnp.pad(flat, (0, SMALL_ROWS * LANES - flat.shape[0])).reshape(SMALL_ROWS, LANES)


def _unpack_small(packed):
    f = packed.reshape(-1)
    out, off = {}, 0
    for name, n in SMALL:
        out[name] = f[off:off + n]
        off += n
    return out


RED = (("c_ctx", D), ("b_in", IN_COLS), ("attn_sink", N_HEADS), ("q_norm_g", HEAD_DIM), ("k_norm_g", HEAD_DIM),
       ("ln1_g", D), ("ln1_b", D), ("conv_w", 3 * 2 * FF), ("conv_b", 2 * FF), ("ln2_g", D), ("ln2_b", D))
RED_TOTAL = sum(n for _, n in RED)
RED_ROWS = -(-RED_TOTAL // (8 * LANES)) * 8


def sum8(name, g):
    _, r, w = g.shape

    def kern(g_ref, o_ref):
        acc = g_ref[0]
        for k in range(1, N_DEV):
            acc = acc + g_ref[k]
        o_ref[...] = acc

    return pl.pallas_call(kern, name=name, out_shape=SDS((r, w), F32))(g)


def _local_step(x, ctx, target, modrows, weights, small):
    s_rows, c_rows = x.shape[0], ctx.shape[0]
    n = s_rows + c_rows
    nl = s_rows // TM
    w_in, wba, wbb, w_out, w_up, w_down = weights
    f = lambda a: a.reshape(1, -1).astype(F32)
    b_in, ln1_g, ln1_b, ln2_g, ln2_b, conv_b = (f(small[k]) for k in ("b_in", "ln1_g", "ln1_b", "ln2_g", "ln2_b", "conv_b"))
    conv_w8 = jnp.pad(small["conv_w_full"], ((0, 5), (0, 0)))
    qg = jnp.tile(small["q_norm_g"].reshape(1, HEAD_DIM), (1, N_HEADS))
    kg = jnp.tile(small["k_norm_g"].reshape(1, HEAD_DIM), (1, N_HEADS))
    sinkcol = jnp.broadcast_to(jnp.repeat(small["attn_sink"].reshape(N_KV, GROUPS), WIN, axis=1)[:, :, None], (N_KV, WQ, LANES))
    bd = jnp.kron(jnp.eye(N_HEADS, dtype=F32), jnp.ones((HEAD_DIM, HEAD_DIM), F32)).astype(BF16)
    cos, sin = _rope_tables(s_rows, c_rows)
    w_ext = _extend_cols(w_in)
    b_ext = _extend_cols(b_in)
    xa = jnp.concatenate([x, ctx], axis=0)

    hb, qa, kat, vat, qb, kbt, vbt, tq, rq, tk, rk, gl = inproj_fwd(xa, cos, sin, modrows, w_ext, b_ext, qg, kg, bd, nl)
    oa, lse_a = win_fwd(qa, kat, vat, sinkcol, s_rows, c_rows)
    ob, lse_b = glob_fwd(qb, kbt, vbt, s_rows)
    ya, yb, mrg, y, xhat1, rstd1 = merge_fwd(oa, ob, gl, x, modrows, wba, wbb, w_out, s_rows)
    h2, u0 = ffn_up_fwd(xhat1, modrows, ln1_g, ln1_b, w_up, s_rows)
    a = conv_swiglu_fwd(u0, conv_w8, conv_b, s_rows)
    (dr2, dy2), (loss, dln2_g, dln2_b, dgate2) = ffn_down_loss(a, xhat1, target, modrows, ln1_g, ln1_b, ln2_g, ln2_b, w_down, s_rows)

    da = ffn_down_bwd(dy2, w_down.T, s_rows)
    dw_down = mm_tn("dw_down", a, dy2, s_rows)
    du, (dconv_b, dcw0, dcw1, dcw2) = swiglu_conv_bwd(u0, da, conv_w8, conv_b, s_rows)
    du0 = conv_bwd_input(du, conv_w8, s_rows)
    dw_up = mm_tn("dw_up", h2, du0, s_rows)
    (dy, dxp), (dscale2, dshift2, dln1_g, dln1_b, dgate1) = ffn_up_ln1_bwd(du0, dr2, xhat1, y, rstd1, modrows, ln1_g, ln1_b, w_up.T, s_rows)
    dya, dyb, dgl, doa, dob = merge_bwd(dy, ya, yb, gl, w_out.T, wba.T, wbb.T, s_rows)
    dw_out = mm_tn("dw_out", mrg, dy, s_rows)
    dwba = mm_tn("dw_branch_a", oa, dya, s_rows)
    dwbb = mm_tn("dw_branch_b", ob, dyb, s_rows)

    dqa, delta_a, dkax, dvax, dsk = win_bwd_dq(qa, kat, vat, sinkcol, oa, doa, lse_a, s_rows, c_rows)
    dka, dva = win_bwd_dkv(qa, kat, vat, doa, lse_a, delta_a, s_rows)
    delta_b = attn_delta(ob, dob, s_rows)
    qb_t = qb[:s_rows].T
    dob_t = dob.astype(MXU_DTYPE).T
    heads = [glob_bwd(qb, qb_t, kbt, vbt, dob, dob_t, lse_b[h], delta_b[h], h, s_rows) for h in range(N_KV)]
    dqb = jnp.concatenate([hd[0] for hd in heads], axis=1)
    pad = jnp.zeros((n, KV_W - HEAD_DIM), F32)
    dkb = jnp.concatenate([t for hd in heads for t in (hd[1].T, pad)], axis=1)
    dvb = jnp.concatenate([t for hd in heads for t in (hd[2].T, pad)], axis=1)
    ctx_cols = lambda t: jnp.transpose(t, (1, 0, 2)).reshape(c_rows, Q_W)
    dproj, (db_ext, dqg, dkg) = qk_bwd(dqa, dka, ctx_cols(dkax), dva, ctx_cols(dvax), dqb, dkb, dvb, dgl, tq, rq, tk, rk,
                                       cos, sin, qg, kg, bd, nl, n)
    w_ext_t = w_ext.T
    (grad_x,), (dscale1, dshift1) = inproj_bwd("inproj_bwd", dproj, xa, dxp, modrows, w_ext_t, ntiles=nl, tile_off=0,
                                               is_ctx=False, out_rows=s_rows)
    _, (dscale_c, dshift_c) = inproj_bwd("inproj_bwd_ctx", dproj, xa, None, modrows, w_ext_t, ntiles=c_rows // TM,
                                         tile_off=nl, is_ctx=True, out_rows=0)
    dw_in = _fold_cols(mm_tn("dw_in", hb, dproj, n))

    dmod = jnp.concatenate([dshift1, dscale1, dgate1, dshift2, dscale2, dgate2], axis=1)
    dmod_c = jnp.concatenate([dshift_c, dscale_c, jnp.zeros((1, (N_MOD - 2) * D), F32)], axis=1)
    fold_g = lambda t: t.reshape(N_HEADS, HEAD_DIM).sum(axis=0)
    red = {
        "b_in": _fold_cols(db_ext), "attn_sink": dsk.reshape(N_HEADS, WIN).sum(axis=1), "q_norm_g": fold_g(dqg),
        "k_norm_g": fold_g(dkg), "ln1_g": dln1_g, "ln1_b": dln1_b, "conv_w": jnp.concatenate([dcw0, dcw1, dcw2], axis=0),
        "conv_b": dconv_b, "ln2_g": dln2_g, "ln2_b": dln2_b,
    }
    return loss[0, 0], grad_x, (dw_in, dwba, dwbb, dw_out, dw_up, dw_down), dmod, dmod_c, red


def kernel(x, c, ctx, c_ctx, w_mod, b_mod, w_in, b_in, attn_sink, q_norm_g, k_norm_g, w_branch_a, w_branch_b, w_out, ln1_g, ln1_b, w_up, conv_w, conv_b, w_down, ln2_g, ln2_b, loss_target, m_c_ctx, m_w_mod, m_b_mod, m_w_in, m_b_in, m_attn_sink, m_q_norm_g, m_k_norm_g, m_w_branch_a, m_w_branch_b, m_w_out, m_ln1_g, m_ln1_b, m_w_up, m_conv_w, m_conv_b, m_w_down, m_ln2_g, m_ln2_b, v_c_ctx, v_w_mod, v_b_mod, v_w_in, v_b_in, v_attn_sink, v_q_norm_g, v_k_norm_g, v_w_branch_a, v_w_branch_b, v_w_out, v_ln1_g, v_ln1_b, v_w_up, v_conv_w, v_conv_b, v_w_down, v_ln2_g, v_ln2_b):
    ax, ay, ac = (lax.axis_index(a) for a in AXES)
    me = 4 * ax + 2 * ay + ac
    chip = 2 * ax + ay
    mod_w = N_MOD * D // N_DEV
    params = dict(c_ctx=c_ctx, w_mod=w_mod, b_mod=b_mod, w_in=w_in, b_in=b_in, attn_sink=attn_sink, q_norm_g=q_norm_g,
                  k_norm_g=k_norm_g, w_branch_a=w_branch_a, w_branch_b=w_branch_b, w_out=w_out, ln1_g=ln1_g, ln1_b=ln1_b,
                  w_up=w_up, conv_w=conv_w, conv_b=conv_b, w_down=w_down, ln2_g=ln2_g, ln2_b=ln2_b)
    mom_m = dict(c_ctx=m_c_ctx, w_mod=m_w_mod, b_mod=m_b_mod, w_in=m_w_in, b_in=m_b_in, attn_sink=m_attn_sink,
                 q_norm_g=m_q_norm_g, k_norm_g=m_k_norm_g, w_branch_a=m_w_branch_a, w_branch_b=m_w_branch_b, w_out=m_w_out,
                 ln1_g=m_ln1_g, ln1_b=m_ln1_b, w_up=m_w_up, conv_w=m_conv_w, conv_b=m_conv_b, w_down=m_w_down,
                 ln2_g=m_ln2_g, ln2_b=m_ln2_b)
    mom_v = dict(c_ctx=v_c_ctx, w_mod=v_w_mod, b_mod=v_b_mod, w_in=v_w_in, b_in=v_b_in, attn_sink=v_attn_sink,
                 q_norm_g=v_q_norm_g, k_norm_g=v_k_norm_g, w_branch_a=v_w_branch_a, w_branch_b=v_w_branch_b, w_out=v_w_out,
                 ln1_g=v_ln1_g, ln1_b=v_ln1_b, w_up=v_w_up, conv_w=v_conv_w, conv_b=v_conv_b, w_down=v_w_down,
                 ln2_g=v_ln2_g, ln2_b=v_ln2_b)
    big_names = [nm for nm, _ in BIG]

    def shard(tree, nm):
        t = tree[nm][0]
        return jnp.pad(t, ((0, 0), (0, FF_SHARD_PAD - FF_SHARD))) if nm == "w_up" else t

    wg = all_gather("ag_weights", _pack_big([shard(params, nm).astype(MXU_DTYPE) for nm in big_names]))
    g_in, g_ba, g_bb, g_out, g_up, g_down = _unpack_big(wg)
    weights = (_cols_to_full(g_in), _cols_to_full(g_ba), _cols_to_full(g_bb), g_out.reshape(D, D), _cols_to_full(g_up),
               _ff_pad_rows(g_down.reshape(D_FF, D)))

    c_all = all_gather("ag_c", c.reshape(8, LANES)).reshape(N_DEV, D)
    cs = jnp.concatenate([c_all, c_ctx.reshape(1, D), jnp.zeros((7, D), F32)], axis=0)
    w_mod_sh = w_mod[0]
    b_mod_sh = lax.dynamic_slice(b_mod, (0, me * mod_w), (1, mod_w))
    mod_part = mod_fwd(cs, w_mod_sh, b_mod_sh)
    mg = all_gather("ag_mod", mod_part.reshape(16 * mod_w // LANES, LANES)).reshape(N_DEV, 16, mod_w)
    mod = lax.dynamic_index_in_dim(mg, me, axis=1, keepdims=False).reshape(N_MOD, D)
    mod_c = mg[:, 8, :].reshape(N_MOD, D)
    modrows = jnp.stack([mod[0], mod[1], mod_c[0], mod_c[1], mod[2], mod[3], mod[4], mod[5]], axis=0)

    conv_w_full = all_gather("ag_conv_w", jnp.pad(conv_w[0], ((0, 5), (0, FF_SHARD_PAD - FF_SHARD))))
    conv_w_full = _cols_to_full(conv_w_full[:, :3, :])
    small = dict(b_in=b_in, ln1_g=ln1_g, ln1_b=ln1_b, ln2_g=ln2_g, ln2_b=ln2_b, conv_b=_ff_pad_cols(conv_b),
                 conv_w_full=conv_w_full, q_norm_g=q_norm_g, k_norm_g=k_norm_g, attn_sink=attn_sink)
    loss, grad_x, big_grads, dmod, dmod_c, red = _local_step(x[0], ctx[0], loss_target[0], modrows, weights, small)
    loss = lax.psum(loss, AXES)

    dm = all_gather("ag_dmod", jnp.concatenate([dmod, dmod_c], axis=0).reshape(2 * N_MOD * D // LANES, LANES))
    dm = dm.reshape(N_DEV, 2, N_MOD * D)
    dm_all = jnp.concatenate([dm[:, 0], dm[:, 1]], axis=0)
    dm_sh = lax.dynamic_slice(dm_all, (0, me * mod_w), (16, mod_w))
    dw_mod, dcc, db_mod = mod_bwd(cs, w_mod_sh, dm_sh, dm_all)
    red["c_ctx"] = dcc[8]

    red_vec = jnp.concatenate([red[nm].reshape(-1) for nm, _ in RED])
    red_vec = jnp.pad(red_vec, (0, RED_ROWS * LANES - RED_TOTAL)).reshape(RED_ROWS, LANES)
    red_sum = sum8("sum_small", all_gather("ag_small", red_vec)).reshape(-1)
    gsm, off = {}, 0
    for nm, k in RED:
        gsm[nm] = red_sum[off:off + k]
        off += k
    gsm["b_mod"] = db_mod.reshape(-1)
    gsm["conv_b"] = _ff_unpad_cols(gsm["conv_b"].reshape(1, 2 * FF))
    gsm["conv_w"] = lax.dynamic_slice(gsm["conv_w"].reshape(3, 2 * FF), (0, me * FF_SHARD_PAD), (3, FF_SHARD_PAD))[:, :FF_SHARD]
    sm_names = [nm for nm, _ in SMALL]
    gs, ds, ms, vs = adamw("adamw_small", _pack_small([params[nm] for nm in sm_names]),
                           _pack_small([mom_m[nm] for nm in sm_names]), _pack_small([mom_v[nm] for nm in sm_names]),
                           [_pack_small([gsm[nm] for nm in sm_names])])
    sm_out = [_unpack_small(t) for t in (gs, ds, ms, vs)]

    dw_in, dwba, dwbb, dw_out, dw_up, dw_down = big_grads
    slabs = jnp.concatenate([t.reshape(N_DEV, -1) for t in (
        _full_to_cols(dw_in), _full_to_cols(dwba), _full_to_cols(dwbb), dw_out, _full_to_cols(dw_up),
        _ff_unpad_rows(dw_down))], axis=1)
    by_core = slabs.reshape(4, 2, BIG_ROWS, LANES)
    keep = lax.dynamic_index_in_dim(by_core, ac, axis=1, keepdims=False)
    give = lax.dynamic_index_in_dim(by_core, 1 - ac, axis=1, keepdims=False)
    got = exchange("rs_sibling", give.reshape(1, 4 * BIG_ROWS, LANES), to_chips=False).reshape(4, BIG_ROWS, LANES)
    pair = add2("rs_pair_sum", keep, got)
    outbox = jnp.stack([lax.dynamic_index_in_dim(pair, jnp.bitwise_xor(chip, m), axis=0, keepdims=False) for m in (1, 2, 3)])
    inbox = exchange("rs_chips", outbox, to_chips=True)
    mine = lax.dynamic_index_in_dim(pair, chip, axis=0, keepdims=False)
    gb, db, mb, vb = adamw("adamw_big", _pack_big([shard(params, nm) for nm in big_names]),
                           _pack_big([shard(mom_m, nm) for nm in big_names]), _pack_big([shard(mom_v, nm) for nm in big_names]),
                           [mine, inbox[0], inbox[1], inbox[2]])
    big_out = [dict(zip(big_names, _unpack_big(t), strict=True)) for t in (gb, db, mb, vb)]
    for out in big_out:
        out["w_up"] = out["w_up"][:, :FF_SHARD]
    gm, dmo, mmo, vmo = adamw("adamw_mod", w_mod[0], m_w_mod[0], v_w_mod[0], [dw_mod])
    mod_out = (gm, dmo, mmo, vmo)

    order = ["c_ctx", "w_mod", "b_mod", "w_in", "b_in", "attn_sink", "q_norm_g", "k_norm_g", "w_branch_a", "w_branch_b",
             "w_out", "ln1_g", "ln1_b", "w_up", "conv_w", "conv_b", "w_down", "ln2_g", "ln2_b"]
    results = [loss, grad_x[None]]
    for kind in range(4):
        for nm in order:
            if nm == "w_mod":
                val = mod_out[kind]
            elif nm in big_out[kind]:
                val = big_out[kind][nm]
            else:
                val = sm_out[kind][nm]
            results.append(val.reshape(params[nm].shape))
    return tuple(results)
```

```python
import functools

import jax
import jax.numpy as jnp
import numpy as np
from jax import lax
from jax.experimental import pallas as pl
from jax.experimental.pallas import tpu as pltpu

F32 = jnp.float32
BF16 = jnp.bfloat16
MXU_DTYPE = BF16

AXES = ("x", "y", "c")
N_DEV = 8
D = 1024
HEAD_DIM = 64
N_HEADS = 8
N_KV = 2
GROUPS = 4
KV_W = GROUPS * HEAD_DIM
Q_W = N_HEADS * HEAD_DIM
GRID_W = 64
WIN = 128
ROPE_THETA = 10000.0
D_FF = 2816
FF_SHARD = 2 * D_FF // N_DEV
FF_SHARD_PAD = 768
FF = N_DEV // 2 * FF_SHARD_PAD
LN_EPS = 1e-5
QK_EPS = 1e-6
N_MOD = 6
ALPHA = 2.0 ** 0.25
Q_SCALE = HEAD_DIM ** -0.5
IN_COLS = 3584
OFF_KA, OFF_VA, OFF_QB, OFF_KB, OFF_VB, OFF_GA = 512, 640, 768, 1280, 1408, 1536
EXT_COLS = 6 * Q_W + 2 * D
X_QA, X_KA, X_VA, X_QB, X_KB, X_VB, X_GL = 0, 512, 1024, 1536, 2048, 2560, 3072
ADAM_LR, ADAM_B1, ADAM_B2, ADAM_EPS, ADAM_WD, ADAM_STEP = 0.001, 0.9, 0.999, 1e-08, 0.01, 10
LANES = 128
TM = 256
VMEM_LIMIT = 56 * 1024 * 1024
ELEMENTWISE_BLOCK_BYTES = 1 << 20
ELEMENTWISE_ROWS = (1824, 1408, 1024, 512, 256, 128, 64, 32, 16, 8)

ANY = pl.BlockSpec(memory_space=pl.ANY)
SDS = jax.ShapeDtypeStruct


def _pick(n, candidates):
    for t in candidates:
        if n % t == 0:
            return t
    raise ValueError(f"no tile for {n}")


def _full(a):
    nd = a.ndim
    return pl.BlockSpec(a.shape, lambda *_: (0,) * nd)


def _rows(tm, w, fn=lambda t: t):
    return pl.BlockSpec((tm, w), lambda i: (fn(i), 0))


def _dot(a, b):
    return jnp.dot(a.astype(MXU_DTYPE), b.astype(MXU_DTYPE), preferred_element_type=F32)


def _dot_nt(a, b):
    return lax.dot_general(a.astype(MXU_DTYPE), b.astype(MXU_DTYPE), (((1,), (1,)), ((), ())), preferred_element_type=F32)


def _dot_tn(a, b):
    return lax.dot_general(a.astype(MXU_DTYPE), b.astype(MXU_DTYPE), (((0,), (0,)), ((), ())), preferred_element_type=F32)


def _cparams(sem):
    return pltpu.CompilerParams(dimension_semantics=sem, vmem_limit_bytes=VMEM_LIMIT)


def all_gather(name, v):
    r, w = v.shape

    def body(x_ref, out_ref, send_sems, recv_sems, local_sem):
        x, y, c = (lax.axis_index(a) for a in AXES)
        me, sibling = (x, y, c), (x, y, 1 - c)
        chips = [(1 - x, y), (x, 1 - y), (1 - x, 1 - y)]

        def rows(px, py, pc):
            return out_ref.at[4 * px + 2 * py + pc]

        def copy(k, block, to, src=None):
            return pltpu.make_async_remote_copy(
                src_ref=rows(*block) if src is None else src, dst_ref=rows(*block),
                send_sem=send_sems.at[k], recv_sem=recv_sems.at[k],
                device_id=to, device_id_type=pl.DeviceIdType.MESH)

        mine = pltpu.make_async_copy(x_ref, rows(*me), local_sem)
        mine.start()
        first = [copy(0, me, sibling, src=x_ref)]
        first += [copy(1 + j, me, (*chip, c), src=x_ref) for j, chip in enumerate(chips)]
        for cp in first:
            cp.start()
        passed = [copy(4 + j, (*chip, c), sibling) for j, chip in enumerate(chips)]
        for j, chip in enumerate(chips):
            copy(1 + j, (*chip, c), me).wait_recv()
            passed[j].start()
        copy(0, sibling, me).wait_recv()
        for j, chip in enumerate(chips):
            copy(4 + j, (*chip, 1 - c), me).wait_recv()
        for cp in first + passed:
            cp.wait_send()
        mine.wait()

    return pl.pallas_call(
        body, name=name, out_shape=SDS((N_DEV, r, w), v.dtype), in_specs=[ANY], out_specs=ANY,
        scratch_shapes=[pltpu.SemaphoreType.DMA((7,)), pltpu.SemaphoreType.DMA((7,)), pltpu.SemaphoreType.DMA],
    )(v)


def exchange(name, outbox, to_chips):
    k = outbox.shape[0]
    assert k == (3 if to_chips else 1)

    def body(out_ref, in_ref, send_sems, recv_sems):
        x, y, c = (lax.axis_index(a) for a in AXES)
        peers = [(x, 1 - y, c), (1 - x, y, c), (1 - x, 1 - y, c)] if to_chips else [(x, y, 1 - c)]
        copies = [
            pltpu.make_async_remote_copy(
                src_ref=out_ref.at[m], dst_ref=in_ref.at[m], send_sem=send_sems.at[m], recv_sem=recv_sems.at[m],
                device_id=peer, device_id_type=pl.DeviceIdType.MESH)
            for m, peer in enumerate(peers)
        ]
        for cp in copies:
            cp.start()
        for cp in copies:
            cp.wait_recv()
        for cp in copies:
            cp.wait_send()

    return pl.pallas_call(
        body, name=name, out_shape=SDS(outbox.shape, outbox.dtype), in_specs=[ANY], out_specs=ANY,
        scratch_shapes=[pltpu.SemaphoreType.DMA((k,)), pltpu.SemaphoreType.DMA((k,))],
    )(outbox)


def rowwise(name, body, *, ntiles, tile_off=0, tiled, full, outs, accs=()):
    nt, nf, no = len(tiled), len(full), len(outs)

    def kern(*refs):
        i = pl.program_id(0)
        out_vals, incs = body(i + tile_off, refs[:nt], refs[nt:nt + nf])
        for r, v in zip(refs[nt + nf:nt + nf + no], out_vals, strict=True):
            r[...] = v.astype(r.dtype)
        acc_refs = refs[nt + nf + no:]

        @pl.when(i == 0)
        def _():
            for r in acc_refs:
                r[...] = jnp.zeros_like(r)

        for r, v in zip(acc_refs, incs, strict=True):
            r[...] += v

    res = pl.pallas_call(
        kern, name=name, grid=(ntiles,),
        in_specs=[s for _, s in tiled] + [_full(a) for a in full],
        out_specs=[s for _, _, s in outs] + [pl.BlockSpec(s, lambda i, n=len(s): (0,) * n) for s in accs],
        out_shape=[SDS(s, d) for s, d, _ in outs] + [SDS(s, F32) for s in accs],
        compiler_params=_cparams(("arbitrary",) if accs else ("parallel",)),
    )(*[a for a, _ in tiled], *full)
    return res[:no], res[no:]


def mm_tn(name, a, b, rows):
    ka, nb = a.shape[1], b.shape[1]
    tr = _pick(rows, (1280, 1024, 768, 512, 256))
    tn = _pick(nb, (512, 256, 128))

    def kern(a_ref, b_ref, o_ref):
        @pl.when(pl.program_id(1) == 0)
        def _():
            o_ref[...] = jnp.zeros_like(o_ref)

        o_ref[...] += _dot_tn(a_ref[...], b_ref[...])

    return pl.pallas_call(
        kern, name=name, grid=(nb // tn, rows // tr),
        in_specs=[pl.BlockSpec((tr, ka), lambda n, r: (r, 0)), pl.BlockSpec((tr, tn), lambda n, r: (r, n))],
        out_specs=pl.BlockSpec((ka, tn), lambda n, r: (0, n)), out_shape=SDS((ka, nb), F32),
        compiler_params=_cparams(("parallel", "arbitrary")),
    )(a, b)


def _swap16(t):
    w = t.shape[1]
    lane = lax.broadcasted_iota(jnp.int32, t.shape, 1)
    return jnp.where((lane & 16) == 0, pltpu.roll(t, w - 16, 1), pltpu.roll(t, 16, 1))


def _rope(t, cos, sin):
    return t * cos + _swap16(t) * sin


def _rope_t(d, cos, sin):
    return d * cos - _swap16(d) * sin


def _seg_sum64(a, bd_ref):
    bd = bd_ref[...]
    hi = a.astype(BF16)
    lo = (a - hi.astype(F32)).astype(BF16)
    return jnp.dot(hi, bd, preferred_element_type=F32) + jnp.dot(lo, bd, preferred_element_type=F32)


def _lane_block(shape):
    return jnp.right_shift(lax.broadcasted_iota(jnp.int32, shape, 1), 6)


def _stack_groups(t, dtype):
    blk = _lane_block(t.shape)
    return jnp.concatenate([jnp.where(blk == g, t, jnp.zeros_like(t)).astype(dtype) for g in range(GROUPS)], axis=0)


def _fold_groups(ts, tq):
    blk = _lane_block((tq, KV_W))
    out = jnp.zeros((tq, KV_W), ts.dtype)
    for g in range(GROUPS):
        out = jnp.where(blk == g, ts[g * tq:(g + 1) * tq], out)
    return out


def _layer_norm_bwd(dxh, xhat, rstd):
    m1 = jnp.mean(dxh, axis=1, keepdims=True)
    m2 = jnp.mean(dxh * xhat, axis=1, keepdims=True)
    return rstd * (dxh - m1 - xhat * m2)


def _colsum(a):
    return jnp.sum(a, axis=0, keepdims=True)


def _shifted_rows(t, prev_row, next_row):
    n = t.shape[0]
    row = lax.broadcasted_iota(jnp.int32, t.shape, 0)
    up = jnp.where(row == 0, prev_row, pltpu.roll(t, 1, 0))
    dn = jnp.where(row == n - 1, next_row, pltpu.roll(t, n - 1, 0))
    return up, dn


def mod_fwd(cs, w_sh, b_sh):
    def kern(c_ref, w_ref, b_ref, o_ref):
        o_ref[...] = _dot(jax.nn.silu(c_ref[...]), w_ref[...]) + b_ref[...]

    return pl.pallas_call(kern, name="mod_fwd", out_shape=SDS((16, w_sh.shape[1]), F32),
                          compiler_params=pltpu.CompilerParams(vmem_limit_bytes=VMEM_LIMIT))(cs, w_sh, b_sh)


def mod_bwd(cs, w_sh, dm_sh, dm_all):
    hp = lax.Precision.HIGHEST

    def kern(c_ref, w_ref, dm_ref, da_ref, dw_ref, dc_ref, db_ref):
        c = c_ref[...]
        sg = jax.nn.sigmoid(c)
        sc = c * sg
        dm = dm_ref[...]
        dmc = dm_ref[8:9, :]
        for i in range(9, 16):
            dmc = dmc + dm_ref[i:i + 1, :]
        row = lax.broadcasted_iota(jnp.int32, dm.shape, 0)
        a = jnp.where(row < 8, dm, jnp.where(row == 8, dmc, 0.0))
        dw_ref[...] = lax.dot_general(sc, a, (((0,), (0,)), ((), ())), precision=hp, preferred_element_type=F32)
        dsc = lax.dot_general(a, w_ref[...], (((1,), (1,)), ((), ())), precision=hp, preferred_element_type=F32)
        dc_ref[...] = dsc * (sg * (1.0 + c * (1.0 - sg)))
        db = da_ref[0:1, :]
        for i in range(1, 16):
            db = db + da_ref[i:i + 1, :]
        db_ref[...] = db

    return pl.pallas_call(
        kern, name="mod_bwd",
        out_shape=[SDS(w_sh.shape, F32), SDS((16, D), F32), SDS((1, dm_all.shape[1]), F32)],
        compiler_params=pltpu.CompilerParams(vmem_limit_bytes=VMEM_LIMIT))(cs, w_sh, dm_sh, dm_all)


M_SHIFT1, M_SCALE1, M_SHIFTC, M_SCALEC, M_GATE1, M_SHIFT2, M_SCALE2, M_GATE2 = range(8)


def _mrow(ref, k):
    return ref[k:k + 1, :]


def inproj_fwd(xa, cos, sin, modrows, w_ext, b_ext, qg, kg, bd, n_lat_tiles):
    n = xa.shape[0]

    def body(t, vals, fr):
        x, cs, sn = (v[...] for v in vals)
        mod, w, b, qg_r, kg_r, bd_r = fr
        is_ctx = t >= n_lat_tiles
        shift = jnp.where(is_ctx, _mrow(mod, M_SHIFTC), _mrow(mod, M_SHIFT1))
        scale = jnp.where(is_ctx, _mrow(mod, M_SCALEC), _mrow(mod, M_SCALE1))
        hb = (x * (1.0 + scale) + shift).astype(MXU_DTYPE)
        proj = jnp.dot(hb, w[...], preferred_element_type=F32) + b[...]
        cos4 = jnp.concatenate([cs] * 4, axis=1)
        sin4 = jnp.concatenate([sn] * 4, axis=1)
        qa = _rope(proj[:, X_QA:X_QA + Q_W], cos4, sin4) * Q_SCALE
        ka = _rope(proj[:, X_KA:X_KA + Q_W], cos4, sin4)
        va = proj[:, X_VA:X_VA + Q_W]
        tq = proj[:, X_QB:X_QB + Q_W]
        rq = lax.rsqrt(_seg_sum64(tq * tq, bd_r) * (1.0 / HEAD_DIM) + QK_EPS)
        qb = _rope(tq * rq * qg_r[...], cos4, sin4) * Q_SCALE
        tk = proj[:, X_KB:X_KB + Q_W]
        rk = lax.rsqrt(_seg_sum64(tk * tk, bd_r) * (1.0 / HEAD_DIM) + QK_EPS)
        kb = _rope(tk * rk * kg_r[...], cos4, sin4)
        vb = proj[:, X_VB:X_VB + Q_W]
        gl = proj[:, X_GL:]
        return [hb, qa, ka, va, qb, kb, vb, tq, rq, tk, rk, gl], []

    mx = MXU_DTYPE
    outs = [((n, D), mx, _rows(TM, D))] + [((n, Q_W), mx, _rows(TM, Q_W))] * 6 + \
           [((n, Q_W), F32, _rows(TM, Q_W))] * 4 + [((n, 2 * D), F32, _rows(TM, 2 * D))]
    res, _ = rowwise("inproj_fwd", body, ntiles=n // TM,
                     tiled=[(xa, _rows(TM, D)), (cos, _rows(TM, LANES)), (sin, _rows(TM, LANES))],
                     full=[modrows, w_ext, b_ext, qg, kg, bd], outs=outs)
    return res


def merge_fwd(oa, ob, gl, x, modrows, wba, wbb, w_out, s_rows):
    def body(t, vals, fr):
        oa_, ob_, gl_, x_ = (v[...] for v in vals)
        mod, wa, wb, wo = fr
        ya = _dot(oa_, wa[...])
        yb = _dot(ob_, wb[...])
        ga = jax.nn.sigmoid(gl_[:, :D])
        gb = jax.nn.sigmoid(gl_[:, D:])
        mrg = ga * ya + gb * yb
        y = _dot(mrg, wo[...])
        r1 = ALPHA * x_ + _mrow(mod, M_GATE1) * y
        mu = jnp.mean(r1, axis=1, keepdims=True)
        xc = r1 - mu
        var = jnp.mean(xc * xc, axis=1, keepdims=True)
        rstd = lax.rsqrt(var + LN_EPS)
        xhat = xc * rstd
        return [ya, yb, mrg, y, xhat, rstd], []

    outs = [((s_rows, D), F32, _rows(TM, D))] * 2 + [((s_rows, D), MXU_DTYPE, _rows(TM, D))] + \
           [((s_rows, D), F32, _rows(TM, D))] * 2 + [((s_rows, 1), F32, _rows(TM, 1))]
    res, _ = rowwise("merge_fwd", body, ntiles=s_rows // TM,
                     tiled=[(oa, _rows(TM, Q_W)), (ob, _rows(TM, Q_W)), (gl, _rows(TM, 2 * D)), (x, _rows(TM, D))],
                     full=[modrows, wba, wbb, w_out], outs=outs)
    return res


def ffn_up_fwd(xhat1, modrows, ln_g, ln_b, w_up, s_rows):
    def body(t, vals, fr):
        xh = vals[0][...]
        mod, g_r, b_r, w = fr
        x1 = xh * g_r[...] + b_r[...]
        h2 = (x1 * (1.0 + _mrow(mod, M_SCALE2)) + _mrow(mod, M_SHIFT2)).astype(MXU_DTYPE)
        return [h2, jnp.dot(h2, w[...], preferred_element_type=F32)], []

    res, _ = rowwise("ffn_up_fwd", body, ntiles=s_rows // TM, tiled=[(xhat1, _rows(TM, D))],
                     full=[modrows, ln_g, ln_b, w_up],
                     outs=[((s_rows, D), MXU_DTYPE, _rows(TM, D)), ((s_rows, 2 * FF), F32, _rows(TM, 2 * FF))])
    return res


TC = 128


def _halo_specs(tm, w, s_rows):
    per = tm // 8
    last = s_rows // 8 - 1
    return (pl.BlockSpec((8, w), lambda i: (jnp.maximum(i * per - 1, 0), 0)),
            pl.BlockSpec((8, w), lambda i: (jnp.minimum((i + 1) * per, last), 0)))


def _halo_rows(t, ntiles, prev_ref, next_ref):
    prev_row = jnp.where(t == 0, 0.0, prev_ref[7:8, :].astype(F32))
    next_row = jnp.where(t == ntiles - 1, 0.0, next_ref[0:1, :].astype(F32))
    return prev_row, next_row


def conv_swiglu_fwd(u0, conv_w8, conv_b, s_rows):
    w2 = 2 * FF
    nt = s_rows // TC

    def body(t, vals, fr):
        u_ref, pv, nx = vals
        cw, cb = fr
        u = u_ref[...]
        up, dn = _shifted_rows(u, *_halo_rows(t, nt, pv, nx))
        uc = cw[0:1, :] * up + cw[1:2, :] * u + cw[2:3, :] * dn + cb[...]
        gate, val = uc[:, :FF], uc[:, FF:]
        return [gate * jax.nn.sigmoid(gate) * val], []

    hp, hn = _halo_specs(TC, w2, s_rows)
    res, _ = rowwise("conv_swiglu_fwd", body, ntiles=nt,
                     tiled=[(u0, _rows(TC, w2)), (u0, hp), (u0, hn)], full=[conv_w8, conv_b],
                     outs=[((s_rows, FF), MXU_DTYPE, _rows(TC, FF))])
    return res[0]


def ffn_down_loss(a, xhat1, target, modrows, ln1_g, ln1_b, ln2_g, ln2_b, w_down, s_rows):
    def body(t, vals, fr):
        a_, xh1, tgt = (v[...] for v in vals)
        mod, g1, b1, g2, b2, wd = fr
        y2 = jnp.dot(a_, wd[...], preferred_element_type=F32)
        x1 = xh1 * g1[...] + b1[...]
        gate2 = _mrow(mod, M_GATE2)
        r2 = ALPHA * x1 + gate2 * y2
        mu = jnp.mean(r2, axis=1, keepdims=True)
        xc = r2 - mu
        var = jnp.mean(xc * xc, axis=1, keepdims=True)
        rstd = lax.rsqrt(var + LN_EPS)
        xhat = xc * rstd
        out = xhat * g2[...] + b2[...]
        diff = out - tgt
        loss = 0.5 * jnp.sum(jnp.mean(diff * diff, axis=1, keepdims=True), axis=0, keepdims=True)
        dout = diff * (1.0 / D)
        dr2 = _layer_norm_bwd(dout * g2[...], xhat, rstd)
        incs = [loss, _colsum(dout * xhat), _colsum(dout), _colsum(dr2 * y2)]
        return [dr2, dr2 * gate2], incs

    res, accs = rowwise("ffn_down_loss", body, ntiles=s_rows // TM,
                        tiled=[(a, _rows(TM, FF)), (xhat1, _rows(TM, D)), (target, _rows(TM, D))],
                        full=[modrows, ln1_g, ln1_b, ln2_g, ln2_b, w_down],
                        outs=[((s_rows, D), F32, _rows(TM, D)), ((s_rows, D), MXU_DTYPE, _rows(TM, D))],
                        accs=[(1, 1), (1, D), (1, D), (1, D)])
    return res, accs


def ffn_down_bwd(dy2, w_down_t, s_rows):
    def body(t, vals, fr):
        return [jnp.dot(vals[0][...], fr[0][...], preferred_element_type=F32)], []

    res, _ = rowwise("ffn_down_bwd", body, ntiles=s_rows // TM, tiled=[(dy2, _rows(TM, D))], full=[w_down_t],
                     outs=[((s_rows, FF), F32, _rows(TM, FF))])
    return res[0]


def swiglu_conv_bwd(u0, da, conv_w8, conv_b, s_rows):
    w2 = 2 * FF
    nt = s_rows // TC

    def body(t, vals, fr):
        u_ref, pv, nx, da_ref = vals
        cw, cb = fr
        u, da_ = u_ref[...], da_ref[...]
        up, dn = _shifted_rows(u, *_halo_rows(t, nt, pv, nx))
        uc = cw[0:1, :] * up + cw[1:2, :] * u + cw[2:3, :] * dn + cb[...]
        gate, val = uc[:, :FF], uc[:, FF:]
        sg = jax.nn.sigmoid(gate)
        dgate = da_ * val * (sg * (1.0 + gate * (1.0 - sg)))
        dval = da_ * (gate * sg)
        du = jnp.concatenate([dgate, dval], axis=1)
        return [du], [_colsum(du), _colsum(up * du), _colsum(u * du), _colsum(dn * du)]

    hp, hn = _halo_specs(TC, w2, s_rows)
    res, accs = rowwise("swiglu_conv_bwd", body, ntiles=nt,
                        tiled=[(u0, _rows(TC, w2)), (u0, hp), (u0, hn), (da, _rows(TC, FF))],
                        full=[conv_w8, conv_b], outs=[((s_rows, w2), F32, _rows(TC, w2))], accs=[(1, w2)] * 4)
    return res[0], accs


def conv_bwd_input(du, conv_w8, s_rows):
    w2 = 2 * FF
    nt = s_rows // TC

    def body(t, vals, fr):
        d_ref, pv, nx = vals
        (cw,) = fr
        d = d_ref[...]
        up, dn = _shifted_rows(d, *_halo_rows(t, nt, pv, nx))
        return [cw[0:1, :] * dn + cw[1:2, :] * d + cw[2:3, :] * up], []

    hp, hn = _halo_specs(TC, w2, s_rows)
    res, _ = rowwise("conv_bwd_input", body, ntiles=nt, tiled=[(du, _rows(TC, w2)), (du, hp), (du, hn)],
                     full=[conv_w8], outs=[((s_rows, w2), MXU_DTYPE, _rows(TC, w2))])
    return res[0]


def ffn_up_ln1_bwd(du0, dr2, xhat1, y, rstd1, modrows, ln_g, ln_b, w_up_t, s_rows):
    def body(t, vals, fr):
        du0_, dr2_, xh, y_, rstd = (v[...] for v in vals)
        mod, g_r, b_r, wt = fr
        dh2 = jnp.dot(du0_, wt[...], preferred_element_type=F32)
        x1 = xh * g_r[...] + b_r[...]
        dx1 = ALPHA * dr2_ + dh2 * (1.0 + _mrow(mod, M_SCALE2))
        dr1 = _layer_norm_bwd(dx1 * g_r[...], xh, rstd)
        incs = [_colsum(dh2 * x1), _colsum(dh2), _colsum(dx1 * xh), _colsum(dx1), _colsum(dr1 * y_)]
        return [dr1 * _mrow(mod, M_GATE1), ALPHA * dr1], incs

    res, accs = rowwise("ffn_up_ln1_bwd", body, ntiles=s_rows // TM,
                        tiled=[(du0, _rows(TM, 2 * FF)), (dr2, _rows(TM, D)), (xhat1, _rows(TM, D)), (y, _rows(TM, D)),
                               (rstd1, _rows(TM, 1))],
                        full=[modrows, ln_g, ln_b, w_up_t],
                        outs=[((s_rows, D), MXU_DTYPE, _rows(TM, D)), ((s_rows, D), F32, _rows(TM, D))],
                        accs=[(1, D)] * 5)
    return res, accs


def merge_bwd(dy, ya, yb, gl, w_out_t, wba_t, wbb_t, s_rows):
    def body(t, vals, fr):
        dy_, ya_, yb_, gl_ = (v[...] for v in vals)
        wot, wat, wbt = fr
        dmrg = jnp.dot(dy_, wot[...], preferred_element_type=F32)
        ga = jax.nn.sigmoid(gl_[:, :D])
        gb = jax.nn.sigmoid(gl_[:, D:])
        dya = dmrg * ga
        dyb = dmrg * gb
        dgl = jnp.concatenate([dmrg * ya_ * ga * (1.0 - ga), dmrg * yb_ * gb * (1.0 - gb)], axis=1)
        return [dya, dyb, dgl, _dot(dya, wat[...]), _dot(dyb, wbt[...])], []

    mx = MXU_DTYPE
    res, _ = rowwise("merge_bwd", body, ntiles=s_rows // TM,
                     tiled=[(dy, _rows(TM, D)), (ya, _rows(TM, D)), (yb, _rows(TM, D)), (gl, _rows(TM, 2 * D))],
                     full=[w_out_t, wba_t, wbb_t],
                     outs=[((s_rows, D), mx, _rows(TM, D))] * 2 + [((s_rows, 2 * D), F32, _rows(TM, 2 * D))] +
                          [((s_rows, Q_W), F32, _rows(TM, Q_W))] * 2)
    return res


def qk_bwd(dqa, dka, dkax, dva, dvax, dqb, dkb, dvb, dgl, tq, rq, tk, rk, cos, sin, qg, kg, bd, n_lat_tiles, n):
    def body(t, vals, fr):
        dqa_, dka_, dkax_, dva_, dvax_, dqb_, dkb_, dvb_, dgl_, tq_, rq_, tk_, rk_, cs, sn = (v[...] for v in vals)
        qg_r, kg_r, bd_r = fr
        is_ctx = t >= n_lat_tiles
        cos4 = jnp.concatenate([cs] * 4, axis=1)
        sin4 = jnp.concatenate([sn] * 4, axis=1)
        zero = jnp.zeros_like(dqa_)
        dpqa = jnp.where(is_ctx, zero, _rope_t(dqa_, cos4, sin4) * Q_SCALE)
        dpka = _rope_t(jnp.where(is_ctx, dkax_, dka_), cos4, sin4)
        dpva = jnp.where(is_ctx, dvax_, dva_)
        dnq = jnp.where(is_ctx, zero, _rope_t(dqb_, cos4, sin4) * Q_SCALE)
        gq = qg_r[...] * dnq
        dtq = rq_ * gq - tq_ * (rq_ * rq_ * rq_) * (_seg_sum64(gq * tq_, bd_r) * (1.0 / HEAD_DIM))
        dnk = _rope_t(dkb_, cos4, sin4)
        gk = kg_r[...] * dnk
        dtk = rk_ * gk - tk_ * (rk_ * rk_ * rk_) * (_seg_sum64(gk * tk_, bd_r) * (1.0 / HEAD_DIM))
        dgl32 = jnp.where(is_ctx, jnp.zeros_like(dgl_), dgl_)
        dproj = jnp.concatenate([dpqa, dpka, dpva, dtq, dtk, dvb_, dgl32], axis=1)
        return [dproj], [_colsum(dproj), _colsum(dnq * tq_ * rq_), _colsum(dnk * tk_ * rk_)]

    lat = lambda t: jnp.minimum(t, n_lat_tiles - 1)
    cx = lambda t: jnp.maximum(t - n_lat_tiles, 0)
    qs = _rows(TM, Q_W)
    res, accs = rowwise(
        "qk_bwd", body, ntiles=n // TM,
        tiled=[(dqa, _rows(TM, Q_W, lat)), (dka, _rows(TM, Q_W, lat)), (dkax, _rows(TM, Q_W, cx)),
               (dva, _rows(TM, Q_W, lat)), (dvax, _rows(TM, Q_W, cx)), (dqb, _rows(TM, Q_W, lat)),
               (dkb, qs), (dvb, qs), (dgl, _rows(TM, 2 * D, lat)), (tq, qs), (rq, qs), (tk, qs), (rk, qs),
               (cos, _rows(TM, LANES)), (sin, _rows(TM, LANES))],
        full=[qg, kg, bd], outs=[((n, EXT_COLS), MXU_DTYPE, _rows(TM, EXT_COLS))],
        accs=[(1, EXT_COLS), (1, Q_W), (1, Q_W)])
    return res[0], accs


def inproj_bwd(name, dproj, xa, dxp, modrows, w_ext_t, *, ntiles, tile_off, is_ctx, out_rows):
    kc = M_SCALEC if is_ctx else M_SCALE1

    def body(t, vals, fr):
        dp, x_ = vals[0][...], vals[1][...]
        mod, wt = fr
        dh = jnp.dot(dp, wt[...], preferred_element_type=F32)
        incs = [_colsum(dh * x_), _colsum(dh)]
        if is_ctx:
            return [], incs
        return [vals[2][...] + dh * (1.0 + _mrow(mod, kc))], incs

    tiled = [(dproj, _rows(TM, EXT_COLS, lambda i: i + tile_off)), (xa, _rows(TM, D, lambda i: i + tile_off))]
    outs = []
    if not is_ctx:
        tiled.append((dxp, _rows(TM, D)))
        outs = [((out_rows, D), F32, _rows(TM, D))]
    return rowwise(name, body, ntiles=ntiles, tiled=tiled, full=[modrows, w_ext_t], outs=outs, accs=[(1, D)] * 2)


def _attn_semantics():
    return _cparams(("arbitrary", "arbitrary", "arbitrary"))


GLOB_TK = (1280, 1024, 768, 512, 256)
KEY_CHUNK = 256


def glob_fwd(q, kt, v_t, s_rows):
    n = kt.shape[0]
    tq = TM
    tk = _pick(n, GLOB_TK)
    nq, nk = s_rows // tq, n // tk
    r = GROUPS * tq
    nch = tk // KEY_CHUNK

    def produce(qs, k_ref, s_buf, c, mx):
        rows = slice(c * KEY_CHUNK, (c + 1) * KEY_CHUNK)
        sn = _dot_nt(k_ref[rows, :], qs[...])
        s_buf[rows, :] = sn
        return jnp.maximum(mx, jnp.max(sn, axis=0, keepdims=True))

    def kern(q_ref, k0_ref, kn_ref, vt_ref, o_ref, lse_ref, qs, s_buf, mx_buf, m_s, l_s, acc):
        j = pl.program_id(2)

        @pl.when(j == 0)
        def _():
            qs[...] = _stack_groups(q_ref[...], qs.dtype)
            mx = jnp.full((1, r), -jnp.inf, F32)
            for c in range(nch):
                mx = produce(qs, k0_ref, s_buf, c, mx)
            mx_buf[...] = mx
            m_s[...] = jnp.full_like(m_s, -jnp.inf)
            l_s[...] = jnp.zeros_like(l_s)
            acc[...] = jnp.zeros_like(acc)

        m_prev = m_s[...]
        m_new = jnp.maximum(m_prev, mx_buf[...])
        alpha = jnp.exp(m_prev - m_new)
        a = alpha * acc[...]
        ls = alpha * l_s[...]
        mx = jnp.full((1, r), -jnp.inf, F32)
        for c in range(nch):
            rows = slice(c * KEY_CHUNK, (c + 1) * KEY_CHUNK)
            p = jnp.exp(s_buf[rows, :] - m_new)
            ls = ls + jnp.sum(p, axis=0, keepdims=True)
            a = a + jnp.dot(vt_ref[0, :, rows], p.astype(MXU_DTYPE), preferred_element_type=F32)
            mx = produce(qs, kn_ref, s_buf, c, mx)
        mx_buf[...] = mx
        l_s[...] = ls
        acc[...] = a
        m_s[...] = m_new

        @pl.when(j == nk - 1)
        def _():
            o_t = acc[...] / l_s[...]
            row = lax.broadcasted_iota(jnp.int32, (HEAD_DIM, KV_W), 0)
            col = lax.broadcasted_iota(jnp.int32, (HEAD_DIM, KV_W), 1)
            o = jnp.zeros((tq, KV_W), F32)
            for g in range(GROUPS):
                sel = jnp.where(col == row + g * HEAD_DIM, 1.0, 0.0).astype(F32)
                o = o + lax.dot_general(o_t[:, g * tq:(g + 1) * tq], sel, (((0,), (0,)), ((), ())),
                                        precision=lax.Precision.HIGHEST, preferred_element_type=F32)
            o_ref[...] = o
            lse_ref[0, 0] = m_s[...] + jnp.log(l_s[...])

    kspec = lambda f: pl.BlockSpec((tk, KV_W), lambda h, i, j: (f(j), h))
    return pl.pallas_call(
        kern, name="glob_fwd", grid=(N_KV, nq, nk),
        in_specs=[pl.BlockSpec((tq, KV_W), lambda h, i, j: (i, h)), kspec(lambda j: 0),
                  kspec(lambda j: jnp.minimum(j + 1, nk - 1)), pl.BlockSpec((1, HEAD_DIM, tk), lambda h, i, j: (h, 0, j))],
        out_specs=[pl.BlockSpec((tq, KV_W), lambda h, i, j: (i, h)),
                   pl.BlockSpec((1, 1, 1, r), lambda h, i, j: (h, i, 0, 0))],
        out_shape=[SDS((s_rows, Q_W), F32), SDS((N_KV, nq, 1, r), F32)],
        scratch_shapes=[pltpu.VMEM((r, KV_W), MXU_DTYPE), pltpu.VMEM((tk, r), F32), pltpu.VMEM((1, r), F32),
                        pltpu.VMEM((1, r), F32), pltpu.VMEM((1, r), F32), pltpu.VMEM((HEAD_DIM, r), F32)],
        compiler_params=_attn_semantics(),
    )(q, kt, kt, v_t)


def attn_delta(o, do, s_rows):
    tq = TM
    nq = s_rows // tq
    r = GROUPS * tq

    def kern(o_ref, do_ref, d_ref):
        d_ref[0, 0] = jnp.sum(_stack_groups(do_ref[...], F32) * _stack_groups(o_ref[...], F32), axis=1, keepdims=True)

    qspec = pl.BlockSpec((tq, KV_W), lambda h, i: (i, h))
    return pl.pallas_call(
        kern, name="attn_delta", grid=(N_KV, nq), in_specs=[qspec, qspec],
        out_specs=pl.BlockSpec((1, 1, r, 1), lambda h, i: (h, i, 0, 0)), out_shape=SDS((N_KV, nq, r, 1), F32),
        compiler_params=_cparams(("parallel", "parallel")),
    )(o, do)


def _compact_t(tt, dtype):
    return jnp.concatenate([tt[g * HEAD_DIM:(g + 1) * HEAD_DIM, :] for g in range(GROUPS)], axis=1).astype(dtype)


def glob_bwd(q, q_t, kt, vt, do, do_t, lse, delta, h, s_rows):
    n = kt.shape[0]
    tq = TM
    tk = _pick(n, GLOB_TK)
    nq, nk = s_rows // tq, n // tk
    r = GROUPS * tq
    nch = tk // KEY_CHUNK

    def kern(q_ref, qt_ref, k_ref, v_ref, do_ref, dot_ref, lse_ref, dl_ref, dq_ref, dkt_ref, dvt_ref, p_buf, ds_buf):
        j = pl.program_id(0)
        i = pl.program_id(1)

        @pl.when(i == 0)
        def _():
            dkt_ref[...] = jnp.zeros_like(dkt_ref)
            dvt_ref[...] = jnp.zeros_like(dvt_ref)

        qs = _stack_groups(q_ref[...], MXU_DTYPE)
        dos = _stack_groups(do_ref[...], MXU_DTYPE)
        lse_b = jnp.broadcast_to(lse_ref[0], (r, LANES))
        dl_b = jnp.broadcast_to(dl_ref[0], (r, LANES))
        for c in range(nch):
            lo = c * KEY_CHUNK
            sc = _dot_nt(qs, k_ref[lo:lo + KEY_CHUNK, :])
            dpc = _dot_nt(dos, v_ref[lo:lo + KEY_CHUNK, :])
            for t in range(KEY_CHUNK // LANES):
                sl = slice(t * LANES, (t + 1) * LANES)
                pt = jnp.exp(sc[:, sl] - lse_b)
                p_buf[:, lo + t * LANES:lo + (t + 1) * LANES] = pt.astype(p_buf.dtype)
                ds_buf[:, lo + t * LANES:lo + (t + 1) * LANES] = (pt * (dpc[:, sl] - dl_b)).astype(ds_buf.dtype)
        dq_t = _fold_groups(jnp.dot(ds_buf[...], k_ref[...], preferred_element_type=F32), tq)
        rows = pl.ds(pl.multiple_of(i * tq, tq), tq)

        @pl.when(j == 0)
        def _():
            dq_ref[rows, :] = dq_t

        @pl.when(j > 0)
        def _():
            dq_ref[rows, :] += dq_t

        dvt_ref[...] += jnp.dot(_compact_t(dot_ref[...], MXU_DTYPE), p_buf[...], preferred_element_type=F32)
        dkt_ref[...] += jnp.dot(_compact_t(qt_ref[...], MXU_DTYPE), ds_buf[...], preferred_element_type=F32)

    col = pl.BlockSpec((1, r, 1), lambda j, i: (i, 0, 0))
    qspec = pl.BlockSpec((tq, KV_W), lambda j, i: (i, h))
    tspec = pl.BlockSpec((KV_W, tq), lambda j, i: (h, i))
    kspec = pl.BlockSpec((tk, KV_W), lambda j, i: (j, h))
    ospec = pl.BlockSpec((HEAD_DIM, tk), lambda j, i: (0, j))
    return pl.pallas_call(
        kern, name=f"glob_bwd_h{h}", grid=(nk, nq),
        in_specs=[qspec, tspec, kspec, kspec, qspec, tspec, col, col],
        out_specs=[pl.BlockSpec(memory_space=pltpu.VMEM), ospec, ospec],
        out_shape=[SDS((s_rows, KV_W), F32), SDS((HEAD_DIM, n), F32), SDS((HEAD_DIM, n), F32)],
        scratch_shapes=[pltpu.VMEM((r, tk), MXU_DTYPE), pltpu.VMEM((r, tk), MXU_DTYPE)],
        compiler_params=_cparams(("arbitrary", "arbitrary")),
    )(q, q_t, kt, vt, do, do_t, lse, delta)


WQ = GROUPS * WIN


def _win_kv_specs(s_rows, c_rows):
    nb = s_rows // WIN
    blk = lambda f: pl.BlockSpec((WIN, KV_W), lambda h, i: (f(i), h))
    return [blk(lambda i: jnp.maximum(i - 1, 0)), blk(lambda i: i), blk(lambda i: jnp.minimum(i + 1, nb - 1)),
            pl.BlockSpec((c_rows, KV_W), lambda h, i: (s_rows // c_rows, h))]


def _win_mask(i, s_rows, shape):
    row = lax.broadcasted_iota(jnp.int32, shape, 0)
    col = lax.broadcasted_iota(jnp.int32, shape, 1)
    qpos = i * WIN + (row & (WIN - 1))
    kpos = (i - 1) * WIN + col
    band = (jnp.abs(qpos - kpos) <= WIN) & (kpos >= 0) & (kpos < s_rows)
    return (col >= 3 * WIN) | band


def _win_cat(dst, parts):
    off = 0
    for p in parts:
        dst[off:off + p.shape[0], :] = p[...]
        off += p.shape[0]


def win_fwd(q, kt, vt, sinkcol, s_rows, c_rows):
    nb = s_rows // WIN
    nkeys = 3 * WIN + c_rows

    def kern(q_ref, kp, kc, kn, kx, vp, vc, vn, vx, sink_ref, o_ref, lse_ref, kcat, vcat):
        i = pl.program_id(1)
        _win_cat(kcat, (kp, kc, kn, kx))
        _win_cat(vcat, (vp, vc, vn, vx))
        qs = _stack_groups(q_ref[...], MXU_DTYPE)
        s = _dot_nt(qs, kcat[...])
        s = jnp.where(_win_mask(i, s_rows, s.shape), s, -jnp.inf)
        sink = sink_ref[0][:, 0:1]
        m = jnp.maximum(jnp.max(s, axis=1, keepdims=True), sink)
        e = jnp.exp(s - m)
        den = jnp.sum(e, axis=1, keepdims=True) + jnp.exp(sink - m)
        o_ref[...] = _fold_groups(_dot(e / den, vcat[...]), WIN)
        lse_ref[0, 0] = m + jnp.log(den)

    kv = _win_kv_specs(s_rows, c_rows)
    qspec = pl.BlockSpec((WIN, KV_W), lambda h, i: (i, h))
    col = pl.BlockSpec((1, 1, WQ, 1), lambda h, i: (h, i, 0, 0))
    return pl.pallas_call(
        kern, name="win_fwd", grid=(N_KV, nb),
        in_specs=[qspec] + kv + kv + [pl.BlockSpec((1, WQ, LANES), lambda h, i: (h, 0, 0))],
        out_specs=[qspec, col], out_shape=[SDS((s_rows, Q_W), F32), SDS((N_KV, nb, WQ, 1), F32)],
        scratch_shapes=[pltpu.VMEM((nkeys, KV_W), MXU_DTYPE), pltpu.VMEM((nkeys, KV_W), MXU_DTYPE)],
        compiler_params=_cparams(("arbitrary", "arbitrary")),
    )(q, kt, kt, kt, kt, vt, vt, vt, vt, sinkcol)


def win_bwd_dq(q, kt, vt, sinkcol, o, do, lse, s_rows, c_rows):
    nb = s_rows // WIN
    nkeys = 3 * WIN + c_rows

    def kern(q_ref, kp, kc, kn, kx, vp, vc, vn, vx, sink_ref, o_ref, do_ref, lse_ref,
             dq_ref, delta_ref, dkx_ref, dvx_ref, dsk_ref, kcat, vcat):
        i = pl.program_id(1)

        @pl.when(i == 0)
        def _():
            dkx_ref[...] = jnp.zeros_like(dkx_ref)
            dvx_ref[...] = jnp.zeros_like(dvx_ref)
            dsk_ref[...] = jnp.zeros_like(dsk_ref)

        _win_cat(kcat, (kp, kc, kn, kx))
        _win_cat(vcat, (vp, vc, vn, vx))
        qs = _stack_groups(q_ref[...], MXU_DTYPE)
        do32 = _stack_groups(do_ref[...], F32)
        delta = jnp.sum(do32 * _stack_groups(o_ref[...], F32), axis=1, keepdims=True)
        dos = do32.astype(MXU_DTYPE)
        lse_c = lse_ref[0, 0]
        s = _dot_nt(qs, kcat[...])
        s = jnp.where(_win_mask(i, s_rows, s.shape), s, -jnp.inf)
        p = jnp.exp(s - lse_c)
        ds = p * (_dot_nt(dos, vcat[...]) - delta)
        dq_ref[...] = _fold_groups(_dot(ds, kcat[...]), WIN)
        delta_ref[0, 0] = delta
        dkx_ref[0] += _dot_tn(ds[:, 3 * WIN:], qs)
        dvx_ref[0] += _dot_tn(p[:, 3 * WIN:], dos)
        dsk_ref[0] += -(jnp.exp(sink_ref[0][:, 0:1] - lse_c) * delta)

    kv = _win_kv_specs(s_rows, c_rows)
    qspec = pl.BlockSpec((WIN, KV_W), lambda h, i: (i, h))
    col = pl.BlockSpec((1, 1, WQ, 1), lambda h, i: (h, i, 0, 0))
    xspec = pl.BlockSpec((1, c_rows, KV_W), lambda h, i: (h, 0, 0))
    return pl.pallas_call(
        kern, name="win_bwd_dq", grid=(N_KV, nb),
        in_specs=[qspec] + kv + kv + [pl.BlockSpec((1, WQ, LANES), lambda h, i: (h, 0, 0)), qspec, qspec, col],
        out_specs=[qspec, col, xspec, xspec, pl.BlockSpec((1, WQ, 1), lambda h, i: (h, 0, 0))],
        out_shape=[SDS((s_rows, Q_W), F32), SDS((N_KV, nb, WQ, 1), F32), SDS((N_KV, c_rows, KV_W), F32),
                   SDS((N_KV, c_rows, KV_W), F32), SDS((N_KV, WQ, 1), F32)],
        scratch_shapes=[pltpu.VMEM((nkeys, KV_W), MXU_DTYPE), pltpu.VMEM((nkeys, KV_W), MXU_DTYPE)],
        compiler_params=_cparams(("arbitrary", "arbitrary")),
    )(q, kt, kt, kt, kt, vt, vt, vt, vt, sinkcol, o, do, lse)


def win_bwd_dkv(q, kt, vt, do, lse, delta, s_rows):
    nb = s_rows // WIN

    def kern(k_ref, v_ref, qp, qc, qn, dop, doc, don, lp, lc, ln, dp_, dc_, dn_, dk_ref, dv_ref):
        j = pl.program_id(1)
        k = k_ref[...]
        v = v_ref[...]
        dk = jnp.zeros((WIN, KV_W), F32)
        dv = jnp.zeros((WIN, KV_W), F32)
        for b, (q_r, do_r, l_r, d_r) in enumerate(((qp, dop, lp, dp_), (qc, doc, lc, dc_), (qn, don, ln, dn_))):
            ib = j - 1 + b
            qs = _stack_groups(q_r[...], MXU_DTYPE)
            dos = _stack_groups(do_r[...], MXU_DTYPE)
            s = _dot_nt(qs, k)
            row = lax.broadcasted_iota(jnp.int32, s.shape, 0)
            col = lax.broadcasted_iota(jnp.int32, s.shape, 1)
            near = jnp.abs(ib * WIN + (row & (WIN - 1)) - (j * WIN + col)) <= WIN
            ok = near & (ib >= 0) & (ib < nb)
            p = jnp.where(ok, jnp.exp(s - l_r[0, 0]), 0.0)
            dv = dv + _dot_tn(p, dos)
            ds = p * (_dot_nt(dos, v) - d_r[0, 0])
            dk = dk + _dot_tn(ds, qs)
        dk_ref[...] = dk
        dv_ref[...] = dv

    fs = (lambda j: jnp.maximum(j - 1, 0), lambda j: j, lambda j: jnp.minimum(j + 1, nb - 1))
    qspecs = [pl.BlockSpec((WIN, KV_W), lambda h, j, f=f: (f(j), h)) for f in fs]
    cols = [pl.BlockSpec((1, 1, WQ, 1), lambda h, j, f=f: (h, f(j), 0, 0)) for f in fs]
    kspec = pl.BlockSpec((WIN, KV_W), lambda h, j: (j, h))
    return pl.pallas_call(
        kern, name="win_bwd_dkv", grid=(N_KV, nb),
        in_specs=[kspec, kspec] + qspecs + qspecs + cols + cols, out_specs=[kspec, kspec],
        out_shape=[SDS((s_rows, Q_W), F32), SDS((s_rows, Q_W), F32)],
        compiler_params=_cparams(("arbitrary", "arbitrary")),
    )(kt, vt, q, q, q, do, do, do, lse, lse, lse, delta, delta, delta)


def adamw(name, w, m, v, grads):
    r, wd = w.shape
    tr = _pick(r, [t for t in ELEMENTWISE_ROWS if t * wd * 4 <= ELEMENTWISE_BLOCK_BYTES])
    stacked = not isinstance(grads, (list, tuple))
    ng = grads.shape[0] if stacked else len(grads)

    def kern(*refs):
        w_ref, m_ref, v_ref = refs[:3]
        g_refs = refs[3:-4]
        g_out, d_out, m_out, v_out = refs[-4:]
        if stacked:
            g = g_refs[0][0]
            for k in range(1, ng):
                g = g + g_refs[0][k]
        else:
            g = g_refs[0][...]
            for gr in g_refs[1:]:
                g = g + gr[...]
        wv = w_ref[...]
        mn = ADAM_B1 * m_ref[...] + (1.0 - ADAM_B1) * g
        vn = ADAM_B2 * v_ref[...] + (1.0 - ADAM_B2) * (g * g)
        m_hat = mn / (1.0 - ADAM_B1 ** ADAM_STEP)
        v_hat = vn / (1.0 - ADAM_B2 ** ADAM_STEP)
        g_out[...] = g
        d_out[...] = -ADAM_LR * (m_hat / (jnp.sqrt(v_hat) + ADAM_EPS) + ADAM_WD * wv)
        m_out[...] = mn
        v_out[...] = vn

    spec = pl.BlockSpec((tr, wd), lambda i: (i, 0))
    gspecs = [pl.BlockSpec((ng, tr, wd), lambda i: (0, i, 0))] if stacked else [spec] * ng
    return pl.pallas_call(
        kern, name=name, grid=(r // tr,), in_specs=[spec] * 3 + gspecs, out_specs=[spec] * 4,
        out_shape=[SDS((r, wd), F32)] * 4, compiler_params=_cparams(("parallel",)),
    )(w, m, v, *([grads] if stacked else grads))


def add2(name, a, b):
    k, r, w = a.shape
    tr = _pick(r, [t for t in ELEMENTWISE_ROWS if t * w * 4 <= ELEMENTWISE_BLOCK_BYTES])

    def kern(a_ref, b_ref, o_ref):
        o_ref[...] = a_ref[...] + b_ref[...]

    spec = pl.BlockSpec((1, tr, w), lambda s, i: (s, i, 0))
    return pl.pallas_call(kern, name=name, grid=(k, r // tr), in_specs=[spec, spec], out_specs=spec,
                          out_shape=SDS(a.shape, a.dtype), compiler_params=_cparams(("parallel", "parallel")))(a, b)


def _rep4(a, off):
    return jnp.concatenate([a[:, off + HEAD_DIM * h: off + HEAD_DIM * (h + 1)] for h in range(N_KV) for _ in range(GROUPS)], axis=1)


def _extend_cols(a):
    return jnp.concatenate([a[:, 0:OFF_KA], _rep4(a, OFF_KA), _rep4(a, OFF_VA), a[:, OFF_QB:OFF_KB],
                            _rep4(a, OFF_KB), _rep4(a, OFF_VB), a[:, OFF_GA:]], axis=1)


def _fold4(a, off):
    r = a.shape[0]
    return a[:, off:off + Q_W].reshape(r, N_KV, GROUPS, HEAD_DIM).sum(axis=2).reshape(r, N_KV * HEAD_DIM)


def _fold_cols(a):
    return jnp.concatenate([a[:, X_QA:X_QA + Q_W], _fold4(a, X_KA), _fold4(a, X_VA), a[:, X_QB:X_QB + Q_W],
                            _fold4(a, X_KB), _fold4(a, X_VB), a[:, X_GL:]], axis=1)


def _rope_tables(s_rows, c_rows):
    pos = jnp.arange(s_rows, dtype=jnp.int32)
    rows = (pos // GRID_W).astype(F32)
    cols = (pos % GRID_W).astype(F32)
    n_freq = HEAD_DIM // 4
    inv_freq = ROPE_THETA ** (-jnp.arange(n_freq, dtype=F32) / n_freq)
    ang_r = rows[:, None] * inv_freq
    ang_c = cols[:, None] * inv_freq
    cos = jnp.concatenate([jnp.cos(ang_r)] * 2 + [jnp.cos(ang_c)] * 2, axis=1)
    sin = jnp.concatenate([-jnp.sin(ang_r), jnp.sin(ang_r), -jnp.sin(ang_c), jnp.sin(ang_c)], axis=1)
    cos = jnp.concatenate([cos, jnp.ones((c_rows, HEAD_DIM), F32)], axis=0)
    sin = jnp.concatenate([sin, jnp.zeros((c_rows, HEAD_DIM), F32)], axis=0)
    return jnp.concatenate([cos, cos], axis=1), jnp.concatenate([sin, sin], axis=1)


def _ff_pad_cols(a):
    r = a.shape[0]
    a = jnp.pad(a.reshape(r, N_DEV, FF_SHARD), ((0, 0), (0, 0), (0, FF_SHARD_PAD - FF_SHARD)))
    return a.reshape(r, 2 * FF)


def _ff_unpad_cols(a):
    r = a.shape[0]
    return a.reshape(r, N_DEV, FF_SHARD_PAD)[:, :, :FF_SHARD].reshape(r, 2 * D_FF)


def _ff_pad_rows(a):
    c = a.shape[1]
    a = jnp.pad(a.reshape(N_DEV // 2, FF_SHARD, c), ((0, 0), (0, FF_SHARD_PAD - FF_SHARD), (0, 0)))
    return a.reshape(FF, c)


def _ff_unpad_rows(a):
    c = a.shape[1]
    return a.reshape(N_DEV // 2, FF_SHARD_PAD, c)[:, :FF_SHARD].reshape(D_FF, c)


BIG = (("w_in", (D, IN_COLS // N_DEV)), ("w_branch_a", (Q_W, D // N_DEV)), ("w_branch_b", (Q_W, D // N_DEV)),
       ("w_out", (D // N_DEV, D)), ("w_up", (D, FF_SHARD_PAD)), ("w_down", (D_FF // N_DEV, D)))
BIG_SIZES = tuple(int(np.prod(s)) for _, s in BIG)
BIG_ROWS = sum(BIG_SIZES) // LANES


def _pack_big(parts):
    return jnp.concatenate([p.reshape(-1) for p in parts]).reshape(BIG_ROWS, LANES)


def _unpack_big(flat):
    lead = flat.shape[:-2]
    f = flat.reshape(*lead, BIG_ROWS * LANES)
    out, off = [], 0
    for (_, shp), sz in zip(BIG, BIG_SIZES, strict=True):
        out.append(f[..., off:off + sz].reshape(*lead, *shp))
        off += sz
    return out


def _cols_to_full(g):
    return jnp.transpose(g, (1, 0, 2)).reshape(g.shape[1], -1)


def _full_to_cols(a):
    r, c = a.shape
    return jnp.transpose(a.reshape(r, N_DEV, c // N_DEV), (1, 0, 2))


SMALL = (("c_ctx", D), ("b_mod", N_MOD * D), ("b_in", IN_COLS), ("attn_sink", N_HEADS), ("q_norm_g", HEAD_DIM),
         ("k_norm_g", HEAD_DIM), ("ln1_g", D), ("ln1_b", D), ("conv_w", 3 * 2 * D_FF // N_DEV), ("conv_b", 2 * D_FF),
         ("ln2_g", D), ("ln2_b", D))
SMALL_TOTAL = sum(n for _, n in SMALL)
SMALL_ROWS = -(-SMALL_TOTAL // (8 * LANES)) * 8


def _pack_small(parts):
    flat = jnp.concatenate([p.reshape(-1).astype(F32) for p in parts])
    return jnp.pad(flat, (0, SMALL_ROWS * LANES - flat.shape[0])).reshape(SMALL_ROWS, LANES)


def _unpack_small(packed):
    f = packed.reshape(-1)
    out, off = {}, 0
    for name, n in SMALL:
        out[name] = f[off:off + n]
        off += n
    return out


RED = (("c_ctx", D), ("b_in", IN_COLS), ("attn_sink", N_HEADS), ("q_norm_g", HEAD_DIM), ("k_norm_g", HEAD_DIM),
       ("ln1_g", D), ("ln1_b", D), ("conv_w", 3 * 2 * FF), ("conv_b", 2 * FF), ("ln2_g", D), ("ln2_b", D))
RED_TOTAL = sum(n for _, n in RED)
RED_ROWS = -(-RED_TOTAL // (8 * LANES)) * 8


def sum8(name, g):
    _, r, w = g.shape

    def kern(g_ref, o_ref):
        acc = g_ref[0]
        for k in range(1, N_DEV):
            acc = acc + g_ref[k]
        o_ref[...] = acc

    return pl.pallas_call(kern, name=name, out_shape=SDS((r, w), F32))(g)


def _local_step(x, ctx, target, modrows, weights, small):
    s_rows, c_rows = x.shape[0], ctx.shape[0]
    n = s_rows + c_rows
    nl = s_rows // TM
    w_in, wba, wbb, w_out, w_up, w_down = weights
    f = lambda a: a.reshape(1, -1).astype(F32)
    b_in, ln1_g, ln1_b, ln2_g, ln2_b, conv_b = (f(small[k]) for k in ("b_in", "ln1_g", "ln1_b", "ln2_g", "ln2_b", "conv_b"))
    conv_w8 = jnp.pad(small["conv_w_full"], ((0, 5), (0, 0)))
    qg = jnp.tile(small["q_norm_g"].reshape(1, HEAD_DIM), (1, N_HEADS))
    kg = jnp.tile(small["k_norm_g"].reshape(1, HEAD_DIM), (1, N_HEADS))
    sinkcol = jnp.broadcast_to(jnp.repeat(small["attn_sink"].reshape(N_KV, GROUPS), WIN, axis=1)[:, :, None], (N_KV, WQ, LANES))
    bd = jnp.kron(jnp.eye(N_HEADS, dtype=F32), jnp.ones((HEAD_DIM, HEAD_DIM), F32)).astype(BF16)
    cos, sin = _rope_tables(s_rows, c_rows)
    w_ext = _extend_cols(w_in)
    b_ext = _extend_cols(b_in)
    xa = jnp.concatenate([x, ctx], axis=0)

    hb, qa, kat, vat, qb, kbt, vbt, tq, rq, tk, rk, gl = inproj_fwd(xa, cos, sin, modrows, w_ext, b_ext, qg, kg, bd, nl)
    oa, lse_a = win_fwd(qa, kat, vat, sinkcol, s_rows, c_rows)
    vb_t = jnp.stack([vbt[:, h * KV_W:h * KV_W + HEAD_DIM].T for h in range(N_KV)])
    ob, lse_b = glob_fwd(qb, kbt, vb_t, s_rows)
    lse_b = lse_b.reshape(N_KV, s_rows // TM, GROUPS * TM, 1)
    ya, yb, mrg, y, xhat1, rstd1 = merge_fwd(oa, ob, gl, x, modrows, wba, wbb, w_out, s_rows)
    h2, u0 = ffn_up_fwd(xhat1, modrows, ln1_g, ln1_b, w_up, s_rows)
    a = conv_swiglu_fwd(u0, conv_w8, conv_b, s_rows)
    (dr2, dy2), (loss, dln2_g, dln2_b, dgate2) = ffn_down_loss(a, xhat1, target, modrows, ln1_g, ln1_b, ln2_g, ln2_b, w_down, s_rows)

    da = ffn_down_bwd(dy2, w_down.T, s_rows)
    dw_down = mm_tn("dw_down", a, dy2, s_rows)
    du, (dconv_b, dcw0, dcw1, dcw2) = swiglu_conv_bwd(u0, da, conv_w8, conv_b, s_rows)
    du0 = conv_bwd_input(du, conv_w8, s_rows)
    dw_up = mm_tn("dw_up", h2, du0, s_rows)
    (dy, dxp), (dscale2, dshift2, dln1_g, dln1_b, dgate1) = ffn_up_ln1_bwd(du0, dr2, xhat1, y, rstd1, modrows, ln1_g, ln1_b, w_up.T, s_rows)
    dya, dyb, dgl, doa, dob = merge_bwd(dy, ya, yb, gl, w_out.T, wba.T, wbb.T, s_rows)
    dw_out = mm_tn("dw_out", mrg, dy, s_rows)
    dwba = mm_tn("dw_branch_a", oa, dya, s_rows)
    dwbb = mm_tn("dw_branch_b", ob, dyb, s_rows)

    dqa, delta_a, dkax, dvax, dsk = win_bwd_dq(qa, kat, vat, sinkcol, oa, doa, lse_a, s_rows, c_rows)
    dka, dva = win_bwd_dkv(qa, kat, vat, doa, lse_a, delta_a, s_rows)
    delta_b = attn_delta(ob, dob, s_rows)
    qb_t = qb[:s_rows].T
    dob_t = dob.astype(MXU_DTYPE).T
    heads = [glob_bwd(qb, qb_t, kbt, vbt, dob, dob_t, lse_b[h], delta_b[h], h, s_rows) for h in range(N_KV)]
    dqb = jnp.concatenate([hd[0] for hd in heads], axis=1)
    pad = jnp.zeros((n, KV_W - HEAD_DIM), F32)
    dkb = jnp.concatenate([t for hd in heads for t in (hd[1].T, pad)], axis=1)
    dvb = jnp.concatenate([t for hd in heads for t in (hd[2].T, pad)], axis=1)
    ctx_cols = lambda t: jnp.transpose(t, (1, 0, 2)).reshape(c_rows, Q_W)
    dproj, (db_ext, dqg, dkg) = qk_bwd(dqa, dka, ctx_cols(dkax), dva, ctx_cols(dvax), dqb, dkb, dvb, dgl, tq, rq, tk, rk,
                                       cos, sin, qg, kg, bd, nl, n)
    w_ext_t = w_ext.T
    (grad_x,), (dscale1, dshift1) = inproj_bwd("inproj_bwd", dproj, xa, dxp, modrows, w_ext_t, ntiles=nl, tile_off=0,
                                               is_ctx=False, out_rows=s_rows)
    _, (dscale_c, dshift_c) = inproj_bwd("inproj_bwd_ctx", dproj, xa, None, modrows, w_ext_t, ntiles=c_rows // TM,
                                         tile_off=nl, is_ctx=True, out_rows=0)
    dw_in = _fold_cols(mm_tn("dw_in", hb, dproj, n))

    dmod = jnp.concatenate([dshift1, dscale1, dgate1, dshift2, dscale2, dgate2], axis=1)
    dmod_c = jnp.concatenate([dshift_c, dscale_c, jnp.zeros((1, (N_MOD - 2) * D), F32)], axis=1)
    fold_g = lambda t: t.reshape(N_HEADS, HEAD_DIM).sum(axis=0)
    red = {
        "b_in": _fold_cols(db_ext), "attn_sink": dsk.reshape(N_HEADS, WIN).sum(axis=1), "q_norm_g": fold_g(dqg),
        "k_norm_g": fold_g(dkg), "ln1_g": dln1_g, "ln1_b": dln1_b, "conv_w": jnp.concatenate([dcw0, dcw1, dcw2], axis=0),
        "conv_b": dconv_b, "ln2_g": dln2_g, "ln2_b": dln2_b,
    }
    return loss[0, 0], grad_x, (dw_in, dwba, dwbb, dw_out, dw_up, dw_down), dmod, dmod_c, red


def kernel(x, c, ctx, c_ctx, w_mod, b_mod, w_in, b_in, attn_sink, q_norm_g, k_norm_g, w_branch_a, w_branch_b, w_out, ln1_g, ln1_b, w_up, conv_w, conv_b, w_down, ln2_g, ln2_b, loss_target, m_c_ctx, m_w_mod, m_b_mod, m_w_in, m_b_in, m_attn_sink, m_q_norm_g, m_k_norm_g, m_w_branch_a, m_w_branch_b, m_w_out, m_ln1_g, m_ln1_b, m_w_up, m_conv_w, m_conv_b, m_w_down, m_ln2_g, m_ln2_b, v_c_ctx, v_w_mod, v_b_mod, v_w_in, v_b_in, v_attn_sink, v_q_norm_g, v_k_norm_g, v_w_branch_a, v_w_branch_b, v_w_out, v_ln1_g, v_ln1_b, v_w_up, v_conv_w, v_conv_b, v_w_down, v_ln2_g, v_ln2_b):
    ax, ay, ac = (lax.axis_index(a) for a in AXES)
    me = 4 * ax + 2 * ay + ac
    chip = 2 * ax + ay
    mod_w = N_MOD * D // N_DEV
    params = dict(c_ctx=c_ctx, w_mod=w_mod, b_mod=b_mod, w_in=w_in, b_in=b_in, attn_sink=attn_sink, q_norm_g=q_norm_g,
                  k_norm_g=k_norm_g, w_branch_a=w_branch_a, w_branch_b=w_branch_b, w_out=w_out, ln1_g=ln1_g, ln1_b=ln1_b,
                  w_up=w_up, conv_w=conv_w, conv_b=conv_b, w_down=w_down, ln2_g=ln2_g, ln2_b=ln2_b)
    mom_m = dict(c_ctx=m_c_ctx, w_mod=m_w_mod, b_mod=m_b_mod, w_in=m_w_in, b_in=m_b_in, attn_sink=m_attn_sink,
                 q_norm_g=m_q_norm_g, k_norm_g=m_k_norm_g, w_branch_a=m_w_branch_a, w_branch_b=m_w_branch_b, w_out=m_w_out,
                 ln1_g=m_ln1_g, ln1_b=m_ln1_b, w_up=m_w_up, conv_w=m_conv_w, conv_b=m_conv_b, w_down=m_w_down,
                 ln2_g=m_ln2_g, ln2_b=m_ln2_b)
    mom_v = dict(c_ctx=v_c_ctx, w_mod=v_w_mod, b_mod=v_b_mod, w_in=v_w_in, b_in=v_b_in, attn_sink=v_attn_sink,
                 q_norm_g=v_q_norm_g, k_norm_g=v_k_norm_g, w_branch_a=v_w_branch_a, w_branch_b=v_w_branch_b, w_out=v_w_out,
                 ln1_g=v_ln1_g, ln1_b=v_ln1_b, w_up=v_w_up, conv_w=v_conv_w, conv_b=v_conv_b, w_down=v_w_down,
                 ln2_g=v_ln2_g, ln2_b=v_ln2_b)
    big_names = [nm for nm, _ in BIG]

    def shard(tree, nm):
        t = tree[nm][0]
        return jnp.pad(t, ((0, 0), (0, FF_SHARD_PAD - FF_SHARD))) if nm == "w_up" else t

    wg = all_gather("ag_weights", _pack_big([shard(params, nm).astype(MXU_DTYPE) for nm in big_names]))
    g_in, g_ba, g_bb, g_out, g_up, g_down = _unpack_big(wg)
    weights = (_cols_to_full(g_in), _cols_to_full(g_ba), _cols_to_full(g_bb), g_out.reshape(D, D), _cols_to_full(g_up),
               _ff_pad_rows(g_down.reshape(D_FF, D)))

    c_all = all_gather("ag_c", c.reshape(8, LANES)).reshape(N_DEV, D)
    cs = jnp.concatenate([c_all, c_ctx.reshape(1, D), jnp.zeros((7, D), F32)], axis=0)
    w_mod_sh = w_mod[0]
    b_mod_sh = lax.dynamic_slice(b_mod, (0, me * mod_w), (1, mod_w))
    mod_part = mod_fwd(cs, w_mod_sh, b_mod_sh)
    mg = all_gather("ag_mod", mod_part.reshape(16 * mod_w // LANES, LANES)).reshape(N_DEV, 16, mod_w)
    mod = lax.dynamic_index_in_dim(mg, me, axis=1, keepdims=False).reshape(N_MOD, D)
    mod_c = mg[:, 8, :].reshape(N_MOD, D)
    modrows = jnp.stack([mod[0], mod[1], mod_c[0], mod_c[1], mod[2], mod[3], mod[4], mod[5]], axis=0)

    conv_w_full = all_gather("ag_conv_w", jnp.pad(conv_w[0], ((0, 5), (0, FF_SHARD_PAD - FF_SHARD))))
    conv_w_full = _cols_to_full(conv_w_full[:, :3, :])
    small = dict(b_in=b_in, ln1_g=ln1_g, ln1_b=ln1_b, ln2_g=ln2_g, ln2_b=ln2_b, conv_b=_ff_pad_cols(conv_b),
                 conv_w_full=conv_w_full, q_norm_g=q_norm_g, k_norm_g=k_norm_g, attn_sink=attn_sink)
    loss, grad_x, big_grads, dmod, dmod_c, red = _local_step(x[0], ctx[0], loss_target[0], modrows, weights, small)
    loss = lax.psum(loss, AXES)

    dm = all_gather("ag_dmod", jnp.concatenate([dmod, dmod_c], axis=0).reshape(2 * N_MOD * D // LANES, LANES))
    dm = dm.reshape(N_DEV, 2, N_MOD * D)
    dm_all = jnp.concatenate([dm[:, 0], dm[:, 1]], axis=0)
    dm_sh = lax.dynamic_slice(dm_all, (0, me * mod_w), (16, mod_w))
    dw_mod, dcc, db_mod = mod_bwd(cs, w_mod_sh, dm_sh, dm_all)
    red["c_ctx"] = dcc[8]

    red_vec = jnp.concatenate([red[nm].reshape(-1) for nm, _ in RED])
    red_vec = jnp.pad(red_vec, (0, RED_ROWS * LANES - RED_TOTAL)).reshape(RED_ROWS, LANES)
    red_sum = sum8("sum_small", all_gather("ag_small", red_vec)).reshape(-1)
    gsm, off = {}, 0
    for nm, k in RED:
        gsm[nm] = red_sum[off:off + k]
        off += k
    gsm["b_mod"] = db_mod.reshape(-1)
    gsm["conv_b"] = _ff_unpad_cols(gsm["conv_b"].reshape(1, 2 * FF))
    gsm["conv_w"] = lax.dynamic_slice(gsm["conv_w"].reshape(3, 2 * FF), (0, me * FF_SHARD_PAD), (3, FF_SHARD_PAD))[:, :FF_SHARD]
    sm_names = [nm for nm, _ in SMALL]
    gs, ds, ms, vs = adamw("adamw_small", _pack_small([params[nm] for nm in sm_names]),
                           _pack_small([mom_m[nm] for nm in sm_names]), _pack_small([mom_v[nm] for nm in sm_names]),
                           [_pack_small([gsm[nm] for nm in sm_names])])
    sm_out = [_unpack_small(t) for t in (gs, ds, ms, vs)]

    dw_in, dwba, dwbb, dw_out, dw_up, dw_down = big_grads
    slabs = jnp.concatenate([t.reshape(N_DEV, -1) for t in (
        _full_to_cols(dw_in), _full_to_cols(dwba), _full_to_cols(dwbb), dw_out, _full_to_cols(dw_up),
        _ff_unpad_rows(dw_down))], axis=1)
    by_core = slabs.reshape(4, 2, BIG_ROWS, LANES)
    keep = lax.dynamic_index_in_dim(by_core, ac, axis=1, keepdims=False)
    give = lax.dynamic_index_in_dim(by_core, 1 - ac, axis=1, keepdims=False)
    got = exchange("rs_sibling", give.reshape(1, 4 * BIG_ROWS, LANES), to_chips=False).reshape(4, BIG_ROWS, LANES)
    pair = add2("rs_pair_sum", keep, got)
    outbox = jnp.stack([lax.dynamic_index_in_dim(pair, jnp.bitwise_xor(chip, m), axis=0, keepdims=False) for m in (1, 2, 3)])
    inbox = exchange("rs_chips", outbox, to_chips=True)
    mine = lax.dynamic_index_in_dim(pair, chip, axis=0, keepdims=False)
    gb, db, mb, vb = adamw("adamw_big", _pack_big([shard(params, nm) for nm in big_names]),
                           _pack_big([shard(mom_m, nm) for nm in big_names]), _pack_big([shard(mom_v, nm) for nm in big_names]),
                           [mine, inbox[0], inbox[1], inbox[2]])
    big_out = [dict(zip(big_names, _unpack_big(t), strict=True)) for t in (gb, db, mb, vb)]
    for out in big_out:
        out["w_up"] = out["w_up"][:, :FF_SHARD]
    gm, dmo, mmo, vmo = adamw("adamw_mod", w_mod[0], m_w_mod[0], v_w_mod[0], [dw_mod])
    mod_out = (gm, dmo, mmo, vmo)

    order = ["c_ctx", "w_mod", "b_mod", "w_in", "b_in", "attn_sink", "q_norm_g", "k_norm_g", "w_branch_a", "w_branch_b",
             "w_out", "ln1_g", "ln1_b", "w_up", "conv_w", "conv_b", "w_down", "ln2_g", "ln2_b"]
    results = [loss, grad_x[None]]
    for kind in range(4):
        for nm in order:
            if nm == "w_mod":
                val = mod_out[kind]
            elif nm in big_out[kind]:
                val = big_out[kind][nm]
            else:
                val = sm_out[kind][nm]
            results.append(val.reshape(params[nm].shape))
    return tuple(results)
```

```python
import functools

import jax
import jax.numpy as jnp
import numpy as np
from jax import lax
from jax.experimental import pallas as pl
from jax.experimental.pallas import tpu as pltpu

F32 = jnp.float32
BF16 = jnp.bfloat16
MXU_DTYPE = BF16

AXES = ("x", "y", "c")
N_DEV = 8
D = 1024
HEAD_DIM = 64
N_HEADS = 8
N_KV = 2
GROUPS = 4
KV_W = GROUPS * HEAD_DIM
Q_W = N_HEADS * HEAD_DIM
GRID_W = 64
WIN = 128
ROPE_THETA = 10000.0
D_FF = 2816
FF_SHARD = 2 * D_FF // N_DEV
FF_SHARD_PAD = 768
FF = N_DEV // 2 * FF_SHARD_PAD
LN_EPS = 1e-5
QK_EPS = 1e-6
N_MOD = 6
ALPHA = 2.0 ** 0.25
Q_SCALE = HEAD_DIM ** -0.5
IN_COLS = 3584
OFF_KA, OFF_VA, OFF_QB, OFF_KB, OFF_VB, OFF_GA = 512, 640, 768, 1280, 1408, 1536
EXT_COLS = 6 * Q_W + 2 * D
X_QA, X_KA, X_VA, X_QB, X_KB, X_VB, X_GL = 0, 512, 1024, 1536, 2048, 2560, 3072
ADAM_LR, ADAM_B1, ADAM_B2, ADAM_EPS, ADAM_WD, ADAM_STEP = 0.001, 0.9, 0.999, 1e-08, 0.01, 10
LANES = 128
TM = 256
VMEM_LIMIT = 56 * 1024 * 1024
ELEMENTWISE_BLOCK_BYTES = 1 << 20
ELEMENTWISE_ROWS = (1824, 1408, 1024, 512, 256, 128, 64, 32, 16, 8)

ANY = pl.BlockSpec(memory_space=pl.ANY)
SDS = jax.ShapeDtypeStruct


def _pick(n, candidates):
    for t in candidates:
        if n % t == 0:
            return t
    raise ValueError(f"no tile for {n}")


def _full(a):
    nd = a.ndim
    return pl.BlockSpec(a.shape, lambda *_: (0,) * nd)


def _rows(tm, w, fn=lambda t: t):
    return pl.BlockSpec((tm, w), lambda i: (fn(i), 0))


def _dot(a, b):
    return jnp.dot(a.astype(MXU_DTYPE), b.astype(MXU_DTYPE), preferred_element_type=F32)


def _dot_nt(a, b):
    return lax.dot_general(a.astype(MXU_DTYPE), b.astype(MXU_DTYPE), (((1,), (1,)), ((), ())), preferred_element_type=F32)


def _dot_tn(a, b):
    return lax.dot_general(a.astype(MXU_DTYPE), b.astype(MXU_DTYPE), (((0,), (0,)), ((), ())), preferred_element_type=F32)


def _cparams(sem):
    return pltpu.CompilerParams(dimension_semantics=sem, vmem_limit_bytes=VMEM_LIMIT)


def all_gather(name, v):
    r, w = v.shape

    def body(x_ref, out_ref, send_sems, recv_sems, local_sem):
        x, y, c = (lax.axis_index(a) for a in AXES)
        me, sibling = (x, y, c), (x, y, 1 - c)
        chips = [(1 - x, y), (x, 1 - y), (1 - x, 1 - y)]

        def rows(px, py, pc):
            return out_ref.at[4 * px + 2 * py + pc]

        def copy(k, block, to, src=None):
            return pltpu.make_async_remote_copy(
                src_ref=rows(*block) if src is None else src, dst_ref=rows(*block),
                send_sem=send_sems.at[k], recv_sem=recv_sems.at[k],
                device_id=to, device_id_type=pl.DeviceIdType.MESH)

        mine = pltpu.make_async_copy(x_ref, rows(*me), local_sem)
        mine.start()
        first = [copy(0, me, sibling, src=x_ref)]
        first += [copy(1 + j, me, (*chip, c), src=x_ref) for j, chip in enumerate(chips)]
        for cp in first:
            cp.start()
        passed = [copy(4 + j, (*chip, c), sibling) for j, chip in enumerate(chips)]
        for j, chip in enumerate(chips):
            copy(1 + j, (*chip, c), me).wait_recv()
            passed[j].start()
        copy(0, sibling, me).wait_recv()
        for j, chip in enumerate(chips):
            copy(4 + j, (*chip, 1 - c), me).wait_recv()
        for cp in first + passed:
            cp.wait_send()
        mine.wait()

    return pl.pallas_call(
        body, name=name, out_shape=SDS((N_DEV, r, w), v.dtype), in_specs=[ANY], out_specs=ANY,
        scratch_shapes=[pltpu.SemaphoreType.DMA((7,)), pltpu.SemaphoreType.DMA((7,)), pltpu.SemaphoreType.DMA],
    )(v)


def exchange(name, outbox, to_chips):
    k = outbox.shape[0]
    assert k == (3 if to_chips else 1)

    def body(out_ref, in_ref, send_sems, recv_sems):
        x, y, c = (lax.axis_index(a) for a in AXES)
        peers = [(x, 1 - y, c), (1 - x, y, c), (1 - x, 1 - y, c)] if to_chips else [(x, y, 1 - c)]
        copies = [
            pltpu.make_async_remote_copy(
                src_ref=out_ref.at[m], dst_ref=in_ref.at[m], send_sem=send_sems.at[m], recv_sem=recv_sems.at[m],
                device_id=peer, device_id_type=pl.DeviceIdType.MESH)
            for m, peer in enumerate(peers)
        ]
        for cp in copies:
            cp.start()
        for cp in copies:
            cp.wait_recv()
        for cp in copies:
            cp.wait_send()

    return pl.pallas_call(
        body, name=name, out_shape=SDS(outbox.shape, outbox.dtype), in_specs=[ANY], out_specs=ANY,
        scratch_shapes=[pltpu.SemaphoreType.DMA((k,)), pltpu.SemaphoreType.DMA((k,))],
    )(outbox)


def rowwise(name, body, *, ntiles, tile_off=0, tiled, full, outs, accs=()):
    nt, nf, no = len(tiled), len(full), len(outs)

    def kern(*refs):
        i = pl.program_id(0)
        out_vals, incs = body(i + tile_off, refs[:nt], refs[nt:nt + nf])
        for r, v in zip(refs[nt + nf:nt + nf + no], out_vals, strict=True):
            r[...] = v.astype(r.dtype)
        acc_refs = refs[nt + nf + no:]

        @pl.when(i == 0)
        def _():
            for r in acc_refs:
                r[...] = jnp.zeros_like(r)

        for r, v in zip(acc_refs, incs, strict=True):
            r[...] += v

    res = pl.pallas_call(
        kern, name=name, grid=(ntiles,),
        in_specs=[s for _, s in tiled] + [_full(a) for a in full],
        out_specs=[s for _, _, s in outs] + [pl.BlockSpec(s, lambda i, n=len(s): (0,) * n) for s in accs],
        out_shape=[SDS(s, d) for s, d, _ in outs] + [SDS(s, F32) for s in accs],
        compiler_params=_cparams(("arbitrary",) if accs else ("parallel",)),
    )(*[a for a, _ in tiled], *full)
    return res[:no], res[no:]


def mm_tn(name, a, b, rows):
    ka, nb = a.shape[1], b.shape[1]
    tr = _pick(rows, (1280, 1024, 768, 512, 256))
    tn = _pick(nb, (512, 256, 128))

    def kern(a_ref, b_ref, o_ref):
        @pl.when(pl.program_id(1) == 0)
        def _():
            o_ref[...] = jnp.zeros_like(o_ref)

        o_ref[...] += _dot_tn(a_ref[...], b_ref[...])

    return pl.pallas_call(
        kern, name=name, grid=(nb // tn, rows // tr),
        in_specs=[pl.BlockSpec((tr, ka), lambda n, r: (r, 0)), pl.BlockSpec((tr, tn), lambda n, r: (r, n))],
        out_specs=pl.BlockSpec((ka, tn), lambda n, r: (0, n)), out_shape=SDS((ka, nb), F32),
        compiler_params=_cparams(("parallel", "arbitrary")),
    )(a, b)


def _swap16(t):
    w = t.shape[1]
    lane = lax.broadcasted_iota(jnp.int32, t.shape, 1)
    return jnp.where((lane & 16) == 0, pltpu.roll(t, w - 16, 1), pltpu.roll(t, 16, 1))


def _rope(t, cos, sin):
    return t * cos + _swap16(t) * sin


def _rope_t(d, cos, sin):
    return d * cos - _swap16(d) * sin


def _seg_sum64(a, bd_ref):
    bd = bd_ref[...]
    hi = a.astype(BF16)
    lo = (a - hi.astype(F32)).astype(BF16)
    return jnp.dot(hi, bd, preferred_element_type=F32) + jnp.dot(lo, bd, preferred_element_type=F32)


def _lane_block(shape):
    return jnp.right_shift(lax.broadcasted_iota(jnp.int32, shape, 1), 6)


def _stack_groups(t, dtype):
    blk = _lane_block(t.shape)
    return jnp.concatenate([jnp.where(blk == g, t, jnp.zeros_like(t)).astype(dtype) for g in range(GROUPS)], axis=0)


def _fold_groups(ts, tq):
    blk = _lane_block((tq, KV_W))
    out = jnp.zeros((tq, KV_W), ts.dtype)
    for g in range(GROUPS):
        out = jnp.where(blk == g, ts[g * tq:(g + 1) * tq], out)
    return out


def _layer_norm_bwd(dxh, xhat, rstd):
    m1 = jnp.mean(dxh, axis=1, keepdims=True)
    m2 = jnp.mean(dxh * xhat, axis=1, keepdims=True)
    return rstd * (dxh - m1 - xhat * m2)


def _colsum(a):
    return jnp.sum(a, axis=0, keepdims=True)


def _shifted_rows(t, prev_row, next_row):
    n = t.shape[0]
    row = lax.broadcasted_iota(jnp.int32, t.shape, 0)
    up = jnp.where(row == 0, prev_row, pltpu.roll(t, 1, 0))
    dn = jnp.where(row == n - 1, next_row, pltpu.roll(t, n - 1, 0))
    return up, dn


def mod_fwd(cs, w_sh, b_sh):
    def kern(c_ref, w_ref, b_ref, o_ref):
        o_ref[...] = _dot(jax.nn.silu(c_ref[...]), w_ref[...]) + b_ref[...]

    return pl.pallas_call(kern, name="mod_fwd", out_shape=SDS((16, w_sh.shape[1]), F32),
                          compiler_params=pltpu.CompilerParams(vmem_limit_bytes=VMEM_LIMIT))(cs, w_sh, b_sh)


def mod_bwd(cs, w_sh, dm_sh, dm_all):
    hp = lax.Precision.HIGHEST

    def kern(c_ref, w_ref, dm_ref, da_ref, dw_ref, dc_ref, db_ref):
        c = c_ref[...]
        sg = jax.nn.sigmoid(c)
        sc = c * sg
        dm = dm_ref[...]
        dmc = dm_ref[8:9, :]
        for i in range(9, 16):
            dmc = dmc + dm_ref[i:i + 1, :]
        row = lax.broadcasted_iota(jnp.int32, dm.shape, 0)
        a = jnp.where(row < 8, dm, jnp.where(row == 8, dmc, 0.0))
        dw_ref[...] = lax.dot_general(sc, a, (((0,), (0,)), ((), ())), precision=hp, preferred_element_type=F32)
        dsc = lax.dot_general(a, w_ref[...], (((1,), (1,)), ((), ())), precision=hp, preferred_element_type=F32)
        dc_ref[...] = dsc * (sg * (1.0 + c * (1.0 - sg)))
        db = da_ref[0:1, :]
        for i in range(1, 16):
            db = db + da_ref[i:i + 1, :]
        db_ref[...] = db

    return pl.pallas_call(
        kern, name="mod_bwd",
        out_shape=[SDS(w_sh.shape, F32), SDS((16, D), F32), SDS((1, dm_all.shape[1]), F32)],
        compiler_params=pltpu.CompilerParams(vmem_limit_bytes=VMEM_LIMIT))(cs, w_sh, dm_sh, dm_all)


M_SHIFT1, M_SCALE1, M_SHIFTC, M_SCALEC, M_GATE1, M_SHIFT2, M_SCALE2, M_GATE2 = range(8)


def _mrow(ref, k):
    return ref[k:k + 1, :]


def inproj_fwd(xa, cos, sin, modrows, w_ext, b_ext, qg, kg, bd, n_lat_tiles):
    n = xa.shape[0]

    def body(t, vals, fr):
        x, cs, sn = (v[...] for v in vals)
        mod, w, b, qg_r, kg_r, bd_r = fr
        is_ctx = t >= n_lat_tiles
        shift = jnp.where(is_ctx, _mrow(mod, M_SHIFTC), _mrow(mod, M_SHIFT1))
        scale = jnp.where(is_ctx, _mrow(mod, M_SCALEC), _mrow(mod, M_SCALE1))
        hb = (x * (1.0 + scale) + shift).astype(MXU_DTYPE)
        proj = jnp.dot(hb, w[...], preferred_element_type=F32) + b[...]
        cos4 = jnp.concatenate([cs] * 4, axis=1)
        sin4 = jnp.concatenate([sn] * 4, axis=1)
        qa = _rope(proj[:, X_QA:X_QA + Q_W], cos4, sin4) * Q_SCALE
        ka = _rope(proj[:, X_KA:X_KA + Q_W], cos4, sin4)
        va = proj[:, X_VA:X_VA + Q_W]
        tq = proj[:, X_QB:X_QB + Q_W]
        rq = lax.rsqrt(_seg_sum64(tq * tq, bd_r) * (1.0 / HEAD_DIM) + QK_EPS)
        qb = _rope(tq * rq * qg_r[...], cos4, sin4) * Q_SCALE
        tk = proj[:, X_KB:X_KB + Q_W]
        rk = lax.rsqrt(_seg_sum64(tk * tk, bd_r) * (1.0 / HEAD_DIM) + QK_EPS)
        kb = _rope(tk * rk * kg_r[...], cos4, sin4)
        vb = proj[:, X_VB:X_VB + Q_W]
        gl = proj[:, X_GL:]
        return [hb, qa, ka, va, qb, kb, vb, tq, rq, tk, rk, gl], []

    mx = MXU_DTYPE
    outs = [((n, D), mx, _rows(TM, D))] + [((n, Q_W), mx, _rows(TM, Q_W))] * 6 + \
           [((n, Q_W), F32, _rows(TM, Q_W))] * 4 + [((n, 2 * D), F32, _rows(TM, 2 * D))]
    res, _ = rowwise("inproj_fwd", body, ntiles=n // TM,
                     tiled=[(xa, _rows(TM, D)), (cos, _rows(TM, LANES)), (sin, _rows(TM, LANES))],
                     full=[modrows, w_ext, b_ext, qg, kg, bd], outs=outs)
    return res


def merge_fwd(oa, ob, gl, x, modrows, wba, wbb, w_out, s_rows):
    def body(t, vals, fr):
        oa_, ob_, gl_, x_ = (v[...] for v in vals)
        mod, wa, wb, wo = fr
        ya = _dot(oa_, wa[...])
        yb = _dot(ob_, wb[...])
        ga = jax.nn.sigmoid(gl_[:, :D])
        gb = jax.nn.sigmoid(gl_[:, D:])
        mrg = ga * ya + gb * yb
        y = _dot(mrg, wo[...])
        r1 = ALPHA * x_ + _mrow(mod, M_GATE1) * y
        mu = jnp.mean(r1, axis=1, keepdims=True)
        xc = r1 - mu
        var = jnp.mean(xc * xc, axis=1, keepdims=True)
        rstd = lax.rsqrt(var + LN_EPS)
        xhat = xc * rstd
        return [ya, yb, mrg, y, xhat, rstd], []

    outs = [((s_rows, D), F32, _rows(TM, D))] * 2 + [((s_rows, D), MXU_DTYPE, _rows(TM, D))] + \
           [((s_rows, D), F32, _rows(TM, D))] * 2 + [((s_rows, 1), F32, _rows(TM, 1))]
    res, _ = rowwise("merge_fwd", body, ntiles=s_rows // TM,
                     tiled=[(oa, _rows(TM, Q_W)), (ob, _rows(TM, Q_W)), (gl, _rows(TM, 2 * D)), (x, _rows(TM, D))],
                     full=[modrows, wba, wbb, w_out], outs=outs)
    return res


def ffn_up_fwd(xhat1, modrows, ln_g, ln_b, w_up, s_rows):
    def body(t, vals, fr):
        xh = vals[0][...]
        mod, g_r, b_r, w = fr
        x1 = xh * g_r[...] + b_r[...]
        h2 = (x1 * (1.0 + _mrow(mod, M_SCALE2)) + _mrow(mod, M_SHIFT2)).astype(MXU_DTYPE)
        return [h2, jnp.dot(h2, w[...], preferred_element_type=F32)], []

    res, _ = rowwise("ffn_up_fwd", body, ntiles=s_rows // TM, tiled=[(xhat1, _rows(TM, D))],
                     full=[modrows, ln_g, ln_b, w_up],
                     outs=[((s_rows, D), MXU_DTYPE, _rows(TM, D)), ((s_rows, 2 * FF), F32, _rows(TM, 2 * FF))])
    return res


TC = 128


def _halo_specs(tm, w, s_rows):
    per = tm // 8
    last = s_rows // 8 - 1
    return (pl.BlockSpec((8, w), lambda i: (jnp.maximum(i * per - 1, 0), 0)),
            pl.BlockSpec((8, w), lambda i: (jnp.minimum((i + 1) * per, last), 0)))


def _halo_rows(t, ntiles, prev_ref, next_ref):
    prev_row = jnp.where(t == 0, 0.0, prev_ref[7:8, :].astype(F32))
    next_row = jnp.where(t == ntiles - 1, 0.0, next_ref[0:1, :].astype(F32))
    return prev_row, next_row


def conv_swiglu_fwd(u0, conv_w8, conv_b, s_rows):
    w2 = 2 * FF
    nt = s_rows // TC

    def body(t, vals, fr):
        u_ref, pv, nx = vals
        cw, cb = fr
        u = u_ref[...]
        up, dn = _shifted_rows(u, *_halo_rows(t, nt, pv, nx))
        uc = cw[0:1, :] * up + cw[1:2, :] * u + cw[2:3, :] * dn + cb[...]
        gate, val = uc[:, :FF], uc[:, FF:]
        return [gate * jax.nn.sigmoid(gate) * val], []

    hp, hn = _halo_specs(TC, w2, s_rows)
    res, _ = rowwise("conv_swiglu_fwd", body, ntiles=nt,
                     tiled=[(u0, _rows(TC, w2)), (u0, hp), (u0, hn)], full=[conv_w8, conv_b],
                     outs=[((s_rows, FF), MXU_DTYPE, _rows(TC, FF))])
    return res[0]


def ffn_down_loss(a, xhat1, target, modrows, ln1_g, ln1_b, ln2_g, ln2_b, w_down, s_rows):
    def body(t, vals, fr):
        a_, xh1, tgt = (v[...] for v in vals)
        mod, g1, b1, g2, b2, wd = fr
        y2 = jnp.dot(a_, wd[...], preferred_element_type=F32)
        x1 = xh1 * g1[...] + b1[...]
        gate2 = _mrow(mod, M_GATE2)
        r2 = ALPHA * x1 + gate2 * y2
        mu = jnp.mean(r2, axis=1, keepdims=True)
        xc = r2 - mu
        var = jnp.mean(xc * xc, axis=1, keepdims=True)
        rstd = lax.rsqrt(var + LN_EPS)
        xhat = xc * rstd
        out = xhat * g2[...] + b2[...]
        diff = out - tgt
        loss = 0.5 * jnp.sum(jnp.mean(diff * diff, axis=1, keepdims=True), axis=0, keepdims=True)
        dout = diff * (1.0 / D)
        dr2 = _layer_norm_bwd(dout * g2[...], xhat, rstd)
        incs = [loss, _colsum(dout * xhat), _colsum(dout), _colsum(dr2 * y2)]
        return [dr2, dr2 * gate2], incs

    res, accs = rowwise("ffn_down_loss", body, ntiles=s_rows // TM,
                        tiled=[(a, _rows(TM, FF)), (xhat1, _rows(TM, D)), (target, _rows(TM, D))],
                        full=[modrows, ln1_g, ln1_b, ln2_g, ln2_b, w_down],
                        outs=[((s_rows, D), F32, _rows(TM, D)), ((s_rows, D), MXU_DTYPE, _rows(TM, D))],
                        accs=[(1, 1), (1, D), (1, D), (1, D)])
    return res, accs


def ffn_down_bwd(dy2, w_down_t, s_rows):
    def body(t, vals, fr):
        return [jnp.dot(vals[0][...], fr[0][...], preferred_element_type=F32)], []

    res, _ = rowwise("ffn_down_bwd", body, ntiles=s_rows // TM, tiled=[(dy2, _rows(TM, D))], full=[w_down_t],
                     outs=[((s_rows, FF), F32, _rows(TM, FF))])
    return res[0]


def swiglu_conv_bwd(u0, da, conv_w8, conv_b, s_rows):
    w2 = 2 * FF
    nt = s_rows // TC
    n = TC + 16

    def body(t, vals, fr):
        u_ref, upv, unx, da_ref, apv, anx = vals
        cw, cb = fr
        first, last = t == 0, t == nt - 1
        ue = jnp.concatenate([jnp.where(first, 0.0, upv[...]), u_ref[...], jnp.where(last, 0.0, unx[...])], axis=0)
        ae = jnp.concatenate([jnp.where(first, 0.0, apv[...]), da_ref[...], jnp.where(last, 0.0, anx[...])], axis=0)
        up = pltpu.roll(ue, 1, 0)
        dn = pltpu.roll(ue, n - 1, 0)
        uc = cw[0:1, :] * up + cw[1:2, :] * ue + cw[2:3, :] * dn + cb[...]
        gate, val = uc[:, :FF], uc[:, FF:]
        sg = jax.nn.sigmoid(gate)
        du = jnp.concatenate([ae * val * (sg * (1.0 + gate * (1.0 - sg))), ae * (gate * sg)], axis=1)
        du0 = cw[0:1, :] * pltpu.roll(du, n - 1, 0) + cw[1:2, :] * du + cw[2:3, :] * pltpu.roll(du, 1, 0)
        rows = slice(8, 8 + TC)
        dut = du[rows]
        return [du0[rows]], [_colsum(dut), _colsum(up[rows] * dut), _colsum(ue[rows] * dut), _colsum(dn[rows] * dut)]

    hp, hn = _halo_specs(TC, w2, s_rows)
    ap, an = _halo_specs(TC, FF, s_rows)
    res, accs = rowwise("swiglu_conv_bwd", body, ntiles=nt,
                        tiled=[(u0, _rows(TC, w2)), (u0, hp), (u0, hn), (da, _rows(TC, FF)), (da, ap), (da, an)],
                        full=[conv_w8, conv_b], outs=[((s_rows, w2), MXU_DTYPE, _rows(TC, w2))], accs=[(1, w2)] * 4)
    return res[0], accs


def ffn_up_ln1_bwd(du0, dr2, xhat1, y, rstd1, modrows, ln_g, ln_b, w_up_t, s_rows):
    def body(t, vals, fr):
        du0_, dr2_, xh, y_, rstd = (v[...] for v in vals)
        mod, g_r, b_r, wt = fr
        dh2 = jnp.dot(du0_, wt[...], preferred_element_type=F32)
        x1 = xh * g_r[...] + b_r[...]
        dx1 = ALPHA * dr2_ + dh2 * (1.0 + _mrow(mod, M_SCALE2))
        dr1 = _layer_norm_bwd(dx1 * g_r[...], xh, rstd)
        incs = [_colsum(dh2 * x1), _colsum(dh2), _colsum(dx1 * xh), _colsum(dx1), _colsum(dr1 * y_)]
        return [dr1 * _mrow(mod, M_GATE1), ALPHA * dr1], incs

    res, accs = rowwise("ffn_up_ln1_bwd", body, ntiles=s_rows // TM,
                        tiled=[(du0, _rows(TM, 2 * FF)), (dr2, _rows(TM, D)), (xhat1, _rows(TM, D)), (y, _rows(TM, D)),
                               (rstd1, _rows(TM, 1))],
                        full=[modrows, ln_g, ln_b, w_up_t],
                        outs=[((s_rows, D), MXU_DTYPE, _rows(TM, D)), ((s_rows, D), F32, _rows(TM, D))],
                        accs=[(1, D)] * 5)
    return res, accs


def merge_bwd(dy, ya, yb, gl, w_out_t, wba_t, wbb_t, s_rows):
    def body(t, vals, fr):
        dy_, ya_, yb_, gl_ = (v[...] for v in vals)
        wot, wat, wbt = fr
        dmrg = jnp.dot(dy_, wot[...], preferred_element_type=F32)
        ga = jax.nn.sigmoid(gl_[:, :D])
        gb = jax.nn.sigmoid(gl_[:, D:])
        dya = dmrg * ga
        dyb = dmrg * gb
        dgl = jnp.concatenate([dmrg * ya_ * ga * (1.0 - ga), dmrg * yb_ * gb * (1.0 - gb)], axis=1)
        return [dya, dyb, dgl, _dot(dya, wat[...]), _dot(dyb, wbt[...])], []

    mx = MXU_DTYPE
    res, _ = rowwise("merge_bwd", body, ntiles=s_rows // TM,
                     tiled=[(dy, _rows(TM, D)), (ya, _rows(TM, D)), (yb, _rows(TM, D)), (gl, _rows(TM, 2 * D))],
                     full=[w_out_t, wba_t, wbb_t],
                     outs=[((s_rows, D), mx, _rows(TM, D))] * 2 + [((s_rows, 2 * D), F32, _rows(TM, 2 * D))] +
                          [((s_rows, Q_W), F32, _rows(TM, Q_W))] * 2)
    return res


def qk_bwd(dqa, dka, dva, dqb, dkb, dvb, dgl, tq, rq, tk, rk, cos, sin, qg, kg, bd, n_lat_tiles, n):
    def body(t, vals, fr):
        dqa_, dka_, dva_, dqb_, dkb_, dvb_, dgl_, tq_, rq_, tk_, rk_, cs, sn = (v[...] for v in vals)
        qg_r, kg_r, bd_r = fr
        is_ctx = t >= n_lat_tiles
        cos4 = jnp.concatenate([cs] * 4, axis=1)
        sin4 = jnp.concatenate([sn] * 4, axis=1)
        zero = jnp.zeros_like(dqa_)
        dpqa = jnp.where(is_ctx, zero, _rope_t(dqa_, cos4, sin4) * Q_SCALE)
        dpka = _rope_t(dka_, cos4, sin4)
        dpva = dva_
        dnq = jnp.where(is_ctx, zero, _rope_t(dqb_, cos4, sin4) * Q_SCALE)
        gq = qg_r[...] * dnq
        dtq = rq_ * gq - tq_ * (rq_ * rq_ * rq_) * (_seg_sum64(gq * tq_, bd_r) * (1.0 / HEAD_DIM))
        dnk = _rope_t(dkb_, cos4, sin4)
        gk = kg_r[...] * dnk
        dtk = rk_ * gk - tk_ * (rk_ * rk_ * rk_) * (_seg_sum64(gk * tk_, bd_r) * (1.0 / HEAD_DIM))
        dgl32 = jnp.where(is_ctx, jnp.zeros_like(dgl_), dgl_)
        dproj = jnp.concatenate([dpqa, dpka, dpva, dtq, dtk, dvb_, dgl32], axis=1)
        return [dproj], [_colsum(dproj), _colsum(dnq * tq_ * rq_), _colsum(dnk * tk_ * rk_)]

    lat = lambda t: jnp.minimum(t, n_lat_tiles - 1)
    qs = _rows(TM, Q_W)
    res, accs = rowwise(
        "qk_bwd", body, ntiles=n // TM,
        tiled=[(dqa, _rows(TM, Q_W, lat)), (dka, qs), (dva, qs), (dqb, _rows(TM, Q_W, lat)),
               (dkb, qs), (dvb, qs), (dgl, _rows(TM, 2 * D, lat)), (tq, qs), (rq, qs), (tk, qs), (rk, qs),
               (cos, _rows(TM, LANES)), (sin, _rows(TM, LANES))],
        full=[qg, kg, bd], outs=[((n, EXT_COLS), MXU_DTYPE, _rows(TM, EXT_COLS))],
        accs=[(1, EXT_COLS), (1, Q_W), (1, Q_W)])
    return res[0], accs


def inproj_bwd(name, dproj, xa, dxp, modrows, w_ext_t, *, ntiles, tile_off, is_ctx, out_rows):
    kc = M_SCALEC if is_ctx else M_SCALE1

    def body(t, vals, fr):
        dp, x_ = vals[0][...], vals[1][...]
        mod, wt = fr
        dh = jnp.dot(dp, wt[...], preferred_element_type=F32)
        incs = [_colsum(dh * x_), _colsum(dh)]
        if is_ctx:
            return [], incs
        return [vals[2][...] + dh * (1.0 + _mrow(mod, kc))], incs

    tiled = [(dproj, _rows(TM, EXT_COLS, lambda i: i + tile_off)), (xa, _rows(TM, D, lambda i: i + tile_off))]
    outs = []
    if not is_ctx:
        tiled.append((dxp, _rows(TM, D)))
        outs = [((out_rows, D), F32, _rows(TM, D))]
    return rowwise(name, body, ntiles=ntiles, tiled=tiled, full=[modrows, w_ext_t], outs=outs, accs=[(1, D)] * 2)


def _attn_semantics():
    return _cparams(("arbitrary", "arbitrary", "arbitrary"))


GLOB_TK = (1280, 1024, 768, 512, 256)
KEY_CHUNK = 256


def glob_fwd(q, kt, v_t, s_rows):
    n = kt.shape[0]
    tq = TM
    tk = _pick(n, GLOB_TK)
    nq, nk = s_rows // tq, n // tk
    r = GROUPS * tq
    nch = tk // KEY_CHUNK

    def produce(qs, k_ref, s_buf, c, mx):
        rows = slice(c * KEY_CHUNK, (c + 1) * KEY_CHUNK)
        sn = _dot_nt(k_ref[rows, :], qs[...])
        s_buf[rows, :] = sn
        return jnp.maximum(mx, jnp.max(sn, axis=0, keepdims=True))

    def kern(q_ref, k0_ref, kn_ref, vt_ref, o_ref, lse_ref, qs, s_buf, mx_buf, m_s, l_s, acc):
        j = pl.program_id(2)

        @pl.when(j == 0)
        def _():
            qs[...] = _stack_groups(q_ref[...], qs.dtype)
            mx = jnp.full((1, r), -jnp.inf, F32)
            for c in range(nch):
                mx = produce(qs, k0_ref, s_buf, c, mx)
            mx_buf[...] = mx
            m_s[...] = jnp.full_like(m_s, -jnp.inf)
            l_s[...] = jnp.zeros_like(l_s)
            acc[...] = jnp.zeros_like(acc)

        m_prev = m_s[...]
        m_new = jnp.maximum(m_prev, mx_buf[...])
        alpha = jnp.exp(m_prev - m_new)
        a = alpha * acc[...]
        ls = alpha * l_s[...]
        mx = jnp.full((1, r), -jnp.inf, F32)
        for c in range(nch):
            rows = slice(c * KEY_CHUNK, (c + 1) * KEY_CHUNK)
            p = jnp.exp(s_buf[rows, :] - m_new)
            ls = ls + jnp.sum(p, axis=0, keepdims=True)
            a = a + jnp.dot(vt_ref[0, :, rows], p.astype(MXU_DTYPE), preferred_element_type=F32)
            mx = produce(qs, kn_ref, s_buf, c, mx)
        mx_buf[...] = mx
        l_s[...] = ls
        acc[...] = a
        m_s[...] = m_new

        @pl.when(j == nk - 1)
        def _():
            o_ref[...] = _untranspose_groups(acc[...] / l_s[...], tq)
            lse_ref[0, 0] = m_s[...] + jnp.log(l_s[...])

    kspec = lambda f: pl.BlockSpec((tk, KV_W), lambda h, i, j: (f(j), h))
    return pl.pallas_call(
        kern, name="glob_fwd", grid=(N_KV, nq, nk),
        in_specs=[pl.BlockSpec((tq, KV_W), lambda h, i, j: (i, h)), kspec(lambda j: 0),
                  kspec(lambda j: jnp.minimum(j + 1, nk - 1)), pl.BlockSpec((1, HEAD_DIM, tk), lambda h, i, j: (h, 0, j))],
        out_specs=[pl.BlockSpec((tq, KV_W), lambda h, i, j: (i, h)),
                   pl.BlockSpec((1, 1, 1, r), lambda h, i, j: (h, i, 0, 0))],
        out_shape=[SDS((s_rows, Q_W), F32), SDS((N_KV, nq, 1, r), F32)],
        scratch_shapes=[pltpu.VMEM((r, KV_W), MXU_DTYPE), pltpu.VMEM((tk, r), F32), pltpu.VMEM((1, r), F32),
                        pltpu.VMEM((1, r), F32), pltpu.VMEM((1, r), F32), pltpu.VMEM((HEAD_DIM, r), F32)],
        compiler_params=_attn_semantics(),
    )(q, kt, kt, v_t)


def attn_delta(o, do, s_rows):
    tq = TM
    nq = s_rows // tq
    r = GROUPS * tq

    def kern(o_ref, do_ref, d_ref):
        d_ref[0, 0] = jnp.sum(_stack_groups(do_ref[...], F32) * _stack_groups(o_ref[...], F32), axis=1, keepdims=True)

    qspec = pl.BlockSpec((tq, KV_W), lambda h, i: (i, h))
    return pl.pallas_call(
        kern, name="attn_delta", grid=(N_KV, nq), in_specs=[qspec, qspec],
        out_specs=pl.BlockSpec((1, 1, r, 1), lambda h, i: (h, i, 0, 0)), out_shape=SDS((N_KV, nq, r, 1), F32),
        compiler_params=_cparams(("parallel", "parallel")),
    )(o, do)


def _compact_t(tt, dtype):
    return jnp.concatenate([tt[g * HEAD_DIM:(g + 1) * HEAD_DIM, :] for g in range(GROUPS)], axis=1).astype(dtype)


def glob_bwd(q, q_t, kt, vt, do, do_t, lse, delta, h, s_rows):
    n = kt.shape[0]
    tq = TM
    tk = _pick(n, GLOB_TK)
    nq, nk = s_rows // tq, n // tk
    r = GROUPS * tq
    nch = tk // KEY_CHUNK

    def kern(q_ref, qt_ref, k_ref, v_ref, do_ref, dot_ref, lse_ref, dl_ref, dq_ref, dkt_ref, dvt_ref, p_buf, ds_buf):
        j = pl.program_id(0)
        i = pl.program_id(1)

        @pl.when(i == 0)
        def _():
            dkt_ref[...] = jnp.zeros_like(dkt_ref)
            dvt_ref[...] = jnp.zeros_like(dvt_ref)

        qs = _stack_groups(q_ref[...], MXU_DTYPE)
        dos = _stack_groups(do_ref[...], MXU_DTYPE)
        lse_b = jnp.broadcast_to(lse_ref[0], (r, LANES))
        dl_b = jnp.broadcast_to(dl_ref[0], (r, LANES))
        for c in range(nch):
            lo = c * KEY_CHUNK
            sc = _dot_nt(qs, k_ref[lo:lo + KEY_CHUNK, :])
            dpc = _dot_nt(dos, v_ref[lo:lo + KEY_CHUNK, :])
            for t in range(KEY_CHUNK // LANES):
                sl = slice(t * LANES, (t + 1) * LANES)
                pt = jnp.exp(sc[:, sl] - lse_b)
                p_buf[:, lo + t * LANES:lo + (t + 1) * LANES] = pt.astype(p_buf.dtype)
                ds_buf[:, lo + t * LANES:lo + (t + 1) * LANES] = (pt * (dpc[:, sl] - dl_b)).astype(ds_buf.dtype)
        dq_t = _fold_groups(jnp.dot(ds_buf[...], k_ref[...], preferred_element_type=F32), tq)
        rows = pl.ds(pl.multiple_of(i * tq, tq), tq)

        @pl.when(j == 0)
        def _():
            dq_ref[rows, :] = dq_t

        @pl.when(j > 0)
        def _():
            dq_ref[rows, :] += dq_t

        dvt_ref[...] += jnp.dot(_compact_t(dot_ref[...], MXU_DTYPE), p_buf[...], preferred_element_type=F32)
        dkt_ref[...] += jnp.dot(_compact_t(qt_ref[...], MXU_DTYPE), ds_buf[...], preferred_element_type=F32)

    col = pl.BlockSpec((1, r, 1), lambda j, i: (i, 0, 0))
    qspec = pl.BlockSpec((tq, KV_W), lambda j, i: (i, h))
    tspec = pl.BlockSpec((KV_W, tq), lambda j, i: (h, i))
    kspec = pl.BlockSpec((tk, KV_W), lambda j, i: (j, h))
    ospec = pl.BlockSpec((HEAD_DIM, tk), lambda j, i: (0, j))
    return pl.pallas_call(
        kern, name=f"glob_bwd_h{h}", grid=(nk, nq),
        in_specs=[qspec, tspec, kspec, kspec, qspec, tspec, col, col],
        out_specs=[pl.BlockSpec(memory_space=pltpu.VMEM), ospec, ospec],
        out_shape=[SDS((s_rows, KV_W), F32), SDS((HEAD_DIM, n), F32), SDS((HEAD_DIM, n), F32)],
        scratch_shapes=[pltpu.VMEM((r, tk), MXU_DTYPE), pltpu.VMEM((r, tk), MXU_DTYPE)],
        compiler_params=_cparams(("arbitrary", "arbitrary")),
    )(q, q_t, kt, vt, do, do_t, lse, delta)


TW = 2 * WIN
WR = GROUPS * TW
WLAT = 4 * WIN


def _win_cat(dst, parts):
    off = 0
    for p in parts:
        dst[off:off + p.shape[0], :] = p[...]
        off += p.shape[0]


def _win_specs(s_rows, c_rows):
    nb = s_rows // WIN
    prev = lambda i: jnp.maximum(2 * i - 1, 0)
    nxt = lambda i: jnp.minimum(2 * i + 2, nb - 1)
    rows = [pl.BlockSpec((WIN, KV_W), lambda h, i: (prev(i), h)), pl.BlockSpec((TW, KV_W), lambda h, i: (i, h)),
            pl.BlockSpec((WIN, KV_W), lambda h, i: (nxt(i), h)), pl.BlockSpec((c_rows, KV_W), lambda h, i: (s_rows // c_rows, h))]
    cols = [pl.BlockSpec((1, HEAD_DIM, WIN), lambda h, i: (h, 0, prev(i))), pl.BlockSpec((1, HEAD_DIM, TW), lambda h, i: (h, 0, i)),
            pl.BlockSpec((1, HEAD_DIM, WIN), lambda h, i: (h, 0, nxt(i))),
            pl.BlockSpec((1, HEAD_DIM, c_rows), lambda h, i: (h, 0, s_rows // c_rows))]
    return rows, cols


def _win_mask(i, s_rows, shape, keys_on_rows):
    a = lax.broadcasted_iota(jnp.int32, shape, 0)
    b = lax.broadcasted_iota(jnp.int32, shape, 1)
    kk, qq = (a, b) if keys_on_rows else (b, a)
    qpos = i * TW + (qq & (TW - 1))
    kpos = (2 * i - 1) * WIN + kk
    band = (jnp.abs(qpos - kpos) <= WIN) & (kpos >= 0) & (kpos < s_rows)
    return (kk >= WLAT) | band


def _untranspose_groups(o_t, tq):
    row = lax.broadcasted_iota(jnp.int32, (HEAD_DIM, KV_W), 0)
    col = lax.broadcasted_iota(jnp.int32, (HEAD_DIM, KV_W), 1)
    hi = o_t.astype(BF16)
    r1 = o_t - hi.astype(F32)
    mid = r1.astype(BF16)
    lo = (r1 - mid.astype(F32)).astype(BF16)
    o = jnp.zeros((tq, KV_W), F32)
    for g in range(GROUPS):
        sel = jnp.where(col == row + g * HEAD_DIM, 1.0, 0.0).astype(BF16)
        for term in (hi, mid, lo):
            o = o + lax.dot_general(term[:, g * tq:(g + 1) * tq], sel, (((0,), (0,)), ((), ())), preferred_element_type=F32)
    return o


def win_fwd(q, kt, v_t, sinkrow, s_rows, c_rows):
    nt = s_rows // TW
    nkeys = WLAT + c_rows

    def kern(q_ref, kp, kc, kn, kx, vp, vc, vn, vx, sink_ref, o_ref, lse_ref, kcat):
        i = pl.program_id(1)
        _win_cat(kcat, (kp, kc, kn, kx))
        qs = _stack_groups(q_ref[...], MXU_DTYPE)
        st = _dot_nt(kcat[...], qs)
        st = jnp.where(_win_mask(i, s_rows, st.shape, True), st, -jnp.inf)
        sink = sink_ref[0]
        m = jnp.maximum(jnp.max(st, axis=0, keepdims=True), sink)
        e = jnp.exp(st - m)
        den = jnp.sum(e, axis=0, keepdims=True) + jnp.exp(sink - m)
        v_cat = jnp.concatenate([vp[0], vc[0], vn[0], vx[0]], axis=1)
        o_t = jnp.dot(v_cat, e.astype(MXU_DTYPE), preferred_element_type=F32) / den
        o_ref[...] = _untranspose_groups(o_t, TW)
        lse_ref[0, 0] = m + jnp.log(den)

    rows, cols = _win_specs(s_rows, c_rows)
    qspec = pl.BlockSpec((TW, KV_W), lambda h, i: (i, h))
    rowv = pl.BlockSpec((1, 1, 1, WR), lambda h, i: (h, i, 0, 0))
    return pl.pallas_call(
        kern, name="win_fwd", grid=(N_KV, nt),
        in_specs=[qspec] + rows + cols + [pl.BlockSpec((1, 1, WR), lambda h, i: (h, 0, 0))],
        out_specs=[qspec, rowv], out_shape=[SDS((s_rows, Q_W), F32), SDS((N_KV, nt, 1, WR), F32)],
        scratch_shapes=[pltpu.VMEM((nkeys, KV_W), MXU_DTYPE)],
        compiler_params=_cparams(("parallel", "parallel")),
    )(q, kt, kt, kt, kt, v_t, v_t, v_t, v_t, sinkrow)


def win_bwd(q, q_t, kt, vt, sinkcol, o, do, do_t, lse, s_rows, c_rows):
    nt = s_rows // TW
    nkeys = WLAT + c_rows
    n = s_rows + c_rows
    ctx0 = WIN + s_rows

    def kern(q_ref, qt_ref, kp, kc, kn, kx, vp, vc, vn, vx, sink_ref, o_ref, do_ref, dot_ref, lse_ref,
             dq_ref, dkt_ref, dvt_ref, dsk_ref, kcat, vcat):
        i = pl.program_id(1)

        @pl.when(i == 0)
        def _():
            dkt_ref[...] = jnp.zeros_like(dkt_ref)
            dvt_ref[...] = jnp.zeros_like(dvt_ref)
            dsk_ref[...] = jnp.zeros_like(dsk_ref)

        _win_cat(kcat, (kp, kc, kn, kx))
        _win_cat(vcat, (vp, vc, vn, vx))
        qs = _stack_groups(q_ref[...], MXU_DTYPE)
        do32 = _stack_groups(do_ref[...], F32)
        delta = jnp.sum(do32 * _stack_groups(o_ref[...], F32), axis=1, keepdims=True)
        dos = do32.astype(MXU_DTYPE)
        lse_c = lse_ref[0, 0]
        s = _dot_nt(qs, kcat[...])
        s = jnp.where(_win_mask(i, s_rows, s.shape, False), s, -jnp.inf)
        p = jnp.exp(s - lse_c)
        ds = p * (_dot_nt(dos, vcat[...]) - delta)
        dq_ref[...] = _fold_groups(_dot(ds, kcat[...]), TW)
        dvt = jnp.dot(_compact_t(dot_ref[...], MXU_DTYPE), p.astype(MXU_DTYPE), preferred_element_type=F32)
        dkt = jnp.dot(_compact_t(qt_ref[...], MXU_DTYPE), ds.astype(MXU_DTYPE), preferred_element_type=F32)
        lat = pl.ds(pl.multiple_of(i * TW, TW), WLAT)
        dkt_ref[0, :, lat] += dkt[:, :WLAT]
        dvt_ref[0, :, lat] += dvt[:, :WLAT]
        dkt_ref[0, :, ctx0:ctx0 + c_rows] += dkt[:, WLAT:]
        dvt_ref[0, :, ctx0:ctx0 + c_rows] += dvt[:, WLAT:]
        dsk_ref[0] += -(jnp.exp(sink_ref[0][:, 0:1] - lse_c) * delta)

    rows, _ = _win_specs(s_rows, c_rows)
    qspec = pl.BlockSpec((TW, KV_W), lambda h, i: (i, h))
    tspec = pl.BlockSpec((KV_W, TW), lambda h, i: (h, i))
    col = pl.BlockSpec((1, 1, WR, 1), lambda h, i: (h, i, 0, 0))
    kvt = pl.BlockSpec((1, HEAD_DIM, WIN + n), lambda h, i: (h, 0, 0))
    return pl.pallas_call(
        kern, name="win_bwd", grid=(N_KV, nt),
        in_specs=[qspec, tspec] + rows + rows + [pl.BlockSpec((1, WR, LANES), lambda h, i: (h, 0, 0)), qspec, qspec, tspec, col],
        out_specs=[qspec, kvt, kvt, pl.BlockSpec((1, WR, 1), lambda h, i: (h, 0, 0))],
        out_shape=[SDS((s_rows, Q_W), F32), SDS((N_KV, HEAD_DIM, WIN + n), F32), SDS((N_KV, HEAD_DIM, WIN + n), F32),
                   SDS((N_KV, WR, 1), F32)],
        scratch_shapes=[pltpu.VMEM((nkeys, KV_W), MXU_DTYPE), pltpu.VMEM((nkeys, KV_W), MXU_DTYPE)],
        compiler_params=_cparams(("arbitrary", "arbitrary")),
    )(q, q_t, kt, kt, kt, kt, vt, vt, vt, vt, sinkcol, o, do, do_t, lse)


def adamw(name, w, m, v, grads):
    r, wd = w.shape
    tr = _pick(r, [t for t in ELEMENTWISE_ROWS if t * wd * 4 <= ELEMENTWISE_BLOCK_BYTES])
    stacked = not isinstance(grads, (list, tuple))
    ng = grads.shape[0] if stacked else len(grads)

    def kern(*refs):
        w_ref, m_ref, v_ref = refs[:3]
        g_refs = refs[3:-4]
        g_out, d_out, m_out, v_out = refs[-4:]
        if stacked:
            g = g_refs[0][0]
            for k in range(1, ng):
                g = g + g_refs[0][k]
        else:
            g = g_refs[0][...]
            for gr in g_refs[1:]:
                g = g + gr[...]
        wv = w_ref[...]
        mn = ADAM_B1 * m_ref[...] + (1.0 - ADAM_B1) * g
        vn = ADAM_B2 * v_ref[...] + (1.0 - ADAM_B2) * (g * g)
        m_hat = mn / (1.0 - ADAM_B1 ** ADAM_STEP)
        v_hat = vn / (1.0 - ADAM_B2 ** ADAM_STEP)
        g_out[...] = g
        d_out[...] = -ADAM_LR * (m_hat / (jnp.sqrt(v_hat) + ADAM_EPS) + ADAM_WD * wv)
        m_out[...] = mn
        v_out[...] = vn

    spec = pl.BlockSpec((tr, wd), lambda i: (i, 0))
    gspecs = [pl.BlockSpec((ng, tr, wd), lambda i: (0, i, 0))] if stacked else [spec] * ng
    return pl.pallas_call(
        kern, name=name, grid=(r // tr,), in_specs=[spec] * 3 + gspecs, out_specs=[spec] * 4,
        out_shape=[SDS((r, wd), F32)] * 4, compiler_params=_cparams(("parallel",)),
    )(w, m, v, *([grads] if stacked else grads))


def add2(name, a, b):
    k, r, w = a.shape
    tr = _pick(r, [t for t in ELEMENTWISE_ROWS if t * w * 4 <= ELEMENTWISE_BLOCK_BYTES])

    def kern(a_ref, b_ref, o_ref):
        o_ref[...] = a_ref[...] + b_ref[...]

    spec = pl.BlockSpec((1, tr, w), lambda s, i: (s, i, 0))
    return pl.pallas_call(kern, name=name, grid=(k, r // tr), in_specs=[spec, spec], out_specs=spec,
                          out_shape=SDS(a.shape, a.dtype), compiler_params=_cparams(("parallel", "parallel")))(a, b)


def _rep4(a, off):
    return jnp.concatenate([a[:, off + HEAD_DIM * h: off + HEAD_DIM * (h + 1)] for h in range(N_KV) for _ in range(GROUPS)], axis=1)


def _extend_cols(a):
    return jnp.concatenate([a[:, 0:OFF_KA], _rep4(a, OFF_KA), _rep4(a, OFF_VA), a[:, OFF_QB:OFF_KB],
                            _rep4(a, OFF_KB), _rep4(a, OFF_VB), a[:, OFF_GA:]], axis=1)


def _fold4(a, off):
    r = a.shape[0]
    return a[:, off:off + Q_W].reshape(r, N_KV, GROUPS, HEAD_DIM).sum(axis=2).reshape(r, N_KV * HEAD_DIM)


def _fold_cols(a):
    return jnp.concatenate([a[:, X_QA:X_QA + Q_W], _fold4(a, X_KA), _fold4(a, X_VA), a[:, X_QB:X_QB + Q_W],
                            _fold4(a, X_KB), _fold4(a, X_VB), a[:, X_GL:]], axis=1)


def _rope_tables(s_rows, c_rows):
    pos = jnp.arange(s_rows, dtype=jnp.int32)
    rows = (pos // GRID_W).astype(F32)
    cols = (pos % GRID_W).astype(F32)
    n_freq = HEAD_DIM // 4
    inv_freq = ROPE_THETA ** (-jnp.arange(n_freq, dtype=F32) / n_freq)
    ang_r = rows[:, None] * inv_freq
    ang_c = cols[:, None] * inv_freq
    cos = jnp.concatenate([jnp.cos(ang_r)] * 2 + [jnp.cos(ang_c)] * 2, axis=1)
    sin = jnp.concatenate([-jnp.sin(ang_r), jnp.sin(ang_r), -jnp.sin(ang_c), jnp.sin(ang_c)], axis=1)
    cos = jnp.concatenate([cos, jnp.ones((c_rows, HEAD_DIM), F32)], axis=0)
    sin = jnp.concatenate([sin, jnp.zeros((c_rows, HEAD_DIM), F32)], axis=0)
    return jnp.concatenate([cos, cos], axis=1), jnp.concatenate([sin, sin], axis=1)


def _ff_pad_cols(a):
    r = a.shape[0]
    a = jnp.pad(a.reshape(r, N_DEV, FF_SHARD), ((0, 0), (0, 0), (0, FF_SHARD_PAD - FF_SHARD)))
    return a.reshape(r, 2 * FF)


def _ff_unpad_cols(a):
    r = a.shape[0]
    return a.reshape(r, N_DEV, FF_SHARD_PAD)[:, :, :FF_SHARD].reshape(r, 2 * D_FF)


def _ff_pad_rows(a):
    c = a.shape[1]
    a = jnp.pad(a.reshape(N_DEV // 2, FF_SHARD, c), ((0, 0), (0, FF_SHARD_PAD - FF_SHARD), (0, 0)))
    return a.reshape(FF, c)


def _ff_unpad_rows(a):
    c = a.shape[1]
    return a.reshape(N_DEV // 2, FF_SHARD_PAD, c)[:, :FF_SHARD].reshape(D_FF, c)


BIG = (("w_in", (D, IN_COLS // N_DEV)), ("w_branch_a", (Q_W, D // N_DEV)), ("w_branch_b", (Q_W, D // N_DEV)),
       ("w_out", (D // N_DEV, D)), ("w_up", (D, FF_SHARD_PAD)), ("w_down", (D_FF // N_DEV, D)))
BIG_SIZES = tuple(int(np.prod(s)) for _, s in BIG)
BIG_ROWS = sum(BIG_SIZES) // LANES


def _pack_big(parts):
    return jnp.concatenate([p.reshape(-1) for p in parts]).reshape(BIG_ROWS, LANES)


def _unpack_big(flat):
    lead = flat.shape[:-2]
    f = flat.reshape(*lead, BIG_ROWS * LANES)
    out, off = [], 0
    for (_, shp), sz in zip(BIG, BIG_SIZES, strict=True):
        out.append(f[..., off:off + sz].reshape(*lead, *shp))
        off += sz
    return out


def _cols_to_full(g):
    return jnp.transpose(g, (1, 0, 2)).reshape(g.shape[1], -1)


def _full_to_cols(a):
    r, c = a.shape
    return jnp.transpose(a.reshape(r, N_DEV, c // N_DEV), (1, 0, 2))


SMALL = (("c_ctx", D), ("b_mod", N_MOD * D), ("b_in", IN_COLS), ("attn_sink", N_HEADS), ("q_norm_g", HEAD_DIM),
         ("k_norm_g", HEAD_DIM), ("ln1_g", D), ("ln1_b", D), ("conv_w", 3 * 2 * D_FF // N_DEV), ("conv_b", 2 * D_FF),
         ("ln2_g", D), ("ln2_b", D))
SMALL_TOTAL = sum(n for _, n in SMALL)
SMALL_ROWS = -(-SMALL_TOTAL // (8 * LANES)) * 8


def _pack_small(parts):
    flat = jnp.concatenate([p.reshape(-1).astype(F32) for p in parts])
    return jnp.pad(flat, (0, SMALL_ROWS * LANES - flat.shape[0])).reshape(SMALL_ROWS, LANES)


def _unpack_small(packed):
    f = packed.reshape(-1)
    out, off = {}, 0
    for name, n in SMALL:
        out[name] = f[off:off + n]
        off += n
    return out


RED = (("c_ctx", D), ("b_in", IN_COLS), ("attn_sink", N_HEADS), ("q_norm_g", HEAD_DIM), ("k_norm_g", HEAD_DIM),
       ("ln1_g", D), ("ln1_b", D), ("conv_w", 3 * 2 * FF), ("conv_b", 2 * FF), ("ln2_g", D), ("ln2_b", D))
RED_TOTAL = sum(n for _, n in RED)
RED_ROWS = -(-RED_TOTAL // (8 * LANES)) * 8


def sum8(name, g):
    _, r, w = g.shape

    def kern(g_ref, o_ref):
        acc = g_ref[0]
        for k in range(1, N_DEV):
            acc = acc + g_ref[k]
        o_ref[...] = acc

    return pl.pallas_call(kern, name=name, out_shape=SDS((r, w), F32))(g)


def _local_step(x, ctx, target, modrows, weights, small):
    s_rows, c_rows = x.shape[0], ctx.shape[0]
    n = s_rows + c_rows
    nl = s_rows // TM
    w_in, wba, wbb, w_out, w_up, w_down = weights
    f = lambda a: a.reshape(1, -1).astype(F32)
    b_in, ln1_g, ln1_b, ln2_g, ln2_b, conv_b = (f(small[k]) for k in ("b_in", "ln1_g", "ln1_b", "ln2_g", "ln2_b", "conv_b"))
    conv_w8 = jnp.pad(small["conv_w_full"], ((0, 5), (0, 0)))
    qg = jnp.tile(small["q_norm_g"].reshape(1, HEAD_DIM), (1, N_HEADS))
    kg = jnp.tile(small["k_norm_g"].reshape(1, HEAD_DIM), (1, N_HEADS))
    sink_rep = jnp.repeat(small["attn_sink"].reshape(N_KV, GROUPS), TW, axis=1)
    sinkrow = sink_rep.reshape(N_KV, 1, WR)
    sinkcol = jnp.broadcast_to(sink_rep[:, :, None], (N_KV, WR, LANES))
    bd = jnp.kron(jnp.eye(N_HEADS, dtype=F32), jnp.ones((HEAD_DIM, HEAD_DIM), F32)).astype(BF16)
    cos, sin = _rope_tables(s_rows, c_rows)
    w_ext = _extend_cols(w_in)
    b_ext = _extend_cols(b_in)
    xa = jnp.concatenate([x, ctx], axis=0)

    hb, qa, kat, vat, qb, kbt, vbt, tq, rq, tk, rk, gl = inproj_fwd(xa, cos, sin, modrows, w_ext, b_ext, qg, kg, bd, nl)
    compact_t = lambda t: jnp.stack([t[:, h * KV_W:h * KV_W + HEAD_DIM].T for h in range(N_KV)])
    oa, lse_a = win_fwd(qa, kat, compact_t(vat), sinkrow, s_rows, c_rows)
    lse_a = lse_a.reshape(N_KV, s_rows // TW, WR, 1)
    vb_t = compact_t(vbt)
    ob, lse_b = glob_fwd(qb, kbt, vb_t, s_rows)
    lse_b = lse_b.reshape(N_KV, s_rows // TM, GROUPS * TM, 1)
    ya, yb, mrg, y, xhat1, rstd1 = merge_fwd(oa, ob, gl, x, modrows, wba, wbb, w_out, s_rows)
    h2, u0 = ffn_up_fwd(xhat1, modrows, ln1_g, ln1_b, w_up, s_rows)
    a = conv_swiglu_fwd(u0, conv_w8, conv_b, s_rows)
    (dr2, dy2), (loss, dln2_g, dln2_b, dgate2) = ffn_down_loss(a, xhat1, target, modrows, ln1_g, ln1_b, ln2_g, ln2_b, w_down, s_rows)

    da = ffn_down_bwd(dy2, w_down.T, s_rows)
    dw_down = mm_tn("dw_down", a, dy2, s_rows)
    du0, (dconv_b, dcw0, dcw1, dcw2) = swiglu_conv_bwd(u0, da, conv_w8, conv_b, s_rows)
    dw_up = mm_tn("dw_up", h2, du0, s_rows)
    (dy, dxp), (dscale2, dshift2, dln1_g, dln1_b, dgate1) = ffn_up_ln1_bwd(du0, dr2, xhat1, y, rstd1, modrows, ln1_g, ln1_b, w_up.T, s_rows)
    dya, dyb, dgl, doa, dob = merge_bwd(dy, ya, yb, gl, w_out.T, wba.T, wbb.T, s_rows)
    dw_out = mm_tn("dw_out", mrg, dy, s_rows)
    dwba = mm_tn("dw_branch_a", oa, dya, s_rows)
    dwbb = mm_tn("dw_branch_b", ob, dyb, s_rows)

    pad = jnp.zeros((n, KV_W - HEAD_DIM), F32)
    spread = lambda per_head: jnp.concatenate([t for th in per_head for t in (th.T, pad)], axis=1)
    dqa, dka_t, dva_t, dsk = win_bwd(qa, qa[:s_rows].T, kat, vat, sinkcol, oa, doa, doa.astype(MXU_DTYPE).T, lse_a,
                                     s_rows, c_rows)
    dka = spread([dka_t[h, :, WIN:] for h in range(N_KV)])
    dva = spread([dva_t[h, :, WIN:] for h in range(N_KV)])
    delta_b = attn_delta(ob, dob, s_rows)
    qb_t = qb[:s_rows].T
    dob_t = dob.astype(MXU_DTYPE).T
    heads = [glob_bwd(qb, qb_t, kbt, vbt, dob, dob_t, lse_b[h], delta_b[h], h, s_rows) for h in range(N_KV)]
    dqb = jnp.concatenate([hd[0] for hd in heads], axis=1)
    dkb = spread([hd[1] for hd in heads])
    dvb = spread([hd[2] for hd in heads])
    dproj, (db_ext, dqg, dkg) = qk_bwd(dqa, dka, dva, dqb, dkb, dvb, dgl, tq, rq, tk, rk, cos, sin, qg, kg, bd, nl, n)
    w_ext_t = w_ext.T
    (grad_x,), (dscale1, dshift1) = inproj_bwd("inproj_bwd", dproj, xa, dxp, modrows, w_ext_t, ntiles=nl, tile_off=0,
                                               is_ctx=False, out_rows=s_rows)
    _, (dscale_c, dshift_c) = inproj_bwd("inproj_bwd_ctx", dproj, xa, None, modrows, w_ext_t, ntiles=c_rows // TM,
                                         tile_off=nl, is_ctx=True, out_rows=0)
    dw_in = _fold_cols(mm_tn("dw_in", hb, dproj, n))

    dmod = jnp.concatenate([dshift1, dscale1, dgate1, dshift2, dscale2, dgate2], axis=1)
    dmod_c = jnp.concatenate([dshift_c, dscale_c, jnp.zeros((1, (N_MOD - 2) * D), F32)], axis=1)
    fold_g = lambda t: t.reshape(N_HEADS, HEAD_DIM).sum(axis=0)
    red = {
        "b_in": _fold_cols(db_ext), "attn_sink": dsk.reshape(N_HEADS, TW).sum(axis=1), "q_norm_g": fold_g(dqg),
        "k_norm_g": fold_g(dkg), "ln1_g": dln1_g, "ln1_b": dln1_b, "conv_w": jnp.concatenate([dcw0, dcw1, dcw2], axis=0),
        "conv_b": dconv_b, "ln2_g": dln2_g, "ln2_b": dln2_b,
    }
    return loss[0, 0], grad_x, (dw_in, dwba, dwbb, dw_out, dw_up, dw_down), dmod, dmod_c, red


def kernel(x, c, ctx, c_ctx, w_mod, b_mod, w_in, b_in, attn_sink, q_norm_g, k_norm_g, w_branch_a, w_branch_b, w_out, ln1_g, ln1_b, w_up, conv_w, conv_b, w_down, ln2_g, ln2_b, loss_target, m_c_ctx, m_w_mod, m_b_mod, m_w_in, m_b_in, m_attn_sink, m_q_norm_g, m_k_norm_g, m_w_branch_a, m_w_branch_b, m_w_out, m_ln1_g, m_ln1_b, m_w_up, m_conv_w, m_conv_b, m_w_down, m_ln2_g, m_ln2_b, v_c_ctx, v_w_mod, v_b_mod, v_w_in, v_b_in, v_attn_sink, v_q_norm_g, v_k_norm_g, v_w_branch_a, v_w_branch_b, v_w_out, v_ln1_g, v_ln1_b, v_w_up, v_conv_w, v_conv_b, v_w_down, v_ln2_g, v_ln2_b):
    ax, ay, ac = (lax.axis_index(a) for a in AXES)
    me = 4 * ax + 2 * ay + ac
    chip = 2 * ax + ay
    mod_w = N_MOD * D // N_DEV
    params = dict(c_ctx=c_ctx, w_mod=w_mod, b_mod=b_mod, w_in=w_in, b_in=b_in, attn_sink=attn_sink, q_norm_g=q_norm_g,
                  k_norm_g=k_norm_g, w_branch_a=w_branch_a, w_branch_b=w_branch_b, w_out=w_out, ln1_g=ln1_g, ln1_b=ln1_b,
                  w_up=w_up, conv_w=conv_w, conv_b=conv_b, w_down=w_down, ln2_g=ln2_g, ln2_b=ln2_b)
    mom_m = dict(c_ctx=m_c_ctx, w_mod=m_w_mod, b_mod=m_b_mod, w_in=m_w_in, b_in=m_b_in, attn_sink=m_attn_sink,
                 q_norm_g=m_q_norm_g, k_norm_g=m_k_norm_g, w_branch_a=m_w_branch_a, w_branch_b=m_w_branch_b, w_out=m_w_out,
                 ln1_g=m_ln1_g, ln1_b=m_ln1_b, w_up=m_w_up, conv_w=m_conv_w, conv_b=m_conv_b, w_down=m_w_down,
                 ln2_g=m_ln2_g, ln2_b=m_ln2_b)
    mom_v = dict(c_ctx=v_c_ctx, w_mod=v_w_mod, b_mod=v_b_mod, w_in=v_w_in, b_in=v_b_in, attn_sink=v_attn_sink,
                 q_norm_g=v_q_norm_g, k_norm_g=v_k_norm_g, w_branch_a=v_w_branch_a, w_branch_b=v_w_branch_b, w_out=v_w_out,
                 ln1_g=v_ln1_g, ln1_b=v_ln1_b, w_up=v_w_up, conv_w=v_conv_w, conv_b=v_conv_b, w_down=v_w_down,
                 ln2_g=v_ln2_g, ln2_b=v_ln2_b)
    big_names = [nm for nm, _ in BIG]

    def shard(tree, nm):
        t = tree[nm][0]
        return jnp.pad(t, ((0, 0), (0, FF_SHARD_PAD - FF_SHARD))) if nm == "w_up" else t

    wg = all_gather("ag_weights", _pack_big([shard(params, nm).astype(MXU_DTYPE) for nm in big_names]))
    g_in, g_ba, g_bb, g_out, g_up, g_down = _unpack_big(wg)
    weights = (_cols_to_full(g_in), _cols_to_full(g_ba), _cols_to_full(g_bb), g_out.reshape(D, D), _cols_to_full(g_up),
               _ff_pad_rows(g_down.reshape(D_FF, D)))

    c_all = all_gather("ag_c", c.reshape(8, LANES)).reshape(N_DEV, D)
    cs = jnp.concatenate([c_all, c_ctx.reshape(1, D), jnp.zeros((7, D), F32)], axis=0)
    w_mod_sh = w_mod[0]
    b_mod_sh = lax.dynamic_slice(b_mod, (0, me * mod_w), (1, mod_w))
    mod_part = mod_fwd(cs, w_mod_sh, b_mod_sh)
    mg = all_gather("ag_mod", mod_part.reshape(16 * mod_w // LANES, LANES)).reshape(N_DEV, 16, mod_w)
    mod = lax.dynamic_index_in_dim(mg, me, axis=1, keepdims=False).reshape(N_MOD, D)
    mod_c = mg[:, 8, :].reshape(N_MOD, D)
    modrows = jnp.stack([mod[0], mod[1], mod_c[0], mod_c[1], mod[2], mod[3], mod[4], mod[5]], axis=0)

    conv_w_full = all_gather("ag_conv_w", jnp.pad(conv_w[0], ((0, 5), (0, FF_SHARD_PAD - FF_SHARD))))
    conv_w_full = _cols_to_full(conv_w_full[:, :3, :])
    small = dict(b_in=b_in, ln1_g=ln1_g, ln1_b=ln1_b, ln2_g=ln2_g, ln2_b=ln2_b, conv_b=_ff_pad_cols(conv_b),
                 conv_w_full=conv_w_full, q_norm_g=q_norm_g, k_norm_g=k_norm_g, attn_sink=attn_sink)
    loss, grad_x, big_grads, dmod, dmod_c, red = _local_step(x[0], ctx[0], loss_target[0], modrows, weights, small)
    loss = lax.psum(loss, AXES)

    dm = all_gather("ag_dmod", jnp.concatenate([dmod, dmod_c], axis=0).reshape(2 * N_MOD * D // LANES, LANES))
    dm = dm.reshape(N_DEV, 2, N_MOD * D)
    dm_all = jnp.concatenate([dm[:, 0], dm[:, 1]], axis=0)
    dm_sh = lax.dynamic_slice(dm_all, (0, me * mod_w), (16, mod_w))
    dw_mod, dcc, db_mod = mod_bwd(cs, w_mod_sh, dm_sh, dm_all)
    red["c_ctx"] = dcc[8]

    red_vec = jnp.concatenate([red[nm].reshape(-1) for nm, _ in RED])
    red_vec = jnp.pad(red_vec, (0, RED_ROWS * LANES - RED_TOTAL)).reshape(RED_ROWS, LANES)
    red_sum = sum8("sum_small", all_gather("ag_small", red_vec)).reshape(-1)
    gsm, off = {}, 0
    for nm, k in RED:
        gsm[nm] = red_sum[off:off + k]
        off += k
    gsm["b_mod"] = db_mod.reshape(-1)
    gsm["conv_b"] = _ff_unpad_cols(gsm["conv_b"].reshape(1, 2 * FF))
    gsm["conv_w"] = lax.dynamic_slice(gsm["conv_w"].reshape(3, 2 * FF), (0, me * FF_SHARD_PAD), (3, FF_SHARD_PAD))[:, :FF_SHARD]
    sm_names = [nm for nm, _ in SMALL]
    gs, ds, ms, vs = adamw("adamw_small", _pack_small([params[nm] for nm in sm_names]),
                           _pack_small([mom_m[nm] for nm in sm_names]), _pack_small([mom_v[nm] for nm in sm_names]),
                           [_pack_small([gsm[nm] for nm in sm_names])])
    sm_out = [_unpack_small(t) for t in (gs, ds, ms, vs)]

    dw_in, dwba, dwbb, dw_out, dw_up, dw_down = big_grads
    slabs = jnp.concatenate([t.reshape(N_DEV, -1) for t in (
        _full_to_cols(dw_in), _full_to_cols(dwba), _full_to_cols(dwbb), dw_out, _full_to_cols(dw_up),
        _ff_unpad_rows(dw_down))], axis=1)
    by_core = slabs.reshape(4, 2, BIG_ROWS, LANES)
    keep = lax.dynamic_index_in_dim(by_core, ac, axis=1, keepdims=False)
    give = lax.dynamic_index_in_dim(by_core, 1 - ac, axis=1, keepdims=False)
    got = exchange("rs_sibling", give.reshape(1, 4 * BIG_ROWS, LANES), to_chips=False).reshape(4, BIG_ROWS, LANES)
    pair = add2("rs_pair_sum", keep, got)
    outbox = jnp.stack([lax.dynamic_index_in_dim(pair, jnp.bitwise_xor(chip, m), axis=0, keepdims=False) for m in (1, 2, 3)])
    inbox = exchange("rs_chips", outbox, to_chips=True)
    mine = lax.dynamic_index_in_dim(pair, chip, axis=0, keepdims=False)
    gb, db, mb, vb = adamw("adamw_big", _pack_big([shard(params, nm) for nm in big_names]),
                           _pack_big([shard(mom_m, nm) for nm in big_names]), _pack_big([shard(mom_v, nm) for nm in big_names]),
                           [mine, inbox[0], inbox[1], inbox[2]])
    big_out = [dict(zip(big_names, _unpack_big(t), strict=True)) for t in (gb, db, mb, vb)]
    for out in big_out:
        out["w_up"] = out["w_up"][:, :FF_SHARD]
    gm, dmo, mmo, vmo = adamw("adamw_mod", w_mod[0], m_w_mod[0], v_w_mod[0], [dw_mod])
    mod_out = (gm, dmo, mmo, vmo)

    order = ["c_ctx", "w_mod", "b_mod", "w_in", "b_in", "attn_sink", "q_norm_g", "k_norm_g", "w_branch_a", "w_branch_b",
             "w_out", "ln1_g", "ln1_b", "w_up", "conv_w", "conv_b", "w_down", "ln2_g", "ln2_b"]
    results = [loss, grad_x[None]]
    for kind in range(4):
        for nm in order:
            if nm == "w_mod":
                val = mod_out[kind]
            elif nm in big_out[kind]:
                val = big_out[kind][nm]
            else:
                val = sm_out[kind][nm]
            results.append(val.reshape(params[nm].shape))
    return tuple(results)
```

```python
import functools

import jax
import jax.numpy as jnp
import numpy as np
from jax import lax
from jax.experimental import pallas as pl
from jax.experimental.pallas import tpu as pltpu

F32 = jnp.float32
BF16 = jnp.bfloat16
MXU_DTYPE = BF16

AXES = ("x", "y", "c")
N_DEV = 8
D = 1024
HEAD_DIM = 64
N_HEADS = 8
N_KV = 2
GROUPS = 4
KV_W = GROUPS * HEAD_DIM
Q_W = N_HEADS * HEAD_DIM
GRID_W = 64
WIN = 128
ROPE_THETA = 10000.0
D_FF = 2816
FF_SHARD = 2 * D_FF // N_DEV
FF_SHARD_PAD = 768
FF = N_DEV // 2 * FF_SHARD_PAD
LN_EPS = 1e-5
QK_EPS = 1e-6
N_MOD = 6
ALPHA = 2.0 ** 0.25
Q_SCALE = HEAD_DIM ** -0.5
IN_COLS = 3584
OFF_KA, OFF_VA, OFF_QB, OFF_KB, OFF_VB, OFF_GA = 512, 640, 768, 1280, 1408, 1536
EXT_COLS = 6 * Q_W + 2 * D
X_QA, X_KA, X_VA, X_QB, X_KB, X_VB, X_GL = 0, 512, 1024, 1536, 2048, 2560, 3072
ADAM_LR, ADAM_B1, ADAM_B2, ADAM_EPS, ADAM_WD, ADAM_STEP = 0.001, 0.9, 0.999, 1e-08, 0.01, 10
LANES = 128
TM = 256
VMEM_LIMIT = 56 * 1024 * 1024
ELEMENTWISE_BLOCK_BYTES = 1 << 20
ELEMENTWISE_ROWS = (1824, 1408, 1024, 512, 256, 128, 64, 32, 16, 8)

ANY = pl.BlockSpec(memory_space=pl.ANY)
SDS = jax.ShapeDtypeStruct


def _pick(n, candidates):
    for t in candidates:
        if n % t == 0:
            return t
    raise ValueError(f"no tile for {n}")


def _full(a):
    nd = a.ndim
    return pl.BlockSpec(a.shape, lambda *_: (0,) * nd)


def _rows(tm, w, fn=lambda t: t):
    return pl.BlockSpec((tm, w), lambda i: (fn(i), 0))


def _dot(a, b):
    return jnp.dot(a.astype(MXU_DTYPE), b.astype(MXU_DTYPE), preferred_element_type=F32)


def _dot_nt(a, b):
    return lax.dot_general(a.astype(MXU_DTYPE), b.astype(MXU_DTYPE), (((1,), (1,)), ((), ())), preferred_element_type=F32)


def _dot_tn(a, b):
    return lax.dot_general(a.astype(MXU_DTYPE), b.astype(MXU_DTYPE), (((0,), (0,)), ((), ())), preferred_element_type=F32)


def _cparams(sem):
    return pltpu.CompilerParams(dimension_semantics=sem, vmem_limit_bytes=VMEM_LIMIT)


def all_gather(name, v):
    r, w = v.shape

    def body(x_ref, out_ref, send_sems, recv_sems, local_sem):
        x, y, c = (lax.axis_index(a) for a in AXES)
        me, sibling = (x, y, c), (x, y, 1 - c)
        chips = [(1 - x, y), (x, 1 - y), (1 - x, 1 - y)]

        def rows(px, py, pc):
            return out_ref.at[4 * px + 2 * py + pc]

        def copy(k, block, to, src=None):
            return pltpu.make_async_remote_copy(
                src_ref=rows(*block) if src is None else src, dst_ref=rows(*block),
                send_sem=send_sems.at[k], recv_sem=recv_sems.at[k],
                device_id=to, device_id_type=pl.DeviceIdType.MESH)

        mine = pltpu.make_async_copy(x_ref, rows(*me), local_sem)
        mine.start()
        first = [copy(0, me, sibling, src=x_ref)]
        first += [copy(1 + j, me, (*chip, c), src=x_ref) for j, chip in enumerate(chips)]
        for cp in first:
            cp.start()
        passed = [copy(4 + j, (*chip, c), sibling) for j, chip in enumerate(chips)]
        for j, chip in enumerate(chips):
            copy(1 + j, (*chip, c), me).wait_recv()
            passed[j].start()
        copy(0, sibling, me).wait_recv()
        for j, chip in enumerate(chips):
            copy(4 + j, (*chip, 1 - c), me).wait_recv()
        for cp in first + passed:
            cp.wait_send()
        mine.wait()

    return pl.pallas_call(
        body, name=name, out_shape=SDS((N_DEV, r, w), v.dtype), in_specs=[ANY], out_specs=ANY,
        scratch_shapes=[pltpu.SemaphoreType.DMA((7,)), pltpu.SemaphoreType.DMA((7,)), pltpu.SemaphoreType.DMA],
    )(v)


def exchange(name, outbox, to_chips):
    k = outbox.shape[0]
    assert k == (3 if to_chips else 1)

    def body(out_ref, in_ref, send_sems, recv_sems):
        x, y, c = (lax.axis_index(a) for a in AXES)
        peers = [(x, 1 - y, c), (1 - x, y, c), (1 - x, 1 - y, c)] if to_chips else [(x, y, 1 - c)]
        copies = [
            pltpu.make_async_remote_copy(
                src_ref=out_ref.at[m], dst_ref=in_ref.at[m], send_sem=send_sems.at[m], recv_sem=recv_sems.at[m],
                device_id=peer, device_id_type=pl.DeviceIdType.MESH)
            for m, peer in enumerate(peers)
        ]
        for cp in copies:
            cp.start()
        for cp in copies:
            cp.wait_recv()
        for cp in copies:
            cp.wait_send()

    return pl.pallas_call(
        body, name=name, out_shape=SDS(outbox.shape, outbox.dtype), in_specs=[ANY], out_specs=ANY,
        scratch_shapes=[pltpu.SemaphoreType.DMA((k,)), pltpu.SemaphoreType.DMA((k,))],
    )(outbox)


def rowwise(name, body, *, ntiles, tile_off=0, tiled, full, outs, accs=()):
    nt, nf, no = len(tiled), len(full), len(outs)

    def kern(*refs):
        i = pl.program_id(0)
        out_vals, incs = body(i + tile_off, refs[:nt], refs[nt:nt + nf])
        for r, v in zip(refs[nt + nf:nt + nf + no], out_vals, strict=True):
            r[...] = v.astype(r.dtype)
        acc_refs = refs[nt + nf + no:]

        @pl.when(i == 0)
        def _():
            for r in acc_refs:
                r[...] = jnp.zeros_like(r)

        for r, v in zip(acc_refs, incs, strict=True):
            r[...] += v

    res = pl.pallas_call(
        kern, name=name, grid=(ntiles,),
        in_specs=[s for _, s in tiled] + [_full(a) for a in full],
        out_specs=[s for _, _, s in outs] + [pl.BlockSpec(s, lambda i, n=len(s): (0,) * n) for s in accs],
        out_shape=[SDS(s, d) for s, d, _ in outs] + [SDS(s, F32) for s in accs],
        compiler_params=_cparams(("arbitrary",) if accs else ("parallel",)),
    )(*[a for a, _ in tiled], *full)
    return res[:no], res[no:]


def mm_tn(name, a, b, rows):
    ka, nb = a.shape[1], b.shape[1]
    tr = _pick(rows, (1280, 1024, 768, 512, 256))
    tn = _pick(nb, (512, 256, 128))

    def kern(a_ref, b_ref, o_ref):
        @pl.when(pl.program_id(1) == 0)
        def _():
            o_ref[...] = jnp.zeros_like(o_ref)

        o_ref[...] += _dot_tn(a_ref[...], b_ref[...])

    return pl.pallas_call(
        kern, name=name, grid=(nb // tn, rows // tr),
        in_specs=[pl.BlockSpec((tr, ka), lambda n, r: (r, 0)), pl.BlockSpec((tr, tn), lambda n, r: (r, n))],
        out_specs=pl.BlockSpec((ka, tn), lambda n, r: (0, n)), out_shape=SDS((ka, nb), F32),
        compiler_params=_cparams(("parallel", "arbitrary")),
    )(a, b)


def _swap16(t):
    w = t.shape[1]
    lane = lax.broadcasted_iota(jnp.int32, t.shape, 1)
    return jnp.where((lane & 16) == 0, pltpu.roll(t, w - 16, 1), pltpu.roll(t, 16, 1))


def _rope(t, cos, sin):
    return t * cos + _swap16(t) * sin


def _rope_t(d, cos, sin):
    return d * cos - _swap16(d) * sin


def _seg_sum64(a, bd_ref):
    bd = bd_ref[...]
    hi = a.astype(BF16)
    lo = (a - hi.astype(F32)).astype(BF16)
    return jnp.dot(hi, bd, preferred_element_type=F32) + jnp.dot(lo, bd, preferred_element_type=F32)


def _lane_block(shape):
    return jnp.right_shift(lax.broadcasted_iota(jnp.int32, shape, 1), 6)


def _stack_groups(t, dtype):
    blk = _lane_block(t.shape)
    return jnp.concatenate([jnp.where(blk == g, t, jnp.zeros_like(t)).astype(dtype) for g in range(GROUPS)], axis=0)


def _fold_groups(ts, tq):
    blk = _lane_block((tq, KV_W))
    out = jnp.zeros((tq, KV_W), ts.dtype)
    for g in range(GROUPS):
        out = jnp.where(blk == g, ts[g * tq:(g + 1) * tq], out)
    return out


def _stack_tiles(t, dtype):
    return jnp.concatenate([_stack_groups(t[a:a + TM], dtype) for a in range(0, t.shape[0], TM)], axis=0)


def _fold_tiles(ts, tq):
    return jnp.concatenate([_fold_groups(ts[GROUPS * a:GROUPS * (a + TM)], TM) for a in range(0, tq, TM)], axis=0)


def _compact_tiles_t(tt, dtype):
    return jnp.concatenate([tt[g * HEAD_DIM:(g + 1) * HEAD_DIM, a:a + TM] for a in range(0, tt.shape[1], TM)
                            for g in range(GROUPS)], axis=1).astype(dtype)


def _layer_norm_bwd(dxh, xhat, rstd):
    m1 = jnp.mean(dxh, axis=1, keepdims=True)
    m2 = jnp.mean(dxh * xhat, axis=1, keepdims=True)
    return rstd * (dxh - m1 - xhat * m2)


def _colsum(a):
    return jnp.sum(a, axis=0, keepdims=True)


def _shifted_rows(t, prev_row, next_row):
    n = t.shape[0]
    row = lax.broadcasted_iota(jnp.int32, t.shape, 0)
    up = jnp.where(row == 0, prev_row, pltpu.roll(t, 1, 0))
    dn = jnp.where(row == n - 1, next_row, pltpu.roll(t, n - 1, 0))
    return up, dn


def mod_fwd(cs, w_sh, b_sh):
    def kern(c_ref, w_ref, b_ref, o_ref):
        o_ref[...] = _dot(jax.nn.silu(c_ref[...]), w_ref[...]) + b_ref[...]

    return pl.pallas_call(kern, name="mod_fwd", out_shape=SDS((16, w_sh.shape[1]), F32),
                          compiler_params=pltpu.CompilerParams(vmem_limit_bytes=VMEM_LIMIT))(cs, w_sh, b_sh)


def mod_bwd(cs, w_sh, dm_sh, dm_all):
    hp = lax.Precision.HIGHEST

    def kern(c_ref, w_ref, dm_ref, da_ref, dw_ref, dc_ref, db_ref):
        c = c_ref[...]
        sg = jax.nn.sigmoid(c)
        sc = c * sg
        dm = dm_ref[...]
        dmc = dm_ref[8:9, :]
        for i in range(9, 16):
            dmc = dmc + dm_ref[i:i + 1, :]
        row = lax.broadcasted_iota(jnp.int32, dm.shape, 0)
        a = jnp.where(row < 8, dm, jnp.where(row == 8, dmc, 0.0))
        dw_ref[...] = lax.dot_general(sc, a, (((0,), (0,)), ((), ())), precision=hp, preferred_element_type=F32)
        dsc = lax.dot_general(a, w_ref[...], (((1,), (1,)), ((), ())), precision=hp, preferred_element_type=F32)
        dc_ref[...] = dsc * (sg * (1.0 + c * (1.0 - sg)))
        db = da_ref[0:1, :]
        for i in range(1, 16):
            db = db + da_ref[i:i + 1, :]
        db_ref[...] = db

    return pl.pallas_call(
        kern, name="mod_bwd",
        out_shape=[SDS(w_sh.shape, F32), SDS((16, D), F32), SDS((1, dm_all.shape[1]), F32)],
        compiler_params=pltpu.CompilerParams(vmem_limit_bytes=VMEM_LIMIT))(cs, w_sh, dm_sh, dm_all)


M_SHIFT1, M_SCALE1, M_SHIFTC, M_SCALEC, M_GATE1, M_SHIFT2, M_SCALE2, M_GATE2 = range(8)


def _mrow(ref, k):
    return ref[k:k + 1, :]


def inproj_fwd(xa, cos, sin, modrows, w_ext, b_ext, qg, kg, bd, n_lat_tiles):
    n = xa.shape[0]

    def body(t, vals, fr):
        x, cs, sn = (v[...] for v in vals)
        mod, w, b, qg_r, kg_r, bd_r = fr
        is_ctx = t >= n_lat_tiles
        shift = jnp.where(is_ctx, _mrow(mod, M_SHIFTC), _mrow(mod, M_SHIFT1))
        scale = jnp.where(is_ctx, _mrow(mod, M_SCALEC), _mrow(mod, M_SCALE1))
        hb = (x * (1.0 + scale) + shift).astype(MXU_DTYPE)
        proj = jnp.dot(hb, w[...], preferred_element_type=F32) + b[...]
        cos4 = jnp.concatenate([cs] * 4, axis=1)
        sin4 = jnp.concatenate([sn] * 4, axis=1)
        qa = _rope(proj[:, X_QA:X_QA + Q_W], cos4, sin4) * Q_SCALE
        ka = _rope(proj[:, X_KA:X_KA + Q_W], cos4, sin4)
        va = proj[:, X_VA:X_VA + Q_W]
        tq = proj[:, X_QB:X_QB + Q_W]
        rq = lax.rsqrt(_seg_sum64(tq * tq, bd_r) * (1.0 / HEAD_DIM) + QK_EPS)
        qb = _rope(tq * rq * qg_r[...], cos4, sin4) * Q_SCALE
        tk = proj[:, X_KB:X_KB + Q_W]
        rk = lax.rsqrt(_seg_sum64(tk * tk, bd_r) * (1.0 / HEAD_DIM) + QK_EPS)
        kb = _rope(tk * rk * kg_r[...], cos4, sin4)
        vb = proj[:, X_VB:X_VB + Q_W]
        gl = proj[:, X_GL:]
        return [hb, qa, ka, va, qb, kb, vb, tq, rq, tk, rk, gl], []

    mx = MXU_DTYPE
    outs = [((n, D), mx, _rows(TM, D))] + [((n, Q_W), mx, _rows(TM, Q_W))] * 6 + \
           [((n, Q_W), F32, _rows(TM, Q_W))] * 4 + [((n, 2 * D), F32, _rows(TM, 2 * D))]
    res, _ = rowwise("inproj_fwd", body, ntiles=n // TM,
                     tiled=[(xa, _rows(TM, D)), (cos, _rows(TM, LANES)), (sin, _rows(TM, LANES))],
                     full=[modrows, w_ext, b_ext, qg, kg, bd], outs=outs)
    return res


def merge_fwd(oa, ob, gl, x, modrows, wba, wbb, w_out, s_rows):
    def body(t, vals, fr):
        oa_, ob_, gl_, x_ = (v[...] for v in vals)
        mod, wa, wb, wo = fr
        ya = _dot(oa_, wa[...])
        yb = _dot(ob_, wb[...])
        ga = jax.nn.sigmoid(gl_[:, :D])
        gb = jax.nn.sigmoid(gl_[:, D:])
        mrg = ga * ya + gb * yb
        y = _dot(mrg, wo[...])
        r1 = ALPHA * x_ + _mrow(mod, M_GATE1) * y
        mu = jnp.mean(r1, axis=1, keepdims=True)
        xc = r1 - mu
        var = jnp.mean(xc * xc, axis=1, keepdims=True)
        rstd = lax.rsqrt(var + LN_EPS)
        xhat = xc * rstd
        return [ya, yb, mrg, y, xhat, rstd], []

    outs = [((s_rows, D), F32, _rows(TM, D))] * 2 + [((s_rows, D), MXU_DTYPE, _rows(TM, D))] + \
           [((s_rows, D), F32, _rows(TM, D))] * 2 + [((s_rows, 1), F32, _rows(TM, 1))]
    res, _ = rowwise("merge_fwd", body, ntiles=s_rows // TM,
                     tiled=[(oa, _rows(TM, Q_W)), (ob, _rows(TM, Q_W)), (gl, _rows(TM, 2 * D)), (x, _rows(TM, D))],
                     full=[modrows, wba, wbb, w_out], outs=outs)
    return res


def ffn_up_fwd(xhat1, modrows, ln_g, ln_b, w_up, s_rows):
    def body(t, vals, fr):
        xh = vals[0][...]
        mod, g_r, b_r, w = fr
        x1 = xh * g_r[...] + b_r[...]
        h2 = (x1 * (1.0 + _mrow(mod, M_SCALE2)) + _mrow(mod, M_SHIFT2)).astype(MXU_DTYPE)
        return [h2, jnp.dot(h2, w[...], preferred_element_type=F32)], []

    res, _ = rowwise("ffn_up_fwd", body, ntiles=s_rows // TM, tiled=[(xhat1, _rows(TM, D))],
                     full=[modrows, ln_g, ln_b, w_up],
                     outs=[((s_rows, D), MXU_DTYPE, _rows(TM, D)), ((s_rows, 2 * FF), F32, _rows(TM, 2 * FF))])
    return res


TC = 128


def _halo_specs(tm, w, s_rows):
    per = tm // 8
    last = s_rows // 8 - 1
    return (pl.BlockSpec((8, w), lambda i: (jnp.maximum(i * per - 1, 0), 0)),
            pl.BlockSpec((8, w), lambda i: (jnp.minimum((i + 1) * per, last), 0)))


def _halo_rows(t, ntiles, prev_ref, next_ref):
    prev_row = jnp.where(t == 0, 0.0, prev_ref[7:8, :].astype(F32))
    next_row = jnp.where(t == ntiles - 1, 0.0, next_ref[0:1, :].astype(F32))
    return prev_row, next_row


def conv_swiglu_fwd(u0, conv_w8, conv_b, s_rows):
    w2 = 2 * FF
    nt = s_rows // TC

    def body(t, vals, fr):
        u_ref, pv, nx = vals
        cw, cb = fr
        u = u_ref[...]
        up, dn = _shifted_rows(u, *_halo_rows(t, nt, pv, nx))
        uc = cw[0:1, :] * up + cw[1:2, :] * u + cw[2:3, :] * dn + cb[...]
        gate, val = uc[:, :FF], uc[:, FF:]
        return [gate * jax.nn.sigmoid(gate) * val], []

    hp, hn = _halo_specs(TC, w2, s_rows)
    res, _ = rowwise("conv_swiglu_fwd", body, ntiles=nt,
                     tiled=[(u0, _rows(TC, w2)), (u0, hp), (u0, hn)], full=[conv_w8, conv_b],
                     outs=[((s_rows, FF), MXU_DTYPE, _rows(TC, FF))])
    return res[0]


def ffn_down_loss(a, xhat1, target, modrows, ln1_g, ln1_b, ln2_g, ln2_b, w_down, s_rows):
    def body(t, vals, fr):
        a_, xh1, tgt = (v[...] for v in vals)
        mod, g1, b1, g2, b2, wd = fr
        y2 = jnp.dot(a_, wd[...], preferred_element_type=F32)
        x1 = xh1 * g1[...] + b1[...]
        gate2 = _mrow(mod, M_GATE2)
        r2 = ALPHA * x1 + gate2 * y2
        mu = jnp.mean(r2, axis=1, keepdims=True)
        xc = r2 - mu
        var = jnp.mean(xc * xc, axis=1, keepdims=True)
        rstd = lax.rsqrt(var + LN_EPS)
        xhat = xc * rstd
        out = xhat * g2[...] + b2[...]
        diff = out - tgt
        loss = 0.5 * jnp.sum(jnp.mean(diff * diff, axis=1, keepdims=True), axis=0, keepdims=True)
        dout = diff * (1.0 / D)
        dr2 = _layer_norm_bwd(dout * g2[...], xhat, rstd)
        incs = [loss, _colsum(dout * xhat), _colsum(dout), _colsum(dr2 * y2)]
        return [dr2, dr2 * gate2], incs

    res, accs = rowwise("ffn_down_loss", body, ntiles=s_rows // TM,
                        tiled=[(a, _rows(TM, FF)), (xhat1, _rows(TM, D)), (target, _rows(TM, D))],
                        full=[modrows, ln1_g, ln1_b, ln2_g, ln2_b, w_down],
                        outs=[((s_rows, D), F32, _rows(TM, D)), ((s_rows, D), MXU_DTYPE, _rows(TM, D))],
                        accs=[(1, 1), (1, D), (1, D), (1, D)])
    return res, accs


def ffn_down_bwd(dy2, w_down_t, s_rows):
    def body(t, vals, fr):
        return [jnp.dot(vals[0][...], fr[0][...], preferred_element_type=F32)], []

    res, _ = rowwise("ffn_down_bwd", body, ntiles=s_rows // TM, tiled=[(dy2, _rows(TM, D))], full=[w_down_t],
                     outs=[((s_rows, FF), F32, _rows(TM, FF))])
    return res[0]


def swiglu_conv_bwd(u0, da, conv_w8, conv_b, s_rows):
    w2 = 2 * FF
    nt = s_rows // TC
    n = TC + 16

    def body(t, vals, fr):
        u_ref, upv, unx, da_ref, apv, anx = vals
        cw, cb = fr
        first, last = t == 0, t == nt - 1
        ue = jnp.concatenate([jnp.where(first, 0.0, upv[...]), u_ref[...], jnp.where(last, 0.0, unx[...])], axis=0)
        ae = jnp.concatenate([jnp.where(first, 0.0, apv[...]), da_ref[...], jnp.where(last, 0.0, anx[...])], axis=0)
        up = pltpu.roll(ue, 1, 0)
        dn = pltpu.roll(ue, n - 1, 0)
        uc = cw[0:1, :] * up + cw[1:2, :] * ue + cw[2:3, :] * dn + cb[...]
        gate, val = uc[:, :FF], uc[:, FF:]
        sg = jax.nn.sigmoid(gate)
        du = jnp.concatenate([ae * val * (sg * (1.0 + gate * (1.0 - sg))), ae * (gate * sg)], axis=1)
        du0 = cw[0:1, :] * pltpu.roll(du, n - 1, 0) + cw[1:2, :] * du + cw[2:3, :] * pltpu.roll(du, 1, 0)
        rows = slice(8, 8 + TC)
        dut = du[rows]
        return [du0[rows]], [_colsum(dut), _colsum(up[rows] * dut), _colsum(ue[rows] * dut), _colsum(dn[rows] * dut)]

    hp, hn = _halo_specs(TC, w2, s_rows)
    ap, an = _halo_specs(TC, FF, s_rows)
    res, accs = rowwise("swiglu_conv_bwd", body, ntiles=nt,
                        tiled=[(u0, _rows(TC, w2)), (u0, hp), (u0, hn), (da, _rows(TC, FF)), (da, ap), (da, an)],
                        full=[conv_w8, conv_b], outs=[((s_rows, w2), MXU_DTYPE, _rows(TC, w2))], accs=[(1, w2)] * 4)
    return res[0], accs


def ffn_up_ln1_bwd(du0, dr2, xhat1, y, rstd1, modrows, ln_g, ln_b, w_up_t, s_rows):
    def body(t, vals, fr):
        du0_, dr2_, xh, y_, rstd = (v[...] for v in vals)
        mod, g_r, b_r, wt = fr
        dh2 = jnp.dot(du0_, wt[...], preferred_element_type=F32)
        x1 = xh * g_r[...] + b_r[...]
        dx1 = ALPHA * dr2_ + dh2 * (1.0 + _mrow(mod, M_SCALE2))
        dr1 = _layer_norm_bwd(dx1 * g_r[...], xh, rstd)
        incs = [_colsum(dh2 * x1), _colsum(dh2), _colsum(dx1 * xh), _colsum(dx1), _colsum(dr1 * y_)]
        return [dr1 * _mrow(mod, M_GATE1), ALPHA * dr1], incs

    res, accs = rowwise("ffn_up_ln1_bwd", body, ntiles=s_rows // TM,
                        tiled=[(du0, _rows(TM, 2 * FF)), (dr2, _rows(TM, D)), (xhat1, _rows(TM, D)), (y, _rows(TM, D)),
                               (rstd1, _rows(TM, 1))],
                        full=[modrows, ln_g, ln_b, w_up_t],
                        outs=[((s_rows, D), MXU_DTYPE, _rows(TM, D)), ((s_rows, D), F32, _rows(TM, D))],
                        accs=[(1, D)] * 5)
    return res, accs


def merge_bwd(dy, ya, yb, gl, w_out_t, wba_t, wbb_t, s_rows):
    def body(t, vals, fr):
        dy_, ya_, yb_, gl_ = (v[...] for v in vals)
        wot, wat, wbt = fr
        dmrg = jnp.dot(dy_, wot[...], preferred_element_type=F32)
        ga = jax.nn.sigmoid(gl_[:, :D])
        gb = jax.nn.sigmoid(gl_[:, D:])
        dya = dmrg * ga
        dyb = dmrg * gb
        dgl = jnp.concatenate([dmrg * ya_ * ga * (1.0 - ga), dmrg * yb_ * gb * (1.0 - gb)], axis=1)
        return [dya, dyb, dgl, _dot(dya, wat[...]), _dot(dyb, wbt[...])], []

    mx = MXU_DTYPE
    res, _ = rowwise("merge_bwd", body, ntiles=s_rows // TM,
                     tiled=[(dy, _rows(TM, D)), (ya, _rows(TM, D)), (yb, _rows(TM, D)), (gl, _rows(TM, 2 * D))],
                     full=[w_out_t, wba_t, wbb_t],
                     outs=[((s_rows, D), mx, _rows(TM, D))] * 2 + [((s_rows, 2 * D), F32, _rows(TM, 2 * D))] +
                          [((s_rows, Q_W), F32, _rows(TM, Q_W))] * 2)
    return res


def qk_bwd(dqa, dka, dva, dqb, dkb, dvb, dgl, tq, rq, tk, rk, cos, sin, qg, kg, bd, n_lat_tiles, n):
    def body(t, vals, fr):
        dqa_, dka_, dva_, dqb_, dkb_, dvb_, dgl_, tq_, rq_, tk_, rk_, cs, sn = (v[...] for v in vals)
        qg_r, kg_r, bd_r = fr
        is_ctx = t >= n_lat_tiles
        cos4 = jnp.concatenate([cs] * 4, axis=1)
        sin4 = jnp.concatenate([sn] * 4, axis=1)
        zero = jnp.zeros_like(dqa_)
        dpqa = jnp.where(is_ctx, zero, _rope_t(dqa_, cos4, sin4) * Q_SCALE)
        dpka = _rope_t(dka_, cos4, sin4)
        dpva = dva_
        dnq = jnp.where(is_ctx, zero, _rope_t(dqb_, cos4, sin4) * Q_SCALE)
        gq = qg_r[...] * dnq
        dtq = rq_ * gq - tq_ * (rq_ * rq_ * rq_) * (_seg_sum64(gq * tq_, bd_r) * (1.0 / HEAD_DIM))
        dnk = _rope_t(dkb_, cos4, sin4)
        gk = kg_r[...] * dnk
        dtk = rk_ * gk - tk_ * (rk_ * rk_ * rk_) * (_seg_sum64(gk * tk_, bd_r) * (1.0 / HEAD_DIM))
        dgl32 = jnp.where(is_ctx, jnp.zeros_like(dgl_), dgl_)
        dproj = jnp.concatenate([dpqa, dpka, dpva, dtq, dtk, dvb_, dgl32], axis=1)
        return [dproj], [_colsum(dproj), _colsum(dnq * tq_ * rq_), _colsum(dnk * tk_ * rk_)]

    lat = lambda t: jnp.minimum(t, n_lat_tiles - 1)
    qs = _rows(TM, Q_W)
    res, accs = rowwise(
        "qk_bwd", body, ntiles=n // TM,
        tiled=[(dqa, _rows(TM, Q_W, lat)), (dka, qs), (dva, qs), (dqb, _rows(TM, Q_W, lat)),
               (dkb, qs), (dvb, qs), (dgl, _rows(TM, 2 * D, lat)), (tq, qs), (rq, qs), (tk, qs), (rk, qs),
               (cos, _rows(TM, LANES)), (sin, _rows(TM, LANES))],
        full=[qg, kg, bd], outs=[((n, EXT_COLS), MXU_DTYPE, _rows(TM, EXT_COLS))],
        accs=[(1, EXT_COLS), (1, Q_W), (1, Q_W)])
    return res[0], accs


def inproj_bwd(name, dproj, xa, dxp, modrows, w_ext_t, *, ntiles, tile_off, is_ctx, out_rows):
    kc = M_SCALEC if is_ctx else M_SCALE1

    def body(t, vals, fr):
        dp, x_ = vals[0][...], vals[1][...]
        mod, wt = fr
        dh = jnp.dot(dp, wt[...], preferred_element_type=F32)
        incs = [_colsum(dh * x_), _colsum(dh)]
        if is_ctx:
            return [], incs
        return [vals[2][...] + dh * (1.0 + _mrow(mod, kc))], incs

    tiled = [(dproj, _rows(TM, EXT_COLS, lambda i: i + tile_off)), (xa, _rows(TM, D, lambda i: i + tile_off))]
    outs = []
    if not is_ctx:
        tiled.append((dxp, _rows(TM, D)))
        outs = [((out_rows, D), F32, _rows(TM, D))]
    return rowwise(name, body, ntiles=ntiles, tiled=tiled, full=[modrows, w_ext_t], outs=outs, accs=[(1, D)] * 2)


def _attn_semantics():
    return _cparams(("arbitrary", "arbitrary", "arbitrary"))


GLOB_TK = (1280, 1024, 768, 512, 256)
GLOB_BWD_TQ = (512, 256)
KEY_CHUNK = 256


def glob_fwd(q, kt, v_t, s_rows):
    n = kt.shape[0]
    tq = TM
    tk = _pick(n, GLOB_TK)
    nq, nk = s_rows // tq, n // tk
    r = GROUPS * tq
    nch = tk // KEY_CHUNK

    def produce(qs, k_ref, s_buf, c, mx):
        rows = slice(c * KEY_CHUNK, (c + 1) * KEY_CHUNK)
        sn = _dot_nt(k_ref[rows, :], qs[...])
        s_buf[rows, :] = sn
        return jnp.maximum(mx, jnp.max(sn, axis=0, keepdims=True))

    def kern(q_ref, k0_ref, kn_ref, vt_ref, o_ref, lse_ref, qs, s_buf, mx_buf, m_s, l_s, acc):
        j = pl.program_id(2)

        @pl.when(j == 0)
        def _():
            qs[...] = _stack_groups(q_ref[...], qs.dtype)
            mx = jnp.full((1, r), -jnp.inf, F32)
            for c in range(nch):
                mx = produce(qs, k0_ref, s_buf, c, mx)
            mx_buf[...] = mx
            m_s[...] = jnp.full_like(m_s, -jnp.inf)
            l_s[...] = jnp.zeros_like(l_s)
            acc[...] = jnp.zeros_like(acc)

        m_prev = m_s[...]
        m_new = jnp.maximum(m_prev, mx_buf[...])
        alpha = jnp.exp(m_prev - m_new)
        a = alpha * acc[...]
        ls = alpha * l_s[...]
        mx = jnp.full((1, r), -jnp.inf, F32)
        for c in range(nch):
            rows = slice(c * KEY_CHUNK, (c + 1) * KEY_CHUNK)
            p = jnp.exp(s_buf[rows, :] - m_new)
            ls = ls + jnp.sum(p, axis=0, keepdims=True)
            a = a + jnp.dot(vt_ref[0, :, rows], p.astype(MXU_DTYPE), preferred_element_type=F32)
            mx = produce(qs, kn_ref, s_buf, c, mx)
        mx_buf[...] = mx
        l_s[...] = ls
        acc[...] = a
        m_s[...] = m_new

        @pl.when(j == nk - 1)
        def _():
            o_ref[...] = _untranspose_groups(acc[...] / l_s[...], tq)
            lse_ref[0, 0] = m_s[...] + jnp.log(l_s[...])

    kspec = lambda f: pl.BlockSpec((tk, KV_W), lambda h, i, j: (f(j), h))
    return pl.pallas_call(
        kern, name="glob_fwd", grid=(N_KV, nq, nk),
        in_specs=[pl.BlockSpec((tq, KV_W), lambda h, i, j: (i, h)), kspec(lambda j: 0),
                  kspec(lambda j: jnp.minimum(j + 1, nk - 1)), pl.BlockSpec((1, HEAD_DIM, tk), lambda h, i, j: (h, 0, j))],
        out_specs=[pl.BlockSpec((tq, KV_W), lambda h, i, j: (i, h)),
                   pl.BlockSpec((1, 1, 1, r), lambda h, i, j: (h, i, 0, 0))],
        out_shape=[SDS((s_rows, Q_W), F32), SDS((N_KV, nq, 1, r), F32)],
        scratch_shapes=[pltpu.VMEM((r, KV_W), MXU_DTYPE), pltpu.VMEM((tk, r), F32), pltpu.VMEM((1, r), F32),
                        pltpu.VMEM((1, r), F32), pltpu.VMEM((1, r), F32), pltpu.VMEM((HEAD_DIM, r), F32)],
        compiler_params=_attn_semantics(),
    )(q, kt, kt, v_t)


def attn_delta(o, do, s_rows):
    tq = _pick(s_rows, GLOB_BWD_TQ)
    nq = s_rows // tq
    r = GROUPS * tq

    def kern(o_ref, do_ref, d_ref):
        d_ref[0, 0] = jnp.sum(_stack_tiles(do_ref[...], F32) * _stack_tiles(o_ref[...], F32), axis=1, keepdims=True)

    qspec = pl.BlockSpec((tq, KV_W), lambda h, i: (i, h))
    return pl.pallas_call(
        kern, name="attn_delta", grid=(N_KV, nq), in_specs=[qspec, qspec],
        out_specs=pl.BlockSpec((1, 1, r, 1), lambda h, i: (h, i, 0, 0)), out_shape=SDS((N_KV, nq, r, 1), F32),
        compiler_params=_cparams(("parallel", "parallel")),
    )(o, do)


def _compact_t(tt, dtype):
    return jnp.concatenate([tt[g * HEAD_DIM:(g + 1) * HEAD_DIM, :] for g in range(GROUPS)], axis=1).astype(dtype)


def glob_bwd(q, q_t, kt, vt, do, do_t, lse, delta, h, s_rows):
    n = kt.shape[0]
    tq = _pick(s_rows, GLOB_BWD_TQ)
    tk = _pick(n, GLOB_TK)
    nq, nk = s_rows // tq, n // tk
    r = GROUPS * tq
    nch = tk // KEY_CHUNK

    def kern(q_ref, qt_ref, k_ref, v_ref, do_ref, dot_ref, lse_ref, dl_ref, dq_ref, dkt_ref, dvt_ref, p_buf, ds_buf):
        j = pl.program_id(0)
        i = pl.program_id(1)

        @pl.when(i == 0)
        def _():
            dkt_ref[...] = jnp.zeros_like(dkt_ref)
            dvt_ref[...] = jnp.zeros_like(dvt_ref)

        qs = _stack_tiles(q_ref[...], MXU_DTYPE)
        dos = _stack_tiles(do_ref[...], MXU_DTYPE)
        lse_b = jnp.broadcast_to(lse_ref[0], (r, LANES))
        dl_b = jnp.broadcast_to(dl_ref[0], (r, LANES))
        for c in range(nch):
            lo = c * KEY_CHUNK
            sc = _dot_nt(qs, k_ref[lo:lo + KEY_CHUNK, :])
            dpc = _dot_nt(dos, v_ref[lo:lo + KEY_CHUNK, :])
            for t in range(KEY_CHUNK // LANES):
                sl = slice(t * LANES, (t + 1) * LANES)
                pt = jnp.exp(sc[:, sl] - lse_b)
                p_buf[:, lo + t * LANES:lo + (t + 1) * LANES] = pt.astype(p_buf.dtype)
                ds_buf[:, lo + t * LANES:lo + (t + 1) * LANES] = (pt * (dpc[:, sl] - dl_b)).astype(ds_buf.dtype)
        dq_t = _fold_tiles(jnp.dot(ds_buf[...], k_ref[...], preferred_element_type=F32), tq)
        rows = pl.ds(pl.multiple_of(i * tq, tq), tq)

        @pl.when(j == 0)
        def _():
            dq_ref[rows, :] = dq_t

        @pl.when(j > 0)
        def _():
            dq_ref[rows, :] += dq_t

        dvt_ref[...] += jnp.dot(_compact_tiles_t(dot_ref[...], MXU_DTYPE), p_buf[...], preferred_element_type=F32)
        dkt_ref[...] += jnp.dot(_compact_tiles_t(qt_ref[...], MXU_DTYPE), ds_buf[...], preferred_element_type=F32)

    col = pl.BlockSpec((1, r, 1), lambda j, i: (i, 0, 0))
    qspec = pl.BlockSpec((tq, KV_W), lambda j, i: (i, h))
    tspec = pl.BlockSpec((KV_W, tq), lambda j, i: (h, i))
    kspec = pl.BlockSpec((tk, KV_W), lambda j, i: (j, h))
    ospec = pl.BlockSpec((HEAD_DIM, tk), lambda j, i: (0, j))
    return pl.pallas_call(
        kern, name=f"glob_bwd_h{h}", grid=(nk, nq),
        in_specs=[qspec, tspec, kspec, kspec, qspec, tspec, col, col],
        out_specs=[pl.BlockSpec(memory_space=pltpu.VMEM), ospec, ospec],
        out_shape=[SDS((s_rows, KV_W), F32), SDS((HEAD_DIM, n), F32), SDS((HEAD_DIM, n), F32)],
        scratch_shapes=[pltpu.VMEM((r, tk), MXU_DTYPE), pltpu.VMEM((r, tk), MXU_DTYPE)],
        compiler_params=_cparams(("arbitrary", "arbitrary")),
    )(q, q_t, kt, vt, do, do_t, lse, delta)


TW = 2 * WIN
WR = GROUPS * TW
WLAT = 4 * WIN


def _win_cat(dst, parts):
    off = 0
    for p in parts:
        dst[off:off + p.shape[0], :] = p[...]
        off += p.shape[0]


def _win_specs(s_rows, c_rows):
    nb = s_rows // WIN
    prev = lambda i: jnp.maximum(2 * i - 1, 0)
    nxt = lambda i: jnp.minimum(2 * i + 2, nb - 1)
    rows = [pl.BlockSpec((WIN, KV_W), lambda h, i: (prev(i), h)), pl.BlockSpec((TW, KV_W), lambda h, i: (i, h)),
            pl.BlockSpec((WIN, KV_W), lambda h, i: (nxt(i), h)), pl.BlockSpec((c_rows, KV_W), lambda h, i: (s_rows // c_rows, h))]
    cols = [pl.BlockSpec((1, HEAD_DIM, WIN), lambda h, i: (h, 0, prev(i))), pl.BlockSpec((1, HEAD_DIM, TW), lambda h, i: (h, 0, i)),
            pl.BlockSpec((1, HEAD_DIM, WIN), lambda h, i: (h, 0, nxt(i))),
            pl.BlockSpec((1, HEAD_DIM, c_rows), lambda h, i: (h, 0, s_rows // c_rows))]
    return rows, cols


def _win_mask(i, s_rows, shape, keys_on_rows):
    a = lax.broadcasted_iota(jnp.int32, shape, 0)
    b = lax.broadcasted_iota(jnp.int32, shape, 1)
    kk, qq = (a, b) if keys_on_rows else (b, a)
    qpos = i * TW + (qq & (TW - 1))
    kpos = (2 * i - 1) * WIN + kk
    band = (jnp.abs(qpos - kpos) <= WIN) & (kpos >= 0) & (kpos < s_rows)
    return (kk >= WLAT) | band


def _untranspose_groups(o_t, tq):
    row = lax.broadcasted_iota(jnp.int32, (HEAD_DIM, KV_W), 0)
    col = lax.broadcasted_iota(jnp.int32, (HEAD_DIM, KV_W), 1)
    hi = o_t.astype(BF16)
    r1 = o_t - hi.astype(F32)
    mid = r1.astype(BF16)
    lo = (r1 - mid.astype(F32)).astype(BF16)
    o = jnp.zeros((tq, KV_W), F32)
    for g in range(GROUPS):
        sel = jnp.where(col == row + g * HEAD_DIM, 1.0, 0.0).astype(BF16)
        for term in (hi, mid, lo):
            o = o + lax.dot_general(term[:, g * tq:(g + 1) * tq], sel, (((0,), (0,)), ((), ())), preferred_element_type=F32)
    return o


def win_fwd(q, kt, v_t, sinkrow, s_rows, c_rows):
    nt = s_rows // TW
    nkeys = WLAT + c_rows

    def kern(q_ref, kp, kc, kn, kx, vp, vc, vn, vx, sink_ref, o_ref, lse_ref, kcat):
        i = pl.program_id(1)
        _win_cat(kcat, (kp, kc, kn, kx))
        qs = _stack_groups(q_ref[...], MXU_DTYPE)
        st = _dot_nt(kcat[...], qs)
        st = jnp.where(_win_mask(i, s_rows, st.shape, True), st, -jnp.inf)
        sink = sink_ref[0]
        m = jnp.maximum(jnp.max(st, axis=0, keepdims=True), sink)
        e = jnp.exp(st - m)
        den = jnp.sum(e, axis=0, keepdims=True) + jnp.exp(sink - m)
        v_cat = jnp.concatenate([vp[0], vc[0], vn[0], vx[0]], axis=1)
        o_t = jnp.dot(v_cat, e.astype(MXU_DTYPE), preferred_element_type=F32) / den
        o_ref[...] = _untranspose_groups(o_t, TW)
        lse_ref[0, 0] = m + jnp.log(den)

    rows, cols = _win_specs(s_rows, c_rows)
    qspec = pl.BlockSpec((TW, KV_W), lambda h, i: (i, h))
    rowv = pl.BlockSpec((1, 1, 1, WR), lambda h, i: (h, i, 0, 0))
    return pl.pallas_call(
        kern, name="win_fwd", grid=(N_KV, nt),
        in_specs=[qspec] + rows + cols + [pl.BlockSpec((1, 1, WR), lambda h, i: (h, 0, 0))],
        out_specs=[qspec, rowv], out_shape=[SDS((s_rows, Q_W), F32), SDS((N_KV, nt, 1, WR), F32)],
        scratch_shapes=[pltpu.VMEM((nkeys, KV_W), MXU_DTYPE)],
        compiler_params=_cparams(("parallel", "parallel")),
    )(q, kt, kt, kt, kt, v_t, v_t, v_t, v_t, sinkrow)


def win_bwd(q, q_t, kt, vt, sinkcol, o, do, do_t, lse, s_rows, c_rows):
    nt = s_rows // TW
    nkeys = WLAT + c_rows
    n = s_rows + c_rows
    ctx0 = WIN + s_rows

    def kern(q_ref, qt_ref, kp, kc, kn, kx, vp, vc, vn, vx, sink_ref, o_ref, do_ref, dot_ref, lse_ref,
             dq_ref, dkt_ref, dvt_ref, dsk_ref, kcat, vcat):
        i = pl.program_id(1)

        @pl.when(i == 0)
        def _():
            dkt_ref[...] = jnp.zeros_like(dkt_ref)
            dvt_ref[...] = jnp.zeros_like(dvt_ref)
            dsk_ref[...] = jnp.zeros_like(dsk_ref)

        _win_cat(kcat, (kp, kc, kn, kx))
        _win_cat(vcat, (vp, vc, vn, vx))
        qs = _stack_groups(q_ref[...], MXU_DTYPE)
        do32 = _stack_groups(do_ref[...], F32)
        delta = jnp.sum(do32 * _stack_groups(o_ref[...], F32), axis=1, keepdims=True)
        dos = do32.astype(MXU_DTYPE)
        lse_c = lse_ref[0, 0]
        s = _dot_nt(qs, kcat[...])
        s = jnp.where(_win_mask(i, s_rows, s.shape, False), s, -jnp.inf)
        p = jnp.exp(s - lse_c)
        ds = p * (_dot_nt(dos, vcat[...]) - delta)
        dq_ref[...] = _fold_groups(_dot(ds, kcat[...]), TW)
        dvt = jnp.dot(_compact_t(dot_ref[...], MXU_DTYPE), p.astype(MXU_DTYPE), preferred_element_type=F32)
        dkt = jnp.dot(_compact_t(qt_ref[...], MXU_DTYPE), ds.astype(MXU_DTYPE), preferred_element_type=F32)
        lat = pl.ds(pl.multiple_of(i * TW, TW), WLAT)
        dkt_ref[0, :, lat] += dkt[:, :WLAT]
        dvt_ref[0, :, lat] += dvt[:, :WLAT]
        dkt_ref[0, :, ctx0:ctx0 + c_rows] += dkt[:, WLAT:]
        dvt_ref[0, :, ctx0:ctx0 + c_rows] += dvt[:, WLAT:]
        dsk_ref[0] += -(jnp.exp(sink_ref[0][:, 0:1] - lse_c) * delta)

    rows, _ = _win_specs(s_rows, c_rows)
    qspec = pl.BlockSpec((TW, KV_W), lambda h, i: (i, h))
    tspec = pl.BlockSpec((KV_W, TW), lambda h, i: (h, i))
    col = pl.BlockSpec((1, 1, WR, 1), lambda h, i: (h, i, 0, 0))
    kvt = pl.BlockSpec((1, HEAD_DIM, WIN + n), lambda h, i: (h, 0, 0))
    return pl.pallas_call(
        kern, name="win_bwd", grid=(N_KV, nt),
        in_specs=[qspec, tspec] + rows + rows + [pl.BlockSpec((1, WR, LANES), lambda h, i: (h, 0, 0)), qspec, qspec, tspec, col],
        out_specs=[qspec, kvt, kvt, pl.BlockSpec((1, WR, 1), lambda h, i: (h, 0, 0))],
        out_shape=[SDS((s_rows, Q_W), F32), SDS((N_KV, HEAD_DIM, WIN + n), F32), SDS((N_KV, HEAD_DIM, WIN + n), F32),
                   SDS((N_KV, WR, 1), F32)],
        scratch_shapes=[pltpu.VMEM((nkeys, KV_W), MXU_DTYPE), pltpu.VMEM((nkeys, KV_W), MXU_DTYPE)],
        compiler_params=_cparams(("arbitrary", "arbitrary")),
    )(q, q_t, kt, kt, kt, kt, vt, vt, vt, vt, sinkcol, o, do, do_t, lse)


def adamw(name, w, m, v, grads):
    r, wd = w.shape
    tr = _pick(r, [t for t in ELEMENTWISE_ROWS if t * wd * 4 <= ELEMENTWISE_BLOCK_BYTES])
    stacked = not isinstance(grads, (list, tuple))
    ng = grads.shape[0] if stacked else len(grads)

    def kern(*refs):
        w_ref, m_ref, v_ref = refs[:3]
        g_refs = refs[3:-4]
        g_out, d_out, m_out, v_out = refs[-4:]
        if stacked:
            g = g_refs[0][0]
            for k in range(1, ng):
                g = g + g_refs[0][k]
        else:
            g = g_refs[0][...]
            for gr in g_refs[1:]:
                g = g + gr[...]
        wv = w_ref[...]
        mn = ADAM_B1 * m_ref[...] + (1.0 - ADAM_B1) * g
        vn = ADAM_B2 * v_ref[...] + (1.0 - ADAM_B2) * (g * g)
        m_hat = mn / (1.0 - ADAM_B1 ** ADAM_STEP)
        v_hat = vn / (1.0 - ADAM_B2 ** ADAM_STEP)
        g_out[...] = g
        d_out[...] = -ADAM_LR * (m_hat / (jnp.sqrt(v_hat) + ADAM_EPS) + ADAM_WD * wv)
        m_out[...] = mn
        v_out[...] = vn

    spec = pl.BlockSpec((tr, wd), lambda i: (i, 0))
    gspecs = [pl.BlockSpec((ng, tr, wd), lambda i: (0, i, 0))] if stacked else [spec] * ng
    return pl.pallas_call(
        kern, name=name, grid=(r // tr,), in_specs=[spec] * 3 + gspecs, out_specs=[spec] * 4,
        out_shape=[SDS((r, wd), F32)] * 4, compiler_params=_cparams(("parallel",)),
    )(w, m, v, *([grads] if stacked else grads))


def add2(name, a, b):
    k, r, w = a.shape
    tr = _pick(r, [t for t in ELEMENTWISE_ROWS if t * w * 4 <= ELEMENTWISE_BLOCK_BYTES])

    def kern(a_ref, b_ref, o_ref):
        o_ref[...] = a_ref[...] + b_ref[...]

    spec = pl.BlockSpec((1, tr, w), lambda s, i: (s, i, 0))
    return pl.pallas_call(kern, name=name, grid=(k, r // tr), in_specs=[spec, spec], out_specs=spec,
                          out_shape=SDS(a.shape, a.dtype), compiler_params=_cparams(("parallel", "parallel")))(a, b)


def _rep4(a, off):
    return jnp.concatenate([a[:, off + HEAD_DIM * h: off + HEAD_DIM * (h + 1)] for h in range(N_KV) for _ in range(GROUPS)], axis=1)


def _extend_cols(a):
    return jnp.concatenate([a[:, 0:OFF_KA], _rep4(a, OFF_KA), _rep4(a, OFF_VA), a[:, OFF_QB:OFF_KB],
                            _rep4(a, OFF_KB), _rep4(a, OFF_VB), a[:, OFF_GA:]], axis=1)


def _fold4(a, off):
    r = a.shape[0]
    return a[:, off:off + Q_W].reshape(r, N_KV, GROUPS, HEAD_DIM).sum(axis=2).reshape(r, N_KV * HEAD_DIM)


def _fold_cols(a):
    return jnp.concatenate([a[:, X_QA:X_QA + Q_W], _fold4(a, X_KA), _fold4(a, X_VA), a[:, X_QB:X_QB + Q_W],
                            _fold4(a, X_KB), _fold4(a, X_VB), a[:, X_GL:]], axis=1)


def _rope_tables(s_rows, c_rows):
    pos = jnp.arange(s_rows, dtype=jnp.int32)
    rows = (pos // GRID_W).astype(F32)
    cols = (pos % GRID_W).astype(F32)
    n_freq = HEAD_DIM // 4
    inv_freq = ROPE_THETA ** (-jnp.arange(n_freq, dtype=F32) / n_freq)
    ang_r = rows[:, None] * inv_freq
    ang_c = cols[:, None] * inv_freq
    cos = jnp.concatenate([jnp.cos(ang_r)] * 2 + [jnp.cos(ang_c)] * 2, axis=1)
    sin = jnp.concatenate([-jnp.sin(ang_r), jnp.sin(ang_r), -jnp.sin(ang_c), jnp.sin(ang_c)], axis=1)
    cos = jnp.concatenate([cos, jnp.ones((c_rows, HEAD_DIM), F32)], axis=0)
    sin = jnp.concatenate([sin, jnp.zeros((c_rows, HEAD_DIM), F32)], axis=0)
    return jnp.concatenate([cos, cos], axis=1), jnp.concatenate([sin, sin], axis=1)


def _ff_pad_cols(a):
    r = a.shape[0]
    a = jnp.pad(a.reshape(r, N_DEV, FF_SHARD), ((0, 0), (0, 0), (0, FF_SHARD_PAD - FF_SHARD)))
    return a.reshape(r, 2 * FF)


def _ff_unpad_cols(a):
    r = a.shape[0]
    return a.reshape(r, N_DEV, FF_SHARD_PAD)[:, :, :FF_SHARD].reshape(r, 2 * D_FF)


def _ff_pad_rows(a):
    c = a.shape[1]
    a = jnp.pad(a.reshape(N_DEV // 2, FF_SHARD, c), ((0, 0), (0, FF_SHARD_PAD - FF_SHARD), (0, 0)))
    return a.reshape(FF, c)


def _ff_unpad_rows(a):
    c = a.shape[1]
    return a.reshape(N_DEV // 2, FF_SHARD_PAD, c)[:, :FF_SHARD].reshape(D_FF, c)


BIG = (("w_in", (D, IN_COLS // N_DEV)), ("w_branch_a", (Q_W, D // N_DEV)), ("w_branch_b", (Q_W, D // N_DEV)),
       ("w_out", (D // N_DEV, D)), ("w_up", (D, FF_SHARD_PAD)), ("w_down", (D_FF // N_DEV, D)))
BIG_SIZES = tuple(int(np.prod(s)) for _, s in BIG)
BIG_ROWS = sum(BIG_SIZES) // LANES


def _pack_big(parts):
    return jnp.concatenate([p.reshape(-1) for p in parts]).reshape(BIG_ROWS, LANES)


def _unpack_big(flat):
    lead = flat.shape[:-2]
    f = flat.reshape(*lead, BIG_ROWS * LANES)
    out, off = [], 0
    for (_, shp), sz in zip(BIG, BIG_SIZES, strict=True):
        out.append(f[..., off:off + sz].reshape(*lead, *shp))
        off += sz
    return out


def _cols_to_full(g):
    return jnp.transpose(g, (1, 0, 2)).reshape(g.shape[1], -1)


def _full_to_cols(a):
    r, c = a.shape
    return jnp.transpose(a.reshape(r, N_DEV, c // N_DEV), (1, 0, 2))


SMALL = (("c_ctx", D), ("b_mod", N_MOD * D), ("b_in", IN_COLS), ("attn_sink", N_HEADS), ("q_norm_g", HEAD_DIM),
         ("k_norm_g", HEAD_DIM), ("ln1_g", D), ("ln1_b", D), ("conv_w", 3 * 2 * D_FF // N_DEV), ("conv_b", 2 * D_FF),
         ("ln2_g", D), ("ln2_b", D))
SMALL_TOTAL = sum(n for _, n in SMALL)
SMALL_ROWS = -(-SMALL_TOTAL // (8 * LANES)) * 8


def _pack_small(parts):
    flat = jnp.concatenate([p.reshape(-1).astype(F32) for p in parts])
    return jnp.pad(flat, (0, SMALL_ROWS * LANES - flat.shape[0])).reshape(SMALL_ROWS, LANES)


def _unpack_small(packed):
    f = packed.reshape(-1)
    out, off = {}, 0
    for name, n in SMALL:
        out[name] = f[off:off + n]
        off += n
    return out


RED = (("c_ctx", D), ("b_in", IN_COLS), ("attn_sink", N_HEADS), ("q_norm_g", HEAD_DIM), ("k_norm_g", HEAD_DIM),
       ("ln1_g", D), ("ln1_b", D), ("conv_w", 3 * 2 * FF), ("conv_b", 2 * FF), ("ln2_g", D), ("ln2_b", D))
RED_TOTAL = sum(n for _, n in RED)
RED_ROWS = -(-RED_TOTAL // (8 * LANES)) * 8


def sum8(name, g):
    _, r, w = g.shape

    def kern(g_ref, o_ref):
        acc = g_ref[0]
        for k in range(1, N_DEV):
            acc = acc + g_ref[k]
        o_ref[...] = acc

    return pl.pallas_call(kern, name=name, out_shape=SDS((r, w), F32))(g)


def _local_step(x, ctx, target, modrows, weights, small):
    s_rows, c_rows = x.shape[0], ctx.shape[0]
    n = s_rows + c_rows
    nl = s_rows // TM
    w_in, wba, wbb, w_out, w_up, w_down = weights
    f = lambda a: a.reshape(1, -1).astype(F32)
    b_in, ln1_g, ln1_b, ln2_g, ln2_b, conv_b = (f(small[k]) for k in ("b_in", "ln1_g", "ln1_b", "ln2_g", "ln2_b", "conv_b"))
    conv_w8 = jnp.pad(small["conv_w_full"], ((0, 5), (0, 0)))
    qg = jnp.tile(small["q_norm_g"].reshape(1, HEAD_DIM), (1, N_HEADS))
    kg = jnp.tile(small["k_norm_g"].reshape(1, HEAD_DIM), (1, N_HEADS))
    sink_rep = jnp.repeat(small["attn_sink"].reshape(N_KV, GROUPS), TW, axis=1)
    sinkrow = sink_rep.reshape(N_KV, 1, WR)
    sinkcol = jnp.broadcast_to(sink_rep[:, :, None], (N_KV, WR, LANES))
    bd = jnp.kron(jnp.eye(N_HEADS, dtype=F32), jnp.ones((HEAD_DIM, HEAD_DIM), F32)).astype(BF16)
    cos, sin = _rope_tables(s_rows, c_rows)
    w_ext = _extend_cols(w_in)
    b_ext = _extend_cols(b_in)
    xa = jnp.concatenate([x, ctx], axis=0)

    hb, qa, kat, vat, qb, kbt, vbt, tq, rq, tk, rk, gl = inproj_fwd(xa, cos, sin, modrows, w_ext, b_ext, qg, kg, bd, nl)
    compact_t = lambda t: jnp.stack([t[:, h * KV_W:h * KV_W + HEAD_DIM].T for h in range(N_KV)])
    oa, lse_a = win_fwd(qa, kat, compact_t(vat), sinkrow, s_rows, c_rows)
    lse_a = lse_a.reshape(N_KV, s_rows // TW, WR, 1)
    vb_t = compact_t(vbt)
    ob, lse_b = glob_fwd(qb, kbt, vb_t, s_rows)
    tqb = _pick(s_rows, GLOB_BWD_TQ)
    lse_b = lse_b.reshape(N_KV, s_rows // tqb, GROUPS * tqb, 1)
    ya, yb, mrg, y, xhat1, rstd1 = merge_fwd(oa, ob, gl, x, modrows, wba, wbb, w_out, s_rows)
    h2, u0 = ffn_up_fwd(xhat1, modrows, ln1_g, ln1_b, w_up, s_rows)
    a = conv_swiglu_fwd(u0, conv_w8, conv_b, s_rows)
    (dr2, dy2), (loss, dln2_g, dln2_b, dgate2) = ffn_down_loss(a, xhat1, target, modrows, ln1_g, ln1_b, ln2_g, ln2_b, w_down, s_rows)

    da = ffn_down_bwd(dy2, w_down.T, s_rows)
    dw_down = mm_tn("dw_down", a, dy2, s_rows)
    du0, (dconv_b, dcw0, dcw1, dcw2) = swiglu_conv_bwd(u0, da, conv_w8, conv_b, s_rows)
    dw_up = mm_tn("dw_up", h2, du0, s_rows)
    (dy, dxp), (dscale2, dshift2, dln1_g, dln1_b, dgate1) = ffn_up_ln1_bwd(du0, dr2, xhat1, y, rstd1, modrows, ln1_g, ln1_b, w_up.T, s_rows)
    dya, dyb, dgl, doa, dob = merge_bwd(dy, ya, yb, gl, w_out.T, wba.T, wbb.T, s_rows)
    dw_out = mm_tn("dw_out", mrg, dy, s_rows)
    dwba = mm_tn("dw_branch_a", oa, dya, s_rows)
    dwbb = mm_tn("dw_branch_b", ob, dyb, s_rows)

    pad = jnp.zeros((n, KV_W - HEAD_DIM), F32)
    spread = lambda per_head: jnp.concatenate([t for th in per_head for t in (th.T, pad)], axis=1)
    dqa, dka_t, dva_t, dsk = win_bwd(qa, qa[:s_rows].T, kat, vat, sinkcol, oa, doa, doa.astype(MXU_DTYPE).T, lse_a,
                                     s_rows, c_rows)
    dka = spread([dka_t[h, :, WIN:] for h in range(N_KV)])
    dva = spread([dva_t[h, :, WIN:] for h in range(N_KV)])
    delta_b = attn_delta(ob, dob, s_rows)
    qb_t = qb[:s_rows].T
    dob_t = dob.astype(MXU_DTYPE).T
    heads = [glob_bwd(qb, qb_t, kbt, vbt, dob, dob_t, lse_b[h], delta_b[h], h, s_rows) for h in range(N_KV)]
    dqb = jnp.concatenate([hd[0] for hd in heads], axis=1)
    dkb = spread([hd[1] for hd in heads])
    dvb = spread([hd[2] for hd in heads])
    dproj, (db_ext, dqg, dkg) = qk_bwd(dqa, dka, dva, dqb, dkb, dvb, dgl, tq, rq, tk, rk, cos, sin, qg, kg, bd, nl, n)
    w_ext_t = w_ext.T
    (grad_x,), (dscale1, dshift1) = inproj_bwd("inproj_bwd", dproj, xa, dxp, modrows, w_ext_t, ntiles=nl, tile_off=0,
                                               is_ctx=False, out_rows=s_rows)
    _, (dscale_c, dshift_c) = inproj_bwd("inproj_bwd_ctx", dproj, xa, None, modrows, w_ext_t, ntiles=c_rows // TM,
                                         tile_off=nl, is_ctx=True, out_rows=0)
    dw_in = _fold_cols(mm_tn("dw_in", hb, dproj, n))

    dmod = jnp.concatenate([dshift1, dscale1, dgate1, dshift2, dscale2, dgate2], axis=1)
    dmod_c = jnp.concatenate([dshift_c, dscale_c, jnp.zeros((1, (N_MOD - 2) * D), F32)], axis=1)
    fold_g = lambda t: t.reshape(N_HEADS, HEAD_DIM).sum(axis=0)
    red = {
        "b_in": _fold_cols(db_ext), "attn_sink": dsk.reshape(N_HEADS, TW).sum(axis=1), "q_norm_g": fold_g(dqg),
        "k_norm_g": fold_g(dkg), "ln1_g": dln1_g, "ln1_b": dln1_b, "conv_w": jnp.concatenate([dcw0, dcw1, dcw2], axis=0),
        "conv_b": dconv_b, "ln2_g": dln2_g, "ln2_b": dln2_b,
    }
    return loss[0, 0], grad_x, (dw_in, dwba, dwbb, dw_out, dw_up, dw_down), dmod, dmod_c, red


def kernel(x, c, ctx, c_ctx, w_mod, b_mod, w_in, b_in, attn_sink, q_norm_g, k_norm_g, w_branch_a, w_branch_b, w_out, ln1_g, ln1_b, w_up, conv_w, conv_b, w_down, ln2_g, ln2_b, loss_target, m_c_ctx, m_w_mod, m_b_mod, m_w_in, m_b_in, m_attn_sink, m_q_norm_g, m_k_norm_g, m_w_branch_a, m_w_branch_b, m_w_out, m_ln1_g, m_ln1_b, m_w_up, m_conv_w, m_conv_b, m_w_down, m_ln2_g, m_ln2_b, v_c_ctx, v_w_mod, v_b_mod, v_w_in, v_b_in, v_attn_sink, v_q_norm_g, v_k_norm_g, v_w_branch_a, v_w_branch_b, v_w_out, v_ln1_g, v_ln1_b, v_w_up, v_conv_w, v_conv_b, v_w_down, v_ln2_g, v_ln2_b):
    ax, ay, ac = (lax.axis_index(a) for a in AXES)
    me = 4 * ax + 2 * ay + ac
    chip = 2 * ax + ay
    mod_w = N_MOD * D // N_DEV
    params = dict(c_ctx=c_ctx, w_mod=w_mod, b_mod=b_mod, w_in=w_in, b_in=b_in, attn_sink=attn_sink, q_norm_g=q_norm_g,
                  k_norm_g=k_norm_g, w_branch_a=w_branch_a, w_branch_b=w_branch_b, w_out=w_out, ln1_g=ln1_g, ln1_b=ln1_b,
                  w_up=w_up, conv_w=conv_w, conv_b=conv_b, w_down=w_down, ln2_g=ln2_g, ln2_b=ln2_b)
    mom_m = dict(c_ctx=m_c_ctx, w_mod=m_w_mod, b_mod=m_b_mod, w_in=m_w_in, b_in=m_b_in, attn_sink=m_attn_sink,
                 q_norm_g=m_q_norm_g, k_norm_g=m_k_norm_g, w_branch_a=m_w_branch_a, w_branch_b=m_w_branch_b, w_out=m_w_out,
                 ln1_g=m_ln1_g, ln1_b=m_ln1_b, w_up=m_w_up, conv_w=m_conv_w, conv_b=m_conv_b, w_down=m_w_down,
                 ln2_g=m_ln2_g, ln2_b=m_ln2_b)
    mom_v = dict(c_ctx=v_c_ctx, w_mod=v_w_mod, b_mod=v_b_mod, w_in=v_w_in, b_in=v_b_in, attn_sink=v_attn_sink,
                 q_norm_g=v_q_norm_g, k_norm_g=v_k_norm_g, w_branch_a=v_w_branch_a, w_branch_b=v_w_branch_b, w_out=v_w_out,
                 ln1_g=v_ln1_g, ln1_b=v_ln1_b, w_up=v_w_up, conv_w=v_conv_w, conv_b=v_conv_b, w_down=v_w_down,
                 ln2_g=v_ln2_g, ln2_b=v_ln2_b)
    big_names = [nm for nm, _ in BIG]

    def shard(tree, nm):
        t = tree[nm][0]
        return jnp.pad(t, ((0, 0), (0, FF_SHARD_PAD - FF_SHARD))) if nm == "w_up" else t

    wg = all_gather("ag_weights", _pack_big([shard(params, nm).astype(MXU_DTYPE) for nm in big_names]))
    g_in, g_ba, g_bb, g_out, g_up, g_down = _unpack_big(wg)
    weights = (_cols_to_full(g_in), _cols_to_full(g_ba), _cols_to_full(g_bb), g_out.reshape(D, D), _cols_to_full(g_up),
               _ff_pad_rows(g_down.reshape(D_FF, D)))

    c_all = all_gather("ag_c", c.reshape(8, LANES)).reshape(N_DEV, D)
    cs = jnp.concatenate([c_all, c_ctx.reshape(1, D), jnp.zeros((7, D), F32)], axis=0)
    w_mod_sh = w_mod[0]
    b_mod_sh = lax.dynamic_slice(b_mod, (0, me * mod_w), (1, mod_w))
    mod_part = mod_fwd(cs, w_mod_sh, b_mod_sh)
    mg = all_gather("ag_mod", mod_part.reshape(16 * mod_w // LANES, LANES)).reshape(N_DEV, 16, mod_w)
    mod = lax.dynamic_index_in_dim(mg, me, axis=1, keepdims=False).reshape(N_MOD, D)
    mod_c = mg[:, 8, :].reshape(N_MOD, D)
    modrows = jnp.stack([mod[0], mod[1], mod_c[0], mod_c[1], mod[2], mod[3], mod[4], mod[5]], axis=0)

    conv_w_full = all_gather("ag_conv_w", jnp.pad(conv_w[0], ((0, 5), (0, FF_SHARD_PAD - FF_SHARD))))
    conv_w_full = _cols_to_full(conv_w_full[:, :3, :])
    small = dict(b_in=b_in, ln1_g=ln1_g, ln1_b=ln1_b, ln2_g=ln2_g, ln2_b=ln2_b, conv_b=_ff_pad_cols(conv_b),
                 conv_w_full=conv_w_full, q_norm_g=q_norm_g, k_norm_g=k_norm_g, attn_sink=attn_sink)
    loss, grad_x, big_grads, dmod, dmod_c, red = _local_step(x[0], ctx[0], loss_target[0], modrows, weights, small)
    loss = lax.psum(loss, AXES)

    dm = all_gather("ag_dmod", jnp.concatenate([dmod, dmod_c], axis=0).reshape(2 * N_MOD * D // LANES, LANES))
    dm = dm.reshape(N_DEV, 2, N_MOD * D)
    dm_all = jnp.concatenate([dm[:, 0], dm[:, 1]], axis=0)
    dm_sh = lax.dynamic_slice(dm_all, (0, me * mod_w), (16, mod_w))
    dw_mod, dcc, db_mod = mod_bwd(cs, w_mod_sh, dm_sh, dm_all)
    red["c_ctx"] = dcc[8]

    red_vec = jnp.concatenate([red[nm].reshape(-1) for nm, _ in RED])
    red_vec = jnp.pad(red_vec, (0, RED_ROWS * LANES - RED_TOTAL)).reshape(RED_ROWS, LANES)
    red_sum = sum8("sum_small", all_gather("ag_small", red_vec)).reshape(-1)
    gsm, off = {}, 0
    for nm, k in RED:
        gsm[nm] = red_sum[off:off + k]
        off += k
    gsm["b_mod"] = db_mod.reshape(-1)
    gsm["conv_b"] = _ff_unpad_cols(gsm["conv_b"].reshape(1, 2 * FF))
    gsm["conv_w"] = lax.dynamic_slice(gsm["conv_w"].reshape(3, 2 * FF), (0, me * FF_SHARD_PAD), (3, FF_SHARD_PAD))[:, :FF_SHARD]
    sm_names = [nm for nm, _ in SMALL]
    gs, ds, ms, vs = adamw("adamw_small", _pack_small([params[nm] for nm in sm_names]),
                           _pack_small([mom_m[nm] for nm in sm_names]), _pack_small([mom_v[nm] for nm in sm_names]),
                           [_pack_small([gsm[nm] for nm in sm_names])])
    sm_out = [_unpack_small(t) for t in (gs, ds, ms, vs)]

    dw_in, dwba, dwbb, dw_out, dw_up, dw_down = big_grads
    slabs = jnp.concatenate([t.reshape(N_DEV, -1) for t in (
        _full_to_cols(dw_in), _full_to_cols(dwba), _full_to_cols(dwbb), dw_out, _full_to_cols(dw_up),
        _ff_unpad_rows(dw_down))], axis=1)
    by_core = slabs.reshape(4, 2, BIG_ROWS, LANES)
    keep = lax.dynamic_index_in_dim(by_core, ac, axis=1, keepdims=False)
    give = lax.dynamic_index_in_dim(by_core, 1 - ac, axis=1, keepdims=False)
    got = exchange("rs_sibling", give.reshape(1, 4 * BIG_ROWS, LANES), to_chips=False).reshape(4, BIG_ROWS, LANES)
    pair = add2("rs_pair_sum", keep, got)
    outbox = jnp.stack([lax.dynamic_index_in_dim(pair, jnp.bitwise_xor(chip, m), axis=0, keepdims=False) for m in (1, 2, 3)])
    inbox = exchange("rs_chips", outbox, to_chips=True)
    mine = lax.dynamic_index_in_dim(pair, chip, axis=0, keepdims=False)
    gb, db, mb, vb = adamw("adamw_big", _pack_big([shard(params, nm) for nm in big_names]),
                           _pack_big([shard(mom_m, nm) for nm in big_names]), _pack_big([shard(mom_v, nm) for nm in big_names]),
                           [mine, inbox[0], inbox[1], inbox[2]])
    big_out = [dict(zip(big_names, _unpack_big(t), strict=True)) for t in (gb, db, mb, vb)]
    for out in big_out:
        out["w_up"] = out["w_up"][:, :FF_SHARD]
    gm, dmo, mmo, vmo = adamw("adamw_mod", w_mod[0], m_w_mod[0], v_w_mod[0], [dw_mod])
    mod_out = (gm, dmo, mmo, vmo)

    order = ["c_ctx", "w_mod", "b_mod", "w_in", "b_in", "attn_sink", "q_norm_g", "k_norm_g", "w_branch_a", "w_branch_b",
             "w_out", "ln1_g", "ln1_b", "w_up", "conv_w", "conv_b", "w_down", "ln2_g", "ln2_b"]
    results = [loss, grad_x[None]]
    for kind in range(4):
        for nm in order:
            if nm == "w_mod":
                val = mod_out[kind]
            elif nm in big_out[kind]:
                val = big_out[kind][nm]
            else:
                val = sm_out[kind][nm]
            results.append(val.reshape(params[nm].shape))
    return tuple(results)
```

```python
import functools

import jax
import jax.numpy as jnp
import numpy as np
from jax import lax
from jax.experimental import pallas as pl
from jax.experimental.pallas import tpu as pltpu

F32 = jnp.float32
BF16 = jnp.bfloat16
MXU_DTYPE = BF16

AXES = ("x", "y", "c")
N_DEV = 8
D = 1024
HEAD_DIM = 64
N_HEADS = 8
N_KV = 2
GROUPS = 4
KV_W = GROUPS * HEAD_DIM
Q_W = N_HEADS * HEAD_DIM
GRID_W = 64
WIN = 128
ROPE_THETA = 10000.0
D_FF = 2816
FF_SHARD = 2 * D_FF // N_DEV
FF_SHARD_PAD = 768
FF = N_DEV // 2 * FF_SHARD_PAD
LN_EPS = 1e-5
QK_EPS = 1e-6
N_MOD = 6
ALPHA = 2.0 ** 0.25
Q_SCALE = HEAD_DIM ** -0.5
IN_COLS = 3584
OFF_KA, OFF_VA, OFF_QB, OFF_KB, OFF_VB, OFF_GA = 512, 640, 768, 1280, 1408, 1536
EXT_COLS = 6 * Q_W + 2 * D
X_QA, X_KA, X_VA, X_QB, X_KB, X_VB, X_GL = 0, 512, 1024, 1536, 2048, 2560, 3072
ADAM_LR, ADAM_B1, ADAM_B2, ADAM_EPS, ADAM_WD, ADAM_STEP = 0.001, 0.9, 0.999, 1e-08, 0.01, 10
LANES = 128
TM = 256
VMEM_LIMIT = 56 * 1024 * 1024
ELEMENTWISE_BLOCK_BYTES = 1 << 20
ELEMENTWISE_ROWS = (1824, 1408, 1024, 512, 256, 128, 64, 32, 16, 8)

ANY = pl.BlockSpec(memory_space=pl.ANY)
SDS = jax.ShapeDtypeStruct


def _pick(n, candidates):
    for t in candidates:
        if n % t == 0:
            return t
    raise ValueError(f"no tile for {n}")


def _full(a):
    nd = a.ndim
    return pl.BlockSpec(a.shape, lambda *_: (0,) * nd)


def _rows(tm, w, fn=lambda t: t):
    return pl.BlockSpec((tm, w), lambda i: (fn(i), 0))


def _dot(a, b):
    return jnp.dot(a.astype(MXU_DTYPE), b.astype(MXU_DTYPE), preferred_element_type=F32)


def _dot_nt(a, b):
    return lax.dot_general(a.astype(MXU_DTYPE), b.astype(MXU_DTYPE), (((1,), (1,)), ((), ())), preferred_element_type=F32)


def _dot_tn(a, b):
    return lax.dot_general(a.astype(MXU_DTYPE), b.astype(MXU_DTYPE), (((0,), (0,)), ((), ())), preferred_element_type=F32)


def _cparams(sem):
    return pltpu.CompilerParams(dimension_semantics=sem, vmem_limit_bytes=VMEM_LIMIT)


def all_gather(name, v):
    r, w = v.shape

    def body(x_ref, out_ref, send_sems, recv_sems, local_sem):
        x, y, c = (lax.axis_index(a) for a in AXES)
        me, sibling = (x, y, c), (x, y, 1 - c)
        chips = [(1 - x, y), (x, 1 - y), (1 - x, 1 - y)]

        def rows(px, py, pc):
            return out_ref.at[4 * px + 2 * py + pc]

        def copy(k, block, to, src=None):
            return pltpu.make_async_remote_copy(
                src_ref=rows(*block) if src is None else src, dst_ref=rows(*block),
                send_sem=send_sems.at[k], recv_sem=recv_sems.at[k],
                device_id=to, device_id_type=pl.DeviceIdType.MESH)

        mine = pltpu.make_async_copy(x_ref, rows(*me), local_sem)
        mine.start()
        first = [copy(0, me, sibling, src=x_ref)]
        first += [copy(1 + j, me, (*chip, c), src=x_ref) for j, chip in enumerate(chips)]
        for cp in first:
            cp.start()
        passed = [copy(4 + j, (*chip, c), sibling) for j, chip in enumerate(chips)]
        for j, chip in enumerate(chips):
            copy(1 + j, (*chip, c), me).wait_recv()
            passed[j].start()
        copy(0, sibling, me).wait_recv()
        for j, chip in enumerate(chips):
            copy(4 + j, (*chip, 1 - c), me).wait_recv()
        for cp in first + passed:
            cp.wait_send()
        mine.wait()

    return pl.pallas_call(
        body, name=name, out_shape=SDS((N_DEV, r, w), v.dtype), in_specs=[ANY], out_specs=ANY,
        scratch_shapes=[pltpu.SemaphoreType.DMA((7,)), pltpu.SemaphoreType.DMA((7,)), pltpu.SemaphoreType.DMA],
    )(v)


def exchange(name, outbox, to_chips):
    k = outbox.shape[0]
    assert k == (3 if to_chips else 1)

    def body(out_ref, in_ref, send_sems, recv_sems):
        x, y, c = (lax.axis_index(a) for a in AXES)
        peers = [(x, 1 - y, c), (1 - x, y, c), (1 - x, 1 - y, c)] if to_chips else [(x, y, 1 - c)]
        copies = [
            pltpu.make_async_remote_copy(
                src_ref=out_ref.at[m], dst_ref=in_ref.at[m], send_sem=send_sems.at[m], recv_sem=recv_sems.at[m],
                device_id=peer, device_id_type=pl.DeviceIdType.MESH)
            for m, peer in enumerate(peers)
        ]
        for cp in copies:
            cp.start()
        for cp in copies:
            cp.wait_recv()
        for cp in copies:
            cp.wait_send()

    return pl.pallas_call(
        body, name=name, out_shape=SDS(outbox.shape, outbox.dtype), in_specs=[ANY], out_specs=ANY,
        scratch_shapes=[pltpu.SemaphoreType.DMA((k,)), pltpu.SemaphoreType.DMA((k,))],
    )(outbox)


def rowwise(name, body, *, ntiles, tile_off=0, tiled, full, outs, accs=()):
    nt, nf, no = len(tiled), len(full), len(outs)

    def kern(*refs):
        i = pl.program_id(0)
        out_vals, incs = body(i + tile_off, refs[:nt], refs[nt:nt + nf])
        for r, v in zip(refs[nt + nf:nt + nf + no], out_vals, strict=True):
            r[...] = v.astype(r.dtype)
        acc_refs = refs[nt + nf + no:]

        @pl.when(i == 0)
        def _():
            for r in acc_refs:
                r[...] = jnp.zeros_like(r)

        for r, v in zip(acc_refs, incs, strict=True):
            r[...] += v

    res = pl.pallas_call(
        kern, name=name, grid=(ntiles,),
        in_specs=[s for _, s in tiled] + [_full(a) for a in full],
        out_specs=[s for _, _, s in outs] + [pl.BlockSpec(s, lambda i, n=len(s): (0,) * n) for s in accs],
        out_shape=[SDS(s, d) for s, d, _ in outs] + [SDS(s, F32) for s in accs],
        compiler_params=_cparams(("arbitrary",) if accs else ("parallel",)),
    )(*[a for a, _ in tiled], *full)
    return res[:no], res[no:]


def mm_tn(name, a, b, rows):
    ka, nb = a.shape[1], b.shape[1]
    tr = _pick(rows, (1280, 1024, 768, 512, 256))
    tn = _pick(nb, (512, 256, 128))

    def kern(a_ref, b_ref, o_ref):
        @pl.when(pl.program_id(1) == 0)
        def _():
            o_ref[...] = jnp.zeros_like(o_ref)

        o_ref[...] += _dot_tn(a_ref[...], b_ref[...])

    return pl.pallas_call(
        kern, name=name, grid=(nb // tn, rows // tr),
        in_specs=[pl.BlockSpec((tr, ka), lambda n, r: (r, 0)), pl.BlockSpec((tr, tn), lambda n, r: (r, n))],
        out_specs=pl.BlockSpec((ka, tn), lambda n, r: (0, n)), out_shape=SDS((ka, nb), F32),
        compiler_params=_cparams(("parallel", "arbitrary")),
    )(a, b)


def _swap16(t):
    w = t.shape[1]
    lane = lax.broadcasted_iota(jnp.int32, t.shape, 1)
    return jnp.where((lane & 16) == 0, pltpu.roll(t, w - 16, 1), pltpu.roll(t, 16, 1))


def _rope(t, cos, sin):
    return t * cos + _swap16(t) * sin


def _rope_t(d, cos, sin):
    return d * cos - _swap16(d) * sin


def _seg_sum64(a, bd_ref):
    bd = bd_ref[...]
    hi = a.astype(BF16)
    lo = (a - hi.astype(F32)).astype(BF16)
    return jnp.dot(hi, bd, preferred_element_type=F32) + jnp.dot(lo, bd, preferred_element_type=F32)


def _lane_block(shape):
    return jnp.right_shift(lax.broadcasted_iota(jnp.int32, shape, 1), 6)


def _stack_groups(t, dtype):
    blk = _lane_block(t.shape)
    return jnp.concatenate([jnp.where(blk == g, t, jnp.zeros_like(t)).astype(dtype) for g in range(GROUPS)], axis=0)


def _fold_groups(ts, tq):
    blk = _lane_block((tq, KV_W))
    out = jnp.zeros((tq, KV_W), ts.dtype)
    for g in range(GROUPS):
        out = jnp.where(blk == g, ts[g * tq:(g + 1) * tq], out)
    return out


def _row_to_col(row):
    hi = row.astype(BF16)
    r1 = row - hi.astype(F32)
    mid = r1.astype(BF16)
    lo = (r1 - mid.astype(F32)).astype(BF16)
    ones = jnp.ones((8, LANES), BF16)
    pad = jnp.zeros((7, row.shape[1]), BF16)
    acc = jnp.zeros((row.shape[1], LANES), F32)
    for term in (hi, mid, lo):
        acc = acc + lax.dot_general(jnp.concatenate([term, pad], axis=0), ones, (((0,), (0,)), ((), ())),
                                    preferred_element_type=F32)
    return acc[:, 0:1]


def _stack_tiles(t, dtype):
    return jnp.concatenate([_stack_groups(t[a:a + TM], dtype) for a in range(0, t.shape[0], TM)], axis=0)


def _fold_tiles(ts, tq):
    return jnp.concatenate([_fold_groups(ts[GROUPS * a:GROUPS * (a + TM)], TM) for a in range(0, tq, TM)], axis=0)


def _compact_tiles_t(tt, dtype):
    return jnp.concatenate([tt[g * HEAD_DIM:(g + 1) * HEAD_DIM, a:a + TM] for a in range(0, tt.shape[1], TM)
                            for g in range(GROUPS)], axis=1).astype(dtype)


def _layer_norm_bwd(dxh, xhat, rstd):
    m1 = jnp.mean(dxh, axis=1, keepdims=True)
    m2 = jnp.mean(dxh * xhat, axis=1, keepdims=True)
    return rstd * (dxh - m1 - xhat * m2)


def _colsum(a):
    return jnp.sum(a, axis=0, keepdims=True)


def _shifted_rows(t, prev_row, next_row):
    n = t.shape[0]
    row = lax.broadcasted_iota(jnp.int32, t.shape, 0)
    up = jnp.where(row == 0, prev_row, pltpu.roll(t, 1, 0))
    dn = jnp.where(row == n - 1, next_row, pltpu.roll(t, n - 1, 0))
    return up, dn


def mod_fwd(cs, w_sh, b_sh):
    def kern(c_ref, w_ref, b_ref, o_ref):
        o_ref[...] = _dot(jax.nn.silu(c_ref[...]), w_ref[...]) + b_ref[...]

    return pl.pallas_call(kern, name="mod_fwd", out_shape=SDS((16, w_sh.shape[1]), F32),
                          compiler_params=pltpu.CompilerParams(vmem_limit_bytes=VMEM_LIMIT))(cs, w_sh, b_sh)


def mod_bwd(cs, w_sh, dm_sh, dm_all):
    hp = lax.Precision.HIGHEST

    def kern(c_ref, w_ref, dm_ref, da_ref, dw_ref, dc_ref, db_ref):
        c = c_ref[...]
        sg = jax.nn.sigmoid(c)
        sc = c * sg
        dm = dm_ref[...]
        dmc = dm_ref[8:9, :]
        for i in range(9, 16):
            dmc = dmc + dm_ref[i:i + 1, :]
        row = lax.broadcasted_iota(jnp.int32, dm.shape, 0)
        a = jnp.where(row < 8, dm, jnp.where(row == 8, dmc, 0.0))
        dw_ref[...] = lax.dot_general(sc, a, (((0,), (0,)), ((), ())), precision=hp, preferred_element_type=F32)
        dsc = lax.dot_general(a, w_ref[...], (((1,), (1,)), ((), ())), precision=hp, preferred_element_type=F32)
        dc_ref[...] = dsc * (sg * (1.0 + c * (1.0 - sg)))
        db = da_ref[0:1, :]
        for i in range(1, 16):
            db = db + da_ref[i:i + 1, :]
        db_ref[...] = db

    return pl.pallas_call(
        kern, name="mod_bwd",
        out_shape=[SDS(w_sh.shape, F32), SDS((16, D), F32), SDS((1, dm_all.shape[1]), F32)],
        compiler_params=pltpu.CompilerParams(vmem_limit_bytes=VMEM_LIMIT))(cs, w_sh, dm_sh, dm_all)


M_SHIFT1, M_SCALE1, M_SHIFTC, M_SCALEC, M_GATE1, M_SHIFT2, M_SCALE2, M_GATE2 = range(8)


def _mrow(ref, k):
    return ref[k:k + 1, :]


def inproj_fwd(xa, cos, sin, modrows, w_ext, b_ext, qg, kg, bd, n_lat_tiles):
    n = xa.shape[0]

    def body(t, vals, fr):
        x, cs, sn = (v[...] for v in vals)
        mod, w, b, qg_r, kg_r, bd_r = fr
        is_ctx = t >= n_lat_tiles
        shift = jnp.where(is_ctx, _mrow(mod, M_SHIFTC), _mrow(mod, M_SHIFT1))
        scale = jnp.where(is_ctx, _mrow(mod, M_SCALEC), _mrow(mod, M_SCALE1))
        hb = (x * (1.0 + scale) + shift).astype(MXU_DTYPE)
        proj = jnp.dot(hb, w[...], preferred_element_type=F32) + b[...]
        cos4 = jnp.concatenate([cs] * 4, axis=1)
        sin4 = jnp.concatenate([sn] * 4, axis=1)
        qa = _rope(proj[:, X_QA:X_QA + Q_W], cos4, sin4) * Q_SCALE
        ka = _rope(proj[:, X_KA:X_KA + Q_W], cos4, sin4)
        va = proj[:, X_VA:X_VA + Q_W]
        tq = proj[:, X_QB:X_QB + Q_W]
        rq = lax.rsqrt(_seg_sum64(tq * tq, bd_r) * (1.0 / HEAD_DIM) + QK_EPS)
        qb = _rope(tq * rq * qg_r[...], cos4, sin4) * Q_SCALE
        tk = proj[:, X_KB:X_KB + Q_W]
        rk = lax.rsqrt(_seg_sum64(tk * tk, bd_r) * (1.0 / HEAD_DIM) + QK_EPS)
        kb = _rope(tk * rk * kg_r[...], cos4, sin4)
        vb = proj[:, X_VB:X_VB + Q_W]
        gl = proj[:, X_GL:]
        return [hb, qa, ka, va, qb, kb, vb, tq, rq, tk, rk, gl], []

    mx = MXU_DTYPE
    outs = [((n, D), mx, _rows(TM, D))] + [((n, Q_W), mx, _rows(TM, Q_W))] * 6 + \
           [((n, Q_W), F32, _rows(TM, Q_W))] * 4 + [((n, 2 * D), F32, _rows(TM, 2 * D))]
    res, _ = rowwise("inproj_fwd", body, ntiles=n // TM,
                     tiled=[(xa, _rows(TM, D)), (cos, _rows(TM, LANES)), (sin, _rows(TM, LANES))],
                     full=[modrows, w_ext, b_ext, qg, kg, bd], outs=outs)
    return res


def merge_fwd(oa, ob, gl, x, modrows, wba, wbb, w_out, s_rows):
    def body(t, vals, fr):
        oa_, ob_, gl_, x_ = (v[...] for v in vals)
        mod, wa, wb, wo = fr
        ya = _dot(oa_, wa[...])
        yb = _dot(ob_, wb[...])
        ga = jax.nn.sigmoid(gl_[:, :D])
        gb = jax.nn.sigmoid(gl_[:, D:])
        mrg = ga * ya + gb * yb
        y = _dot(mrg, wo[...])
        r1 = ALPHA * x_ + _mrow(mod, M_GATE1) * y
        mu = jnp.mean(r1, axis=1, keepdims=True)
        xc = r1 - mu
        var = jnp.mean(xc * xc, axis=1, keepdims=True)
        rstd = lax.rsqrt(var + LN_EPS)
        xhat = xc * rstd
        return [ya, yb, mrg, y, xhat, rstd], []

    outs = [((s_rows, D), F32, _rows(TM, D))] * 2 + [((s_rows, D), MXU_DTYPE, _rows(TM, D))] + \
           [((s_rows, D), F32, _rows(TM, D))] * 2 + [((s_rows, 1), F32, _rows(TM, 1))]
    res, _ = rowwise("merge_fwd", body, ntiles=s_rows // TM,
                     tiled=[(oa, _rows(TM, Q_W)), (ob, _rows(TM, Q_W)), (gl, _rows(TM, 2 * D)), (x, _rows(TM, D))],
                     full=[modrows, wba, wbb, w_out], outs=outs)
    return res


def ffn_up_fwd(xhat1, modrows, ln_g, ln_b, w_up, s_rows):
    def body(t, vals, fr):
        xh = vals[0][...]
        mod, g_r, b_r, w = fr
        x1 = xh * g_r[...] + b_r[...]
        h2 = (x1 * (1.0 + _mrow(mod, M_SCALE2)) + _mrow(mod, M_SHIFT2)).astype(MXU_DTYPE)
        return [h2, jnp.dot(h2, w[...], preferred_element_type=F32)], []

    res, _ = rowwise("ffn_up_fwd", body, ntiles=s_rows // TM, tiled=[(xhat1, _rows(TM, D))],
                     full=[modrows, ln_g, ln_b, w_up],
                     outs=[((s_rows, D), MXU_DTYPE, _rows(TM, D)), ((s_rows, 2 * FF), F32, _rows(TM, 2 * FF))])
    return res


TC = 128


def _halo_specs(tm, w, s_rows):
    per = tm // 8
    last = s_rows // 8 - 1
    return (pl.BlockSpec((8, w), lambda i: (jnp.maximum(i * per - 1, 0), 0)),
            pl.BlockSpec((8, w), lambda i: (jnp.minimum((i + 1) * per, last), 0)))


def _halo_rows(t, ntiles, prev_ref, next_ref):
    prev_row = jnp.where(t == 0, 0.0, prev_ref[7:8, :].astype(F32))
    next_row = jnp.where(t == ntiles - 1, 0.0, next_ref[0:1, :].astype(F32))
    return prev_row, next_row


def conv_swiglu_fwd(u0, conv_w8, conv_b, s_rows):
    w2 = 2 * FF
    nt = s_rows // TC

    def body(t, vals, fr):
        u_ref, pv, nx = vals
        cw, cb = fr
        u = u_ref[...]
        up, dn = _shifted_rows(u, *_halo_rows(t, nt, pv, nx))
        uc = cw[0:1, :] * up + cw[1:2, :] * u + cw[2:3, :] * dn + cb[...]
        gate, val = uc[:, :FF], uc[:, FF:]
        return [gate * jax.nn.sigmoid(gate) * val], []

    hp, hn = _halo_specs(TC, w2, s_rows)
    res, _ = rowwise("conv_swiglu_fwd", body, ntiles=nt,
                     tiled=[(u0, _rows(TC, w2)), (u0, hp), (u0, hn)], full=[conv_w8, conv_b],
                     outs=[((s_rows, FF), MXU_DTYPE, _rows(TC, FF))])
    return res[0]


def ffn_down_loss(a, xhat1, target, modrows, ln1_g, ln1_b, ln2_g, ln2_b, w_down, s_rows):
    def body(t, vals, fr):
        a_, xh1, tgt = (v[...] for v in vals)
        mod, g1, b1, g2, b2, wd = fr
        y2 = jnp.dot(a_, wd[...], preferred_element_type=F32)
        x1 = xh1 * g1[...] + b1[...]
        gate2 = _mrow(mod, M_GATE2)
        r2 = ALPHA * x1 + gate2 * y2
        mu = jnp.mean(r2, axis=1, keepdims=True)
        xc = r2 - mu
        var = jnp.mean(xc * xc, axis=1, keepdims=True)
        rstd = lax.rsqrt(var + LN_EPS)
        xhat = xc * rstd
        out = xhat * g2[...] + b2[...]
        diff = out - tgt
        loss = 0.5 * jnp.sum(jnp.mean(diff * diff, axis=1, keepdims=True), axis=0, keepdims=True)
        dout = diff * (1.0 / D)
        dr2 = _layer_norm_bwd(dout * g2[...], xhat, rstd)
        incs = [loss, _colsum(dout * xhat), _colsum(dout), _colsum(dr2 * y2)]
        return [dr2, dr2 * gate2], incs

    res, accs = rowwise("ffn_down_loss", body, ntiles=s_rows // TM,
                        tiled=[(a, _rows(TM, FF)), (xhat1, _rows(TM, D)), (target, _rows(TM, D))],
                        full=[modrows, ln1_g, ln1_b, ln2_g, ln2_b, w_down],
                        outs=[((s_rows, D), F32, _rows(TM, D)), ((s_rows, D), MXU_DTYPE, _rows(TM, D))],
                        accs=[(1, 1), (1, D), (1, D), (1, D)])
    return res, accs


def ffn_down_bwd(dy2, w_down_t, s_rows):
    def body(t, vals, fr):
        return [jnp.dot(vals[0][...], fr[0][...], preferred_element_type=F32)], []

    res, _ = rowwise("ffn_down_bwd", body, ntiles=s_rows // TM, tiled=[(dy2, _rows(TM, D))], full=[w_down_t],
                     outs=[((s_rows, FF), F32, _rows(TM, FF))])
    return res[0]


def swiglu_conv_bwd(u0, da, conv_w8, conv_b, s_rows):
    w2 = 2 * FF
    nt = s_rows // TC
    n = TC + 16

    def body(t, vals, fr):
        u_ref, upv, unx, da_ref, apv, anx = vals
        cw, cb = fr
        first, last = t == 0, t == nt - 1
        ue = jnp.concatenate([jnp.where(first, 0.0, upv[...]), u_ref[...], jnp.where(last, 0.0, unx[...])], axis=0)
        ae = jnp.concatenate([jnp.where(first, 0.0, apv[...]), da_ref[...], jnp.where(last, 0.0, anx[...])], axis=0)
        up = pltpu.roll(ue, 1, 0)
        dn = pltpu.roll(ue, n - 1, 0)
        uc = cw[0:1, :] * up + cw[1:2, :] * ue + cw[2:3, :] * dn + cb[...]
        gate, val = uc[:, :FF], uc[:, FF:]
        sg = jax.nn.sigmoid(gate)
        du = jnp.concatenate([ae * val * (sg * (1.0 + gate * (1.0 - sg))), ae * (gate * sg)], axis=1)
        du0 = cw[0:1, :] * pltpu.roll(du, n - 1, 0) + cw[1:2, :] * du + cw[2:3, :] * pltpu.roll(du, 1, 0)
        rows = slice(8, 8 + TC)
        dut = du[rows]
        return [du0[rows]], [_colsum(dut), _colsum(up[rows] * dut), _colsum(ue[rows] * dut), _colsum(dn[rows] * dut)]

    hp, hn = _halo_specs(TC, w2, s_rows)
    ap, an = _halo_specs(TC, FF, s_rows)
    res, accs = rowwise("swiglu_conv_bwd", body, ntiles=nt,
                        tiled=[(u0, _rows(TC, w2)), (u0, hp), (u0, hn), (da, _rows(TC, FF)), (da, ap), (da, an)],
                        full=[conv_w8, conv_b], outs=[((s_rows, w2), MXU_DTYPE, _rows(TC, w2))], accs=[(1, w2)] * 4)
    return res[0], accs


def ffn_up_ln1_bwd(du0, dr2, xhat1, y, rstd1, modrows, ln_g, ln_b, w_up_t, s_rows):
    def body(t, vals, fr):
        du0_, dr2_, xh, y_, rstd = (v[...] for v in vals)
        mod, g_r, b_r, wt = fr
        dh2 = jnp.dot(du0_, wt[...], preferred_element_type=F32)
        x1 = xh * g_r[...] + b_r[...]
        dx1 = ALPHA * dr2_ + dh2 * (1.0 + _mrow(mod, M_SCALE2))
        dr1 = _layer_norm_bwd(dx1 * g_r[...], xh, rstd)
        incs = [_colsum(dh2 * x1), _colsum(dh2), _colsum(dx1 * xh), _colsum(dx1), _colsum(dr1 * y_)]
        return [dr1 * _mrow(mod, M_GATE1), ALPHA * dr1], incs

    res, accs = rowwise("ffn_up_ln1_bwd", body, ntiles=s_rows // TM,
                        tiled=[(du0, _rows(TM, 2 * FF)), (dr2, _rows(TM, D)), (xhat1, _rows(TM, D)), (y, _rows(TM, D)),
                               (rstd1, _rows(TM, 1))],
                        full=[modrows, ln_g, ln_b, w_up_t],
                        outs=[((s_rows, D), MXU_DTYPE, _rows(TM, D)), ((s_rows, D), F32, _rows(TM, D))],
                        accs=[(1, D)] * 5)
    return res, accs


def merge_bwd(dy, ya, yb, gl, w_out_t, wba_t, wbb_t, s_rows):
    def body(t, vals, fr):
        dy_, ya_, yb_, gl_ = (v[...] for v in vals)
        wot, wat, wbt = fr
        dmrg = jnp.dot(dy_, wot[...], preferred_element_type=F32)
        ga = jax.nn.sigmoid(gl_[:, :D])
        gb = jax.nn.sigmoid(gl_[:, D:])
        dya = dmrg * ga
        dyb = dmrg * gb
        dgl = jnp.concatenate([dmrg * ya_ * ga * (1.0 - ga), dmrg * yb_ * gb * (1.0 - gb)], axis=1)
        return [dya, dyb, dgl, _dot(dya, wat[...]), _dot(dyb, wbt[...])], []

    mx = MXU_DTYPE
    res, _ = rowwise("merge_bwd", body, ntiles=s_rows // TM,
                     tiled=[(dy, _rows(TM, D)), (ya, _rows(TM, D)), (yb, _rows(TM, D)), (gl, _rows(TM, 2 * D))],
                     full=[w_out_t, wba_t, wbb_t],
                     outs=[((s_rows, D), mx, _rows(TM, D))] * 2 + [((s_rows, 2 * D), F32, _rows(TM, 2 * D))] +
                          [((s_rows, Q_W), F32, _rows(TM, Q_W))] * 2)
    return res


def qk_bwd(dqa, dka, dva, dqb, dkb, dvb, dgl, tq, rq, tk, rk, cos, sin, qg, kg, bd, n_lat_tiles, n):
    def body(t, vals, fr):
        dqa_, dka_, dva_, dqb_, dkb_, dvb_, dgl_, tq_, rq_, tk_, rk_, cs, sn = (v[...] for v in vals)
        qg_r, kg_r, bd_r = fr
        is_ctx = t >= n_lat_tiles
        cos4 = jnp.concatenate([cs] * 4, axis=1)
        sin4 = jnp.concatenate([sn] * 4, axis=1)
        zero = jnp.zeros_like(dqa_)
        dpqa = jnp.where(is_ctx, zero, _rope_t(dqa_, cos4, sin4) * Q_SCALE)
        dpka = _rope_t(dka_, cos4, sin4)
        dpva = dva_
        dnq = jnp.where(is_ctx, zero, _rope_t(dqb_, cos4, sin4) * Q_SCALE)
        gq = qg_r[...] * dnq
        dtq = rq_ * gq - tq_ * (rq_ * rq_ * rq_) * (_seg_sum64(gq * tq_, bd_r) * (1.0 / HEAD_DIM))
        dnk = _rope_t(dkb_, cos4, sin4)
        gk = kg_r[...] * dnk
        dtk = rk_ * gk - tk_ * (rk_ * rk_ * rk_) * (_seg_sum64(gk * tk_, bd_r) * (1.0 / HEAD_DIM))
        dgl32 = jnp.where(is_ctx, jnp.zeros_like(dgl_), dgl_)
        dproj = jnp.concatenate([dpqa, dpka, dpva, dtq, dtk, dvb_, dgl32], axis=1)
        return [dproj], [_colsum(dproj), _colsum(dnq * tq_ * rq_), _colsum(dnk * tk_ * rk_)]

    lat = lambda t: jnp.minimum(t, n_lat_tiles - 1)
    qs = _rows(TM, Q_W)
    res, accs = rowwise(
        "qk_bwd", body, ntiles=n // TM,
        tiled=[(dqa, _rows(TM, Q_W, lat)), (dka, qs), (dva, qs), (dqb, _rows(TM, Q_W, lat)),
               (dkb, qs), (dvb, qs), (dgl, _rows(TM, 2 * D, lat)), (tq, qs), (rq, qs), (tk, qs), (rk, qs),
               (cos, _rows(TM, LANES)), (sin, _rows(TM, LANES))],
        full=[qg, kg, bd], outs=[((n, EXT_COLS), MXU_DTYPE, _rows(TM, EXT_COLS))],
        accs=[(1, EXT_COLS), (1, Q_W), (1, Q_W)])
    return res[0], accs


def inproj_bwd(name, dproj, xa, dxp, modrows, w_ext_t, *, ntiles, tile_off, is_ctx, out_rows):
    kc = M_SCALEC if is_ctx else M_SCALE1

    def body(t, vals, fr):
        dp, x_ = vals[0][...], vals[1][...]
        mod, wt = fr
        dh = jnp.dot(dp, wt[...], preferred_element_type=F32)
        incs = [_colsum(dh * x_), _colsum(dh)]
        if is_ctx:
            return [], incs
        return [vals[2][...] + dh * (1.0 + _mrow(mod, kc))], incs

    tiled = [(dproj, _rows(TM, EXT_COLS, lambda i: i + tile_off)), (xa, _rows(TM, D, lambda i: i + tile_off))]
    outs = []
    if not is_ctx:
        tiled.append((dxp, _rows(TM, D)))
        outs = [((out_rows, D), F32, _rows(TM, D))]
    return rowwise(name, body, ntiles=ntiles, tiled=tiled, full=[modrows, w_ext_t], outs=outs, accs=[(1, D)] * 2)


def _attn_semantics():
    return _cparams(("arbitrary", "arbitrary", "arbitrary"))


GLOB_TK = (1280, 1024, 768, 512, 256)
GLOB_TQ = (512, 256)
GLOB_BWD_TQ = GLOB_TQ
KEY_CHUNK = 256


def glob_fwd(q, kt, v_t, s_rows):
    n = kt.shape[0]
    tq = _pick(s_rows, GLOB_TQ)
    tk = _pick(n, GLOB_TK)
    nq, nk = s_rows // tq, n // tk
    r = GROUPS * tq
    nch = tk // KEY_CHUNK

    def produce(qs, k_ref, s_buf, c, mx):
        rows = slice(c * KEY_CHUNK, (c + 1) * KEY_CHUNK)
        sn = _dot_nt(k_ref[rows, :], qs[...])
        s_buf[rows, :] = sn
        return jnp.maximum(mx, jnp.max(sn, axis=0, keepdims=True))

    def kern(q_ref, k0_ref, kn_ref, vt_ref, o_ref, lse_ref, qs, s_buf, mx_buf, m_s, l_s, acc):
        j = pl.program_id(2)

        @pl.when(j == 0)
        def _():
            qs[...] = _stack_tiles(q_ref[...], qs.dtype)
            mx = jnp.full((1, r), -jnp.inf, F32)
            for c in range(nch):
                mx = produce(qs, k0_ref, s_buf, c, mx)
            mx_buf[...] = mx
            m_s[...] = jnp.full_like(m_s, -jnp.inf)
            l_s[...] = jnp.zeros_like(l_s)
            acc[...] = jnp.zeros_like(acc)

        m_prev = m_s[...]
        m_new = jnp.maximum(m_prev, mx_buf[...])
        alpha = jnp.exp(m_prev - m_new)
        a = alpha * acc[...]
        ls = alpha * l_s[...]
        mx = jnp.full((1, r), -jnp.inf, F32)
        for c in range(nch):
            rows = slice(c * KEY_CHUNK, (c + 1) * KEY_CHUNK)
            p = jnp.exp(s_buf[rows, :] - m_new)
            ls = ls + jnp.sum(p, axis=0, keepdims=True)
            a = a + jnp.dot(vt_ref[0, :, rows], p.astype(MXU_DTYPE), preferred_element_type=F32)
            mx = produce(qs, kn_ref, s_buf, c, mx)
        mx_buf[...] = mx
        l_s[...] = ls
        acc[...] = a
        m_s[...] = m_new

        @pl.when(j == nk - 1)
        def _():
            o_t = acc[...] / l_s[...]
            o_ref[...] = jnp.concatenate([_untranspose_groups(o_t[:, GROUPS * a:GROUPS * (a + TM)], TM)
                                          for a in range(0, tq, TM)], axis=0)
            lse_ref[0, 0] = _row_to_col(m_s[...] + jnp.log(l_s[...]))

    kspec = lambda f: pl.BlockSpec((tk, KV_W), lambda h, i, j: (f(j), h))
    return pl.pallas_call(
        kern, name="glob_fwd", grid=(N_KV, nq, nk),
        in_specs=[pl.BlockSpec((tq, KV_W), lambda h, i, j: (i, h)), kspec(lambda j: 0),
                  kspec(lambda j: jnp.minimum(j + 1, nk - 1)), pl.BlockSpec((1, HEAD_DIM, tk), lambda h, i, j: (h, 0, j))],
        out_specs=[pl.BlockSpec((tq, KV_W), lambda h, i, j: (i, h)),
                   pl.BlockSpec((1, 1, r, 1), lambda h, i, j: (h, i, 0, 0))],
        out_shape=[SDS((s_rows, Q_W), F32), SDS((N_KV, nq, r, 1), F32)],
        scratch_shapes=[pltpu.VMEM((r, KV_W), MXU_DTYPE), pltpu.VMEM((tk, r), F32), pltpu.VMEM((1, r), F32),
                        pltpu.VMEM((1, r), F32), pltpu.VMEM((1, r), F32), pltpu.VMEM((HEAD_DIM, r), F32)],
        compiler_params=_attn_semantics(),
    )(q, kt, kt, v_t)


def attn_delta(o, do, s_rows):
    tq = _pick(s_rows, GLOB_BWD_TQ)
    nq = s_rows // tq
    r = GROUPS * tq

    def kern(o_ref, do_ref, d_ref):
        d_ref[0, 0] = jnp.sum(_stack_tiles(do_ref[...], F32) * _stack_tiles(o_ref[...], F32), axis=1, keepdims=True)

    qspec = pl.BlockSpec((tq, KV_W), lambda h, i: (i, h))
    return pl.pallas_call(
        kern, name="attn_delta", grid=(N_KV, nq), in_specs=[qspec, qspec],
        out_specs=pl.BlockSpec((1, 1, r, 1), lambda h, i: (h, i, 0, 0)), out_shape=SDS((N_KV, nq, r, 1), F32),
        compiler_params=_cparams(("parallel", "parallel")),
    )(o, do)


def _compact_t(tt, dtype):
    return jnp.concatenate([tt[g * HEAD_DIM:(g + 1) * HEAD_DIM, :] for g in range(GROUPS)], axis=1).astype(dtype)


def glob_bwd(q, q_t, kt, vt, do, do_t, lse, delta, h, s_rows):
    n = kt.shape[0]
    tq = _pick(s_rows, GLOB_BWD_TQ)
    tk = _pick(n, GLOB_TK)
    nq, nk = s_rows // tq, n // tk
    r = GROUPS * tq
    nch = tk // KEY_CHUNK

    def kern(q_ref, qt_ref, k_ref, v_ref, do_ref, dot_ref, lse_ref, dl_ref, dq_ref, dkt_ref, dvt_ref, p_buf, ds_buf):
        j = pl.program_id(0)
        i = pl.program_id(1)

        @pl.when(i == 0)
        def _():
            dkt_ref[...] = jnp.zeros_like(dkt_ref)
            dvt_ref[...] = jnp.zeros_like(dvt_ref)

        qs = _stack_tiles(q_ref[...], MXU_DTYPE)
        dos = _stack_tiles(do_ref[...], MXU_DTYPE)
        lse_b = jnp.broadcast_to(lse_ref[0], (r, LANES))
        dl_b = jnp.broadcast_to(dl_ref[0], (r, LANES))
        for c in range(nch):
            lo = c * KEY_CHUNK
            sc = _dot_nt(qs, k_ref[lo:lo + KEY_CHUNK, :])
            dpc = _dot_nt(dos, v_ref[lo:lo + KEY_CHUNK, :])
            for t in range(KEY_CHUNK // LANES):
                sl = slice(t * LANES, (t + 1) * LANES)
                pt = jnp.exp(sc[:, sl] - lse_b)
                p_buf[:, lo + t * LANES:lo + (t + 1) * LANES] = pt.astype(p_buf.dtype)
                ds_buf[:, lo + t * LANES:lo + (t + 1) * LANES] = (pt * (dpc[:, sl] - dl_b)).astype(ds_buf.dtype)
        dq_t = _fold_tiles(jnp.dot(ds_buf[...], k_ref[...], preferred_element_type=F32), tq)
        rows = pl.ds(pl.multiple_of(i * tq, tq), tq)

        @pl.when(j == 0)
        def _():
            dq_ref[rows, :] = dq_t

        @pl.when(j > 0)
        def _():
            dq_ref[rows, :] += dq_t

        dvt_ref[...] += jnp.dot(_compact_tiles_t(dot_ref[...], MXU_DTYPE), p_buf[...], preferred_element_type=F32)
        dkt_ref[...] += jnp.dot(_compact_tiles_t(qt_ref[...], MXU_DTYPE), ds_buf[...], preferred_element_type=F32)

    col = pl.BlockSpec((1, r, 1), lambda j, i: (i, 0, 0))
    qspec = pl.BlockSpec((tq, KV_W), lambda j, i: (i, h))
    tspec = pl.BlockSpec((KV_W, tq), lambda j, i: (h, i))
    kspec = pl.BlockSpec((tk, KV_W), lambda j, i: (j, h))
    ospec = pl.BlockSpec((HEAD_DIM, tk), lambda j, i: (0, j))
    return pl.pallas_call(
        kern, name=f"glob_bwd_h{h}", grid=(nk, nq),
        in_specs=[qspec, tspec, kspec, kspec, qspec, tspec, col, col],
        out_specs=[pl.BlockSpec(memory_space=pltpu.VMEM), ospec, ospec],
        out_shape=[SDS((s_rows, KV_W), F32), SDS((HEAD_DIM, n), F32), SDS((HEAD_DIM, n), F32)],
        scratch_shapes=[pltpu.VMEM((r, tk), MXU_DTYPE), pltpu.VMEM((r, tk), MXU_DTYPE)],
        compiler_params=_cparams(("arbitrary", "arbitrary")),
    )(q, q_t, kt, vt, do, do_t, lse, delta)


TW = 2 * WIN
WR = GROUPS * TW
WLAT = 4 * WIN


def _win_cat(dst, parts):
    off = 0
    for p in parts:
        dst[off:off + p.shape[0], :] = p[...]
        off += p.shape[0]


def _win_specs(s_rows, c_rows):
    nb = s_rows // WIN
    prev = lambda i: jnp.maximum(2 * i - 1, 0)
    nxt = lambda i: jnp.minimum(2 * i + 2, nb - 1)
    rows = [pl.BlockSpec((WIN, KV_W), lambda h, i: (prev(i), h)), pl.BlockSpec((TW, KV_W), lambda h, i: (i, h)),
            pl.BlockSpec((WIN, KV_W), lambda h, i: (nxt(i), h)), pl.BlockSpec((c_rows, KV_W), lambda h, i: (s_rows // c_rows, h))]
    cols = [pl.BlockSpec((1, HEAD_DIM, WIN), lambda h, i: (h, 0, prev(i))), pl.BlockSpec((1, HEAD_DIM, TW), lambda h, i: (h, 0, i)),
            pl.BlockSpec((1, HEAD_DIM, WIN), lambda h, i: (h, 0, nxt(i))),
            pl.BlockSpec((1, HEAD_DIM, c_rows), lambda h, i: (h, 0, s_rows // c_rows))]
    return rows, cols


def _win_mask(i, s_rows, shape, keys_on_rows):
    a = lax.broadcasted_iota(jnp.int32, shape, 0)
    b = lax.broadcasted_iota(jnp.int32, shape, 1)
    kk, qq = (a, b) if keys_on_rows else (b, a)
    qpos = i * TW + (qq & (TW - 1))
    kpos = (2 * i - 1) * WIN + kk
    band = (jnp.abs(qpos - kpos) <= WIN) & (kpos >= 0) & (kpos < s_rows)
    return (kk >= WLAT) | band


def _untranspose_groups(o_t, tq):
    row = lax.broadcasted_iota(jnp.int32, (HEAD_DIM, KV_W), 0)
    col = lax.broadcasted_iota(jnp.int32, (HEAD_DIM, KV_W), 1)
    hi = o_t.astype(BF16)
    r1 = o_t - hi.astype(F32)
    mid = r1.astype(BF16)
    lo = (r1 - mid.astype(F32)).astype(BF16)
    o = jnp.zeros((tq, KV_W), F32)
    for g in range(GROUPS):
        sel = jnp.where(col == row + g * HEAD_DIM, 1.0, 0.0).astype(BF16)
        for term in (hi, mid, lo):
            o = o + lax.dot_general(term[:, g * tq:(g + 1) * tq], sel, (((0,), (0,)), ((), ())), preferred_element_type=F32)
    return o


def win_fwd(q, kt, v_t, sinkrow, s_rows, c_rows):
    nt = s_rows // TW
    nkeys = WLAT + c_rows

    def kern(q_ref, kp, kc, kn, kx, vp, vc, vn, vx, sink_ref, o_ref, lse_ref, kcat):
        i = pl.program_id(1)
        _win_cat(kcat, (kp, kc, kn, kx))
        qs = _stack_groups(q_ref[...], MXU_DTYPE)
        st = _dot_nt(kcat[...], qs)
        st = jnp.where(_win_mask(i, s_rows, st.shape, True), st, -jnp.inf)
        sink = sink_ref[0]
        m = jnp.maximum(jnp.max(st, axis=0, keepdims=True), sink)
        e = jnp.exp(st - m)
        den = jnp.sum(e, axis=0, keepdims=True) + jnp.exp(sink - m)
        v_cat = jnp.concatenate([vp[0], vc[0], vn[0], vx[0]], axis=1)
        o_t = jnp.dot(v_cat, e.astype(MXU_DTYPE), preferred_element_type=F32) / den
        o_ref[...] = _untranspose_groups(o_t, TW)
        lse_ref[0, 0] = _row_to_col(m + jnp.log(den))

    rows, cols = _win_specs(s_rows, c_rows)
    qspec = pl.BlockSpec((TW, KV_W), lambda h, i: (i, h))
    rowv = pl.BlockSpec((1, 1, WR, 1), lambda h, i: (h, i, 0, 0))
    return pl.pallas_call(
        kern, name="win_fwd", grid=(N_KV, nt),
        in_specs=[qspec] + rows + cols + [pl.BlockSpec((1, 1, WR), lambda h, i: (h, 0, 0))],
        out_specs=[qspec, rowv], out_shape=[SDS((s_rows, Q_W), F32), SDS((N_KV, nt, WR, 1), F32)],
        scratch_shapes=[pltpu.VMEM((nkeys, KV_W), MXU_DTYPE)],
        compiler_params=_cparams(("parallel", "parallel")),
    )(q, kt, kt, kt, kt, v_t, v_t, v_t, v_t, sinkrow)


def win_bwd(q, q_t, kt, vt, sinkcol, o, do, do_t, lse, s_rows, c_rows):
    nt = s_rows // TW
    nkeys = WLAT + c_rows
    n = s_rows + c_rows
    ctx0 = WIN + s_rows

    def kern(q_ref, qt_ref, kp, kc, kn, kx, vp, vc, vn, vx, sink_ref, o_ref, do_ref, dot_ref, lse_ref,
             dq_ref, dkt_ref, dvt_ref, dsk_ref, kcat, vcat):
        i = pl.program_id(1)

        @pl.when(i == 0)
        def _():
            dkt_ref[...] = jnp.zeros_like(dkt_ref)
            dvt_ref[...] = jnp.zeros_like(dvt_ref)
            dsk_ref[...] = jnp.zeros_like(dsk_ref)

        _win_cat(kcat, (kp, kc, kn, kx))
        _win_cat(vcat, (vp, vc, vn, vx))
        qs = _stack_groups(q_ref[...], MXU_DTYPE)
        do32 = _stack_groups(do_ref[...], F32)
        delta = jnp.sum(do32 * _stack_groups(o_ref[...], F32), axis=1, keepdims=True)
        dos = do32.astype(MXU_DTYPE)
        lse_c = lse_ref[0, 0]
        s = _dot_nt(qs, kcat[...])
        s = jnp.where(_win_mask(i, s_rows, s.shape, False), s, -jnp.inf)
        p = jnp.exp(s - lse_c)
        ds = p * (_dot_nt(dos, vcat[...]) - delta)
        dq_ref[...] = _fold_groups(_dot(ds, kcat[...]), TW)
        dvt = jnp.dot(_compact_t(dot_ref[...], MXU_DTYPE), p.astype(MXU_DTYPE), preferred_element_type=F32)
        dkt = jnp.dot(_compact_t(qt_ref[...], MXU_DTYPE), ds.astype(MXU_DTYPE), preferred_element_type=F32)
        lat = pl.ds(pl.multiple_of(i * TW, TW), WLAT)
        dkt_ref[0, :, lat] += dkt[:, :WLAT]
        dvt_ref[0, :, lat] += dvt[:, :WLAT]
        dkt_ref[0, :, ctx0:ctx0 + c_rows] += dkt[:, WLAT:]
        dvt_ref[0, :, ctx0:ctx0 + c_rows] += dvt[:, WLAT:]
        dsk_ref[0] += -(jnp.exp(sink_ref[0][:, 0:1] - lse_c) * delta)

    rows, _ = _win_specs(s_rows, c_rows)
    qspec = pl.BlockSpec((TW, KV_W), lambda h, i: (i, h))
    tspec = pl.BlockSpec((KV_W, TW), lambda h, i: (h, i))
    col = pl.BlockSpec((1, 1, WR, 1), lambda h, i: (h, i, 0, 0))
    kvt = pl.BlockSpec((1, HEAD_DIM, WIN + n), lambda h, i: (h, 0, 0))
    return pl.pallas_call(
        kern, name="win_bwd", grid=(N_KV, nt),
        in_specs=[qspec, tspec] + rows + rows + [pl.BlockSpec((1, WR, LANES), lambda h, i: (h, 0, 0)), qspec, qspec, tspec, col],
        out_specs=[qspec, kvt, kvt, pl.BlockSpec((1, WR, 1), lambda h, i: (h, 0, 0))],
        out_shape=[SDS((s_rows, Q_W), F32), SDS((N_KV, HEAD_DIM, WIN + n), F32), SDS((N_KV, HEAD_DIM, WIN + n), F32),
                   SDS((N_KV, WR, 1), F32)],
        scratch_shapes=[pltpu.VMEM((nkeys, KV_W), MXU_DTYPE), pltpu.VMEM((nkeys, KV_W), MXU_DTYPE)],
        compiler_params=_cparams(("arbitrary", "arbitrary")),
    )(q, q_t, kt, kt, kt, kt, vt, vt, vt, vt, sinkcol, o, do, do_t, lse)


def adamw(name, w, m, v, grads):
    r, wd = w.shape
    tr = _pick(r, [t for t in ELEMENTWISE_ROWS if t * wd * 4 <= ELEMENTWISE_BLOCK_BYTES])
    stacked = not isinstance(grads, (list, tuple))
    ng = grads.shape[0] if stacked else len(grads)

    def kern(*refs):
        w_ref, m_ref, v_ref = refs[:3]
        g_refs = refs[3:-4]
        g_out, d_out, m_out, v_out = refs[-4:]
        if stacked:
            g = g_refs[0][0]
            for k in range(1, ng):
                g = g + g_refs[0][k]
        else:
            g = g_refs[0][...]
            for gr in g_refs[1:]:
                g = g + gr[...]
        wv = w_ref[...]
        mn = ADAM_B1 * m_ref[...] + (1.0 - ADAM_B1) * g
        vn = ADAM_B2 * v_ref[...] + (1.0 - ADAM_B2) * (g * g)
        m_hat = mn / (1.0 - ADAM_B1 ** ADAM_STEP)
        v_hat = vn / (1.0 - ADAM_B2 ** ADAM_STEP)
        g_out[...] = g
        d_out[...] = -ADAM_LR * (m_hat / (jnp.sqrt(v_hat) + ADAM_EPS) + ADAM_WD * wv)
        m_out[...] = mn
        v_out[...] = vn

    spec = pl.BlockSpec((tr, wd), lambda i: (i, 0))
    gspecs = [pl.BlockSpec((ng, tr, wd), lambda i: (0, i, 0))] if stacked else [spec] * ng
    return pl.pallas_call(
        kern, name=name, grid=(r // tr,), in_specs=[spec] * 3 + gspecs, out_specs=[spec] * 4,
        out_shape=[SDS((r, wd), F32)] * 4, compiler_params=_cparams(("parallel",)),
    )(w, m, v, *([grads] if stacked else grads))


def add2(name, a, b):
    k, r, w = a.shape
    tr = _pick(r, [t for t in ELEMENTWISE_ROWS if t * w * 4 <= ELEMENTWISE_BLOCK_BYTES])

    def kern(a_ref, b_ref, o_ref):
        o_ref[...] = a_ref[...] + b_ref[...]

    spec = pl.BlockSpec((1, tr, w), lambda s, i: (s, i, 0))
    return pl.pallas_call(kern, name=name, grid=(k, r // tr), in_specs=[spec, spec], out_specs=spec,
                          out_shape=SDS(a.shape, a.dtype), compiler_params=_cparams(("parallel", "parallel")))(a, b)


def _rep4(a, off):
    return jnp.concatenate([a[:, off + HEAD_DIM * h: off + HEAD_DIM * (h + 1)] for h in range(N_KV) for _ in range(GROUPS)], axis=1)


def _extend_cols(a):
    return jnp.concatenate([a[:, 0:OFF_KA], _rep4(a, OFF_KA), _rep4(a, OFF_VA), a[:, OFF_QB:OFF_KB],
                            _rep4(a, OFF_KB), _rep4(a, OFF_VB), a[:, OFF_GA:]], axis=1)


def _fold4(a, off):
    r = a.shape[0]
    return a[:, off:off + Q_W].reshape(r, N_KV, GROUPS, HEAD_DIM).sum(axis=2).reshape(r, N_KV * HEAD_DIM)


def _fold_cols(a):
    return jnp.concatenate([a[:, X_QA:X_QA + Q_W], _fold4(a, X_KA), _fold4(a, X_VA), a[:, X_QB:X_QB + Q_W],
                            _fold4(a, X_KB), _fold4(a, X_VB), a[:, X_GL:]], axis=1)


def _rope_tables(s_rows, c_rows):
    pos = jnp.arange(s_rows, dtype=jnp.int32)
    rows = (pos // GRID_W).astype(F32)
    cols = (pos % GRID_W).astype(F32)
    n_freq = HEAD_DIM // 4
    inv_freq = ROPE_THETA ** (-jnp.arange(n_freq, dtype=F32) / n_freq)
    ang_r = rows[:, None] * inv_freq
    ang_c = cols[:, None] * inv_freq
    cos = jnp.concatenate([jnp.cos(ang_r)] * 2 + [jnp.cos(ang_c)] * 2, axis=1)
    sin = jnp.concatenate([-jnp.sin(ang_r), jnp.sin(ang_r), -jnp.sin(ang_c), jnp.sin(ang_c)], axis=1)
    cos = jnp.concatenate([cos, jnp.ones((c_rows, HEAD_DIM), F32)], axis=0)
    sin = jnp.concatenate([sin, jnp.zeros((c_rows, HEAD_DIM), F32)], axis=0)
    return jnp.concatenate([cos, cos], axis=1), jnp.concatenate([sin, sin], axis=1)


def _ff_pad_cols(a):
    r = a.shape[0]
    a = jnp.pad(a.reshape(r, N_DEV, FF_SHARD), ((0, 0), (0, 0), (0, FF_SHARD_PAD - FF_SHARD)))
    return a.reshape(r, 2 * FF)


def _ff_unpad_cols(a):
    r = a.shape[0]
    return a.reshape(r, N_DEV, FF_SHARD_PAD)[:, :, :FF_SHARD].reshape(r, 2 * D_FF)


def _ff_pad_rows(a):
    c = a.shape[1]
    a = jnp.pad(a.reshape(N_DEV // 2, FF_SHARD, c), ((0, 0), (0, FF_SHARD_PAD - FF_SHARD), (0, 0)))
    return a.reshape(FF, c)


def _ff_unpad_rows(a):
    c = a.shape[1]
    return a.reshape(N_DEV // 2, FF_SHARD_PAD, c)[:, :FF_SHARD].reshape(D_FF, c)


BIG = (("w_in", (D, IN_COLS // N_DEV)), ("w_branch_a", (Q_W, D // N_DEV)), ("w_branch_b", (Q_W, D // N_DEV)),
       ("w_out", (D // N_DEV, D)), ("w_up", (D, FF_SHARD_PAD)), ("w_down", (D_FF // N_DEV, D)))
BIG_SIZES = tuple(int(np.prod(s)) for _, s in BIG)
BIG_ROWS = sum(BIG_SIZES) // LANES


def _pack_big(parts):
    return jnp.concatenate([p.reshape(-1) for p in parts]).reshape(BIG_ROWS, LANES)


def _unpack_big(flat):
    lead = flat.shape[:-2]
    f = flat.reshape(*lead, BIG_ROWS * LANES)
    out, off = [], 0
    for (_, shp), sz in zip(BIG, BIG_SIZES, strict=True):
        out.append(f[..., off:off + sz].reshape(*lead, *shp))
        off += sz
    return out


def _cols_to_full(g):
    return jnp.transpose(g, (1, 0, 2)).reshape(g.shape[1], -1)


def _full_to_cols(a):
    r, c = a.shape
    return jnp.transpose(a.reshape(r, N_DEV, c // N_DEV), (1, 0, 2))


SMALL = (("c_ctx", D), ("b_mod", N_MOD * D), ("b_in", IN_COLS), ("attn_sink", N_HEADS), ("q_norm_g", HEAD_DIM),
         ("k_norm_g", HEAD_DIM), ("ln1_g", D), ("ln1_b", D), ("conv_w", 3 * 2 * D_FF // N_DEV), ("conv_b", 2 * D_FF),
         ("ln2_g", D), ("ln2_b", D))
SMALL_TOTAL = sum(n for _, n in SMALL)
SMALL_ROWS = -(-SMALL_TOTAL // (8 * LANES)) * 8


def _pack_small(parts):
    flat = jnp.concatenate([p.reshape(-1).astype(F32) for p in parts])
    return jnp.pad(flat, (0, SMALL_ROWS * LANES - flat.shape[0])).reshape(SMALL_ROWS, LANES)


def _unpack_small(packed):
    f = packed.reshape(-1)
    out, off = {}, 0
    for name, n in SMALL:
        out[name] = f[off:off + n]
        off += n
    return out


RED = (("c_ctx", D), ("b_in", IN_COLS), ("attn_sink", N_HEADS), ("q_norm_g", HEAD_DIM), ("k_norm_g", HEAD_DIM),
       ("ln1_g", D), ("ln1_b", D), ("conv_w", 3 * 2 * FF), ("conv_b", 2 * FF), ("ln2_g", D), ("ln2_b", D))
RED_TOTAL = sum(n for _, n in RED)
RED_ROWS = -(-RED_TOTAL // (8 * LANES)) * 8


def sum8(name, g):
    _, r, w = g.shape

    def kern(g_ref, o_ref):
        acc = g_ref[0]
        for k in range(1, N_DEV):
            acc = acc + g_ref[k]
        o_ref[...] = acc

    return pl.pallas_call(kern, name=name, out_shape=SDS((r, w), F32))(g)


def _local_step(x, ctx, target, modrows, weights, small):
    s_rows, c_rows = x.shape[0], ctx.shape[0]
    n = s_rows + c_rows
    nl = s_rows // TM
    w_in, wba, wbb, w_out, w_up, w_down = weights
    f = lambda a: a.reshape(1, -1).astype(F32)
    b_in, ln1_g, ln1_b, ln2_g, ln2_b, conv_b = (f(small[k]) for k in ("b_in", "ln1_g", "ln1_b", "ln2_g", "ln2_b", "conv_b"))
    conv_w8 = jnp.pad(small["conv_w_full"], ((0, 5), (0, 0)))
    qg = jnp.tile(small["q_norm_g"].reshape(1, HEAD_DIM), (1, N_HEADS))
    kg = jnp.tile(small["k_norm_g"].reshape(1, HEAD_DIM), (1, N_HEADS))
    sink_rep = jnp.repeat(small["attn_sink"].reshape(N_KV, GROUPS), TW, axis=1)
    sinkrow = sink_rep.reshape(N_KV, 1, WR)
    sinkcol = jnp.broadcast_to(sink_rep[:, :, None], (N_KV, WR, LANES))
    bd = jnp.kron(jnp.eye(N_HEADS, dtype=F32), jnp.ones((HEAD_DIM, HEAD_DIM), F32)).astype(BF16)
    cos, sin = _rope_tables(s_rows, c_rows)
    w_ext = _extend_cols(w_in)
    b_ext = _extend_cols(b_in)
    xa = jnp.concatenate([x, ctx], axis=0)

    hb, qa, kat, vat, qb, kbt, vbt, tq, rq, tk, rk, gl = inproj_fwd(xa, cos, sin, modrows, w_ext, b_ext, qg, kg, bd, nl)
    compact_t = lambda t: jnp.stack([t[:, h * KV_W:h * KV_W + HEAD_DIM].T for h in range(N_KV)])
    oa, lse_a = win_fwd(qa, kat, compact_t(vat), sinkrow, s_rows, c_rows)
    vb_t = compact_t(vbt)
    ob, lse_b = glob_fwd(qb, kbt, vb_t, s_rows)
    tqb = _pick(s_rows, GLOB_BWD_TQ)
    lse_b = lse_b.reshape(N_KV, s_rows // tqb, GROUPS * tqb, 1)
    ya, yb, mrg, y, xhat1, rstd1 = merge_fwd(oa, ob, gl, x, modrows, wba, wbb, w_out, s_rows)
    h2, u0 = ffn_up_fwd(xhat1, modrows, ln1_g, ln1_b, w_up, s_rows)
    a = conv_swiglu_fwd(u0, conv_w8, conv_b, s_rows)
    (dr2, dy2), (loss, dln2_g, dln2_b, dgate2) = ffn_down_loss(a, xhat1, target, modrows, ln1_g, ln1_b, ln2_g, ln2_b, w_down, s_rows)

    da = ffn_down_bwd(dy2, w_down.T, s_rows)
    dw_down = mm_tn("dw_down", a, dy2, s_rows)
    du0, (dconv_b, dcw0, dcw1, dcw2) = swiglu_conv_bwd(u0, da, conv_w8, conv_b, s_rows)
    dw_up = mm_tn("dw_up", h2, du0, s_rows)
    (dy, dxp), (dscale2, dshift2, dln1_g, dln1_b, dgate1) = ffn_up_ln1_bwd(du0, dr2, xhat1, y, rstd1, modrows, ln1_g, ln1_b, w_up.T, s_rows)
    dya, dyb, dgl, doa, dob = merge_bwd(dy, ya, yb, gl, w_out.T, wba.T, wbb.T, s_rows)
    dw_out = mm_tn("dw_out", mrg, dy, s_rows)
    dwba = mm_tn("dw_branch_a", oa, dya, s_rows)
    dwbb = mm_tn("dw_branch_b", ob, dyb, s_rows)

    pad = jnp.zeros((n, KV_W - HEAD_DIM), F32)
    spread = lambda per_head: jnp.concatenate([t for th in per_head for t in (th.T, pad)], axis=1)
    dqa, dka_t, dva_t, dsk = win_bwd(qa, qa[:s_rows].T, kat, vat, sinkcol, oa, doa, doa.astype(MXU_DTYPE).T, lse_a,
                                     s_rows, c_rows)
    dka = spread([dka_t[h, :, WIN:] for h in range(N_KV)])
    dva = spread([dva_t[h, :, WIN:] for h in range(N_KV)])
    delta_b = attn_delta(ob, dob, s_rows)
    qb_t = qb[:s_rows].T
    dob_t = dob.astype(MXU_DTYPE).T
    heads = [glob_bwd(qb, qb_t, kbt, vbt, dob, dob_t, lse_b[h], delta_b[h], h, s_rows) for h in range(N_KV)]
    dqb = jnp.concatenate([hd[0] for hd in heads], axis=1)
    dkb = spread([hd[1] for hd in heads])
    dvb = spread([hd[2] for hd in heads])
    dproj, (db_ext, dqg, dkg) = qk_bwd(dqa, dka, dva, dqb, dkb, dvb, dgl, tq, rq, tk, rk, cos, sin, qg, kg, bd, nl, n)
    w_ext_t = w_ext.T
    (grad_x,), (dscale1, dshift1) = inproj_bwd("inproj_bwd", dproj, xa, dxp, modrows, w_ext_t, ntiles=nl, tile_off=0,
                                               is_ctx=False, out_rows=s_rows)
    _, (dscale_c, dshift_c) = inproj_bwd("inproj_bwd_ctx", dproj, xa, None, modrows, w_ext_t, ntiles=c_rows // TM,
                                         tile_off=nl, is_ctx=True, out_rows=0)
    dw_in = _fold_cols(mm_tn("dw_in", hb, dproj, n))

    dmod = jnp.concatenate([dshift1, dscale1, dgate1, dshift2, dscale2, dgate2], axis=1)
    dmod_c = jnp.concatenate([dshift_c, dscale_c, jnp.zeros((1, (N_MOD - 2) * D), F32)], axis=1)
    fold_g = lambda t: t.reshape(N_HEADS, HEAD_DIM).sum(axis=0)
    red = {
        "b_in": _fold_cols(db_ext), "attn_sink": dsk.reshape(N_HEADS, TW).sum(axis=1), "q_norm_g": fold_g(dqg),
        "k_norm_g": fold_g(dkg), "ln1_g": dln1_g, "ln1_b": dln1_b, "conv_w": jnp.concatenate([dcw0, dcw1, dcw2], axis=0),
        "conv_b": dconv_b, "ln2_g": dln2_g, "ln2_b": dln2_b,
    }
    return loss[0, 0], grad_x, (dw_in, dwba, dwbb, dw_out, dw_up, dw_down), dmod, dmod_c, red


def kernel(x, c, ctx, c_ctx, w_mod, b_mod, w_in, b_in, attn_sink, q_norm_g, k_norm_g, w_branch_a, w_branch_b, w_out, ln1_g, ln1_b, w_up, conv_w, conv_b, w_down, ln2_g, ln2_b, loss_target, m_c_ctx, m_w_mod, m_b_mod, m_w_in, m_b_in, m_attn_sink, m_q_norm_g, m_k_norm_g, m_w_branch_a, m_w_branch_b, m_w_out, m_ln1_g, m_ln1_b, m_w_up, m_conv_w, m_conv_b, m_w_down, m_ln2_g, m_ln2_b, v_c_ctx, v_w_mod, v_b_mod, v_w_in, v_b_in, v_attn_sink, v_q_norm_g, v_k_norm_g, v_w_branch_a, v_w_branch_b, v_w_out, v_ln1_g, v_ln1_b, v_w_up, v_conv_w, v_conv_b, v_w_down, v_ln2_g, v_ln2_b):
    ax, ay, ac = (lax.axis_index(a) for a in AXES)
    me = 4 * ax + 2 * ay + ac
    chip = 2 * ax + ay
    mod_w = N_MOD * D // N_DEV
    params = dict(c_ctx=c_ctx, w_mod=w_mod, b_mod=b_mod, w_in=w_in, b_in=b_in, attn_sink=attn_sink, q_norm_g=q_norm_g,
                  k_norm_g=k_norm_g, w_branch_a=w_branch_a, w_branch_b=w_branch_b, w_out=w_out, ln1_g=ln1_g, ln1_b=ln1_b,
                  w_up=w_up, conv_w=conv_w, conv_b=conv_b, w_down=w_down, ln2_g=ln2_g, ln2_b=ln2_b)
    mom_m = dict(c_ctx=m_c_ctx, w_mod=m_w_mod, b_mod=m_b_mod, w_in=m_w_in, b_in=m_b_in, attn_sink=m_attn_sink,
                 q_norm_g=m_q_norm_g, k_norm_g=m_k_norm_g, w_branch_a=m_w_branch_a, w_branch_b=m_w_branch_b, w_out=m_w_out,
                 ln1_g=m_ln1_g, ln1_b=m_ln1_b, w_up=m_w_up, conv_w=m_conv_w, conv_b=m_conv_b, w_down=m_w_down,
                 ln2_g=m_ln2_g, ln2_b=m_ln2_b)
    mom_v = dict(c_ctx=v_c_ctx, w_mod=v_w_mod, b_mod=v_b_mod, w_in=v_w_in, b_in=v_b_in, attn_sink=v_attn_sink,
                 q_norm_g=v_q_norm_g, k_norm_g=v_k_norm_g, w_branch_a=v_w_branch_a, w_branch_b=v_w_branch_b, w_out=v_w_out,
                 ln1_g=v_ln1_g, ln1_b=v_ln1_b, w_up=v_w_up, conv_w=v_conv_w, conv_b=v_conv_b, w_down=v_w_down,
                 ln2_g=v_ln2_g, ln2_b=v_ln2_b)
    big_names = [nm for nm, _ in BIG]

    def shard(tree, nm):
        t = tree[nm][0]
        return jnp.pad(t, ((0, 0), (0, FF_SHARD_PAD - FF_SHARD))) if nm == "w_up" else t

    wg = all_gather("ag_weights", _pack_big([shard(params, nm).astype(MXU_DTYPE) for nm in big_names]))
    g_in, g_ba, g_bb, g_out, g_up, g_down = _unpack_big(wg)
    weights = (_cols_to_full(g_in), _cols_to_full(g_ba), _cols_to_full(g_bb), g_out.reshape(D, D), _cols_to_full(g_up),
               _ff_pad_rows(g_down.reshape(D_FF, D)))

    c_all = all_gather("ag_c", c.reshape(8, LANES)).reshape(N_DEV, D)
    cs = jnp.concatenate([c_all, c_ctx.reshape(1, D), jnp.zeros((7, D), F32)], axis=0)
    w_mod_sh = w_mod[0]
    b_mod_sh = lax.dynamic_slice(b_mod, (0, me * mod_w), (1, mod_w))
    mod_part = mod_fwd(cs, w_mod_sh, b_mod_sh)
    mg = all_gather("ag_mod", mod_part.reshape(16 * mod_w // LANES, LANES)).reshape(N_DEV, 16, mod_w)
    mod = lax.dynamic_index_in_dim(mg, me, axis=1, keepdims=False).reshape(N_MOD, D)
    mod_c = mg[:, 8, :].reshape(N_MOD, D)
    modrows = jnp.stack([mod[0], mod[1], mod_c[0], mod_c[1], mod[2], mod[3], mod[4], mod[5]], axis=0)

    conv_w_full = all_gather("ag_conv_w", jnp.pad(conv_w[0], ((0, 5), (0, FF_SHARD_PAD - FF_SHARD))))
    conv_w_full = _cols_to_full(conv_w_full[:, :3, :])
    small = dict(b_in=b_in, ln1_g=ln1_g, ln1_b=ln1_b, ln2_g=ln2_g, ln2_b=ln2_b, conv_b=_ff_pad_cols(conv_b),
                 conv_w_full=conv_w_full, q_norm_g=q_norm_g, k_norm_g=k_norm_g, attn_sink=attn_sink)
    loss, grad_x, big_grads, dmod, dmod_c, red = _local_step(x[0], ctx[0], loss_target[0], modrows, weights, small)
    loss = lax.psum(loss, AXES)

    dm = all_gather("ag_dmod", jnp.concatenate([dmod, dmod_c], axis=0).reshape(2 * N_MOD * D // LANES, LANES))
    dm = dm.reshape(N_DEV, 2, N_MOD * D)
    dm_all = jnp.concatenate([dm[:, 0], dm[:, 1]], axis=0)
    dm_sh = lax.dynamic_slice(dm_all, (0, me * mod_w), (16, mod_w))
    dw_mod, dcc, db_mod = mod_bwd(cs, w_mod_sh, dm_sh, dm_all)
    red["c_ctx"] = dcc[8]

    red_vec = jnp.concatenate([red[nm].reshape(-1) for nm, _ in RED])
    red_vec = jnp.pad(red_vec, (0, RED_ROWS * LANES - RED_TOTAL)).reshape(RED_ROWS, LANES)
    red_sum = sum8("sum_small", all_gather("ag_small", red_vec)).reshape(-1)
    gsm, off = {}, 0
    for nm, k in RED:
        gsm[nm] = red_sum[off:off + k]
        off += k
    gsm["b_mod"] = db_mod.reshape(-1)
    gsm["conv_b"] = _ff_unpad_cols(gsm["conv_b"].reshape(1, 2 * FF))
    gsm["conv_w"] = lax.dynamic_slice(gsm["conv_w"].reshape(3, 2 * FF), (0, me * FF_SHARD_PAD), (3, FF_SHARD_PAD))[:, :FF_SHARD]
    sm_names = [nm for nm, _ in SMALL]
    gs, ds, ms, vs = adamw("adamw_small", _pack_small([params[nm] for nm in sm_names]),
                           _pack_small([mom_m[nm] for nm in sm_names]), _pack_small([mom_v[nm] for nm in sm_names]),
                           [_pack_small([gsm[nm] for nm in sm_names])])
    sm_out = [_unpack_small(t) for t in (gs, ds, ms, vs)]

    dw_in, dwba, dwbb, dw_out, dw_up, dw_down = big_grads
    slabs = jnp.concatenate([t.reshape(N_DEV, -1) for t in (
        _full_to_cols(dw_in), _full_to_cols(dwba), _full_to_cols(dwbb), dw_out, _full_to_cols(dw_up),
        _ff_unpad_rows(dw_down))], axis=1)
    by_core = slabs.reshape(4, 2, BIG_ROWS, LANES)
    keep = lax.dynamic_index_in_dim(by_core, ac, axis=1, keepdims=False)
    give = lax.dynamic_index_in_dim(by_core, 1 - ac, axis=1, keepdims=False)
    got = exchange("rs_sibling", give.reshape(1, 4 * BIG_ROWS, LANES), to_chips=False).reshape(4, BIG_ROWS, LANES)
    pair = add2("rs_pair_sum", keep, got)
    outbox = jnp.stack([lax.dynamic_index_in_dim(pair, jnp.bitwise_xor(chip, m), axis=0, keepdims=False) for m in (1, 2, 3)])
    inbox = exchange("rs_chips", outbox.astype(MXU_DTYPE), to_chips=True)
    mine = lax.dynamic_index_in_dim(pair, chip, axis=0, keepdims=False)
    gb, db, mb, vb = adamw("adamw_big", _pack_big([shard(params, nm) for nm in big_names]),
                           _pack_big([shard(mom_m, nm) for nm in big_names]), _pack_big([shard(mom_v, nm) for nm in big_names]),
                           [mine, inbox[0], inbox[1], inbox[2]])
    big_out = [dict(zip(big_names, _unpack_big(t), strict=True)) for t in (gb, db, mb, vb)]
    for out in big_out:
        out["w_up"] = out["w_up"][:, :FF_SHARD]
    gm, dmo, mmo, vmo = adamw("adamw_mod", w_mod[0], m_w_mod[0], v_w_mod[0], [dw_mod])
    mod_out = (gm, dmo, mmo, vmo)

    order = ["c_ctx", "w_mod", "b_mod", "w_in", "b_in", "attn_sink", "q_norm_g", "k_norm_g", "w_branch_a", "w_branch_b",
             "w_out", "ln1_g", "ln1_b", "w_up", "conv_w", "conv_b", "w_down", "ln2_g", "ln2_b"]
    results = [loss, grad_x[None]]
    for kind in range(4):
        for nm in order:
            if nm == "w_mod":
                val = mod_out[kind]
            elif nm in big_out[kind]:
                val = big_out[kind][nm]
            else:
                val = sm_out[kind][nm]
            results.append(val.reshape(params[nm].shape))
    return tuple(results)
```

```python
import functools

import jax
import jax.numpy as jnp
import numpy as np
from jax import lax
from jax.experimental import pallas as pl
from jax.experimental.pallas import tpu as pltpu

F32 = jnp.float32
BF16 = jnp.bfloat16
MXU_DTYPE = BF16

AXES = ("x", "y", "c")
N_DEV = 8
D = 1024
HEAD_DIM = 64
N_HEADS = 8
N_KV = 2
GROUPS = 4
KV_W = GROUPS * HEAD_DIM
Q_W = N_HEADS * HEAD_DIM
GRID_W = 64
WIN = 128
ROPE_THETA = 10000.0
D_FF = 2816
FF_SHARD = 2 * D_FF // N_DEV
FF_SHARD_PAD = 768
FF = N_DEV // 2 * FF_SHARD_PAD
LN_EPS = 1e-5
QK_EPS = 1e-6
N_MOD = 6
ALPHA = 2.0 ** 0.25
Q_SCALE = HEAD_DIM ** -0.5
IN_COLS = 3584
OFF_KA, OFF_VA, OFF_QB, OFF_KB, OFF_VB, OFF_GA = 512, 640, 768, 1280, 1408, 1536
EXT_COLS = 6 * Q_W + 2 * D
X_QA, X_KA, X_VA, X_QB, X_KB, X_VB, X_GL = 0, 512, 1024, 1536, 2048, 2560, 3072
ADAM_LR, ADAM_B1, ADAM_B2, ADAM_EPS, ADAM_WD, ADAM_STEP = 0.001, 0.9, 0.999, 1e-08, 0.01, 10
LANES = 128
TM = 256
VMEM_LIMIT = 56 * 1024 * 1024
ELEMENTWISE_BLOCK_BYTES = 1 << 20
ELEMENTWISE_ROWS = (1824, 1408, 1024, 512, 256, 128, 64, 32, 16, 8)

ANY = pl.BlockSpec(memory_space=pl.ANY)
SDS = jax.ShapeDtypeStruct


def _pick(n, candidates):
    for t in candidates:
        if n % t == 0:
            return t
    raise ValueError(f"no tile for {n}")


def _full(a):
    nd = a.ndim
    return pl.BlockSpec(a.shape, lambda *_: (0,) * nd)


def _rows(tm, w, fn=lambda t: t):
    return pl.BlockSpec((tm, w), lambda i: (fn(i), 0))


def _dot(a, b):
    return jnp.dot(a.astype(MXU_DTYPE), b.astype(MXU_DTYPE), preferred_element_type=F32)


def _dot_nt(a, b):
    return lax.dot_general(a.astype(MXU_DTYPE), b.astype(MXU_DTYPE), (((1,), (1,)), ((), ())), preferred_element_type=F32)


def _dot_tn(a, b):
    return lax.dot_general(a.astype(MXU_DTYPE), b.astype(MXU_DTYPE), (((0,), (0,)), ((), ())), preferred_element_type=F32)


def _cparams(sem):
    return pltpu.CompilerParams(dimension_semantics=sem, vmem_limit_bytes=VMEM_LIMIT)


def all_gather(name, v):
    r, w = v.shape

    def body(x_ref, out_ref, send_sems, recv_sems, local_sem):
        x, y, c = (lax.axis_index(a) for a in AXES)
        me, sibling = (x, y, c), (x, y, 1 - c)
        chips = [(1 - x, y), (x, 1 - y), (1 - x, 1 - y)]

        def rows(px, py, pc):
            return out_ref.at[4 * px + 2 * py + pc]

        def copy(k, block, to, src=None):
            return pltpu.make_async_remote_copy(
                src_ref=rows(*block) if src is None else src, dst_ref=rows(*block),
                send_sem=send_sems.at[k], recv_sem=recv_sems.at[k],
                device_id=to, device_id_type=pl.DeviceIdType.MESH)

        mine = pltpu.make_async_copy(x_ref, rows(*me), local_sem)
        mine.start()
        first = [copy(0, me, sibling, src=x_ref)]
        first += [copy(1 + j, me, (*chip, c), src=x_ref) for j, chip in enumerate(chips)]
        for cp in first:
            cp.start()
        passed = [copy(4 + j, (*chip, c), sibling) for j, chip in enumerate(chips)]
        for j, chip in enumerate(chips):
            copy(1 + j, (*chip, c), me).wait_recv()
            passed[j].start()
        copy(0, sibling, me).wait_recv()
        for j, chip in enumerate(chips):
            copy(4 + j, (*chip, 1 - c), me).wait_recv()
        for cp in first + passed:
            cp.wait_send()
        mine.wait()

    return pl.pallas_call(
        body, name=name, out_shape=SDS((N_DEV, r, w), v.dtype), in_specs=[ANY], out_specs=ANY,
        scratch_shapes=[pltpu.SemaphoreType.DMA((7,)), pltpu.SemaphoreType.DMA((7,)), pltpu.SemaphoreType.DMA],
    )(v)


def exchange(name, outbox, to_chips):
    k = outbox.shape[0]
    assert k == (3 if to_chips else 1)

    def body(out_ref, in_ref, send_sems, recv_sems):
        x, y, c = (lax.axis_index(a) for a in AXES)
        peers = [(x, 1 - y, c), (1 - x, y, c), (1 - x, 1 - y, c)] if to_chips else [(x, y, 1 - c)]
        copies = [
            pltpu.make_async_remote_copy(
                src_ref=out_ref.at[m], dst_ref=in_ref.at[m], send_sem=send_sems.at[m], recv_sem=recv_sems.at[m],
                device_id=peer, device_id_type=pl.DeviceIdType.MESH)
            for m, peer in enumerate(peers)
        ]
        for cp in copies:
            cp.start()
        for cp in copies:
            cp.wait_recv()
        for cp in copies:
            cp.wait_send()

    return pl.pallas_call(
        body, name=name, out_shape=SDS(outbox.shape, outbox.dtype), in_specs=[ANY], out_specs=ANY,
        scratch_shapes=[pltpu.SemaphoreType.DMA((k,)), pltpu.SemaphoreType.DMA((k,))],
    )(outbox)


def rowwise(name, body, *, ntiles, tile_off=0, tiled, full, outs, accs=()):
    nt, nf, no = len(tiled), len(full), len(outs)

    def kern(*refs):
        i = pl.program_id(0)
        out_vals, incs = body(i + tile_off, refs[:nt], refs[nt:nt + nf])
        for r, v in zip(refs[nt + nf:nt + nf + no], out_vals, strict=True):
            r[...] = v.astype(r.dtype)
        acc_refs = refs[nt + nf + no:]

        @pl.when(i == 0)
        def _():
            for r in acc_refs:
                r[...] = jnp.zeros_like(r)

        for r, v in zip(acc_refs, incs, strict=True):
            r[...] += v

    res = pl.pallas_call(
        kern, name=name, grid=(ntiles,),
        in_specs=[s for _, s in tiled] + [_full(a) for a in full],
        out_specs=[s for _, _, s in outs] + [pl.BlockSpec(s, lambda i, n=len(s): (0,) * n) for s in accs],
        out_shape=[SDS(s, d) for s, d, _ in outs] + [SDS(s, F32) for s in accs],
        compiler_params=_cparams(("arbitrary",) if accs else ("parallel",)),
    )(*[a for a, _ in tiled], *full)
    return res[:no], res[no:]


def mm_tn(name, a, b, rows):
    ka, nb = a.shape[1], b.shape[1]
    tr = _pick(rows, (1280, 1024, 768, 512, 256))
    tn = _pick(nb, (512, 256, 128))

    def kern(a_ref, b_ref, o_ref):
        @pl.when(pl.program_id(1) == 0)
        def _():
            o_ref[...] = jnp.zeros_like(o_ref)

        o_ref[...] += _dot_tn(a_ref[...], b_ref[...])

    return pl.pallas_call(
        kern, name=name, grid=(nb // tn, rows // tr),
        in_specs=[pl.BlockSpec((tr, ka), lambda n, r: (r, 0)), pl.BlockSpec((tr, tn), lambda n, r: (r, n))],
        out_specs=pl.BlockSpec((ka, tn), lambda n, r: (0, n)), out_shape=SDS((ka, nb), F32),
        compiler_params=_cparams(("parallel", "arbitrary")),
    )(a, b)


def _swap16(t):
    w = t.shape[1]
    lane = lax.broadcasted_iota(jnp.int32, t.shape, 1)
    return jnp.where((lane & 16) == 0, pltpu.roll(t, w - 16, 1), pltpu.roll(t, 16, 1))


def _rope(t, cos, sin):
    return t * cos + _swap16(t) * sin


def _rope_t(d, cos, sin):
    return d * cos - _swap16(d) * sin


def _seg_sum64(a, bd_ref):
    bd = bd_ref[...]
    hi = a.astype(BF16)
    lo = (a - hi.astype(F32)).astype(BF16)
    return jnp.dot(hi, bd, preferred_element_type=F32) + jnp.dot(lo, bd, preferred_element_type=F32)


def _lane_block(shape):
    return jnp.right_shift(lax.broadcasted_iota(jnp.int32, shape, 1), 6)


def _stack_groups(t, dtype):
    blk = _lane_block(t.shape)
    return jnp.concatenate([jnp.where(blk == g, t, jnp.zeros_like(t)).astype(dtype) for g in range(GROUPS)], axis=0)


def _fold_groups(ts, tq):
    blk = _lane_block((tq, KV_W))
    out = jnp.zeros((tq, KV_W), ts.dtype)
    for g in range(GROUPS):
        out = jnp.where(blk == g, ts[g * tq:(g + 1) * tq], out)
    return out


def _stacked_heads(t):
    return jnp.stack([_stack_groups(t[:, h * KV_W:(h + 1) * KV_W].astype(MXU_DTYPE), MXU_DTYPE) for h in range(N_KV)])


def _stacked_out(rows):
    return ((N_KV, GROUPS * rows, KV_W), MXU_DTYPE, pl.BlockSpec((N_KV, GROUPS * TM, KV_W), lambda i: (0, i, 0)))


def _row_to_col(row, lanes=1):
    hi = row.astype(BF16)
    r1 = row - hi.astype(F32)
    mid = r1.astype(BF16)
    lo = (r1 - mid.astype(F32)).astype(BF16)
    ones = jnp.ones((8, LANES), BF16)
    pad = jnp.zeros((7, row.shape[1]), BF16)
    acc = jnp.zeros((row.shape[1], LANES), F32)
    for term in (hi, mid, lo):
        acc = acc + lax.dot_general(jnp.concatenate([term, pad], axis=0), ones, (((0,), (0,)), ((), ())),
                                    preferred_element_type=F32)
    return acc[:, 0:lanes]


def _stack_tiles(t, dtype):
    return jnp.concatenate([_stack_groups(t[a:a + TM], dtype) for a in range(0, t.shape[0], TM)], axis=0)


def _fold_tiles(ts, tq):
    return jnp.concatenate([_fold_groups(ts[GROUPS * a:GROUPS * (a + TM)], TM) for a in range(0, tq, TM)], axis=0)


def _compact_tiles_t(tt, dtype):
    return jnp.concatenate([tt[g * HEAD_DIM:(g + 1) * HEAD_DIM, a:a + TM] for a in range(0, tt.shape[1], TM)
                            for g in range(GROUPS)], axis=1).astype(dtype)


def _layer_norm_bwd(dxh, xhat, rstd):
    m1 = jnp.mean(dxh, axis=1, keepdims=True)
    m2 = jnp.mean(dxh * xhat, axis=1, keepdims=True)
    return rstd * (dxh - m1 - xhat * m2)


def _colsum(a):
    return jnp.sum(a, axis=0, keepdims=True)


def _shifted_rows(t, prev_row, next_row):
    n = t.shape[0]
    row = lax.broadcasted_iota(jnp.int32, t.shape, 0)
    up = jnp.where(row == 0, prev_row, pltpu.roll(t, 1, 0))
    dn = jnp.where(row == n - 1, next_row, pltpu.roll(t, n - 1, 0))
    return up, dn


def mod_fwd(cs, w_sh, b_sh):
    def kern(c_ref, w_ref, b_ref, o_ref):
        o_ref[...] = _dot(jax.nn.silu(c_ref[...]), w_ref[...]) + b_ref[...]

    return pl.pallas_call(kern, name="mod_fwd", out_shape=SDS((16, w_sh.shape[1]), F32),
                          compiler_params=pltpu.CompilerParams(vmem_limit_bytes=VMEM_LIMIT))(cs, w_sh, b_sh)


def mod_bwd(cs, w_sh, dm_sh, dm_all):
    hp = lax.Precision.HIGHEST

    def kern(c_ref, w_ref, dm_ref, da_ref, dw_ref, dc_ref, db_ref):
        c = c_ref[...]
        sg = jax.nn.sigmoid(c)
        sc = c * sg
        dm = dm_ref[...]
        dmc = dm_ref[8:9, :]
        for i in range(9, 16):
            dmc = dmc + dm_ref[i:i + 1, :]
        row = lax.broadcasted_iota(jnp.int32, dm.shape, 0)
        a = jnp.where(row < 8, dm, jnp.where(row == 8, dmc, 0.0))
        dw_ref[...] = lax.dot_general(sc, a, (((0,), (0,)), ((), ())), precision=hp, preferred_element_type=F32)
        dsc = lax.dot_general(a, w_ref[...], (((1,), (1,)), ((), ())), precision=hp, preferred_element_type=F32)
        dc_ref[...] = dsc * (sg * (1.0 + c * (1.0 - sg)))
        db = da_ref[0:1, :]
        for i in range(1, 16):
            db = db + da_ref[i:i + 1, :]
        db_ref[...] = db

    return pl.pallas_call(
        kern, name="mod_bwd",
        out_shape=[SDS(w_sh.shape, F32), SDS((16, D), F32), SDS((1, dm_all.shape[1]), F32)],
        compiler_params=pltpu.CompilerParams(vmem_limit_bytes=VMEM_LIMIT))(cs, w_sh, dm_sh, dm_all)


M_SHIFT1, M_SCALE1, M_SHIFTC, M_SCALEC, M_GATE1, M_SHIFT2, M_SCALE2, M_GATE2 = range(8)


def _mrow(ref, k):
    return ref[k:k + 1, :]


def inproj_fwd(xa, cos, sin, modrows, w_ext, b_ext, qg, kg, bd, n_lat_tiles):
    n = xa.shape[0]

    def body(t, vals, fr):
        x, cs, sn = (v[...] for v in vals)
        mod, w, b, qg_r, kg_r, bd_r = fr
        is_ctx = t >= n_lat_tiles
        shift = jnp.where(is_ctx, _mrow(mod, M_SHIFTC), _mrow(mod, M_SHIFT1))
        scale = jnp.where(is_ctx, _mrow(mod, M_SCALEC), _mrow(mod, M_SCALE1))
        hb = (x * (1.0 + scale) + shift).astype(MXU_DTYPE)
        proj = jnp.dot(hb, w[...], preferred_element_type=F32) + b[...]
        cos4 = jnp.concatenate([cs] * 4, axis=1)
        sin4 = jnp.concatenate([sn] * 4, axis=1)
        qa = _rope(proj[:, X_QA:X_QA + Q_W], cos4, sin4) * Q_SCALE
        ka = _rope(proj[:, X_KA:X_KA + Q_W], cos4, sin4)
        va = proj[:, X_VA:X_VA + Q_W]
        tq = proj[:, X_QB:X_QB + Q_W]
        rq = lax.rsqrt(_seg_sum64(tq * tq, bd_r) * (1.0 / HEAD_DIM) + QK_EPS)
        qb = _rope(tq * rq * qg_r[...], cos4, sin4) * Q_SCALE
        tk = proj[:, X_KB:X_KB + Q_W]
        rk = lax.rsqrt(_seg_sum64(tk * tk, bd_r) * (1.0 / HEAD_DIM) + QK_EPS)
        kb = _rope(tk * rk * kg_r[...], cos4, sin4)
        vb = proj[:, X_VB:X_VB + Q_W]
        gl = proj[:, X_GL:]
        return [hb, qa, ka, va, qb, kb, vb, tq, rq, tk, rk, gl, _stacked_heads(qb)], []

    mx = MXU_DTYPE
    outs = [((n, D), mx, _rows(TM, D))] + [((n, Q_W), mx, _rows(TM, Q_W))] * 6 + \
           [((n, Q_W), F32, _rows(TM, Q_W))] * 4 + [((n, 2 * D), F32, _rows(TM, 2 * D)), _stacked_out(n)]
    res, _ = rowwise("inproj_fwd", body, ntiles=n // TM,
                     tiled=[(xa, _rows(TM, D)), (cos, _rows(TM, LANES)), (sin, _rows(TM, LANES))],
                     full=[modrows, w_ext, b_ext, qg, kg, bd], outs=outs)
    return res


def merge_fwd(oa, ob, gl, x, modrows, wba, wbb, w_out, s_rows):
    def body(t, vals, fr):
        oa_, ob_, gl_, x_ = (v[...] for v in vals)
        mod, wa, wb, wo = fr
        ya = _dot(oa_, wa[...])
        yb = _dot(ob_, wb[...])
        ga = jax.nn.sigmoid(gl_[:, :D])
        gb = jax.nn.sigmoid(gl_[:, D:])
        mrg = ga * ya + gb * yb
        y = _dot(mrg, wo[...])
        r1 = ALPHA * x_ + _mrow(mod, M_GATE1) * y
        mu = jnp.mean(r1, axis=1, keepdims=True)
        xc = r1 - mu
        var = jnp.mean(xc * xc, axis=1, keepdims=True)
        rstd = lax.rsqrt(var + LN_EPS)
        xhat = xc * rstd
        return [ya, yb, mrg, y, xhat, rstd], []

    outs = [((s_rows, D), F32, _rows(TM, D))] * 2 + [((s_rows, D), MXU_DTYPE, _rows(TM, D))] + \
           [((s_rows, D), F32, _rows(TM, D))] * 2 + [((s_rows, 1), F32, _rows(TM, 1))]
    res, _ = rowwise("merge_fwd", body, ntiles=s_rows // TM,
                     tiled=[(oa, _rows(TM, Q_W)), (ob, _rows(TM, Q_W)), (gl, _rows(TM, 2 * D)), (x, _rows(TM, D))],
                     full=[modrows, wba, wbb, w_out], outs=outs)
    return res


def ffn_up_fwd(xhat1, modrows, ln_g, ln_b, w_up, s_rows):
    def body(t, vals, fr):
        xh = vals[0][...]
        mod, g_r, b_r, w = fr
        x1 = xh * g_r[...] + b_r[...]
        h2 = (x1 * (1.0 + _mrow(mod, M_SCALE2)) + _mrow(mod, M_SHIFT2)).astype(MXU_DTYPE)
        return [h2, jnp.dot(h2, w[...], preferred_element_type=F32)], []

    res, _ = rowwise("ffn_up_fwd", body, ntiles=s_rows // TM, tiled=[(xhat1, _rows(TM, D))],
                     full=[modrows, ln_g, ln_b, w_up],
                     outs=[((s_rows, D), MXU_DTYPE, _rows(TM, D)), ((s_rows, 2 * FF), F32, _rows(TM, 2 * FF))])
    return res


TC = 128


def _halo_specs(tm, w, s_rows):
    per = tm // 8
    last = s_rows // 8 - 1
    return (pl.BlockSpec((8, w), lambda i: (jnp.maximum(i * per - 1, 0), 0)),
            pl.BlockSpec((8, w), lambda i: (jnp.minimum((i + 1) * per, last), 0)))


def _halo_rows(t, ntiles, prev_ref, next_ref):
    prev_row = jnp.where(t == 0, 0.0, prev_ref[7:8, :].astype(F32))
    next_row = jnp.where(t == ntiles - 1, 0.0, next_ref[0:1, :].astype(F32))
    return prev_row, next_row


def conv_swiglu_fwd(u0, conv_w8, conv_b, s_rows):
    w2 = 2 * FF
    nt = s_rows // TC

    def body(t, vals, fr):
        u_ref, pv, nx = vals
        cw, cb = fr
        u = u_ref[...]
        up, dn = _shifted_rows(u, *_halo_rows(t, nt, pv, nx))
        uc = cw[0:1, :] * up + cw[1:2, :] * u + cw[2:3, :] * dn + cb[...]
        gate, val = uc[:, :FF], uc[:, FF:]
        return [gate * jax.nn.sigmoid(gate) * val], []

    hp, hn = _halo_specs(TC, w2, s_rows)
    res, _ = rowwise("conv_swiglu_fwd", body, ntiles=nt,
                     tiled=[(u0, _rows(TC, w2)), (u0, hp), (u0, hn)], full=[conv_w8, conv_b],
                     outs=[((s_rows, FF), MXU_DTYPE, _rows(TC, FF))])
    return res[0]


def ffn_down_loss(a, xhat1, target, modrows, ln1_g, ln1_b, ln2_g, ln2_b, w_down, s_rows):
    def body(t, vals, fr):
        a_, xh1, tgt = (v[...] for v in vals)
        mod, g1, b1, g2, b2, wd = fr
        y2 = jnp.dot(a_, wd[...], preferred_element_type=F32)
        x1 = xh1 * g1[...] + b1[...]
        gate2 = _mrow(mod, M_GATE2)
        r2 = ALPHA * x1 + gate2 * y2
        mu = jnp.mean(r2, axis=1, keepdims=True)
        xc = r2 - mu
        var = jnp.mean(xc * xc, axis=1, keepdims=True)
        rstd = lax.rsqrt(var + LN_EPS)
        xhat = xc * rstd
        out = xhat * g2[...] + b2[...]
        diff = out - tgt
        loss = 0.5 * jnp.sum(jnp.mean(diff * diff, axis=1, keepdims=True), axis=0, keepdims=True)
        dout = diff * (1.0 / D)
        dr2 = _layer_norm_bwd(dout * g2[...], xhat, rstd)
        incs = [loss, _colsum(dout * xhat), _colsum(dout), _colsum(dr2 * y2)]
        return [dr2, dr2 * gate2], incs

    res, accs = rowwise("ffn_down_loss", body, ntiles=s_rows // TM,
                        tiled=[(a, _rows(TM, FF)), (xhat1, _rows(TM, D)), (target, _rows(TM, D))],
                        full=[modrows, ln1_g, ln1_b, ln2_g, ln2_b, w_down],
                        outs=[((s_rows, D), F32, _rows(TM, D)), ((s_rows, D), MXU_DTYPE, _rows(TM, D))],
                        accs=[(1, 1), (1, D), (1, D), (1, D)])
    return res, accs


def ffn_down_bwd(dy2, w_down_t, s_rows):
    def body(t, vals, fr):
        return [jnp.dot(vals[0][...], fr[0][...], preferred_element_type=F32)], []

    res, _ = rowwise("ffn_down_bwd", body, ntiles=s_rows // TM, tiled=[(dy2, _rows(TM, D))], full=[w_down_t],
                     outs=[((s_rows, FF), F32, _rows(TM, FF))])
    return res[0]


def swiglu_conv_bwd(u0, da, conv_w8, conv_b, s_rows):
    w2 = 2 * FF
    nt = s_rows // TC
    n = TC + 16

    def body(t, vals, fr):
        u_ref, upv, unx, da_ref, apv, anx = vals
        cw, cb = fr
        first, last = t == 0, t == nt - 1
        ue = jnp.concatenate([jnp.where(first, 0.0, upv[...]), u_ref[...], jnp.where(last, 0.0, unx[...])], axis=0)
        ae = jnp.concatenate([jnp.where(first, 0.0, apv[...]), da_ref[...], jnp.where(last, 0.0, anx[...])], axis=0)
        up = pltpu.roll(ue, 1, 0)
        dn = pltpu.roll(ue, n - 1, 0)
        uc = cw[0:1, :] * up + cw[1:2, :] * ue + cw[2:3, :] * dn + cb[...]
        gate, val = uc[:, :FF], uc[:, FF:]
        sg = jax.nn.sigmoid(gate)
        du = jnp.concatenate([ae * val * (sg * (1.0 + gate * (1.0 - sg))), ae * (gate * sg)], axis=1)
        du0 = cw[0:1, :] * pltpu.roll(du, n - 1, 0) + cw[1:2, :] * du + cw[2:3, :] * pltpu.roll(du, 1, 0)
        rows = slice(8, 8 + TC)
        dut = du[rows]
        return [du0[rows]], [_colsum(dut), _colsum(up[rows] * dut), _colsum(ue[rows] * dut), _colsum(dn[rows] * dut)]

    hp, hn = _halo_specs(TC, w2, s_rows)
    ap, an = _halo_specs(TC, FF, s_rows)
    res, accs = rowwise("swiglu_conv_bwd", body, ntiles=nt,
                        tiled=[(u0, _rows(TC, w2)), (u0, hp), (u0, hn), (da, _rows(TC, FF)), (da, ap), (da, an)],
                        full=[conv_w8, conv_b], outs=[((s_rows, w2), MXU_DTYPE, _rows(TC, w2))], accs=[(1, w2)] * 4)
    return res[0], accs


def ffn_up_ln1_bwd(du0, dr2, xhat1, y, rstd1, modrows, ln_g, ln_b, w_up_t, s_rows):
    def body(t, vals, fr):
        du0_, dr2_, xh, y_, rstd = (v[...] for v in vals)
        mod, g_r, b_r, wt = fr
        dh2 = jnp.dot(du0_, wt[...], preferred_element_type=F32)
        x1 = xh * g_r[...] + b_r[...]
        dx1 = ALPHA * dr2_ + dh2 * (1.0 + _mrow(mod, M_SCALE2))
        dr1 = _layer_norm_bwd(dx1 * g_r[...], xh, rstd)
        incs = [_colsum(dh2 * x1), _colsum(dh2), _colsum(dx1 * xh), _colsum(dx1), _colsum(dr1 * y_)]
        return [dr1 * _mrow(mod, M_GATE1), ALPHA * dr1], incs

    res, accs = rowwise("ffn_up_ln1_bwd", body, ntiles=s_rows // TM,
                        tiled=[(du0, _rows(TM, 2 * FF)), (dr2, _rows(TM, D)), (xhat1, _rows(TM, D)), (y, _rows(TM, D)),
                               (rstd1, _rows(TM, 1))],
                        full=[modrows, ln_g, ln_b, w_up_t],
                        outs=[((s_rows, D), MXU_DTYPE, _rows(TM, D)), ((s_rows, D), F32, _rows(TM, D))],
                        accs=[(1, D)] * 5)
    return res, accs


def merge_bwd(dy, ya, yb, gl, w_out_t, wba_t, wbb_t, s_rows):
    def body(t, vals, fr):
        dy_, ya_, yb_, gl_ = (v[...] for v in vals)
        wot, wat, wbt = fr
        dmrg = jnp.dot(dy_, wot[...], preferred_element_type=F32)
        ga = jax.nn.sigmoid(gl_[:, :D])
        gb = jax.nn.sigmoid(gl_[:, D:])
        dya = dmrg * ga
        dyb = dmrg * gb
        dgl = jnp.concatenate([dmrg * ya_ * ga * (1.0 - ga), dmrg * yb_ * gb * (1.0 - gb)], axis=1)
        dob = _dot(dyb, wbt[...])
        return [dya, dyb, dgl, _dot(dya, wat[...]), dob, _stacked_heads(dob)], []

    mx = MXU_DTYPE
    res, _ = rowwise("merge_bwd", body, ntiles=s_rows // TM,
                     tiled=[(dy, _rows(TM, D)), (ya, _rows(TM, D)), (yb, _rows(TM, D)), (gl, _rows(TM, 2 * D))],
                     full=[w_out_t, wba_t, wbb_t],
                     outs=[((s_rows, D), mx, _rows(TM, D))] * 2 + [((s_rows, 2 * D), F32, _rows(TM, 2 * D))] +
                          [((s_rows, Q_W), F32, _rows(TM, Q_W))] * 2 + [_stacked_out(s_rows)])
    return res


def qk_bwd(dqa, dka, dva, dqb, dkb, dvb, dgl, tq, rq, tk, rk, cos, sin, qg, kg, bd, n_lat_tiles, n):
    def body(t, vals, fr):
        dqa_, dka_, dva_, dqb_, dkb_, dvb_, dgl_, tq_, rq_, tk_, rk_, cs, sn = (v[...] for v in vals)
        qg_r, kg_r, bd_r = fr
        is_ctx = t >= n_lat_tiles
        cos4 = jnp.concatenate([cs] * 4, axis=1)
        sin4 = jnp.concatenate([sn] * 4, axis=1)
        zero = jnp.zeros_like(dqa_)
        dpqa = jnp.where(is_ctx, zero, _rope_t(dqa_, cos4, sin4) * Q_SCALE)
        dpka = _rope_t(dka_, cos4, sin4)
        dpva = dva_
        dnq = jnp.where(is_ctx, zero, _rope_t(dqb_, cos4, sin4) * Q_SCALE)
        gq = qg_r[...] * dnq
        dtq = rq_ * gq - tq_ * (rq_ * rq_ * rq_) * (_seg_sum64(gq * tq_, bd_r) * (1.0 / HEAD_DIM))
        dnk = _rope_t(dkb_, cos4, sin4)
        gk = kg_r[...] * dnk
        dtk = rk_ * gk - tk_ * (rk_ * rk_ * rk_) * (_seg_sum64(gk * tk_, bd_r) * (1.0 / HEAD_DIM))
        dgl32 = jnp.where(is_ctx, jnp.zeros_like(dgl_), dgl_)
        dproj = jnp.concatenate([dpqa, dpka, dpva, dtq, dtk, dvb_, dgl32], axis=1)
        return [dproj], [_colsum(dproj), _colsum(dnq * tq_ * rq_), _colsum(dnk * tk_ * rk_)]

    lat = lambda t: jnp.minimum(t, n_lat_tiles - 1)
    qs = _rows(TM, Q_W)
    res, accs = rowwise(
        "qk_bwd", body, ntiles=n // TM,
        tiled=[(dqa, _rows(TM, Q_W, lat)), (dka, qs), (dva, qs), (dqb, _rows(TM, Q_W, lat)),
               (dkb, qs), (dvb, qs), (dgl, _rows(TM, 2 * D, lat)), (tq, qs), (rq, qs), (tk, qs), (rk, qs),
               (cos, _rows(TM, LANES)), (sin, _rows(TM, LANES))],
        full=[qg, kg, bd], outs=[((n, EXT_COLS), MXU_DTYPE, _rows(TM, EXT_COLS))],
        accs=[(1, EXT_COLS), (1, Q_W), (1, Q_W)])
    return res[0], accs


def inproj_bwd(name, dproj, xa, dxp, modrows, w_ext_t, *, ntiles, tile_off, is_ctx, out_rows):
    kc = M_SCALEC if is_ctx else M_SCALE1

    def body(t, vals, fr):
        dp, x_ = vals[0][...], vals[1][...]
        mod, wt = fr
        dh = jnp.dot(dp, wt[...], preferred_element_type=F32)
        incs = [_colsum(dh * x_), _colsum(dh)]
        if is_ctx:
            return [], incs
        return [vals[2][...] + dh * (1.0 + _mrow(mod, kc))], incs

    tiled = [(dproj, _rows(TM, EXT_COLS, lambda i: i + tile_off)), (xa, _rows(TM, D, lambda i: i + tile_off))]
    outs = []
    if not is_ctx:
        tiled.append((dxp, _rows(TM, D)))
        outs = [((out_rows, D), F32, _rows(TM, D))]
    return rowwise(name, body, ntiles=ntiles, tiled=tiled, full=[modrows, w_ext_t], outs=outs, accs=[(1, D)] * 2)


def _attn_semantics():
    return _cparams(("arbitrary", "arbitrary", "arbitrary"))


GLOB_TK = (1280, 1024, 768, 512, 256)
GLOB_TQ = (512, 256)
GLOB_BWD_TQ = GLOB_TQ
KEY_CHUNK = 256


def glob_fwd(q, kt, v_t, s_rows):
    n = kt.shape[0]
    tq = _pick(s_rows, GLOB_TQ)
    tk = _pick(n, GLOB_TK)
    nq, nk = s_rows // tq, n // tk
    r = GROUPS * tq
    nch = tk // KEY_CHUNK

    def produce(qs, k_ref, s_buf, c, mx):
        rows = slice(c * KEY_CHUNK, (c + 1) * KEY_CHUNK)
        sn = _dot_nt(k_ref[rows, :], qs[...])
        s_buf[rows, :] = sn
        return jnp.maximum(mx, jnp.max(sn, axis=0, keepdims=True))

    def kern(q_ref, k0_ref, kn_ref, vt_ref, o_ref, lse_ref, qs, s_buf, mx_buf, m_s, l_s, acc):
        j = pl.program_id(2)

        @pl.when(j == 0)
        def _():
            qs[...] = _stack_tiles(q_ref[...], qs.dtype)
            mx = jnp.full((1, r), -jnp.inf, F32)
            for c in range(nch):
                mx = produce(qs, k0_ref, s_buf, c, mx)
            mx_buf[...] = mx
            m_s[...] = jnp.full_like(m_s, -jnp.inf)
            l_s[...] = jnp.zeros_like(l_s)
            acc[...] = jnp.zeros_like(acc)

        m_prev = m_s[...]
        m_new = jnp.maximum(m_prev, mx_buf[...])
        alpha = jnp.exp(m_prev - m_new)
        a = alpha * acc[...]
        ls = alpha * l_s[...]
        mx = jnp.full((1, r), -jnp.inf, F32)
        for c in range(nch):
            rows = slice(c * KEY_CHUNK, (c + 1) * KEY_CHUNK)
            p = jnp.exp(s_buf[rows, :] - m_new)
            ls = ls + jnp.sum(p, axis=0, keepdims=True)
            a = a + jnp.dot(vt_ref[0, :, rows], p.astype(MXU_DTYPE), preferred_element_type=F32)
            mx = produce(qs, kn_ref, s_buf, c, mx)
        mx_buf[...] = mx
        l_s[...] = ls
        acc[...] = a
        m_s[...] = m_new

        @pl.when(j == nk - 1)
        def _():
            o_t = acc[...] / l_s[...]
            o_ref[...] = jnp.concatenate([_untranspose_groups(o_t[:, GROUPS * a:GROUPS * (a + TM)], TM)
                                          for a in range(0, tq, TM)], axis=0)
            lse_ref[0, 0] = _row_to_col(m_s[...] + jnp.log(l_s[...]), LANES)

    kspec = lambda f: pl.BlockSpec((tk, KV_W), lambda h, i, j: (f(j), h))
    return pl.pallas_call(
        kern, name="glob_fwd", grid=(N_KV, nq, nk),
        in_specs=[pl.BlockSpec((tq, KV_W), lambda h, i, j: (i, h)), kspec(lambda j: 0),
                  kspec(lambda j: jnp.minimum(j + 1, nk - 1)), pl.BlockSpec((1, HEAD_DIM, tk), lambda h, i, j: (h, 0, j))],
        out_specs=[pl.BlockSpec((tq, KV_W), lambda h, i, j: (i, h)),
                   pl.BlockSpec((1, 1, r, LANES), lambda h, i, j: (h, i, 0, 0))],
        out_shape=[SDS((s_rows, Q_W), F32), SDS((N_KV, nq, r, LANES), F32)],
        scratch_shapes=[pltpu.VMEM((r, KV_W), MXU_DTYPE), pltpu.VMEM((tk, r), F32), pltpu.VMEM((1, r), F32),
                        pltpu.VMEM((1, r), F32), pltpu.VMEM((1, r), F32), pltpu.VMEM((HEAD_DIM, r), F32)],
        compiler_params=_attn_semantics(),
    )(q, kt, kt, v_t)


def attn_delta(o, do, s_rows):
    tq = _pick(s_rows, GLOB_BWD_TQ)
    nq = s_rows // tq
    r = GROUPS * tq

    def kern(o_ref, do_ref, d_ref):
        d = jnp.sum(_stack_tiles(do_ref[...], F32) * _stack_tiles(o_ref[...], F32), axis=1, keepdims=True)
        d_ref[0, 0] = jnp.broadcast_to(d, (r, LANES))

    qspec = pl.BlockSpec((tq, KV_W), lambda h, i: (i, h))
    return pl.pallas_call(
        kern, name="attn_delta", grid=(N_KV, nq), in_specs=[qspec, qspec],
        out_specs=pl.BlockSpec((1, 1, r, LANES), lambda h, i: (h, i, 0, 0)), out_shape=SDS((N_KV, nq, r, LANES), F32),
        compiler_params=_cparams(("parallel", "parallel")),
    )(o, do)


def _compact_t(tt, dtype):
    return jnp.concatenate([tt[g * HEAD_DIM:(g + 1) * HEAD_DIM, :] for g in range(GROUPS)], axis=1).astype(dtype)


def glob_bwd(qs_all, q_t, kt, vt, dos_all, do_t, lse, delta, h, s_rows):
    n = kt.shape[0]
    tq = _pick(s_rows, GLOB_BWD_TQ)
    tk = _pick(n, GLOB_TK)
    nq, nk = s_rows // tq, n // tk
    r = GROUPS * tq
    nch = tk // KEY_CHUNK

    def kern(qs_ref, qt_ref, k_ref, v_ref, dos_ref, dot_ref, lse_ref, dl_ref, dq_ref, dkt_ref, dvt_ref, p_buf, ds_buf):
        j = pl.program_id(0)
        i = pl.program_id(1)

        @pl.when(i == 0)
        def _():
            dkt_ref[...] = jnp.zeros_like(dkt_ref)
            dvt_ref[...] = jnp.zeros_like(dvt_ref)

        qs = qs_ref[0]
        dos = dos_ref[0]
        q_c = _compact_tiles_t(qt_ref[...], MXU_DTYPE)
        do_c = _compact_tiles_t(dot_ref[...], MXU_DTYPE)
        lse_b = lse_ref[0]
        dl_b = dl_ref[0]
        for c in range(nch):
            lo = c * KEY_CHUNK
            cols = slice(lo, lo + KEY_CHUNK)
            sc = _dot_nt(qs, k_ref[cols, :])
            dpc = _dot_nt(dos, v_ref[cols, :])
            for t in range(KEY_CHUNK // LANES):
                sl = slice(t * LANES, (t + 1) * LANES)
                pt = jnp.exp(sc[:, sl] - lse_b)
                p_buf[:, sl] = pt.astype(p_buf.dtype)
                ds_buf[:, lo + t * LANES:lo + (t + 1) * LANES] = (pt * (dpc[:, sl] - dl_b)).astype(ds_buf.dtype)
            dvt_ref[:, cols] += jnp.dot(do_c, p_buf[...], preferred_element_type=F32)
            dkt_ref[:, cols] += jnp.dot(q_c, ds_buf[:, cols], preferred_element_type=F32)
        dq_t = _fold_tiles(jnp.dot(ds_buf[...], k_ref[...], preferred_element_type=F32), tq)
        rows = pl.ds(pl.multiple_of(i * tq, tq), tq)

        @pl.when(j == 0)
        def _():
            dq_ref[rows, :] = dq_t

        @pl.when(j > 0)
        def _():
            dq_ref[rows, :] += dq_t

    col = pl.BlockSpec((1, r, LANES), lambda j, i: (i, 0, 0))
    sspec = pl.BlockSpec((1, r, KV_W), lambda j, i: (h, i, 0))
    tspec = pl.BlockSpec((KV_W, tq), lambda j, i: (h, i))
    kspec = pl.BlockSpec((tk, KV_W), lambda j, i: (j, h))
    ospec = pl.BlockSpec((HEAD_DIM, tk), lambda j, i: (0, j))
    return pl.pallas_call(
        kern, name=f"glob_bwd_h{h}", grid=(nk, nq),
        in_specs=[sspec, tspec, kspec, kspec, sspec, tspec, col, col],
        out_specs=[pl.BlockSpec(memory_space=pltpu.VMEM), ospec, ospec],
        out_shape=[SDS((s_rows, KV_W), F32), SDS((HEAD_DIM, n), F32), SDS((HEAD_DIM, n), F32)],
        scratch_shapes=[pltpu.VMEM((r, KEY_CHUNK), MXU_DTYPE), pltpu.VMEM((r, tk), MXU_DTYPE)],
        compiler_params=_cparams(("arbitrary", "arbitrary")),
    )(qs_all, q_t, kt, vt, dos_all, do_t, lse, delta)


TW = 2 * WIN
WR = GROUPS * TW
WLAT = 4 * WIN


def _win_cat(dst, parts):
    off = 0
    for p in parts:
        dst[off:off + p.shape[0], :] = p[...]
        off += p.shape[0]


def _win_specs(s_rows, c_rows):
    nb = s_rows // WIN
    prev = lambda i: jnp.maximum(2 * i - 1, 0)
    nxt = lambda i: jnp.minimum(2 * i + 2, nb - 1)
    rows = [pl.BlockSpec((WIN, KV_W), lambda h, i: (prev(i), h)), pl.BlockSpec((TW, KV_W), lambda h, i: (i, h)),
            pl.BlockSpec((WIN, KV_W), lambda h, i: (nxt(i), h)), pl.BlockSpec((c_rows, KV_W), lambda h, i: (s_rows // c_rows, h))]
    cols = [pl.BlockSpec((1, HEAD_DIM, WIN), lambda h, i: (h, 0, prev(i))), pl.BlockSpec((1, HEAD_DIM, TW), lambda h, i: (h, 0, i)),
            pl.BlockSpec((1, HEAD_DIM, WIN), lambda h, i: (h, 0, nxt(i))),
            pl.BlockSpec((1, HEAD_DIM, c_rows), lambda h, i: (h, 0, s_rows // c_rows))]
    return rows, cols


def _win_mask(i, s_rows, shape, keys_on_rows):
    a = lax.broadcasted_iota(jnp.int32, shape, 0)
    b = lax.broadcasted_iota(jnp.int32, shape, 1)
    kk, qq = (a, b) if keys_on_rows else (b, a)
    qpos = i * TW + (qq & (TW - 1))
    kpos = (2 * i - 1) * WIN + kk
    band = (jnp.abs(qpos - kpos) <= WIN) & (kpos >= 0) & (kpos < s_rows)
    return (kk >= WLAT) | band


def _untranspose_groups(o_t, tq):
    row = lax.broadcasted_iota(jnp.int32, (HEAD_DIM, KV_W), 0)
    col = lax.broadcasted_iota(jnp.int32, (HEAD_DIM, KV_W), 1)
    hi = o_t.astype(BF16)
    r1 = o_t - hi.astype(F32)
    mid = r1.astype(BF16)
    lo = (r1 - mid.astype(F32)).astype(BF16)
    o = jnp.zeros((tq, KV_W), F32)
    for g in range(GROUPS):
        sel = jnp.where(col == row + g * HEAD_DIM, 1.0, 0.0).astype(BF16)
        for term in (hi, mid, lo):
            o = o + lax.dot_general(term[:, g * tq:(g + 1) * tq], sel, (((0,), (0,)), ((), ())), preferred_element_type=F32)
    return o


def win_fwd(q, kt, v_t, sinkrow, s_rows, c_rows):
    nt = s_rows // TW
    nkeys = WLAT + c_rows

    def kern(q_ref, kp, kc, kn, kx, vp, vc, vn, vx, sink_ref, o_ref, lse_ref, kcat):
        i = pl.program_id(1)
        _win_cat(kcat, (kp, kc, kn, kx))
        qs = _stack_groups(q_ref[...], MXU_DTYPE)
        st = _dot_nt(kcat[...], qs)
        st = jnp.where(_win_mask(i, s_rows, st.shape, True), st, -jnp.inf)
        sink = sink_ref[0]
        m = jnp.maximum(jnp.max(st, axis=0, keepdims=True), sink)
        e = jnp.exp(st - m)
        den = jnp.sum(e, axis=0, keepdims=True) + jnp.exp(sink - m)
        v_cat = jnp.concatenate([vp[0], vc[0], vn[0], vx[0]], axis=1)
        o_t = jnp.dot(v_cat, e.astype(MXU_DTYPE), preferred_element_type=F32) / den
        o_ref[...] = _untranspose_groups(o_t, TW)
        lse_ref[0, 0] = _row_to_col(m + jnp.log(den))

    rows, cols = _win_specs(s_rows, c_rows)
    qspec = pl.BlockSpec((TW, KV_W), lambda h, i: (i, h))
    rowv = pl.BlockSpec((1, 1, WR, 1), lambda h, i: (h, i, 0, 0))
    return pl.pallas_call(
        kern, name="win_fwd", grid=(N_KV, nt),
        in_specs=[qspec] + rows + cols + [pl.BlockSpec((1, 1, WR), lambda h, i: (h, 0, 0))],
        out_specs=[qspec, rowv], out_shape=[SDS((s_rows, Q_W), F32), SDS((N_KV, nt, WR, 1), F32)],
        scratch_shapes=[pltpu.VMEM((nkeys, KV_W), MXU_DTYPE)],
        compiler_params=_cparams(("parallel", "parallel")),
    )(q, kt, kt, kt, kt, v_t, v_t, v_t, v_t, sinkrow)


def win_bwd(q, q_t, kt, vt, sinkcol, o, do, do_t, lse, s_rows, c_rows):
    nt = s_rows // TW
    nkeys = WLAT + c_rows
    n = s_rows + c_rows
    ctx0 = WIN + s_rows

    def kern(q_ref, qt_ref, kp, kc, kn, kx, vp, vc, vn, vx, sink_ref, o_ref, do_ref, dot_ref, lse_ref,
             dq_ref, dkt_ref, dvt_ref, dsk_ref, kcat, vcat):
        i = pl.program_id(1)

        @pl.when(i == 0)
        def _():
            dkt_ref[...] = jnp.zeros_like(dkt_ref)
            dvt_ref[...] = jnp.zeros_like(dvt_ref)
            dsk_ref[...] = jnp.zeros_like(dsk_ref)

        _win_cat(kcat, (kp, kc, kn, kx))
        _win_cat(vcat, (vp, vc, vn, vx))
        qs = _stack_groups(q_ref[...], MXU_DTYPE)
        do32 = _stack_groups(do_ref[...], F32)
        delta = jnp.sum(do32 * _stack_groups(o_ref[...], F32), axis=1, keepdims=True)
        dos = do32.astype(MXU_DTYPE)
        lse_c = lse_ref[0, 0]
        s = _dot_nt(qs, kcat[...])
        s = jnp.where(_win_mask(i, s_rows, s.shape, False), s, -jnp.inf)
        p = jnp.exp(s - lse_c)
        ds = p * (_dot_nt(dos, vcat[...]) - delta)
        dq_ref[...] = _fold_groups(_dot(ds, kcat[...]), TW)
        dvt = jnp.dot(_compact_t(dot_ref[...], MXU_DTYPE), p.astype(MXU_DTYPE), preferred_element_type=F32)
        dkt = jnp.dot(_compact_t(qt_ref[...], MXU_DTYPE), ds.astype(MXU_DTYPE), preferred_element_type=F32)
        lat = pl.ds(pl.multiple_of(i * TW, TW), WLAT)
        dkt_ref[0, :, lat] += dkt[:, :WLAT]
        dvt_ref[0, :, lat] += dvt[:, :WLAT]
        dkt_ref[0, :, ctx0:ctx0 + c_rows] += dkt[:, WLAT:]
        dvt_ref[0, :, ctx0:ctx0 + c_rows] += dvt[:, WLAT:]
        dsk_ref[0] += -(jnp.exp(sink_ref[0][:, 0:1] - lse_c) * delta)

    rows, _ = _win_specs(s_rows, c_rows)
    qspec = pl.BlockSpec((TW, KV_W), lambda h, i: (i, h))
    tspec = pl.BlockSpec((KV_W, TW), lambda h, i: (h, i))
    col = pl.BlockSpec((1, 1, WR, 1), lambda h, i: (h, i, 0, 0))
    kvt = pl.BlockSpec((1, HEAD_DIM, WIN + n), lambda h, i: (h, 0, 0))
    return pl.pallas_call(
        kern, name="win_bwd", grid=(N_KV, nt),
        in_specs=[qspec, tspec] + rows + rows + [pl.BlockSpec((1, WR, LANES), lambda h, i: (h, 0, 0)), qspec, qspec, tspec, col],
        out_specs=[qspec, kvt, kvt, pl.BlockSpec((1, WR, 1), lambda h, i: (h, 0, 0))],
        out_shape=[SDS((s_rows, Q_W), F32), SDS((N_KV, HEAD_DIM, WIN + n), F32), SDS((N_KV, HEAD_DIM, WIN + n), F32),
                   SDS((N_KV, WR, 1), F32)],
        scratch_shapes=[pltpu.VMEM((nkeys, KV_W), MXU_DTYPE), pltpu.VMEM((nkeys, KV_W), MXU_DTYPE)],
        compiler_params=_cparams(("arbitrary", "arbitrary")),
    )(q, q_t, kt, kt, kt, kt, vt, vt, vt, vt, sinkcol, o, do, do_t, lse)


def adamw(name, w, m, v, grads):
    r, wd = w.shape
    tr = _pick(r, [t for t in ELEMENTWISE_ROWS if t * wd * 4 <= ELEMENTWISE_BLOCK_BYTES])
    stacked = not isinstance(grads, (list, tuple))
    ng = grads.shape[0] if stacked else len(grads)

    def kern(*refs):
        w_ref, m_ref, v_ref = refs[:3]
        g_refs = refs[3:-4]
        g_out, d_out, m_out, v_out = refs[-4:]
        if stacked:
            g = g_refs[0][0]
            for k in range(1, ng):
                g = g + g_refs[0][k]
        else:
            g = g_refs[0][...]
            for gr in g_refs[1:]:
                g = g + gr[...]
        wv = w_ref[...]
        mn = ADAM_B1 * m_ref[...] + (1.0 - ADAM_B1) * g
        vn = ADAM_B2 * v_ref[...] + (1.0 - ADAM_B2) * (g * g)
        m_hat = mn / (1.0 - ADAM_B1 ** ADAM_STEP)
        v_hat = vn / (1.0 - ADAM_B2 ** ADAM_STEP)
        g_out[...] = g
        d_out[...] = -ADAM_LR * (m_hat / (jnp.sqrt(v_hat) + ADAM_EPS) + ADAM_WD * wv)
        m_out[...] = mn
        v_out[...] = vn

    spec = pl.BlockSpec((tr, wd), lambda i: (i, 0))
    gspecs = [pl.BlockSpec((ng, tr, wd), lambda i: (0, i, 0))] if stacked else [spec] * ng
    return pl.pallas_call(
        kern, name=name, grid=(r // tr,), in_specs=[spec] * 3 + gspecs, out_specs=[spec] * 4,
        out_shape=[SDS((r, wd), F32)] * 4, compiler_params=_cparams(("parallel",)),
    )(w, m, v, *([grads] if stacked else grads))


def add2(name, a, b):
    k, r, w = a.shape
    tr = _pick(r, [t for t in ELEMENTWISE_ROWS if t * w * 4 <= ELEMENTWISE_BLOCK_BYTES])

    def kern(a_ref, b_ref, o_ref):
        o_ref[...] = a_ref[...] + b_ref[...]

    spec = pl.BlockSpec((1, tr, w), lambda s, i: (s, i, 0))
    return pl.pallas_call(kern, name=name, grid=(k, r // tr), in_specs=[spec, spec], out_specs=spec,
                          out_shape=SDS(a.shape, a.dtype), compiler_params=_cparams(("parallel", "parallel")))(a, b)


def _rep4(a, off):
    return jnp.concatenate([a[:, off + HEAD_DIM * h: off + HEAD_DIM * (h + 1)] for h in range(N_KV) for _ in range(GROUPS)], axis=1)


def _extend_cols(a):
    return jnp.concatenate([a[:, 0:OFF_KA], _rep4(a, OFF_KA), _rep4(a, OFF_VA), a[:, OFF_QB:OFF_KB],
                            _rep4(a, OFF_KB), _rep4(a, OFF_VB), a[:, OFF_GA:]], axis=1)


def _fold4(a, off):
    r = a.shape[0]
    return a[:, off:off + Q_W].reshape(r, N_KV, GROUPS, HEAD_DIM).sum(axis=2).reshape(r, N_KV * HEAD_DIM)


def _fold_cols(a):
    return jnp.concatenate([a[:, X_QA:X_QA + Q_W], _fold4(a, X_KA), _fold4(a, X_VA), a[:, X_QB:X_QB + Q_W],
                            _fold4(a, X_KB), _fold4(a, X_VB), a[:, X_GL:]], axis=1)


def _rope_tables(s_rows, c_rows):
    pos = jnp.arange(s_rows, dtype=jnp.int32)
    rows = (pos // GRID_W).astype(F32)
    cols = (pos % GRID_W).astype(F32)
    n_freq = HEAD_DIM // 4
    inv_freq = ROPE_THETA ** (-jnp.arange(n_freq, dtype=F32) / n_freq)
    ang_r = rows[:, None] * inv_freq
    ang_c = cols[:, None] * inv_freq
    cos = jnp.concatenate([jnp.cos(ang_r)] * 2 + [jnp.cos(ang_c)] * 2, axis=1)
    sin = jnp.concatenate([-jnp.sin(ang_r), jnp.sin(ang_r), -jnp.sin(ang_c), jnp.sin(ang_c)], axis=1)
    cos = jnp.concatenate([cos, jnp.ones((c_rows, HEAD_DIM), F32)], axis=0)
    sin = jnp.concatenate([sin, jnp.zeros((c_rows, HEAD_DIM), F32)], axis=0)
    return jnp.concatenate([cos, cos], axis=1), jnp.concatenate([sin, sin], axis=1)


def _ff_pad_cols(a):
    r = a.shape[0]
    a = jnp.pad(a.reshape(r, N_DEV, FF_SHARD), ((0, 0), (0, 0), (0, FF_SHARD_PAD - FF_SHARD)))
    return a.reshape(r, 2 * FF)


def _ff_unpad_cols(a):
    r = a.shape[0]
    return a.reshape(r, N_DEV, FF_SHARD_PAD)[:, :, :FF_SHARD].reshape(r, 2 * D_FF)


def _ff_pad_rows(a):
    c = a.shape[1]
    a = jnp.pad(a.reshape(N_DEV // 2, FF_SHARD, c), ((0, 0), (0, FF_SHARD_PAD - FF_SHARD), (0, 0)))
    return a.reshape(FF, c)


def _ff_unpad_rows(a):
    c = a.shape[1]
    return a.reshape(N_DEV // 2, FF_SHARD_PAD, c)[:, :FF_SHARD].reshape(D_FF, c)


BIG = (("w_in", (D, IN_COLS // N_DEV)), ("w_branch_a", (Q_W, D // N_DEV)), ("w_branch_b", (Q_W, D // N_DEV)),
       ("w_out", (D // N_DEV, D)), ("w_up", (D, FF_SHARD_PAD)), ("w_down", (D_FF // N_DEV, D)))
BIG_SIZES = tuple(int(np.prod(s)) for _, s in BIG)
BIG_ROWS = sum(BIG_SIZES) // LANES


def _pack_big(parts):
    return jnp.concatenate([p.reshape(-1) for p in parts]).reshape(BIG_ROWS, LANES)


def _unpack_big(flat):
    lead = flat.shape[:-2]
    f = flat.reshape(*lead, BIG_ROWS * LANES)
    out, off = [], 0
    for (_, shp), sz in zip(BIG, BIG_SIZES, strict=True):
        out.append(f[..., off:off + sz].reshape(*lead, *shp))
        off += sz
    return out


def _cols_to_full(g):
    return jnp.transpose(g, (1, 0, 2)).reshape(g.shape[1], -1)


def _full_to_cols(a):
    r, c = a.shape
    return jnp.transpose(a.reshape(r, N_DEV, c // N_DEV), (1, 0, 2))


SMALL = (("c_ctx", D), ("b_mod", N_MOD * D), ("b_in", IN_COLS), ("attn_sink", N_HEADS), ("q_norm_g", HEAD_DIM),
         ("k_norm_g", HEAD_DIM), ("ln1_g", D), ("ln1_b", D), ("conv_w", 3 * 2 * D_FF // N_DEV), ("conv_b", 2 * D_FF),
         ("ln2_g", D), ("ln2_b", D))
SMALL_TOTAL = sum(n for _, n in SMALL)
SMALL_ROWS = -(-SMALL_TOTAL // (8 * LANES)) * 8


def _pack_small(parts):
    flat = jnp.concatenate([p.reshape(-1).astype(F32) for p in parts])
    return jnp.pad(flat, (0, SMALL_ROWS * LANES - flat.shape[0])).reshape(SMALL_ROWS, LANES)


def _unpack_small(packed):
    f = packed.reshape(-1)
    out, off = {}, 0
    for name, n in SMALL:
        out[name] = f[off:off + n]
        off += n
    return out


RED = (("c_ctx", D), ("b_in", IN_COLS), ("attn_sink", N_HEADS), ("q_norm_g", HEAD_DIM), ("k_norm_g", HEAD_DIM),
       ("ln1_g", D), ("ln1_b", D), ("conv_w", 3 * 2 * FF), ("conv_b", 2 * FF), ("ln2_g", D), ("ln2_b", D))
RED_TOTAL = sum(n for _, n in RED)
RED_ROWS = -(-RED_TOTAL // (8 * LANES)) * 8


def sum8(name, g):
    _, r, w = g.shape

    def kern(g_ref, o_ref):
        acc = g_ref[0]
        for k in range(1, N_DEV):
            acc = acc + g_ref[k]
        o_ref[...] = acc

    return pl.pallas_call(kern, name=name, out_shape=SDS((r, w), F32))(g)


def _local_step(x, ctx, target, modrows, weights, small):
    s_rows, c_rows = x.shape[0], ctx.shape[0]
    n = s_rows + c_rows
    nl = s_rows // TM
    w_in, wba, wbb, w_out, w_up, w_down = weights
    f = lambda a: a.reshape(1, -1).astype(F32)
    b_in, ln1_g, ln1_b, ln2_g, ln2_b, conv_b = (f(small[k]) for k in ("b_in", "ln1_g", "ln1_b", "ln2_g", "ln2_b", "conv_b"))
    conv_w8 = jnp.pad(small["conv_w_full"], ((0, 5), (0, 0)))
    qg = jnp.tile(small["q_norm_g"].reshape(1, HEAD_DIM), (1, N_HEADS))
    kg = jnp.tile(small["k_norm_g"].reshape(1, HEAD_DIM), (1, N_HEADS))
    sink_rep = jnp.repeat(small["attn_sink"].reshape(N_KV, GROUPS), TW, axis=1)
    sinkrow = sink_rep.reshape(N_KV, 1, WR)
    sinkcol = jnp.broadcast_to(sink_rep[:, :, None], (N_KV, WR, LANES))
    bd = jnp.kron(jnp.eye(N_HEADS, dtype=F32), jnp.ones((HEAD_DIM, HEAD_DIM), F32)).astype(BF16)
    cos, sin = _rope_tables(s_rows, c_rows)
    w_ext = _extend_cols(w_in)
    b_ext = _extend_cols(b_in)
    xa = jnp.concatenate([x, ctx], axis=0)

    hb, qa, kat, vat, qb, kbt, vbt, tq, rq, tk, rk, gl, qbs = inproj_fwd(xa, cos, sin, modrows, w_ext, b_ext, qg, kg, bd, nl)
    compact_t = lambda t: jnp.stack([t[:, h * KV_W:h * KV_W + HEAD_DIM].T for h in range(N_KV)])
    oa, lse_a = win_fwd(qa, kat, compact_t(vat), sinkrow, s_rows, c_rows)
    vb_t = compact_t(vbt)
    ob, lse_b = glob_fwd(qb, kbt, vb_t, s_rows)
    tqb = _pick(s_rows, GLOB_BWD_TQ)
    lse_b = lse_b.reshape(N_KV, s_rows // tqb, GROUPS * tqb, LANES)
    ya, yb, mrg, y, xhat1, rstd1 = merge_fwd(oa, ob, gl, x, modrows, wba, wbb, w_out, s_rows)
    h2, u0 = ffn_up_fwd(xhat1, modrows, ln1_g, ln1_b, w_up, s_rows)
    a = conv_swiglu_fwd(u0, conv_w8, conv_b, s_rows)
    (dr2, dy2), (loss, dln2_g, dln2_b, dgate2) = ffn_down_loss(a, xhat1, target, modrows, ln1_g, ln1_b, ln2_g, ln2_b, w_down, s_rows)

    da = ffn_down_bwd(dy2, w_down.T, s_rows)
    dw_down = mm_tn("dw_down", a, dy2, s_rows)
    du0, (dconv_b, dcw0, dcw1, dcw2) = swiglu_conv_bwd(u0, da, conv_w8, conv_b, s_rows)
    dw_up = mm_tn("dw_up", h2, du0, s_rows)
    (dy, dxp), (dscale2, dshift2, dln1_g, dln1_b, dgate1) = ffn_up_ln1_bwd(du0, dr2, xhat1, y, rstd1, modrows, ln1_g, ln1_b, w_up.T, s_rows)
    dya, dyb, dgl, doa, dob, dobs = merge_bwd(dy, ya, yb, gl, w_out.T, wba.T, wbb.T, s_rows)
    dw_out = mm_tn("dw_out", mrg, dy, s_rows)
    dwba = mm_tn("dw_branch_a", oa, dya, s_rows)
    dwbb = mm_tn("dw_branch_b", ob, dyb, s_rows)

    pad = jnp.zeros((n, KV_W - HEAD_DIM), F32)
    spread = lambda per_head: jnp.concatenate([t for th in per_head for t in (th.T, pad)], axis=1)
    dqa, dka_t, dva_t, dsk = win_bwd(qa, qa[:s_rows].T, kat, vat, sinkcol, oa, doa, doa.astype(MXU_DTYPE).T, lse_a,
                                     s_rows, c_rows)
    dka = spread([dka_t[h, :, WIN:] for h in range(N_KV)])
    dva = spread([dva_t[h, :, WIN:] for h in range(N_KV)])
    delta_b = attn_delta(ob, dob, s_rows)
    qb_t = qb[:s_rows].T
    dob_t = dob.astype(MXU_DTYPE).T
    heads = [glob_bwd(qbs, qb_t, kbt, vbt, dobs, dob_t, lse_b[h], delta_b[h], h, s_rows) for h in range(N_KV)]
    dqb = jnp.concatenate([hd[0] for hd in heads], axis=1)
    dkb = spread([hd[1] for hd in heads])
    dvb = spread([hd[2] for hd in heads])
    dproj, (db_ext, dqg, dkg) = qk_bwd(dqa, dka, dva, dqb, dkb, dvb, dgl, tq, rq, tk, rk, cos, sin, qg, kg, bd, nl, n)
    w_ext_t = w_ext.T
    (grad_x,), (dscale1, dshift1) = inproj_bwd("inproj_bwd", dproj, xa, dxp, modrows, w_ext_t, ntiles=nl, tile_off=0,
                                               is_ctx=False, out_rows=s_rows)
    _, (dscale_c, dshift_c) = inproj_bwd("inproj_bwd_ctx", dproj, xa, None, modrows, w_ext_t, ntiles=c_rows // TM,
                                         tile_off=nl, is_ctx=True, out_rows=0)
    dw_in = _fold_cols(mm_tn("dw_in", hb, dproj, n))

    dmod = jnp.concatenate([dshift1, dscale1, dgate1, dshift2, dscale2, dgate2], axis=1)
    dmod_c = jnp.concatenate([dshift_c, dscale_c, jnp.zeros((1, (N_MOD - 2) * D), F32)], axis=1)
    fold_g = lambda t: t.reshape(N_HEADS, HEAD_DIM).sum(axis=0)
    red = {
        "b_in": _fold_cols(db_ext), "attn_sink": dsk.reshape(N_HEADS, TW).sum(axis=1), "q_norm_g": fold_g(dqg),
        "k_norm_g": fold_g(dkg), "ln1_g": dln1_g, "ln1_b": dln1_b, "conv_w": jnp.concatenate([dcw0, dcw1, dcw2], axis=0),
        "conv_b": dconv_b, "ln2_g": dln2_g, "ln2_b": dln2_b,
    }
    return loss[0, 0], grad_x, (dw_in, dwba, dwbb, dw_out, dw_up, dw_down), dmod, dmod_c, red


def kernel(x, c, ctx, c_ctx, w_mod, b_mod, w_in, b_in, attn_sink, q_norm_g, k_norm_g, w_branch_a, w_branch_b, w_out, ln1_g, ln1_b, w_up, conv_w, conv_b, w_down, ln2_g, ln2_b, loss_target, m_c_ctx, m_w_mod, m_b_mod, m_w_in, m_b_in, m_attn_sink, m_q_norm_g, m_k_norm_g, m_w_branch_a, m_w_branch_b, m_w_out, m_ln1_g, m_ln1_b, m_w_up, m_conv_w, m_conv_b, m_w_down, m_ln2_g, m_ln2_b, v_c_ctx, v_w_mod, v_b_mod, v_w_in, v_b_in, v_attn_sink, v_q_norm_g, v_k_norm_g, v_w_branch_a, v_w_branch_b, v_w_out, v_ln1_g, v_ln1_b, v_w_up, v_conv_w, v_conv_b, v_w_down, v_ln2_g, v_ln2_b):
    ax, ay, ac = (lax.axis_index(a) for a in AXES)
    me = 4 * ax + 2 * ay + ac
    chip = 2 * ax + ay
    mod_w = N_MOD * D // N_DEV
    params = dict(c_ctx=c_ctx, w_mod=w_mod, b_mod=b_mod, w_in=w_in, b_in=b_in, attn_sink=attn_sink, q_norm_g=q_norm_g,
                  k_norm_g=k_norm_g, w_branch_a=w_branch_a, w_branch_b=w_branch_b, w_out=w_out, ln1_g=ln1_g, ln1_b=ln1_b,
                  w_up=w_up, conv_w=conv_w, conv_b=conv_b, w_down=w_down, ln2_g=ln2_g, ln2_b=ln2_b)
    mom_m = dict(c_ctx=m_c_ctx, w_mod=m_w_mod, b_mod=m_b_mod, w_in=m_w_in, b_in=m_b_in, attn_sink=m_attn_sink,
                 q_norm_g=m_q_norm_g, k_norm_g=m_k_norm_g, w_branch_a=m_w_branch_a, w_branch_b=m_w_branch_b, w_out=m_w_out,
                 ln1_g=m_ln1_g, ln1_b=m_ln1_b, w_up=m_w_up, conv_w=m_conv_w, conv_b=m_conv_b, w_down=m_w_down,
                 ln2_g=m_ln2_g, ln2_b=m_ln2_b)
    mom_v = dict(c_ctx=v_c_ctx, w_mod=v_w_mod, b_mod=v_b_mod, w_in=v_w_in, b_in=v_b_in, attn_sink=v_attn_sink,
                 q_norm_g=v_q_norm_g, k_norm_g=v_k_norm_g, w_branch_a=v_w_branch_a, w_branch_b=v_w_branch_b, w_out=v_w_out,
                 ln1_g=v_ln1_g, ln1_b=v_ln1_b, w_up=v_w_up, conv_w=v_conv_w, conv_b=v_conv_b, w_down=v_w_down,
                 ln2_g=v_ln2_g, ln2_b=v_ln2_b)
    big_names = [nm for nm, _ in BIG]

    def shard(tree, nm):
        t = tree[nm][0]
        return jnp.pad(t, ((0, 0), (0, FF_SHARD_PAD - FF_SHARD))) if nm == "w_up" else t

    wg = all_gather("ag_weights", _pack_big([shard(params, nm).astype(MXU_DTYPE) for nm in big_names]))
    g_in, g_ba, g_bb, g_out, g_up, g_down = _unpack_big(wg)
    weights = (_cols_to_full(g_in), _cols_to_full(g_ba), _cols_to_full(g_bb), g_out.reshape(D, D), _cols_to_full(g_up),
               _ff_pad_rows(g_down.reshape(D_FF, D)))

    c_all = all_gather("ag_c", c.reshape(8, LANES)).reshape(N_DEV, D)
    cs = jnp.concatenate([c_all, c_ctx.reshape(1, D), jnp.zeros((7, D), F32)], axis=0)
    w_mod_sh = w_mod[0]
    b_mod_sh = lax.dynamic_slice(b_mod, (0, me * mod_w), (1, mod_w))
    mod_part = mod_fwd(cs, w_mod_sh, b_mod_sh)
    mg = all_gather("ag_mod", mod_part.reshape(16 * mod_w // LANES, LANES)).reshape(N_DEV, 16, mod_w)
    mod = lax.dynamic_index_in_dim(mg, me, axis=1, keepdims=False).reshape(N_MOD, D)
    mod_c = mg[:, 8, :].reshape(N_MOD, D)
    modrows = jnp.stack([mod[0], mod[1], mod_c[0], mod_c[1], mod[2], mod[3], mod[4], mod[5]], axis=0)

    conv_w_full = all_gather("ag_conv_w", jnp.pad(conv_w[0], ((0, 5), (0, FF_SHARD_PAD - FF_SHARD))))
    conv_w_full = _cols_to_full(conv_w_full[:, :3, :])
    small = dict(b_in=b_in, ln1_g=ln1_g, ln1_b=ln1_b, ln2_g=ln2_g, ln2_b=ln2_b, conv_b=_ff_pad_cols(conv_b),
                 conv_w_full=conv_w_full, q_norm_g=q_norm_g, k_norm_g=k_norm_g, attn_sink=attn_sink)
    loss, grad_x, big_grads, dmod, dmod_c, red = _local_step(x[0], ctx[0], loss_target[0], modrows, weights, small)
    loss = lax.psum(loss, AXES)

    dm = all_gather("ag_dmod", jnp.concatenate([dmod, dmod_c], axis=0).reshape(2 * N_MOD * D // LANES, LANES))
    dm = dm.reshape(N_DEV, 2, N_MOD * D)
    dm_all = jnp.concatenate([dm[:, 0], dm[:, 1]], axis=0)
    dm_sh = lax.dynamic_slice(dm_all, (0, me * mod_w), (16, mod_w))
    dw_mod, dcc, db_mod = mod_bwd(cs, w_mod_sh, dm_sh, dm_all)
    red["c_ctx"] = dcc[8]

    red_vec = jnp.concatenate([red[nm].reshape(-1) for nm, _ in RED])
    red_vec = jnp.pad(red_vec, (0, RED_ROWS * LANES - RED_TOTAL)).reshape(RED_ROWS, LANES)
    red_sum = sum8("sum_small", all_gather("ag_small", red_vec)).reshape(-1)
    gsm, off = {}, 0
    for nm, k in RED:
        gsm[nm] = red_sum[off:off + k]
        off += k
    gsm["b_mod"] = db_mod.reshape(-1)
    gsm["conv_b"] = _ff_unpad_cols(gsm["conv_b"].reshape(1, 2 * FF))
    gsm["conv_w"] = lax.dynamic_slice(gsm["conv_w"].reshape(3, 2 * FF), (0, me * FF_SHARD_PAD), (3, FF_SHARD_PAD))[:, :FF_SHARD]
    sm_names = [nm for nm, _ in SMALL]
    gs, ds, ms, vs = adamw("adamw_small", _pack_small([params[nm] for nm in sm_names]),
                           _pack_small([mom_m[nm] for nm in sm_names]), _pack_small([mom_v[nm] for nm in sm_names]),
                           [_pack_small([gsm[nm] for nm in sm_names])])
    sm_out = [_unpack_small(t) for t in (gs, ds, ms, vs)]

    dw_in, dwba, dwbb, dw_out, dw_up, dw_down = big_grads
    slabs = jnp.concatenate([t.reshape(N_DEV, -1) for t in (
        _full_to_cols(dw_in), _full_to_cols(dwba), _full_to_cols(dwbb), dw_out, _full_to_cols(dw_up),
        _ff_unpad_rows(dw_down))], axis=1)
    by_core = slabs.reshape(4, 2, BIG_ROWS, LANES)
    keep = lax.dynamic_index_in_dim(by_core, ac, axis=1, keepdims=False)
    give = lax.dynamic_index_in_dim(by_core, 1 - ac, axis=1, keepdims=False)
    got = exchange("rs_sibling", give.reshape(1, 4 * BIG_ROWS, LANES), to_chips=False).reshape(4, BIG_ROWS, LANES)
    pair = add2("rs_pair_sum", keep, got)
    outbox = jnp.stack([lax.dynamic_index_in_dim(pair, jnp.bitwise_xor(chip, m), axis=0, keepdims=False) for m in (1, 2, 3)])
    inbox = exchange("rs_chips", outbox.astype(MXU_DTYPE), to_chips=True)
    mine = lax.dynamic_index_in_dim(pair, chip, axis=0, keepdims=False)
    gb, db, mb, vb = adamw("adamw_big", _pack_big([shard(params, nm) for nm in big_names]),
                           _pack_big([shard(mom_m, nm) for nm in big_names]), _pack_big([shard(mom_v, nm) for nm in big_names]),
                           [mine, inbox[0], inbox[1], inbox[2]])
    big_out = [dict(zip(big_names, _unpack_big(t), strict=True)) for t in (gb, db, mb, vb)]
    for out in big_out:
        out["w_up"] = out["w_up"][:, :FF_SHARD]
    gm, dmo, mmo, vmo = adamw("adamw_mod", w_mod[0], m_w_mod[0], v_w_mod[0], [dw_mod])
    mod_out = (gm, dmo, mmo, vmo)

    order = ["c_ctx", "w_mod", "b_mod", "w_in", "b_in", "attn_sink", "q_norm_g", "k_norm_g", "w_branch_a", "w_branch_b",
             "w_out", "ln1_g", "ln1_b", "w_up", "conv_w", "conv_b", "w_down", "ln2_g", "ln2_b"]
    results = [loss, grad_x[None]]
    for kind in range(4):
        for nm in order:
            if nm == "w_mod":
                val = mod_out[kind]
            elif nm in big_out[kind]:
                val = big_out[kind][nm]
            else:
                val = sm_out[kind][nm]
            results.append(val.reshape(params[nm].shape))
    return tuple(results)
```

```python
import functools

import jax
import jax.numpy as jnp
import numpy as np
from jax import lax
from jax.experimental import pallas as pl
from jax.experimental.pallas import tpu as pltpu

F32 = jnp.float32
BF16 = jnp.bfloat16
MXU_DTYPE = BF16

AXES = ("x", "y", "c")
N_DEV = 8
D = 1024
HEAD_DIM = 64
N_HEADS = 8
N_KV = 2
GROUPS = 4
KV_W = GROUPS * HEAD_DIM
Q_W = N_HEADS * HEAD_DIM
GRID_W = 64
WIN = 128
ROPE_THETA = 10000.0
D_FF = 2816
FF_SHARD = 2 * D_FF // N_DEV
FF_SHARD_PAD = 768
FF = N_DEV // 2 * FF_SHARD_PAD
LN_EPS = 1e-5
QK_EPS = 1e-6
N_MOD = 6
ALPHA = 2.0 ** 0.25
Q_SCALE = HEAD_DIM ** -0.5
IN_COLS = 3584
OFF_KA, OFF_VA, OFF_QB, OFF_KB, OFF_VB, OFF_GA = 512, 640, 768, 1280, 1408, 1536
EXT_COLS = 6 * Q_W + 2 * D
X_QA, X_KA, X_VA, X_QB, X_KB, X_VB, X_GL = 0, 512, 1024, 1536, 2048, 2560, 3072
ADAM_LR, ADAM_B1, ADAM_B2, ADAM_EPS, ADAM_WD, ADAM_STEP = 0.001, 0.9, 0.999, 1e-08, 0.01, 10
LANES = 128
TM = 256
VMEM_LIMIT = 56 * 1024 * 1024
ELEMENTWISE_BLOCK_BYTES = 1 << 20
ELEMENTWISE_ROWS = (1824, 1408, 1024, 512, 256, 128, 64, 32, 16, 8)

ANY = pl.BlockSpec(memory_space=pl.ANY)
SDS = jax.ShapeDtypeStruct


def _pick(n, candidates):
    for t in candidates:
        if n % t == 0:
            return t
    raise ValueError(f"no tile for {n}")


def _full(a):
    nd = a.ndim
    return pl.BlockSpec(a.shape, lambda *_: (0,) * nd)


def _rows(tm, w, fn=lambda t: t):
    return pl.BlockSpec((tm, w), lambda i: (fn(i), 0))


def _dot(a, b):
    return jnp.dot(a.astype(MXU_DTYPE), b.astype(MXU_DTYPE), preferred_element_type=F32)


def _dot_nt(a, b):
    return lax.dot_general(a.astype(MXU_DTYPE), b.astype(MXU_DTYPE), (((1,), (1,)), ((), ())), preferred_element_type=F32)


def _dot_tn(a, b):
    return lax.dot_general(a.astype(MXU_DTYPE), b.astype(MXU_DTYPE), (((0,), (0,)), ((), ())), preferred_element_type=F32)


def _cparams(sem):
    return pltpu.CompilerParams(dimension_semantics=sem, vmem_limit_bytes=VMEM_LIMIT)


def all_gather(name, v):
    r, w = v.shape

    def body(x_ref, out_ref, send_sems, recv_sems, local_sem):
        x, y, c = (lax.axis_index(a) for a in AXES)
        me, sibling = (x, y, c), (x, y, 1 - c)
        chips = [(1 - x, y), (x, 1 - y), (1 - x, 1 - y)]

        def rows(px, py, pc):
            return out_ref.at[4 * px + 2 * py + pc]

        def copy(k, block, to, src=None):
            return pltpu.make_async_remote_copy(
                src_ref=rows(*block) if src is None else src, dst_ref=rows(*block),
                send_sem=send_sems.at[k], recv_sem=recv_sems.at[k],
                device_id=to, device_id_type=pl.DeviceIdType.MESH)

        mine = pltpu.make_async_copy(x_ref, rows(*me), local_sem)
        mine.start()
        first = [copy(0, me, sibling, src=x_ref)]
        first += [copy(1 + j, me, (*chip, c), src=x_ref) for j, chip in enumerate(chips)]
        for cp in first:
            cp.start()
        passed = [copy(4 + j, (*chip, c), sibling) for j, chip in enumerate(chips)]
        for j, chip in enumerate(chips):
            copy(1 + j, (*chip, c), me).wait_recv()
            passed[j].start()
        copy(0, sibling, me).wait_recv()
        for j, chip in enumerate(chips):
            copy(4 + j, (*chip, 1 - c), me).wait_recv()
        for cp in first + passed:
            cp.wait_send()
        mine.wait()

    return pl.pallas_call(
        body, name=name, out_shape=SDS((N_DEV, r, w), v.dtype), in_specs=[ANY], out_specs=ANY,
        scratch_shapes=[pltpu.SemaphoreType.DMA((7,)), pltpu.SemaphoreType.DMA((7,)), pltpu.SemaphoreType.DMA],
    )(v)


def exchange(name, outbox, to_chips):
    k = outbox.shape[0]
    assert k == (3 if to_chips else 1)

    def body(out_ref, in_ref, send_sems, recv_sems):
        x, y, c = (lax.axis_index(a) for a in AXES)
        peers = [(x, 1 - y, c), (1 - x, y, c), (1 - x, 1 - y, c)] if to_chips else [(x, y, 1 - c)]
        copies = [
            pltpu.make_async_remote_copy(
                src_ref=out_ref.at[m], dst_ref=in_ref.at[m], send_sem=send_sems.at[m], recv_sem=recv_sems.at[m],
                device_id=peer, device_id_type=pl.DeviceIdType.MESH)
            for m, peer in enumerate(peers)
        ]
        for cp in copies:
            cp.start()
        for cp in copies:
            cp.wait_recv()
        for cp in copies:
            cp.wait_send()

    return pl.pallas_call(
        body, name=name, out_shape=SDS(outbox.shape, outbox.dtype), in_specs=[ANY], out_specs=ANY,
        scratch_shapes=[pltpu.SemaphoreType.DMA((k,)), pltpu.SemaphoreType.DMA((k,))],
    )(outbox)


def rowwise(name, body, *, ntiles, tile_off=0, tiled, full, outs, accs=()):
    nt, nf, no = len(tiled), len(full), len(outs)

    def kern(*refs):
        i = pl.program_id(0)
        out_vals, incs = body(i + tile_off, refs[:nt], refs[nt:nt + nf])
        for r, v in zip(refs[nt + nf:nt + nf + no], out_vals, strict=True):
            r[...] = v.astype(r.dtype)
        acc_refs = refs[nt + nf + no:]

        @pl.when(i == 0)
        def _():
            for r in acc_refs:
                r[...] = jnp.zeros_like(r)

        for r, v in zip(acc_refs, incs, strict=True):
            r[...] += v

    res = pl.pallas_call(
        kern, name=name, grid=(ntiles,),
        in_specs=[s for _, s in tiled] + [_full(a) for a in full],
        out_specs=[s for _, _, s in outs] + [pl.BlockSpec(s, lambda i, n=len(s): (0,) * n) for s in accs],
        out_shape=[SDS(s, d) for s, d, _ in outs] + [SDS(s, F32) for s in accs],
        compiler_params=_cparams(("arbitrary",) if accs else ("parallel",)),
    )(*[a for a, _ in tiled], *full)
    return res[:no], res[no:]


def mm_tn(name, a, b, rows):
    ka, nb = a.shape[1], b.shape[1]
    tr = _pick(rows, (1280, 1024, 768, 512, 256))
    tn = _pick(nb, (512, 256, 128))

    def kern(a_ref, b_ref, o_ref):
        @pl.when(pl.program_id(1) == 0)
        def _():
            o_ref[...] = jnp.zeros_like(o_ref)

        o_ref[...] += _dot_tn(a_ref[...], b_ref[...])

    return pl.pallas_call(
        kern, name=name, grid=(nb // tn, rows // tr),
        in_specs=[pl.BlockSpec((tr, ka), lambda n, r: (r, 0)), pl.BlockSpec((tr, tn), lambda n, r: (r, n))],
        out_specs=pl.BlockSpec((ka, tn), lambda n, r: (0, n)), out_shape=SDS((ka, nb), F32),
        compiler_params=_cparams(("parallel", "arbitrary")),
    )(a, b)


def _swap16(t):
    w = t.shape[1]
    lane = lax.broadcasted_iota(jnp.int32, t.shape, 1)
    return jnp.where((lane & 16) == 0, pltpu.roll(t, w - 16, 1), pltpu.roll(t, 16, 1))


def _rope(t, cos, sin):
    return t * cos + _swap16(t) * sin


def _rope_t(d, cos, sin):
    return d * cos - _swap16(d) * sin


def _seg_sum64(a, bd_ref):
    bd = bd_ref[...]
    hi = a.astype(BF16)
    lo = (a - hi.astype(F32)).astype(BF16)
    return jnp.dot(hi, bd, preferred_element_type=F32) + jnp.dot(lo, bd, preferred_element_type=F32)


def _lane_block(shape):
    return jnp.right_shift(lax.broadcasted_iota(jnp.int32, shape, 1), 6)


def _stack_groups(t, dtype):
    blk = _lane_block(t.shape)
    return jnp.concatenate([jnp.where(blk == g, t, jnp.zeros_like(t)).astype(dtype) for g in range(GROUPS)], axis=0)


def _fold_groups(ts, tq):
    blk = _lane_block((tq, KV_W))
    out = jnp.zeros((tq, KV_W), ts.dtype)
    for g in range(GROUPS):
        out = jnp.where(blk == g, ts[g * tq:(g + 1) * tq], out)
    return out


def _row_to_col(row):
    hi = row.astype(BF16)
    r1 = row - hi.astype(F32)
    mid = r1.astype(BF16)
    lo = (r1 - mid.astype(F32)).astype(BF16)
    ones = jnp.ones((8, LANES), BF16)
    pad = jnp.zeros((7, row.shape[1]), BF16)
    acc = jnp.zeros((row.shape[1], LANES), F32)
    for term in (hi, mid, lo):
        acc = acc + lax.dot_general(jnp.concatenate([term, pad], axis=0), ones, (((0,), (0,)), ((), ())),
                                    preferred_element_type=F32)
    return acc[:, 0:1]


def _stack_tiles(t, dtype):
    return jnp.concatenate([_stack_groups(t[a:a + TM], dtype) for a in range(0, t.shape[0], TM)], axis=0)


def _fold_tiles(ts, tq):
    return jnp.concatenate([_fold_groups(ts[GROUPS * a:GROUPS * (a + TM)], TM) for a in range(0, tq, TM)], axis=0)


def _compact_tiles_t(tt, dtype):
    return jnp.concatenate([tt[g * HEAD_DIM:(g + 1) * HEAD_DIM, a:a + TM] for a in range(0, tt.shape[1], TM)
                            for g in range(GROUPS)], axis=1).astype(dtype)


def _layer_norm_bwd(dxh, xhat, rstd):
    m1 = jnp.mean(dxh, axis=1, keepdims=True)
    m2 = jnp.mean(dxh * xhat, axis=1, keepdims=True)
    return rstd * (dxh - m1 - xhat * m2)


def _colsum(a):
    return jnp.sum(a, axis=0, keepdims=True)


def _shifted_rows(t, prev_row, next_row):
    n = t.shape[0]
    row = lax.broadcasted_iota(jnp.int32, t.shape, 0)
    up = jnp.where(row == 0, prev_row, pltpu.roll(t, 1, 0))
    dn = jnp.where(row == n - 1, next_row, pltpu.roll(t, n - 1, 0))
    return up, dn


def mod_fwd(cs, w_sh, b_sh):
    def kern(c_ref, w_ref, b_ref, o_ref):
        o_ref[...] = _dot(jax.nn.silu(c_ref[...]), w_ref[...]) + b_ref[...]

    return pl.pallas_call(kern, name="mod_fwd", out_shape=SDS((16, w_sh.shape[1]), F32),
                          compiler_params=pltpu.CompilerParams(vmem_limit_bytes=VMEM_LIMIT))(cs, w_sh, b_sh)


def mod_bwd(cs, w_sh, dm_sh, dm_all):
    hp = lax.Precision.HIGHEST

    def kern(c_ref, w_ref, dm_ref, da_ref, dw_ref, dc_ref, db_ref):
        c = c_ref[...]
        sg = jax.nn.sigmoid(c)
        sc = c * sg
        dm = dm_ref[...]
        dmc = dm_ref[8:9, :]
        for i in range(9, 16):
            dmc = dmc + dm_ref[i:i + 1, :]
        row = lax.broadcasted_iota(jnp.int32, dm.shape, 0)
        a = jnp.where(row < 8, dm, jnp.where(row == 8, dmc, 0.0))
        dw_ref[...] = lax.dot_general(sc, a, (((0,), (0,)), ((), ())), precision=hp, preferred_element_type=F32)
        dsc = lax.dot_general(a, w_ref[...], (((1,), (1,)), ((), ())), precision=hp, preferred_element_type=F32)
        dc_ref[...] = dsc * (sg * (1.0 + c * (1.0 - sg)))
        db = da_ref[0:1, :]
        for i in range(1, 16):
            db = db + da_ref[i:i + 1, :]
        db_ref[...] = db

    return pl.pallas_call(
        kern, name="mod_bwd",
        out_shape=[SDS(w_sh.shape, F32), SDS((16, D), F32), SDS((1, dm_all.shape[1]), F32)],
        compiler_params=pltpu.CompilerParams(vmem_limit_bytes=VMEM_LIMIT))(cs, w_sh, dm_sh, dm_all)


M_SHIFT1, M_SCALE1, M_SHIFTC, M_SCALEC, M_GATE1, M_SHIFT2, M_SCALE2, M_GATE2 = range(8)


def _mrow(ref, k):
    return ref[k:k + 1, :]


def inproj_fwd(xa, cos, sin, modrows, w_ext, b_ext, qg, kg, bd, n_lat_tiles):
    n = xa.shape[0]

    def body(t, vals, fr):
        x, cs, sn = (v[...] for v in vals)
        mod, w, b, qg_r, kg_r, bd_r = fr
        is_ctx = t >= n_lat_tiles
        shift = jnp.where(is_ctx, _mrow(mod, M_SHIFTC), _mrow(mod, M_SHIFT1))
        scale = jnp.where(is_ctx, _mrow(mod, M_SCALEC), _mrow(mod, M_SCALE1))
        hb = (x * (1.0 + scale) + shift).astype(MXU_DTYPE)
        proj = jnp.dot(hb, w[...], preferred_element_type=F32) + b[...]
        cos4 = jnp.concatenate([cs] * 4, axis=1)
        sin4 = jnp.concatenate([sn] * 4, axis=1)
        qa = _rope(proj[:, X_QA:X_QA + Q_W], cos4, sin4) * Q_SCALE
        ka = _rope(proj[:, X_KA:X_KA + Q_W], cos4, sin4)
        va = proj[:, X_VA:X_VA + Q_W]
        tq = proj[:, X_QB:X_QB + Q_W]
        rq = lax.rsqrt(_seg_sum64(tq * tq, bd_r) * (1.0 / HEAD_DIM) + QK_EPS)
        qb = _rope(tq * rq * qg_r[...], cos4, sin4) * Q_SCALE
        tk = proj[:, X_KB:X_KB + Q_W]
        rk = lax.rsqrt(_seg_sum64(tk * tk, bd_r) * (1.0 / HEAD_DIM) + QK_EPS)
        kb = _rope(tk * rk * kg_r[...], cos4, sin4)
        vb = proj[:, X_VB:X_VB + Q_W]
        gl = proj[:, X_GL:]
        return [hb, qa, ka, va, qb, kb, vb, tq, rq, tk, rk, gl], []

    mx = MXU_DTYPE
    outs = [((n, D), mx, _rows(TM, D))] + [((n, Q_W), mx, _rows(TM, Q_W))] * 6 + \
           [((n, Q_W), F32, _rows(TM, Q_W))] * 4 + [((n, 2 * D), F32, _rows(TM, 2 * D))]
    res, _ = rowwise("inproj_fwd", body, ntiles=n // TM,
                     tiled=[(xa, _rows(TM, D)), (cos, _rows(TM, LANES)), (sin, _rows(TM, LANES))],
                     full=[modrows, w_ext, b_ext, qg, kg, bd], outs=outs)
    return res


def merge_fwd(oa, ob, gl, x, modrows, wba, wbb, w_out, s_rows):
    def body(t, vals, fr):
        oa_, ob_, gl_, x_ = (v[...] for v in vals)
        mod, wa, wb, wo = fr
        ya = _dot(oa_, wa[...])
        yb = _dot(ob_, wb[...])
        ga = jax.nn.sigmoid(gl_[:, :D])
        gb = jax.nn.sigmoid(gl_[:, D:])
        mrg = ga * ya + gb * yb
        y = _dot(mrg, wo[...])
        r1 = ALPHA * x_ + _mrow(mod, M_GATE1) * y
        mu = jnp.mean(r1, axis=1, keepdims=True)
        xc = r1 - mu
        var = jnp.mean(xc * xc, axis=1, keepdims=True)
        rstd = lax.rsqrt(var + LN_EPS)
        xhat = xc * rstd
        return [ya, yb, mrg, y, xhat, rstd], []

    outs = [((s_rows, D), F32, _rows(TM, D))] * 2 + [((s_rows, D), MXU_DTYPE, _rows(TM, D))] + \
           [((s_rows, D), F32, _rows(TM, D))] * 2 + [((s_rows, 1), F32, _rows(TM, 1))]
    res, _ = rowwise("merge_fwd", body, ntiles=s_rows // TM,
                     tiled=[(oa, _rows(TM, Q_W)), (ob, _rows(TM, Q_W)), (gl, _rows(TM, 2 * D)), (x, _rows(TM, D))],
                     full=[modrows, wba, wbb, w_out], outs=outs)
    return res


def ffn_up_fwd(xhat1, modrows, ln_g, ln_b, w_up, s_rows):
    def body(t, vals, fr):
        xh = vals[0][...]
        mod, g_r, b_r, w = fr
        x1 = xh * g_r[...] + b_r[...]
        h2 = (x1 * (1.0 + _mrow(mod, M_SCALE2)) + _mrow(mod, M_SHIFT2)).astype(MXU_DTYPE)
        return [h2, jnp.dot(h2, w[...], preferred_element_type=F32)], []

    res, _ = rowwise("ffn_up_fwd", body, ntiles=s_rows // TM, tiled=[(xhat1, _rows(TM, D))],
                     full=[modrows, ln_g, ln_b, w_up],
                     outs=[((s_rows, D), MXU_DTYPE, _rows(TM, D)), ((s_rows, 2 * FF), F32, _rows(TM, 2 * FF))])
    return res


TC = 128


def _halo_specs(tm, w, s_rows):
    per = tm // 8
    last = s_rows // 8 - 1
    return (pl.BlockSpec((8, w), lambda i: (jnp.maximum(i * per - 1, 0), 0)),
            pl.BlockSpec((8, w), lambda i: (jnp.minimum((i + 1) * per, last), 0)))


def _halo_rows(t, ntiles, prev_ref, next_ref):
    prev_row = jnp.where(t == 0, 0.0, prev_ref[7:8, :].astype(F32))
    next_row = jnp.where(t == ntiles - 1, 0.0, next_ref[0:1, :].astype(F32))
    return prev_row, next_row


def conv_swiglu_fwd(u0, conv_w8, conv_b, s_rows):
    w2 = 2 * FF
    nt = s_rows // TC

    def body(t, vals, fr):
        u_ref, pv, nx = vals
        cw, cb = fr
        u = u_ref[...]
        up, dn = _shifted_rows(u, *_halo_rows(t, nt, pv, nx))
        uc = cw[0:1, :] * up + cw[1:2, :] * u + cw[2:3, :] * dn + cb[...]
        gate, val = uc[:, :FF], uc[:, FF:]
        return [gate * jax.nn.sigmoid(gate) * val], []

    hp, hn = _halo_specs(TC, w2, s_rows)
    res, _ = rowwise("conv_swiglu_fwd", body, ntiles=nt,
                     tiled=[(u0, _rows(TC, w2)), (u0, hp), (u0, hn)], full=[conv_w8, conv_b],
                     outs=[((s_rows, FF), MXU_DTYPE, _rows(TC, FF))])
    return res[0]


def ffn_down_loss(a, xhat1, target, modrows, ln1_g, ln1_b, ln2_g, ln2_b, w_down, s_rows):
    def body(t, vals, fr):
        a_, xh1, tgt = (v[...] for v in vals)
        mod, g1, b1, g2, b2, wd = fr
        y2 = jnp.dot(a_, wd[...], preferred_element_type=F32)
        x1 = xh1 * g1[...] + b1[...]
        gate2 = _mrow(mod, M_GATE2)
        r2 = ALPHA * x1 + gate2 * y2
        mu = jnp.mean(r2, axis=1, keepdims=True)
        xc = r2 - mu
        var = jnp.mean(xc * xc, axis=1, keepdims=True)
        rstd = lax.rsqrt(var + LN_EPS)
        xhat = xc * rstd
        out = xhat * g2[...] + b2[...]
        diff = out - tgt
        loss = 0.5 * jnp.sum(jnp.mean(diff * diff, axis=1, keepdims=True), axis=0, keepdims=True)
        dout = diff * (1.0 / D)
        dr2 = _layer_norm_bwd(dout * g2[...], xhat, rstd)
        incs = [loss, _colsum(dout * xhat), _colsum(dout), _colsum(dr2 * y2)]
        return [dr2, dr2 * gate2], incs

    res, accs = rowwise("ffn_down_loss", body, ntiles=s_rows // TM,
                        tiled=[(a, _rows(TM, FF)), (xhat1, _rows(TM, D)), (target, _rows(TM, D))],
                        full=[modrows, ln1_g, ln1_b, ln2_g, ln2_b, w_down],
                        outs=[((s_rows, D), F32, _rows(TM, D)), ((s_rows, D), MXU_DTYPE, _rows(TM, D))],
                        accs=[(1, 1), (1, D), (1, D), (1, D)])
    return res, accs


def ffn_down_bwd(dy2, w_down_t, s_rows):
    def body(t, vals, fr):
        return [jnp.dot(vals[0][...], fr[0][...], preferred_element_type=F32)], []

    res, _ = rowwise("ffn_down_bwd", body, ntiles=s_rows // TM, tiled=[(dy2, _rows(TM, D))], full=[w_down_t],
                     outs=[((s_rows, FF), F32, _rows(TM, FF))])
    return res[0]


def swiglu_conv_bwd(u0, da, conv_w8, conv_b, s_rows):
    w2 = 2 * FF
    nt = s_rows // TC
    n = TC + 16

    def body(t, vals, fr):
        u_ref, upv, unx, da_ref, apv, anx = vals
        cw, cb = fr
        first, last = t == 0, t == nt - 1
        ue = jnp.concatenate([jnp.where(first, 0.0, upv[...]), u_ref[...], jnp.where(last, 0.0, unx[...])], axis=0)
        ae = jnp.concatenate([jnp.where(first, 0.0, apv[...]), da_ref[...], jnp.where(last, 0.0, anx[...])], axis=0)
        up = pltpu.roll(ue, 1, 0)
        dn = pltpu.roll(ue, n - 1, 0)
        uc = cw[0:1, :] * up + cw[1:2, :] * ue + cw[2:3, :] * dn + cb[...]
        gate, val = uc[:, :FF], uc[:, FF:]
        sg = jax.nn.sigmoid(gate)
        du = jnp.concatenate([ae * val * (sg * (1.0 + gate * (1.0 - sg))), ae * (gate * sg)], axis=1)
        du0 = cw[0:1, :] * pltpu.roll(du, n - 1, 0) + cw[1:2, :] * du + cw[2:3, :] * pltpu.roll(du, 1, 0)
        rows = slice(8, 8 + TC)
        dut = du[rows]
        return [du0[rows]], [_colsum(dut), _colsum(up[rows] * dut), _colsum(ue[rows] * dut), _colsum(dn[rows] * dut)]

    hp, hn = _halo_specs(TC, w2, s_rows)
    ap, an = _halo_specs(TC, FF, s_rows)
    res, accs = rowwise("swiglu_conv_bwd", body, ntiles=nt,
                        tiled=[(u0, _rows(TC, w2)), (u0, hp), (u0, hn), (da, _rows(TC, FF)), (da, ap), (da, an)],
                        full=[conv_w8, conv_b], outs=[((s_rows, w2), MXU_DTYPE, _rows(TC, w2))], accs=[(1, w2)] * 4)
    return res[0], accs


def ffn_up_ln1_bwd(du0, dr2, xhat1, y, rstd1, modrows, ln_g, ln_b, w_up_t, s_rows):
    def body(t, vals, fr):
        du0_, dr2_, xh, y_, rstd = (v[...] for v in vals)
        mod, g_r, b_r, wt = fr
        dh2 = jnp.dot(du0_, wt[...], preferred_element_type=F32)
        x1 = xh * g_r[...] + b_r[...]
        dx1 = ALPHA * dr2_ + dh2 * (1.0 + _mrow(mod, M_SCALE2))
        dr1 = _layer_norm_bwd(dx1 * g_r[...], xh, rstd)
        incs = [_colsum(dh2 * x1), _colsum(dh2), _colsum(dx1 * xh), _colsum(dx1), _colsum(dr1 * y_)]
        return [dr1 * _mrow(mod, M_GATE1), ALPHA * dr1], incs

    res, accs = rowwise("ffn_up_ln1_bwd", body, ntiles=s_rows // TM,
                        tiled=[(du0, _rows(TM, 2 * FF)), (dr2, _rows(TM, D)), (xhat1, _rows(TM, D)), (y, _rows(TM, D)),
                               (rstd1, _rows(TM, 1))],
                        full=[modrows, ln_g, ln_b, w_up_t],
                        outs=[((s_rows, D), MXU_DTYPE, _rows(TM, D)), ((s_rows, D), F32, _rows(TM, D))],
                        accs=[(1, D)] * 5)
    return res, accs


def merge_bwd(dy, ya, yb, gl, w_out_t, wba_t, wbb_t, s_rows):
    def body(t, vals, fr):
        dy_, ya_, yb_, gl_ = (v[...] for v in vals)
        wot, wat, wbt = fr
        dmrg = jnp.dot(dy_, wot[...], preferred_element_type=F32)
        ga = jax.nn.sigmoid(gl_[:, :D])
        gb = jax.nn.sigmoid(gl_[:, D:])
        dya = dmrg * ga
        dyb = dmrg * gb
        dgl = jnp.concatenate([dmrg * ya_ * ga * (1.0 - ga), dmrg * yb_ * gb * (1.0 - gb)], axis=1)
        return [dya, dyb, dgl, _dot(dya, wat[...]), _dot(dyb, wbt[...])], []

    mx = MXU_DTYPE
    res, _ = rowwise("merge_bwd", body, ntiles=s_rows // TM,
                     tiled=[(dy, _rows(TM, D)), (ya, _rows(TM, D)), (yb, _rows(TM, D)), (gl, _rows(TM, 2 * D))],
                     full=[w_out_t, wba_t, wbb_t],
                     outs=[((s_rows, D), mx, _rows(TM, D))] * 2 + [((s_rows, 2 * D), F32, _rows(TM, 2 * D))] +
                          [((s_rows, Q_W), F32, _rows(TM, Q_W))] * 2)
    return res


def qk_bwd(dqa, dka_t, dva_t, dqb_heads, dkb_t, dvb_t, dgl, tq, rq, tk, rk, cos, sin, qg, kg, bd, place, n_lat_tiles, n):
    def placed(xt_ref, place_ref):
        xt = xt_ref[...]
        hi = xt.astype(BF16)
        r1 = xt - hi.astype(F32)
        mid = r1.astype(BF16)
        lo = (r1 - mid.astype(F32)).astype(BF16)
        pm = place_ref[...]
        return sum(lax.dot_general(term, pm, (((0,), (0,)), ((), ())), preferred_element_type=F32) for term in (hi, mid, lo))

    def body(t, vals, fr):
        dqa_, dgl_, tq_, rq_, tk_, rk_, cs, sn = (v[...] for v in vals[:8])
        qg_r, kg_r, bd_r, pl_r = fr
        dka_, dva_, dkb_, dvb_ = (placed(v, pl_r) for v in vals[8:12])
        dqb_ = jnp.concatenate([v[...] for v in vals[12:]], axis=1)
        is_ctx = t >= n_lat_tiles
        cos4 = jnp.concatenate([cs] * 4, axis=1)
        sin4 = jnp.concatenate([sn] * 4, axis=1)
        zero = jnp.zeros_like(dqa_)
        dpqa = jnp.where(is_ctx, zero, _rope_t(dqa_, cos4, sin4) * Q_SCALE)
        dpka = _rope_t(dka_, cos4, sin4)
        dpva = dva_
        dnq = jnp.where(is_ctx, zero, _rope_t(dqb_, cos4, sin4) * Q_SCALE)
        gq = qg_r[...] * dnq
        dtq = rq_ * gq - tq_ * (rq_ * rq_ * rq_) * (_seg_sum64(gq * tq_, bd_r) * (1.0 / HEAD_DIM))
        dnk = _rope_t(dkb_, cos4, sin4)
        gk = kg_r[...] * dnk
        dtk = rk_ * gk - tk_ * (rk_ * rk_ * rk_) * (_seg_sum64(gk * tk_, bd_r) * (1.0 / HEAD_DIM))
        dgl32 = jnp.where(is_ctx, jnp.zeros_like(dgl_), dgl_)
        dproj = jnp.concatenate([dpqa, dpka, dpva, dtq, dtk, dvb_, dgl32], axis=1)
        return [dproj], [_colsum(dproj), _colsum(dnq * tq_ * rq_), _colsum(dnk * tk_ * rk_)]

    lat = lambda t: jnp.minimum(t, n_lat_tiles - 1)
    qs = _rows(TM, Q_W)
    ts = pl.BlockSpec((N_KV * HEAD_DIM, TM), lambda i: (0, i))
    res, accs = rowwise(
        "qk_bwd", body, ntiles=n // TM,
        tiled=[(dqa, _rows(TM, Q_W, lat)), (dgl, _rows(TM, 2 * D, lat)),
               (tq, qs), (rq, qs), (tk, qs), (rk, qs), (cos, _rows(TM, LANES)), (sin, _rows(TM, LANES)),
               (dka_t, ts), (dva_t, ts), (dkb_t, ts), (dvb_t, ts)] + [(d, _rows(TM, KV_W, lat)) for d in dqb_heads],
        full=[qg, kg, bd, place], outs=[((n, EXT_COLS), MXU_DTYPE, _rows(TM, EXT_COLS))],
        accs=[(1, EXT_COLS), (1, Q_W), (1, Q_W)])
    return res[0], accs


def inproj_bwd(name, dproj, xa, dxp, modrows, w_ext_t, *, ntiles, tile_off, is_ctx, out_rows):
    kc = M_SCALEC if is_ctx else M_SCALE1

    def body(t, vals, fr):
        dp, x_ = vals[0][...], vals[1][...]
        mod, wt = fr
        dh = jnp.dot(dp, wt[...], preferred_element_type=F32)
        incs = [_colsum(dh * x_), _colsum(dh)]
        if is_ctx:
            return [], incs
        return [vals[2][...] + dh * (1.0 + _mrow(mod, kc))], incs

    tiled = [(dproj, _rows(TM, EXT_COLS, lambda i: i + tile_off)), (xa, _rows(TM, D, lambda i: i + tile_off))]
    outs = []
    if not is_ctx:
        tiled.append((dxp, _rows(TM, D)))
        outs = [((out_rows, D), F32, _rows(TM, D))]
    return rowwise(name, body, ntiles=ntiles, tiled=tiled, full=[modrows, w_ext_t], outs=outs, accs=[(1, D)] * 2)


def _attn_semantics():
    return _cparams(("arbitrary", "arbitrary", "arbitrary"))


GLOB_TK = (1280, 1024, 768, 512, 256)
GLOB_TQ = (512, 256)
GLOB_BWD_TQ = GLOB_TQ
KEY_CHUNK = 256


def glob_fwd(q, kt, v_t, s_rows):
    n = kt.shape[0]
    tq = _pick(s_rows, GLOB_TQ)
    tk = _pick(n, GLOB_TK)
    nq, nk = s_rows // tq, n // tk
    r = GROUPS * tq
    nch = tk // KEY_CHUNK

    def produce(qs, k_ref, s_buf, c, mx):
        rows = slice(c * KEY_CHUNK, (c + 1) * KEY_CHUNK)
        sn = _dot_nt(k_ref[rows, :], qs[...])
        s_buf[rows, :] = sn
        return jnp.maximum(mx, jnp.max(sn, axis=0, keepdims=True))

    def kern(q_ref, k0_ref, kn_ref, vt_ref, o_ref, lse_ref, qs, s_buf, mx_buf, m_s, l_s, acc):
        j = pl.program_id(2)

        @pl.when(j == 0)
        def _():
            qs[...] = _stack_tiles(q_ref[...], qs.dtype)
            mx = jnp.full((1, r), -jnp.inf, F32)
            for c in range(nch):
                mx = produce(qs, k0_ref, s_buf, c, mx)
            mx_buf[...] = mx
            m_s[...] = jnp.full_like(m_s, -jnp.inf)
            l_s[...] = jnp.zeros_like(l_s)
            acc[...] = jnp.zeros_like(acc)

        m_prev = m_s[...]
        m_new = jnp.maximum(m_prev, mx_buf[...])
        alpha = jnp.exp(m_prev - m_new)
        a = alpha * acc[...]
        ls = alpha * l_s[...]
        mx = jnp.full((1, r), -jnp.inf, F32)
        for c in range(nch):
            rows = slice(c * KEY_CHUNK, (c + 1) * KEY_CHUNK)
            p = jnp.exp(s_buf[rows, :] - m_new)
            ls = ls + jnp.sum(p, axis=0, keepdims=True)
            a = a + jnp.dot(vt_ref[0, :, rows], p.astype(MXU_DTYPE), preferred_element_type=F32)
            mx = produce(qs, kn_ref, s_buf, c, mx)
        mx_buf[...] = mx
        l_s[...] = ls
        acc[...] = a
        m_s[...] = m_new

        @pl.when(j == nk - 1)
        def _():
            o_t = acc[...] / l_s[...]
            o_ref[...] = jnp.concatenate([_untranspose_groups(o_t[:, GROUPS * a:GROUPS * (a + TM)], TM)
                                          for a in range(0, tq, TM)], axis=0)
            lse_ref[0, 0] = _row_to_col(m_s[...] + jnp.log(l_s[...]))

    kspec = lambda f: pl.BlockSpec((tk, KV_W), lambda h, i, j: (f(j), h))
    return pl.pallas_call(
        kern, name="glob_fwd", grid=(N_KV, nq, nk),
        in_specs=[pl.BlockSpec((tq, KV_W), lambda h, i, j: (i, h)), kspec(lambda j: 0),
                  kspec(lambda j: jnp.minimum(j + 1, nk - 1)), pl.BlockSpec((1, HEAD_DIM, tk), lambda h, i, j: (h, 0, j))],
        out_specs=[pl.BlockSpec((tq, KV_W), lambda h, i, j: (i, h)),
                   pl.BlockSpec((1, 1, r, 1), lambda h, i, j: (h, i, 0, 0))],
        out_shape=[SDS((s_rows, Q_W), F32), SDS((N_KV, nq, r, 1), F32)],
        scratch_shapes=[pltpu.VMEM((r, KV_W), MXU_DTYPE), pltpu.VMEM((tk, r), F32), pltpu.VMEM((1, r), F32),
                        pltpu.VMEM((1, r), F32), pltpu.VMEM((1, r), F32), pltpu.VMEM((HEAD_DIM, r), F32)],
        compiler_params=_attn_semantics(),
    )(q, kt, kt, v_t)


def attn_delta(o, do, s_rows):
    tq = _pick(s_rows, GLOB_BWD_TQ)
    nq = s_rows // tq
    r = GROUPS * tq

    def kern(o_ref, do_ref, d_ref):
        d_ref[0, 0] = jnp.sum(_stack_tiles(do_ref[...], F32) * _stack_tiles(o_ref[...], F32), axis=1, keepdims=True)

    qspec = pl.BlockSpec((tq, KV_W), lambda h, i: (i, h))
    return pl.pallas_call(
        kern, name="attn_delta", grid=(N_KV, nq), in_specs=[qspec, qspec],
        out_specs=pl.BlockSpec((1, 1, r, 1), lambda h, i: (h, i, 0, 0)), out_shape=SDS((N_KV, nq, r, 1), F32),
        compiler_params=_cparams(("parallel", "parallel")),
    )(o, do)


def _compact_t(tt, dtype):
    return jnp.concatenate([tt[g * HEAD_DIM:(g + 1) * HEAD_DIM, :] for g in range(GROUPS)], axis=1).astype(dtype)


def glob_bwd(q, q_t, kt, vt, do, do_t, lse, delta, h, s_rows):
    n = kt.shape[0]
    tq = _pick(s_rows, GLOB_BWD_TQ)
    tk = _pick(n, GLOB_TK)
    nq, nk = s_rows // tq, n // tk
    r = GROUPS * tq
    nch = tk // KEY_CHUNK

    def kern(q_ref, qt_ref, k_ref, v_ref, do_ref, dot_ref, lse_ref, dl_ref, dq_ref, dkt_ref, dvt_ref, p_buf, ds_buf):
        j = pl.program_id(0)
        i = pl.program_id(1)

        @pl.when(i == 0)
        def _():
            dkt_ref[...] = jnp.zeros_like(dkt_ref)
            dvt_ref[...] = jnp.zeros_like(dvt_ref)

        qs = _stack_tiles(q_ref[...], MXU_DTYPE)
        dos = _stack_tiles(do_ref[...], MXU_DTYPE)
        lse_b = jnp.broadcast_to(lse_ref[0], (r, LANES))
        dl_b = jnp.broadcast_to(dl_ref[0], (r, LANES))
        for c in range(nch):
            lo = c * KEY_CHUNK
            sc = _dot_nt(qs, k_ref[lo:lo + KEY_CHUNK, :])
            dpc = _dot_nt(dos, v_ref[lo:lo + KEY_CHUNK, :])
            for t in range(KEY_CHUNK // LANES):
                sl = slice(t * LANES, (t + 1) * LANES)
                pt = jnp.exp(sc[:, sl] - lse_b)
                p_buf[:, lo + t * LANES:lo + (t + 1) * LANES] = pt.astype(p_buf.dtype)
                ds_buf[:, lo + t * LANES:lo + (t + 1) * LANES] = (pt * (dpc[:, sl] - dl_b)).astype(ds_buf.dtype)
        dq_t = _fold_tiles(jnp.dot(ds_buf[...], k_ref[...], preferred_element_type=F32), tq)
        rows = pl.ds(pl.multiple_of(i * tq, tq), tq)

        @pl.when(j == 0)
        def _():
            dq_ref[rows, :] = dq_t

        @pl.when(j > 0)
        def _():
            dq_ref[rows, :] += dq_t

        dvt_ref[...] += jnp.dot(_compact_tiles_t(dot_ref[...], MXU_DTYPE), p_buf[...], preferred_element_type=F32)
        dkt_ref[...] += jnp.dot(_compact_tiles_t(qt_ref[...], MXU_DTYPE), ds_buf[...], preferred_element_type=F32)

    col = pl.BlockSpec((1, r, 1), lambda j, i: (i, 0, 0))
    qspec = pl.BlockSpec((tq, KV_W), lambda j, i: (i, h))
    tspec = pl.BlockSpec((KV_W, tq), lambda j, i: (h, i))
    kspec = pl.BlockSpec((tk, KV_W), lambda j, i: (j, h))
    ospec = pl.BlockSpec((HEAD_DIM, tk), lambda j, i: (0, j))
    return pl.pallas_call(
        kern, name=f"glob_bwd_h{h}", grid=(nk, nq),
        in_specs=[qspec, tspec, kspec, kspec, qspec, tspec, col, col],
        out_specs=[pl.BlockSpec(memory_space=pltpu.VMEM), ospec, ospec],
        out_shape=[SDS((s_rows, KV_W), F32), SDS((HEAD_DIM, n), F32), SDS((HEAD_DIM, n), F32)],
        scratch_shapes=[pltpu.VMEM((r, tk), MXU_DTYPE), pltpu.VMEM((r, tk), MXU_DTYPE)],
        compiler_params=_cparams(("arbitrary", "arbitrary")),
    )(q, q_t, kt, vt, do, do_t, lse, delta)


TW = 2 * WIN
WR = GROUPS * TW
WLAT = 4 * WIN


def _win_cat(dst, parts):
    off = 0
    for p in parts:
        dst[off:off + p.shape[0], :] = p[...]
        off += p.shape[0]


def _win_specs(s_rows, c_rows):
    nb = s_rows // WIN
    prev = lambda i: jnp.maximum(2 * i - 1, 0)
    nxt = lambda i: jnp.minimum(2 * i + 2, nb - 1)
    rows = [pl.BlockSpec((WIN, KV_W), lambda h, i: (prev(i), h)), pl.BlockSpec((TW, KV_W), lambda h, i: (i, h)),
            pl.BlockSpec((WIN, KV_W), lambda h, i: (nxt(i), h)), pl.BlockSpec((c_rows, KV_W), lambda h, i: (s_rows // c_rows, h))]
    cols = [pl.BlockSpec((1, HEAD_DIM, WIN), lambda h, i: (h, 0, prev(i))), pl.BlockSpec((1, HEAD_DIM, TW), lambda h, i: (h, 0, i)),
            pl.BlockSpec((1, HEAD_DIM, WIN), lambda h, i: (h, 0, nxt(i))),
            pl.BlockSpec((1, HEAD_DIM, c_rows), lambda h, i: (h, 0, s_rows // c_rows))]
    return rows, cols


def _win_mask(i, s_rows, shape, keys_on_rows):
    a = lax.broadcasted_iota(jnp.int32, shape, 0)
    b = lax.broadcasted_iota(jnp.int32, shape, 1)
    kk, qq = (a, b) if keys_on_rows else (b, a)
    qpos = i * TW + (qq & (TW - 1))
    kpos = (2 * i - 1) * WIN + kk
    band = (jnp.abs(qpos - kpos) <= WIN) & (kpos >= 0) & (kpos < s_rows)
    return (kk >= WLAT) | band


def _untranspose_groups(o_t, tq):
    row = lax.broadcasted_iota(jnp.int32, (HEAD_DIM, KV_W), 0)
    col = lax.broadcasted_iota(jnp.int32, (HEAD_DIM, KV_W), 1)
    hi = o_t.astype(BF16)
    r1 = o_t - hi.astype(F32)
    mid = r1.astype(BF16)
    lo = (r1 - mid.astype(F32)).astype(BF16)
    o = jnp.zeros((tq, KV_W), F32)
    for g in range(GROUPS):
        sel = jnp.where(col == row + g * HEAD_DIM, 1.0, 0.0).astype(BF16)
        for term in (hi, mid, lo):
            o = o + lax.dot_general(term[:, g * tq:(g + 1) * tq], sel, (((0,), (0,)), ((), ())), preferred_element_type=F32)
    return o


def win_fwd(q, kt, v_t, sinkrow, s_rows, c_rows):
    nt = s_rows // TW
    nkeys = WLAT + c_rows

    def kern(q_ref, kp, kc, kn, kx, vp, vc, vn, vx, sink_ref, o_ref, lse_ref, kcat):
        i = pl.program_id(1)
        _win_cat(kcat, (kp, kc, kn, kx))
        qs = _stack_groups(q_ref[...], MXU_DTYPE)
        st = _dot_nt(kcat[...], qs)
        st = jnp.where(_win_mask(i, s_rows, st.shape, True), st, -jnp.inf)
        sink = sink_ref[0]
        m = jnp.maximum(jnp.max(st, axis=0, keepdims=True), sink)
        e = jnp.exp(st - m)
        den = jnp.sum(e, axis=0, keepdims=True) + jnp.exp(sink - m)
        v_cat = jnp.concatenate([vp[0], vc[0], vn[0], vx[0]], axis=1)
        o_t = jnp.dot(v_cat, e.astype(MXU_DTYPE), preferred_element_type=F32) / den
        o_ref[...] = _untranspose_groups(o_t, TW)
        lse_ref[0, 0] = _row_to_col(m + jnp.log(den))

    rows, cols = _win_specs(s_rows, c_rows)
    qspec = pl.BlockSpec((TW, KV_W), lambda h, i: (i, h))
    rowv = pl.BlockSpec((1, 1, WR, 1), lambda h, i: (h, i, 0, 0))
    return pl.pallas_call(
        kern, name="win_fwd", grid=(N_KV, nt),
        in_specs=[qspec] + rows + cols + [pl.BlockSpec((1, 1, WR), lambda h, i: (h, 0, 0))],
        out_specs=[qspec, rowv], out_shape=[SDS((s_rows, Q_W), F32), SDS((N_KV, nt, WR, 1), F32)],
        scratch_shapes=[pltpu.VMEM((nkeys, KV_W), MXU_DTYPE)],
        compiler_params=_cparams(("parallel", "parallel")),
    )(q, kt, kt, kt, kt, v_t, v_t, v_t, v_t, sinkrow)


def win_bwd(q, q_t, kt, vt, sinkcol, o, do, do_t, lse, s_rows, c_rows):
    nt = s_rows // TW
    nkeys = WLAT + c_rows
    n = s_rows + c_rows
    ctx0 = WIN + s_rows

    def kern(q_ref, qt_ref, kp, kc, kn, kx, vp, vc, vn, vx, sink_ref, o_ref, do_ref, dot_ref, lse_ref,
             dq_ref, dkt_ref, dvt_ref, dsk_ref, kcat, vcat):
        i = pl.program_id(1)

        @pl.when(i == 0)
        def _():
            dkt_ref[...] = jnp.zeros_like(dkt_ref)
            dvt_ref[...] = jnp.zeros_like(dvt_ref)
            dsk_ref[...] = jnp.zeros_like(dsk_ref)

        _win_cat(kcat, (kp, kc, kn, kx))
        _win_cat(vcat, (vp, vc, vn, vx))
        qs = _stack_groups(q_ref[...], MXU_DTYPE)
        do32 = _stack_groups(do_ref[...], F32)
        delta = jnp.sum(do32 * _stack_groups(o_ref[...], F32), axis=1, keepdims=True)
        dos = do32.astype(MXU_DTYPE)
        lse_c = lse_ref[0, 0]
        s = _dot_nt(qs, kcat[...])
        s = jnp.where(_win_mask(i, s_rows, s.shape, False), s, -jnp.inf)
        p = jnp.exp(s - lse_c)
        ds = p * (_dot_nt(dos, vcat[...]) - delta)
        dq_ref[...] = _fold_groups(_dot(ds, kcat[...]), TW)
        dvt = jnp.dot(_compact_t(dot_ref[...], MXU_DTYPE), p.astype(MXU_DTYPE), preferred_element_type=F32)
        dkt = jnp.dot(_compact_t(qt_ref[...], MXU_DTYPE), ds.astype(MXU_DTYPE), preferred_element_type=F32)
        lat = pl.ds(pl.multiple_of(i * TW, TW), WLAT)
        dkt_ref[0, :, lat] += dkt[:, :WLAT]
        dvt_ref[0, :, lat] += dvt[:, :WLAT]
        dkt_ref[0, :, ctx0:ctx0 + c_rows] += dkt[:, WLAT:]
        dvt_ref[0, :, ctx0:ctx0 + c_rows] += dvt[:, WLAT:]
        dsk_ref[0] += -(jnp.exp(sink_ref[0][:, 0:1] - lse_c) * delta)

    rows, _ = _win_specs(s_rows, c_rows)
    qspec = pl.BlockSpec((TW, KV_W), lambda h, i: (i, h))
    tspec = pl.BlockSpec((KV_W, TW), lambda h, i: (h, i))
    col = pl.BlockSpec((1, 1, WR, 1), lambda h, i: (h, i, 0, 0))
    kvt = pl.BlockSpec((1, HEAD_DIM, WIN + n), lambda h, i: (h, 0, 0))
    return pl.pallas_call(
        kern, name="win_bwd", grid=(N_KV, nt),
        in_specs=[qspec, tspec] + rows + rows + [pl.BlockSpec((1, WR, LANES), lambda h, i: (h, 0, 0)), qspec, qspec, tspec, col],
        out_specs=[qspec, kvt, kvt, pl.BlockSpec((1, WR, 1), lambda h, i: (h, 0, 0))],
        out_shape=[SDS((s_rows, Q_W), F32), SDS((N_KV, HEAD_DIM, WIN + n), F32), SDS((N_KV, HEAD_DIM, WIN + n), F32),
                   SDS((N_KV, WR, 1), F32)],
        scratch_shapes=[pltpu.VMEM((nkeys, KV_W), MXU_DTYPE), pltpu.VMEM((nkeys, KV_W), MXU_DTYPE)],
        compiler_params=_cparams(("arbitrary", "arbitrary")),
    )(q, q_t, kt, kt, kt, kt, vt, vt, vt, vt, sinkcol, o, do, do_t, lse)


def adamw(name, w, m, v, grads):
    r, wd = w.shape
    tr = _pick(r, [t for t in ELEMENTWISE_ROWS if t * wd * 4 <= ELEMENTWISE_BLOCK_BYTES])
    stacked = not isinstance(grads, (list, tuple))
    ng = grads.shape[0] if stacked else len(grads)

    def kern(*refs):
        w_ref, m_ref, v_ref = refs[:3]
        g_refs = refs[3:-4]
        g_out, d_out, m_out, v_out = refs[-4:]
        if stacked:
            g = g_refs[0][0]
            for k in range(1, ng):
                g = g + g_refs[0][k]
        else:
            g = g_refs[0][...]
            for gr in g_refs[1:]:
                g = g + gr[...]
        wv = w_ref[...]
        mn = ADAM_B1 * m_ref[...] + (1.0 - ADAM_B1) * g
        vn = ADAM_B2 * v_ref[...] + (1.0 - ADAM_B2) * (g * g)
        m_hat = mn / (1.0 - ADAM_B1 ** ADAM_STEP)
        v_hat = vn / (1.0 - ADAM_B2 ** ADAM_STEP)
        g_out[...] = g
        d_out[...] = -ADAM_LR * (m_hat / (jnp.sqrt(v_hat) + ADAM_EPS) + ADAM_WD * wv)
        m_out[...] = mn
        v_out[...] = vn

    spec = pl.BlockSpec((tr, wd), lambda i: (i, 0))
    gspecs = [pl.BlockSpec((ng, tr, wd), lambda i: (0, i, 0))] if stacked else [spec] * ng
    return pl.pallas_call(
        kern, name=name, grid=(r // tr,), in_specs=[spec] * 3 + gspecs, out_specs=[spec] * 4,
        out_shape=[SDS((r, wd), F32)] * 4, compiler_params=_cparams(("parallel",)),
    )(w, m, v, *([grads] if stacked else grads))


def add2(name, a, b):
    k, r, w = a.shape
    tr = _pick(r, [t for t in ELEMENTWISE_ROWS if t * w * 4 <= ELEMENTWISE_BLOCK_BYTES])

    def kern(a_ref, b_ref, o_ref):
        o_ref[...] = a_ref[...] + b_ref[...]

    spec = pl.BlockSpec((1, tr, w), lambda s, i: (s, i, 0))
    return pl.pallas_call(kern, name=name, grid=(k, r // tr), in_specs=[spec, spec], out_specs=spec,
                          out_shape=SDS(a.shape, a.dtype), compiler_params=_cparams(("parallel", "parallel")))(a, b)


def _rep4(a, off):
    return jnp.concatenate([a[:, off + HEAD_DIM * h: off + HEAD_DIM * (h + 1)] for h in range(N_KV) for _ in range(GROUPS)], axis=1)


def _extend_cols(a):
    return jnp.concatenate([a[:, 0:OFF_KA], _rep4(a, OFF_KA), _rep4(a, OFF_VA), a[:, OFF_QB:OFF_KB],
                            _rep4(a, OFF_KB), _rep4(a, OFF_VB), a[:, OFF_GA:]], axis=1)


def _fold4(a, off):
    r = a.shape[0]
    return a[:, off:off + Q_W].reshape(r, N_KV, GROUPS, HEAD_DIM).sum(axis=2).reshape(r, N_KV * HEAD_DIM)


def _fold_cols(a):
    return jnp.concatenate([a[:, X_QA:X_QA + Q_W], _fold4(a, X_KA), _fold4(a, X_VA), a[:, X_QB:X_QB + Q_W],
                            _fold4(a, X_KB), _fold4(a, X_VB), a[:, X_GL:]], axis=1)


def _rope_tables(s_rows, c_rows):
    pos = jnp.arange(s_rows, dtype=jnp.int32)
    rows = (pos // GRID_W).astype(F32)
    cols = (pos % GRID_W).astype(F32)
    n_freq = HEAD_DIM // 4
    inv_freq = ROPE_THETA ** (-jnp.arange(n_freq, dtype=F32) / n_freq)
    ang_r = rows[:, None] * inv_freq
    ang_c = cols[:, None] * inv_freq
    cos = jnp.concatenate([jnp.cos(ang_r)] * 2 + [jnp.cos(ang_c)] * 2, axis=1)
    sin = jnp.concatenate([-jnp.sin(ang_r), jnp.sin(ang_r), -jnp.sin(ang_c), jnp.sin(ang_c)], axis=1)
    cos = jnp.concatenate([cos, jnp.ones((c_rows, HEAD_DIM), F32)], axis=0)
    sin = jnp.concatenate([sin, jnp.zeros((c_rows, HEAD_DIM), F32)], axis=0)
    return jnp.concatenate([cos, cos], axis=1), jnp.concatenate([sin, sin], axis=1)


def _ff_pad_cols(a):
    r = a.shape[0]
    a = jnp.pad(a.reshape(r, N_DEV, FF_SHARD), ((0, 0), (0, 0), (0, FF_SHARD_PAD - FF_SHARD)))
    return a.reshape(r, 2 * FF)


def _ff_unpad_cols(a):
    r = a.shape[0]
    return a.reshape(r, N_DEV, FF_SHARD_PAD)[:, :, :FF_SHARD].reshape(r, 2 * D_FF)


def _ff_pad_rows(a):
    c = a.shape[1]
    a = jnp.pad(a.reshape(N_DEV // 2, FF_SHARD, c), ((0, 0), (0, FF_SHARD_PAD - FF_SHARD), (0, 0)))
    return a.reshape(FF, c)


def _ff_unpad_rows(a):
    c = a.shape[1]
    return a.reshape(N_DEV // 2, FF_SHARD_PAD, c)[:, :FF_SHARD].reshape(D_FF, c)


BIG = (("w_in", (D, IN_COLS // N_DEV)), ("w_branch_a", (Q_W, D // N_DEV)), ("w_branch_b", (Q_W, D // N_DEV)),
       ("w_out", (D // N_DEV, D)), ("w_up", (D, FF_SHARD_PAD)), ("w_down", (D_FF // N_DEV, D)))
BIG_SIZES = tuple(int(np.prod(s)) for _, s in BIG)
BIG_ROWS = sum(BIG_SIZES) // LANES


def _pack_big(parts):
    return jnp.concatenate([p.reshape(-1) for p in parts]).reshape(BIG_ROWS, LANES)


def _unpack_big(flat):
    lead = flat.shape[:-2]
    f = flat.reshape(*lead, BIG_ROWS * LANES)
    out, off = [], 0
    for (_, shp), sz in zip(BIG, BIG_SIZES, strict=True):
        out.append(f[..., off:off + sz].reshape(*lead, *shp))
        off += sz
    return out


def _cols_to_full(g):
    return jnp.transpose(g, (1, 0, 2)).reshape(g.shape[1], -1)


def _full_to_cols(a):
    r, c = a.shape
    return jnp.transpose(a.reshape(r, N_DEV, c // N_DEV), (1, 0, 2))


SMALL = (("c_ctx", D), ("b_mod", N_MOD * D), ("b_in", IN_COLS), ("attn_sink", N_HEADS), ("q_norm_g", HEAD_DIM),
         ("k_norm_g", HEAD_DIM), ("ln1_g", D), ("ln1_b", D), ("conv_w", 3 * 2 * D_FF // N_DEV), ("conv_b", 2 * D_FF),
         ("ln2_g", D), ("ln2_b", D))
SMALL_TOTAL = sum(n for _, n in SMALL)
SMALL_ROWS = -(-SMALL_TOTAL // (8 * LANES)) * 8


def _pack_small(parts):
    flat = jnp.concatenate([p.reshape(-1).astype(F32) for p in parts])
    return jnp.pad(flat, (0, SMALL_ROWS * LANES - flat.shape[0])).reshape(SMALL_ROWS, LANES)


def _unpack_small(packed):
    f = packed.reshape(-1)
    out, off = {}, 0
    for name, n in SMALL:
        out[name] = f[off:off + n]
        off += n
    return out


RED = (("c_ctx", D), ("b_in", IN_COLS), ("attn_sink", N_HEADS), ("q_norm_g", HEAD_DIM), ("k_norm_g", HEAD_DIM),
       ("ln1_g", D), ("ln1_b", D), ("conv_w", 3 * 2 * FF), ("conv_b", 2 * FF), ("ln2_g", D), ("ln2_b", D))
RED_TOTAL = sum(n for _, n in RED)
RED_ROWS = -(-RED_TOTAL // (8 * LANES)) * 8


def sum8(name, g):
    _, r, w = g.shape

    def kern(g_ref, o_ref):
        acc = g_ref[0]
        for k in range(1, N_DEV):
            acc = acc + g_ref[k]
        o_ref[...] = acc

    return pl.pallas_call(kern, name=name, out_shape=SDS((r, w), F32))(g)


def _local_step(x, ctx, target, modrows, weights, small):
    s_rows, c_rows = x.shape[0], ctx.shape[0]
    n = s_rows + c_rows
    nl = s_rows // TM
    w_in, wba, wbb, w_out, w_up, w_down = weights
    f = lambda a: a.reshape(1, -1).astype(F32)
    b_in, ln1_g, ln1_b, ln2_g, ln2_b, conv_b = (f(small[k]) for k in ("b_in", "ln1_g", "ln1_b", "ln2_g", "ln2_b", "conv_b"))
    conv_w8 = jnp.pad(small["conv_w_full"], ((0, 5), (0, 0)))
    qg = jnp.tile(small["q_norm_g"].reshape(1, HEAD_DIM), (1, N_HEADS))
    kg = jnp.tile(small["k_norm_g"].reshape(1, HEAD_DIM), (1, N_HEADS))
    sink_rep = jnp.repeat(small["attn_sink"].reshape(N_KV, GROUPS), TW, axis=1)
    sinkrow = sink_rep.reshape(N_KV, 1, WR)
    sinkcol = jnp.broadcast_to(sink_rep[:, :, None], (N_KV, WR, LANES))
    bd = jnp.kron(jnp.eye(N_HEADS, dtype=F32), jnp.ones((HEAD_DIM, HEAD_DIM), F32)).astype(BF16)
    cos, sin = _rope_tables(s_rows, c_rows)
    w_ext = _extend_cols(w_in)
    b_ext = _extend_cols(b_in)
    xa = jnp.concatenate([x, ctx], axis=0)

    hb, qa, kat, vat, qb, kbt, vbt, tq, rq, tk, rk, gl = inproj_fwd(xa, cos, sin, modrows, w_ext, b_ext, qg, kg, bd, nl)
    compact_t = lambda t: jnp.stack([t[:, h * KV_W:h * KV_W + HEAD_DIM].T for h in range(N_KV)])
    oa, lse_a = win_fwd(qa, kat, compact_t(vat), sinkrow, s_rows, c_rows)
    vb_t = compact_t(vbt)
    ob, lse_b = glob_fwd(qb, kbt, vb_t, s_rows)
    tqb = _pick(s_rows, GLOB_BWD_TQ)
    lse_b = lse_b.reshape(N_KV, s_rows // tqb, GROUPS * tqb, 1)
    ya, yb, mrg, y, xhat1, rstd1 = merge_fwd(oa, ob, gl, x, modrows, wba, wbb, w_out, s_rows)
    h2, u0 = ffn_up_fwd(xhat1, modrows, ln1_g, ln1_b, w_up, s_rows)
    a = conv_swiglu_fwd(u0, conv_w8, conv_b, s_rows)
    (dr2, dy2), (loss, dln2_g, dln2_b, dgate2) = ffn_down_loss(a, xhat1, target, modrows, ln1_g, ln1_b, ln2_g, ln2_b, w_down, s_rows)

    da = ffn_down_bwd(dy2, w_down.T, s_rows)
    dw_down = mm_tn("dw_down", a, dy2, s_rows)
    du0, (dconv_b, dcw0, dcw1, dcw2) = swiglu_conv_bwd(u0, da, conv_w8, conv_b, s_rows)
    dw_up = mm_tn("dw_up", h2, du0, s_rows)
    (dy, dxp), (dscale2, dshift2, dln1_g, dln1_b, dgate1) = ffn_up_ln1_bwd(du0, dr2, xhat1, y, rstd1, modrows, ln1_g, ln1_b, w_up.T, s_rows)
    dya, dyb, dgl, doa, dob = merge_bwd(dy, ya, yb, gl, w_out.T, wba.T, wbb.T, s_rows)
    dw_out = mm_tn("dw_out", mrg, dy, s_rows)
    dwba = mm_tn("dw_branch_a", oa, dya, s_rows)
    dwbb = mm_tn("dw_branch_b", ob, dyb, s_rows)

    head_rows = lambda t: t.reshape(N_KV * HEAD_DIM, t.shape[-1])
    place = np.zeros((N_KV * HEAD_DIM, Q_W), np.float32)
    for h in range(N_KV):
        place[h * HEAD_DIM + np.arange(HEAD_DIM), h * KV_W + np.arange(HEAD_DIM)] = 1.0
    place = jnp.asarray(place, BF16)
    dqa, dka_t, dva_t, dsk = win_bwd(qa, qa[:s_rows].T, kat, vat, sinkcol, oa, doa, doa.astype(MXU_DTYPE).T, lse_a,
                                     s_rows, c_rows)
    dka_t, dva_t = head_rows(dka_t[:, :, WIN:]), head_rows(dva_t[:, :, WIN:])
    delta_b = attn_delta(ob, dob, s_rows)
    qb_t = qb[:s_rows].T
    dob_t = dob.astype(MXU_DTYPE).T
    heads = [glob_bwd(qb, qb_t, kbt, vbt, dob, dob_t, lse_b[h], delta_b[h], h, s_rows) for h in range(N_KV)]
    dqb = [hd[0] for hd in heads]
    dkb_t = jnp.concatenate([hd[1] for hd in heads], axis=0)
    dvb_t = jnp.concatenate([hd[2] for hd in heads], axis=0)
    dproj, (db_ext, dqg, dkg) = qk_bwd(dqa, dka_t, dva_t, dqb, dkb_t, dvb_t, dgl, tq, rq, tk, rk, cos, sin, qg, kg, bd,
                                       place, nl, n)
    w_ext_t = w_ext.T
    (grad_x,), (dscale1, dshift1) = inproj_bwd("inproj_bwd", dproj, xa, dxp, modrows, w_ext_t, ntiles=nl, tile_off=0,
                                               is_ctx=False, out_rows=s_rows)
    _, (dscale_c, dshift_c) = inproj_bwd("inproj_bwd_ctx", dproj, xa, None, modrows, w_ext_t, ntiles=c_rows // TM,
                                         tile_off=nl, is_ctx=True, out_rows=0)
    dw_in = _fold_cols(mm_tn("dw_in", hb, dproj, n))

    dmod = jnp.concatenate([dshift1, dscale1, dgate1, dshift2, dscale2, dgate2], axis=1)
    dmod_c = jnp.concatenate([dshift_c, dscale_c, jnp.zeros((1, (N_MOD - 2) * D), F32)], axis=1)
    fold_g = lambda t: t.reshape(N_HEADS, HEAD_DIM).sum(axis=0)
    red = {
        "b_in": _fold_cols(db_ext), "attn_sink": dsk.reshape(N_HEADS, TW).sum(axis=1), "q_norm_g": fold_g(dqg),
        "k_norm_g": fold_g(dkg), "ln1_g": dln1_g, "ln1_b": dln1_b, "conv_w": jnp.concatenate([dcw0, dcw1, dcw2], axis=0),
        "conv_b": dconv_b, "ln2_g": dln2_g, "ln2_b": dln2_b,
    }
    return loss[0, 0], grad_x, (dw_in, dwba, dwbb, dw_out, dw_up, dw_down), dmod, dmod_c, red


def kernel(x, c, ctx, c_ctx, w_mod, b_mod, w_in, b_in, attn_sink, q_norm_g, k_norm_g, w_branch_a, w_branch_b, w_out, ln1_g, ln1_b, w_up, conv_w, conv_b, w_down, ln2_g, ln2_b, loss_target, m_c_ctx, m_w_mod, m_b_mod, m_w_in, m_b_in, m_attn_sink, m_q_norm_g, m_k_norm_g, m_w_branch_a, m_w_branch_b, m_w_out, m_ln1_g, m_ln1_b, m_w_up, m_conv_w, m_conv_b, m_w_down, m_ln2_g, m_ln2_b, v_c_ctx, v_w_mod, v_b_mod, v_w_in, v_b_in, v_attn_sink, v_q_norm_g, v_k_norm_g, v_w_branch_a, v_w_branch_b, v_w_out, v_ln1_g, v_ln1_b, v_w_up, v_conv_w, v_conv_b, v_w_down, v_ln2_g, v_ln2_b):
    ax, ay, ac = (lax.axis_index(a) for a in AXES)
    me = 4 * ax + 2 * ay + ac
    chip = 2 * ax + ay
    mod_w = N_MOD * D // N_DEV
    params = dict(c_ctx=c_ctx, w_mod=w_mod, b_mod=b_mod, w_in=w_in, b_in=b_in, attn_sink=attn_sink, q_norm_g=q_norm_g,
                  k_norm_g=k_norm_g, w_branch_a=w_branch_a, w_branch_b=w_branch_b, w_out=w_out, ln1_g=ln1_g, ln1_b=ln1_b,
                  w_up=w_up, conv_w=conv_w, conv_b=conv_b, w_down=w_down, ln2_g=ln2_g, ln2_b=ln2_b)
    mom_m = dict(c_ctx=m_c_ctx, w_mod=m_w_mod, b_mod=m_b_mod, w_in=m_w_in, b_in=m_b_in, attn_sink=m_attn_sink,
                 q_norm_g=m_q_norm_g, k_norm_g=m_k_norm_g, w_branch_a=m_w_branch_a, w_branch_b=m_w_branch_b, w_out=m_w_out,
                 ln1_g=m_ln1_g, ln1_b=m_ln1_b, w_up=m_w_up, conv_w=m_conv_w, conv_b=m_conv_b, w_down=m_w_down,
                 ln2_g=m_ln2_g, ln2_b=m_ln2_b)
    mom_v = dict(c_ctx=v_c_ctx, w_mod=v_w_mod, b_mod=v_b_mod, w_in=v_w_in, b_in=v_b_in, attn_sink=v_attn_sink,
                 q_norm_g=v_q_norm_g, k_norm_g=v_k_norm_g, w_branch_a=v_w_branch_a, w_branch_b=v_w_branch_b, w_out=v_w_out,
                 ln1_g=v_ln1_g, ln1_b=v_ln1_b, w_up=v_w_up, conv_w=v_conv_w, conv_b=v_conv_b, w_down=v_w_down,
                 ln2_g=v_ln2_g, ln2_b=v_ln2_b)
    big_names = [nm for nm, _ in BIG]

    def shard(tree, nm):
        t = tree[nm][0]
        return jnp.pad(t, ((0, 0), (0, FF_SHARD_PAD - FF_SHARD))) if nm == "w_up" else t

    wg = all_gather("ag_weights", _pack_big([shard(params, nm).astype(MXU_DTYPE) for nm in big_names]))
    g_in, g_ba, g_bb, g_out, g_up, g_down = _unpack_big(wg)
    weights = (_cols_to_full(g_in), _cols_to_full(g_ba), _cols_to_full(g_bb), g_out.reshape(D, D), _cols_to_full(g_up),
               _ff_pad_rows(g_down.reshape(D_FF, D)))

    c_all = all_gather("ag_c", c.reshape(8, LANES)).reshape(N_DEV, D)
    cs = jnp.concatenate([c_all, c_ctx.reshape(1, D), jnp.zeros((7, D), F32)], axis=0)
    w_mod_sh = w_mod[0]
    b_mod_sh = lax.dynamic_slice(b_mod, (0, me * mod_w), (1, mod_w))
    mod_part = mod_fwd(cs, w_mod_sh, b_mod_sh)
    mg = all_gather("ag_mod", mod_part.reshape(16 * mod_w // LANES, LANES)).reshape(N_DEV, 16, mod_w)
    mod = lax.dynamic_index_in_dim(mg, me, axis=1, keepdims=False).reshape(N_MOD, D)
    mod_c = mg[:, 8, :].reshape(N_MOD, D)
    modrows = jnp.stack([mod[0], mod[1], mod_c[0], mod_c[1], mod[2], mod[3], mod[4], mod[5]], axis=0)

    conv_w_full = all_gather("ag_conv_w", jnp.pad(conv_w[0], ((0, 5), (0, FF_SHARD_PAD - FF_SHARD))))
    conv_w_full = _cols_to_full(conv_w_full[:, :3, :])
    small = dict(b_in=b_in, ln1_g=ln1_g, ln1_b=ln1_b, ln2_g=ln2_g, ln2_b=ln2_b, conv_b=_ff_pad_cols(conv_b),
                 conv_w_full=conv_w_full, q_norm_g=q_norm_g, k_norm_g=k_norm_g, attn_sink=attn_sink)
    loss, grad_x, big_grads, dmod, dmod_c, red = _local_step(x[0], ctx[0], loss_target[0], modrows, weights, small)
    loss = lax.psum(loss, AXES)

    dm = all_gather("ag_dmod", jnp.concatenate([dmod, dmod_c], axis=0).reshape(2 * N_MOD * D // LANES, LANES))
    dm = dm.reshape(N_DEV, 2, N_MOD * D)
    dm_all = jnp.concatenate([dm[:, 0], dm[:, 1]], axis=0)
    dm_sh = lax.dynamic_slice(dm_all, (0, me * mod_w), (16, mod_w))
    dw_mod, dcc, db_mod = mod_bwd(cs, w_mod_sh, dm_sh, dm_all)
    red["c_ctx"] = dcc[8]

    red_vec = jnp.concatenate([red[nm].reshape(-1) for nm, _ in RED])
    red_vec = jnp.pad(red_vec, (0, RED_ROWS * LANES - RED_TOTAL)).reshape(RED_ROWS, LANES)
    red_sum = sum8("sum_small", all_gather("ag_small", red_vec)).reshape(-1)
    gsm, off = {}, 0
    for nm, k in RED:
        gsm[nm] = red_sum[off:off + k]
        off += k
    gsm["b_mod"] = db_mod.reshape(-1)
    gsm["conv_b"] = _ff_unpad_cols(gsm["conv_b"].reshape(1, 2 * FF))
    gsm["conv_w"] = lax.dynamic_slice(gsm["conv_w"].reshape(3, 2 * FF), (0, me * FF_SHARD_PAD), (3, FF_SHARD_PAD))[:, :FF_SHARD]
    sm_names = [nm for nm, _ in SMALL]
    gs, ds, ms, vs = adamw("adamw_small", _pack_small([params[nm] for nm in sm_names]),
                           _pack_small([mom_m[nm] for nm in sm_names]), _pack_small([mom_v[nm] for nm in sm_names]),
                           [_pack_small([gsm[nm] for nm in sm_names])])
    sm_out = [_unpack_small(t) for t in (gs, ds, ms, vs)]

    dw_in, dwba, dwbb, dw_out, dw_up, dw_down = big_grads
    slabs = jnp.concatenate([t.reshape(N_DEV, -1) for t in (
        _full_to_cols(dw_in), _full_to_cols(dwba), _full_to_cols(dwbb), dw_out, _full_to_cols(dw_up),
        _ff_unpad_rows(dw_down))], axis=1)
    by_core = slabs.reshape(4, 2, BIG_ROWS, LANES)
    keep = lax.dynamic_index_in_dim(by_core, ac, axis=1, keepdims=False)
    give = lax.dynamic_index_in_dim(by_core, 1 - ac, axis=1, keepdims=False)
    got = exchange("rs_sibling", give.reshape(1, 4 * BIG_ROWS, LANES), to_chips=False).reshape(4, BIG_ROWS, LANES)
    pair = add2("rs_pair_sum", keep, got)
    outbox = jnp.stack([lax.dynamic_index_in_dim(pair, jnp.bitwise_xor(chip, m), axis=0, keepdims=False) for m in (1, 2, 3)])
    inbox = exchange("rs_chips", outbox.astype(MXU_DTYPE), to_chips=True)
    mine = lax.dynamic_index_in_dim(pair, chip, axis=0, keepdims=False)
    gb, db, mb, vb = adamw("adamw_big", _pack_big([shard(params, nm) for nm in big_names]),
                           _pack_big([shard(mom_m, nm) for nm in big_names]), _pack_big([shard(mom_v, nm) for nm in big_names]),
                           [mine, inbox[0], inbox[1], inbox[2]])
    big_out = [dict(zip(big_names, _unpack_big(t), strict=True)) for t in (gb, db, mb, vb)]
    for out in big_out:
        out["w_up"] = out["w_up"][:, :FF_SHARD]
    gm, dmo, mmo, vmo = adamw("adamw_mod", w_mod[0], m_w_mod[0], v_w_mod[0], [dw_mod])
    mod_out = (gm, dmo, mmo, vmo)

    order = ["c_ctx", "w_mod", "b_mod", "w_in", "b_in", "attn_sink", "q_norm_g", "k_norm_g", "w_branch_a", "w_branch_b",
             "w_out", "ln1_g", "ln1_b", "w_up", "conv_w", "conv_b", "w_down", "ln2_g", "ln2_b"]
    results = [loss, grad_x[None]]
    for kind in range(4):
        for nm in order:
            if nm == "w_mod":
                val = mod_out[kind]
            elif nm in big_out[kind]:
                val = big_out[kind][nm]
            else:
                val = sm_out[kind][nm]
            results.append(val.reshape(params[nm].shape))
    return tuple(results)
```

```python
import functools

import jax
import jax.numpy as jnp
import numpy as np
from jax import lax
from jax.experimental import pallas as pl
from jax.experimental.pallas import tpu as pltpu

F32 = jnp.float32
BF16 = jnp.bfloat16
MXU_DTYPE = BF16

AXES = ("x", "y", "c")
N_DEV = 8
D = 1024
HEAD_DIM = 64
N_HEADS = 8
N_KV = 2
GROUPS = 4
KV_W = GROUPS * HEAD_DIM
Q_W = N_HEADS * HEAD_DIM
GRID_W = 64
WIN = 128
ROPE_THETA = 10000.0
D_FF = 2816
FF_SHARD = 2 * D_FF // N_DEV
FF_SHARD_PAD = 768
FF = N_DEV // 2 * FF_SHARD_PAD
LN_EPS = 1e-5
QK_EPS = 1e-6
N_MOD = 6
ALPHA = 2.0 ** 0.25
Q_SCALE = HEAD_DIM ** -0.5
IN_COLS = 3584
OFF_KA, OFF_VA, OFF_QB, OFF_KB, OFF_VB, OFF_GA = 512, 640, 768, 1280, 1408, 1536
EXT_COLS = 6 * Q_W + 2 * D
X_QA, X_KA, X_VA, X_QB, X_KB, X_VB, X_GL = 0, 512, 1024, 1536, 2048, 2560, 3072
ADAM_LR, ADAM_B1, ADAM_B2, ADAM_EPS, ADAM_WD, ADAM_STEP = 0.001, 0.9, 0.999, 1e-08, 0.01, 10
LANES = 128
TM = 256
VMEM_LIMIT = 56 * 1024 * 1024
ELEMENTWISE_BLOCK_BYTES = 1 << 20
ELEMENTWISE_ROWS = (1824, 1408, 1024, 512, 256, 128, 64, 32, 16, 8)

ANY = pl.BlockSpec(memory_space=pl.ANY)
SDS = jax.ShapeDtypeStruct


def _pick(n, candidates):
    for t in candidates:
        if n % t == 0:
            return t
    raise ValueError(f"no tile for {n}")


def _full(a):
    nd = a.ndim
    return pl.BlockSpec(a.shape, lambda *_: (0,) * nd)


def _rows(tm, w, fn=lambda t: t):
    return pl.BlockSpec((tm, w), lambda i: (fn(i), 0))


def _dot(a, b):
    return jnp.dot(a.astype(MXU_DTYPE), b.astype(MXU_DTYPE), preferred_element_type=F32)


def _dot_nt(a, b):
    return lax.dot_general(a.astype(MXU_DTYPE), b.astype(MXU_DTYPE), (((1,), (1,)), ((), ())), preferred_element_type=F32)


def _dot_tn(a, b):
    return lax.dot_general(a.astype(MXU_DTYPE), b.astype(MXU_DTYPE), (((0,), (0,)), ((), ())), preferred_element_type=F32)


def _cparams(sem):
    return pltpu.CompilerParams(dimension_semantics=sem, vmem_limit_bytes=VMEM_LIMIT)


def all_gather(name, v):
    r, w = v.shape

    def body(x_ref, out_ref, send_sems, recv_sems, local_sem):
        x, y, c = (lax.axis_index(a) for a in AXES)
        me, sibling = (x, y, c), (x, y, 1 - c)
        chips = [(1 - x, y), (x, 1 - y), (1 - x, 1 - y)]

        def rows(px, py, pc):
            return out_ref.at[4 * px + 2 * py + pc]

        def copy(k, block, to, src=None):
            return pltpu.make_async_remote_copy(
                src_ref=rows(*block) if src is None else src, dst_ref=rows(*block),
                send_sem=send_sems.at[k], recv_sem=recv_sems.at[k],
                device_id=to, device_id_type=pl.DeviceIdType.MESH)

        mine = pltpu.make_async_copy(x_ref, rows(*me), local_sem)
        mine.start()
        first = [copy(0, me, sibling, src=x_ref)]
        first += [copy(1 + j, me, (*chip, c), src=x_ref) for j, chip in enumerate(chips)]
        for cp in first:
            cp.start()
        passed = [copy(4 + j, (*chip, c), sibling) for j, chip in enumerate(chips)]
        for j, chip in enumerate(chips):
            copy(1 + j, (*chip, c), me).wait_recv()
            passed[j].start()
        copy(0, sibling, me).wait_recv()
        for j, chip in enumerate(chips):
            copy(4 + j, (*chip, 1 - c), me).wait_recv()
        for cp in first + passed:
            cp.wait_send()
        mine.wait()

    return pl.pallas_call(
        body, name=name, out_shape=SDS((N_DEV, r, w), v.dtype), in_specs=[ANY], out_specs=ANY,
        scratch_shapes=[pltpu.SemaphoreType.DMA((7,)), pltpu.SemaphoreType.DMA((7,)), pltpu.SemaphoreType.DMA],
    )(v)


def exchange(name, outbox, to_chips):
    k = outbox.shape[0]
    assert k == (3 if to_chips else 1)

    def body(out_ref, in_ref, send_sems, recv_sems):
        x, y, c = (lax.axis_index(a) for a in AXES)
        peers = [(x, 1 - y, c), (1 - x, y, c), (1 - x, 1 - y, c)] if to_chips else [(x, y, 1 - c)]
        copies = [
            pltpu.make_async_remote_copy(
                src_ref=out_ref.at[m], dst_ref=in_ref.at[m], send_sem=send_sems.at[m], recv_sem=recv_sems.at[m],
                device_id=peer, device_id_type=pl.DeviceIdType.MESH)
            for m, peer in enumerate(peers)
        ]
        for cp in copies:
            cp.start()
        for cp in copies:
            cp.wait_recv()
        for cp in copies:
            cp.wait_send()

    return pl.pallas_call(
        body, name=name, out_shape=SDS(outbox.shape, outbox.dtype), in_specs=[ANY], out_specs=ANY,
        scratch_shapes=[pltpu.SemaphoreType.DMA((k,)), pltpu.SemaphoreType.DMA((k,))],
    )(outbox)


def rowwise(name, body, *, ntiles, tile_off=0, tiled, full, outs, accs=()):
    nt, nf, no = len(tiled), len(full), len(outs)

    def kern(*refs):
        i = pl.program_id(0)
        out_vals, incs = body(i + tile_off, refs[:nt], refs[nt:nt + nf])
        for r, v in zip(refs[nt + nf:nt + nf + no], out_vals, strict=True):
            r[...] = v.astype(r.dtype)
        acc_refs = refs[nt + nf + no:]

        @pl.when(i == 0)
        def _():
            for r in acc_refs:
                r[...] = jnp.zeros_like(r)

        for r, v in zip(acc_refs, incs, strict=True):
            r[...] += v

    res = pl.pallas_call(
        kern, name=name, grid=(ntiles,),
        in_specs=[s for _, s in tiled] + [_full(a) for a in full],
        out_specs=[s for _, _, s in outs] + [pl.BlockSpec(s, lambda i, n=len(s): (0,) * n) for s in accs],
        out_shape=[SDS(s, d) for s, d, _ in outs] + [SDS(s, F32) for s in accs],
        compiler_params=_cparams(("arbitrary",) if accs else ("parallel",)),
    )(*[a for a, _ in tiled], *full)
    return res[:no], res[no:]


def mm_tn(name, a, b, rows):
    ka, nb = a.shape[1], b.shape[1]
    tr = _pick(rows, (1280, 1024, 768, 512, 256))
    tn = _pick(nb, (512, 256, 128))

    def kern(a_ref, b_ref, o_ref):
        @pl.when(pl.program_id(1) == 0)
        def _():
            o_ref[...] = jnp.zeros_like(o_ref)

        o_ref[...] += _dot_tn(a_ref[...], b_ref[...])

    return pl.pallas_call(
        kern, name=name, grid=(nb // tn, rows // tr),
        in_specs=[pl.BlockSpec((tr, ka), lambda n, r: (r, 0)), pl.BlockSpec((tr, tn), lambda n, r: (r, n))],
        out_specs=pl.BlockSpec((ka, tn), lambda n, r: (0, n)), out_shape=SDS((ka, nb), F32),
        compiler_params=_cparams(("parallel", "arbitrary")),
    )(a, b)


def _swap16(t):
    w = t.shape[1]
    lane = lax.broadcasted_iota(jnp.int32, t.shape, 1)
    return jnp.where((lane & 16) == 0, pltpu.roll(t, w - 16, 1), pltpu.roll(t, 16, 1))


def _rope(t, cos, sin):
    return t * cos + _swap16(t) * sin


def _rope_t(d, cos, sin):
    return d * cos - _swap16(d) * sin


def _seg_sum64(a, bd_ref):
    bd = bd_ref[...]
    hi = a.astype(BF16)
    lo = (a - hi.astype(F32)).astype(BF16)
    return jnp.dot(hi, bd, preferred_element_type=F32) + jnp.dot(lo, bd, preferred_element_type=F32)


def _lane_block(shape):
    return jnp.right_shift(lax.broadcasted_iota(jnp.int32, shape, 1), 6)


def _stack_groups(t, dtype):
    blk = _lane_block(t.shape)
    return jnp.concatenate([jnp.where(blk == g, t, jnp.zeros_like(t)).astype(dtype) for g in range(GROUPS)], axis=0)


def _fold_groups(ts, tq):
    blk = _lane_block((tq, KV_W))
    out = jnp.zeros((tq, KV_W), ts.dtype)
    for g in range(GROUPS):
        out = jnp.where(blk == g, ts[g * tq:(g + 1) * tq], out)
    return out


def _row_to_col(row):
    hi = row.astype(BF16)
    r1 = row - hi.astype(F32)
    mid = r1.astype(BF16)
    lo = (r1 - mid.astype(F32)).astype(BF16)
    ones = jnp.ones((8, LANES), BF16)
    pad = jnp.zeros((7, row.shape[1]), BF16)
    acc = jnp.zeros((row.shape[1], LANES), F32)
    for term in (hi, mid, lo):
        acc = acc + lax.dot_general(jnp.concatenate([term, pad], axis=0), ones, (((0,), (0,)), ((), ())),
                                    preferred_element_type=F32)
    return acc[:, 0:1]


def _stack_tiles(t, dtype):
    return jnp.concatenate([_stack_groups(t[a:a + TM], dtype) for a in range(0, t.shape[0], TM)], axis=0)


def _fold_tiles(ts, tq):
    return jnp.concatenate([_fold_groups(ts[GROUPS * a:GROUPS * (a + TM)], TM) for a in range(0, tq, TM)], axis=0)


def _compact_tiles_t(tt, dtype):
    return jnp.concatenate([tt[g * HEAD_DIM:(g + 1) * HEAD_DIM, a:a + TM] for a in range(0, tt.shape[1], TM)
                            for g in range(GROUPS)], axis=1).astype(dtype)


def _layer_norm_bwd(dxh, xhat, rstd):
    m1 = jnp.mean(dxh, axis=1, keepdims=True)
    m2 = jnp.mean(dxh * xhat, axis=1, keepdims=True)
    return rstd * (dxh - m1 - xhat * m2)


def _colsum(a):
    return jnp.sum(a, axis=0, keepdims=True)


def _shifted_rows(t, prev_row, next_row):
    n = t.shape[0]
    row = lax.broadcasted_iota(jnp.int32, t.shape, 0)
    up = jnp.where(row == 0, prev_row, pltpu.roll(t, 1, 0))
    dn = jnp.where(row == n - 1, next_row, pltpu.roll(t, n - 1, 0))
    return up, dn


def mod_fwd(cs, w_sh, b_sh):
    def kern(c_ref, w_ref, b_ref, o_ref):
        o_ref[...] = _dot(jax.nn.silu(c_ref[...]), w_ref[...]) + b_ref[...]

    return pl.pallas_call(kern, name="mod_fwd", out_shape=SDS((16, w_sh.shape[1]), F32),
                          compiler_params=pltpu.CompilerParams(vmem_limit_bytes=VMEM_LIMIT))(cs, w_sh, b_sh)


def mod_bwd(cs, w_sh, dm_sh, dm_all):
    hp = lax.Precision.HIGHEST

    def kern(c_ref, w_ref, dm_ref, da_ref, dw_ref, dc_ref, db_ref):
        c = c_ref[...]
        sg = jax.nn.sigmoid(c)
        sc = c * sg
        dm = dm_ref[...]
        dmc = dm_ref[8:9, :]
        for i in range(9, 16):
            dmc = dmc + dm_ref[i:i + 1, :]
        row = lax.broadcasted_iota(jnp.int32, dm.shape, 0)
        a = jnp.where(row < 8, dm, jnp.where(row == 8, dmc, 0.0))
        dw_ref[...] = lax.dot_general(sc, a, (((0,), (0,)), ((), ())), precision=hp, preferred_element_type=F32)
        dsc = lax.dot_general(a, w_ref[...], (((1,), (1,)), ((), ())), precision=hp, preferred_element_type=F32)
        dc_ref[...] = dsc * (sg * (1.0 + c * (1.0 - sg)))
        db = da_ref[0:1, :]
        for i in range(1, 16):
            db = db + da_ref[i:i + 1, :]
        db_ref[...] = db

    return pl.pallas_call(
        kern, name="mod_bwd",
        out_shape=[SDS(w_sh.shape, F32), SDS((16, D), F32), SDS((1, dm_all.shape[1]), F32)],
        compiler_params=pltpu.CompilerParams(vmem_limit_bytes=VMEM_LIMIT))(cs, w_sh, dm_sh, dm_all)


M_SHIFT1, M_SCALE1, M_SHIFTC, M_SCALEC, M_GATE1, M_SHIFT2, M_SCALE2, M_GATE2 = range(8)


def _mrow(ref, k):
    return ref[k:k + 1, :]


def inproj_fwd(xa, cos, sin, modrows, w_ext, b_ext, qg, kg, bd, n_lat_tiles):
    n = xa.shape[0]

    def body(t, vals, fr):
        x, cs, sn = (v[...] for v in vals)
        mod, w, b, qg_r, kg_r, bd_r = fr
        is_ctx = t >= n_lat_tiles
        shift = jnp.where(is_ctx, _mrow(mod, M_SHIFTC), _mrow(mod, M_SHIFT1))
        scale = jnp.where(is_ctx, _mrow(mod, M_SCALEC), _mrow(mod, M_SCALE1))
        hb = (x * (1.0 + scale) + shift).astype(MXU_DTYPE)
        proj = jnp.dot(hb, w[...], preferred_element_type=F32) + b[...]
        cos4 = jnp.concatenate([cs] * 4, axis=1)
        sin4 = jnp.concatenate([sn] * 4, axis=1)
        qa = _rope(proj[:, X_QA:X_QA + Q_W], cos4, sin4) * Q_SCALE
        ka = _rope(proj[:, X_KA:X_KA + Q_W], cos4, sin4)
        va = proj[:, X_VA:X_VA + Q_W]
        tq = proj[:, X_QB:X_QB + Q_W]
        rq = lax.rsqrt(_seg_sum64(tq * tq, bd_r) * (1.0 / HEAD_DIM) + QK_EPS)
        qb = _rope(tq * rq * qg_r[...], cos4, sin4) * Q_SCALE
        tk = proj[:, X_KB:X_KB + Q_W]
        rk = lax.rsqrt(_seg_sum64(tk * tk, bd_r) * (1.0 / HEAD_DIM) + QK_EPS)
        kb = _rope(tk * rk * kg_r[...], cos4, sin4)
        vb = proj[:, X_VB:X_VB + Q_W]
        gl = proj[:, X_GL:]
        return [hb, qa, ka, va, qb, kb, vb, tq, rq, tk, rk, gl], []

    mx = MXU_DTYPE
    outs = [((n, D), mx, _rows(TM, D))] + [((n, Q_W), mx, _rows(TM, Q_W))] * 6 + \
           [((n, Q_W), F32, _rows(TM, Q_W))] * 4 + [((n, 2 * D), F32, _rows(TM, 2 * D))]
    res, _ = rowwise("inproj_fwd", body, ntiles=n // TM,
                     tiled=[(xa, _rows(TM, D)), (cos, _rows(TM, LANES)), (sin, _rows(TM, LANES))],
                     full=[modrows, w_ext, b_ext, qg, kg, bd], outs=outs)
    return res


def merge_fwd(oa, ob, gl, x, modrows, wba, wbb, w_out, s_rows):
    def body(t, vals, fr):
        oa_, ob_, gl_, x_ = (v[...] for v in vals)
        mod, wa, wb, wo = fr
        ya = _dot(oa_, wa[...])
        yb = _dot(ob_, wb[...])
        ga = jax.nn.sigmoid(gl_[:, :D])
        gb = jax.nn.sigmoid(gl_[:, D:])
        mrg = ga * ya + gb * yb
        y = _dot(mrg, wo[...])
        r1 = ALPHA * x_ + _mrow(mod, M_GATE1) * y
        mu = jnp.mean(r1, axis=1, keepdims=True)
        xc = r1 - mu
        var = jnp.mean(xc * xc, axis=1, keepdims=True)
        rstd = lax.rsqrt(var + LN_EPS)
        xhat = xc * rstd
        return [ya, yb, mrg, y, xhat, rstd], []

    outs = [((s_rows, D), F32, _rows(TM, D))] * 2 + [((s_rows, D), MXU_DTYPE, _rows(TM, D))] + \
           [((s_rows, D), F32, _rows(TM, D))] * 2 + [((s_rows, 1), F32, _rows(TM, 1))]
    res, _ = rowwise("merge_fwd", body, ntiles=s_rows // TM,
                     tiled=[(oa, _rows(TM, Q_W)), (ob, _rows(TM, Q_W)), (gl, _rows(TM, 2 * D)), (x, _rows(TM, D))],
                     full=[modrows, wba, wbb, w_out], outs=outs)
    return res


def ffn_up_fwd(xhat1, modrows, ln_g, ln_b, w_up, s_rows):
    def body(t, vals, fr):
        xh = vals[0][...]
        mod, g_r, b_r, w = fr
        x1 = xh * g_r[...] + b_r[...]
        h2 = (x1 * (1.0 + _mrow(mod, M_SCALE2)) + _mrow(mod, M_SHIFT2)).astype(MXU_DTYPE)
        return [h2, jnp.dot(h2, w[...], preferred_element_type=F32)], []

    res, _ = rowwise("ffn_up_fwd", body, ntiles=s_rows // TM, tiled=[(xhat1, _rows(TM, D))],
                     full=[modrows, ln_g, ln_b, w_up],
                     outs=[((s_rows, D), MXU_DTYPE, _rows(TM, D)), ((s_rows, 2 * FF), F32, _rows(TM, 2 * FF))])
    return res


TC = 128


def _halo_specs(tm, w, s_rows):
    per = tm // 8
    last = s_rows // 8 - 1
    return (pl.BlockSpec((8, w), lambda i: (jnp.maximum(i * per - 1, 0), 0)),
            pl.BlockSpec((8, w), lambda i: (jnp.minimum((i + 1) * per, last), 0)))


def _halo_rows(t, ntiles, prev_ref, next_ref):
    prev_row = jnp.where(t == 0, 0.0, prev_ref[7:8, :].astype(F32))
    next_row = jnp.where(t == ntiles - 1, 0.0, next_ref[0:1, :].astype(F32))
    return prev_row, next_row


def conv_swiglu_fwd(u0, conv_w8, conv_b, s_rows):
    w2 = 2 * FF
    nt = s_rows // TC

    def body(t, vals, fr):
        u_ref, pv, nx = vals
        cw, cb = fr
        u = u_ref[...]
        up, dn = _shifted_rows(u, *_halo_rows(t, nt, pv, nx))
        uc = cw[0:1, :] * up + cw[1:2, :] * u + cw[2:3, :] * dn + cb[...]
        gate, val = uc[:, :FF], uc[:, FF:]
        return [gate * jax.nn.sigmoid(gate) * val], []

    hp, hn = _halo_specs(TC, w2, s_rows)
    res, _ = rowwise("conv_swiglu_fwd", body, ntiles=nt,
                     tiled=[(u0, _rows(TC, w2)), (u0, hp), (u0, hn)], full=[conv_w8, conv_b],
                     outs=[((s_rows, FF), MXU_DTYPE, _rows(TC, FF))])
    return res[0]


def ffn_down_loss(a, xhat1, target, modrows, ln1_g, ln1_b, ln2_g, ln2_b, w_down, s_rows):
    def body(t, vals, fr):
        a_, xh1, tgt = (v[...] for v in vals)
        mod, g1, b1, g2, b2, wd = fr
        y2 = jnp.dot(a_, wd[...], preferred_element_type=F32)
        x1 = xh1 * g1[...] + b1[...]
        gate2 = _mrow(mod, M_GATE2)
        r2 = ALPHA * x1 + gate2 * y2
        mu = jnp.mean(r2, axis=1, keepdims=True)
        xc = r2 - mu
        var = jnp.mean(xc * xc, axis=1, keepdims=True)
        rstd = lax.rsqrt(var + LN_EPS)
        xhat = xc * rstd
        out = xhat * g2[...] + b2[...]
        diff = out - tgt
        loss = 0.5 * jnp.sum(jnp.mean(diff * diff, axis=1, keepdims=True), axis=0, keepdims=True)
        dout = diff * (1.0 / D)
        dr2 = _layer_norm_bwd(dout * g2[...], xhat, rstd)
        incs = [loss, _colsum(dout * xhat), _colsum(dout), _colsum(dr2 * y2)]
        return [dr2, dr2 * gate2], incs

    res, accs = rowwise("ffn_down_loss", body, ntiles=s_rows // TM,
                        tiled=[(a, _rows(TM, FF)), (xhat1, _rows(TM, D)), (target, _rows(TM, D))],
                        full=[modrows, ln1_g, ln1_b, ln2_g, ln2_b, w_down],
                        outs=[((s_rows, D), F32, _rows(TM, D)), ((s_rows, D), MXU_DTYPE, _rows(TM, D))],
                        accs=[(1, 1), (1, D), (1, D), (1, D)])
    return res, accs


def ffn_down_bwd(dy2, w_down_t, s_rows):
    def body(t, vals, fr):
        return [jnp.dot(vals[0][...], fr[0][...], preferred_element_type=F32)], []

    res, _ = rowwise("ffn_down_bwd", body, ntiles=s_rows // TM, tiled=[(dy2, _rows(TM, D))], full=[w_down_t],
                     outs=[((s_rows, FF), F32, _rows(TM, FF))])
    return res[0]


def swiglu_conv_bwd(u0, da, conv_w8, conv_b, s_rows):
    w2 = 2 * FF
    nt = s_rows // TC
    n = TC + 16

    def body(t, vals, fr):
        u_ref, upv, unx, da_ref, apv, anx = vals
        cw, cb = fr
        first, last = t == 0, t == nt - 1
        ue = jnp.concatenate([jnp.where(first, 0.0, upv[...]), u_ref[...], jnp.where(last, 0.0, unx[...])], axis=0)
        ae = jnp.concatenate([jnp.where(first, 0.0, apv[...]), da_ref[...], jnp.where(last, 0.0, anx[...])], axis=0)
        up = pltpu.roll(ue, 1, 0)
        dn = pltpu.roll(ue, n - 1, 0)
        uc = cw[0:1, :] * up + cw[1:2, :] * ue + cw[2:3, :] * dn + cb[...]
        gate, val = uc[:, :FF], uc[:, FF:]
        sg = jax.nn.sigmoid(gate)
        du = jnp.concatenate([ae * val * (sg * (1.0 + gate * (1.0 - sg))), ae * (gate * sg)], axis=1)
        du0 = cw[0:1, :] * pltpu.roll(du, n - 1, 0) + cw[1:2, :] * du + cw[2:3, :] * pltpu.roll(du, 1, 0)
        rows = slice(8, 8 + TC)
        dut = du[rows]
        return [du0[rows]], [_colsum(dut), _colsum(up[rows] * dut), _colsum(ue[rows] * dut), _colsum(dn[rows] * dut)]

    hp, hn = _halo_specs(TC, w2, s_rows)
    ap, an = _halo_specs(TC, FF, s_rows)
    res, accs = rowwise("swiglu_conv_bwd", body, ntiles=nt,
                        tiled=[(u0, _rows(TC, w2)), (u0, hp), (u0, hn), (da, _rows(TC, FF)), (da, ap), (da, an)],
                        full=[conv_w8, conv_b], outs=[((s_rows, w2), MXU_DTYPE, _rows(TC, w2))], accs=[(1, w2)] * 4)
    return res[0], accs


def ffn_up_ln1_bwd(du0, dr2, xhat1, y, rstd1, modrows, ln_g, ln_b, w_up_t, s_rows):
    def body(t, vals, fr):
        du0_, dr2_, xh, y_, rstd = (v[...] for v in vals)
        mod, g_r, b_r, wt = fr
        dh2 = jnp.dot(du0_, wt[...], preferred_element_type=F32)
        x1 = xh * g_r[...] + b_r[...]
        dx1 = ALPHA * dr2_ + dh2 * (1.0 + _mrow(mod, M_SCALE2))
        dr1 = _layer_norm_bwd(dx1 * g_r[...], xh, rstd)
        incs = [_colsum(dh2 * x1), _colsum(dh2), _colsum(dx1 * xh), _colsum(dx1), _colsum(dr1 * y_)]
        return [dr1 * _mrow(mod, M_GATE1), ALPHA * dr1], incs

    res, accs = rowwise("ffn_up_ln1_bwd", body, ntiles=s_rows // TM,
                        tiled=[(du0, _rows(TM, 2 * FF)), (dr2, _rows(TM, D)), (xhat1, _rows(TM, D)), (y, _rows(TM, D)),
                               (rstd1, _rows(TM, 1))],
                        full=[modrows, ln_g, ln_b, w_up_t],
                        outs=[((s_rows, D), MXU_DTYPE, _rows(TM, D)), ((s_rows, D), F32, _rows(TM, D))],
                        accs=[(1, D)] * 5)
    return res, accs


def merge_bwd(dy, ya, yb, gl, w_out_t, wba_t, wbb_t, s_rows):
    def body(t, vals, fr):
        dy_, ya_, yb_, gl_ = (v[...] for v in vals)
        wot, wat, wbt = fr
        dmrg = jnp.dot(dy_, wot[...], preferred_element_type=F32)
        ga = jax.nn.sigmoid(gl_[:, :D])
        gb = jax.nn.sigmoid(gl_[:, D:])
        dya = dmrg * ga
        dyb = dmrg * gb
        dgl = jnp.concatenate([dmrg * ya_ * ga * (1.0 - ga), dmrg * yb_ * gb * (1.0 - gb)], axis=1)
        return [dya, dyb, dgl, _dot(dya, wat[...]), _dot(dyb, wbt[...])], []

    mx = MXU_DTYPE
    res, _ = rowwise("merge_bwd", body, ntiles=s_rows // TM,
                     tiled=[(dy, _rows(TM, D)), (ya, _rows(TM, D)), (yb, _rows(TM, D)), (gl, _rows(TM, 2 * D))],
                     full=[w_out_t, wba_t, wbb_t],
                     outs=[((s_rows, D), mx, _rows(TM, D))] * 2 + [((s_rows, 2 * D), F32, _rows(TM, 2 * D))] +
                          [((s_rows, Q_W), F32, _rows(TM, Q_W))] * 2)
    return res


def qk_bwd(dqa, dka_t, dva_t, dqb_heads, dkb_t, dvb_t, dgl, tq, rq, tk, rk, cos, sin, qg, kg, bd, place, n_lat_tiles, n):
    def placed(xt_ref, place_ref):
        xt = xt_ref[...]
        hi = xt.astype(BF16)
        r1 = xt - hi.astype(F32)
        mid = r1.astype(BF16)
        lo = (r1 - mid.astype(F32)).astype(BF16)
        pm = place_ref[...]
        return sum(lax.dot_general(term, pm, (((0,), (0,)), ((), ())), preferred_element_type=F32) for term in (hi, mid, lo))

    def body(t, vals, fr):
        dqa_, dgl_, tq_, rq_, tk_, rk_, cs, sn = (v[...] for v in vals[:8])
        qg_r, kg_r, bd_r, pl_r = fr
        dka_, dva_, dkb_, dvb_ = (placed(v, pl_r) for v in vals[8:12])
        dqb_ = jnp.concatenate([v[...] for v in vals[12:]], axis=1)
        is_ctx = t >= n_lat_tiles
        cos4 = jnp.concatenate([cs] * 4, axis=1)
        sin4 = jnp.concatenate([sn] * 4, axis=1)
        zero = jnp.zeros_like(dqa_)
        dpqa = jnp.where(is_ctx, zero, _rope_t(dqa_, cos4, sin4) * Q_SCALE)
        dpka = _rope_t(dka_, cos4, sin4)
        dpva = dva_
        dnq = jnp.where(is_ctx, zero, _rope_t(dqb_, cos4, sin4) * Q_SCALE)
        gq = qg_r[...] * dnq
        dtq = rq_ * gq - tq_ * (rq_ * rq_ * rq_) * (_seg_sum64(gq * tq_, bd_r) * (1.0 / HEAD_DIM))
        dnk = _rope_t(dkb_, cos4, sin4)
        gk = kg_r[...] * dnk
        dtk = rk_ * gk - tk_ * (rk_ * rk_ * rk_) * (_seg_sum64(gk * tk_, bd_r) * (1.0 / HEAD_DIM))
        dgl32 = jnp.where(is_ctx, jnp.zeros_like(dgl_), dgl_)
        dproj = jnp.concatenate([dpqa, dpka, dpva, dtq, dtk, dvb_, dgl32], axis=1)
        return [dproj], [_colsum(dproj), _colsum(dnq * tq_ * rq_), _colsum(dnk * tk_ * rk_)]

    lat = lambda t: jnp.minimum(t, n_lat_tiles - 1)
    qs = _rows(TM, Q_W)
    ts = pl.BlockSpec((N_KV * HEAD_DIM, TM), lambda i: (0, i))
    res, accs = rowwise(
        "qk_bwd", body, ntiles=n // TM,
        tiled=[(dqa, _rows(TM, Q_W, lat)), (dgl, _rows(TM, 2 * D, lat)),
               (tq, qs), (rq, qs), (tk, qs), (rk, qs), (cos, _rows(TM, LANES)), (sin, _rows(TM, LANES)),
               (dka_t, ts), (dva_t, ts), (dkb_t, ts), (dvb_t, ts)] + [(d, _rows(TM, KV_W, lat)) for d in dqb_heads],
        full=[qg, kg, bd, place], outs=[((n, EXT_COLS), MXU_DTYPE, _rows(TM, EXT_COLS))],
        accs=[(1, EXT_COLS), (1, Q_W), (1, Q_W)])
    return res[0], accs


def inproj_bwd(name, dproj, xa, dxp, modrows, w_ext_t, *, ntiles, tile_off, is_ctx, out_rows):
    kc = M_SCALEC if is_ctx else M_SCALE1

    def body(t, vals, fr):
        dp, x_ = vals[0][...], vals[1][...]
        mod, wt = fr
        dh = jnp.dot(dp, wt[...], preferred_element_type=F32)
        incs = [_colsum(dh * x_), _colsum(dh)]
        if is_ctx:
            return [], incs
        return [vals[2][...] + dh * (1.0 + _mrow(mod, kc))], incs

    tiled = [(dproj, _rows(TM, EXT_COLS, lambda i: i + tile_off)), (xa, _rows(TM, D, lambda i: i + tile_off))]
    outs = []
    if not is_ctx:
        tiled.append((dxp, _rows(TM, D)))
        outs = [((out_rows, D), F32, _rows(TM, D))]
    return rowwise(name, body, ntiles=ntiles, tiled=tiled, full=[modrows, w_ext_t], outs=outs, accs=[(1, D)] * 2)


def _attn_semantics():
    return _cparams(("arbitrary", "arbitrary", "arbitrary"))


GLOB_TK = (1280, 1024, 768, 512, 256)
GLOB_TQ = (512, 256)
GLOB_BWD_TQ = GLOB_TQ
KEY_CHUNK = 256


def glob_fwd(q, kt, v_t, s_rows):
    n = kt.shape[0]
    tq = _pick(s_rows, GLOB_TQ)
    tk = _pick(n, GLOB_TK)
    nq, nk = s_rows // tq, n // tk
    r = GROUPS * tq
    nch = tk // KEY_CHUNK

    def produce(qs, k_ref, s_buf, c, mx):
        rows = slice(c * KEY_CHUNK, (c + 1) * KEY_CHUNK)
        sn = _dot_nt(k_ref[rows, :], qs[...])
        s_buf[rows, :] = sn
        return jnp.maximum(mx, jnp.max(sn, axis=0, keepdims=True))

    def kern(q_ref, k0_ref, kn_ref, vt_ref, o_ref, lse_ref, qs, s_buf, mx_buf, m_s, l_s, acc):
        j = pl.program_id(2)

        @pl.when(j == 0)
        def _():
            qs[...] = _stack_tiles(q_ref[...], qs.dtype)
            mx = jnp.full((1, r), -jnp.inf, F32)
            for c in range(nch):
                mx = produce(qs, k0_ref, s_buf, c, mx)
            mx_buf[...] = mx
            m_s[...] = jnp.full_like(m_s, -jnp.inf)
            l_s[...] = jnp.zeros_like(l_s)
            acc[...] = jnp.zeros_like(acc)

        m_prev = m_s[...]
        m_new = jnp.maximum(m_prev, mx_buf[...])
        alpha = jnp.exp(m_prev - m_new)
        a = alpha * acc[...]
        ls = alpha * l_s[...]
        mx = jnp.full((1, r), -jnp.inf, F32)
        for c in range(nch):
            rows = slice(c * KEY_CHUNK, (c + 1) * KEY_CHUNK)
            p = jnp.exp(s_buf[rows, :] - m_new)
            ls = ls + jnp.sum(p, axis=0, keepdims=True)
            a = a + jnp.dot(vt_ref[0, :, rows], p.astype(MXU_DTYPE), preferred_element_type=F32)
            mx = produce(qs, kn_ref, s_buf, c, mx)
        mx_buf[...] = mx
        l_s[...] = ls
        acc[...] = a
        m_s[...] = m_new

        @pl.when(j == nk - 1)
        def _():
            o_t = acc[...] / l_s[...]
            o_ref[...] = jnp.concatenate([_untranspose_groups(o_t[:, GROUPS * a:GROUPS * (a + TM)], TM)
                                          for a in range(0, tq, TM)], axis=0)
            lse_ref[0, 0] = _row_to_col(m_s[...] + jnp.log(l_s[...]))

    kspec = lambda f: pl.BlockSpec((tk, KV_W), lambda h, i, j: (f(j), h))
    return pl.pallas_call(
        kern, name="glob_fwd", grid=(N_KV, nq, nk),
        in_specs=[pl.BlockSpec((tq, KV_W), lambda h, i, j: (i, h)), kspec(lambda j: 0),
                  kspec(lambda j: jnp.minimum(j + 1, nk - 1)), pl.BlockSpec((1, HEAD_DIM, tk), lambda h, i, j: (h, 0, j))],
        out_specs=[pl.BlockSpec((tq, KV_W), lambda h, i, j: (i, h)),
                   pl.BlockSpec((1, 1, r, 1), lambda h, i, j: (h, i, 0, 0))],
        out_shape=[SDS((s_rows, Q_W), F32), SDS((N_KV, nq, r, 1), F32)],
        scratch_shapes=[pltpu.VMEM((r, KV_W), MXU_DTYPE), pltpu.VMEM((tk, r), F32), pltpu.VMEM((1, r), F32),
                        pltpu.VMEM((1, r), F32), pltpu.VMEM((1, r), F32), pltpu.VMEM((HEAD_DIM, r), F32)],
        compiler_params=_attn_semantics(),
    )(q, kt, kt, v_t)


def attn_delta(o, do, s_rows):
    tq = _pick(s_rows, GLOB_BWD_TQ)
    nq = s_rows // tq
    r = GROUPS * tq

    def kern(o_ref, do_ref, d_ref):
        d_ref[0, 0] = jnp.sum(_stack_tiles(do_ref[...], F32) * _stack_tiles(o_ref[...], F32), axis=1, keepdims=True)

    qspec = pl.BlockSpec((tq, KV_W), lambda h, i: (i, h))
    return pl.pallas_call(
        kern, name="attn_delta", grid=(N_KV, nq), in_specs=[qspec, qspec],
        out_specs=pl.BlockSpec((1, 1, r, 1), lambda h, i: (h, i, 0, 0)), out_shape=SDS((N_KV, nq, r, 1), F32),
        compiler_params=_cparams(("parallel", "parallel")),
    )(o, do)


def _compact_t(tt, dtype):
    return jnp.concatenate([tt[g * HEAD_DIM:(g + 1) * HEAD_DIM, :] for g in range(GROUPS)], axis=1).astype(dtype)


def glob_bwd(q, q_t, kt, vt, do, do_t, lse, delta, h, s_rows):
    n = kt.shape[0]
    tq = _pick(s_rows, GLOB_BWD_TQ)
    tk = _pick(n, GLOB_TK)
    nq, nk = s_rows // tq, n // tk
    r = GROUPS * tq
    nch = tk // KEY_CHUNK

    def kern(q_ref, qt_ref, k_ref, v_ref, do_ref, dot_ref, lse_ref, dl_ref, dq_ref, dkt_ref, dvt_ref, p_buf, ds_buf):
        j = pl.program_id(0)
        i = pl.program_id(1)

        @pl.when(i == 0)
        def _():
            dkt_ref[...] = jnp.zeros_like(dkt_ref)
            dvt_ref[...] = jnp.zeros_like(dvt_ref)

        qs = _stack_tiles(q_ref[...], MXU_DTYPE)
        dos = _stack_tiles(do_ref[...], MXU_DTYPE)
        lse_b = jnp.broadcast_to(lse_ref[0, 0], (r, LANES))
        dl_b = jnp.broadcast_to(dl_ref[0, 0], (r, LANES))
        for c in range(nch):
            lo = c * KEY_CHUNK
            sc = _dot_nt(qs, k_ref[lo:lo + KEY_CHUNK, :])
            dpc = _dot_nt(dos, v_ref[lo:lo + KEY_CHUNK, :])
            for t in range(KEY_CHUNK // LANES):
                sl = slice(t * LANES, (t + 1) * LANES)
                pt = jnp.exp(sc[:, sl] - lse_b)
                p_buf[:, lo + t * LANES:lo + (t + 1) * LANES] = pt.astype(p_buf.dtype)
                ds_buf[:, lo + t * LANES:lo + (t + 1) * LANES] = (pt * (dpc[:, sl] - dl_b)).astype(ds_buf.dtype)
        dq_t = _fold_tiles(jnp.dot(ds_buf[...], k_ref[...], preferred_element_type=F32), tq)
        rows = pl.ds(pl.multiple_of(i * tq, tq), tq)

        @pl.when(j == 0)
        def _():
            dq_ref[rows, :] = dq_t

        @pl.when(j > 0)
        def _():
            dq_ref[rows, :] += dq_t

        dvt_ref[...] += jnp.dot(_compact_tiles_t(dot_ref[...], MXU_DTYPE), p_buf[...], preferred_element_type=F32)
        dkt_ref[...] += jnp.dot(_compact_tiles_t(qt_ref[...], MXU_DTYPE), ds_buf[...], preferred_element_type=F32)

    col = pl.BlockSpec((1, 1, r, 1), lambda j, i: (h, i, 0, 0))
    qspec = pl.BlockSpec((tq, KV_W), lambda j, i: (i, h))
    tspec = pl.BlockSpec((KV_W, tq), lambda j, i: (h, i))
    kspec = pl.BlockSpec((tk, KV_W), lambda j, i: (j, h))
    ospec = pl.BlockSpec((HEAD_DIM, tk), lambda j, i: (0, j))
    return pl.pallas_call(
        kern, name=f"glob_bwd_h{h}", grid=(nk, nq),
        in_specs=[qspec, tspec, kspec, kspec, qspec, tspec, col, col],
        out_specs=[pl.BlockSpec(memory_space=pltpu.VMEM), ospec, ospec],
        out_shape=[SDS((s_rows, KV_W), F32), SDS((HEAD_DIM, n), F32), SDS((HEAD_DIM, n), F32)],
        scratch_shapes=[pltpu.VMEM((r, tk), MXU_DTYPE), pltpu.VMEM((r, tk), MXU_DTYPE)],
        compiler_params=_cparams(("arbitrary", "arbitrary")),
    )(q, q_t, kt, vt, do, do_t, lse, delta)


TW = 2 * WIN
WR = GROUPS * TW
WLAT = 4 * WIN


def _win_cat(dst, parts):
    off = 0
    for p in parts:
        dst[off:off + p.shape[0], :] = p[...]
        off += p.shape[0]


def _win_specs(s_rows, c_rows):
    nb = s_rows // WIN
    prev = lambda i: jnp.maximum(2 * i - 1, 0)
    nxt = lambda i: jnp.minimum(2 * i + 2, nb - 1)
    rows = [pl.BlockSpec((WIN, KV_W), lambda h, i: (prev(i), h)), pl.BlockSpec((TW, KV_W), lambda h, i: (i, h)),
            pl.BlockSpec((WIN, KV_W), lambda h, i: (nxt(i), h)), pl.BlockSpec((c_rows, KV_W), lambda h, i: (s_rows // c_rows, h))]
    cols = [pl.BlockSpec((1, HEAD_DIM, WIN), lambda h, i: (h, 0, prev(i))), pl.BlockSpec((1, HEAD_DIM, TW), lambda h, i: (h, 0, i)),
            pl.BlockSpec((1, HEAD_DIM, WIN), lambda h, i: (h, 0, nxt(i))),
            pl.BlockSpec((1, HEAD_DIM, c_rows), lambda h, i: (h, 0, s_rows // c_rows))]
    return rows, cols


def _win_mask(i, s_rows, shape, keys_on_rows):
    a = lax.broadcasted_iota(jnp.int32, shape, 0)
    b = lax.broadcasted_iota(jnp.int32, shape, 1)
    kk, qq = (a, b) if keys_on_rows else (b, a)
    qpos = i * TW + (qq & (TW - 1))
    kpos = (2 * i - 1) * WIN + kk
    band = (jnp.abs(qpos - kpos) <= WIN) & (kpos >= 0) & (kpos < s_rows)
    return (kk >= WLAT) | band


def _untranspose_groups(o_t, tq):
    row = lax.broadcasted_iota(jnp.int32, (HEAD_DIM, KV_W), 0)
    col = lax.broadcasted_iota(jnp.int32, (HEAD_DIM, KV_W), 1)
    hi = o_t.astype(BF16)
    r1 = o_t - hi.astype(F32)
    mid = r1.astype(BF16)
    lo = (r1 - mid.astype(F32)).astype(BF16)
    o = jnp.zeros((tq, KV_W), F32)
    for g in range(GROUPS):
        sel = jnp.where(col == row + g * HEAD_DIM, 1.0, 0.0).astype(BF16)
        for term in (hi, mid, lo):
            o = o + lax.dot_general(term[:, g * tq:(g + 1) * tq], sel, (((0,), (0,)), ((), ())), preferred_element_type=F32)
    return o


def win_fwd(q, kt, v_t, sinkrow, s_rows, c_rows):
    nt = s_rows // TW
    nkeys = WLAT + c_rows

    def kern(q_ref, kp, kc, kn, kx, vp, vc, vn, vx, sink_ref, o_ref, lse_ref, kcat):
        i = pl.program_id(1)
        _win_cat(kcat, (kp, kc, kn, kx))
        qs = _stack_groups(q_ref[...], MXU_DTYPE)
        st = _dot_nt(kcat[...], qs)
        st = jnp.where(_win_mask(i, s_rows, st.shape, True), st, -jnp.inf)
        sink = sink_ref[0]
        m = jnp.maximum(jnp.max(st, axis=0, keepdims=True), sink)
        e = jnp.exp(st - m)
        den = jnp.sum(e, axis=0, keepdims=True) + jnp.exp(sink - m)
        v_cat = jnp.concatenate([vp[0], vc[0], vn[0], vx[0]], axis=1)
        o_t = jnp.dot(v_cat, e.astype(MXU_DTYPE), preferred_element_type=F32) / den
        o_ref[...] = _untranspose_groups(o_t, TW)
        lse_ref[0, 0] = _row_to_col(m + jnp.log(den))

    rows, cols = _win_specs(s_rows, c_rows)
    qspec = pl.BlockSpec((TW, KV_W), lambda h, i: (i, h))
    rowv = pl.BlockSpec((1, 1, WR, 1), lambda h, i: (h, i, 0, 0))
    return pl.pallas_call(
        kern, name="win_fwd", grid=(N_KV, nt),
        in_specs=[qspec] + rows + cols + [pl.BlockSpec((1, 1, WR), lambda h, i: (h, 0, 0))],
        out_specs=[qspec, rowv], out_shape=[SDS((s_rows, Q_W), F32), SDS((N_KV, nt, WR, 1), F32)],
        scratch_shapes=[pltpu.VMEM((nkeys, KV_W), MXU_DTYPE)],
        compiler_params=_cparams(("parallel", "parallel")),
    )(q, kt, kt, kt, kt, v_t, v_t, v_t, v_t, sinkrow)


def win_bwd(q, q_t, kt, vt, sinkcol, o, do, do_t, lse, s_rows, c_rows):
    nt = s_rows // TW
    nkeys = WLAT + c_rows
    n = s_rows + c_rows
    ctx0 = WIN + s_rows

    def kern(q_ref, qt_ref, kp, kc, kn, kx, vp, vc, vn, vx, sink_ref, o_ref, do_ref, dot_ref, lse_ref,
             dq_ref, dkt_ref, dvt_ref, dsk_ref, kcat, vcat):
        i = pl.program_id(1)

        @pl.when(i == 0)
        def _():
            dkt_ref[...] = jnp.zeros_like(dkt_ref)
            dvt_ref[...] = jnp.zeros_like(dvt_ref)
            dsk_ref[...] = jnp.zeros_like(dsk_ref)

        _win_cat(kcat, (kp, kc, kn, kx))
        _win_cat(vcat, (vp, vc, vn, vx))
        qs = _stack_groups(q_ref[...], MXU_DTYPE)
        do32 = _stack_groups(do_ref[...], F32)
        delta = jnp.sum(do32 * _stack_groups(o_ref[...], F32), axis=1, keepdims=True)
        dos = do32.astype(MXU_DTYPE)
        lse_c = lse_ref[0, 0]
        s = _dot_nt(qs, kcat[...])
        s = jnp.where(_win_mask(i, s_rows, s.shape, False), s, -jnp.inf)
        p = jnp.exp(s - lse_c)
        ds = p * (_dot_nt(dos, vcat[...]) - delta)
        dq_ref[...] = _fold_groups(_dot(ds, kcat[...]), TW)
        dvt = jnp.dot(_compact_t(dot_ref[...], MXU_DTYPE), p.astype(MXU_DTYPE), preferred_element_type=F32)
        dkt = jnp.dot(_compact_t(qt_ref[...], MXU_DTYPE), ds.astype(MXU_DTYPE), preferred_element_type=F32)
        lat = pl.ds(pl.multiple_of(i * TW, TW), WLAT)
        dkt_ref[0, :, lat] += dkt[:, :WLAT]
        dvt_ref[0, :, lat] += dvt[:, :WLAT]
        dkt_ref[0, :, ctx0:ctx0 + c_rows] += dkt[:, WLAT:]
        dvt_ref[0, :, ctx0:ctx0 + c_rows] += dvt[:, WLAT:]
        dsk_ref[0] += -(jnp.exp(sink_ref[0][:, 0:1] - lse_c) * delta)

    rows, _ = _win_specs(s_rows, c_rows)
    qspec = pl.BlockSpec((TW, KV_W), lambda h, i: (i, h))
    tspec = pl.BlockSpec((KV_W, TW), lambda h, i: (h, i))
    col = pl.BlockSpec((1, 1, WR, 1), lambda h, i: (h, i, 0, 0))
    kvt = pl.BlockSpec((1, HEAD_DIM, WIN + n), lambda h, i: (h, 0, 0))
    return pl.pallas_call(
        kern, name="win_bwd", grid=(N_KV, nt),
        in_specs=[qspec, tspec] + rows + rows + [pl.BlockSpec((1, WR, LANES), lambda h, i: (h, 0, 0)), qspec, qspec, tspec, col],
        out_specs=[qspec, kvt, kvt, pl.BlockSpec((1, WR, 1), lambda h, i: (h, 0, 0))],
        out_shape=[SDS((s_rows, Q_W), F32), SDS((N_KV, HEAD_DIM, WIN + n), F32), SDS((N_KV, HEAD_DIM, WIN + n), F32),
                   SDS((N_KV, WR, 1), F32)],
        scratch_shapes=[pltpu.VMEM((nkeys, KV_W), MXU_DTYPE), pltpu.VMEM((nkeys, KV_W), MXU_DTYPE)],
        compiler_params=_cparams(("arbitrary", "arbitrary")),
    )(q, q_t, kt, kt, kt, kt, vt, vt, vt, vt, sinkcol, o, do, do_t, lse)


def adamw(name, w, m, v, grads):
    r, wd = w.shape
    tr = _pick(r, [t for t in ELEMENTWISE_ROWS if t * wd * 4 <= ELEMENTWISE_BLOCK_BYTES])
    stacked = not isinstance(grads, (list, tuple))
    ng = grads.shape[0] if stacked else len(grads)

    def kern(*refs):
        w_ref, m_ref, v_ref = refs[:3]
        g_refs = refs[3:-4]
        g_out, d_out, m_out, v_out = refs[-4:]
        if stacked:
            g = g_refs[0][0]
            for k in range(1, ng):
                g = g + g_refs[0][k]
        else:
            g = g_refs[0][...]
            for gr in g_refs[1:]:
                g = g + gr[...]
        wv = w_ref[...]
        mn = ADAM_B1 * m_ref[...] + (1.0 - ADAM_B1) * g
        vn = ADAM_B2 * v_ref[...] + (1.0 - ADAM_B2) * (g * g)
        m_hat = mn / (1.0 - ADAM_B1 ** ADAM_STEP)
        v_hat = vn / (1.0 - ADAM_B2 ** ADAM_STEP)
        g_out[...] = g
        d_out[...] = -ADAM_LR * (m_hat / (jnp.sqrt(v_hat) + ADAM_EPS) + ADAM_WD * wv)
        m_out[...] = mn
        v_out[...] = vn

    spec = pl.BlockSpec((tr, wd), lambda i: (i, 0))
    gspecs = [pl.BlockSpec((ng, tr, wd), lambda i: (0, i, 0))] if stacked else [spec] * ng
    return pl.pallas_call(
        kern, name=name, grid=(r // tr,), in_specs=[spec] * 3 + gspecs, out_specs=[spec] * 4,
        out_shape=[SDS((r, wd), F32)] * 4, compiler_params=_cparams(("parallel",)),
    )(w, m, v, *([grads] if stacked else grads))


def add2(name, a, b):
    k, r, w = a.shape
    tr = _pick(r, [t for t in ELEMENTWISE_ROWS if t * w * 4 <= ELEMENTWISE_BLOCK_BYTES])

    def kern(a_ref, b_ref, o_ref):
        o_ref[...] = a_ref[...] + b_ref[...]

    spec = pl.BlockSpec((1, tr, w), lambda s, i: (s, i, 0))
    return pl.pallas_call(kern, name=name, grid=(k, r // tr), in_specs=[spec, spec], out_specs=spec,
                          out_shape=SDS(a.shape, a.dtype), compiler_params=_cparams(("parallel", "parallel")))(a, b)


def _rep4(a, off):
    return jnp.concatenate([a[:, off + HEAD_DIM * h: off + HEAD_DIM * (h + 1)] for h in range(N_KV) for _ in range(GROUPS)], axis=1)


def _extend_cols(a):
    return jnp.concatenate([a[:, 0:OFF_KA], _rep4(a, OFF_KA), _rep4(a, OFF_VA), a[:, OFF_QB:OFF_KB],
                            _rep4(a, OFF_KB), _rep4(a, OFF_VB), a[:, OFF_GA:]], axis=1)


def _fold4(a, off):
    r = a.shape[0]
    return a[:, off:off + Q_W].reshape(r, N_KV, GROUPS, HEAD_DIM).sum(axis=2).reshape(r, N_KV * HEAD_DIM)


def _fold_cols(a):
    return jnp.concatenate([a[:, X_QA:X_QA + Q_W], _fold4(a, X_KA), _fold4(a, X_VA), a[:, X_QB:X_QB + Q_W],
                            _fold4(a, X_KB), _fold4(a, X_VB), a[:, X_GL:]], axis=1)


def _rope_tables(s_rows, c_rows):
    n_rows = s_rows // GRID_W
    n_freq = HEAD_DIM // 4
    inv_freq = ROPE_THETA ** (-jnp.arange(n_freq, dtype=F32) / n_freq)
    ang_r = jnp.arange(n_rows, dtype=jnp.int32).astype(F32)[:, None] * inv_freq
    ang_c = jnp.arange(GRID_W, dtype=jnp.int32).astype(F32)[:, None] * inv_freq
    by_row = lambda t: jnp.repeat(t, GRID_W, axis=0)
    by_col = lambda t: jnp.tile(t, (n_rows, 1))
    cos = jnp.concatenate([by_row(jnp.cos(ang_r))] * 2 + [by_col(jnp.cos(ang_c))] * 2, axis=1)
    sin_r, sin_c = by_row(jnp.sin(ang_r)), by_col(jnp.sin(ang_c))
    sin = jnp.concatenate([-sin_r, sin_r, -sin_c, sin_c], axis=1)
    cos = jnp.concatenate([cos, jnp.ones((c_rows, HEAD_DIM), F32)], axis=0)
    sin = jnp.concatenate([sin, jnp.zeros((c_rows, HEAD_DIM), F32)], axis=0)
    return jnp.concatenate([cos, cos], axis=1), jnp.concatenate([sin, sin], axis=1)


def _ff_pad_cols(a):
    r = a.shape[0]
    a = jnp.pad(a.reshape(r, N_DEV, FF_SHARD), ((0, 0), (0, 0), (0, FF_SHARD_PAD - FF_SHARD)))
    return a.reshape(r, 2 * FF)


def _ff_unpad_cols(a):
    r = a.shape[0]
    return a.reshape(r, N_DEV, FF_SHARD_PAD)[:, :, :FF_SHARD].reshape(r, 2 * D_FF)


def _ff_pad_rows(a):
    c = a.shape[1]
    a = jnp.pad(a.reshape(N_DEV // 2, FF_SHARD, c), ((0, 0), (0, FF_SHARD_PAD - FF_SHARD), (0, 0)))
    return a.reshape(FF, c)


def _ff_unpad_rows(a):
    c = a.shape[1]
    return a.reshape(N_DEV // 2, FF_SHARD_PAD, c)[:, :FF_SHARD].reshape(D_FF, c)


BIG = (("w_in", (D, IN_COLS // N_DEV)), ("w_branch_a", (Q_W, D // N_DEV)), ("w_branch_b", (Q_W, D // N_DEV)),
       ("w_out", (D // N_DEV, D)), ("w_up", (D, FF_SHARD_PAD)), ("w_down", (D_FF // N_DEV, D)))
BIG_SIZES = tuple(int(np.prod(s)) for _, s in BIG)
BIG_ROWS = sum(BIG_SIZES) // LANES


def _pack_big(parts):
    return jnp.concatenate([p.reshape(-1) for p in parts]).reshape(BIG_ROWS, LANES)


def _unpack_big(flat):
    lead = flat.shape[:-2]
    f = flat.reshape(*lead, BIG_ROWS * LANES)
    out, off = [], 0
    for (_, shp), sz in zip(BIG, BIG_SIZES, strict=True):
        out.append(f[..., off:off + sz].reshape(*lead, *shp))
        off += sz
    return out


def _cols_to_full(g):
    return jnp.transpose(g, (1, 0, 2)).reshape(g.shape[1], -1)


def _full_to_cols(a):
    r, c = a.shape
    return jnp.transpose(a.reshape(r, N_DEV, c // N_DEV), (1, 0, 2))


SMALL = (("c_ctx", D), ("b_mod", N_MOD * D), ("b_in", IN_COLS), ("attn_sink", N_HEADS), ("q_norm_g", HEAD_DIM),
         ("k_norm_g", HEAD_DIM), ("ln1_g", D), ("ln1_b", D), ("conv_w", 3 * 2 * D_FF // N_DEV), ("conv_b", 2 * D_FF),
         ("ln2_g", D), ("ln2_b", D))
SMALL_TOTAL = sum(n for _, n in SMALL)
SMALL_ROWS = -(-SMALL_TOTAL // (8 * LANES)) * 8


def _pack_small(parts):
    flat = jnp.concatenate([p.reshape(-1).astype(F32) for p in parts])
    return jnp.pad(flat, (0, SMALL_ROWS * LANES - flat.shape[0])).reshape(SMALL_ROWS, LANES)


def _unpack_small(packed):
    f = packed.reshape(-1)
    out, off = {}, 0
    for name, n in SMALL:
        out[name] = f[off:off + n]
        off += n
    return out


RED = (("c_ctx", D), ("b_in", IN_COLS), ("attn_sink", N_HEADS), ("q_norm_g", HEAD_DIM), ("k_norm_g", HEAD_DIM),
       ("ln1_g", D), ("ln1_b", D), ("conv_w", 3 * 2 * FF), ("conv_b", 2 * FF), ("ln2_g", D), ("ln2_b", D))
RED_TOTAL = sum(n for _, n in RED)
RED_ROWS = -(-RED_TOTAL // (8 * LANES)) * 8


def sum8(name, g):
    _, r, w = g.shape

    def kern(g_ref, o_ref):
        acc = g_ref[0]
        for k in range(1, N_DEV):
            acc = acc + g_ref[k]
        o_ref[...] = acc

    return pl.pallas_call(kern, name=name, out_shape=SDS((r, w), F32))(g)


def _local_step(x, ctx, target, modrows, weights, small):
    s_rows, c_rows = x.shape[0], ctx.shape[0]
    n = s_rows + c_rows
    nl = s_rows // TM
    w_in, wba, wbb, w_out, w_up, w_down = weights
    f = lambda a: a.reshape(1, -1).astype(F32)
    b_in, ln1_g, ln1_b, ln2_g, ln2_b, conv_b = (f(small[k]) for k in ("b_in", "ln1_g", "ln1_b", "ln2_g", "ln2_b", "conv_b"))
    conv_w8 = jnp.pad(small["conv_w_full"], ((0, 5), (0, 0)))
    qg = jnp.tile(small["q_norm_g"].reshape(1, HEAD_DIM), (1, N_HEADS))
    kg = jnp.tile(small["k_norm_g"].reshape(1, HEAD_DIM), (1, N_HEADS))
    sink_rep = jnp.repeat(small["attn_sink"].reshape(N_KV, GROUPS), TW, axis=1)
    sinkrow = sink_rep.reshape(N_KV, 1, WR)
    sinkcol = jnp.broadcast_to(sink_rep[:, :, None], (N_KV, WR, LANES))
    bd = jnp.kron(jnp.eye(N_HEADS, dtype=F32), jnp.ones((HEAD_DIM, HEAD_DIM), F32)).astype(BF16)
    cos, sin = _rope_tables(s_rows, c_rows)
    w_ext = _extend_cols(w_in)
    b_ext = _extend_cols(b_in)
    xa = jnp.concatenate([x, ctx], axis=0)

    hb, qa, kat, vat, qb, kbt, vbt, tq, rq, tk, rk, gl = inproj_fwd(xa, cos, sin, modrows, w_ext, b_ext, qg, kg, bd, nl)
    compact_t = lambda t: jnp.stack([t[:, h * KV_W:h * KV_W + HEAD_DIM].T for h in range(N_KV)])
    oa, lse_a = win_fwd(qa, kat, compact_t(vat), sinkrow, s_rows, c_rows)
    vb_t = compact_t(vbt)
    ob, lse_b = glob_fwd(qb, kbt, vb_t, s_rows)
    tqb = _pick(s_rows, GLOB_BWD_TQ)
    lse_b = lse_b.reshape(N_KV, s_rows // tqb, GROUPS * tqb, 1)
    ya, yb, mrg, y, xhat1, rstd1 = merge_fwd(oa, ob, gl, x, modrows, wba, wbb, w_out, s_rows)
    h2, u0 = ffn_up_fwd(xhat1, modrows, ln1_g, ln1_b, w_up, s_rows)
    a = conv_swiglu_fwd(u0, conv_w8, conv_b, s_rows)
    (dr2, dy2), (loss, dln2_g, dln2_b, dgate2) = ffn_down_loss(a, xhat1, target, modrows, ln1_g, ln1_b, ln2_g, ln2_b, w_down, s_rows)

    da = ffn_down_bwd(dy2, w_down.T, s_rows)
    dw_down = mm_tn("dw_down", a, dy2, s_rows)
    du0, (dconv_b, dcw0, dcw1, dcw2) = swiglu_conv_bwd(u0, da, conv_w8, conv_b, s_rows)
    dw_up = mm_tn("dw_up", h2, du0, s_rows)
    (dy, dxp), (dscale2, dshift2, dln1_g, dln1_b, dgate1) = ffn_up_ln1_bwd(du0, dr2, xhat1, y, rstd1, modrows, ln1_g, ln1_b, w_up.T, s_rows)
    dya, dyb, dgl, doa, dob = merge_bwd(dy, ya, yb, gl, w_out.T, wba.T, wbb.T, s_rows)
    dw_out = mm_tn("dw_out", mrg, dy, s_rows)
    dwba = mm_tn("dw_branch_a", oa, dya, s_rows)
    dwbb = mm_tn("dw_branch_b", ob, dyb, s_rows)

    head_rows = lambda t: t.reshape(N_KV * HEAD_DIM, t.shape[-1])
    place = np.zeros((N_KV * HEAD_DIM, Q_W), np.float32)
    for h in range(N_KV):
        place[h * HEAD_DIM + np.arange(HEAD_DIM), h * KV_W + np.arange(HEAD_DIM)] = 1.0
    place = jnp.asarray(place, BF16)
    dqa, dka_t, dva_t, dsk = win_bwd(qa, qa[:s_rows].T, kat, vat, sinkcol, oa, doa, doa.astype(MXU_DTYPE).T, lse_a,
                                     s_rows, c_rows)
    dka_t, dva_t = head_rows(dka_t[:, :, WIN:]), head_rows(dva_t[:, :, WIN:])
    delta_b = attn_delta(ob, dob, s_rows)
    qb_t = qb[:s_rows].T
    dob_t = dob.astype(MXU_DTYPE).T
    heads = [glob_bwd(qb, qb_t, kbt, vbt, dob, dob_t, lse_b, delta_b, h, s_rows) for h in range(N_KV)]
    dqb = [hd[0] for hd in heads]
    dkb_t = jnp.concatenate([hd[1] for hd in heads], axis=0)
    dvb_t = jnp.concatenate([hd[2] for hd in heads], axis=0)
    dproj, (db_ext, dqg, dkg) = qk_bwd(dqa, dka_t, dva_t, dqb, dkb_t, dvb_t, dgl, tq, rq, tk, rk, cos, sin, qg, kg, bd,
                                       place, nl, n)
    w_ext_t = w_ext.T
    (grad_x,), (dscale1, dshift1) = inproj_bwd("inproj_bwd", dproj, xa, dxp, modrows, w_ext_t, ntiles=nl, tile_off=0,
                                               is_ctx=False, out_rows=s_rows)
    _, (dscale_c, dshift_c) = inproj_bwd("inproj_bwd_ctx", dproj, xa, None, modrows, w_ext_t, ntiles=c_rows // TM,
                                         tile_off=nl, is_ctx=True, out_rows=0)
    dw_in = _fold_cols(mm_tn("dw_in", hb, dproj, n))

    dmod = jnp.concatenate([dshift1, dscale1, dgate1, dshift2, dscale2, dgate2], axis=1)
    dmod_c = jnp.concatenate([dshift_c, dscale_c, jnp.zeros((1, (N_MOD - 2) * D), F32)], axis=1)
    fold_g = lambda t: t.reshape(N_HEADS, HEAD_DIM).sum(axis=0)
    red = {
        "b_in": _fold_cols(db_ext), "attn_sink": dsk.reshape(N_HEADS, TW).sum(axis=1), "q_norm_g": fold_g(dqg),
        "k_norm_g": fold_g(dkg), "ln1_g": dln1_g, "ln1_b": dln1_b, "conv_w": jnp.concatenate([dcw0, dcw1, dcw2], axis=0),
        "conv_b": dconv_b, "ln2_g": dln2_g, "ln2_b": dln2_b,
    }
    return loss[0, 0], grad_x, (dw_in, dwba, dwbb, dw_out, dw_up, dw_down), dmod, dmod_c, red


def kernel(x, c, ctx, c_ctx, w_mod, b_mod, w_in, b_in, attn_sink, q_norm_g, k_norm_g, w_branch_a, w_branch_b, w_out, ln1_g, ln1_b, w_up, conv_w, conv_b, w_down, ln2_g, ln2_b, loss_target, m_c_ctx, m_w_mod, m_b_mod, m_w_in, m_b_in, m_attn_sink, m_q_norm_g, m_k_norm_g, m_w_branch_a, m_w_branch_b, m_w_out, m_ln1_g, m_ln1_b, m_w_up, m_conv_w, m_conv_b, m_w_down, m_ln2_g, m_ln2_b, v_c_ctx, v_w_mod, v_b_mod, v_w_in, v_b_in, v_attn_sink, v_q_norm_g, v_k_norm_g, v_w_branch_a, v_w_branch_b, v_w_out, v_ln1_g, v_ln1_b, v_w_up, v_conv_w, v_conv_b, v_w_down, v_ln2_g, v_ln2_b):
    ax, ay, ac = (lax.axis_index(a) for a in AXES)
    me = 4 * ax + 2 * ay + ac
    chip = 2 * ax + ay
    mod_w = N_MOD * D // N_DEV
    params = dict(c_ctx=c_ctx, w_mod=w_mod, b_mod=b_mod, w_in=w_in, b_in=b_in, attn_sink=attn_sink, q_norm_g=q_norm_g,
                  k_norm_g=k_norm_g, w_branch_a=w_branch_a, w_branch_b=w_branch_b, w_out=w_out, ln1_g=ln1_g, ln1_b=ln1_b,
                  w_up=w_up, conv_w=conv_w, conv_b=conv_b, w_down=w_down, ln2_g=ln2_g, ln2_b=ln2_b)
    mom_m = dict(c_ctx=m_c_ctx, w_mod=m_w_mod, b_mod=m_b_mod, w_in=m_w_in, b_in=m_b_in, attn_sink=m_attn_sink,
                 q_norm_g=m_q_norm_g, k_norm_g=m_k_norm_g, w_branch_a=m_w_branch_a, w_branch_b=m_w_branch_b, w_out=m_w_out,
                 ln1_g=m_ln1_g, ln1_b=m_ln1_b, w_up=m_w_up, conv_w=m_conv_w, conv_b=m_conv_b, w_down=m_w_down,
                 ln2_g=m_ln2_g, ln2_b=m_ln2_b)
    mom_v = dict(c_ctx=v_c_ctx, w_mod=v_w_mod, b_mod=v_b_mod, w_in=v_w_in, b_in=v_b_in, attn_sink=v_attn_sink,
                 q_norm_g=v_q_norm_g, k_norm_g=v_k_norm_g, w_branch_a=v_w_branch_a, w_branch_b=v_w_branch_b, w_out=v_w_out,
                 ln1_g=v_ln1_g, ln1_b=v_ln1_b, w_up=v_w_up, conv_w=v_conv_w, conv_b=v_conv_b, w_down=v_w_down,
                 ln2_g=v_ln2_g, ln2_b=v_ln2_b)
    big_names = [nm for nm, _ in BIG]

    def shard(tree, nm):
        t = tree[nm][0]
        return jnp.pad(t, ((0, 0), (0, FF_SHARD_PAD - FF_SHARD))) if nm == "w_up" else t

    wg = all_gather("ag_weights", _pack_big([shard(params, nm).astype(MXU_DTYPE) for nm in big_names]))
    g_in, g_ba, g_bb, g_out, g_up, g_down = _unpack_big(wg)
    weights = (_cols_to_full(g_in), _cols_to_full(g_ba), _cols_to_full(g_bb), g_out.reshape(D, D), _cols_to_full(g_up),
               _ff_pad_rows(g_down.reshape(D_FF, D)))

    c_all = all_gather("ag_c", c.reshape(8, LANES)).reshape(N_DEV, D)
    cs = jnp.concatenate([c_all, c_ctx.reshape(1, D), jnp.zeros((7, D), F32)], axis=0)
    w_mod_sh = w_mod[0]
    b_mod_sh = lax.dynamic_slice(b_mod, (0, me * mod_w), (1, mod_w))
    mod_part = mod_fwd(cs, w_mod_sh, b_mod_sh)
    mg = all_gather("ag_mod", mod_part.reshape(16 * mod_w // LANES, LANES)).reshape(N_DEV, 16, mod_w)
    mod = lax.dynamic_index_in_dim(mg, me, axis=1, keepdims=False).reshape(N_MOD, D)
    mod_c = mg[:, 8, :].reshape(N_MOD, D)
    modrows = jnp.stack([mod[0], mod[1], mod_c[0], mod_c[1], mod[2], mod[3], mod[4], mod[5]], axis=0)

    conv_w_full = all_gather("ag_conv_w", jnp.pad(conv_w[0], ((0, 5), (0, FF_SHARD_PAD - FF_SHARD))))
    conv_w_full = _cols_to_full(conv_w_full[:, :3, :])
    small = dict(b_in=b_in, ln1_g=ln1_g, ln1_b=ln1_b, ln2_g=ln2_g, ln2_b=ln2_b, conv_b=_ff_pad_cols(conv_b),
                 conv_w_full=conv_w_full, q_norm_g=q_norm_g, k_norm_g=k_norm_g, attn_sink=attn_sink)
    loss, grad_x, big_grads, dmod, dmod_c, red = _local_step(x[0], ctx[0], loss_target[0], modrows, weights, small)
    loss = lax.psum(loss, AXES)

    dm = all_gather("ag_dmod", jnp.concatenate([dmod, dmod_c], axis=0).reshape(2 * N_MOD * D // LANES, LANES))
    dm = dm.reshape(N_DEV, 2, N_MOD * D)
    dm_all = jnp.concatenate([dm[:, 0], dm[:, 1]], axis=0)
    dm_sh = lax.dynamic_slice(dm_all, (0, me * mod_w), (16, mod_w))
    dw_mod, dcc, db_mod = mod_bwd(cs, w_mod_sh, dm_sh, dm_all)
    red["c_ctx"] = dcc[8]

    red_vec = jnp.concatenate([red[nm].reshape(-1) for nm, _ in RED])
    red_vec = jnp.pad(red_vec, (0, RED_ROWS * LANES - RED_TOTAL)).reshape(RED_ROWS, LANES)
    red_sum = sum8("sum_small", all_gather("ag_small", red_vec)).reshape(-1)
    gsm, off = {}, 0
    for nm, k in RED:
        gsm[nm] = red_sum[off:off + k]
        off += k
    gsm["b_mod"] = db_mod.reshape(-1)
    gsm["conv_b"] = _ff_unpad_cols(gsm["conv_b"].reshape(1, 2 * FF))
    gsm["conv_w"] = lax.dynamic_slice(gsm["conv_w"].reshape(3, 2 * FF), (0, me * FF_SHARD_PAD), (3, FF_SHARD_PAD))[:, :FF_SHARD]
    sm_names = [nm for nm, _ in SMALL]
    gs, ds, ms, vs = adamw("adamw_small", _pack_small([params[nm] for nm in sm_names]),
                           _pack_small([mom_m[nm] for nm in sm_names]), _pack_small([mom_v[nm] for nm in sm_names]),
                           [_pack_small([gsm[nm] for nm in sm_names])])
    sm_out = [_unpack_small(t) for t in (gs, ds, ms, vs)]

    dw_in, dwba, dwbb, dw_out, dw_up, dw_down = big_grads
    slabs = jnp.concatenate([t.reshape(N_DEV, -1) for t in (
        _full_to_cols(dw_in), _full_to_cols(dwba), _full_to_cols(dwbb), dw_out, _full_to_cols(dw_up),
        _ff_unpad_rows(dw_down))], axis=1)
    by_core = slabs.reshape(4, 2, BIG_ROWS, LANES)
    keep = lax.dynamic_index_in_dim(by_core, ac, axis=1, keepdims=False)
    give = lax.dynamic_index_in_dim(by_core, 1 - ac, axis=1, keepdims=False)
    got = exchange("rs_sibling", give.reshape(1, 4 * BIG_ROWS, LANES), to_chips=False).reshape(4, BIG_ROWS, LANES)
    pair = add2("rs_pair_sum", keep, got)
    outbox = jnp.stack([lax.dynamic_index_in_dim(pair, jnp.bitwise_xor(chip, m), axis=0, keepdims=False) for m in (1, 2, 3)])
    inbox = exchange("rs_chips", outbox.astype(MXU_DTYPE), to_chips=True)
    mine = lax.dynamic_index_in_dim(pair, chip, axis=0, keepdims=False)
    gb, db, mb, vb = adamw("adamw_big", _pack_big([shard(params, nm) for nm in big_names]),
                           _pack_big([shard(mom_m, nm) for nm in big_names]), _pack_big([shard(mom_v, nm) for nm in big_names]),
                           [mine, inbox[0], inbox[1], inbox[2]])
    big_out = [dict(zip(big_names, _unpack_big(t), strict=True)) for t in (gb, db, mb, vb)]
    for out in big_out:
        out["w_up"] = out["w_up"][:, :FF_SHARD]
    gm, dmo, mmo, vmo = adamw("adamw_mod", w_mod[0], m_w_mod[0], v_w_mod[0], [dw_mod])
    mod_out = (gm, dmo, mmo, vmo)

    order = ["c_ctx", "w_mod", "b_mod", "w_in", "b_in", "attn_sink", "q_norm_g", "k_norm_g", "w_branch_a", "w_branch_b",
             "w_out", "ln1_g", "ln1_b", "w_up", "conv_w", "conv_b", "w_down", "ln2_g", "ln2_b"]
    results = [loss, grad_x[None]]
    for kind in range(4):
        for nm in order:
            if nm == "w_mod":
                val = mod_out[kind]
            elif nm in big_out[kind]:
                val = big_out[kind][nm]
            else:
                val = sm_out[kind][nm]
            results.append(val.reshape(params[nm].shape))
    return tuple(results)
```

```python
import functools

import jax
import jax.numpy as jnp
import numpy as np
from jax import lax
from jax.experimental import pallas as pl
from jax.experimental.pallas import tpu as pltpu

F32 = jnp.float32
BF16 = jnp.bfloat16
MXU_DTYPE = BF16

AXES = ("x", "y", "c")
N_DEV = 8
D = 1024
HEAD_DIM = 64
N_HEADS = 8
N_KV = 2
GROUPS = 4
KV_W = GROUPS * HEAD_DIM
Q_W = N_HEADS * HEAD_DIM
GRID_W = 64
WIN = 128
ROPE_THETA = 10000.0
D_FF = 2816
FF_SHARD = 2 * D_FF // N_DEV
FF_SHARD_PAD = 768
FF = N_DEV // 2 * FF_SHARD_PAD
LN_EPS = 1e-5
QK_EPS = 1e-6
N_MOD = 6
ALPHA = 2.0 ** 0.25
Q_SCALE = HEAD_DIM ** -0.5
IN_COLS = 3584
OFF_KA, OFF_VA, OFF_QB, OFF_KB, OFF_VB, OFF_GA = 512, 640, 768, 1280, 1408, 1536
EXT_COLS = 6 * Q_W + 2 * D
X_QA, X_KA, X_VA, X_QB, X_KB, X_VB, X_GL = 0, 512, 1024, 1536, 2048, 2560, 3072
ADAM_LR, ADAM_B1, ADAM_B2, ADAM_EPS, ADAM_WD, ADAM_STEP = 0.001, 0.9, 0.999, 1e-08, 0.01, 10
LANES = 128
TM = 256
VMEM_LIMIT = 56 * 1024 * 1024
ELEMENTWISE_BLOCK_BYTES = 1 << 20
ELEMENTWISE_ROWS = (1824, 1408, 1024, 512, 256, 128, 64, 32, 16, 8)

ANY = pl.BlockSpec(memory_space=pl.ANY)
SDS = jax.ShapeDtypeStruct


def _pick(n, candidates):
    for t in candidates:
        if n % t == 0:
            return t
    raise ValueError(f"no tile for {n}")


def _full(a):
    nd = a.ndim
    return pl.BlockSpec(a.shape, lambda *_: (0,) * nd)


def _rows(tm, w, fn=lambda t: t):
    return pl.BlockSpec((tm, w), lambda i: (fn(i), 0))


def _dot(a, b):
    return jnp.dot(a.astype(MXU_DTYPE), b.astype(MXU_DTYPE), preferred_element_type=F32)


def _dot_nt(a, b):
    return lax.dot_general(a.astype(MXU_DTYPE), b.astype(MXU_DTYPE), (((1,), (1,)), ((), ())), preferred_element_type=F32)


def _dot_tn(a, b):
    return lax.dot_general(a.astype(MXU_DTYPE), b.astype(MXU_DTYPE), (((0,), (0,)), ((), ())), preferred_element_type=F32)


def _cparams(sem):
    return pltpu.CompilerParams(dimension_semantics=sem, vmem_limit_bytes=VMEM_LIMIT)


def all_gather(name, v):
    r, w = v.shape

    def body(x_ref, out_ref, send_sems, recv_sems, local_sem):
        x, y, c = (lax.axis_index(a) for a in AXES)
        me, sibling = (x, y, c), (x, y, 1 - c)
        chips = [(1 - x, y), (x, 1 - y), (1 - x, 1 - y)]

        def rows(px, py, pc):
            return out_ref.at[4 * px + 2 * py + pc]

        def copy(k, block, to, src=None):
            return pltpu.make_async_remote_copy(
                src_ref=rows(*block) if src is None else src, dst_ref=rows(*block),
                send_sem=send_sems.at[k], recv_sem=recv_sems.at[k],
                device_id=to, device_id_type=pl.DeviceIdType.MESH)

        mine = pltpu.make_async_copy(x_ref, rows(*me), local_sem)
        mine.start()
        first = [copy(0, me, sibling, src=x_ref)]
        first += [copy(1 + j, me, (*chip, c), src=x_ref) for j, chip in enumerate(chips)]
        for cp in first:
            cp.start()
        passed = [copy(4 + j, (*chip, c), sibling) for j, chip in enumerate(chips)]
        for j, chip in enumerate(chips):
            copy(1 + j, (*chip, c), me).wait_recv()
            passed[j].start()
        copy(0, sibling, me).wait_recv()
        for j, chip in enumerate(chips):
            copy(4 + j, (*chip, 1 - c), me).wait_recv()
        for cp in first + passed:
            cp.wait_send()
        mine.wait()

    return pl.pallas_call(
        body, name=name, out_shape=SDS((N_DEV, r, w), v.dtype), in_specs=[ANY], out_specs=ANY,
        scratch_shapes=[pltpu.SemaphoreType.DMA((7,)), pltpu.SemaphoreType.DMA((7,)), pltpu.SemaphoreType.DMA],
    )(v)


def exchange(name, outbox, to_chips):
    k = outbox.shape[0]
    assert k == (3 if to_chips else 1)

    def body(out_ref, in_ref, send_sems, recv_sems):
        x, y, c = (lax.axis_index(a) for a in AXES)
        peers = [(x, 1 - y, c), (1 - x, y, c), (1 - x, 1 - y, c)] if to_chips else [(x, y, 1 - c)]
        copies = [
            pltpu.make_async_remote_copy(
                src_ref=out_ref.at[m], dst_ref=in_ref.at[m], send_sem=send_sems.at[m], recv_sem=recv_sems.at[m],
                device_id=peer, device_id_type=pl.DeviceIdType.MESH)
            for m, peer in enumerate(peers)
        ]
        for cp in copies:
            cp.start()
        for cp in copies:
            cp.wait_recv()
        for cp in copies:
            cp.wait_send()

    return pl.pallas_call(
        body, name=name, out_shape=SDS(outbox.shape, outbox.dtype), in_specs=[ANY], out_specs=ANY,
        scratch_shapes=[pltpu.SemaphoreType.DMA((k,)), pltpu.SemaphoreType.DMA((k,))],
    )(outbox)


def rowwise(name, body, *, ntiles, tile_off=0, tiled, full, outs, accs=()):
    nt, nf, no = len(tiled), len(full), len(outs)

    def kern(*refs):
        i = pl.program_id(0)
        out_vals, incs = body(i + tile_off, refs[:nt], refs[nt:nt + nf])
        for r, v in zip(refs[nt + nf:nt + nf + no], out_vals, strict=True):
            r[...] = v.astype(r.dtype)
        acc_refs = refs[nt + nf + no:]

        @pl.when(i == 0)
        def _():
            for r in acc_refs:
                r[...] = jnp.zeros_like(r)

        for r, v in zip(acc_refs, incs, strict=True):
            r[...] += v

    res = pl.pallas_call(
        kern, name=name, grid=(ntiles,),
        in_specs=[s for _, s in tiled] + [_full(a) for a in full],
        out_specs=[s for _, _, s in outs] + [pl.BlockSpec(s, lambda i, n=len(s): (0,) * n) for s in accs],
        out_shape=[SDS(s, d) for s, d, _ in outs] + [SDS(s, F32) for s in accs],
        compiler_params=_cparams(("arbitrary",) if accs else ("parallel",)),
    )(*[a for a, _ in tiled], *full)
    return res[:no], res[no:]


def mm_tn(name, a, b, rows):
    ka, nb = a.shape[1], b.shape[1]
    tr = _pick(rows, (1280, 1024, 768, 512, 256))
    tn = _pick(nb, (1024, 512, 256, 128))

    def kern(a_ref, b_ref, o_ref):
        @pl.when(pl.program_id(1) == 0)
        def _():
            o_ref[...] = jnp.zeros_like(o_ref)

        o_ref[...] += _dot_tn(a_ref[...], b_ref[...])

    return pl.pallas_call(
        kern, name=name, grid=(nb // tn, rows // tr),
        in_specs=[pl.BlockSpec((tr, ka), lambda n, r: (r, 0)), pl.BlockSpec((tr, tn), lambda n, r: (r, n))],
        out_specs=pl.BlockSpec((ka, tn), lambda n, r: (0, n)), out_shape=SDS((ka, nb), F32),
        compiler_params=_cparams(("parallel", "arbitrary")),
    )(a, b)


def _swap16(t):
    w = t.shape[1]
    lane = lax.broadcasted_iota(jnp.int32, t.shape, 1)
    return jnp.where((lane & 16) == 0, pltpu.roll(t, w - 16, 1), pltpu.roll(t, 16, 1))


def _rope(t, cos, sin):
    return t * cos + _swap16(t) * sin


def _rope_t(d, cos, sin):
    return d * cos - _swap16(d) * sin


def _seg_sum64(a, bd_ref):
    bd = bd_ref[...]
    hi = a.astype(BF16)
    lo = (a - hi.astype(F32)).astype(BF16)
    return jnp.dot(hi, bd, preferred_element_type=F32) + jnp.dot(lo, bd, preferred_element_type=F32)


def _lane_block(shape):
    return jnp.right_shift(lax.broadcasted_iota(jnp.int32, shape, 1), 6)


def _stack_groups(t, dtype):
    blk = _lane_block(t.shape)
    return jnp.concatenate([jnp.where(blk == g, t, jnp.zeros_like(t)).astype(dtype) for g in range(GROUPS)], axis=0)


def _fold_groups(ts, tq):
    blk = _lane_block((tq, KV_W))
    out = jnp.zeros((tq, KV_W), ts.dtype)
    for g in range(GROUPS):
        out = jnp.where(blk == g, ts[g * tq:(g + 1) * tq], out)
    return out


def _row_to_col(row):
    hi = row.astype(BF16)
    r1 = row - hi.astype(F32)
    mid = r1.astype(BF16)
    lo = (r1 - mid.astype(F32)).astype(BF16)
    ones = jnp.ones((8, LANES), BF16)
    pad = jnp.zeros((7, row.shape[1]), BF16)
    acc = jnp.zeros((row.shape[1], LANES), F32)
    for term in (hi, mid, lo):
        acc = acc + lax.dot_general(jnp.concatenate([term, pad], axis=0), ones, (((0,), (0,)), ((), ())),
                                    preferred_element_type=F32)
    return acc[:, 0:1]


def _stack_tiles(t, dtype):
    return jnp.concatenate([_stack_groups(t[a:a + TM], dtype) for a in range(0, t.shape[0], TM)], axis=0)


def _fold_tiles(ts, tq):
    return jnp.concatenate([_fold_groups(ts[GROUPS * a:GROUPS * (a + TM)], TM) for a in range(0, tq, TM)], axis=0)


def _compact_tiles_t(tt, dtype):
    return jnp.concatenate([tt[g * HEAD_DIM:(g + 1) * HEAD_DIM, a:a + TM] for a in range(0, tt.shape[1], TM)
                            for g in range(GROUPS)], axis=1).astype(dtype)


def _layer_norm_bwd(dxh, xhat, rstd):
    m1 = jnp.mean(dxh, axis=1, keepdims=True)
    m2 = jnp.mean(dxh * xhat, axis=1, keepdims=True)
    return rstd * (dxh - m1 - xhat * m2)


def _colsum(a):
    return jnp.sum(a, axis=0, keepdims=True)


def _shifted_rows(t, prev_row, next_row):
    n = t.shape[0]
    row = lax.broadcasted_iota(jnp.int32, t.shape, 0)
    up = jnp.where(row == 0, prev_row, pltpu.roll(t, 1, 0))
    dn = jnp.where(row == n - 1, next_row, pltpu.roll(t, n - 1, 0))
    return up, dn


def mod_fwd(cs, w_sh, b_sh):
    def kern(c_ref, w_ref, b_ref, o_ref):
        o_ref[...] = _dot(jax.nn.silu(c_ref[...]), w_ref[...]) + b_ref[...]

    return pl.pallas_call(kern, name="mod_fwd", out_shape=SDS((16, w_sh.shape[1]), F32),
                          compiler_params=pltpu.CompilerParams(vmem_limit_bytes=VMEM_LIMIT))(cs, w_sh, b_sh)


def mod_bwd(cs, w_sh, dm_sh, dm_all):
    hp = lax.Precision.HIGHEST

    def kern(c_ref, w_ref, dm_ref, da_ref, dw_ref, dc_ref, db_ref):
        c = c_ref[...]
        sg = jax.nn.sigmoid(c)
        sc = c * sg
        dm = dm_ref[...]
        dmc = dm_ref[8:9, :]
        for i in range(9, 16):
            dmc = dmc + dm_ref[i:i + 1, :]
        row = lax.broadcasted_iota(jnp.int32, dm.shape, 0)
        a = jnp.where(row < 8, dm, jnp.where(row == 8, dmc, 0.0))
        dw_ref[...] = lax.dot_general(sc, a, (((0,), (0,)), ((), ())), precision=hp, preferred_element_type=F32)
        dsc = lax.dot_general(a, w_ref[...], (((1,), (1,)), ((), ())), precision=hp, preferred_element_type=F32)
        dc_ref[...] = dsc * (sg * (1.0 + c * (1.0 - sg)))
        db = da_ref[0:1, :]
        for i in range(1, 16):
            db = db + da_ref[i:i + 1, :]
        db_ref[...] = db

    return pl.pallas_call(
        kern, name="mod_bwd",
        out_shape=[SDS(w_sh.shape, F32), SDS((16, D), F32), SDS((1, dm_all.shape[1]), F32)],
        compiler_params=pltpu.CompilerParams(vmem_limit_bytes=VMEM_LIMIT))(cs, w_sh, dm_sh, dm_all)


M_SHIFT1, M_SCALE1, M_SHIFTC, M_SCALEC, M_GATE1, M_SHIFT2, M_SCALE2, M_GATE2 = range(8)


def _mrow(ref, k):
    return ref[k:k + 1, :]


def inproj_fwd(xa, cos, sin, modrows, w_ext, b_ext, qg, kg, bd, n_lat_tiles):
    n = xa.shape[0]

    def body(t, vals, fr):
        x, cs, sn = (v[...] for v in vals)
        mod, w, b, qg_r, kg_r, bd_r = fr
        is_ctx = t >= n_lat_tiles
        shift = jnp.where(is_ctx, _mrow(mod, M_SHIFTC), _mrow(mod, M_SHIFT1))
        scale = jnp.where(is_ctx, _mrow(mod, M_SCALEC), _mrow(mod, M_SCALE1))
        hb = (x * (1.0 + scale) + shift).astype(MXU_DTYPE)
        proj = jnp.dot(hb, w[...], preferred_element_type=F32) + b[...]
        cos4 = jnp.concatenate([cs] * 4, axis=1)
        sin4 = jnp.concatenate([sn] * 4, axis=1)
        qa = _rope(proj[:, X_QA:X_QA + Q_W], cos4, sin4) * Q_SCALE
        ka = _rope(proj[:, X_KA:X_KA + Q_W], cos4, sin4)
        va = proj[:, X_VA:X_VA + Q_W]
        tq = proj[:, X_QB:X_QB + Q_W]
        rq = lax.rsqrt(_seg_sum64(tq * tq, bd_r) * (1.0 / HEAD_DIM) + QK_EPS)
        qb = _rope(tq * rq * qg_r[...], cos4, sin4) * Q_SCALE
        tk = proj[:, X_KB:X_KB + Q_W]
        rk = lax.rsqrt(_seg_sum64(tk * tk, bd_r) * (1.0 / HEAD_DIM) + QK_EPS)
        kb = _rope(tk * rk * kg_r[...], cos4, sin4)
        vb = proj[:, X_VB:X_VB + Q_W]
        gl = proj[:, X_GL:]
        return [hb, qa, ka, va, qb, kb, vb, tq, rq, tk, rk, gl], []

    mx = MXU_DTYPE
    outs = [((n, D), mx, _rows(TM, D))] + [((n, Q_W), mx, _rows(TM, Q_W))] * 6 + \
           [((n, Q_W), F32, _rows(TM, Q_W))] * 4 + [((n, 2 * D), F32, _rows(TM, 2 * D))]
    res, _ = rowwise("inproj_fwd", body, ntiles=n // TM,
                     tiled=[(xa, _rows(TM, D)), (cos, _rows(TM, LANES)), (sin, _rows(TM, LANES))],
                     full=[modrows, w_ext, b_ext, qg, kg, bd], outs=outs)
    return res


def merge_fwd(oa, ob, gl, x, modrows, wba, wbb, w_out, s_rows):
    def body(t, vals, fr):
        oa_, ob_, gl_, x_ = (v[...] for v in vals)
        mod, wa, wb, wo = fr
        ya = _dot(oa_, wa[...])
        yb = _dot(ob_, wb[...])
        ga = jax.nn.sigmoid(gl_[:, :D])
        gb = jax.nn.sigmoid(gl_[:, D:])
        mrg = ga * ya + gb * yb
        y = _dot(mrg, wo[...])
        r1 = ALPHA * x_ + _mrow(mod, M_GATE1) * y
        mu = jnp.mean(r1, axis=1, keepdims=True)
        xc = r1 - mu
        var = jnp.mean(xc * xc, axis=1, keepdims=True)
        rstd = lax.rsqrt(var + LN_EPS)
        xhat = xc * rstd
        return [ya, yb, mrg, y, xhat, rstd], []

    outs = [((s_rows, D), F32, _rows(TM, D))] * 2 + [((s_rows, D), MXU_DTYPE, _rows(TM, D))] + \
           [((s_rows, D), F32, _rows(TM, D))] * 2 + [((s_rows, 1), F32, _rows(TM, 1))]
    res, _ = rowwise("merge_fwd", body, ntiles=s_rows // TM,
                     tiled=[(oa, _rows(TM, Q_W)), (ob, _rows(TM, Q_W)), (gl, _rows(TM, 2 * D)), (x, _rows(TM, D))],
                     full=[modrows, wba, wbb, w_out], outs=outs)
    return res


def ffn_up_fwd(xhat1, modrows, ln_g, ln_b, w_up, s_rows):
    def body(t, vals, fr):
        xh = vals[0][...]
        mod, g_r, b_r, w = fr
        x1 = xh * g_r[...] + b_r[...]
        h2 = (x1 * (1.0 + _mrow(mod, M_SCALE2)) + _mrow(mod, M_SHIFT2)).astype(MXU_DTYPE)
        return [h2, jnp.dot(h2, w[...], preferred_element_type=F32)], []

    res, _ = rowwise("ffn_up_fwd", body, ntiles=s_rows // TM, tiled=[(xhat1, _rows(TM, D))],
                     full=[modrows, ln_g, ln_b, w_up],
                     outs=[((s_rows, D), MXU_DTYPE, _rows(TM, D)), ((s_rows, 2 * FF), F32, _rows(TM, 2 * FF))])
    return res


TC = 128


def _halo_specs(tm, w, s_rows):
    per = tm // 8
    last = s_rows // 8 - 1
    return (pl.BlockSpec((8, w), lambda i: (jnp.maximum(i * per - 1, 0), 0)),
            pl.BlockSpec((8, w), lambda i: (jnp.minimum((i + 1) * per, last), 0)))


def _halo_rows(t, ntiles, prev_ref, next_ref):
    prev_row = jnp.where(t == 0, 0.0, prev_ref[7:8, :].astype(F32))
    next_row = jnp.where(t == ntiles - 1, 0.0, next_ref[0:1, :].astype(F32))
    return prev_row, next_row


def conv_swiglu_fwd(u0, conv_w8, conv_b, s_rows):
    w2 = 2 * FF
    nt = s_rows // TC

    def body(t, vals, fr):
        u_ref, pv, nx = vals
        cw, cb = fr
        u = u_ref[...]
        up, dn = _shifted_rows(u, *_halo_rows(t, nt, pv, nx))
        uc = cw[0:1, :] * up + cw[1:2, :] * u + cw[2:3, :] * dn + cb[...]
        gate, val = uc[:, :FF], uc[:, FF:]
        return [gate * jax.nn.sigmoid(gate) * val], []

    hp, hn = _halo_specs(TC, w2, s_rows)
    res, _ = rowwise("conv_swiglu_fwd", body, ntiles=nt,
                     tiled=[(u0, _rows(TC, w2)), (u0, hp), (u0, hn)], full=[conv_w8, conv_b],
                     outs=[((s_rows, FF), MXU_DTYPE, _rows(TC, FF))])
    return res[0]


def ffn_down_loss(a, xhat1, target, modrows, ln1_g, ln1_b, ln2_g, ln2_b, w_down, s_rows):
    def body(t, vals, fr):
        a_, xh1, tgt = (v[...] for v in vals)
        mod, g1, b1, g2, b2, wd = fr
        y2 = jnp.dot(a_, wd[...], preferred_element_type=F32)
        x1 = xh1 * g1[...] + b1[...]
        gate2 = _mrow(mod, M_GATE2)
        r2 = ALPHA * x1 + gate2 * y2
        mu = jnp.mean(r2, axis=1, keepdims=True)
        xc = r2 - mu
        var = jnp.mean(xc * xc, axis=1, keepdims=True)
        rstd = lax.rsqrt(var + LN_EPS)
        xhat = xc * rstd
        out = xhat * g2[...] + b2[...]
        diff = out - tgt
        loss = 0.5 * jnp.sum(jnp.mean(diff * diff, axis=1, keepdims=True), axis=0, keepdims=True)
        dout = diff * (1.0 / D)
        dr2 = _layer_norm_bwd(dout * g2[...], xhat, rstd)
        incs = [loss, _colsum(dout * xhat), _colsum(dout), _colsum(dr2 * y2)]
        return [dr2, dr2 * gate2], incs

    res, accs = rowwise("ffn_down_loss", body, ntiles=s_rows // TM,
                        tiled=[(a, _rows(TM, FF)), (xhat1, _rows(TM, D)), (target, _rows(TM, D))],
                        full=[modrows, ln1_g, ln1_b, ln2_g, ln2_b, w_down],
                        outs=[((s_rows, D), F32, _rows(TM, D)), ((s_rows, D), MXU_DTYPE, _rows(TM, D))],
                        accs=[(1, 1), (1, D), (1, D), (1, D)])
    return res, accs


def ffn_down_bwd(dy2, w_down_t, s_rows):
    def body(t, vals, fr):
        return [jnp.dot(vals[0][...], fr[0][...], preferred_element_type=F32)], []

    res, _ = rowwise("ffn_down_bwd", body, ntiles=s_rows // TM, tiled=[(dy2, _rows(TM, D))], full=[w_down_t],
                     outs=[((s_rows, FF), F32, _rows(TM, FF))])
    return res[0]


def swiglu_conv_bwd(u0, da, conv_w8, conv_b, s_rows):
    w2 = 2 * FF
    nt = s_rows // TC
    n = TC + 16

    def body(t, vals, fr):
        u_ref, upv, unx, da_ref, apv, anx = vals
        cw, cb = fr
        first, last = t == 0, t == nt - 1
        ue = jnp.concatenate([jnp.where(first, 0.0, upv[...]), u_ref[...], jnp.where(last, 0.0, unx[...])], axis=0)
        ae = jnp.concatenate([jnp.where(first, 0.0, apv[...]), da_ref[...], jnp.where(last, 0.0, anx[...])], axis=0)
        up = pltpu.roll(ue, 1, 0)
        dn = pltpu.roll(ue, n - 1, 0)
        uc = cw[0:1, :] * up + cw[1:2, :] * ue + cw[2:3, :] * dn + cb[...]
        gate, val = uc[:, :FF], uc[:, FF:]
        sg = jax.nn.sigmoid(gate)
        du = jnp.concatenate([ae * val * (sg * (1.0 + gate * (1.0 - sg))), ae * (gate * sg)], axis=1)
        du0 = cw[0:1, :] * pltpu.roll(du, n - 1, 0) + cw[1:2, :] * du + cw[2:3, :] * pltpu.roll(du, 1, 0)
        rows = slice(8, 8 + TC)
        dut = du[rows]
        return [du0[rows]], [_colsum(dut), _colsum(up[rows] * dut), _colsum(ue[rows] * dut), _colsum(dn[rows] * dut)]

    hp, hn = _halo_specs(TC, w2, s_rows)
    ap, an = _halo_specs(TC, FF, s_rows)
    res, accs = rowwise("swiglu_conv_bwd", body, ntiles=nt,
                        tiled=[(u0, _rows(TC, w2)), (u0, hp), (u0, hn), (da, _rows(TC, FF)), (da, ap), (da, an)],
                        full=[conv_w8, conv_b], outs=[((s_rows, w2), MXU_DTYPE, _rows(TC, w2))], accs=[(1, w2)] * 4)
    return res[0], accs


def ffn_up_ln1_bwd(du0, dr2, xhat1, y, rstd1, modrows, ln_g, ln_b, w_up_t, s_rows):
    def body(t, vals, fr):
        du0_, dr2_, xh, y_, rstd = (v[...] for v in vals)
        mod, g_r, b_r, wt = fr
        dh2 = jnp.dot(du0_, wt[...], preferred_element_type=F32)
        x1 = xh * g_r[...] + b_r[...]
        dx1 = ALPHA * dr2_ + dh2 * (1.0 + _mrow(mod, M_SCALE2))
        dr1 = _layer_norm_bwd(dx1 * g_r[...], xh, rstd)
        incs = [_colsum(dh2 * x1), _colsum(dh2), _colsum(dx1 * xh), _colsum(dx1), _colsum(dr1 * y_)]
        return [dr1 * _mrow(mod, M_GATE1), ALPHA * dr1], incs

    res, accs = rowwise("ffn_up_ln1_bwd", body, ntiles=s_rows // TM,
                        tiled=[(du0, _rows(TM, 2 * FF)), (dr2, _rows(TM, D)), (xhat1, _rows(TM, D)), (y, _rows(TM, D)),
                               (rstd1, _rows(TM, 1))],
                        full=[modrows, ln_g, ln_b, w_up_t],
                        outs=[((s_rows, D), MXU_DTYPE, _rows(TM, D)), ((s_rows, D), F32, _rows(TM, D))],
                        accs=[(1, D)] * 5)
    return res, accs


def merge_bwd(dy, ya, yb, gl, w_out_t, wba_t, wbb_t, s_rows):
    def body(t, vals, fr):
        dy_, ya_, yb_, gl_ = (v[...] for v in vals)
        wot, wat, wbt = fr
        dmrg = jnp.dot(dy_, wot[...], preferred_element_type=F32)
        ga = jax.nn.sigmoid(gl_[:, :D])
        gb = jax.nn.sigmoid(gl_[:, D:])
        dya = dmrg * ga
        dyb = dmrg * gb
        dgl = jnp.concatenate([dmrg * ya_ * ga * (1.0 - ga), dmrg * yb_ * gb * (1.0 - gb)], axis=1)
        return [dya, dyb, dgl, _dot(dya, wat[...]), _dot(dyb, wbt[...])], []

    mx = MXU_DTYPE
    res, _ = rowwise("merge_bwd", body, ntiles=s_rows // TM,
                     tiled=[(dy, _rows(TM, D)), (ya, _rows(TM, D)), (yb, _rows(TM, D)), (gl, _rows(TM, 2 * D))],
                     full=[w_out_t, wba_t, wbb_t],
                     outs=[((s_rows, D), mx, _rows(TM, D))] * 2 + [((s_rows, 2 * D), F32, _rows(TM, 2 * D))] +
                          [((s_rows, Q_W), F32, _rows(TM, Q_W))] * 2)
    return res


def qk_bwd(dqa, dka_t, dva_t, dqb_heads, dkb_t, dvb_t, dgl, tq, rq, tk, rk, cos, sin, qg, kg, bd, place, n_lat_tiles, n):
    def placed(xt_ref, place_ref):
        xt = xt_ref[...]
        hi = xt.astype(BF16)
        r1 = xt - hi.astype(F32)
        mid = r1.astype(BF16)
        lo = (r1 - mid.astype(F32)).astype(BF16)
        pm = place_ref[...]
        return sum(lax.dot_general(term, pm, (((0,), (0,)), ((), ())), preferred_element_type=F32) for term in (hi, mid, lo))

    def body(t, vals, fr):
        dqa_, dgl_, tq_, rq_, tk_, rk_, cs, sn = (v[...] for v in vals[:8])
        qg_r, kg_r, bd_r, pl_r = fr
        dka_, dva_, dkb_, dvb_ = (placed(v, pl_r) for v in vals[8:12])
        dqb_ = jnp.concatenate([v[...] for v in vals[12:]], axis=1)
        is_ctx = t >= n_lat_tiles
        cos4 = jnp.concatenate([cs] * 4, axis=1)
        sin4 = jnp.concatenate([sn] * 4, axis=1)
        zero = jnp.zeros_like(dqa_)
        dpqa = jnp.where(is_ctx, zero, _rope_t(dqa_, cos4, sin4) * Q_SCALE)
        dpka = _rope_t(dka_, cos4, sin4)
        dpva = dva_
        dnq = jnp.where(is_ctx, zero, _rope_t(dqb_, cos4, sin4) * Q_SCALE)
        gq = qg_r[...] * dnq
        dtq = rq_ * gq - tq_ * (rq_ * rq_ * rq_) * (_seg_sum64(gq * tq_, bd_r) * (1.0 / HEAD_DIM))
        dnk = _rope_t(dkb_, cos4, sin4)
        gk = kg_r[...] * dnk
        dtk = rk_ * gk - tk_ * (rk_ * rk_ * rk_) * (_seg_sum64(gk * tk_, bd_r) * (1.0 / HEAD_DIM))
        dgl32 = jnp.where(is_ctx, jnp.zeros_like(dgl_), dgl_)
        dproj = jnp.concatenate([dpqa, dpka, dpva, dtq, dtk, dvb_, dgl32], axis=1)
        return [dproj], [_colsum(dproj), _colsum(dnq * tq_ * rq_), _colsum(dnk * tk_ * rk_)]

    lat = lambda t: jnp.minimum(t, n_lat_tiles - 1)
    qs = _rows(TM, Q_W)
    ts = pl.BlockSpec((N_KV * HEAD_DIM, TM), lambda i: (0, i))
    res, accs = rowwise(
        "qk_bwd", body, ntiles=n // TM,
        tiled=[(dqa, _rows(TM, Q_W, lat)), (dgl, _rows(TM, 2 * D, lat)),
               (tq, qs), (rq, qs), (tk, qs), (rk, qs), (cos, _rows(TM, LANES)), (sin, _rows(TM, LANES)),
               (dka_t, ts), (dva_t, ts), (dkb_t, ts), (dvb_t, ts)] + [(d, _rows(TM, KV_W, lat)) for d in dqb_heads],
        full=[qg, kg, bd, place], outs=[((n, EXT_COLS), MXU_DTYPE, _rows(TM, EXT_COLS))],
        accs=[(1, EXT_COLS), (1, Q_W), (1, Q_W)])
    return res[0], accs


def inproj_bwd(name, dproj, xa, dxp, modrows, w_ext_t, *, ntiles, tile_off, is_ctx, out_rows):
    kc = M_SCALEC if is_ctx else M_SCALE1

    def body(t, vals, fr):
        dp, x_ = vals[0][...], vals[1][...]
        mod, wt = fr
        dh = jnp.dot(dp, wt[...], preferred_element_type=F32)
        incs = [_colsum(dh * x_), _colsum(dh)]
        if is_ctx:
            return [], incs
        return [vals[2][...] + dh * (1.0 + _mrow(mod, kc))], incs

    tiled = [(dproj, _rows(TM, EXT_COLS, lambda i: i + tile_off)), (xa, _rows(TM, D, lambda i: i + tile_off))]
    outs = []
    if not is_ctx:
        tiled.append((dxp, _rows(TM, D)))
        outs = [((out_rows, D), F32, _rows(TM, D))]
    return rowwise(name, body, ntiles=ntiles, tiled=tiled, full=[modrows, w_ext_t], outs=outs, accs=[(1, D)] * 2)


def _attn_semantics():
    return _cparams(("arbitrary", "arbitrary", "arbitrary"))


GLOB_TK = (1280, 1024, 768, 512, 256)
GLOB_TQ = (512, 256)
GLOB_BWD_TQ = GLOB_TQ
KEY_CHUNK = 256


def glob_fwd(q, kt, v_t, s_rows):
    n = kt.shape[0]
    tq = _pick(s_rows, GLOB_TQ)
    tk = _pick(n, GLOB_TK)
    nq, nk = s_rows // tq, n // tk
    r = GROUPS * tq
    nch = tk // KEY_CHUNK

    def produce(qs, k_ref, s_buf, c, mx):
        rows = slice(c * KEY_CHUNK, (c + 1) * KEY_CHUNK)
        sn = _dot_nt(k_ref[rows, :], qs[...])
        s_buf[rows, :] = sn
        return jnp.maximum(mx, jnp.max(sn, axis=0, keepdims=True))

    def kern(q_ref, k0_ref, kn_ref, vt_ref, o_ref, lse_ref, qs, s_buf, mx_buf, m_s, l_s, acc):
        j = pl.program_id(2)

        @pl.when(j == 0)
        def _():
            qs[...] = _stack_tiles(q_ref[...], qs.dtype)
            mx = jnp.full((1, r), -jnp.inf, F32)
            for c in range(nch):
                mx = produce(qs, k0_ref, s_buf, c, mx)
            mx_buf[...] = mx
            m_s[...] = jnp.full_like(m_s, -jnp.inf)
            l_s[...] = jnp.zeros_like(l_s)
            acc[...] = jnp.zeros_like(acc)

        m_prev = m_s[...]
        m_new = jnp.maximum(m_prev, mx_buf[...])
        alpha = jnp.exp(m_prev - m_new)
        a = alpha * acc[...]
        ls = alpha * l_s[...]
        mx = jnp.full((1, r), -jnp.inf, F32)
        for c in range(nch):
            rows = slice(c * KEY_CHUNK, (c + 1) * KEY_CHUNK)
            p = jnp.exp(s_buf[rows, :] - m_new)
            ls = ls + jnp.sum(p, axis=0, keepdims=True)
            a = a + jnp.dot(vt_ref[0, :, rows], p.astype(MXU_DTYPE), preferred_element_type=F32)
            mx = produce(qs, kn_ref, s_buf, c, mx)
        mx_buf[...] = mx
        l_s[...] = ls
        acc[...] = a
        m_s[...] = m_new

        @pl.when(j == nk - 1)
        def _():
            o_t = acc[...] / l_s[...]
            o_ref[...] = jnp.concatenate([_untranspose_groups(o_t[:, GROUPS * a:GROUPS * (a + TM)], TM)
                                          for a in range(0, tq, TM)], axis=0)
            lse_ref[0, 0] = _row_to_col(m_s[...] + jnp.log(l_s[...]))

    kspec = lambda f: pl.BlockSpec((tk, KV_W), lambda h, i, j: (f(j), h))
    return pl.pallas_call(
        kern, name="glob_fwd", grid=(N_KV, nq, nk),
        in_specs=[pl.BlockSpec((tq, KV_W), lambda h, i, j: (i, h)), kspec(lambda j: 0),
                  kspec(lambda j: jnp.minimum(j + 1, nk - 1)), pl.BlockSpec((1, HEAD_DIM, tk), lambda h, i, j: (h, 0, j))],
        out_specs=[pl.BlockSpec((tq, KV_W), lambda h, i, j: (i, h)),
                   pl.BlockSpec((1, 1, r, 1), lambda h, i, j: (h, i, 0, 0))],
        out_shape=[SDS((s_rows, Q_W), F32), SDS((N_KV, nq, r, 1), F32)],
        scratch_shapes=[pltpu.VMEM((r, KV_W), MXU_DTYPE), pltpu.VMEM((tk, r), F32), pltpu.VMEM((1, r), F32),
                        pltpu.VMEM((1, r), F32), pltpu.VMEM((1, r), F32), pltpu.VMEM((HEAD_DIM, r), F32)],
        compiler_params=_attn_semantics(),
    )(q, kt, kt, v_t)


def attn_delta(o, do, s_rows):
    tq = _pick(s_rows, GLOB_BWD_TQ)
    nq = s_rows // tq
    r = GROUPS * tq

    def kern(o_ref, do_ref, d_ref):
        d_ref[0, 0] = jnp.sum(_stack_tiles(do_ref[...], F32) * _stack_tiles(o_ref[...], F32), axis=1, keepdims=True)

    qspec = pl.BlockSpec((tq, KV_W), lambda h, i: (i, h))
    return pl.pallas_call(
        kern, name="attn_delta", grid=(N_KV, nq), in_specs=[qspec, qspec],
        out_specs=pl.BlockSpec((1, 1, r, 1), lambda h, i: (h, i, 0, 0)), out_shape=SDS((N_KV, nq, r, 1), F32),
        compiler_params=_cparams(("parallel", "parallel")),
    )(o, do)


def _compact_t(tt, dtype):
    return jnp.concatenate([tt[g * HEAD_DIM:(g + 1) * HEAD_DIM, :] for g in range(GROUPS)], axis=1).astype(dtype)


def glob_bwd(q, q_t, kt, vt, do, do_t, lse, delta, h, s_rows):
    n = kt.shape[0]
    tq = _pick(s_rows, GLOB_BWD_TQ)
    tk = _pick(n, GLOB_TK)
    nq, nk = s_rows // tq, n // tk
    r = GROUPS * tq
    nch = tk // KEY_CHUNK

    def kern(q_ref, qt_ref, k_ref, v_ref, do_ref, dot_ref, lse_ref, dl_ref, dq_ref, dkt_ref, dvt_ref, p_buf, ds_buf):
        j = pl.program_id(0)
        i = pl.program_id(1)

        @pl.when(i == 0)
        def _():
            dkt_ref[...] = jnp.zeros_like(dkt_ref)
            dvt_ref[...] = jnp.zeros_like(dvt_ref)

        qs = _stack_tiles(q_ref[...], MXU_DTYPE)
        dos = _stack_tiles(do_ref[...], MXU_DTYPE)
        lse_b = jnp.broadcast_to(lse_ref[0, 0], (r, LANES))
        dl_b = jnp.broadcast_to(dl_ref[0, 0], (r, LANES))
        for c in range(nch):
            lo = c * KEY_CHUNK
            sc = _dot_nt(qs, k_ref[lo:lo + KEY_CHUNK, :])
            dpc = _dot_nt(dos, v_ref[lo:lo + KEY_CHUNK, :])
            for t in range(KEY_CHUNK // LANES):
                sl = slice(t * LANES, (t + 1) * LANES)
                pt = jnp.exp(sc[:, sl] - lse_b)
                p_buf[:, lo + t * LANES:lo + (t + 1) * LANES] = pt.astype(p_buf.dtype)
                ds_buf[:, lo + t * LANES:lo + (t + 1) * LANES] = (pt * (dpc[:, sl] - dl_b)).astype(ds_buf.dtype)
        dq_t = _fold_tiles(jnp.dot(ds_buf[...], k_ref[...], preferred_element_type=F32), tq)
        rows = pl.ds(pl.multiple_of(i * tq, tq), tq)

        @pl.when(j == 0)
        def _():
            dq_ref[rows, :] = dq_t

        @pl.when(j > 0)
        def _():
            dq_ref[rows, :] += dq_t

        dvt_ref[...] += jnp.dot(_compact_tiles_t(dot_ref[...], MXU_DTYPE), p_buf[...], preferred_element_type=F32)
        dkt_ref[...] += jnp.dot(_compact_tiles_t(qt_ref[...], MXU_DTYPE), ds_buf[...], preferred_element_type=F32)

    col = pl.BlockSpec((1, 1, r, 1), lambda j, i: (h, i, 0, 0))
    qspec = pl.BlockSpec((tq, KV_W), lambda j, i: (i, h))
    tspec = pl.BlockSpec((KV_W, tq), lambda j, i: (h, i))
    kspec = pl.BlockSpec((tk, KV_W), lambda j, i: (j, h))
    ospec = pl.BlockSpec((HEAD_DIM, tk), lambda j, i: (0, j))
    return pl.pallas_call(
        kern, name=f"glob_bwd_h{h}", grid=(nk, nq),
        in_specs=[qspec, tspec, kspec, kspec, qspec, tspec, col, col],
        out_specs=[pl.BlockSpec(memory_space=pltpu.VMEM), ospec, ospec],
        out_shape=[SDS((s_rows, KV_W), F32), SDS((HEAD_DIM, n), F32), SDS((HEAD_DIM, n), F32)],
        scratch_shapes=[pltpu.VMEM((r, tk), MXU_DTYPE), pltpu.VMEM((r, tk), MXU_DTYPE)],
        compiler_params=_cparams(("arbitrary", "arbitrary")),
    )(q, q_t, kt, vt, do, do_t, lse, delta)


TW = 2 * WIN
WR = GROUPS * TW
WLAT = 4 * WIN


def _win_cat(dst, parts):
    off = 0
    for p in parts:
        dst[off:off + p.shape[0], :] = p[...]
        off += p.shape[0]


def _win_specs(s_rows, c_rows):
    nb = s_rows // WIN
    prev = lambda i: jnp.maximum(2 * i - 1, 0)
    nxt = lambda i: jnp.minimum(2 * i + 2, nb - 1)
    rows = [pl.BlockSpec((WIN, KV_W), lambda h, i: (prev(i), h)), pl.BlockSpec((TW, KV_W), lambda h, i: (i, h)),
            pl.BlockSpec((WIN, KV_W), lambda h, i: (nxt(i), h)), pl.BlockSpec((c_rows, KV_W), lambda h, i: (s_rows // c_rows, h))]
    cols = [pl.BlockSpec((1, HEAD_DIM, WIN), lambda h, i: (h, 0, prev(i))), pl.BlockSpec((1, HEAD_DIM, TW), lambda h, i: (h, 0, i)),
            pl.BlockSpec((1, HEAD_DIM, WIN), lambda h, i: (h, 0, nxt(i))),
            pl.BlockSpec((1, HEAD_DIM, c_rows), lambda h, i: (h, 0, s_rows // c_rows))]
    return rows, cols


def _win_mask(i, s_rows, shape, keys_on_rows):
    a = lax.broadcasted_iota(jnp.int32, shape, 0)
    b = lax.broadcasted_iota(jnp.int32, shape, 1)
    kk, qq = (a, b) if keys_on_rows else (b, a)
    qpos = i * TW + (qq & (TW - 1))
    kpos = (2 * i - 1) * WIN + kk
    band = (jnp.abs(qpos - kpos) <= WIN) & (kpos >= 0) & (kpos < s_rows)
    return (kk >= WLAT) | band


def _untranspose_groups(o_t, tq):
    row = lax.broadcasted_iota(jnp.int32, (HEAD_DIM, KV_W), 0)
    col = lax.broadcasted_iota(jnp.int32, (HEAD_DIM, KV_W), 1)
    hi = o_t.astype(BF16)
    r1 = o_t - hi.astype(F32)
    mid = r1.astype(BF16)
    lo = (r1 - mid.astype(F32)).astype(BF16)
    o = jnp.zeros((tq, KV_W), F32)
    for g in range(GROUPS):
        sel = jnp.where(col == row + g * HEAD_DIM, 1.0, 0.0).astype(BF16)
        for term in (hi, mid, lo):
            o = o + lax.dot_general(term[:, g * tq:(g + 1) * tq], sel, (((0,), (0,)), ((), ())), preferred_element_type=F32)
    return o


def win_fwd(q, kt, v_t, sinkrow, s_rows, c_rows):
    nt = s_rows // TW
    nkeys = WLAT + c_rows

    def kern(q_ref, kp, kc, kn, kx, vp, vc, vn, vx, sink_ref, o_ref, lse_ref, kcat):
        i = pl.program_id(1)
        _win_cat(kcat, (kp, kc, kn, kx))
        qs = _stack_groups(q_ref[...], MXU_DTYPE)
        st = _dot_nt(kcat[...], qs)
        st = jnp.where(_win_mask(i, s_rows, st.shape, True), st, -jnp.inf)
        sink = sink_ref[0]
        m = jnp.maximum(jnp.max(st, axis=0, keepdims=True), sink)
        e = jnp.exp(st - m)
        den = jnp.sum(e, axis=0, keepdims=True) + jnp.exp(sink - m)
        v_cat = jnp.concatenate([vp[0], vc[0], vn[0], vx[0]], axis=1)
        o_t = jnp.dot(v_cat, e.astype(MXU_DTYPE), preferred_element_type=F32) / den
        o_ref[...] = _untranspose_groups(o_t, TW)
        lse_ref[0, 0] = _row_to_col(m + jnp.log(den))

    rows, cols = _win_specs(s_rows, c_rows)
    qspec = pl.BlockSpec((TW, KV_W), lambda h, i: (i, h))
    rowv = pl.BlockSpec((1, 1, WR, 1), lambda h, i: (h, i, 0, 0))
    return pl.pallas_call(
        kern, name="win_fwd", grid=(N_KV, nt),
        in_specs=[qspec] + rows + cols + [pl.BlockSpec((1, 1, WR), lambda h, i: (h, 0, 0))],
        out_specs=[qspec, rowv], out_shape=[SDS((s_rows, Q_W), F32), SDS((N_KV, nt, WR, 1), F32)],
        scratch_shapes=[pltpu.VMEM((nkeys, KV_W), MXU_DTYPE)],
        compiler_params=_cparams(("parallel", "parallel")),
    )(q, kt, kt, kt, kt, v_t, v_t, v_t, v_t, sinkrow)


def win_bwd(q, q_t, kt, vt, sinkcol, o, do, do_t, lse, s_rows, c_rows):
    nt = s_rows // TW
    nkeys = WLAT + c_rows
    n = s_rows + c_rows
    ctx0 = WIN + s_rows

    def kern(q_ref, qt_ref, kp, kc, kn, kx, vp, vc, vn, vx, sink_ref, o_ref, do_ref, dot_ref, lse_ref,
             dq_ref, dkt_ref, dvt_ref, dsk_ref, kcat, vcat):
        i = pl.program_id(1)

        @pl.when(i == 0)
        def _():
            dkt_ref[...] = jnp.zeros_like(dkt_ref)
            dvt_ref[...] = jnp.zeros_like(dvt_ref)
            dsk_ref[...] = jnp.zeros_like(dsk_ref)

        _win_cat(kcat, (kp, kc, kn, kx))
        _win_cat(vcat, (vp, vc, vn, vx))
        qs = _stack_groups(q_ref[...], MXU_DTYPE)
        do32 = _stack_groups(do_ref[...], F32)
        delta = jnp.sum(do32 * _stack_groups(o_ref[...], F32), axis=1, keepdims=True)
        dos = do32.astype(MXU_DTYPE)
        lse_c = lse_ref[0, 0]
        s = _dot_nt(qs, kcat[...])
        s = jnp.where(_win_mask(i, s_rows, s.shape, False), s, -jnp.inf)
        p = jnp.exp(s - lse_c)
        ds = p * (_dot_nt(dos, vcat[...]) - delta)
        dq_ref[...] = _fold_groups(_dot(ds, kcat[...]), TW)
        dvt = jnp.dot(_compact_t(dot_ref[...], MXU_DTYPE), p.astype(MXU_DTYPE), preferred_element_type=F32)
        dkt = jnp.dot(_compact_t(qt_ref[...], MXU_DTYPE), ds.astype(MXU_DTYPE), preferred_element_type=F32)
        lat = pl.ds(pl.multiple_of(i * TW, TW), WLAT)
        dkt_ref[0, :, lat] += dkt[:, :WLAT]
        dvt_ref[0, :, lat] += dvt[:, :WLAT]
        dkt_ref[0, :, ctx0:ctx0 + c_rows] += dkt[:, WLAT:]
        dvt_ref[0, :, ctx0:ctx0 + c_rows] += dvt[:, WLAT:]
        dsk_ref[0] += -(jnp.exp(sink_ref[0][:, 0:1] - lse_c) * delta)

    rows, _ = _win_specs(s_rows, c_rows)
    qspec = pl.BlockSpec((TW, KV_W), lambda h, i: (i, h))
    tspec = pl.BlockSpec((KV_W, TW), lambda h, i: (h, i))
    col = pl.BlockSpec((1, 1, WR, 1), lambda h, i: (h, i, 0, 0))
    kvt = pl.BlockSpec((1, HEAD_DIM, WIN + n), lambda h, i: (h, 0, 0))
    return pl.pallas_call(
        kern, name="win_bwd", grid=(N_KV, nt),
        in_specs=[qspec, tspec] + rows + rows + [pl.BlockSpec((1, WR, LANES), lambda h, i: (h, 0, 0)), qspec, qspec, tspec, col],
        out_specs=[qspec, kvt, kvt, pl.BlockSpec((1, WR, 1), lambda h, i: (h, 0, 0))],
        out_shape=[SDS((s_rows, Q_W), F32), SDS((N_KV, HEAD_DIM, WIN + n), F32), SDS((N_KV, HEAD_DIM, WIN + n), F32),
                   SDS((N_KV, WR, 1), F32)],
        scratch_shapes=[pltpu.VMEM((nkeys, KV_W), MXU_DTYPE), pltpu.VMEM((nkeys, KV_W), MXU_DTYPE)],
        compiler_params=_cparams(("arbitrary", "arbitrary")),
    )(q, q_t, kt, kt, kt, kt, vt, vt, vt, vt, sinkcol, o, do, do_t, lse)


def adamw(name, w, m, v, grads):
    r, wd = w.shape
    tr = _pick(r, [t for t in ELEMENTWISE_ROWS if t * wd * 4 <= ELEMENTWISE_BLOCK_BYTES])
    stacked = not isinstance(grads, (list, tuple))
    ng = grads.shape[0] if stacked else len(grads)

    def kern(*refs):
        w_ref, m_ref, v_ref = refs[:3]
        g_refs = refs[3:-4]
        g_out, d_out, m_out, v_out = refs[-4:]
        if stacked:
            g = g_refs[0][0]
            for k in range(1, ng):
                g = g + g_refs[0][k]
        else:
            g = g_refs[0][...]
            for gr in g_refs[1:]:
                g = g + gr[...]
        wv = w_ref[...]
        mn = ADAM_B1 * m_ref[...] + (1.0 - ADAM_B1) * g
        vn = ADAM_B2 * v_ref[...] + (1.0 - ADAM_B2) * (g * g)
        m_hat = mn / (1.0 - ADAM_B1 ** ADAM_STEP)
        v_hat = vn / (1.0 - ADAM_B2 ** ADAM_STEP)
        g_out[...] = g
        d_out[...] = -ADAM_LR * (m_hat / (jnp.sqrt(v_hat) + ADAM_EPS) + ADAM_WD * wv)
        m_out[...] = mn
        v_out[...] = vn

    spec = pl.BlockSpec((tr, wd), lambda i: (i, 0))
    gspecs = [pl.BlockSpec((ng, tr, wd), lambda i: (0, i, 0))] if stacked else [spec] * ng
    return pl.pallas_call(
        kern, name=name, grid=(r // tr,), in_specs=[spec] * 3 + gspecs, out_specs=[spec] * 4,
        out_shape=[SDS((r, wd), F32)] * 4, compiler_params=_cparams(("parallel",)),
    )(w, m, v, *([grads] if stacked else grads))


def add2(name, a, b):
    k, r, w = a.shape
    tr = _pick(r, [t for t in ELEMENTWISE_ROWS if t * w * 4 <= ELEMENTWISE_BLOCK_BYTES])

    def kern(a_ref, b_ref, o_ref):
        o_ref[...] = a_ref[...] + b_ref[...]

    spec = pl.BlockSpec((1, tr, w), lambda s, i: (s, i, 0))
    return pl.pallas_call(kern, name=name, grid=(k, r // tr), in_specs=[spec, spec], out_specs=spec,
                          out_shape=SDS(a.shape, a.dtype), compiler_params=_cparams(("parallel", "parallel")))(a, b)


def _rep4(a, off):
    return jnp.concatenate([a[:, off + HEAD_DIM * h: off + HEAD_DIM * (h + 1)] for h in range(N_KV) for _ in range(GROUPS)], axis=1)


def _extend_cols(a):
    return jnp.concatenate([a[:, 0:OFF_KA], _rep4(a, OFF_KA), _rep4(a, OFF_VA), a[:, OFF_QB:OFF_KB],
                            _rep4(a, OFF_KB), _rep4(a, OFF_VB), a[:, OFF_GA:]], axis=1)


def _fold4(a, off):
    r = a.shape[0]
    return a[:, off:off + Q_W].reshape(r, N_KV, GROUPS, HEAD_DIM).sum(axis=2).reshape(r, N_KV * HEAD_DIM)


def _fold_cols(a):
    return jnp.concatenate([a[:, X_QA:X_QA + Q_W], _fold4(a, X_KA), _fold4(a, X_VA), a[:, X_QB:X_QB + Q_W],
                            _fold4(a, X_KB), _fold4(a, X_VB), a[:, X_GL:]], axis=1)


def _rope_tables(s_rows, c_rows):
    n_rows = s_rows // GRID_W
    n_freq = HEAD_DIM // 4
    inv_freq = ROPE_THETA ** (-jnp.arange(n_freq, dtype=F32) / n_freq)
    ang_r = jnp.arange(n_rows, dtype=jnp.int32).astype(F32)[:, None] * inv_freq
    ang_c = jnp.arange(GRID_W, dtype=jnp.int32).astype(F32)[:, None] * inv_freq
    by_row = lambda t: jnp.repeat(t, GRID_W, axis=0)
    by_col = lambda t: jnp.tile(t, (n_rows, 1))
    cos = jnp.concatenate([by_row(jnp.cos(ang_r))] * 2 + [by_col(jnp.cos(ang_c))] * 2, axis=1)
    sin_r, sin_c = by_row(jnp.sin(ang_r)), by_col(jnp.sin(ang_c))
    sin = jnp.concatenate([-sin_r, sin_r, -sin_c, sin_c], axis=1)
    cos = jnp.concatenate([cos, jnp.ones((c_rows, HEAD_DIM), F32)], axis=0)
    sin = jnp.concatenate([sin, jnp.zeros((c_rows, HEAD_DIM), F32)], axis=0)
    return jnp.concatenate([cos, cos], axis=1), jnp.concatenate([sin, sin], axis=1)


def _ff_pad_cols(a):
    r = a.shape[0]
    a = jnp.pad(a.reshape(r, N_DEV, FF_SHARD), ((0, 0), (0, 0), (0, FF_SHARD_PAD - FF_SHARD)))
    return a.reshape(r, 2 * FF)


def _ff_unpad_cols(a):
    r = a.shape[0]
    return a.reshape(r, N_DEV, FF_SHARD_PAD)[:, :, :FF_SHARD].reshape(r, 2 * D_FF)


def _ff_pad_rows(a):
    c = a.shape[1]
    a = jnp.pad(a.reshape(N_DEV // 2, FF_SHARD, c), ((0, 0), (0, FF_SHARD_PAD - FF_SHARD), (0, 0)))
    return a.reshape(FF, c)


def _ff_unpad_rows(a):
    c = a.shape[1]
    return a.reshape(N_DEV // 2, FF_SHARD_PAD, c)[:, :FF_SHARD].reshape(D_FF, c)


BIG = (("w_in", (D, IN_COLS // N_DEV)), ("w_branch_a", (Q_W, D // N_DEV)), ("w_branch_b", (Q_W, D // N_DEV)),
       ("w_out", (D // N_DEV, D)), ("w_up", (D, FF_SHARD_PAD)), ("w_down", (D_FF // N_DEV, D)))
BIG_SIZES = tuple(int(np.prod(s)) for _, s in BIG)
BIG_ROWS = sum(BIG_SIZES) // LANES


def _pack_big(parts):
    return jnp.concatenate([p.reshape(-1) for p in parts]).reshape(BIG_ROWS, LANES)


def _unpack_big(flat):
    lead = flat.shape[:-2]
    f = flat.reshape(*lead, BIG_ROWS * LANES)
    out, off = [], 0
    for (_, shp), sz in zip(BIG, BIG_SIZES, strict=True):
        out.append(f[..., off:off + sz].reshape(*lead, *shp))
        off += sz
    return out


def _cols_to_full(g):
    return jnp.transpose(g, (1, 0, 2)).reshape(g.shape[1], -1)


def _full_to_cols(a):
    r, c = a.shape
    return jnp.transpose(a.reshape(r, N_DEV, c // N_DEV), (1, 0, 2))


SMALL = (("c_ctx", D), ("b_mod", N_MOD * D), ("b_in", IN_COLS), ("attn_sink", N_HEADS), ("q_norm_g", HEAD_DIM),
         ("k_norm_g", HEAD_DIM), ("ln1_g", D), ("ln1_b", D), ("conv_w", 3 * 2 * D_FF // N_DEV), ("conv_b", 2 * D_FF),
         ("ln2_g", D), ("ln2_b", D))
SMALL_TOTAL = sum(n for _, n in SMALL)
SMALL_ROWS = -(-SMALL_TOTAL // (8 * LANES)) * 8


def _pack_small(parts):
    flat = jnp.concatenate([p.reshape(-1).astype(F32) for p in parts])
    return jnp.pad(flat, (0, SMALL_ROWS * LANES - flat.shape[0])).reshape(SMALL_ROWS, LANES)


def _unpack_small(packed):
    f = packed.reshape(-1)
    out, off = {}, 0
    for name, n in SMALL:
        out[name] = f[off:off + n]
        off += n
    return out


RED = (("c_ctx", D), ("b_in", IN_COLS), ("attn_sink", N_HEADS), ("q_norm_g", HEAD_DIM), ("k_norm_g", HEAD_DIM),
       ("ln1_g", D), ("ln1_b", D), ("conv_w", 3 * 2 * FF), ("conv_b", 2 * FF), ("ln2_g", D), ("ln2_b", D))
RED_TOTAL = sum(n for _, n in RED)
RED_ROWS = -(-RED_TOTAL // (8 * LANES)) * 8


def sum8(name, g):
    _, r, w = g.shape

    def kern(g_ref, o_ref):
        acc = g_ref[0]
        for k in range(1, N_DEV):
            acc = acc + g_ref[k]
        o_ref[...] = acc

    return pl.pallas_call(kern, name=name, out_shape=SDS((r, w), F32))(g)


def _local_step(x, ctx, target, modrows, weights, small):
    s_rows, c_rows = x.shape[0], ctx.shape[0]
    n = s_rows + c_rows
    nl = s_rows // TM
    w_in, wba, wbb, w_out, w_up, w_down = weights
    f = lambda a: a.reshape(1, -1).astype(F32)
    b_in, ln1_g, ln1_b, ln2_g, ln2_b, conv_b = (f(small[k]) for k in ("b_in", "ln1_g", "ln1_b", "ln2_g", "ln2_b", "conv_b"))
    conv_w8 = jnp.pad(small["conv_w_full"], ((0, 5), (0, 0)))
    qg = jnp.tile(small["q_norm_g"].reshape(1, HEAD_DIM), (1, N_HEADS))
    kg = jnp.tile(small["k_norm_g"].reshape(1, HEAD_DIM), (1, N_HEADS))
    sink_rep = jnp.repeat(small["attn_sink"].reshape(N_KV, GROUPS), TW, axis=1)
    sinkrow = sink_rep.reshape(N_KV, 1, WR)
    sinkcol = jnp.broadcast_to(sink_rep[:, :, None], (N_KV, WR, LANES))
    bd = jnp.kron(jnp.eye(N_HEADS, dtype=F32), jnp.ones((HEAD_DIM, HEAD_DIM), F32)).astype(BF16)
    cos, sin = _rope_tables(s_rows, c_rows)
    w_ext = _extend_cols(w_in)
    b_ext = _extend_cols(b_in)
    xa = jnp.concatenate([x, ctx], axis=0)

    hb, qa, kat, vat, qb, kbt, vbt, tq, rq, tk, rk, gl = inproj_fwd(xa, cos, sin, modrows, w_ext, b_ext, qg, kg, bd, nl)
    compact_t = lambda t: jnp.stack([t[:, h * KV_W:h * KV_W + HEAD_DIM].T for h in range(N_KV)])
    oa, lse_a = win_fwd(qa, kat, compact_t(vat), sinkrow, s_rows, c_rows)
    vb_t = compact_t(vbt)
    ob, lse_b = glob_fwd(qb, kbt, vb_t, s_rows)
    tqb = _pick(s_rows, GLOB_BWD_TQ)
    lse_b = lse_b.reshape(N_KV, s_rows // tqb, GROUPS * tqb, 1)
    ya, yb, mrg, y, xhat1, rstd1 = merge_fwd(oa, ob, gl, x, modrows, wba, wbb, w_out, s_rows)
    h2, u0 = ffn_up_fwd(xhat1, modrows, ln1_g, ln1_b, w_up, s_rows)
    a = conv_swiglu_fwd(u0, conv_w8, conv_b, s_rows)
    (dr2, dy2), (loss, dln2_g, dln2_b, dgate2) = ffn_down_loss(a, xhat1, target, modrows, ln1_g, ln1_b, ln2_g, ln2_b, w_down, s_rows)

    da = ffn_down_bwd(dy2, w_down.T, s_rows)
    dw_down = mm_tn("dw_down", a, dy2, s_rows)
    du0, (dconv_b, dcw0, dcw1, dcw2) = swiglu_conv_bwd(u0, da, conv_w8, conv_b, s_rows)
    dw_up = mm_tn("dw_up", h2, du0, s_rows)
    (dy, dxp), (dscale2, dshift2, dln1_g, dln1_b, dgate1) = ffn_up_ln1_bwd(du0, dr2, xhat1, y, rstd1, modrows, ln1_g, ln1_b, w_up.T, s_rows)
    dya, dyb, dgl, doa, dob = merge_bwd(dy, ya, yb, gl, w_out.T, wba.T, wbb.T, s_rows)
    dw_out = mm_tn("dw_out", mrg, dy, s_rows)
    dwba = mm_tn("dw_branch_a", oa, dya, s_rows)
    dwbb = mm_tn("dw_branch_b", ob, dyb, s_rows)

    head_rows = lambda t: t.reshape(N_KV * HEAD_DIM, t.shape[-1])
    place = np.zeros((N_KV * HEAD_DIM, Q_W), np.float32)
    for h in range(N_KV):
        place[h * HEAD_DIM + np.arange(HEAD_DIM), h * KV_W + np.arange(HEAD_DIM)] = 1.0
    place = jnp.asarray(place, BF16)
    dqa, dka_t, dva_t, dsk = win_bwd(qa, qa[:s_rows].T, kat, vat, sinkcol, oa, doa, doa.astype(MXU_DTYPE).T, lse_a,
                                     s_rows, c_rows)
    dka_t, dva_t = head_rows(dka_t[:, :, WIN:]), head_rows(dva_t[:, :, WIN:])
    delta_b = attn_delta(ob, dob, s_rows)
    qb_t = qb[:s_rows].T
    dob_t = dob.astype(MXU_DTYPE).T
    heads = [glob_bwd(qb, qb_t, kbt, vbt, dob, dob_t, lse_b, delta_b, h, s_rows) for h in range(N_KV)]
    dqb = [hd[0] for hd in heads]
    dkb_t = jnp.concatenate([hd[1] for hd in heads], axis=0)
    dvb_t = jnp.concatenate([hd[2] for hd in heads], axis=0)
    dproj, (db_ext, dqg, dkg) = qk_bwd(dqa, dka_t, dva_t, dqb, dkb_t, dvb_t, dgl, tq, rq, tk, rk, cos, sin, qg, kg, bd,
                                       place, nl, n)
    w_ext_t = w_ext.T
    (grad_x,), (dscale1, dshift1) = inproj_bwd("inproj_bwd", dproj, xa, dxp, modrows, w_ext_t, ntiles=nl, tile_off=0,
                                               is_ctx=False, out_rows=s_rows)
    _, (dscale_c, dshift_c) = inproj_bwd("inproj_bwd_ctx", dproj, xa, None, modrows, w_ext_t, ntiles=c_rows // TM,
                                         tile_off=nl, is_ctx=True, out_rows=0)
    dw_in = _fold_cols(mm_tn("dw_in", hb, dproj, n))

    dmod = jnp.concatenate([dshift1, dscale1, dgate1, dshift2, dscale2, dgate2], axis=1)
    dmod_c = jnp.concatenate([dshift_c, dscale_c, jnp.zeros((1, (N_MOD - 2) * D), F32)], axis=1)
    fold_g = lambda t: t.reshape(N_HEADS, HEAD_DIM).sum(axis=0)
    red = {
        "b_in": _fold_cols(db_ext), "attn_sink": dsk.reshape(N_HEADS, TW).sum(axis=1), "q_norm_g": fold_g(dqg),
        "k_norm_g": fold_g(dkg), "ln1_g": dln1_g, "ln1_b": dln1_b, "conv_w": jnp.concatenate([dcw0, dcw1, dcw2], axis=0),
        "conv_b": dconv_b, "ln2_g": dln2_g, "ln2_b": dln2_b,
    }
    return loss[0, 0], grad_x, (dw_in, dwba, dwbb, dw_out, dw_up, dw_down), dmod, dmod_c, red


def kernel(x, c, ctx, c_ctx, w_mod, b_mod, w_in, b_in, attn_sink, q_norm_g, k_norm_g, w_branch_a, w_branch_b, w_out, ln1_g, ln1_b, w_up, conv_w, conv_b, w_down, ln2_g, ln2_b, loss_target, m_c_ctx, m_w_mod, m_b_mod, m_w_in, m_b_in, m_attn_sink, m_q_norm_g, m_k_norm_g, m_w_branch_a, m_w_branch_b, m_w_out, m_ln1_g, m_ln1_b, m_w_up, m_conv_w, m_conv_b, m_w_down, m_ln2_g, m_ln2_b, v_c_ctx, v_w_mod, v_b_mod, v_w_in, v_b_in, v_attn_sink, v_q_norm_g, v_k_norm_g, v_w_branch_a, v_w_branch_b, v_w_out, v_ln1_g, v_ln1_b, v_w_up, v_conv_w, v_conv_b, v_w_down, v_ln2_g, v_ln2_b):
    ax, ay, ac = (lax.axis_index(a) for a in AXES)
    me = 4 * ax + 2 * ay + ac
    chip = 2 * ax + ay
    mod_w = N_MOD * D // N_DEV
    params = dict(c_ctx=c_ctx, w_mod=w_mod, b_mod=b_mod, w_in=w_in, b_in=b_in, attn_sink=attn_sink, q_norm_g=q_norm_g,
                  k_norm_g=k_norm_g, w_branch_a=w_branch_a, w_branch_b=w_branch_b, w_out=w_out, ln1_g=ln1_g, ln1_b=ln1_b,
                  w_up=w_up, conv_w=conv_w, conv_b=conv_b, w_down=w_down, ln2_g=ln2_g, ln2_b=ln2_b)
    mom_m = dict(c_ctx=m_c_ctx, w_mod=m_w_mod, b_mod=m_b_mod, w_in=m_w_in, b_in=m_b_in, attn_sink=m_attn_sink,
                 q_norm_g=m_q_norm_g, k_norm_g=m_k_norm_g, w_branch_a=m_w_branch_a, w_branch_b=m_w_branch_b, w_out=m_w_out,
                 ln1_g=m_ln1_g, ln1_b=m_ln1_b, w_up=m_w_up, conv_w=m_conv_w, conv_b=m_conv_b, w_down=m_w_down,
                 ln2_g=m_ln2_g, ln2_b=m_ln2_b)
    mom_v = dict(c_ctx=v_c_ctx, w_mod=v_w_mod, b_mod=v_b_mod, w_in=v_w_in, b_in=v_b_in, attn_sink=v_attn_sink,
                 q_norm_g=v_q_norm_g, k_norm_g=v_k_norm_g, w_branch_a=v_w_branch_a, w_branch_b=v_w_branch_b, w_out=v_w_out,
                 ln1_g=v_ln1_g, ln1_b=v_ln1_b, w_up=v_w_up, conv_w=v_conv_w, conv_b=v_conv_b, w_down=v_w_down,
                 ln2_g=v_ln2_g, ln2_b=v_ln2_b)
    big_names = [nm for nm, _ in BIG]

    def shard(tree, nm):
        t = tree[nm][0]
        return jnp.pad(t, ((0, 0), (0, FF_SHARD_PAD - FF_SHARD))) if nm == "w_up" else t

    wg = all_gather("ag_weights", _pack_big([shard(params, nm).astype(MXU_DTYPE) for nm in big_names]))
    g_in, g_ba, g_bb, g_out, g_up, g_down = _unpack_big(wg)
    weights = (_cols_to_full(g_in), _cols_to_full(g_ba), _cols_to_full(g_bb), g_out.reshape(D, D), _cols_to_full(g_up),
               _ff_pad_rows(g_down.reshape(D_FF, D)))

    c_all = all_gather("ag_c", c.reshape(8, LANES)).reshape(N_DEV, D)
    cs = jnp.concatenate([c_all, c_ctx.reshape(1, D), jnp.zeros((7, D), F32)], axis=0)
    w_mod_sh = w_mod[0]
    b_mod_sh = lax.dynamic_slice(b_mod, (0, me * mod_w), (1, mod_w))
    mod_part = mod_fwd(cs, w_mod_sh, b_mod_sh)
    mg = all_gather("ag_mod", mod_part.reshape(16 * mod_w // LANES, LANES)).reshape(N_DEV, 16, mod_w)
    mod = lax.dynamic_index_in_dim(mg, me, axis=1, keepdims=False).reshape(N_MOD, D)
    mod_c = mg[:, 8, :].reshape(N_MOD, D)
    modrows = jnp.stack([mod[0], mod[1], mod_c[0], mod_c[1], mod[2], mod[3], mod[4], mod[5]], axis=0)

    conv_w_full = all_gather("ag_conv_w", jnp.pad(conv_w[0], ((0, 5), (0, FF_SHARD_PAD - FF_SHARD))))
    conv_w_full = _cols_to_full(conv_w_full[:, :3, :])
    small = dict(b_in=b_in, ln1_g=ln1_g, ln1_b=ln1_b, ln2_g=ln2_g, ln2_b=ln2_b, conv_b=_ff_pad_cols(conv_b),
                 conv_w_full=conv_w_full, q_norm_g=q_norm_g, k_norm_g=k_norm_g, attn_sink=attn_sink)
    loss, grad_x, big_grads, dmod, dmod_c, red = _local_step(x[0], ctx[0], loss_target[0], modrows, weights, small)
    loss = lax.psum(loss, AXES)

    dm = all_gather("ag_dmod", jnp.concatenate([dmod, dmod_c], axis=0).reshape(2 * N_MOD * D // LANES, LANES))
    dm = dm.reshape(N_DEV, 2, N_MOD * D)
    dm_all = jnp.concatenate([dm[:, 0], dm[:, 1]], axis=0)
    dm_sh = lax.dynamic_slice(dm_all, (0, me * mod_w), (16, mod_w))
    dw_mod, dcc, db_mod = mod_bwd(cs, w_mod_sh, dm_sh, dm_all)
    red["c_ctx"] = dcc[8]

    red_vec = jnp.concatenate([red[nm].reshape(-1) for nm, _ in RED])
    red_vec = jnp.pad(red_vec, (0, RED_ROWS * LANES - RED_TOTAL)).reshape(RED_ROWS, LANES)
    red_sum = sum8("sum_small", all_gather("ag_small", red_vec)).reshape(-1)
    gsm, off = {}, 0
    for nm, k in RED:
        gsm[nm] = red_sum[off:off + k]
        off += k
    gsm["b_mod"] = db_mod.reshape(-1)
    gsm["conv_b"] = _ff_unpad_cols(gsm["conv_b"].reshape(1, 2 * FF))
    gsm["conv_w"] = lax.dynamic_slice(gsm["conv_w"].reshape(3, 2 * FF), (0, me * FF_SHARD_PAD), (3, FF_SHARD_PAD))[:, :FF_SHARD]
    sm_names = [nm for nm, _ in SMALL]
    gs, ds, ms, vs = adamw("adamw_small", _pack_small([params[nm] for nm in sm_names]),
                           _pack_small([mom_m[nm] for nm in sm_names]), _pack_small([mom_v[nm] for nm in sm_names]),
                           [_pack_small([gsm[nm] for nm in sm_names])])
    sm_out = [_unpack_small(t) for t in (gs, ds, ms, vs)]

    dw_in, dwba, dwbb, dw_out, dw_up, dw_down = big_grads
    slabs = jnp.concatenate([t.reshape(N_DEV, -1) for t in (
        _full_to_cols(dw_in), _full_to_cols(dwba), _full_to_cols(dwbb), dw_out, _full_to_cols(dw_up),
        _ff_unpad_rows(dw_down))], axis=1)
    by_core = slabs.reshape(4, 2, BIG_ROWS, LANES)
    keep = lax.dynamic_index_in_dim(by_core, ac, axis=1, keepdims=False)
    give = lax.dynamic_index_in_dim(by_core, 1 - ac, axis=1, keepdims=False)
    got = exchange("rs_sibling", give.reshape(1, 4 * BIG_ROWS, LANES), to_chips=False).reshape(4, BIG_ROWS, LANES)
    pair = add2("rs_pair_sum", keep, got)
    outbox = jnp.stack([lax.dynamic_index_in_dim(pair, jnp.bitwise_xor(chip, m), axis=0, keepdims=False) for m in (1, 2, 3)])
    inbox = exchange("rs_chips", outbox.astype(MXU_DTYPE), to_chips=True)
    mine = lax.dynamic_index_in_dim(pair, chip, axis=0, keepdims=False)
    gb, db, mb, vb = adamw("adamw_big", _pack_big([shard(params, nm) for nm in big_names]),
                           _pack_big([shard(mom_m, nm) for nm in big_names]), _pack_big([shard(mom_v, nm) for nm in big_names]),
                           [mine, inbox[0], inbox[1], inbox[2]])
    big_out = [dict(zip(big_names, _unpack_big(t), strict=True)) for t in (gb, db, mb, vb)]
    for out in big_out:
        out["w_up"] = out["w_up"][:, :FF_SHARD]
    gm, dmo, mmo, vmo = adamw("adamw_mod", w_mod[0], m_w_mod[0], v_w_mod[0], [dw_mod])
    mod_out = (gm, dmo, mmo, vmo)

    order = ["c_ctx", "w_mod", "b_mod", "w_in", "b_in", "attn_sink", "q_norm_g", "k_norm_g", "w_branch_a", "w_branch_b",
             "w_out", "ln1_g", "ln1_b", "w_up", "conv_w", "conv_b", "w_down", "ln2_g", "ln2_b"]
    results = [loss, grad_x[None]]
    for kind in range(4):
        for nm in order:
            if nm == "w_mod":
                val = mod_out[kind]
            elif nm in big_out[kind]:
                val = big_out[kind][nm]
            else:
                val = sm_out[kind][nm]
            results.append(val.reshape(params[nm].shape))
    return tuple(results)
```

```python
import functools

import jax
import jax.numpy as jnp
import numpy as np
from jax import lax
from jax.experimental import pallas as pl
from jax.experimental.pallas import tpu as pltpu

F32 = jnp.float32
BF16 = jnp.bfloat16
MXU_DTYPE = BF16

AXES = ("x", "y", "c")
N_DEV = 8
D = 1024
HEAD_DIM = 64
N_HEADS = 8
N_KV = 2
GROUPS = 4
KV_W = GROUPS * HEAD_DIM
Q_W = N_HEADS * HEAD_DIM
GRID_W = 64
WIN = 128
ROPE_THETA = 10000.0
D_FF = 2816
FF_SHARD = 2 * D_FF // N_DEV
FF_SHARD_PAD = 768
FF = N_DEV // 2 * FF_SHARD_PAD
LN_EPS = 1e-5
QK_EPS = 1e-6
N_MOD = 6
ALPHA = 2.0 ** 0.25
Q_SCALE = HEAD_DIM ** -0.5
IN_COLS = 3584
OFF_KA, OFF_VA, OFF_QB, OFF_KB, OFF_VB, OFF_GA = 512, 640, 768, 1280, 1408, 1536
EXT_COLS = 6 * Q_W + 2 * D
X_QA, X_KA, X_VA, X_QB, X_KB, X_VB, X_GL = 0, 512, 1024, 1536, 2048, 2560, 3072
ADAM_LR, ADAM_B1, ADAM_B2, ADAM_EPS, ADAM_WD, ADAM_STEP = 0.001, 0.9, 0.999, 1e-08, 0.01, 10
LANES = 128
TM = 256
VMEM_LIMIT = 56 * 1024 * 1024
ELEMENTWISE_BLOCK_BYTES = 1 << 20
MM_TN_LHS_BYTES = 8 << 20
ELEMENTWISE_ROWS = (1824, 1408, 1024, 512, 256, 128, 64, 32, 16, 8)

ANY = pl.BlockSpec(memory_space=pl.ANY)
SDS = jax.ShapeDtypeStruct


def _pick(n, candidates):
    for t in candidates:
        if n % t == 0:
            return t
    raise ValueError(f"no tile for {n}")


def _full(a):
    nd = a.ndim
    return pl.BlockSpec(a.shape, lambda *_: (0,) * nd)


def _rows(tm, w, fn=lambda t: t):
    return pl.BlockSpec((tm, w), lambda i: (fn(i), 0))


def _dot(a, b):
    return jnp.dot(a.astype(MXU_DTYPE), b.astype(MXU_DTYPE), preferred_element_type=F32)


def _dot_nt(a, b):
    return lax.dot_general(a.astype(MXU_DTYPE), b.astype(MXU_DTYPE), (((1,), (1,)), ((), ())), preferred_element_type=F32)


def _dot_tn(a, b):
    return lax.dot_general(a.astype(MXU_DTYPE), b.astype(MXU_DTYPE), (((0,), (0,)), ((), ())), preferred_element_type=F32)


def _cparams(sem):
    return pltpu.CompilerParams(dimension_semantics=sem, vmem_limit_bytes=VMEM_LIMIT)


def all_gather(name, v):
    r, w = v.shape

    def body(x_ref, out_ref, send_sems, recv_sems, local_sem):
        x, y, c = (lax.axis_index(a) for a in AXES)
        me, sibling = (x, y, c), (x, y, 1 - c)
        chips = [(1 - x, y), (x, 1 - y), (1 - x, 1 - y)]

        def rows(px, py, pc):
            return out_ref.at[4 * px + 2 * py + pc]

        def copy(k, block, to, src=None):
            return pltpu.make_async_remote_copy(
                src_ref=rows(*block) if src is None else src, dst_ref=rows(*block),
                send_sem=send_sems.at[k], recv_sem=recv_sems.at[k],
                device_id=to, device_id_type=pl.DeviceIdType.MESH)

        mine = pltpu.make_async_copy(x_ref, rows(*me), local_sem)
        mine.start()
        first = [copy(0, me, sibling, src=x_ref)]
        first += [copy(1 + j, me, (*chip, c), src=x_ref) for j, chip in enumerate(chips)]
        for cp in first:
            cp.start()
        passed = [copy(4 + j, (*chip, c), sibling) for j, chip in enumerate(chips)]
        for j, chip in enumerate(chips):
            copy(1 + j, (*chip, c), me).wait_recv()
            passed[j].start()
        copy(0, sibling, me).wait_recv()
        for j, chip in enumerate(chips):
            copy(4 + j, (*chip, 1 - c), me).wait_recv()
        for cp in first + passed:
            cp.wait_send()
        mine.wait()

    return pl.pallas_call(
        body, name=name, out_shape=SDS((N_DEV, r, w), v.dtype), in_specs=[ANY], out_specs=ANY,
        scratch_shapes=[pltpu.SemaphoreType.DMA((7,)), pltpu.SemaphoreType.DMA((7,)), pltpu.SemaphoreType.DMA],
    )(v)


def exchange(name, outbox, to_chips):
    k = outbox.shape[0]
    assert k == (3 if to_chips else 1)

    def body(out_ref, in_ref, send_sems, recv_sems):
        x, y, c = (lax.axis_index(a) for a in AXES)
        peers = [(x, 1 - y, c), (1 - x, y, c), (1 - x, 1 - y, c)] if to_chips else [(x, y, 1 - c)]
        copies = [
            pltpu.make_async_remote_copy(
                src_ref=out_ref.at[m], dst_ref=in_ref.at[m], send_sem=send_sems.at[m], recv_sem=recv_sems.at[m],
                device_id=peer, device_id_type=pl.DeviceIdType.MESH)
            for m, peer in enumerate(peers)
        ]
        for cp in copies:
            cp.start()
        for cp in copies:
            cp.wait_recv()
        for cp in copies:
            cp.wait_send()

    return pl.pallas_call(
        body, name=name, out_shape=SDS(outbox.shape, outbox.dtype), in_specs=[ANY], out_specs=ANY,
        scratch_shapes=[pltpu.SemaphoreType.DMA((k,)), pltpu.SemaphoreType.DMA((k,))],
    )(outbox)


def rowwise(name, body, *, ntiles, tile_off=0, tiled, full, outs, accs=()):
    nt, nf, no = len(tiled), len(full), len(outs)

    def kern(*refs):
        i = pl.program_id(0)
        out_vals, incs = body(i + tile_off, refs[:nt], refs[nt:nt + nf])
        for r, v in zip(refs[nt + nf:nt + nf + no], out_vals, strict=True):
            r[...] = v.astype(r.dtype)
        acc_refs = refs[nt + nf + no:]

        @pl.when(i == 0)
        def _():
            for r in acc_refs:
                r[...] = jnp.zeros_like(r)

        for r, v in zip(acc_refs, incs, strict=True):
            r[...] += v

    res = pl.pallas_call(
        kern, name=name, grid=(ntiles,),
        in_specs=[s for _, s in tiled] + [_full(a) for a in full],
        out_specs=[s for _, _, s in outs] + [pl.BlockSpec(s, lambda i, n=len(s): (0,) * n) for s in accs],
        out_shape=[SDS(s, d) for s, d, _ in outs] + [SDS(s, F32) for s in accs],
        compiler_params=_cparams(("arbitrary",) if accs else ("parallel",)),
    )(*[a for a, _ in tiled], *full)
    return res[:no], res[no:]


def mm_tn(name, a, b, rows):
    ka, nb = a.shape[1], b.shape[1]
    tr = _pick(rows, [t for t in (2048, 1280, 1024, 768, 512, 256) if t * ka * a.dtype.itemsize <= MM_TN_LHS_BYTES])
    tn = _pick(nb, (1024, 512, 256, 128))

    def kern(a_ref, b_ref, o_ref):
        @pl.when(pl.program_id(1) == 0)
        def _():
            o_ref[...] = jnp.zeros_like(o_ref)

        o_ref[...] += _dot_tn(a_ref[...], b_ref[...])

    return pl.pallas_call(
        kern, name=name, grid=(nb // tn, rows // tr),
        in_specs=[pl.BlockSpec((tr, ka), lambda n, r: (r, 0)), pl.BlockSpec((tr, tn), lambda n, r: (r, n))],
        out_specs=pl.BlockSpec((ka, tn), lambda n, r: (0, n)), out_shape=SDS((ka, nb), F32),
        compiler_params=_cparams(("parallel", "arbitrary")),
    )(a, b)


def _swap16(t):
    w = t.shape[1]
    lane = lax.broadcasted_iota(jnp.int32, t.shape, 1)
    return jnp.where((lane & 16) == 0, pltpu.roll(t, w - 16, 1), pltpu.roll(t, 16, 1))


def _rope(t, cos, sin):
    return t * cos + _swap16(t) * sin


def _rope_t(d, cos, sin):
    return d * cos - _swap16(d) * sin


def _seg_sum64(a, bd_ref):
    bd = bd_ref[...]
    hi = a.astype(BF16)
    lo = (a - hi.astype(F32)).astype(BF16)
    return jnp.dot(hi, bd, preferred_element_type=F32) + jnp.dot(lo, bd, preferred_element_type=F32)


def _lane_block(shape):
    return jnp.right_shift(lax.broadcasted_iota(jnp.int32, shape, 1), 6)


def _stack_groups(t, dtype):
    blk = _lane_block(t.shape)
    return jnp.concatenate([jnp.where(blk == g, t, jnp.zeros_like(t)).astype(dtype) for g in range(GROUPS)], axis=0)


def _fold_groups(ts, tq):
    blk = _lane_block((tq, KV_W))
    out = jnp.zeros((tq, KV_W), ts.dtype)
    for g in range(GROUPS):
        out = jnp.where(blk == g, ts[g * tq:(g + 1) * tq], out)
    return out


def _row_to_col(row):
    hi = row.astype(BF16)
    r1 = row - hi.astype(F32)
    mid = r1.astype(BF16)
    lo = (r1 - mid.astype(F32)).astype(BF16)
    ones = jnp.ones((8, LANES), BF16)
    pad = jnp.zeros((7, row.shape[1]), BF16)
    acc = jnp.zeros((row.shape[1], LANES), F32)
    for term in (hi, mid, lo):
        acc = acc + lax.dot_general(jnp.concatenate([term, pad], axis=0), ones, (((0,), (0,)), ((), ())),
                                    preferred_element_type=F32)
    return acc[:, 0:1]


def _stack_tiles(t, dtype):
    return jnp.concatenate([_stack_groups(t[a:a + TM], dtype) for a in range(0, t.shape[0], TM)], axis=0)


def _fold_tiles(ts, tq):
    return jnp.concatenate([_fold_groups(ts[GROUPS * a:GROUPS * (a + TM)], TM) for a in range(0, tq, TM)], axis=0)


def _compact_tiles_t(tt, dtype):
    return jnp.concatenate([tt[g * HEAD_DIM:(g + 1) * HEAD_DIM, a:a + TM] for a in range(0, tt.shape[1], TM)
                            for g in range(GROUPS)], axis=1).astype(dtype)


def _layer_norm_bwd(dxh, xhat, rstd):
    m1 = jnp.mean(dxh, axis=1, keepdims=True)
    m2 = jnp.mean(dxh * xhat, axis=1, keepdims=True)
    return rstd * (dxh - m1 - xhat * m2)


def _colsum(a):
    return jnp.sum(a, axis=0, keepdims=True)


def _shifted_rows(t, prev_row, next_row):
    n = t.shape[0]
    row = lax.broadcasted_iota(jnp.int32, t.shape, 0)
    up = jnp.where(row == 0, prev_row, pltpu.roll(t, 1, 0))
    dn = jnp.where(row == n - 1, next_row, pltpu.roll(t, n - 1, 0))
    return up, dn


def mod_fwd(cs, w_sh, b_sh):
    def kern(c_ref, w_ref, b_ref, o_ref):
        o_ref[...] = _dot(jax.nn.silu(c_ref[...]), w_ref[...]) + b_ref[...]

    return pl.pallas_call(kern, name="mod_fwd", out_shape=SDS((16, w_sh.shape[1]), F32),
                          compiler_params=pltpu.CompilerParams(vmem_limit_bytes=VMEM_LIMIT))(cs, w_sh, b_sh)


def mod_bwd(cs, w_sh, dm_sh, dm_all):
    hp = lax.Precision.HIGHEST

    def kern(c_ref, w_ref, dm_ref, da_ref, dw_ref, dc_ref, db_ref):
        c = c_ref[...]
        sg = jax.nn.sigmoid(c)
        sc = c * sg
        dm = dm_ref[...]
        dmc = dm_ref[8:9, :]
        for i in range(9, 16):
            dmc = dmc + dm_ref[i:i + 1, :]
        row = lax.broadcasted_iota(jnp.int32, dm.shape, 0)
        a = jnp.where(row < 8, dm, jnp.where(row == 8, dmc, 0.0))
        dw_ref[...] = lax.dot_general(sc, a, (((0,), (0,)), ((), ())), precision=hp, preferred_element_type=F32)
        dsc = lax.dot_general(a, w_ref[...], (((1,), (1,)), ((), ())), precision=hp, preferred_element_type=F32)
        dc_ref[...] = dsc * (sg * (1.0 + c * (1.0 - sg)))
        db = da_ref[0:1, :]
        for i in range(1, 16):
            db = db + da_ref[i:i + 1, :]
        db_ref[...] = db

    return pl.pallas_call(
        kern, name="mod_bwd",
        out_shape=[SDS(w_sh.shape, F32), SDS((16, D), F32), SDS((1, dm_all.shape[1]), F32)],
        compiler_params=pltpu.CompilerParams(vmem_limit_bytes=VMEM_LIMIT))(cs, w_sh, dm_sh, dm_all)


M_SHIFT1, M_SCALE1, M_SHIFTC, M_SCALEC, M_GATE1, M_SHIFT2, M_SCALE2, M_GATE2 = range(8)


def _mrow(ref, k):
    return ref[k:k + 1, :]


def inproj_fwd(xa, cos, sin, modrows, w_ext, b_ext, qg, kg, bd, n_lat_tiles):
    n = xa.shape[0]

    def body(t, vals, fr):
        x, cs, sn = (v[...] for v in vals)
        mod, w, b, qg_r, kg_r, bd_r = fr
        is_ctx = t >= n_lat_tiles
        shift = jnp.where(is_ctx, _mrow(mod, M_SHIFTC), _mrow(mod, M_SHIFT1))
        scale = jnp.where(is_ctx, _mrow(mod, M_SCALEC), _mrow(mod, M_SCALE1))
        hb = (x * (1.0 + scale) + shift).astype(MXU_DTYPE)
        proj = jnp.dot(hb, w[...], preferred_element_type=F32) + b[...]
        cos4 = jnp.concatenate([cs] * 4, axis=1)
        sin4 = jnp.concatenate([sn] * 4, axis=1)
        qa = _rope(proj[:, X_QA:X_QA + Q_W], cos4, sin4) * Q_SCALE
        ka = _rope(proj[:, X_KA:X_KA + Q_W], cos4, sin4)
        va = proj[:, X_VA:X_VA + Q_W]
        tq = proj[:, X_QB:X_QB + Q_W]
        rq = lax.rsqrt(_seg_sum64(tq * tq, bd_r) * (1.0 / HEAD_DIM) + QK_EPS)
        qb = _rope(tq * rq * qg_r[...], cos4, sin4) * Q_SCALE
        tk = proj[:, X_KB:X_KB + Q_W]
        rk = lax.rsqrt(_seg_sum64(tk * tk, bd_r) * (1.0 / HEAD_DIM) + QK_EPS)
        kb = _rope(tk * rk * kg_r[...], cos4, sin4)
        vb = proj[:, X_VB:X_VB + Q_W]
        gl = proj[:, X_GL:]
        return [hb, qa, ka, va, qb, kb, vb, tq, rq, tk, rk, gl], []

    mx = MXU_DTYPE
    outs = [((n, D), mx, _rows(TM, D))] + [((n, Q_W), mx, _rows(TM, Q_W))] * 6 + \
           [((n, Q_W), F32, _rows(TM, Q_W))] * 4 + [((n, 2 * D), F32, _rows(TM, 2 * D))]
    res, _ = rowwise("inproj_fwd", body, ntiles=n // TM,
                     tiled=[(xa, _rows(TM, D)), (cos, _rows(TM, LANES)), (sin, _rows(TM, LANES))],
                     full=[modrows, w_ext, b_ext, qg, kg, bd], outs=outs)
    return res


def merge_fwd(oa, ob, gl, x, modrows, wba, wbb, w_out, s_rows):
    def body(t, vals, fr):
        oa_, ob_, gl_, x_ = (v[...] for v in vals)
        mod, wa, wb, wo = fr
        ya = _dot(oa_, wa[...])
        yb = _dot(ob_, wb[...])
        ga = jax.nn.sigmoid(gl_[:, :D])
        gb = jax.nn.sigmoid(gl_[:, D:])
        mrg = ga * ya + gb * yb
        y = _dot(mrg, wo[...])
        r1 = ALPHA * x_ + _mrow(mod, M_GATE1) * y
        mu = jnp.mean(r1, axis=1, keepdims=True)
        xc = r1 - mu
        var = jnp.mean(xc * xc, axis=1, keepdims=True)
        rstd = lax.rsqrt(var + LN_EPS)
        xhat = xc * rstd
        return [ya, yb, mrg, y, xhat, rstd], []

    outs = [((s_rows, D), F32, _rows(TM, D))] * 2 + [((s_rows, D), MXU_DTYPE, _rows(TM, D))] + \
           [((s_rows, D), F32, _rows(TM, D))] * 2 + [((s_rows, 1), F32, _rows(TM, 1))]
    res, _ = rowwise("merge_fwd", body, ntiles=s_rows // TM,
                     tiled=[(oa, _rows(TM, Q_W)), (ob, _rows(TM, Q_W)), (gl, _rows(TM, 2 * D)), (x, _rows(TM, D))],
                     full=[modrows, wba, wbb, w_out], outs=outs)
    return res


def ffn_up_fwd(xhat1, modrows, ln_g, ln_b, w_up, s_rows):
    def body(t, vals, fr):
        xh = vals[0][...]
        mod, g_r, b_r, w = fr
        x1 = xh * g_r[...] + b_r[...]
        h2 = (x1 * (1.0 + _mrow(mod, M_SCALE2)) + _mrow(mod, M_SHIFT2)).astype(MXU_DTYPE)
        return [h2, jnp.dot(h2, w[...], preferred_element_type=F32)], []

    res, _ = rowwise("ffn_up_fwd", body, ntiles=s_rows // TM, tiled=[(xhat1, _rows(TM, D))],
                     full=[modrows, ln_g, ln_b, w_up],
                     outs=[((s_rows, D), MXU_DTYPE, _rows(TM, D)), ((s_rows, 2 * FF), F32, _rows(TM, 2 * FF))])
    return res


TC = 128


def _halo_specs(tm, w, s_rows):
    per = tm // 8
    last = s_rows // 8 - 1
    return (pl.BlockSpec((8, w), lambda i: (jnp.maximum(i * per - 1, 0), 0)),
            pl.BlockSpec((8, w), lambda i: (jnp.minimum((i + 1) * per, last), 0)))


def _halo_rows(t, ntiles, prev_ref, next_ref):
    prev_row = jnp.where(t == 0, 0.0, prev_ref[7:8, :].astype(F32))
    next_row = jnp.where(t == ntiles - 1, 0.0, next_ref[0:1, :].astype(F32))
    return prev_row, next_row


def conv_swiglu_fwd(u0, conv_w8, conv_b, s_rows):
    w2 = 2 * FF
    nt = s_rows // TC

    def body(t, vals, fr):
        u_ref, pv, nx = vals
        cw, cb = fr
        u = u_ref[...]
        up, dn = _shifted_rows(u, *_halo_rows(t, nt, pv, nx))
        uc = cw[0:1, :] * up + cw[1:2, :] * u + cw[2:3, :] * dn + cb[...]
        gate, val = uc[:, :FF], uc[:, FF:]
        return [gate * jax.nn.sigmoid(gate) * val], []

    hp, hn = _halo_specs(TC, w2, s_rows)
    res, _ = rowwise("conv_swiglu_fwd", body, ntiles=nt,
                     tiled=[(u0, _rows(TC, w2)), (u0, hp), (u0, hn)], full=[conv_w8, conv_b],
                     outs=[((s_rows, FF), MXU_DTYPE, _rows(TC, FF))])
    return res[0]


def ffn_down_loss(a, xhat1, target, modrows, ln1_g, ln1_b, ln2_g, ln2_b, w_down, s_rows):
    def body(t, vals, fr):
        a_, xh1, tgt = (v[...] for v in vals)
        mod, g1, b1, g2, b2, wd = fr
        y2 = jnp.dot(a_, wd[...], preferred_element_type=F32)
        x1 = xh1 * g1[...] + b1[...]
        gate2 = _mrow(mod, M_GATE2)
        r2 = ALPHA * x1 + gate2 * y2
        mu = jnp.mean(r2, axis=1, keepdims=True)
        xc = r2 - mu
        var = jnp.mean(xc * xc, axis=1, keepdims=True)
        rstd = lax.rsqrt(var + LN_EPS)
        xhat = xc * rstd
        out = xhat * g2[...] + b2[...]
        diff = out - tgt
        loss = 0.5 * jnp.sum(jnp.mean(diff * diff, axis=1, keepdims=True), axis=0, keepdims=True)
        dout = diff * (1.0 / D)
        dr2 = _layer_norm_bwd(dout * g2[...], xhat, rstd)
        incs = [loss, _colsum(dout * xhat), _colsum(dout), _colsum(dr2 * y2)]
        return [dr2, dr2 * gate2], incs

    res, accs = rowwise("ffn_down_loss", body, ntiles=s_rows // TM,
                        tiled=[(a, _rows(TM, FF)), (xhat1, _rows(TM, D)), (target, _rows(TM, D))],
                        full=[modrows, ln1_g, ln1_b, ln2_g, ln2_b, w_down],
                        outs=[((s_rows, D), F32, _rows(TM, D)), ((s_rows, D), MXU_DTYPE, _rows(TM, D))],
                        accs=[(1, 1), (1, D), (1, D), (1, D)])
    return res, accs


def ffn_down_bwd(dy2, w_down_t, s_rows):
    def body(t, vals, fr):
        return [jnp.dot(vals[0][...], fr[0][...], preferred_element_type=F32)], []

    res, _ = rowwise("ffn_down_bwd", body, ntiles=s_rows // TM, tiled=[(dy2, _rows(TM, D))], full=[w_down_t],
                     outs=[((s_rows, FF), F32, _rows(TM, FF))])
    return res[0]


def swiglu_conv_bwd(u0, da, conv_w8, conv_b, s_rows):
    w2 = 2 * FF
    nt = s_rows // TC
    n = TC + 16

    def body(t, vals, fr):
        u_ref, upv, unx, da_ref, apv, anx = vals
        cw, cb = fr
        first, last = t == 0, t == nt - 1
        ue = jnp.concatenate([jnp.where(first, 0.0, upv[...]), u_ref[...], jnp.where(last, 0.0, unx[...])], axis=0)
        ae = jnp.concatenate([jnp.where(first, 0.0, apv[...]), da_ref[...], jnp.where(last, 0.0, anx[...])], axis=0)
        up = pltpu.roll(ue, 1, 0)
        dn = pltpu.roll(ue, n - 1, 0)
        uc = cw[0:1, :] * up + cw[1:2, :] * ue + cw[2:3, :] * dn + cb[...]
        gate, val = uc[:, :FF], uc[:, FF:]
        sg = jax.nn.sigmoid(gate)
        du = jnp.concatenate([ae * val * (sg * (1.0 + gate * (1.0 - sg))), ae * (gate * sg)], axis=1)
        du0 = cw[0:1, :] * pltpu.roll(du, n - 1, 0) + cw[1:2, :] * du + cw[2:3, :] * pltpu.roll(du, 1, 0)
        rows = slice(8, 8 + TC)
        dut = du[rows]
        return [du0[rows]], [_colsum(dut), _colsum(up[rows] * dut), _colsum(ue[rows] * dut), _colsum(dn[rows] * dut)]

    hp, hn = _halo_specs(TC, w2, s_rows)
    ap, an = _halo_specs(TC, FF, s_rows)
    res, accs = rowwise("swiglu_conv_bwd", body, ntiles=nt,
                        tiled=[(u0, _rows(TC, w2)), (u0, hp), (u0, hn), (da, _rows(TC, FF)), (da, ap), (da, an)],
                        full=[conv_w8, conv_b], outs=[((s_rows, w2), MXU_DTYPE, _rows(TC, w2))], accs=[(1, w2)] * 4)
    return res[0], accs


def ffn_up_ln1_bwd(du0, dr2, xhat1, y, rstd1, modrows, ln_g, ln_b, w_up_t, s_rows):
    def body(t, vals, fr):
        du0_, dr2_, xh, y_, rstd = (v[...] for v in vals)
        mod, g_r, b_r, wt = fr
        dh2 = jnp.dot(du0_, wt[...], preferred_element_type=F32)
        x1 = xh * g_r[...] + b_r[...]
        dx1 = ALPHA * dr2_ + dh2 * (1.0 + _mrow(mod, M_SCALE2))
        dr1 = _layer_norm_bwd(dx1 * g_r[...], xh, rstd)
        incs = [_colsum(dh2 * x1), _colsum(dh2), _colsum(dx1 * xh), _colsum(dx1), _colsum(dr1 * y_)]
        return [dr1 * _mrow(mod, M_GATE1), ALPHA * dr1], incs

    res, accs = rowwise("ffn_up_ln1_bwd", body, ntiles=s_rows // TM,
                        tiled=[(du0, _rows(TM, 2 * FF)), (dr2, _rows(TM, D)), (xhat1, _rows(TM, D)), (y, _rows(TM, D)),
                               (rstd1, _rows(TM, 1))],
                        full=[modrows, ln_g, ln_b, w_up_t],
                        outs=[((s_rows, D), MXU_DTYPE, _rows(TM, D)), ((s_rows, D), F32, _rows(TM, D))],
                        accs=[(1, D)] * 5)
    return res, accs


def merge_bwd(dy, ya, yb, gl, w_out_t, wba_t, wbb_t, s_rows):
    def body(t, vals, fr):
        dy_, ya_, yb_, gl_ = (v[...] for v in vals)
        wot, wat, wbt = fr
        dmrg = jnp.dot(dy_, wot[...], preferred_element_type=F32)
        ga = jax.nn.sigmoid(gl_[:, :D])
        gb = jax.nn.sigmoid(gl_[:, D:])
        dya = dmrg * ga
        dyb = dmrg * gb
        dgl = jnp.concatenate([dmrg * ya_ * ga * (1.0 - ga), dmrg * yb_ * gb * (1.0 - gb)], axis=1)
        return [dya, dyb, dgl, _dot(dya, wat[...]), _dot(dyb, wbt[...])], []

    mx = MXU_DTYPE
    res, _ = rowwise("merge_bwd", body, ntiles=s_rows // TM,
                     tiled=[(dy, _rows(TM, D)), (ya, _rows(TM, D)), (yb, _rows(TM, D)), (gl, _rows(TM, 2 * D))],
                     full=[w_out_t, wba_t, wbb_t],
                     outs=[((s_rows, D), mx, _rows(TM, D))] * 2 + [((s_rows, 2 * D), F32, _rows(TM, 2 * D))] +
                          [((s_rows, Q_W), F32, _rows(TM, Q_W))] * 2)
    return res


def qk_bwd(dqa, dka_t, dva_t, dqb_heads, dkb_t, dvb_t, dgl, tq, rq, tk, rk, cos, sin, qg, kg, bd, place, n_lat_tiles, n):
    def placed(xt_ref, place_ref):
        xt = xt_ref[...]
        hi = xt.astype(BF16)
        r1 = xt - hi.astype(F32)
        mid = r1.astype(BF16)
        lo = (r1 - mid.astype(F32)).astype(BF16)
        pm = place_ref[...]
        return sum(lax.dot_general(term, pm, (((0,), (0,)), ((), ())), preferred_element_type=F32) for term in (hi, mid, lo))

    def body(t, vals, fr):
        dqa_, dgl_, tq_, rq_, tk_, rk_, cs, sn = (v[...] for v in vals[:8])
        qg_r, kg_r, bd_r, pl_r = fr
        dka_, dva_, dkb_, dvb_ = (placed(v, pl_r) for v in vals[8:12])
        dqb_ = jnp.concatenate([v[...] for v in vals[12:]], axis=1)
        is_ctx = t >= n_lat_tiles
        cos4 = jnp.concatenate([cs] * 4, axis=1)
        sin4 = jnp.concatenate([sn] * 4, axis=1)
        zero = jnp.zeros_like(dqa_)
        dpqa = jnp.where(is_ctx, zero, _rope_t(dqa_, cos4, sin4) * Q_SCALE)
        dpka = _rope_t(dka_, cos4, sin4)
        dpva = dva_
        dnq = jnp.where(is_ctx, zero, _rope_t(dqb_, cos4, sin4) * Q_SCALE)
        gq = qg_r[...] * dnq
        dtq = rq_ * gq - tq_ * (rq_ * rq_ * rq_) * (_seg_sum64(gq * tq_, bd_r) * (1.0 / HEAD_DIM))
        dnk = _rope_t(dkb_, cos4, sin4)
        gk = kg_r[...] * dnk
        dtk = rk_ * gk - tk_ * (rk_ * rk_ * rk_) * (_seg_sum64(gk * tk_, bd_r) * (1.0 / HEAD_DIM))
        dgl32 = jnp.where(is_ctx, jnp.zeros_like(dgl_), dgl_)
        dproj = jnp.concatenate([dpqa, dpka, dpva, dtq, dtk, dvb_, dgl32], axis=1)
        return [dproj], [_colsum(dproj), _colsum(dnq * tq_ * rq_), _colsum(dnk * tk_ * rk_)]

    lat = lambda t: jnp.minimum(t, n_lat_tiles - 1)
    qs = _rows(TM, Q_W)
    ts = pl.BlockSpec((N_KV * HEAD_DIM, TM), lambda i: (0, i))
    res, accs = rowwise(
        "qk_bwd", body, ntiles=n // TM,
        tiled=[(dqa, _rows(TM, Q_W, lat)), (dgl, _rows(TM, 2 * D, lat)),
               (tq, qs), (rq, qs), (tk, qs), (rk, qs), (cos, _rows(TM, LANES)), (sin, _rows(TM, LANES)),
               (dka_t, ts), (dva_t, ts), (dkb_t, ts), (dvb_t, ts)] + [(d, _rows(TM, KV_W, lat)) for d in dqb_heads],
        full=[qg, kg, bd, place], outs=[((n, EXT_COLS), MXU_DTYPE, _rows(TM, EXT_COLS))],
        accs=[(1, EXT_COLS), (1, Q_W), (1, Q_W)])
    return res[0], accs


def inproj_bwd(name, dproj, xa, dxp, modrows, w_ext_t, *, ntiles, tile_off, is_ctx, out_rows):
    kc = M_SCALEC if is_ctx else M_SCALE1

    def body(t, vals, fr):
        dp, x_ = vals[0][...], vals[1][...]
        mod, wt = fr
        dh = jnp.dot(dp, wt[...], preferred_element_type=F32)
        incs = [_colsum(dh * x_), _colsum(dh)]
        if is_ctx:
            return [], incs
        return [vals[2][...] + dh * (1.0 + _mrow(mod, kc))], incs

    tiled = [(dproj, _rows(TM, EXT_COLS, lambda i: i + tile_off)), (xa, _rows(TM, D, lambda i: i + tile_off))]
    outs = []
    if not is_ctx:
        tiled.append((dxp, _rows(TM, D)))
        outs = [((out_rows, D), F32, _rows(TM, D))]
    return rowwise(name, body, ntiles=ntiles, tiled=tiled, full=[modrows, w_ext_t], outs=outs, accs=[(1, D)] * 2)


def _attn_semantics():
    return _cparams(("arbitrary", "arbitrary", "arbitrary"))


GLOB_TK = (1280, 1024, 768, 512, 256)
GLOB_TQ = (512, 256)
GLOB_BWD_TQ = GLOB_TQ
KEY_CHUNK = 256


def glob_fwd(q, kt, v_t, s_rows):
    n = kt.shape[0]
    tq = _pick(s_rows, GLOB_TQ)
    tk = _pick(n, GLOB_TK)
    nq, nk = s_rows // tq, n // tk
    r = GROUPS * tq
    nch = tk // KEY_CHUNK

    def produce(qs, k_ref, s_buf, c, mx):
        rows = slice(c * KEY_CHUNK, (c + 1) * KEY_CHUNK)
        sn = _dot_nt(k_ref[rows, :], qs[...])
        s_buf[rows, :] = sn
        return jnp.maximum(mx, jnp.max(sn, axis=0, keepdims=True))

    def kern(q_ref, k0_ref, kn_ref, vt_ref, o_ref, lse_ref, qs, s_buf, mx_buf, m_s, l_s, acc):
        j = pl.program_id(2)

        @pl.when(j == 0)
        def _():
            qs[...] = _stack_tiles(q_ref[...], qs.dtype)
            mx = jnp.full((1, r), -jnp.inf, F32)
            for c in range(nch):
                mx = produce(qs, k0_ref, s_buf, c, mx)
            mx_buf[...] = mx
            m_s[...] = jnp.full_like(m_s, -jnp.inf)
            l_s[...] = jnp.zeros_like(l_s)
            acc[...] = jnp.zeros_like(acc)

        m_prev = m_s[...]
        m_new = jnp.maximum(m_prev, mx_buf[...])
        alpha = jnp.exp(m_prev - m_new)
        a = alpha * acc[...]
        ls = alpha * l_s[...]
        mx = jnp.full((1, r), -jnp.inf, F32)
        for c in range(nch):
            rows = slice(c * KEY_CHUNK, (c + 1) * KEY_CHUNK)
            p = jnp.exp(s_buf[rows, :] - m_new)
            ls = ls + jnp.sum(p, axis=0, keepdims=True)
            a = a + jnp.dot(vt_ref[0, :, rows], p.astype(MXU_DTYPE), preferred_element_type=F32)
            mx = produce(qs, kn_ref, s_buf, c, mx)
        mx_buf[...] = mx
        l_s[...] = ls
        acc[...] = a
        m_s[...] = m_new

        @pl.when(j == nk - 1)
        def _():
            o_t = acc[...] / l_s[...]
            o_ref[...] = jnp.concatenate([_untranspose_groups(o_t[:, GROUPS * a:GROUPS * (a + TM)], TM)
                                          for a in range(0, tq, TM)], axis=0)
            lse_ref[0, 0] = _row_to_col(m_s[...] + jnp.log(l_s[...]))

    kspec = lambda f: pl.BlockSpec((tk, KV_W), lambda h, i, j: (f(j), h))
    return pl.pallas_call(
        kern, name="glob_fwd", grid=(N_KV, nq, nk),
        in_specs=[pl.BlockSpec((tq, KV_W), lambda h, i, j: (i, h)), kspec(lambda j: 0),
                  kspec(lambda j: jnp.minimum(j + 1, nk - 1)), pl.BlockSpec((1, HEAD_DIM, tk), lambda h, i, j: (h, 0, j))],
        out_specs=[pl.BlockSpec((tq, KV_W), lambda h, i, j: (i, h)),
                   pl.BlockSpec((1, 1, r, 1), lambda h, i, j: (h, i, 0, 0))],
        out_shape=[SDS((s_rows, Q_W), F32), SDS((N_KV, nq, r, 1), F32)],
        scratch_shapes=[pltpu.VMEM((r, KV_W), MXU_DTYPE), pltpu.VMEM((tk, r), F32), pltpu.VMEM((1, r), F32),
                        pltpu.VMEM((1, r), F32), pltpu.VMEM((1, r), F32), pltpu.VMEM((HEAD_DIM, r), F32)],
        compiler_params=_attn_semantics(),
    )(q, kt, kt, v_t)


def attn_delta(o, do, s_rows):
    tq = _pick(s_rows, GLOB_BWD_TQ)
    nq = s_rows // tq
    r = GROUPS * tq

    def kern(o_ref, do_ref, d_ref):
        d_ref[0, 0] = jnp.sum(_stack_tiles(do_ref[...], F32) * _stack_tiles(o_ref[...], F32), axis=1, keepdims=True)

    qspec = pl.BlockSpec((tq, KV_W), lambda h, i: (i, h))
    return pl.pallas_call(
        kern, name="attn_delta", grid=(N_KV, nq), in_specs=[qspec, qspec],
        out_specs=pl.BlockSpec((1, 1, r, 1), lambda h, i: (h, i, 0, 0)), out_shape=SDS((N_KV, nq, r, 1), F32),
        compiler_params=_cparams(("parallel", "parallel")),
    )(o, do)


def _compact_t(tt, dtype):
    return jnp.concatenate([tt[g * HEAD_DIM:(g + 1) * HEAD_DIM, :] for g in range(GROUPS)], axis=1).astype(dtype)


def glob_bwd(q, q_t, kt, vt, do, do_t, lse, delta, h, s_rows):
    n = kt.shape[0]
    tq = _pick(s_rows, GLOB_BWD_TQ)
    tk = _pick(n, GLOB_TK)
    nq, nk = s_rows // tq, n // tk
    r = GROUPS * tq
    nch = tk // KEY_CHUNK

    def kern(q_ref, qt_ref, k_ref, v_ref, do_ref, dot_ref, lse_ref, dl_ref, dq_ref, dkt_ref, dvt_ref, p_buf, ds_buf):
        j = pl.program_id(0)
        i = pl.program_id(1)

        @pl.when(i == 0)
        def _():
            dkt_ref[...] = jnp.zeros_like(dkt_ref)
            dvt_ref[...] = jnp.zeros_like(dvt_ref)

        qs = _stack_tiles(q_ref[...], MXU_DTYPE)
        dos = _stack_tiles(do_ref[...], MXU_DTYPE)
        lse_b = jnp.broadcast_to(lse_ref[0, 0], (r, LANES))
        dl_b = jnp.broadcast_to(dl_ref[0, 0], (r, LANES))
        for c in range(nch):
            lo = c * KEY_CHUNK
            sc = _dot_nt(qs, k_ref[lo:lo + KEY_CHUNK, :])
            dpc = _dot_nt(dos, v_ref[lo:lo + KEY_CHUNK, :])
            for t in range(KEY_CHUNK // LANES):
                sl = slice(t * LANES, (t + 1) * LANES)
                pt = jnp.exp(sc[:, sl] - lse_b)
                p_buf[:, lo + t * LANES:lo + (t + 1) * LANES] = pt.astype(p_buf.dtype)
                ds_buf[:, lo + t * LANES:lo + (t + 1) * LANES] = (pt * (dpc[:, sl] - dl_b)).astype(ds_buf.dtype)
        dq_t = _fold_tiles(jnp.dot(ds_buf[...], k_ref[...], preferred_element_type=F32), tq)
        rows = pl.ds(pl.multiple_of(i * tq, tq), tq)

        @pl.when(j == 0)
        def _():
            dq_ref[rows, :] = dq_t

        @pl.when(j > 0)
        def _():
            dq_ref[rows, :] += dq_t

        dvt_ref[...] += jnp.dot(_compact_tiles_t(dot_ref[...], MXU_DTYPE), p_buf[...], preferred_element_type=F32)
        dkt_ref[...] += jnp.dot(_compact_tiles_t(qt_ref[...], MXU_DTYPE), ds_buf[...], preferred_element_type=F32)

    col = pl.BlockSpec((1, 1, r, 1), lambda j, i: (h, i, 0, 0))
    qspec = pl.BlockSpec((tq, KV_W), lambda j, i: (i, h))
    tspec = pl.BlockSpec((KV_W, tq), lambda j, i: (h, i))
    kspec = pl.BlockSpec((tk, KV_W), lambda j, i: (j, h))
    ospec = pl.BlockSpec((HEAD_DIM, tk), lambda j, i: (0, j))
    return pl.pallas_call(
        kern, name=f"glob_bwd_h{h}", grid=(nk, nq),
        in_specs=[qspec, tspec, kspec, kspec, qspec, tspec, col, col],
        out_specs=[pl.BlockSpec(memory_space=pltpu.VMEM), ospec, ospec],
        out_shape=[SDS((s_rows, KV_W), F32), SDS((HEAD_DIM, n), F32), SDS((HEAD_DIM, n), F32)],
        scratch_shapes=[pltpu.VMEM((r, tk), MXU_DTYPE), pltpu.VMEM((r, tk), MXU_DTYPE)],
        compiler_params=_cparams(("arbitrary", "arbitrary")),
    )(q, q_t, kt, vt, do, do_t, lse, delta)


TW = 2 * WIN
WR = GROUPS * TW
WLAT = 4 * WIN


def _win_cat(dst, parts):
    off = 0
    for p in parts:
        dst[off:off + p.shape[0], :] = p[...]
        off += p.shape[0]


def _win_specs(s_rows, c_rows):
    nb = s_rows // WIN
    prev = lambda i: jnp.maximum(2 * i - 1, 0)
    nxt = lambda i: jnp.minimum(2 * i + 2, nb - 1)
    rows = [pl.BlockSpec((WIN, KV_W), lambda h, i: (prev(i), h)), pl.BlockSpec((TW, KV_W), lambda h, i: (i, h)),
            pl.BlockSpec((WIN, KV_W), lambda h, i: (nxt(i), h)), pl.BlockSpec((c_rows, KV_W), lambda h, i: (s_rows // c_rows, h))]
    cols = [pl.BlockSpec((1, HEAD_DIM, WIN), lambda h, i: (h, 0, prev(i))), pl.BlockSpec((1, HEAD_DIM, TW), lambda h, i: (h, 0, i)),
            pl.BlockSpec((1, HEAD_DIM, WIN), lambda h, i: (h, 0, nxt(i))),
            pl.BlockSpec((1, HEAD_DIM, c_rows), lambda h, i: (h, 0, s_rows // c_rows))]
    return rows, cols


def _win_mask(i, s_rows, shape, keys_on_rows):
    a = lax.broadcasted_iota(jnp.int32, shape, 0)
    b = lax.broadcasted_iota(jnp.int32, shape, 1)
    kk, qq = (a, b) if keys_on_rows else (b, a)
    qpos = i * TW + (qq & (TW - 1))
    kpos = (2 * i - 1) * WIN + kk
    band = (jnp.abs(qpos - kpos) <= WIN) & (kpos >= 0) & (kpos < s_rows)
    return (kk >= WLAT) | band


def _untranspose_groups(o_t, tq):
    row = lax.broadcasted_iota(jnp.int32, (HEAD_DIM, KV_W), 0)
    col = lax.broadcasted_iota(jnp.int32, (HEAD_DIM, KV_W), 1)
    hi = o_t.astype(BF16)
    r1 = o_t - hi.astype(F32)
    mid = r1.astype(BF16)
    lo = (r1 - mid.astype(F32)).astype(BF16)
    o = jnp.zeros((tq, KV_W), F32)
    for g in range(GROUPS):
        sel = jnp.where(col == row + g * HEAD_DIM, 1.0, 0.0).astype(BF16)
        for term in (hi, mid, lo):
            o = o + lax.dot_general(term[:, g * tq:(g + 1) * tq], sel, (((0,), (0,)), ((), ())), preferred_element_type=F32)
    return o


def win_fwd(q, kt, v_t, sinkrow, s_rows, c_rows):
    nt = s_rows // TW
    nkeys = WLAT + c_rows

    def kern(q_ref, kp, kc, kn, kx, vp, vc, vn, vx, sink_ref, o_ref, lse_ref, kcat):
        i = pl.program_id(1)
        _win_cat(kcat, (kp, kc, kn, kx))
        qs = _stack_groups(q_ref[...], MXU_DTYPE)
        st = _dot_nt(kcat[...], qs)
        st = jnp.where(_win_mask(i, s_rows, st.shape, True), st, -jnp.inf)
        sink = sink_ref[0]
        m = jnp.maximum(jnp.max(st, axis=0, keepdims=True), sink)
        e = jnp.exp(st - m)
        den = jnp.sum(e, axis=0, keepdims=True) + jnp.exp(sink - m)
        v_cat = jnp.concatenate([vp[0], vc[0], vn[0], vx[0]], axis=1)
        o_t = jnp.dot(v_cat, e.astype(MXU_DTYPE), preferred_element_type=F32) / den
        o_ref[...] = _untranspose_groups(o_t, TW)
        lse_ref[0, 0] = _row_to_col(m + jnp.log(den))

    rows, cols = _win_specs(s_rows, c_rows)
    qspec = pl.BlockSpec((TW, KV_W), lambda h, i: (i, h))
    rowv = pl.BlockSpec((1, 1, WR, 1), lambda h, i: (h, i, 0, 0))
    return pl.pallas_call(
        kern, name="win_fwd", grid=(N_KV, nt),
        in_specs=[qspec] + rows + cols + [pl.BlockSpec((1, 1, WR), lambda h, i: (h, 0, 0))],
        out_specs=[qspec, rowv], out_shape=[SDS((s_rows, Q_W), F32), SDS((N_KV, nt, WR, 1), F32)],
        scratch_shapes=[pltpu.VMEM((nkeys, KV_W), MXU_DTYPE)],
        compiler_params=_cparams(("parallel", "parallel")),
    )(q, kt, kt, kt, kt, v_t, v_t, v_t, v_t, sinkrow)


def win_bwd(q, q_t, kt, vt, sinkcol, o, do, do_t, lse, s_rows, c_rows):
    nt = s_rows // TW
    nkeys = WLAT + c_rows
    n = s_rows + c_rows
    ctx0 = WIN + s_rows

    def kern(q_ref, qt_ref, kp, kc, kn, kx, vp, vc, vn, vx, sink_ref, o_ref, do_ref, dot_ref, lse_ref,
             dq_ref, dkt_ref, dvt_ref, dsk_ref, kcat, vcat):
        i = pl.program_id(1)

        @pl.when(i == 0)
        def _():
            dkt_ref[...] = jnp.zeros_like(dkt_ref)
            dvt_ref[...] = jnp.zeros_like(dvt_ref)
            dsk_ref[...] = jnp.zeros_like(dsk_ref)

        _win_cat(kcat, (kp, kc, kn, kx))
        _win_cat(vcat, (vp, vc, vn, vx))
        qs = _stack_groups(q_ref[...], MXU_DTYPE)
        do32 = _stack_groups(do_ref[...], F32)
        delta = jnp.sum(do32 * _stack_groups(o_ref[...], F32), axis=1, keepdims=True)
        dos = do32.astype(MXU_DTYPE)
        lse_c = lse_ref[0, 0]
        s = _dot_nt(qs, kcat[...])
        s = jnp.where(_win_mask(i, s_rows, s.shape, False), s, -jnp.inf)
        p = jnp.exp(s - lse_c)
        ds = p * (_dot_nt(dos, vcat[...]) - delta)
        dq_ref[...] = _fold_groups(_dot(ds, kcat[...]), TW)
        dvt = jnp.dot(_compact_t(dot_ref[...], MXU_DTYPE), p.astype(MXU_DTYPE), preferred_element_type=F32)
        dkt = jnp.dot(_compact_t(qt_ref[...], MXU_DTYPE), ds.astype(MXU_DTYPE), preferred_element_type=F32)
        lat = pl.ds(pl.multiple_of(i * TW, TW), WLAT)
        dkt_ref[0, :, lat] += dkt[:, :WLAT]
        dvt_ref[0, :, lat] += dvt[:, :WLAT]
        dkt_ref[0, :, ctx0:ctx0 + c_rows] += dkt[:, WLAT:]
        dvt_ref[0, :, ctx0:ctx0 + c_rows] += dvt[:, WLAT:]
        dsk_ref[0] += -(jnp.exp(sink_ref[0][:, 0:1] - lse_c) * delta)

    rows, _ = _win_specs(s_rows, c_rows)
    qspec = pl.BlockSpec((TW, KV_W), lambda h, i: (i, h))
    tspec = pl.BlockSpec((KV_W, TW), lambda h, i: (h, i))
    col = pl.BlockSpec((1, 1, WR, 1), lambda h, i: (h, i, 0, 0))
    kvt = pl.BlockSpec((1, HEAD_DIM, WIN + n), lambda h, i: (h, 0, 0))
    return pl.pallas_call(
        kern, name="win_bwd", grid=(N_KV, nt),
        in_specs=[qspec, tspec] + rows + rows + [pl.BlockSpec((1, WR, LANES), lambda h, i: (h, 0, 0)), qspec, qspec, tspec, col],
        out_specs=[qspec, kvt, kvt, pl.BlockSpec((1, WR, 1), lambda h, i: (h, 0, 0))],
        out_shape=[SDS((s_rows, Q_W), F32), SDS((N_KV, HEAD_DIM, WIN + n), F32), SDS((N_KV, HEAD_DIM, WIN + n), F32),
                   SDS((N_KV, WR, 1), F32)],
        scratch_shapes=[pltpu.VMEM((nkeys, KV_W), MXU_DTYPE), pltpu.VMEM((nkeys, KV_W), MXU_DTYPE)],
        compiler_params=_cparams(("arbitrary", "arbitrary")),
    )(q, q_t, kt, kt, kt, kt, vt, vt, vt, vt, sinkcol, o, do, do_t, lse)


def adamw(name, w, m, v, grads):
    r, wd = w.shape
    tr = _pick(r, [t for t in ELEMENTWISE_ROWS if t * wd * 4 <= ELEMENTWISE_BLOCK_BYTES])
    stacked = not isinstance(grads, (list, tuple))
    ng = grads.shape[0] if stacked else len(grads)

    def kern(*refs):
        w_ref, m_ref, v_ref = refs[:3]
        g_refs = refs[3:-4]
        g_out, d_out, m_out, v_out = refs[-4:]
        if stacked:
            g = g_refs[0][0]
            for k in range(1, ng):
                g = g + g_refs[0][k]
        else:
            g = g_refs[0][...]
            for gr in g_refs[1:]:
                g = g + gr[...]
        wv = w_ref[...]
        mn = ADAM_B1 * m_ref[...] + (1.0 - ADAM_B1) * g
        vn = ADAM_B2 * v_ref[...] + (1.0 - ADAM_B2) * (g * g)
        m_hat = mn / (1.0 - ADAM_B1 ** ADAM_STEP)
        v_hat = vn / (1.0 - ADAM_B2 ** ADAM_STEP)
        g_out[...] = g
        d_out[...] = -ADAM_LR * (m_hat / (jnp.sqrt(v_hat) + ADAM_EPS) + ADAM_WD * wv)
        m_out[...] = mn
        v_out[...] = vn

    spec = pl.BlockSpec((tr, wd), lambda i: (i, 0))
    gspecs = [pl.BlockSpec((ng, tr, wd), lambda i: (0, i, 0))] if stacked else [spec] * ng
    return pl.pallas_call(
        kern, name=name, grid=(r // tr,), in_specs=[spec] * 3 + gspecs, out_specs=[spec] * 4,
        out_shape=[SDS((r, wd), F32)] * 4, compiler_params=_cparams(("parallel",)),
    )(w, m, v, *([grads] if stacked else grads))


def add2(name, a, b):
    k, r, w = a.shape
    tr = _pick(r, [t for t in ELEMENTWISE_ROWS if t * w * 4 <= ELEMENTWISE_BLOCK_BYTES])

    def kern(a_ref, b_ref, o_ref):
        o_ref[...] = a_ref[...] + b_ref[...]

    spec = pl.BlockSpec((1, tr, w), lambda s, i: (s, i, 0))
    return pl.pallas_call(kern, name=name, grid=(k, r // tr), in_specs=[spec, spec], out_specs=spec,
                          out_shape=SDS(a.shape, a.dtype), compiler_params=_cparams(("parallel", "parallel")))(a, b)


def _rep4(a, off):
    return jnp.concatenate([a[:, off + HEAD_DIM * h: off + HEAD_DIM * (h + 1)] for h in range(N_KV) for _ in range(GROUPS)], axis=1)


def _extend_cols(a):
    return jnp.concatenate([a[:, 0:OFF_KA], _rep4(a, OFF_KA), _rep4(a, OFF_VA), a[:, OFF_QB:OFF_KB],
                            _rep4(a, OFF_KB), _rep4(a, OFF_VB), a[:, OFF_GA:]], axis=1)


def _fold4(a, off):
    r = a.shape[0]
    return a[:, off:off + Q_W].reshape(r, N_KV, GROUPS, HEAD_DIM).sum(axis=2).reshape(r, N_KV * HEAD_DIM)


def _fold_cols(a):
    return jnp.concatenate([a[:, X_QA:X_QA + Q_W], _fold4(a, X_KA), _fold4(a, X_VA), a[:, X_QB:X_QB + Q_W],
                            _fold4(a, X_KB), _fold4(a, X_VB), a[:, X_GL:]], axis=1)


def _rope_tables(s_rows, c_rows):
    n_rows = s_rows // GRID_W
    n_freq = HEAD_DIM // 4
    inv_freq = ROPE_THETA ** (-jnp.arange(n_freq, dtype=F32) / n_freq)
    ang_r = jnp.arange(n_rows, dtype=jnp.int32).astype(F32)[:, None] * inv_freq
    ang_c = jnp.arange(GRID_W, dtype=jnp.int32).astype(F32)[:, None] * inv_freq
    by_row = lambda t: jnp.repeat(t, GRID_W, axis=0)
    by_col = lambda t: jnp.tile(t, (n_rows, 1))
    cos = jnp.concatenate([by_row(jnp.cos(ang_r))] * 2 + [by_col(jnp.cos(ang_c))] * 2, axis=1)
    sin_r, sin_c = by_row(jnp.sin(ang_r)), by_col(jnp.sin(ang_c))
    sin = jnp.concatenate([-sin_r, sin_r, -sin_c, sin_c], axis=1)
    cos = jnp.concatenate([cos, jnp.ones((c_rows, HEAD_DIM), F32)], axis=0)
    sin = jnp.concatenate([sin, jnp.zeros((c_rows, HEAD_DIM), F32)], axis=0)
    return jnp.concatenate([cos, cos], axis=1), jnp.concatenate([sin, sin], axis=1)


def _ff_pad_cols(a):
    r = a.shape[0]
    a = jnp.pad(a.reshape(r, N_DEV, FF_SHARD), ((0, 0), (0, 0), (0, FF_SHARD_PAD - FF_SHARD)))
    return a.reshape(r, 2 * FF)


def _ff_unpad_cols(a):
    r = a.shape[0]
    return a.reshape(r, N_DEV, FF_SHARD_PAD)[:, :, :FF_SHARD].reshape(r, 2 * D_FF)


def _ff_pad_rows(a):
    c = a.shape[1]
    a = jnp.pad(a.reshape(N_DEV // 2, FF_SHARD, c), ((0, 0), (0, FF_SHARD_PAD - FF_SHARD), (0, 0)))
    return a.reshape(FF, c)


def _ff_unpad_rows(a):
    c = a.shape[1]
    return a.reshape(N_DEV // 2, FF_SHARD_PAD, c)[:, :FF_SHARD].reshape(D_FF, c)


BIG = (("w_in", (D, IN_COLS // N_DEV)), ("w_branch_a", (Q_W, D // N_DEV)), ("w_branch_b", (Q_W, D // N_DEV)),
       ("w_out", (D // N_DEV, D)), ("w_up", (D, FF_SHARD_PAD)), ("w_down", (D_FF // N_DEV, D)))
BIG_SIZES = tuple(int(np.prod(s)) for _, s in BIG)
BIG_ROWS = sum(BIG_SIZES) // LANES


def _pack_big(parts):
    return jnp.concatenate([p.reshape(-1) for p in parts]).reshape(BIG_ROWS, LANES)


def _unpack_big(flat):
    lead = flat.shape[:-2]
    f = flat.reshape(*lead, BIG_ROWS * LANES)
    out, off = [], 0
    for (_, shp), sz in zip(BIG, BIG_SIZES, strict=True):
        out.append(f[..., off:off + sz].reshape(*lead, *shp))
        off += sz
    return out


def _cols_to_full(g):
    return jnp.transpose(g, (1, 0, 2)).reshape(g.shape[1], -1)


def _full_to_cols(a):
    r, c = a.shape
    return jnp.transpose(a.reshape(r, N_DEV, c // N_DEV), (1, 0, 2))


SMALL = (("c_ctx", D), ("b_mod", N_MOD * D), ("b_in", IN_COLS), ("attn_sink", N_HEADS), ("q_norm_g", HEAD_DIM),
         ("k_norm_g", HEAD_DIM), ("ln1_g", D), ("ln1_b", D), ("conv_w", 3 * 2 * D_FF // N_DEV), ("conv_b", 2 * D_FF),
         ("ln2_g", D), ("ln2_b", D))
SMALL_TOTAL = sum(n for _, n in SMALL)
SMALL_ROWS = -(-SMALL_TOTAL // (8 * LANES)) * 8


def _pack_small(parts):
    flat = jnp.concatenate([p.reshape(-1).astype(F32) for p in parts])
    return jnp.pad(flat, (0, SMALL_ROWS * LANES - flat.shape[0])).reshape(SMALL_ROWS, LANES)


def _unpack_small(packed):
    f = packed.reshape(-1)
    out, off = {}, 0
    for name, n in SMALL:
        out[name] = f[off:off + n]
        off += n
    return out


RED = (("c_ctx", D), ("b_in", IN_COLS), ("attn_sink", N_HEADS), ("q_norm_g", HEAD_DIM), ("k_norm_g", HEAD_DIM),
       ("ln1_g", D), ("ln1_b", D), ("conv_w", 3 * 2 * FF), ("conv_b", 2 * FF), ("ln2_g", D), ("ln2_b", D))
RED_TOTAL = sum(n for _, n in RED)
RED_ROWS = -(-RED_TOTAL // (8 * LANES)) * 8


def sum8(name, g):
    _, r, w = g.shape

    def kern(g_ref, o_ref):
        acc = g_ref[0]
        for k in range(1, N_DEV):
            acc = acc + g_ref[k]
        o_ref[...] = acc

    return pl.pallas_call(kern, name=name, out_shape=SDS((r, w), F32))(g)


def _local_step(x, ctx, target, modrows, weights, small):
    s_rows, c_rows = x.shape[0], ctx.shape[0]
    n = s_rows + c_rows
    nl = s_rows // TM
    w_in, wba, wbb, w_out, w_up, w_down = weights
    f = lambda a: a.reshape(1, -1).astype(F32)
    b_in, ln1_g, ln1_b, ln2_g, ln2_b, conv_b = (f(small[k]) for k in ("b_in", "ln1_g", "ln1_b", "ln2_g", "ln2_b", "conv_b"))
    conv_w8 = jnp.pad(small["conv_w_full"], ((0, 5), (0, 0)))
    qg = jnp.tile(small["q_norm_g"].reshape(1, HEAD_DIM), (1, N_HEADS))
    kg = jnp.tile(small["k_norm_g"].reshape(1, HEAD_DIM), (1, N_HEADS))
    sink_rep = jnp.repeat(small["attn_sink"].reshape(N_KV, GROUPS), TW, axis=1)
    sinkrow = sink_rep.reshape(N_KV, 1, WR)
    sinkcol = jnp.broadcast_to(sink_rep[:, :, None], (N_KV, WR, LANES))
    bd = jnp.kron(jnp.eye(N_HEADS, dtype=F32), jnp.ones((HEAD_DIM, HEAD_DIM), F32)).astype(BF16)
    cos, sin = _rope_tables(s_rows, c_rows)
    w_ext = _extend_cols(w_in)
    b_ext = _extend_cols(b_in)
    xa = jnp.concatenate([x, ctx], axis=0)

    hb, qa, kat, vat, qb, kbt, vbt, tq, rq, tk, rk, gl = inproj_fwd(xa, cos, sin, modrows, w_ext, b_ext, qg, kg, bd, nl)
    compact_t = lambda t: jnp.stack([t[:, h * KV_W:h * KV_W + HEAD_DIM].T for h in range(N_KV)])
    oa, lse_a = win_fwd(qa, kat, compact_t(vat), sinkrow, s_rows, c_rows)
    vb_t = compact_t(vbt)
    ob, lse_b = glob_fwd(qb, kbt, vb_t, s_rows)
    tqb = _pick(s_rows, GLOB_BWD_TQ)
    lse_b = lse_b.reshape(N_KV, s_rows // tqb, GROUPS * tqb, 1)
    ya, yb, mrg, y, xhat1, rstd1 = merge_fwd(oa, ob, gl, x, modrows, wba, wbb, w_out, s_rows)
    h2, u0 = ffn_up_fwd(xhat1, modrows, ln1_g, ln1_b, w_up, s_rows)
    a = conv_swiglu_fwd(u0, conv_w8, conv_b, s_rows)
    (dr2, dy2), (loss, dln2_g, dln2_b, dgate2) = ffn_down_loss(a, xhat1, target, modrows, ln1_g, ln1_b, ln2_g, ln2_b, w_down, s_rows)

    da = ffn_down_bwd(dy2, w_down.T, s_rows)
    dw_down = mm_tn("dw_down", a, dy2, s_rows)
    du0, (dconv_b, dcw0, dcw1, dcw2) = swiglu_conv_bwd(u0, da, conv_w8, conv_b, s_rows)
    dw_up = mm_tn("dw_up", h2, du0, s_rows)
    (dy, dxp), (dscale2, dshift2, dln1_g, dln1_b, dgate1) = ffn_up_ln1_bwd(du0, dr2, xhat1, y, rstd1, modrows, ln1_g, ln1_b, w_up.T, s_rows)
    dya, dyb, dgl, doa, dob = merge_bwd(dy, ya, yb, gl, w_out.T, wba.T, wbb.T, s_rows)
    dw_out = mm_tn("dw_out", mrg, dy, s_rows)
    dwba = mm_tn("dw_branch_a", oa, dya, s_rows)
    dwbb = mm_tn("dw_branch_b", ob, dyb, s_rows)

    head_rows = lambda t: t.reshape(N_KV * HEAD_DIM, t.shape[-1])
    place = np.zeros((N_KV * HEAD_DIM, Q_W), np.float32)
    for h in range(N_KV):
        place[h * HEAD_DIM + np.arange(HEAD_DIM), h * KV_W + np.arange(HEAD_DIM)] = 1.0
    place = jnp.asarray(place, BF16)
    dqa, dka_t, dva_t, dsk = win_bwd(qa, qa[:s_rows].T, kat, vat, sinkcol, oa, doa, doa.astype(MXU_DTYPE).T, lse_a,
                                     s_rows, c_rows)
    dka_t, dva_t = head_rows(dka_t[:, :, WIN:]), head_rows(dva_t[:, :, WIN:])
    delta_b = attn_delta(ob, dob, s_rows)
    qb_t = qb[:s_rows].T
    dob_t = dob.astype(MXU_DTYPE).T
    heads = [glob_bwd(qb, qb_t, kbt, vbt, dob, dob_t, lse_b, delta_b, h, s_rows) for h in range(N_KV)]
    dqb = [hd[0] for hd in heads]
    dkb_t = jnp.concatenate([hd[1] for hd in heads], axis=0)
    dvb_t = jnp.concatenate([hd[2] for hd in heads], axis=0)
    dproj, (db_ext, dqg, dkg) = qk_bwd(dqa, dka_t, dva_t, dqb, dkb_t, dvb_t, dgl, tq, rq, tk, rk, cos, sin, qg, kg, bd,
                                       place, nl, n)
    w_ext_t = w_ext.T
    (grad_x,), (dscale1, dshift1) = inproj_bwd("inproj_bwd", dproj, xa, dxp, modrows, w_ext_t, ntiles=nl, tile_off=0,
                                               is_ctx=False, out_rows=s_rows)
    _, (dscale_c, dshift_c) = inproj_bwd("inproj_bwd_ctx", dproj, xa, None, modrows, w_ext_t, ntiles=c_rows // TM,
                                         tile_off=nl, is_ctx=True, out_rows=0)
    dw_in = _fold_cols(mm_tn("dw_in", hb, dproj, n))

    dmod = jnp.concatenate([dshift1, dscale1, dgate1, dshift2, dscale2, dgate2], axis=1)
    dmod_c = jnp.concatenate([dshift_c, dscale_c, jnp.zeros((1, (N_MOD - 2) * D), F32)], axis=1)
    fold_g = lambda t: t.reshape(N_HEADS, HEAD_DIM).sum(axis=0)
    red = {
        "b_in": _fold_cols(db_ext), "attn_sink": dsk.reshape(N_HEADS, TW).sum(axis=1), "q_norm_g": fold_g(dqg),
        "k_norm_g": fold_g(dkg), "ln1_g": dln1_g, "ln1_b": dln1_b, "conv_w": jnp.concatenate([dcw0, dcw1, dcw2], axis=0),
        "conv_b": dconv_b, "ln2_g": dln2_g, "ln2_b": dln2_b,
    }
    return loss[0, 0], grad_x, (dw_in, dwba, dwbb, dw_out, dw_up, dw_down), dmod, dmod_c, red


def kernel(x, c, ctx, c_ctx, w_mod, b_mod, w_in, b_in, attn_sink, q_norm_g, k_norm_g, w_branch_a, w_branch_b, w_out, ln1_g, ln1_b, w_up, conv_w, conv_b, w_down, ln2_g, ln2_b, loss_target, m_c_ctx, m_w_mod, m_b_mod, m_w_in, m_b_in, m_attn_sink, m_q_norm_g, m_k_norm_g, m_w_branch_a, m_w_branch_b, m_w_out, m_ln1_g, m_ln1_b, m_w_up, m_conv_w, m_conv_b, m_w_down, m_ln2_g, m_ln2_b, v_c_ctx, v_w_mod, v_b_mod, v_w_in, v_b_in, v_attn_sink, v_q_norm_g, v_k_norm_g, v_w_branch_a, v_w_branch_b, v_w_out, v_ln1_g, v_ln1_b, v_w_up, v_conv_w, v_conv_b, v_w_down, v_ln2_g, v_ln2_b):
    ax, ay, ac = (lax.axis_index(a) for a in AXES)
    me = 4 * ax + 2 * ay + ac
    chip = 2 * ax + ay
    mod_w = N_MOD * D // N_DEV
    params = dict(c_ctx=c_ctx, w_mod=w_mod, b_mod=b_mod, w_in=w_in, b_in=b_in, attn_sink=attn_sink, q_norm_g=q_norm_g,
                  k_norm_g=k_norm_g, w_branch_a=w_branch_a, w_branch_b=w_branch_b, w_out=w_out, ln1_g=ln1_g, ln1_b=ln1_b,
                  w_up=w_up, conv_w=conv_w, conv_b=conv_b, w_down=w_down, ln2_g=ln2_g, ln2_b=ln2_b)
    mom_m = dict(c_ctx=m_c_ctx, w_mod=m_w_mod, b_mod=m_b_mod, w_in=m_w_in, b_in=m_b_in, attn_sink=m_attn_sink,
                 q_norm_g=m_q_norm_g, k_norm_g=m_k_norm_g, w_branch_a=m_w_branch_a, w_branch_b=m_w_branch_b, w_out=m_w_out,
                 ln1_g=m_ln1_g, ln1_b=m_ln1_b, w_up=m_w_up, conv_w=m_conv_w, conv_b=m_conv_b, w_down=m_w_down,
                 ln2_g=m_ln2_g, ln2_b=m_ln2_b)
    mom_v = dict(c_ctx=v_c_ctx, w_mod=v_w_mod, b_mod=v_b_mod, w_in=v_w_in, b_in=v_b_in, attn_sink=v_attn_sink,
                 q_norm_g=v_q_norm_g, k_norm_g=v_k_norm_g, w_branch_a=v_w_branch_a, w_branch_b=v_w_branch_b, w_out=v_w_out,
                 ln1_g=v_ln1_g, ln1_b=v_ln1_b, w_up=v_w_up, conv_w=v_conv_w, conv_b=v_conv_b, w_down=v_w_down,
                 ln2_g=v_ln2_g, ln2_b=v_ln2_b)
    big_names = [nm for nm, _ in BIG]

    def shard(tree, nm):
        t = tree[nm][0]
        return jnp.pad(t, ((0, 0), (0, FF_SHARD_PAD - FF_SHARD))) if nm == "w_up" else t

    wg = all_gather("ag_weights", _pack_big([shard(params, nm).astype(MXU_DTYPE) for nm in big_names]))
    g_in, g_ba, g_bb, g_out, g_up, g_down = _unpack_big(wg)
    weights = (_cols_to_full(g_in), _cols_to_full(g_ba), _cols_to_full(g_bb), g_out.reshape(D, D), _cols_to_full(g_up),
               _ff_pad_rows(g_down.reshape(D_FF, D)))

    c_all = all_gather("ag_c", c.reshape(8, LANES)).reshape(N_DEV, D)
    cs = jnp.concatenate([c_all, c_ctx.reshape(1, D), jnp.zeros((7, D), F32)], axis=0)
    w_mod_sh = w_mod[0]
    b_mod_sh = lax.dynamic_slice(b_mod, (0, me * mod_w), (1, mod_w))
    mod_part = mod_fwd(cs, w_mod_sh, b_mod_sh)
    mg = all_gather("ag_mod", mod_part.reshape(16 * mod_w // LANES, LANES)).reshape(N_DEV, 16, mod_w)
    mod = lax.dynamic_index_in_dim(mg, me, axis=1, keepdims=False).reshape(N_MOD, D)
    mod_c = mg[:, 8, :].reshape(N_MOD, D)
    modrows = jnp.stack([mod[0], mod[1], mod_c[0], mod_c[1], mod[2], mod[3], mod[4], mod[5]], axis=0)

    conv_w_full = all_gather("ag_conv_w", jnp.pad(conv_w[0], ((0, 5), (0, FF_SHARD_PAD - FF_SHARD))))
    conv_w_full = _cols_to_full(conv_w_full[:, :3, :])
    small = dict(b_in=b_in, ln1_g=ln1_g, ln1_b=ln1_b, ln2_g=ln2_g, ln2_b=ln2_b, conv_b=_ff_pad_cols(conv_b),
                 conv_w_full=conv_w_full, q_norm_g=q_norm_g, k_norm_g=k_norm_g, attn_sink=attn_sink)
    loss, grad_x, big_grads, dmod, dmod_c, red = _local_step(x[0], ctx[0], loss_target[0], modrows, weights, small)
    loss = lax.psum(loss, AXES)

    dm = all_gather("ag_dmod", jnp.concatenate([dmod, dmod_c], axis=0).reshape(2 * N_MOD * D // LANES, LANES))
    dm = dm.reshape(N_DEV, 2, N_MOD * D)
    dm_all = jnp.concatenate([dm[:, 0], dm[:, 1]], axis=0)
    dm_sh = lax.dynamic_slice(dm_all, (0, me * mod_w), (16, mod_w))
    dw_mod, dcc, db_mod = mod_bwd(cs, w_mod_sh, dm_sh, dm_all)
    red["c_ctx"] = dcc[8]

    red_vec = jnp.concatenate([red[nm].reshape(-1) for nm, _ in RED])
    red_vec = jnp.pad(red_vec, (0, RED_ROWS * LANES - RED_TOTAL)).reshape(RED_ROWS, LANES)
    red_sum = sum8("sum_small", all_gather("ag_small", red_vec)).reshape(-1)
    gsm, off = {}, 0
    for nm, k in RED:
        gsm[nm] = red_sum[off:off + k]
        off += k
    gsm["b_mod"] = db_mod.reshape(-1)
    gsm["conv_b"] = _ff_unpad_cols(gsm["conv_b"].reshape(1, 2 * FF))
    gsm["conv_w"] = lax.dynamic_slice(gsm["conv_w"].reshape(3, 2 * FF), (0, me * FF_SHARD_PAD), (3, FF_SHARD_PAD))[:, :FF_SHARD]
    sm_names = [nm for nm, _ in SMALL]
    gs, ds, ms, vs = adamw("adamw_small", _pack_small([params[nm] for nm in sm_names]),
                           _pack_small([mom_m[nm] for nm in sm_names]), _pack_small([mom_v[nm] for nm in sm_names]),
                           [_pack_small([gsm[nm] for nm in sm_names])])
    sm_out = [_unpack_small(t) for t in (gs, ds, ms, vs)]

    dw_in, dwba, dwbb, dw_out, dw_up, dw_down = big_grads
    slabs = jnp.concatenate([t.reshape(N_DEV, -1) for t in (
        _full_to_cols(dw_in), _full_to_cols(dwba), _full_to_cols(dwbb), dw_out, _full_to_cols(dw_up),
        _ff_unpad_rows(dw_down))], axis=1)
    by_core = slabs.reshape(4, 2, BIG_ROWS, LANES)
    keep = lax.dynamic_index_in_dim(by_core, ac, axis=1, keepdims=False)
    give = lax.dynamic_index_in_dim(by_core, 1 - ac, axis=1, keepdims=False)
    got = exchange("rs_sibling", give.reshape(1, 4 * BIG_ROWS, LANES), to_chips=False).reshape(4, BIG_ROWS, LANES)
    pair = add2("rs_pair_sum", keep, got)
    outbox = jnp.stack([lax.dynamic_index_in_dim(pair, jnp.bitwise_xor(chip, m), axis=0, keepdims=False) for m in (1, 2, 3)])
    inbox = exchange("rs_chips", outbox.astype(MXU_DTYPE), to_chips=True)
    mine = lax.dynamic_index_in_dim(pair, chip, axis=0, keepdims=False)
    gb, db, mb, vb = adamw("adamw_big", _pack_big([shard(params, nm) for nm in big_names]),
                           _pack_big([shard(mom_m, nm) for nm in big_names]), _pack_big([shard(mom_v, nm) for nm in big_names]),
                           [mine, inbox[0], inbox[1], inbox[2]])
    big_out = [dict(zip(big_names, _unpack_big(t), strict=True)) for t in (gb, db, mb, vb)]
    for out in big_out:
        out["w_up"] = out["w_up"][:, :FF_SHARD]
    gm, dmo, mmo, vmo = adamw("adamw_mod", w_mod[0], m_w_mod[0], v_w_mod[0], [dw_mod])
    mod_out = (gm, dmo, mmo, vmo)

    order = ["c_ctx", "w_mod", "b_mod", "w_in", "b_in", "attn_sink", "q_norm_g", "k_norm_g", "w_branch_a", "w_branch_b",
             "w_out", "ln1_g", "ln1_b", "w_up", "conv_w", "conv_b", "w_down", "ln2_g", "ln2_b"]
    results = [loss, grad_x[None]]
    for kind in range(4):
        for nm in order:
            if nm == "w_mod":
                val = mod_out[kind]
            elif nm in big_out[kind]:
                val = big_out[kind][nm]
            else:
                val = sm_out[kind][nm]
            results.append(val.reshape(params[nm].shape))
    return tuple(results)
```

```python
import functools

import jax
import jax.numpy as jnp
import numpy as np
from jax import lax
from jax.experimental import pallas as pl
from jax.experimental.pallas import tpu as pltpu

F32 = jnp.float32
BF16 = jnp.bfloat16
MXU_DTYPE = BF16

AXES = ("x", "y", "c")
N_DEV = 8
D = 1024
HEAD_DIM = 64
N_HEADS = 8
N_KV = 2
GROUPS = 4
KV_W = GROUPS * HEAD_DIM
Q_W = N_HEADS * HEAD_DIM
GRID_W = 64
WIN = 128
ROPE_THETA = 10000.0
D_FF = 2816
FF_SHARD = 2 * D_FF // N_DEV
FF_SHARD_PAD = 768
FF = N_DEV // 2 * FF_SHARD_PAD
LN_EPS = 1e-5
QK_EPS = 1e-6
N_MOD = 6
ALPHA = 2.0 ** 0.25
Q_SCALE = HEAD_DIM ** -0.5
IN_COLS = 3584
OFF_KA, OFF_VA, OFF_QB, OFF_KB, OFF_VB, OFF_GA = 512, 640, 768, 1280, 1408, 1536
EXT_COLS = 6 * Q_W + 2 * D
X_QA, X_KA, X_VA, X_QB, X_KB, X_VB, X_GL = 0, 512, 1024, 1536, 2048, 2560, 3072
ADAM_LR, ADAM_B1, ADAM_B2, ADAM_EPS, ADAM_WD, ADAM_STEP = 0.001, 0.9, 0.999, 1e-08, 0.01, 10
LANES = 128
TM = 256
VMEM_LIMIT = 56 * 1024 * 1024
ELEMENTWISE_BLOCK_BYTES = 1 << 20
MM_TN_LHS_BYTES = 8 << 20
ELEMENTWISE_ROWS = (1824, 1408, 1024, 512, 256, 128, 64, 32, 16, 8)

ANY = pl.BlockSpec(memory_space=pl.ANY)
SDS = jax.ShapeDtypeStruct


def _pick(n, candidates):
    for t in candidates:
        if n % t == 0:
            return t
    raise ValueError(f"no tile for {n}")


def _full(a):
    nd = a.ndim
    return pl.BlockSpec(a.shape, lambda *_: (0,) * nd)


def _rows(tm, w, fn=lambda t: t):
    return pl.BlockSpec((tm, w), lambda i: (fn(i), 0))


def _dot(a, b):
    return jnp.dot(a.astype(MXU_DTYPE), b.astype(MXU_DTYPE), preferred_element_type=F32)


def _dot_nt(a, b):
    return lax.dot_general(a.astype(MXU_DTYPE), b.astype(MXU_DTYPE), (((1,), (1,)), ((), ())), preferred_element_type=F32)


def _dot_tn(a, b):
    return lax.dot_general(a.astype(MXU_DTYPE), b.astype(MXU_DTYPE), (((0,), (0,)), ((), ())), preferred_element_type=F32)


def _cparams(sem):
    return pltpu.CompilerParams(dimension_semantics=sem, vmem_limit_bytes=VMEM_LIMIT)


def all_gather(name, v):
    r, w = v.shape

    def body(x_ref, out_ref, send_sems, recv_sems, local_sem):
        x, y, c = (lax.axis_index(a) for a in AXES)
        me, sibling = (x, y, c), (x, y, 1 - c)
        chips = [(1 - x, y), (x, 1 - y), (1 - x, 1 - y)]

        def rows(px, py, pc):
            return out_ref.at[4 * px + 2 * py + pc]

        def copy(k, block, to, src=None):
            return pltpu.make_async_remote_copy(
                src_ref=rows(*block) if src is None else src, dst_ref=rows(*block),
                send_sem=send_sems.at[k], recv_sem=recv_sems.at[k],
                device_id=to, device_id_type=pl.DeviceIdType.MESH)

        mine = pltpu.make_async_copy(x_ref, rows(*me), local_sem)
        mine.start()
        first = [copy(0, me, sibling, src=x_ref)]
        first += [copy(1 + j, me, (*chip, c), src=x_ref) for j, chip in enumerate(chips)]
        for cp in first:
            cp.start()
        passed = [copy(4 + j, (*chip, c), sibling) for j, chip in enumerate(chips)]
        for j, chip in enumerate(chips):
            copy(1 + j, (*chip, c), me).wait_recv()
            passed[j].start()
        copy(0, sibling, me).wait_recv()
        for j, chip in enumerate(chips):
            copy(4 + j, (*chip, 1 - c), me).wait_recv()
        for cp in first + passed:
            cp.wait_send()
        mine.wait()

    return pl.pallas_call(
        body, name=name, out_shape=SDS((N_DEV, r, w), v.dtype), in_specs=[ANY], out_specs=ANY,
        scratch_shapes=[pltpu.SemaphoreType.DMA((7,)), pltpu.SemaphoreType.DMA((7,)), pltpu.SemaphoreType.DMA],
    )(v)


def exchange(name, outbox, to_chips):
    k = outbox.shape[0]
    assert k == (3 if to_chips else 1)

    def body(out_ref, in_ref, send_sems, recv_sems):
        x, y, c = (lax.axis_index(a) for a in AXES)
        peers = [(x, 1 - y, c), (1 - x, y, c), (1 - x, 1 - y, c)] if to_chips else [(x, y, 1 - c)]
        copies = [
            pltpu.make_async_remote_copy(
                src_ref=out_ref.at[m], dst_ref=in_ref.at[m], send_sem=send_sems.at[m], recv_sem=recv_sems.at[m],
                device_id=peer, device_id_type=pl.DeviceIdType.MESH)
            for m, peer in enumerate(peers)
        ]
        for cp in copies:
            cp.start()
        for cp in copies:
            cp.wait_recv()
        for cp in copies:
            cp.wait_send()

    return pl.pallas_call(
        body, name=name, out_shape=SDS(outbox.shape, outbox.dtype), in_specs=[ANY], out_specs=ANY,
        scratch_shapes=[pltpu.SemaphoreType.DMA((k,)), pltpu.SemaphoreType.DMA((k,))],
    )(outbox)


def rowwise(name, body, *, ntiles, tile_off=0, tiled, full, outs, accs=()):
    nt, nf, no = len(tiled), len(full), len(outs)

    def kern(*refs):
        i = pl.program_id(0)
        out_vals, incs = body(i + tile_off, refs[:nt], refs[nt:nt + nf])
        for r, v in zip(refs[nt + nf:nt + nf + no], out_vals, strict=True):
            r[...] = v.astype(r.dtype)
        acc_refs = refs[nt + nf + no:]

        @pl.when(i == 0)
        def _():
            for r in acc_refs:
                r[...] = jnp.zeros_like(r)

        for r, v in zip(acc_refs, incs, strict=True):
            r[...] += v

    res = pl.pallas_call(
        kern, name=name, grid=(ntiles,),
        in_specs=[s for _, s in tiled] + [_full(a) for a in full],
        out_specs=[s for _, _, s in outs] + [pl.BlockSpec(s, lambda i, n=len(s): (0,) * n) for s in accs],
        out_shape=[SDS(s, d) for s, d, _ in outs] + [SDS(s, F32) for s in accs],
        compiler_params=_cparams(("arbitrary",) if accs else ("parallel",)),
    )(*[a for a, _ in tiled], *full)
    return res[:no], res[no:]


def mm_tn(name, a, b, rows):
    ka, nb = a.shape[1], b.shape[1]
    tr = _pick(rows, [t for t in (2048, 1280, 1024, 768, 512, 256) if t * ka * a.dtype.itemsize <= MM_TN_LHS_BYTES])
    tn = _pick(nb, (1024, 512, 256, 128))

    def kern(a_ref, b_ref, o_ref):
        @pl.when(pl.program_id(1) == 0)
        def _():
            o_ref[...] = jnp.zeros_like(o_ref)

        o_ref[...] += _dot_tn(a_ref[...], b_ref[...])

    return pl.pallas_call(
        kern, name=name, grid=(nb // tn, rows // tr),
        in_specs=[pl.BlockSpec((tr, ka), lambda n, r: (r, 0)), pl.BlockSpec((tr, tn), lambda n, r: (r, n))],
        out_specs=pl.BlockSpec((ka, tn), lambda n, r: (0, n)), out_shape=SDS((ka, nb), F32),
        compiler_params=_cparams(("parallel", "arbitrary")),
    )(a, b)


def _swap16(t):
    w = t.shape[1]
    lane = lax.broadcasted_iota(jnp.int32, t.shape, 1)
    return jnp.where((lane & 16) == 0, pltpu.roll(t, w - 16, 1), pltpu.roll(t, 16, 1))


def _rope(t, cos, sin):
    return t * cos + _swap16(t) * sin


def _rope_t(d, cos, sin):
    return d * cos - _swap16(d) * sin


def _seg_sum64(a, bd_ref):
    bd = bd_ref[...]
    hi = a.astype(BF16)
    lo = (a - hi.astype(F32)).astype(BF16)
    return jnp.dot(hi, bd, preferred_element_type=F32) + jnp.dot(lo, bd, preferred_element_type=F32)


def _lane_block(shape):
    return jnp.right_shift(lax.broadcasted_iota(jnp.int32, shape, 1), 6)


def _stack_groups(t, dtype):
    blk = _lane_block(t.shape)
    return jnp.concatenate([jnp.where(blk == g, t, jnp.zeros_like(t)).astype(dtype) for g in range(GROUPS)], axis=0)


def _fold_groups(ts, tq):
    blk = _lane_block((tq, KV_W))
    out = jnp.zeros((tq, KV_W), ts.dtype)
    for g in range(GROUPS):
        out = jnp.where(blk == g, ts[g * tq:(g + 1) * tq], out)
    return out


def _row_to_col(row):
    hi = row.astype(BF16)
    r1 = row - hi.astype(F32)
    mid = r1.astype(BF16)
    lo = (r1 - mid.astype(F32)).astype(BF16)
    ones = jnp.ones((8, LANES), BF16)
    pad = jnp.zeros((7, row.shape[1]), BF16)
    acc = jnp.zeros((row.shape[1], LANES), F32)
    for term in (hi, mid, lo):
        acc = acc + lax.dot_general(jnp.concatenate([term, pad], axis=0), ones, (((0,), (0,)), ((), ())),
                                    preferred_element_type=F32)
    return acc[:, 0:1]


def _stack_tiles(t, dtype):
    return jnp.concatenate([_stack_groups(t[a:a + TM], dtype) for a in range(0, t.shape[0], TM)], axis=0)


def _fold_tiles(ts, tq):
    return jnp.concatenate([_fold_groups(ts[GROUPS * a:GROUPS * (a + TM)], TM) for a in range(0, tq, TM)], axis=0)


def _compact_tiles_t(tt, dtype):
    return jnp.concatenate([tt[g * HEAD_DIM:(g + 1) * HEAD_DIM, a:a + TM] for a in range(0, tt.shape[1], TM)
                            for g in range(GROUPS)], axis=1).astype(dtype)


def _layer_norm_bwd(dxh, xhat, rstd):
    m1 = jnp.mean(dxh, axis=1, keepdims=True)
    m2 = jnp.mean(dxh * xhat, axis=1, keepdims=True)
    return rstd * (dxh - m1 - xhat * m2)


def _colsum(a):
    return jnp.sum(a, axis=0, keepdims=True)


def _shifted_rows(t, prev_row, next_row):
    n = t.shape[0]
    row = lax.broadcasted_iota(jnp.int32, t.shape, 0)
    up = jnp.where(row == 0, prev_row, pltpu.roll(t, 1, 0))
    dn = jnp.where(row == n - 1, next_row, pltpu.roll(t, n - 1, 0))
    return up, dn


def mod_fwd(cs, w_sh, b_sh):
    def kern(c_ref, w_ref, b_ref, o_ref):
        o_ref[...] = _dot(jax.nn.silu(c_ref[...]), w_ref[...]) + b_ref[...]

    return pl.pallas_call(kern, name="mod_fwd", out_shape=SDS((16, w_sh.shape[1]), F32),
                          compiler_params=pltpu.CompilerParams(vmem_limit_bytes=VMEM_LIMIT))(cs, w_sh, b_sh)


def mod_bwd(cs, w_sh, dm_sh, dm_all):
    hp = lax.Precision.HIGHEST

    def kern(c_ref, w_ref, dm_ref, da_ref, dw_ref, dc_ref, db_ref):
        c = c_ref[...]
        sg = jax.nn.sigmoid(c)
        sc = c * sg
        dm = dm_ref[...]
        dmc = dm_ref[8:9, :]
        for i in range(9, 16):
            dmc = dmc + dm_ref[i:i + 1, :]
        row = lax.broadcasted_iota(jnp.int32, dm.shape, 0)
        a = jnp.where(row < 8, dm, jnp.where(row == 8, dmc, 0.0))
        dw_ref[...] = lax.dot_general(sc, a, (((0,), (0,)), ((), ())), precision=hp, preferred_element_type=F32)
        dsc = lax.dot_general(a, w_ref[...], (((1,), (1,)), ((), ())), precision=hp, preferred_element_type=F32)
        dc_ref[...] = dsc * (sg * (1.0 + c * (1.0 - sg)))
        db = da_ref[0:1, :]
        for i in range(1, 16):
            db = db + da_ref[i:i + 1, :]
        db_ref[...] = db

    return pl.pallas_call(
        kern, name="mod_bwd",
        out_shape=[SDS(w_sh.shape, F32), SDS((16, D), F32), SDS((1, dm_all.shape[1]), F32)],
        compiler_params=pltpu.CompilerParams(vmem_limit_bytes=VMEM_LIMIT))(cs, w_sh, dm_sh, dm_all)


M_SHIFT1, M_SCALE1, M_SHIFTC, M_SCALEC, M_GATE1, M_SHIFT2, M_SCALE2, M_GATE2 = range(8)


def _mrow(ref, k):
    return ref[k:k + 1, :]


def inproj_fwd(xa, cos, sin, modrows, w_ext, b_ext, qg, kg, bd, n_lat_tiles):
    n = xa.shape[0]

    def body(t, vals, fr):
        x, cs, sn = (v[...] for v in vals)
        mod, w, b, qg_r, kg_r, bd_r = fr
        is_ctx = t >= n_lat_tiles
        shift = jnp.where(is_ctx, _mrow(mod, M_SHIFTC), _mrow(mod, M_SHIFT1))
        scale = jnp.where(is_ctx, _mrow(mod, M_SCALEC), _mrow(mod, M_SCALE1))
        hb = (x * (1.0 + scale) + shift).astype(MXU_DTYPE)
        proj = jnp.dot(hb, w[...], preferred_element_type=F32) + b[...]
        cos4 = jnp.concatenate([cs] * 4, axis=1)
        sin4 = jnp.concatenate([sn] * 4, axis=1)
        qa = _rope(proj[:, X_QA:X_QA + Q_W], cos4, sin4) * Q_SCALE
        ka = _rope(proj[:, X_KA:X_KA + Q_W], cos4, sin4)
        va = proj[:, X_VA:X_VA + Q_W]
        tq = proj[:, X_QB:X_QB + Q_W]
        rq = lax.rsqrt(_seg_sum64(tq * tq, bd_r) * (1.0 / HEAD_DIM) + QK_EPS)
        qb = _rope(tq * rq * qg_r[...], cos4, sin4) * Q_SCALE
        tk = proj[:, X_KB:X_KB + Q_W]
        rk = lax.rsqrt(_seg_sum64(tk * tk, bd_r) * (1.0 / HEAD_DIM) + QK_EPS)
        kb = _rope(tk * rk * kg_r[...], cos4, sin4)
        vb = proj[:, X_VB:X_VB + Q_W]
        gl = proj[:, X_GL:]
        return [hb, qa, ka, va, qb, kb, vb, tq, rq, tk, rk, gl], []

    mx = MXU_DTYPE
    outs = [((n, D), mx, _rows(TM, D))] + [((n, Q_W), mx, _rows(TM, Q_W))] * 6 + \
           [((n, Q_W), F32, _rows(TM, Q_W))] * 4 + [((n, 2 * D), F32, _rows(TM, 2 * D))]
    res, _ = rowwise("inproj_fwd", body, ntiles=n // TM,
                     tiled=[(xa, _rows(TM, D)), (cos, _rows(TM, LANES)), (sin, _rows(TM, LANES))],
                     full=[modrows, w_ext, b_ext, qg, kg, bd], outs=outs)
    return res


def merge_fwd(oa, ob, gl, x, modrows, wba, wbb, w_out, s_rows):
    def body(t, vals, fr):
        oa_, ob_, gl_, x_ = (v[...] for v in vals)
        mod, wa, wb, wo = fr
        ya = _dot(oa_, wa[...])
        yb = _dot(ob_, wb[...])
        ga = jax.nn.sigmoid(gl_[:, :D])
        gb = jax.nn.sigmoid(gl_[:, D:])
        mrg = ga * ya + gb * yb
        y = _dot(mrg, wo[...])
        r1 = ALPHA * x_ + _mrow(mod, M_GATE1) * y
        mu = jnp.mean(r1, axis=1, keepdims=True)
        xc = r1 - mu
        var = jnp.mean(xc * xc, axis=1, keepdims=True)
        rstd = lax.rsqrt(var + LN_EPS)
        xhat = xc * rstd
        return [ya, yb, mrg, y, xhat, rstd], []

    outs = [((s_rows, D), F32, _rows(TM, D))] * 2 + [((s_rows, D), MXU_DTYPE, _rows(TM, D))] + \
           [((s_rows, D), F32, _rows(TM, D))] * 2 + [((s_rows, 1), F32, _rows(TM, 1))]
    res, _ = rowwise("merge_fwd", body, ntiles=s_rows // TM,
                     tiled=[(oa, _rows(TM, Q_W)), (ob, _rows(TM, Q_W)), (gl, _rows(TM, 2 * D)), (x, _rows(TM, D))],
                     full=[modrows, wba, wbb, w_out], outs=outs)
    return res


def ffn_up_fwd(xhat1, modrows, ln_g, ln_b, w_up, s_rows):
    def body(t, vals, fr):
        xh = vals[0][...]
        mod, g_r, b_r, w = fr
        x1 = xh * g_r[...] + b_r[...]
        h2 = (x1 * (1.0 + _mrow(mod, M_SCALE2)) + _mrow(mod, M_SHIFT2)).astype(MXU_DTYPE)
        return [h2, jnp.dot(h2, w[...], preferred_element_type=F32)], []

    res, _ = rowwise("ffn_up_fwd", body, ntiles=s_rows // TM, tiled=[(xhat1, _rows(TM, D))],
                     full=[modrows, ln_g, ln_b, w_up],
                     outs=[((s_rows, D), MXU_DTYPE, _rows(TM, D)), ((s_rows, 2 * FF), F32, _rows(TM, 2 * FF))])
    return res


TC = 128


def _halo_specs(tm, w, s_rows):
    per = tm // 8
    last = s_rows // 8 - 1
    return (pl.BlockSpec((8, w), lambda i: (jnp.maximum(i * per - 1, 0), 0)),
            pl.BlockSpec((8, w), lambda i: (jnp.minimum((i + 1) * per, last), 0)))


def _halo_rows(t, ntiles, prev_ref, next_ref):
    prev_row = jnp.where(t == 0, 0.0, prev_ref[7:8, :].astype(F32))
    next_row = jnp.where(t == ntiles - 1, 0.0, next_ref[0:1, :].astype(F32))
    return prev_row, next_row


def conv_swiglu_fwd(u0, conv_w8, conv_b, s_rows):
    w2 = 2 * FF
    nt = s_rows // TC

    def body(t, vals, fr):
        u_ref, pv, nx = vals
        cw, cb = fr
        u = u_ref[...]
        up, dn = _shifted_rows(u, *_halo_rows(t, nt, pv, nx))
        uc = cw[0:1, :] * up + cw[1:2, :] * u + cw[2:3, :] * dn + cb[...]
        gate, val = uc[:, :FF], uc[:, FF:]
        return [gate * jax.nn.sigmoid(gate) * val], []

    hp, hn = _halo_specs(TC, w2, s_rows)
    res, _ = rowwise("conv_swiglu_fwd", body, ntiles=nt,
                     tiled=[(u0, _rows(TC, w2)), (u0, hp), (u0, hn)], full=[conv_w8, conv_b],
                     outs=[((s_rows, FF), MXU_DTYPE, _rows(TC, FF))])
    return res[0]


def ffn_down_loss(a, xhat1, target, modrows, ln1_g, ln1_b, ln2_g, ln2_b, w_down, s_rows):
    def body(t, vals, fr):
        a_, xh1, tgt = (v[...] for v in vals)
        mod, g1, b1, g2, b2, wd = fr
        y2 = jnp.dot(a_, wd[...], preferred_element_type=F32)
        x1 = xh1 * g1[...] + b1[...]
        gate2 = _mrow(mod, M_GATE2)
        r2 = ALPHA * x1 + gate2 * y2
        mu = jnp.mean(r2, axis=1, keepdims=True)
        xc = r2 - mu
        var = jnp.mean(xc * xc, axis=1, keepdims=True)
        rstd = lax.rsqrt(var + LN_EPS)
        xhat = xc * rstd
        out = xhat * g2[...] + b2[...]
        diff = out - tgt
        loss = 0.5 * jnp.sum(jnp.mean(diff * diff, axis=1, keepdims=True), axis=0, keepdims=True)
        dout = diff * (1.0 / D)
        dr2 = _layer_norm_bwd(dout * g2[...], xhat, rstd)
        incs = [loss, _colsum(dout * xhat), _colsum(dout), _colsum(dr2 * y2)]
        return [dr2, dr2 * gate2], incs

    res, accs = rowwise("ffn_down_loss", body, ntiles=s_rows // TM,
                        tiled=[(a, _rows(TM, FF)), (xhat1, _rows(TM, D)), (target, _rows(TM, D))],
                        full=[modrows, ln1_g, ln1_b, ln2_g, ln2_b, w_down],
                        outs=[((s_rows, D), F32, _rows(TM, D)), ((s_rows, D), MXU_DTYPE, _rows(TM, D))],
                        accs=[(1, 1), (1, D), (1, D), (1, D)])
    return res, accs


def ffn_down_bwd(dy2, w_down_t, s_rows):
    def body(t, vals, fr):
        return [jnp.dot(vals[0][...], fr[0][...], preferred_element_type=F32)], []

    res, _ = rowwise("ffn_down_bwd", body, ntiles=s_rows // TM, tiled=[(dy2, _rows(TM, D))], full=[w_down_t],
                     outs=[((s_rows, FF), F32, _rows(TM, FF))])
    return res[0]


def swiglu_conv_bwd(u0, da, conv_w8, conv_b, s_rows):
    w2 = 2 * FF
    nt = s_rows // TC
    n = TC + 16

    def body(t, vals, fr):
        u_ref, upv, unx, da_ref, apv, anx = vals
        cw, cb = fr
        first, last = t == 0, t == nt - 1
        ue = jnp.concatenate([jnp.where(first, 0.0, upv[...]), u_ref[...], jnp.where(last, 0.0, unx[...])], axis=0)
        ae = jnp.concatenate([jnp.where(first, 0.0, apv[...]), da_ref[...], jnp.where(last, 0.0, anx[...])], axis=0)
        up = pltpu.roll(ue, 1, 0)
        dn = pltpu.roll(ue, n - 1, 0)
        uc = cw[0:1, :] * up + cw[1:2, :] * ue + cw[2:3, :] * dn + cb[...]
        gate, val = uc[:, :FF], uc[:, FF:]
        sg = jax.nn.sigmoid(gate)
        du = jnp.concatenate([ae * val * (sg * (1.0 + gate * (1.0 - sg))), ae * (gate * sg)], axis=1)
        du0 = cw[0:1, :] * pltpu.roll(du, n - 1, 0) + cw[1:2, :] * du + cw[2:3, :] * pltpu.roll(du, 1, 0)
        rows = slice(8, 8 + TC)
        dut = du[rows]
        return [du0[rows]], [_colsum(dut), _colsum(up[rows] * dut), _colsum(ue[rows] * dut), _colsum(dn[rows] * dut)]

    hp, hn = _halo_specs(TC, w2, s_rows)
    ap, an = _halo_specs(TC, FF, s_rows)
    res, accs = rowwise("swiglu_conv_bwd", body, ntiles=nt,
                        tiled=[(u0, _rows(TC, w2)), (u0, hp), (u0, hn), (da, _rows(TC, FF)), (da, ap), (da, an)],
                        full=[conv_w8, conv_b], outs=[((s_rows, w2), MXU_DTYPE, _rows(TC, w2))], accs=[(1, w2)] * 4)
    return res[0], accs


def ffn_up_ln1_bwd(du0, dr2, xhat1, y, rstd1, modrows, ln_g, ln_b, w_up_t, s_rows):
    def body(t, vals, fr):
        du0_, dr2_, xh, y_, rstd = (v[...] for v in vals)
        mod, g_r, b_r, wt = fr
        dh2 = jnp.dot(du0_, wt[...], preferred_element_type=F32)
        x1 = xh * g_r[...] + b_r[...]
        dx1 = ALPHA * dr2_ + dh2 * (1.0 + _mrow(mod, M_SCALE2))
        dr1 = _layer_norm_bwd(dx1 * g_r[...], xh, rstd)
        incs = [_colsum(dh2 * x1), _colsum(dh2), _colsum(dx1 * xh), _colsum(dx1), _colsum(dr1 * y_)]
        return [dr1 * _mrow(mod, M_GATE1), ALPHA * dr1], incs

    res, accs = rowwise("ffn_up_ln1_bwd", body, ntiles=s_rows // TM,
                        tiled=[(du0, _rows(TM, 2 * FF)), (dr2, _rows(TM, D)), (xhat1, _rows(TM, D)), (y, _rows(TM, D)),
                               (rstd1, _rows(TM, 1))],
                        full=[modrows, ln_g, ln_b, w_up_t],
                        outs=[((s_rows, D), MXU_DTYPE, _rows(TM, D)), ((s_rows, D), F32, _rows(TM, D))],
                        accs=[(1, D)] * 5)
    return res, accs


def merge_bwd(dy, ya, yb, gl, w_out_t, wba_t, wbb_t, s_rows):
    def body(t, vals, fr):
        dy_, ya_, yb_, gl_ = (v[...] for v in vals)
        wot, wat, wbt = fr
        dmrg = jnp.dot(dy_, wot[...], preferred_element_type=F32)
        ga = jax.nn.sigmoid(gl_[:, :D])
        gb = jax.nn.sigmoid(gl_[:, D:])
        dya = dmrg * ga
        dyb = dmrg * gb
        dgl = jnp.concatenate([dmrg * ya_ * ga * (1.0 - ga), dmrg * yb_ * gb * (1.0 - gb)], axis=1)
        return [dya, dyb, dgl, _dot(dya, wat[...]), _dot(dyb, wbt[...])], []

    mx = MXU_DTYPE
    res, _ = rowwise("merge_bwd", body, ntiles=s_rows // TM,
                     tiled=[(dy, _rows(TM, D)), (ya, _rows(TM, D)), (yb, _rows(TM, D)), (gl, _rows(TM, 2 * D))],
                     full=[w_out_t, wba_t, wbb_t],
                     outs=[((s_rows, D), mx, _rows(TM, D))] * 2 + [((s_rows, 2 * D), F32, _rows(TM, 2 * D))] +
                          [((s_rows, Q_W), F32, _rows(TM, Q_W))] * 2)
    return res


def qk_bwd(dqa, dka_t, dva_t, dqb_heads, dkb_t, dvb_t, dgl, tq, rq, tk, rk, cos, sin, qg, kg, bd, place, n_lat_tiles, n):
    def placed(xt_ref, place_ref):
        xt = xt_ref[...]
        hi = xt.astype(BF16)
        r1 = xt - hi.astype(F32)
        mid = r1.astype(BF16)
        lo = (r1 - mid.astype(F32)).astype(BF16)
        pm = place_ref[...]
        return sum(lax.dot_general(term, pm, (((0,), (0,)), ((), ())), preferred_element_type=F32) for term in (hi, mid, lo))

    def body(t, vals, fr):
        dqa_, dgl_, tq_, rq_, tk_, rk_, cs, sn = (v[...] for v in vals[:8])
        qg_r, kg_r, bd_r, pl_r = fr
        dka_, dva_, dkb_, dvb_ = (placed(v, pl_r) for v in vals[8:12])
        dqb_ = jnp.concatenate([v[...] for v in vals[12:]], axis=1)
        is_ctx = t >= n_lat_tiles
        cos4 = jnp.concatenate([cs] * 4, axis=1)
        sin4 = jnp.concatenate([sn] * 4, axis=1)
        zero = jnp.zeros_like(dqa_)
        dpqa = jnp.where(is_ctx, zero, _rope_t(dqa_, cos4, sin4) * Q_SCALE)
        dpka = _rope_t(dka_, cos4, sin4)
        dpva = dva_
        dnq = jnp.where(is_ctx, zero, _rope_t(dqb_, cos4, sin4) * Q_SCALE)
        gq = qg_r[...] * dnq
        dtq = rq_ * gq - tq_ * (rq_ * rq_ * rq_) * (_seg_sum64(gq * tq_, bd_r) * (1.0 / HEAD_DIM))
        dnk = _rope_t(dkb_, cos4, sin4)
        gk = kg_r[...] * dnk
        dtk = rk_ * gk - tk_ * (rk_ * rk_ * rk_) * (_seg_sum64(gk * tk_, bd_r) * (1.0 / HEAD_DIM))
        dgl32 = jnp.where(is_ctx, jnp.zeros_like(dgl_), dgl_)
        dproj = jnp.concatenate([dpqa, dpka, dpva, dtq, dtk, dvb_, dgl32], axis=1)
        return [dproj], [_colsum(dproj), _colsum(dnq * tq_ * rq_), _colsum(dnk * tk_ * rk_)]

    lat = lambda t: jnp.minimum(t, n_lat_tiles - 1)
    qs = _rows(TM, Q_W)
    ts = pl.BlockSpec((N_KV * HEAD_DIM, TM), lambda i: (0, i))
    res, accs = rowwise(
        "qk_bwd", body, ntiles=n // TM,
        tiled=[(dqa, _rows(TM, Q_W, lat)), (dgl, _rows(TM, 2 * D, lat)),
               (tq, qs), (rq, qs), (tk, qs), (rk, qs), (cos, _rows(TM, LANES)), (sin, _rows(TM, LANES)),
               (dka_t, ts), (dva_t, ts), (dkb_t, ts), (dvb_t, ts)] + [(d, _rows(TM, KV_W, lat)) for d in dqb_heads],
        full=[qg, kg, bd, place], outs=[((n, EXT_COLS), MXU_DTYPE, _rows(TM, EXT_COLS))],
        accs=[(1, EXT_COLS), (1, Q_W), (1, Q_W)])
    return res[0], accs


def inproj_bwd(name, dproj, xa, dxp, modrows, w_ext_t, *, ntiles, tile_off, is_ctx, out_rows):
    kc = M_SCALEC if is_ctx else M_SCALE1

    def body(t, vals, fr):
        dp, x_ = vals[0][...], vals[1][...]
        mod, wt = fr
        dh = jnp.dot(dp, wt[...], preferred_element_type=F32)
        incs = [_colsum(dh * x_), _colsum(dh)]
        if is_ctx:
            return [], incs
        return [vals[2][...] + dh * (1.0 + _mrow(mod, kc))], incs

    tiled = [(dproj, _rows(TM, EXT_COLS, lambda i: i + tile_off)), (xa, _rows(TM, D, lambda i: i + tile_off))]
    outs = []
    if not is_ctx:
        tiled.append((dxp, _rows(TM, D)))
        outs = [((out_rows, D), F32, _rows(TM, D))]
    return rowwise(name, body, ntiles=ntiles, tiled=tiled, full=[modrows, w_ext_t], outs=outs, accs=[(1, D)] * 2)


def _attn_semantics():
    return _cparams(("arbitrary", "arbitrary", "arbitrary"))


GLOB_TK = (1280, 1024, 768, 512, 256)
GLOB_TQ = (512, 256)
GLOB_BWD_TQ = GLOB_TQ
KEY_CHUNK = 256


def glob_fwd(q, kt, v_t, s_rows):
    n = kt.shape[0]
    tq = _pick(s_rows, GLOB_TQ)
    tk = _pick(n, GLOB_TK)
    nq, nk = s_rows // tq, n // tk
    r = GROUPS * tq
    nch = tk // KEY_CHUNK

    def produce(qs, k_ref, s_buf, c, mx):
        rows = slice(c * KEY_CHUNK, (c + 1) * KEY_CHUNK)
        sn = _dot_nt(k_ref[rows, :], qs[...])
        s_buf[rows, :] = sn
        return jnp.maximum(mx, jnp.max(sn, axis=0, keepdims=True))

    def kern(q_ref, k0_ref, kn_ref, vt_ref, o_ref, lse_ref, qs, s_buf, mx_buf, m_s, l_s, acc):
        j = pl.program_id(2)

        @pl.when(j == 0)
        def _():
            qs[...] = _stack_tiles(q_ref[...], qs.dtype)
            mx = jnp.full((1, r), -jnp.inf, F32)
            for c in range(nch):
                mx = produce(qs, k0_ref, s_buf, c, mx)
            mx_buf[...] = mx
            m_s[...] = jnp.full_like(m_s, -jnp.inf)
            l_s[...] = jnp.zeros_like(l_s)
            acc[...] = jnp.zeros_like(acc)

        m_prev = m_s[...]
        m_new = jnp.maximum(m_prev, mx_buf[...])
        alpha = jnp.exp(m_prev - m_new)
        a = alpha * acc[...]
        ls = alpha * l_s[...]
        mx = jnp.full((1, r), -jnp.inf, F32)
        for c in range(nch):
            rows = slice(c * KEY_CHUNK, (c + 1) * KEY_CHUNK)
            p = jnp.exp(s_buf[rows, :] - m_new)
            ls = ls + jnp.sum(p, axis=0, keepdims=True)
            a = a + jnp.dot(vt_ref[0, :, rows], p.astype(MXU_DTYPE), preferred_element_type=F32)
            mx = produce(qs, kn_ref, s_buf, c, mx)
        mx_buf[...] = mx
        l_s[...] = ls
        acc[...] = a
        m_s[...] = m_new

        @pl.when(j == nk - 1)
        def _():
            o_t = acc[...] / l_s[...]
            o_ref[...] = jnp.concatenate([_untranspose_groups(o_t[:, GROUPS * a:GROUPS * (a + TM)], TM)
                                          for a in range(0, tq, TM)], axis=0)
            lse_ref[0, 0] = _row_to_col(m_s[...] + jnp.log(l_s[...]))

    kspec = lambda f: pl.BlockSpec((tk, KV_W), lambda h, i, j: (f(j), h))
    return pl.pallas_call(
        kern, name="glob_fwd", grid=(N_KV, nq, nk),
        in_specs=[pl.BlockSpec((tq, KV_W), lambda h, i, j: (i, h)), kspec(lambda j: 0),
                  kspec(lambda j: jnp.minimum(j + 1, nk - 1)), pl.BlockSpec((1, HEAD_DIM, tk), lambda h, i, j: (h, 0, j))],
        out_specs=[pl.BlockSpec((tq, KV_W), lambda h, i, j: (i, h)),
                   pl.BlockSpec((1, 1, r, 1), lambda h, i, j: (h, i, 0, 0))],
        out_shape=[SDS((s_rows, Q_W), F32), SDS((N_KV, nq, r, 1), F32)],
        scratch_shapes=[pltpu.VMEM((r, KV_W), MXU_DTYPE), pltpu.VMEM((tk, r), F32), pltpu.VMEM((1, r), F32),
                        pltpu.VMEM((1, r), F32), pltpu.VMEM((1, r), F32), pltpu.VMEM((HEAD_DIM, r), F32)],
        compiler_params=_attn_semantics(),
    )(q, kt, kt, v_t)


def attn_delta(o, do, s_rows):
    tq = _pick(s_rows, GLOB_BWD_TQ)
    nq = s_rows // tq
    r = GROUPS * tq

    def kern(o_ref, do_ref, d_ref):
        d_ref[0, 0] = jnp.sum(_stack_tiles(do_ref[...] * o_ref[...], F32), axis=1, keepdims=True)

    qspec = pl.BlockSpec((tq, KV_W), lambda h, i: (i, h))
    return pl.pallas_call(
        kern, name="attn_delta", grid=(N_KV, nq), in_specs=[qspec, qspec],
        out_specs=pl.BlockSpec((1, 1, r, 1), lambda h, i: (h, i, 0, 0)), out_shape=SDS((N_KV, nq, r, 1), F32),
        compiler_params=_cparams(("parallel", "parallel")),
    )(o, do)


def _compact_t(tt, dtype):
    return jnp.concatenate([tt[g * HEAD_DIM:(g + 1) * HEAD_DIM, :] for g in range(GROUPS)], axis=1).astype(dtype)


def glob_bwd(q, q_t, kt, vt, do, do_t, lse, delta, h, s_rows):
    n = kt.shape[0]
    tq = _pick(s_rows, GLOB_BWD_TQ)
    tk = _pick(n, GLOB_TK)
    nq, nk = s_rows // tq, n // tk
    r = GROUPS * tq
    nch = tk // KEY_CHUNK

    def kern(q_ref, qt_ref, k_ref, v_ref, do_ref, dot_ref, lse_ref, dl_ref, dq_ref, dkt_ref, dvt_ref, p_buf, ds_buf):
        j = pl.program_id(0)
        i = pl.program_id(1)

        @pl.when(i == 0)
        def _():
            dkt_ref[...] = jnp.zeros_like(dkt_ref)
            dvt_ref[...] = jnp.zeros_like(dvt_ref)

        qs = _stack_tiles(q_ref[...], MXU_DTYPE)
        dos = _stack_tiles(do_ref[...], MXU_DTYPE)
        lse_b = jnp.broadcast_to(lse_ref[0, 0], (r, LANES))
        dl_b = jnp.broadcast_to(dl_ref[0, 0], (r, LANES))
        for c in range(nch):
            lo = c * KEY_CHUNK
            sc = _dot_nt(qs, k_ref[lo:lo + KEY_CHUNK, :])
            dpc = _dot_nt(dos, v_ref[lo:lo + KEY_CHUNK, :])
            for t in range(KEY_CHUNK // LANES):
                sl = slice(t * LANES, (t + 1) * LANES)
                pt = jnp.exp(sc[:, sl] - lse_b)
                p_buf[:, lo + t * LANES:lo + (t + 1) * LANES] = pt.astype(p_buf.dtype)
                ds_buf[:, lo + t * LANES:lo + (t + 1) * LANES] = (pt * (dpc[:, sl] - dl_b)).astype(ds_buf.dtype)
        dq_t = _fold_tiles(jnp.dot(ds_buf[...], k_ref[...], preferred_element_type=F32), tq)
        rows = pl.ds(pl.multiple_of(i * tq, tq), tq)

        @pl.when(j == 0)
        def _():
            dq_ref[rows, :] = dq_t

        @pl.when(j > 0)
        def _():
            dq_ref[rows, :] += dq_t

        dvt_ref[...] += jnp.dot(_compact_tiles_t(dot_ref[...], MXU_DTYPE), p_buf[...], preferred_element_type=F32)
        dkt_ref[...] += jnp.dot(_compact_tiles_t(qt_ref[...], MXU_DTYPE), ds_buf[...], preferred_element_type=F32)

    col = pl.BlockSpec((1, 1, r, 1), lambda j, i: (h, i, 0, 0))
    qspec = pl.BlockSpec((tq, KV_W), lambda j, i: (i, h))
    tspec = pl.BlockSpec((KV_W, tq), lambda j, i: (h, i))
    kspec = pl.BlockSpec((tk, KV_W), lambda j, i: (j, h))
    ospec = pl.BlockSpec((HEAD_DIM, tk), lambda j, i: (0, j))
    return pl.pallas_call(
        kern, name=f"glob_bwd_h{h}", grid=(nk, nq),
        in_specs=[qspec, tspec, kspec, kspec, qspec, tspec, col, col],
        out_specs=[pl.BlockSpec(memory_space=pltpu.VMEM), ospec, ospec],
        out_shape=[SDS((s_rows, KV_W), F32), SDS((HEAD_DIM, n), F32), SDS((HEAD_DIM, n), F32)],
        scratch_shapes=[pltpu.VMEM((r, tk), MXU_DTYPE), pltpu.VMEM((r, tk), MXU_DTYPE)],
        compiler_params=_cparams(("arbitrary", "arbitrary")),
    )(q, q_t, kt, vt, do, do_t, lse, delta)


TW = 2 * WIN
WR = GROUPS * TW
WLAT = 4 * WIN


def _win_cat(dst, parts):
    off = 0
    for p in parts:
        dst[off:off + p.shape[0], :] = p[...]
        off += p.shape[0]


def _win_specs(s_rows, c_rows):
    nb = s_rows // WIN
    prev = lambda i: jnp.maximum(2 * i - 1, 0)
    nxt = lambda i: jnp.minimum(2 * i + 2, nb - 1)
    rows = [pl.BlockSpec((WIN, KV_W), lambda h, i: (prev(i), h)), pl.BlockSpec((TW, KV_W), lambda h, i: (i, h)),
            pl.BlockSpec((WIN, KV_W), lambda h, i: (nxt(i), h)), pl.BlockSpec((c_rows, KV_W), lambda h, i: (s_rows // c_rows, h))]
    cols = [pl.BlockSpec((1, HEAD_DIM, WIN), lambda h, i: (h, 0, prev(i))), pl.BlockSpec((1, HEAD_DIM, TW), lambda h, i: (h, 0, i)),
            pl.BlockSpec((1, HEAD_DIM, WIN), lambda h, i: (h, 0, nxt(i))),
            pl.BlockSpec((1, HEAD_DIM, c_rows), lambda h, i: (h, 0, s_rows // c_rows))]
    return rows, cols


def _win_mask(i, s_rows, shape, keys_on_rows):
    a = lax.broadcasted_iota(jnp.int32, shape, 0)
    b = lax.broadcasted_iota(jnp.int32, shape, 1)
    kk, qq = (a, b) if keys_on_rows else (b, a)
    qpos = i * TW + (qq & (TW - 1))
    kpos = (2 * i - 1) * WIN + kk
    band = (jnp.abs(qpos - kpos) <= WIN) & (kpos >= 0) & (kpos < s_rows)
    return (kk >= WLAT) | band


def _untranspose_groups(o_t, tq):
    row = lax.broadcasted_iota(jnp.int32, (HEAD_DIM, KV_W), 0)
    col = lax.broadcasted_iota(jnp.int32, (HEAD_DIM, KV_W), 1)
    hi = o_t.astype(BF16)
    r1 = o_t - hi.astype(F32)
    mid = r1.astype(BF16)
    lo = (r1 - mid.astype(F32)).astype(BF16)
    o = jnp.zeros((tq, KV_W), F32)
    for g in range(GROUPS):
        sel = jnp.where(col == row + g * HEAD_DIM, 1.0, 0.0).astype(BF16)
        for term in (hi, mid, lo):
            o = o + lax.dot_general(term[:, g * tq:(g + 1) * tq], sel, (((0,), (0,)), ((), ())), preferred_element_type=F32)
    return o


def win_fwd(q, kt, v_t, sinkrow, s_rows, c_rows):
    nt = s_rows // TW
    nkeys = WLAT + c_rows

    def kern(q_ref, kp, kc, kn, kx, vp, vc, vn, vx, sink_ref, o_ref, lse_ref, kcat):
        i = pl.program_id(1)
        _win_cat(kcat, (kp, kc, kn, kx))
        qs = _stack_groups(q_ref[...], MXU_DTYPE)
        st = _dot_nt(kcat[...], qs)
        st = jnp.where(_win_mask(i, s_rows, st.shape, True), st, -jnp.inf)
        sink = sink_ref[0]
        m = jnp.maximum(jnp.max(st, axis=0, keepdims=True), sink)
        e = jnp.exp(st - m)
        den = jnp.sum(e, axis=0, keepdims=True) + jnp.exp(sink - m)
        v_cat = jnp.concatenate([vp[0], vc[0], vn[0], vx[0]], axis=1)
        o_t = jnp.dot(v_cat, e.astype(MXU_DTYPE), preferred_element_type=F32) / den
        o_ref[...] = _untranspose_groups(o_t, TW)
        lse_ref[0, 0] = _row_to_col(m + jnp.log(den))

    rows, cols = _win_specs(s_rows, c_rows)
    qspec = pl.BlockSpec((TW, KV_W), lambda h, i: (i, h))
    rowv = pl.BlockSpec((1, 1, WR, 1), lambda h, i: (h, i, 0, 0))
    return pl.pallas_call(
        kern, name="win_fwd", grid=(N_KV, nt),
        in_specs=[qspec] + rows + cols + [pl.BlockSpec((1, 1, WR), lambda h, i: (h, 0, 0))],
        out_specs=[qspec, rowv], out_shape=[SDS((s_rows, Q_W), F32), SDS((N_KV, nt, WR, 1), F32)],
        scratch_shapes=[pltpu.VMEM((nkeys, KV_W), MXU_DTYPE)],
        compiler_params=_cparams(("parallel", "parallel")),
    )(q, kt, kt, kt, kt, v_t, v_t, v_t, v_t, sinkrow)


def win_bwd(q, q_t, kt, vt, sinkcol, o, do, do_t, lse, s_rows, c_rows):
    nt = s_rows // TW
    nkeys = WLAT + c_rows
    n = s_rows + c_rows
    ctx0 = WIN + s_rows

    def kern(q_ref, qt_ref, kp, kc, kn, kx, vp, vc, vn, vx, sink_ref, o_ref, do_ref, dot_ref, lse_ref,
             dq_ref, dkt_ref, dvt_ref, dsk_ref, kcat, vcat):
        i = pl.program_id(1)

        @pl.when(i == 0)
        def _():
            dkt_ref[...] = jnp.zeros_like(dkt_ref)
            dvt_ref[...] = jnp.zeros_like(dvt_ref)
            dsk_ref[...] = jnp.zeros_like(dsk_ref)

        _win_cat(kcat, (kp, kc, kn, kx))
        _win_cat(vcat, (vp, vc, vn, vx))
        qs = _stack_groups(q_ref[...], MXU_DTYPE)
        do32 = _stack_groups(do_ref[...], F32)
        delta = jnp.sum(do32 * _stack_groups(o_ref[...], F32), axis=1, keepdims=True)
        dos = do32.astype(MXU_DTYPE)
        lse_c = lse_ref[0, 0]
        s = _dot_nt(qs, kcat[...])
        s = jnp.where(_win_mask(i, s_rows, s.shape, False), s, -jnp.inf)
        p = jnp.exp(s - lse_c)
        ds = p * (_dot_nt(dos, vcat[...]) - delta)
        dq_ref[...] = _fold_groups(_dot(ds, kcat[...]), TW)
        dvt = jnp.dot(_compact_t(dot_ref[...], MXU_DTYPE), p.astype(MXU_DTYPE), preferred_element_type=F32)
        dkt = jnp.dot(_compact_t(qt_ref[...], MXU_DTYPE), ds.astype(MXU_DTYPE), preferred_element_type=F32)
        lat = pl.ds(pl.multiple_of(i * TW, TW), WLAT)
        dkt_ref[0, :, lat] += dkt[:, :WLAT]
        dvt_ref[0, :, lat] += dvt[:, :WLAT]
        dkt_ref[0, :, ctx0:ctx0 + c_rows] += dkt[:, WLAT:]
        dvt_ref[0, :, ctx0:ctx0 + c_rows] += dvt[:, WLAT:]
        dsk_ref[0] += -(jnp.exp(sink_ref[0][:, 0:1] - lse_c) * delta)

    rows, _ = _win_specs(s_rows, c_rows)
    qspec = pl.BlockSpec((TW, KV_W), lambda h, i: (i, h))
    tspec = pl.BlockSpec((KV_W, TW), lambda h, i: (h, i))
    col = pl.BlockSpec((1, 1, WR, 1), lambda h, i: (h, i, 0, 0))
    kvt = pl.BlockSpec((1, HEAD_DIM, WIN + n), lambda h, i: (h, 0, 0))
    return pl.pallas_call(
        kern, name="win_bwd", grid=(N_KV, nt),
        in_specs=[qspec, tspec] + rows + rows + [pl.BlockSpec((1, WR, LANES), lambda h, i: (h, 0, 0)), qspec, qspec, tspec, col],
        out_specs=[qspec, kvt, kvt, pl.BlockSpec((1, WR, 1), lambda h, i: (h, 0, 0))],
        out_shape=[SDS((s_rows, Q_W), F32), SDS((N_KV, HEAD_DIM, WIN + n), F32), SDS((N_KV, HEAD_DIM, WIN + n), F32),
                   SDS((N_KV, WR, 1), F32)],
        scratch_shapes=[pltpu.VMEM((nkeys, KV_W), MXU_DTYPE), pltpu.VMEM((nkeys, KV_W), MXU_DTYPE)],
        compiler_params=_cparams(("arbitrary", "arbitrary")),
    )(q, q_t, kt, kt, kt, kt, vt, vt, vt, vt, sinkcol, o, do, do_t, lse)


def adamw(name, w, m, v, grads):
    r, wd = w.shape
    tr = _pick(r, [t for t in ELEMENTWISE_ROWS if t * wd * 4 <= ELEMENTWISE_BLOCK_BYTES])
    stacked = not isinstance(grads, (list, tuple))
    ng = grads.shape[0] if stacked else len(grads)

    def kern(*refs):
        w_ref, m_ref, v_ref = refs[:3]
        g_refs = refs[3:-4]
        g_out, d_out, m_out, v_out = refs[-4:]
        if stacked:
            g = g_refs[0][0]
            for k in range(1, ng):
                g = g + g_refs[0][k]
        else:
            g = g_refs[0][...]
            for gr in g_refs[1:]:
                g = g + gr[...]
        wv = w_ref[...]
        mn = ADAM_B1 * m_ref[...] + (1.0 - ADAM_B1) * g
        vn = ADAM_B2 * v_ref[...] + (1.0 - ADAM_B2) * (g * g)
        m_hat = mn / (1.0 - ADAM_B1 ** ADAM_STEP)
        v_hat = vn / (1.0 - ADAM_B2 ** ADAM_STEP)
        g_out[...] = g
        d_out[...] = -ADAM_LR * (m_hat / (jnp.sqrt(v_hat) + ADAM_EPS) + ADAM_WD * wv)
        m_out[...] = mn
        v_out[...] = vn

    spec = pl.BlockSpec((tr, wd), lambda i: (i, 0))
    gspecs = [pl.BlockSpec((ng, tr, wd), lambda i: (0, i, 0))] if stacked else [spec] * ng
    return pl.pallas_call(
        kern, name=name, grid=(r // tr,), in_specs=[spec] * 3 + gspecs, out_specs=[spec] * 4,
        out_shape=[SDS((r, wd), F32)] * 4, compiler_params=_cparams(("parallel",)),
    )(w, m, v, *([grads] if stacked else grads))


def add2(name, a, b):
    k, r, w = a.shape
    tr = _pick(r, [t for t in ELEMENTWISE_ROWS if t * w * 4 <= ELEMENTWISE_BLOCK_BYTES])

    def kern(a_ref, b_ref, o_ref):
        o_ref[...] = a_ref[...] + b_ref[...]

    spec = pl.BlockSpec((1, tr, w), lambda s, i: (s, i, 0))
    return pl.pallas_call(kern, name=name, grid=(k, r // tr), in_specs=[spec, spec], out_specs=spec,
                          out_shape=SDS(a.shape, a.dtype), compiler_params=_cparams(("parallel", "parallel")))(a, b)


def _rep4(a, off):
    return jnp.concatenate([a[:, off + HEAD_DIM * h: off + HEAD_DIM * (h + 1)] for h in range(N_KV) for _ in range(GROUPS)], axis=1)


def _extend_cols(a):
    return jnp.concatenate([a[:, 0:OFF_KA], _rep4(a, OFF_KA), _rep4(a, OFF_VA), a[:, OFF_QB:OFF_KB],
                            _rep4(a, OFF_KB), _rep4(a, OFF_VB), a[:, OFF_GA:]], axis=1)


def _fold4(a, off):
    r = a.shape[0]
    return a[:, off:off + Q_W].reshape(r, N_KV, GROUPS, HEAD_DIM).sum(axis=2).reshape(r, N_KV * HEAD_DIM)


def _fold_cols(a):
    return jnp.concatenate([a[:, X_QA:X_QA + Q_W], _fold4(a, X_KA), _fold4(a, X_VA), a[:, X_QB:X_QB + Q_W],
                            _fold4(a, X_KB), _fold4(a, X_VB), a[:, X_GL:]], axis=1)


def _rope_tables(s_rows, c_rows):
    n_rows = s_rows // GRID_W
    n_freq = HEAD_DIM // 4
    inv_freq = ROPE_THETA ** (-jnp.arange(n_freq, dtype=F32) / n_freq)
    ang_r = jnp.arange(n_rows, dtype=jnp.int32).astype(F32)[:, None] * inv_freq
    ang_c = jnp.arange(GRID_W, dtype=jnp.int32).astype(F32)[:, None] * inv_freq
    by_row = lambda t: jnp.repeat(t, GRID_W, axis=0)
    by_col = lambda t: jnp.tile(t, (n_rows, 1))
    cos = jnp.concatenate([by_row(jnp.cos(ang_r))] * 2 + [by_col(jnp.cos(ang_c))] * 2, axis=1)
    sin_r, sin_c = by_row(jnp.sin(ang_r)), by_col(jnp.sin(ang_c))
    sin = jnp.concatenate([-sin_r, sin_r, -sin_c, sin_c], axis=1)
    cos = jnp.concatenate([cos, jnp.ones((c_rows, HEAD_DIM), F32)], axis=0)
    sin = jnp.concatenate([sin, jnp.zeros((c_rows, HEAD_DIM), F32)], axis=0)
    return jnp.concatenate([cos, cos], axis=1), jnp.concatenate([sin, sin], axis=1)


def _ff_pad_cols(a):
    r = a.shape[0]
    a = jnp.pad(a.reshape(r, N_DEV, FF_SHARD), ((0, 0), (0, 0), (0, FF_SHARD_PAD - FF_SHARD)))
    return a.reshape(r, 2 * FF)


def _ff_unpad_cols(a):
    r = a.shape[0]
    return a.reshape(r, N_DEV, FF_SHARD_PAD)[:, :, :FF_SHARD].reshape(r, 2 * D_FF)


def _ff_pad_rows(a):
    c = a.shape[1]
    a = jnp.pad(a.reshape(N_DEV // 2, FF_SHARD, c), ((0, 0), (0, FF_SHARD_PAD - FF_SHARD), (0, 0)))
    return a.reshape(FF, c)


def _ff_unpad_rows(a):
    c = a.shape[1]
    return a.reshape(N_DEV // 2, FF_SHARD_PAD, c)[:, :FF_SHARD].reshape(D_FF, c)


BIG = (("w_in", (D, IN_COLS // N_DEV)), ("w_branch_a", (Q_W, D // N_DEV)), ("w_branch_b", (Q_W, D // N_DEV)),
       ("w_out", (D // N_DEV, D)), ("w_up", (D, FF_SHARD_PAD)), ("w_down", (D_FF // N_DEV, D)))
BIG_SIZES = tuple(int(np.prod(s)) for _, s in BIG)
BIG_ROWS = sum(BIG_SIZES) // LANES


def _pack_big(parts):
    return jnp.concatenate([p.reshape(-1) for p in parts]).reshape(BIG_ROWS, LANES)


def _unpack_big(flat):
    lead = flat.shape[:-2]
    f = flat.reshape(*lead, BIG_ROWS * LANES)
    out, off = [], 0
    for (_, shp), sz in zip(BIG, BIG_SIZES, strict=True):
        out.append(f[..., off:off + sz].reshape(*lead, *shp))
        off += sz
    return out


def _cols_to_full(g):
    return jnp.transpose(g, (1, 0, 2)).reshape(g.shape[1], -1)


def _full_to_cols(a):
    r, c = a.shape
    return jnp.transpose(a.reshape(r, N_DEV, c // N_DEV), (1, 0, 2))


SMALL = (("c_ctx", D), ("b_mod", N_MOD * D), ("b_in", IN_COLS), ("attn_sink", N_HEADS), ("q_norm_g", HEAD_DIM),
         ("k_norm_g", HEAD_DIM), ("ln1_g", D), ("ln1_b", D), ("conv_w", 3 * 2 * D_FF // N_DEV), ("conv_b", 2 * D_FF),
         ("ln2_g", D), ("ln2_b", D))
SMALL_TOTAL = sum(n for _, n in SMALL)
SMALL_ROWS = -(-SMALL_TOTAL // (8 * LANES)) * 8


def _pack_small(parts):
    flat = jnp.concatenate([p.reshape(-1).astype(F32) for p in parts])
    return jnp.pad(flat, (0, SMALL_ROWS * LANES - flat.shape[0])).reshape(SMALL_ROWS, LANES)


def _unpack_small(packed):
    f = packed.reshape(-1)
    out, off = {}, 0
    for name, n in SMALL:
        out[name] = f[off:off + n]
        off += n
    return out


RED = (("c_ctx", D), ("b_in", IN_COLS), ("attn_sink", N_HEADS), ("q_norm_g", HEAD_DIM), ("k_norm_g", HEAD_DIM),
       ("ln1_g", D), ("ln1_b", D), ("conv_w", 3 * 2 * FF), ("conv_b", 2 * FF), ("ln2_g", D), ("ln2_b", D))
RED_TOTAL = sum(n for _, n in RED)
RED_ROWS = -(-RED_TOTAL // (8 * LANES)) * 8


def sum8(name, g):
    _, r, w = g.shape

    def kern(g_ref, o_ref):
        acc = g_ref[0]
        for k in range(1, N_DEV):
            acc = acc + g_ref[k]
        o_ref[...] = acc

    return pl.pallas_call(kern, name=name, out_shape=SDS((r, w), F32))(g)


def _local_step(x, ctx, target, modrows, weights, small):
    s_rows, c_rows = x.shape[0], ctx.shape[0]
    n = s_rows + c_rows
    nl = s_rows // TM
    w_in, wba, wbb, w_out, w_up, w_down = weights
    f = lambda a: a.reshape(1, -1).astype(F32)
    b_in, ln1_g, ln1_b, ln2_g, ln2_b, conv_b = (f(small[k]) for k in ("b_in", "ln1_g", "ln1_b", "ln2_g", "ln2_b", "conv_b"))
    conv_w8 = jnp.pad(small["conv_w_full"], ((0, 5), (0, 0)))
    qg = jnp.tile(small["q_norm_g"].reshape(1, HEAD_DIM), (1, N_HEADS))
    kg = jnp.tile(small["k_norm_g"].reshape(1, HEAD_DIM), (1, N_HEADS))
    sink_rep = jnp.repeat(small["attn_sink"].reshape(N_KV, GROUPS), TW, axis=1)
    sinkrow = sink_rep.reshape(N_KV, 1, WR)
    sinkcol = jnp.broadcast_to(sink_rep[:, :, None], (N_KV, WR, LANES))
    bd = jnp.kron(jnp.eye(N_HEADS, dtype=F32), jnp.ones((HEAD_DIM, HEAD_DIM), F32)).astype(BF16)
    cos, sin = _rope_tables(s_rows, c_rows)
    w_ext = _extend_cols(w_in)
    b_ext = _extend_cols(b_in)
    xa = jnp.concatenate([x, ctx], axis=0)

    hb, qa, kat, vat, qb, kbt, vbt, tq, rq, tk, rk, gl = inproj_fwd(xa, cos, sin, modrows, w_ext, b_ext, qg, kg, bd, nl)
    compact_t = lambda t: jnp.stack([t[:, h * KV_W:h * KV_W + HEAD_DIM].T for h in range(N_KV)])
    oa, lse_a = win_fwd(qa, kat, compact_t(vat), sinkrow, s_rows, c_rows)
    vb_t = compact_t(vbt)
    ob, lse_b = glob_fwd(qb, kbt, vb_t, s_rows)
    tqb = _pick(s_rows, GLOB_BWD_TQ)
    lse_b = lse_b.reshape(N_KV, s_rows // tqb, GROUPS * tqb, 1)
    ya, yb, mrg, y, xhat1, rstd1 = merge_fwd(oa, ob, gl, x, modrows, wba, wbb, w_out, s_rows)
    h2, u0 = ffn_up_fwd(xhat1, modrows, ln1_g, ln1_b, w_up, s_rows)
    a = conv_swiglu_fwd(u0, conv_w8, conv_b, s_rows)
    (dr2, dy2), (loss, dln2_g, dln2_b, dgate2) = ffn_down_loss(a, xhat1, target, modrows, ln1_g, ln1_b, ln2_g, ln2_b, w_down, s_rows)

    da = ffn_down_bwd(dy2, w_down.T, s_rows)
    dw_down = mm_tn("dw_down", a, dy2, s_rows)
    du0, (dconv_b, dcw0, dcw1, dcw2) = swiglu_conv_bwd(u0, da, conv_w8, conv_b, s_rows)
    dw_up = mm_tn("dw_up", h2, du0, s_rows)
    (dy, dxp), (dscale2, dshift2, dln1_g, dln1_b, dgate1) = ffn_up_ln1_bwd(du0, dr2, xhat1, y, rstd1, modrows, ln1_g, ln1_b, w_up.T, s_rows)
    dya, dyb, dgl, doa, dob = merge_bwd(dy, ya, yb, gl, w_out.T, wba.T, wbb.T, s_rows)
    dw_out = mm_tn("dw_out", mrg, dy, s_rows)
    dwba = mm_tn("dw_branch_a", oa, dya, s_rows)
    dwbb = mm_tn("dw_branch_b", ob, dyb, s_rows)

    head_rows = lambda t: t.reshape(N_KV * HEAD_DIM, t.shape[-1])
    place = np.zeros((N_KV * HEAD_DIM, Q_W), np.float32)
    for h in range(N_KV):
        place[h * HEAD_DIM + np.arange(HEAD_DIM), h * KV_W + np.arange(HEAD_DIM)] = 1.0
    place = jnp.asarray(place, BF16)
    dqa, dka_t, dva_t, dsk = win_bwd(qa, qa[:s_rows].T, kat, vat, sinkcol, oa, doa, doa.astype(MXU_DTYPE).T, lse_a,
                                     s_rows, c_rows)
    dka_t, dva_t = head_rows(dka_t[:, :, WIN:]), head_rows(dva_t[:, :, WIN:])
    delta_b = attn_delta(ob, dob, s_rows)
    qb_t = qb[:s_rows].T
    dob_t = dob.astype(MXU_DTYPE).T
    heads = [glob_bwd(qb, qb_t, kbt, vbt, dob, dob_t, lse_b, delta_b, h, s_rows) for h in range(N_KV)]
    dqb = [hd[0] for hd in heads]
    dkb_t = jnp.concatenate([hd[1] for hd in heads], axis=0)
    dvb_t = jnp.concatenate([hd[2] for hd in heads], axis=0)
    dproj, (db_ext, dqg, dkg) = qk_bwd(dqa, dka_t, dva_t, dqb, dkb_t, dvb_t, dgl, tq, rq, tk, rk, cos, sin, qg, kg, bd,
                                       place, nl, n)
    w_ext_t = w_ext.T
    (grad_x,), (dscale1, dshift1) = inproj_bwd("inproj_bwd", dproj, xa, dxp, modrows, w_ext_t, ntiles=nl, tile_off=0,
                                               is_ctx=False, out_rows=s_rows)
    _, (dscale_c, dshift_c) = inproj_bwd("inproj_bwd_ctx", dproj, xa, None, modrows, w_ext_t, ntiles=c_rows // TM,
                                         tile_off=nl, is_ctx=True, out_rows=0)
    dw_in = _fold_cols(mm_tn("dw_in", hb, dproj, n))

    dmod = jnp.concatenate([dshift1, dscale1, dgate1, dshift2, dscale2, dgate2], axis=1)
    dmod_c = jnp.concatenate([dshift_c, dscale_c, jnp.zeros((1, (N_MOD - 2) * D), F32)], axis=1)
    fold_g = lambda t: t.reshape(N_HEADS, HEAD_DIM).sum(axis=0)
    red = {
        "b_in": _fold_cols(db_ext), "attn_sink": dsk.reshape(N_HEADS, TW).sum(axis=1), "q_norm_g": fold_g(dqg),
        "k_norm_g": fold_g(dkg), "ln1_g": dln1_g, "ln1_b": dln1_b, "conv_w": jnp.concatenate([dcw0, dcw1, dcw2], axis=0),
        "conv_b": dconv_b, "ln2_g": dln2_g, "ln2_b": dln2_b,
    }
    return loss[0, 0], grad_x, (dw_in, dwba, dwbb, dw_out, dw_up, dw_down), dmod, dmod_c, red


def kernel(x, c, ctx, c_ctx, w_mod, b_mod, w_in, b_in, attn_sink, q_norm_g, k_norm_g, w_branch_a, w_branch_b, w_out, ln1_g, ln1_b, w_up, conv_w, conv_b, w_down, ln2_g, ln2_b, loss_target, m_c_ctx, m_w_mod, m_b_mod, m_w_in, m_b_in, m_attn_sink, m_q_norm_g, m_k_norm_g, m_w_branch_a, m_w_branch_b, m_w_out, m_ln1_g, m_ln1_b, m_w_up, m_conv_w, m_conv_b, m_w_down, m_ln2_g, m_ln2_b, v_c_ctx, v_w_mod, v_b_mod, v_w_in, v_b_in, v_attn_sink, v_q_norm_g, v_k_norm_g, v_w_branch_a, v_w_branch_b, v_w_out, v_ln1_g, v_ln1_b, v_w_up, v_conv_w, v_conv_b, v_w_down, v_ln2_g, v_ln2_b):
    ax, ay, ac = (lax.axis_index(a) for a in AXES)
    me = 4 * ax + 2 * ay + ac
    chip = 2 * ax + ay
    mod_w = N_MOD * D // N_DEV
    params = dict(c_ctx=c_ctx, w_mod=w_mod, b_mod=b_mod, w_in=w_in, b_in=b_in, attn_sink=attn_sink, q_norm_g=q_norm_g,
                  k_norm_g=k_norm_g, w_branch_a=w_branch_a, w_branch_b=w_branch_b, w_out=w_out, ln1_g=ln1_g, ln1_b=ln1_b,
                  w_up=w_up, conv_w=conv_w, conv_b=conv_b, w_down=w_down, ln2_g=ln2_g, ln2_b=ln2_b)
    mom_m = dict(c_ctx=m_c_ctx, w_mod=m_w_mod, b_mod=m_b_mod, w_in=m_w_in, b_in=m_b_in, attn_sink=m_attn_sink,
                 q_norm_g=m_q_norm_g, k_norm_g=m_k_norm_g, w_branch_a=m_w_branch_a, w_branch_b=m_w_branch_b, w_out=m_w_out,
                 ln1_g=m_ln1_g, ln1_b=m_ln1_b, w_up=m_w_up, conv_w=m_conv_w, conv_b=m_conv_b, w_down=m_w_down,
                 ln2_g=m_ln2_g, ln2_b=m_ln2_b)
    mom_v = dict(c_ctx=v_c_ctx, w_mod=v_w_mod, b_mod=v_b_mod, w_in=v_w_in, b_in=v_b_in, attn_sink=v_attn_sink,
                 q_norm_g=v_q_norm_g, k_norm_g=v_k_norm_g, w_branch_a=v_w_branch_a, w_branch_b=v_w_branch_b, w_out=v_w_out,
                 ln1_g=v_ln1_g, ln1_b=v_ln1_b, w_up=v_w_up, conv_w=v_conv_w, conv_b=v_conv_b, w_down=v_w_down,
                 ln2_g=v_ln2_g, ln2_b=v_ln2_b)
    big_names = [nm for nm, _ in BIG]

    def shard(tree, nm):
        t = tree[nm][0]
        return jnp.pad(t, ((0, 0), (0, FF_SHARD_PAD - FF_SHARD))) if nm == "w_up" else t

    wg = all_gather("ag_weights", _pack_big([shard(params, nm).astype(MXU_DTYPE) for nm in big_names]))
    g_in, g_ba, g_bb, g_out, g_up, g_down = _unpack_big(wg)
    weights = (_cols_to_full(g_in), _cols_to_full(g_ba), _cols_to_full(g_bb), g_out.reshape(D, D), _cols_to_full(g_up),
               _ff_pad_rows(g_down.reshape(D_FF, D)))

    c_all = all_gather("ag_c", c.reshape(8, LANES)).reshape(N_DEV, D)
    cs = jnp.concatenate([c_all, c_ctx.reshape(1, D), jnp.zeros((7, D), F32)], axis=0)
    w_mod_sh = w_mod[0]
    b_mod_sh = lax.dynamic_slice(b_mod, (0, me * mod_w), (1, mod_w))
    mod_part = mod_fwd(cs, w_mod_sh, b_mod_sh)
    mg = all_gather("ag_mod", mod_part.reshape(16 * mod_w // LANES, LANES)).reshape(N_DEV, 16, mod_w)
    mod = lax.dynamic_index_in_dim(mg, me, axis=1, keepdims=False).reshape(N_MOD, D)
    mod_c = mg[:, 8, :].reshape(N_MOD, D)
    modrows = jnp.stack([mod[0], mod[1], mod_c[0], mod_c[1], mod[2], mod[3], mod[4], mod[5]], axis=0)

    conv_w_full = all_gather("ag_conv_w", jnp.pad(conv_w[0], ((0, 5), (0, FF_SHARD_PAD - FF_SHARD))))
    conv_w_full = _cols_to_full(conv_w_full[:, :3, :])
    small = dict(b_in=b_in, ln1_g=ln1_g, ln1_b=ln1_b, ln2_g=ln2_g, ln2_b=ln2_b, conv_b=_ff_pad_cols(conv_b),
                 conv_w_full=conv_w_full, q_norm_g=q_norm_g, k_norm_g=k_norm_g, attn_sink=attn_sink)
    loss, grad_x, big_grads, dmod, dmod_c, red = _local_step(x[0], ctx[0], loss_target[0], modrows, weights, small)
    loss = lax.psum(loss, AXES)

    dm = all_gather("ag_dmod", jnp.concatenate([dmod, dmod_c], axis=0).reshape(2 * N_MOD * D // LANES, LANES))
    dm = dm.reshape(N_DEV, 2, N_MOD * D)
    dm_all = jnp.concatenate([dm[:, 0], dm[:, 1]], axis=0)
    dm_sh = lax.dynamic_slice(dm_all, (0, me * mod_w), (16, mod_w))
    dw_mod, dcc, db_mod = mod_bwd(cs, w_mod_sh, dm_sh, dm_all)
    red["c_ctx"] = dcc[8]

    red_vec = jnp.concatenate([red[nm].reshape(-1) for nm, _ in RED])
    red_vec = jnp.pad(red_vec, (0, RED_ROWS * LANES - RED_TOTAL)).reshape(RED_ROWS, LANES)
    red_sum = sum8("sum_small", all_gather("ag_small", red_vec)).reshape(-1)
    gsm, off = {}, 0
    for nm, k in RED:
        gsm[nm] = red_sum[off:off + k]
        off += k
    gsm["b_mod"] = db_mod.reshape(-1)
    gsm["conv_b"] = _ff_unpad_cols(gsm["conv_b"].reshape(1, 2 * FF))
    gsm["conv_w"] = lax.dynamic_slice(gsm["conv_w"].reshape(3, 2 * FF), (0, me * FF_SHARD_PAD), (3, FF_SHARD_PAD))[:, :FF_SHARD]
    sm_names = [nm for nm, _ in SMALL]
    gs, ds, ms, vs = adamw("adamw_small", _pack_small([params[nm] for nm in sm_names]),
                           _pack_small([mom_m[nm] for nm in sm_names]), _pack_small([mom_v[nm] for nm in sm_names]),
                           [_pack_small([gsm[nm] for nm in sm_names])])
    sm_out = [_unpack_small(t) for t in (gs, ds, ms, vs)]

    dw_in, dwba, dwbb, dw_out, dw_up, dw_down = big_grads
    slabs = jnp.concatenate([t.reshape(N_DEV, -1) for t in (
        _full_to_cols(dw_in), _full_to_cols(dwba), _full_to_cols(dwbb), dw_out, _full_to_cols(dw_up),
        _ff_unpad_rows(dw_down))], axis=1)
    by_core = slabs.reshape(4, 2, BIG_ROWS, LANES)
    keep = lax.dynamic_index_in_dim(by_core, ac, axis=1, keepdims=False)
    give = lax.dynamic_index_in_dim(by_core, 1 - ac, axis=1, keepdims=False)
    got = exchange("rs_sibling", give.reshape(1, 4 * BIG_ROWS, LANES), to_chips=False).reshape(4, BIG_ROWS, LANES)
    pair = add2("rs_pair_sum", keep, got)
    outbox = jnp.stack([lax.dynamic_index_in_dim(pair, jnp.bitwise_xor(chip, m), axis=0, keepdims=False) for m in (1, 2, 3)])
    inbox = exchange("rs_chips", outbox.astype(MXU_DTYPE), to_chips=True)
    mine = lax.dynamic_index_in_dim(pair, chip, axis=0, keepdims=False)
    gb, db, mb, vb = adamw("adamw_big", _pack_big([shard(params, nm) for nm in big_names]),
                           _pack_big([shard(mom_m, nm) for nm in big_names]), _pack_big([shard(mom_v, nm) for nm in big_names]),
                           [mine, inbox[0], inbox[1], inbox[2]])
    big_out = [dict(zip(big_names, _unpack_big(t), strict=True)) for t in (gb, db, mb, vb)]
    for out in big_out:
        out["w_up"] = out["w_up"][:, :FF_SHARD]
    gm, dmo, mmo, vmo = adamw("adamw_mod", w_mod[0], m_w_mod[0], v_w_mod[0], [dw_mod])
    mod_out = (gm, dmo, mmo, vmo)

    order = ["c_ctx", "w_mod", "b_mod", "w_in", "b_in", "attn_sink", "q_norm_g", "k_norm_g", "w_branch_a", "w_branch_b",
             "w_out", "ln1_g", "ln1_b", "w_up", "conv_w", "conv_b", "w_down", "ln2_g", "ln2_b"]
    results = [loss, grad_x[None]]
    for kind in range(4):
        for nm in order:
            if nm == "w_mod":
                val = mod_out[kind]
            elif nm in big_out[kind]:
                val = big_out[kind][nm]
            else:
                val = sm_out[kind][nm]
            results.append(val.reshape(params[nm].shape))
    return tuple(results)
```

```python
import functools

import jax
import jax.numpy as jnp
import numpy as np
from jax import lax
from jax.experimental import pallas as pl
from jax.experimental.pallas import tpu as pltpu

F32 = jnp.float32
BF16 = jnp.bfloat16
MXU_DTYPE = BF16

AXES = ("x", "y", "c")
N_DEV = 8
D = 1024
HEAD_DIM = 64
N_HEADS = 8
N_KV = 2
GROUPS = 4
KV_W = GROUPS * HEAD_DIM
Q_W = N_HEADS * HEAD_DIM
GRID_W = 64
WIN = 128
ROPE_THETA = 10000.0
D_FF = 2816
FF_SHARD = 2 * D_FF // N_DEV
FF_SHARD_PAD = 768
FF = N_DEV // 2 * FF_SHARD_PAD
LN_EPS = 1e-5
QK_EPS = 1e-6
N_MOD = 6
ALPHA = 2.0 ** 0.25
Q_SCALE = HEAD_DIM ** -0.5
IN_COLS = 3584
OFF_KA, OFF_VA, OFF_QB, OFF_KB, OFF_VB, OFF_GA = 512, 640, 768, 1280, 1408, 1536
EXT_COLS = 6 * Q_W + 2 * D
X_QA, X_KA, X_VA, X_QB, X_KB, X_VB, X_GL = 0, 512, 1024, 1536, 2048, 2560, 3072
ADAM_LR, ADAM_B1, ADAM_B2, ADAM_EPS, ADAM_WD, ADAM_STEP = 0.001, 0.9, 0.999, 1e-08, 0.01, 10
LANES = 128
TM = 256
VMEM_LIMIT = 56 * 1024 * 1024
ELEMENTWISE_BLOCK_BYTES = 1 << 20
MM_TN_LHS_BYTES = 8 << 20
ELEMENTWISE_ROWS = (1824, 1408, 1024, 512, 256, 128, 64, 32, 16, 8)

ANY = pl.BlockSpec(memory_space=pl.ANY)
SDS = jax.ShapeDtypeStruct


def _pick(n, candidates):
    for t in candidates:
        if n % t == 0:
            return t
    raise ValueError(f"no tile for {n}")


def _full(a):
    nd = a.ndim
    return pl.BlockSpec(a.shape, lambda *_: (0,) * nd)


def _rows(tm, w, fn=lambda t: t):
    return pl.BlockSpec((tm, w), lambda i: (fn(i), 0))


def _dot(a, b):
    return jnp.dot(a.astype(MXU_DTYPE), b.astype(MXU_DTYPE), preferred_element_type=F32)


def _dot_nt(a, b):
    return lax.dot_general(a.astype(MXU_DTYPE), b.astype(MXU_DTYPE), (((1,), (1,)), ((), ())), preferred_element_type=F32)


def _dot_tn(a, b):
    return lax.dot_general(a.astype(MXU_DTYPE), b.astype(MXU_DTYPE), (((0,), (0,)), ((), ())), preferred_element_type=F32)


def _cparams(sem):
    return pltpu.CompilerParams(dimension_semantics=sem, vmem_limit_bytes=VMEM_LIMIT)


def all_gather(name, v):
    r, w = v.shape

    def body(x_ref, out_ref, send_sems, recv_sems, local_sem):
        x, y, c = (lax.axis_index(a) for a in AXES)
        me, sibling = (x, y, c), (x, y, 1 - c)
        chips = [(1 - x, y), (x, 1 - y), (1 - x, 1 - y)]

        def rows(px, py, pc):
            return out_ref.at[4 * px + 2 * py + pc]

        def copy(k, block, to, src=None):
            return pltpu.make_async_remote_copy(
                src_ref=rows(*block) if src is None else src, dst_ref=rows(*block),
                send_sem=send_sems.at[k], recv_sem=recv_sems.at[k],
                device_id=to, device_id_type=pl.DeviceIdType.MESH)

        mine = pltpu.make_async_copy(x_ref, rows(*me), local_sem)
        mine.start()
        first = [copy(0, me, sibling, src=x_ref)]
        first += [copy(1 + j, me, (*chip, c), src=x_ref) for j, chip in enumerate(chips)]
        for cp in first:
            cp.start()
        passed = [copy(4 + j, (*chip, c), sibling) for j, chip in enumerate(chips)]
        for j, chip in enumerate(chips):
            copy(1 + j, (*chip, c), me).wait_recv()
            passed[j].start()
        copy(0, sibling, me).wait_recv()
        for j, chip in enumerate(chips):
            copy(4 + j, (*chip, 1 - c), me).wait_recv()
        for cp in first + passed:
            cp.wait_send()
        mine.wait()

    return pl.pallas_call(
        body, name=name, out_shape=SDS((N_DEV, r, w), v.dtype), in_specs=[ANY], out_specs=ANY,
        scratch_shapes=[pltpu.SemaphoreType.DMA((7,)), pltpu.SemaphoreType.DMA((7,)), pltpu.SemaphoreType.DMA],
    )(v)


def exchange(name, outbox, to_chips):
    k = outbox.shape[0]
    assert k == (3 if to_chips else 1)

    def body(out_ref, in_ref, send_sems, recv_sems):
        x, y, c = (lax.axis_index(a) for a in AXES)
        peers = [(x, 1 - y, c), (1 - x, y, c), (1 - x, 1 - y, c)] if to_chips else [(x, y, 1 - c)]
        copies = [
            pltpu.make_async_remote_copy(
                src_ref=out_ref.at[m], dst_ref=in_ref.at[m], send_sem=send_sems.at[m], recv_sem=recv_sems.at[m],
                device_id=peer, device_id_type=pl.DeviceIdType.MESH)
            for m, peer in enumerate(peers)
        ]
        for cp in copies:
            cp.start()
        for cp in copies:
            cp.wait_recv()
        for cp in copies:
            cp.wait_send()

    return pl.pallas_call(
        body, name=name, out_shape=SDS(outbox.shape, outbox.dtype), in_specs=[ANY], out_specs=ANY,
        scratch_shapes=[pltpu.SemaphoreType.DMA((k,)), pltpu.SemaphoreType.DMA((k,))],
    )(outbox)


def rowwise(name, body, *, ntiles, tile_off=0, tiled, full, outs, accs=()):
    nt, nf, no = len(tiled), len(full), len(outs)

    def kern(*refs):
        i = pl.program_id(0)
        out_vals, incs = body(i + tile_off, refs[:nt], refs[nt:nt + nf])
        for r, v in zip(refs[nt + nf:nt + nf + no], out_vals, strict=True):
            r[...] = v.astype(r.dtype)
        acc_refs = refs[nt + nf + no:]

        @pl.when(i == 0)
        def _():
            for r in acc_refs:
                r[...] = jnp.zeros_like(r)

        for r, v in zip(acc_refs, incs, strict=True):
            r[...] += v

    res = pl.pallas_call(
        kern, name=name, grid=(ntiles,),
        in_specs=[s for _, s in tiled] + [_full(a) for a in full],
        out_specs=[s for _, _, s in outs] + [pl.BlockSpec(s, lambda i, n=len(s): (0,) * n) for s in accs],
        out_shape=[SDS(s, d) for s, d, _ in outs] + [SDS(s, F32) for s in accs],
        compiler_params=_cparams(("arbitrary",) if accs else ("parallel",)),
    )(*[a for a, _ in tiled], *full)
    return res[:no], res[no:]


def mm_tn(name, a, b, rows):
    ka, nb = a.shape[1], b.shape[1]
    tr = _pick(rows, [t for t in (2048, 1280, 1024, 768, 512, 256) if t * ka * a.dtype.itemsize <= MM_TN_LHS_BYTES])
    tn = _pick(nb, (1024, 512, 256, 128))

    def kern(a_ref, b_ref, o_ref):
        @pl.when(pl.program_id(1) == 0)
        def _():
            o_ref[...] = jnp.zeros_like(o_ref)

        o_ref[...] += _dot_tn(a_ref[...], b_ref[...])

    return pl.pallas_call(
        kern, name=name, grid=(nb // tn, rows // tr),
        in_specs=[pl.BlockSpec((tr, ka), lambda n, r: (r, 0)), pl.BlockSpec((tr, tn), lambda n, r: (r, n))],
        out_specs=pl.BlockSpec((ka, tn), lambda n, r: (0, n)), out_shape=SDS((ka, nb), F32),
        compiler_params=_cparams(("parallel", "arbitrary")),
    )(a, b)


def _swap16(t):
    w = t.shape[1]
    lane = lax.broadcasted_iota(jnp.int32, t.shape, 1)
    return jnp.where((lane & 16) == 0, pltpu.roll(t, w - 16, 1), pltpu.roll(t, 16, 1))


def _rope(t, cos, sin):
    return t * cos + _swap16(t) * sin


def _rope_t(d, cos, sin):
    return d * cos - _swap16(d) * sin


def _seg_sum64(a, bd_ref):
    bd = bd_ref[...]
    hi = a.astype(BF16)
    lo = (a - hi.astype(F32)).astype(BF16)
    return jnp.dot(hi, bd, preferred_element_type=F32) + jnp.dot(lo, bd, preferred_element_type=F32)


def _lane_block(shape):
    return jnp.right_shift(lax.broadcasted_iota(jnp.int32, shape, 1), 6)


def _stack_groups(t, dtype):
    blk = _lane_block(t.shape)
    return jnp.concatenate([jnp.where(blk == g, t, jnp.zeros_like(t)).astype(dtype) for g in range(GROUPS)], axis=0)


def _fold_groups(ts, tq):
    blk = _lane_block((tq, KV_W))
    out = jnp.zeros((tq, KV_W), ts.dtype)
    for g in range(GROUPS):
        out = jnp.where(blk == g, ts[g * tq:(g + 1) * tq], out)
    return out


def _row_to_col(row):
    hi = row.astype(BF16)
    r1 = row - hi.astype(F32)
    mid = r1.astype(BF16)
    lo = (r1 - mid.astype(F32)).astype(BF16)
    ones = jnp.ones((8, LANES), BF16)
    pad = jnp.zeros((7, row.shape[1]), BF16)
    acc = jnp.zeros((row.shape[1], LANES), F32)
    for term in (hi, mid, lo):
        acc = acc + lax.dot_general(jnp.concatenate([term, pad], axis=0), ones, (((0,), (0,)), ((), ())),
                                    preferred_element_type=F32)
    return acc[:, 0:1]


def _stack_tiles(t, dtype):
    return jnp.concatenate([_stack_groups(t[a:a + TM], dtype) for a in range(0, t.shape[0], TM)], axis=0)


def _fold_tiles(ts, tq):
    return jnp.concatenate([_fold_groups(ts[GROUPS * a:GROUPS * (a + TM)], TM) for a in range(0, tq, TM)], axis=0)


def _compact_tiles_t(tt, dtype):
    return jnp.concatenate([tt[g * HEAD_DIM:(g + 1) * HEAD_DIM, a:a + TM] for a in range(0, tt.shape[1], TM)
                            for g in range(GROUPS)], axis=1).astype(dtype)


def _layer_norm_bwd(dxh, xhat, rstd):
    m1 = jnp.mean(dxh, axis=1, keepdims=True)
    m2 = jnp.mean(dxh * xhat, axis=1, keepdims=True)
    return rstd * (dxh - m1 - xhat * m2)


def _colsum(a):
    return jnp.sum(a, axis=0, keepdims=True)


def _shifted_rows(t, prev_row, next_row):
    n = t.shape[0]
    row = lax.broadcasted_iota(jnp.int32, t.shape, 0)
    up = jnp.where(row == 0, prev_row, pltpu.roll(t, 1, 0))
    dn = jnp.where(row == n - 1, next_row, pltpu.roll(t, n - 1, 0))
    return up, dn


def mod_fwd(cs, w_sh, b_sh):
    def kern(c_ref, w_ref, b_ref, o_ref):
        o_ref[...] = _dot(jax.nn.silu(c_ref[...]), w_ref[...]) + b_ref[...]

    return pl.pallas_call(kern, name="mod_fwd", out_shape=SDS((16, w_sh.shape[1]), F32),
                          compiler_params=pltpu.CompilerParams(vmem_limit_bytes=VMEM_LIMIT))(cs, w_sh, b_sh)


def mod_bwd(cs, w_sh, dm_sh, dm_all):
    hp = lax.Precision.HIGHEST

    def kern(c_ref, w_ref, dm_ref, da_ref, dw_ref, dc_ref, db_ref):
        c = c_ref[...]
        sg = jax.nn.sigmoid(c)
        sc = c * sg
        dm = dm_ref[...]
        dmc = dm_ref[8:9, :]
        for i in range(9, 16):
            dmc = dmc + dm_ref[i:i + 1, :]
        row = lax.broadcasted_iota(jnp.int32, dm.shape, 0)
        a = jnp.where(row < 8, dm, jnp.where(row == 8, dmc, 0.0))
        dw_ref[...] = lax.dot_general(sc, a, (((0,), (0,)), ((), ())), precision=hp, preferred_element_type=F32)
        dsc = lax.dot_general(a, w_ref[...], (((1,), (1,)), ((), ())), precision=hp, preferred_element_type=F32)
        dc_ref[...] = dsc * (sg * (1.0 + c * (1.0 - sg)))
        db = da_ref[0:1, :]
        for i in range(1, 16):
            db = db + da_ref[i:i + 1, :]
        db_ref[...] = db

    return pl.pallas_call(
        kern, name="mod_bwd",
        out_shape=[SDS(w_sh.shape, F32), SDS((16, D), F32), SDS((1, dm_all.shape[1]), F32)],
        compiler_params=pltpu.CompilerParams(vmem_limit_bytes=VMEM_LIMIT))(cs, w_sh, dm_sh, dm_all)


M_SHIFT1, M_SCALE1, M_SHIFTC, M_SCALEC, M_GATE1, M_SHIFT2, M_SCALE2, M_GATE2 = range(8)


def _mrow(ref, k):
    return ref[k:k + 1, :]


def inproj_fwd(xa, cos, sin, modrows, w_ext, b_ext, qg, kg, bd, n_lat_tiles):
    n = xa.shape[0]

    def body(t, vals, fr):
        x, cs, sn = (v[...] for v in vals)
        mod, w, b, qg_r, kg_r, bd_r = fr
        is_ctx = t >= n_lat_tiles
        shift = jnp.where(is_ctx, _mrow(mod, M_SHIFTC), _mrow(mod, M_SHIFT1))
        scale = jnp.where(is_ctx, _mrow(mod, M_SCALEC), _mrow(mod, M_SCALE1))
        hb = (x * (1.0 + scale) + shift).astype(MXU_DTYPE)
        proj = jnp.dot(hb, w[...], preferred_element_type=F32) + b[...]
        cos4 = jnp.concatenate([cs] * 4, axis=1)
        sin4 = jnp.concatenate([sn] * 4, axis=1)
        qa = _rope(proj[:, X_QA:X_QA + Q_W], cos4, sin4) * Q_SCALE
        ka = _rope(proj[:, X_KA:X_KA + Q_W], cos4, sin4)
        va = proj[:, X_VA:X_VA + Q_W]
        tq = proj[:, X_QB:X_QB + Q_W]
        rq = lax.rsqrt(_seg_sum64(tq * tq, bd_r) * (1.0 / HEAD_DIM) + QK_EPS)
        qb = _rope(tq * rq * qg_r[...], cos4, sin4) * Q_SCALE
        tk = proj[:, X_KB:X_KB + Q_W]
        rk = lax.rsqrt(_seg_sum64(tk * tk, bd_r) * (1.0 / HEAD_DIM) + QK_EPS)
        kb = _rope(tk * rk * kg_r[...], cos4, sin4)
        vb = proj[:, X_VB:X_VB + Q_W]
        gl = proj[:, X_GL:]
        return [hb, qa, ka, va, qb, kb, vb, tq, rq, tk, rk, gl], []

    mx = MXU_DTYPE
    outs = [((n, D), mx, _rows(TM, D))] + [((n, Q_W), mx, _rows(TM, Q_W))] * 6 + \
           [((n, Q_W), F32, _rows(TM, Q_W))] * 4 + [((n, 2 * D), F32, _rows(TM, 2 * D))]
    res, _ = rowwise("inproj_fwd", body, ntiles=n // TM,
                     tiled=[(xa, _rows(TM, D)), (cos, _rows(TM, LANES)), (sin, _rows(TM, LANES))],
                     full=[modrows, w_ext, b_ext, qg, kg, bd], outs=outs)
    return res


def merge_fwd(oa, ob, gl, x, modrows, wba, wbb, w_out, s_rows):
    def body(t, vals, fr):
        oa_, ob_, gl_, x_ = (v[...] for v in vals)
        mod, wa, wb, wo = fr
        ya = _dot(oa_, wa[...])
        yb = _dot(ob_, wb[...])
        ga = jax.nn.sigmoid(gl_[:, :D])
        gb = jax.nn.sigmoid(gl_[:, D:])
        mrg = ga * ya + gb * yb
        y = _dot(mrg, wo[...])
        r1 = ALPHA * x_ + _mrow(mod, M_GATE1) * y
        mu = jnp.mean(r1, axis=1, keepdims=True)
        xc = r1 - mu
        var = jnp.mean(xc * xc, axis=1, keepdims=True)
        rstd = lax.rsqrt(var + LN_EPS)
        xhat = xc * rstd
        return [ya, yb, mrg, y, xhat, rstd], []

    outs = [((s_rows, D), F32, _rows(TM, D))] * 2 + [((s_rows, D), MXU_DTYPE, _rows(TM, D))] + \
           [((s_rows, D), F32, _rows(TM, D))] * 2 + [((s_rows, 1), F32, _rows(TM, 1))]
    res, _ = rowwise("merge_fwd", body, ntiles=s_rows // TM,
                     tiled=[(oa, _rows(TM, Q_W)), (ob, _rows(TM, Q_W)), (gl, _rows(TM, 2 * D)), (x, _rows(TM, D))],
                     full=[modrows, wba, wbb, w_out], outs=outs)
    return res


def ffn_up_fwd(xhat1, modrows, ln_g, ln_b, w_up, s_rows):
    def body(t, vals, fr):
        xh = vals[0][...]
        mod, g_r, b_r, w = fr
        x1 = xh * g_r[...] + b_r[...]
        h2 = (x1 * (1.0 + _mrow(mod, M_SCALE2)) + _mrow(mod, M_SHIFT2)).astype(MXU_DTYPE)
        return [h2, jnp.dot(h2, w[...], preferred_element_type=F32)], []

    res, _ = rowwise("ffn_up_fwd", body, ntiles=s_rows // TM, tiled=[(xhat1, _rows(TM, D))],
                     full=[modrows, ln_g, ln_b, w_up],
                     outs=[((s_rows, D), MXU_DTYPE, _rows(TM, D)), ((s_rows, 2 * FF), F32, _rows(TM, 2 * FF))])
    return res


TC = 128


def _halo_specs(tm, w, s_rows):
    per = tm // 8
    last = s_rows // 8 - 1
    return (pl.BlockSpec((8, w), lambda i: (jnp.maximum(i * per - 1, 0), 0)),
            pl.BlockSpec((8, w), lambda i: (jnp.minimum((i + 1) * per, last), 0)))


def _halo_rows(t, ntiles, prev_ref, next_ref):
    prev_row = jnp.where(t == 0, 0.0, prev_ref[7:8, :].astype(F32))
    next_row = jnp.where(t == ntiles - 1, 0.0, next_ref[0:1, :].astype(F32))
    return prev_row, next_row


def conv_swiglu_fwd(u0, conv_w8, conv_b, s_rows):
    w2 = 2 * FF
    tc = _pick(s_rows, (2 * TC, TC))
    nt = s_rows // tc

    def body(t, vals, fr):
        u_ref, pv, nx = vals
        cw, cb = fr
        u = u_ref[...]
        up, dn = _shifted_rows(u, *_halo_rows(t, nt, pv, nx))
        uc = cw[0:1, :] * up + cw[1:2, :] * u + cw[2:3, :] * dn + cb[...]
        gate, val = uc[:, :FF], uc[:, FF:]
        return [gate * jax.nn.sigmoid(gate) * val], []

    hp, hn = _halo_specs(tc, w2, s_rows)
    res, _ = rowwise("conv_swiglu_fwd", body, ntiles=nt,
                     tiled=[(u0, _rows(tc, w2)), (u0, hp), (u0, hn)], full=[conv_w8, conv_b],
                     outs=[((s_rows, FF), MXU_DTYPE, _rows(tc, FF))])
    return res[0]


def ffn_down_loss(a, xhat1, target, modrows, ln1_g, ln1_b, ln2_g, ln2_b, w_down, s_rows):
    def body(t, vals, fr):
        a_, xh1, tgt = (v[...] for v in vals)
        mod, g1, b1, g2, b2, wd = fr
        y2 = jnp.dot(a_, wd[...], preferred_element_type=F32)
        x1 = xh1 * g1[...] + b1[...]
        gate2 = _mrow(mod, M_GATE2)
        r2 = ALPHA * x1 + gate2 * y2
        mu = jnp.mean(r2, axis=1, keepdims=True)
        xc = r2 - mu
        var = jnp.mean(xc * xc, axis=1, keepdims=True)
        rstd = lax.rsqrt(var + LN_EPS)
        xhat = xc * rstd
        out = xhat * g2[...] + b2[...]
        diff = out - tgt
        loss = 0.5 * jnp.sum(jnp.mean(diff * diff, axis=1, keepdims=True), axis=0, keepdims=True)
        dout = diff * (1.0 / D)
        dr2 = _layer_norm_bwd(dout * g2[...], xhat, rstd)
        incs = [loss, _colsum(dout * xhat), _colsum(dout), _colsum(dr2 * y2)]
        return [dr2, dr2 * gate2], incs

    res, accs = rowwise("ffn_down_loss", body, ntiles=s_rows // TM,
                        tiled=[(a, _rows(TM, FF)), (xhat1, _rows(TM, D)), (target, _rows(TM, D))],
                        full=[modrows, ln1_g, ln1_b, ln2_g, ln2_b, w_down],
                        outs=[((s_rows, D), F32, _rows(TM, D)), ((s_rows, D), MXU_DTYPE, _rows(TM, D))],
                        accs=[(1, 1), (1, D), (1, D), (1, D)])
    return res, accs


def ffn_down_bwd(dy2, w_down_t, s_rows):
    def body(t, vals, fr):
        return [jnp.dot(vals[0][...], fr[0][...], preferred_element_type=F32)], []

    res, _ = rowwise("ffn_down_bwd", body, ntiles=s_rows // TM, tiled=[(dy2, _rows(TM, D))], full=[w_down_t],
                     outs=[((s_rows, FF), F32, _rows(TM, FF))])
    return res[0]


def swiglu_conv_bwd(u0, da, conv_w8, conv_b, s_rows):
    w2 = 2 * FF
    nt = s_rows // TC
    n = TC + 16

    def body(t, vals, fr):
        u_ref, upv, unx, da_ref, apv, anx = vals
        cw, cb = fr
        first, last = t == 0, t == nt - 1
        ue = jnp.concatenate([jnp.where(first, 0.0, upv[...]), u_ref[...], jnp.where(last, 0.0, unx[...])], axis=0)
        ae = jnp.concatenate([jnp.where(first, 0.0, apv[...]), da_ref[...], jnp.where(last, 0.0, anx[...])], axis=0)
        up = pltpu.roll(ue, 1, 0)
        dn = pltpu.roll(ue, n - 1, 0)
        uc = cw[0:1, :] * up + cw[1:2, :] * ue + cw[2:3, :] * dn + cb[...]
        gate, val = uc[:, :FF], uc[:, FF:]
        sg = jax.nn.sigmoid(gate)
        du = jnp.concatenate([ae * val * (sg * (1.0 + gate * (1.0 - sg))), ae * (gate * sg)], axis=1)
        du0 = cw[0:1, :] * pltpu.roll(du, n - 1, 0) + cw[1:2, :] * du + cw[2:3, :] * pltpu.roll(du, 1, 0)
        rows = slice(8, 8 + TC)
        dut = du[rows]
        return [du0[rows]], [_colsum(dut), _colsum(up[rows] * dut), _colsum(ue[rows] * dut), _colsum(dn[rows] * dut)]

    hp, hn = _halo_specs(TC, w2, s_rows)
    ap, an = _halo_specs(TC, FF, s_rows)
    res, accs = rowwise("swiglu_conv_bwd", body, ntiles=nt,
                        tiled=[(u0, _rows(TC, w2)), (u0, hp), (u0, hn), (da, _rows(TC, FF)), (da, ap), (da, an)],
                        full=[conv_w8, conv_b], outs=[((s_rows, w2), MXU_DTYPE, _rows(TC, w2))], accs=[(1, w2)] * 4)
    return res[0], accs


def ffn_up_ln1_bwd(du0, dr2, xhat1, y, rstd1, modrows, ln_g, ln_b, w_up_t, s_rows):
    def body(t, vals, fr):
        du0_, dr2_, xh, y_, rstd = (v[...] for v in vals)
        mod, g_r, b_r, wt = fr
        dh2 = jnp.dot(du0_, wt[...], preferred_element_type=F32)
        x1 = xh * g_r[...] + b_r[...]
        dx1 = ALPHA * dr2_ + dh2 * (1.0 + _mrow(mod, M_SCALE2))
        dr1 = _layer_norm_bwd(dx1 * g_r[...], xh, rstd)
        incs = [_colsum(dh2 * x1), _colsum(dh2), _colsum(dx1 * xh), _colsum(dx1), _colsum(dr1 * y_)]
        return [dr1 * _mrow(mod, M_GATE1), ALPHA * dr1], incs

    res, accs = rowwise("ffn_up_ln1_bwd", body, ntiles=s_rows // TM,
                        tiled=[(du0, _rows(TM, 2 * FF)), (dr2, _rows(TM, D)), (xhat1, _rows(TM, D)), (y, _rows(TM, D)),
                               (rstd1, _rows(TM, 1))],
                        full=[modrows, ln_g, ln_b, w_up_t],
                        outs=[((s_rows, D), MXU_DTYPE, _rows(TM, D)), ((s_rows, D), F32, _rows(TM, D))],
                        accs=[(1, D)] * 5)
    return res, accs


def merge_bwd(dy, ya, yb, gl, w_out_t, wba_t, wbb_t, s_rows):
    def body(t, vals, fr):
        dy_, ya_, yb_, gl_ = (v[...] for v in vals)
        wot, wat, wbt = fr
        dmrg = jnp.dot(dy_, wot[...], preferred_element_type=F32)
        ga = jax.nn.sigmoid(gl_[:, :D])
        gb = jax.nn.sigmoid(gl_[:, D:])
        dya = dmrg * ga
        dyb = dmrg * gb
        dgl = jnp.concatenate([dmrg * ya_ * ga * (1.0 - ga), dmrg * yb_ * gb * (1.0 - gb)], axis=1)
        return [dya, dyb, dgl, _dot(dya, wat[...]), _dot(dyb, wbt[...])], []

    mx = MXU_DTYPE
    res, _ = rowwise("merge_bwd", body, ntiles=s_rows // TM,
                     tiled=[(dy, _rows(TM, D)), (ya, _rows(TM, D)), (yb, _rows(TM, D)), (gl, _rows(TM, 2 * D))],
                     full=[w_out_t, wba_t, wbb_t],
                     outs=[((s_rows, D), mx, _rows(TM, D))] * 2 + [((s_rows, 2 * D), F32, _rows(TM, 2 * D))] +
                          [((s_rows, Q_W), F32, _rows(TM, Q_W))] * 2)
    return res


def qk_bwd(dqa, dka_t, dva_t, dqb_heads, dkb_t, dvb_t, dgl, tq, rq, tk, rk, cos, sin, qg, kg, bd, place, n_lat_tiles, n):
    def placed(xt_ref, place_ref):
        xt = xt_ref[...]
        hi = xt.astype(BF16)
        r1 = xt - hi.astype(F32)
        mid = r1.astype(BF16)
        lo = (r1 - mid.astype(F32)).astype(BF16)
        pm = place_ref[...]
        return sum(lax.dot_general(term, pm, (((0,), (0,)), ((), ())), preferred_element_type=F32) for term in (hi, mid, lo))

    def body(t, vals, fr):
        dqa_, dgl_, tq_, rq_, tk_, rk_, cs, sn = (v[...] for v in vals[:8])
        qg_r, kg_r, bd_r, pl_r = fr
        dka_, dva_, dkb_, dvb_ = (placed(v, pl_r) for v in vals[8:12])
        dqb_ = jnp.concatenate([v[...] for v in vals[12:]], axis=1)
        is_ctx = t >= n_lat_tiles
        cos4 = jnp.concatenate([cs] * 4, axis=1)
        sin4 = jnp.concatenate([sn] * 4, axis=1)
        zero = jnp.zeros_like(dqa_)
        dpqa = jnp.where(is_ctx, zero, _rope_t(dqa_, cos4, sin4) * Q_SCALE)
        dpka = _rope_t(dka_, cos4, sin4)
        dpva = dva_
        dnq = jnp.where(is_ctx, zero, _rope_t(dqb_, cos4, sin4) * Q_SCALE)
        gq = qg_r[...] * dnq
        dtq = rq_ * gq - tq_ * (rq_ * rq_ * rq_) * (_seg_sum64(gq * tq_, bd_r) * (1.0 / HEAD_DIM))
        dnk = _rope_t(dkb_, cos4, sin4)
        gk = kg_r[...] * dnk
        dtk = rk_ * gk - tk_ * (rk_ * rk_ * rk_) * (_seg_sum64(gk * tk_, bd_r) * (1.0 / HEAD_DIM))
        dgl32 = jnp.where(is_ctx, jnp.zeros_like(dgl_), dgl_)
        dproj = jnp.concatenate([dpqa, dpka, dpva, dtq, dtk, dvb_, dgl32], axis=1)
        return [dproj], [_colsum(dproj), _colsum(dnq * tq_ * rq_), _colsum(dnk * tk_ * rk_)]

    lat = lambda t: jnp.minimum(t, n_lat_tiles - 1)
    qs = _rows(TM, Q_W)
    ts = pl.BlockSpec((N_KV * HEAD_DIM, TM), lambda i: (0, i))
    res, accs = rowwise(
        "qk_bwd", body, ntiles=n // TM,
        tiled=[(dqa, _rows(TM, Q_W, lat)), (dgl, _rows(TM, 2 * D, lat)),
               (tq, qs), (rq, qs), (tk, qs), (rk, qs), (cos, _rows(TM, LANES)), (sin, _rows(TM, LANES)),
               (dka_t, ts), (dva_t, ts), (dkb_t, ts), (dvb_t, ts)] + [(d, _rows(TM, KV_W, lat)) for d in dqb_heads],
        full=[qg, kg, bd, place], outs=[((n, EXT_COLS), MXU_DTYPE, _rows(TM, EXT_COLS))],
        accs=[(1, EXT_COLS), (1, Q_W), (1, Q_W)])
    return res[0], accs


def inproj_bwd(name, dproj, xa, dxp, modrows, w_ext_t, *, ntiles, tile_off, is_ctx, out_rows):
    kc = M_SCALEC if is_ctx else M_SCALE1

    def body(t, vals, fr):
        dp, x_ = vals[0][...], vals[1][...]
        mod, wt = fr
        dh = jnp.dot(dp, wt[...], preferred_element_type=F32)
        incs = [_colsum(dh * x_), _colsum(dh)]
        if is_ctx:
            return [], incs
        return [vals[2][...] + dh * (1.0 + _mrow(mod, kc))], incs

    tiled = [(dproj, _rows(TM, EXT_COLS, lambda i: i + tile_off)), (xa, _rows(TM, D, lambda i: i + tile_off))]
    outs = []
    if not is_ctx:
        tiled.append((dxp, _rows(TM, D)))
        outs = [((out_rows, D), F32, _rows(TM, D))]
    return rowwise(name, body, ntiles=ntiles, tiled=tiled, full=[modrows, w_ext_t], outs=outs, accs=[(1, D)] * 2)


def _attn_semantics():
    return _cparams(("arbitrary", "arbitrary", "arbitrary"))


GLOB_TK = (1280, 1024, 768, 512, 256)
GLOB_TQ = (512, 256)
GLOB_BWD_TQ = GLOB_TQ
KEY_CHUNK = 256


def glob_fwd(q, kt, v_t, s_rows):
    n = kt.shape[0]
    tq = _pick(s_rows, GLOB_TQ)
    tk = _pick(n, GLOB_TK)
    nq, nk = s_rows // tq, n // tk
    r = GROUPS * tq
    nch = tk // KEY_CHUNK

    def produce(qs, k_ref, s_buf, c, mx):
        rows = slice(c * KEY_CHUNK, (c + 1) * KEY_CHUNK)
        sn = _dot_nt(k_ref[rows, :], qs[...])
        s_buf[rows, :] = sn
        return jnp.maximum(mx, jnp.max(sn, axis=0, keepdims=True))

    def kern(q_ref, k0_ref, kn_ref, vt_ref, o_ref, lse_ref, qs, s_buf, mx_buf, m_s, l_s, acc):
        j = pl.program_id(2)

        @pl.when(j == 0)
        def _():
            qs[...] = _stack_tiles(q_ref[...], qs.dtype)
            mx = jnp.full((1, r), -jnp.inf, F32)
            for c in range(nch):
                mx = produce(qs, k0_ref, s_buf, c, mx)
            mx_buf[...] = mx
            m_s[...] = jnp.full_like(m_s, -jnp.inf)
            l_s[...] = jnp.zeros_like(l_s)
            acc[...] = jnp.zeros_like(acc)

        m_prev = m_s[...]
        m_new = jnp.maximum(m_prev, mx_buf[...])
        alpha = jnp.exp(m_prev - m_new)
        a = alpha * acc[...]
        ls = alpha * l_s[...]
        mx = jnp.full((1, r), -jnp.inf, F32)
        for c in range(nch):
            rows = slice(c * KEY_CHUNK, (c + 1) * KEY_CHUNK)
            p = jnp.exp(s_buf[rows, :] - m_new)
            ls = ls + jnp.sum(p, axis=0, keepdims=True)
            a = a + jnp.dot(vt_ref[0, :, rows], p.astype(MXU_DTYPE), preferred_element_type=F32)
            mx = produce(qs, kn_ref, s_buf, c, mx)
        mx_buf[...] = mx
        l_s[...] = ls
        acc[...] = a
        m_s[...] = m_new

        @pl.when(j == nk - 1)
        def _():
            o_t = acc[...] / l_s[...]
            o_ref[...] = jnp.concatenate([_untranspose_groups(o_t[:, GROUPS * a:GROUPS * (a + TM)], TM)
                                          for a in range(0, tq, TM)], axis=0)
            lse_ref[0, 0] = _row_to_col(m_s[...] + jnp.log(l_s[...]))

    kspec = lambda f: pl.BlockSpec((tk, KV_W), lambda h, i, j: (f(j), h))
    return pl.pallas_call(
        kern, name="glob_fwd", grid=(N_KV, nq, nk),
        in_specs=[pl.BlockSpec((tq, KV_W), lambda h, i, j: (i, h)), kspec(lambda j: 0),
                  kspec(lambda j: jnp.minimum(j + 1, nk - 1)), pl.BlockSpec((1, HEAD_DIM, tk), lambda h, i, j: (h, 0, j))],
        out_specs=[pl.BlockSpec((tq, KV_W), lambda h, i, j: (i, h)),
                   pl.BlockSpec((1, 1, r, 1), lambda h, i, j: (h, i, 0, 0))],
        out_shape=[SDS((s_rows, Q_W), F32), SDS((N_KV, nq, r, 1), F32)],
        scratch_shapes=[pltpu.VMEM((r, KV_W), MXU_DTYPE), pltpu.VMEM((tk, r), F32), pltpu.VMEM((1, r), F32),
                        pltpu.VMEM((1, r), F32), pltpu.VMEM((1, r), F32), pltpu.VMEM((HEAD_DIM, r), F32)],
        compiler_params=_attn_semantics(),
    )(q, kt, kt, v_t)


def attn_delta(o, do, s_rows):
    tq = _pick(s_rows, GLOB_BWD_TQ)
    nq = s_rows // tq
    r = GROUPS * tq

    def kern(o_ref, do_ref, d_ref):
        d_ref[0, 0] = jnp.sum(_stack_tiles(do_ref[...] * o_ref[...], F32), axis=1, keepdims=True)

    qspec = pl.BlockSpec((tq, KV_W), lambda h, i: (i, h))
    return pl.pallas_call(
        kern, name="attn_delta", grid=(N_KV, nq), in_specs=[qspec, qspec],
        out_specs=pl.BlockSpec((1, 1, r, 1), lambda h, i: (h, i, 0, 0)), out_shape=SDS((N_KV, nq, r, 1), F32),
        compiler_params=_cparams(("parallel", "parallel")),
    )(o, do)


def _compact_t(tt, dtype):
    return jnp.concatenate([tt[g * HEAD_DIM:(g + 1) * HEAD_DIM, :] for g in range(GROUPS)], axis=1).astype(dtype)


def glob_bwd(q, q_t, kt, vt, do, do_t, lse, delta, h, s_rows):
    n = kt.shape[0]
    tq = _pick(s_rows, GLOB_BWD_TQ)
    tk = _pick(n, GLOB_TK)
    nq, nk = s_rows // tq, n // tk
    r = GROUPS * tq
    nch = tk // KEY_CHUNK

    def kern(q_ref, qt_ref, k_ref, v_ref, do_ref, dot_ref, lse_ref, dl_ref, dq_ref, dkt_ref, dvt_ref, p_buf, ds_buf):
        j = pl.program_id(0)
        i = pl.program_id(1)

        @pl.when(i == 0)
        def _():
            dkt_ref[...] = jnp.zeros_like(dkt_ref)
            dvt_ref[...] = jnp.zeros_like(dvt_ref)

        qs = _stack_tiles(q_ref[...], MXU_DTYPE)
        dos = _stack_tiles(do_ref[...], MXU_DTYPE)
        lse_b = jnp.broadcast_to(lse_ref[0, 0], (r, LANES))
        dl_b = jnp.broadcast_to(dl_ref[0, 0], (r, LANES))
        for c in range(nch):
            lo = c * KEY_CHUNK
            sc = _dot_nt(qs, k_ref[lo:lo + KEY_CHUNK, :])
            dpc = _dot_nt(dos, v_ref[lo:lo + KEY_CHUNK, :])
            for t in range(KEY_CHUNK // LANES):
                sl = slice(t * LANES, (t + 1) * LANES)
                pt = jnp.exp(sc[:, sl] - lse_b)
                p_buf[:, lo + t * LANES:lo + (t + 1) * LANES] = pt.astype(p_buf.dtype)
                ds_buf[:, lo + t * LANES:lo + (t + 1) * LANES] = (pt * (dpc[:, sl] - dl_b)).astype(ds_buf.dtype)
        dq_t = _fold_tiles(jnp.dot(ds_buf[...], k_ref[...], preferred_element_type=F32), tq)
        rows = pl.ds(pl.multiple_of(i * tq, tq), tq)

        @pl.when(j == 0)
        def _():
            dq_ref[rows, :] = dq_t

        @pl.when(j > 0)
        def _():
            dq_ref[rows, :] += dq_t

        dvt_ref[...] += jnp.dot(_compact_tiles_t(dot_ref[...], MXU_DTYPE), p_buf[...], preferred_element_type=F32)
        dkt_ref[...] += jnp.dot(_compact_tiles_t(qt_ref[...], MXU_DTYPE), ds_buf[...], preferred_element_type=F32)

    col = pl.BlockSpec((1, 1, r, 1), lambda j, i: (h, i, 0, 0))
    qspec = pl.BlockSpec((tq, KV_W), lambda j, i: (i, h))
    tspec = pl.BlockSpec((KV_W, tq), lambda j, i: (h, i))
    kspec = pl.BlockSpec((tk, KV_W), lambda j, i: (j, h))
    ospec = pl.BlockSpec((HEAD_DIM, tk), lambda j, i: (0, j))
    return pl.pallas_call(
        kern, name=f"glob_bwd_h{h}", grid=(nk, nq),
        in_specs=[qspec, tspec, kspec, kspec, qspec, tspec, col, col],
        out_specs=[pl.BlockSpec(memory_space=pltpu.VMEM), ospec, ospec],
        out_shape=[SDS((s_rows, KV_W), F32), SDS((HEAD_DIM, n), F32), SDS((HEAD_DIM, n), F32)],
        scratch_shapes=[pltpu.VMEM((r, tk), MXU_DTYPE), pltpu.VMEM((r, tk), MXU_DTYPE)],
        compiler_params=_cparams(("arbitrary", "arbitrary")),
    )(q, q_t, kt, vt, do, do_t, lse, delta)


TW = 2 * WIN
WR = GROUPS * TW
WLAT = 4 * WIN


def _win_cat(dst, parts):
    off = 0
    for p in parts:
        dst[off:off + p.shape[0], :] = p[...]
        off += p.shape[0]


def _win_specs(s_rows, c_rows):
    nb = s_rows // WIN
    prev = lambda i: jnp.maximum(2 * i - 1, 0)
    nxt = lambda i: jnp.minimum(2 * i + 2, nb - 1)
    rows = [pl.BlockSpec((WIN, KV_W), lambda h, i: (prev(i), h)), pl.BlockSpec((TW, KV_W), lambda h, i: (i, h)),
            pl.BlockSpec((WIN, KV_W), lambda h, i: (nxt(i), h)), pl.BlockSpec((c_rows, KV_W), lambda h, i: (s_rows // c_rows, h))]
    cols = [pl.BlockSpec((1, HEAD_DIM, WIN), lambda h, i: (h, 0, prev(i))), pl.BlockSpec((1, HEAD_DIM, TW), lambda h, i: (h, 0, i)),
            pl.BlockSpec((1, HEAD_DIM, WIN), lambda h, i: (h, 0, nxt(i))),
            pl.BlockSpec((1, HEAD_DIM, c_rows), lambda h, i: (h, 0, s_rows // c_rows))]
    return rows, cols


def _win_mask(i, s_rows, shape, keys_on_rows):
    a = lax.broadcasted_iota(jnp.int32, shape, 0)
    b = lax.broadcasted_iota(jnp.int32, shape, 1)
    kk, qq = (a, b) if keys_on_rows else (b, a)
    qpos = i * TW + (qq & (TW - 1))
    kpos = (2 * i - 1) * WIN + kk
    band = (jnp.abs(qpos - kpos) <= WIN) & (kpos >= 0) & (kpos < s_rows)
    return (kk >= WLAT) | band


def _untranspose_groups(o_t, tq):
    row = lax.broadcasted_iota(jnp.int32, (HEAD_DIM, KV_W), 0)
    col = lax.broadcasted_iota(jnp.int32, (HEAD_DIM, KV_W), 1)
    hi = o_t.astype(BF16)
    r1 = o_t - hi.astype(F32)
    mid = r1.astype(BF16)
    lo = (r1 - mid.astype(F32)).astype(BF16)
    o = jnp.zeros((tq, KV_W), F32)
    for g in range(GROUPS):
        sel = jnp.where(col == row + g * HEAD_DIM, 1.0, 0.0).astype(BF16)
        for term in (hi, mid, lo):
            o = o + lax.dot_general(term[:, g * tq:(g + 1) * tq], sel, (((0,), (0,)), ((), ())), preferred_element_type=F32)
    return o


def win_fwd(q, kt, v_t, sinkrow, s_rows, c_rows):
    nt = s_rows // TW
    nkeys = WLAT + c_rows

    def kern(q_ref, kp, kc, kn, kx, vp, vc, vn, vx, sink_ref, o_ref, lse_ref, kcat):
        i = pl.program_id(1)
        _win_cat(kcat, (kp, kc, kn, kx))
        qs = _stack_groups(q_ref[...], MXU_DTYPE)
        st = _dot_nt(kcat[...], qs)
        st = jnp.where(_win_mask(i, s_rows, st.shape, True), st, -jnp.inf)
        sink = sink_ref[0]
        m = jnp.maximum(jnp.max(st, axis=0, keepdims=True), sink)
        e = jnp.exp(st - m)
        den = jnp.sum(e, axis=0, keepdims=True) + jnp.exp(sink - m)
        v_cat = jnp.concatenate([vp[0], vc[0], vn[0], vx[0]], axis=1)
        o_t = jnp.dot(v_cat, e.astype(MXU_DTYPE), preferred_element_type=F32) / den
        o_ref[...] = _untranspose_groups(o_t, TW)
        lse_ref[0, 0] = _row_to_col(m + jnp.log(den))

    rows, cols = _win_specs(s_rows, c_rows)
    qspec = pl.BlockSpec((TW, KV_W), lambda h, i: (i, h))
    rowv = pl.BlockSpec((1, 1, WR, 1), lambda h, i: (h, i, 0, 0))
    return pl.pallas_call(
        kern, name="win_fwd", grid=(N_KV, nt),
        in_specs=[qspec] + rows + cols + [pl.BlockSpec((1, 1, WR), lambda h, i: (h, 0, 0))],
        out_specs=[qspec, rowv], out_shape=[SDS((s_rows, Q_W), F32), SDS((N_KV, nt, WR, 1), F32)],
        scratch_shapes=[pltpu.VMEM((nkeys, KV_W), MXU_DTYPE)],
        compiler_params=_cparams(("parallel", "parallel")),
    )(q, kt, kt, kt, kt, v_t, v_t, v_t, v_t, sinkrow)


def win_bwd(q, q_t, kt, vt, sinkcol, o, do, do_t, lse, s_rows, c_rows):
    nt = s_rows // TW
    nkeys = WLAT + c_rows
    n = s_rows + c_rows
    ctx0 = WIN + s_rows

    def kern(q_ref, qt_ref, kp, kc, kn, kx, vp, vc, vn, vx, sink_ref, o_ref, do_ref, dot_ref, lse_ref,
             dq_ref, dkt_ref, dvt_ref, dsk_ref, kcat, vcat):
        i = pl.program_id(1)

        @pl.when(i == 0)
        def _():
            dkt_ref[...] = jnp.zeros_like(dkt_ref)
            dvt_ref[...] = jnp.zeros_like(dvt_ref)
            dsk_ref[...] = jnp.zeros_like(dsk_ref)

        _win_cat(kcat, (kp, kc, kn, kx))
        _win_cat(vcat, (vp, vc, vn, vx))
        qs = _stack_groups(q_ref[...], MXU_DTYPE)
        do32 = _stack_groups(do_ref[...], F32)
        delta = jnp.sum(do32 * _stack_groups(o_ref[...], F32), axis=1, keepdims=True)
        dos = do32.astype(MXU_DTYPE)
        lse_c = lse_ref[0, 0]
        s = _dot_nt(qs, kcat[...])
        s = jnp.where(_win_mask(i, s_rows, s.shape, False), s, -jnp.inf)
        p = jnp.exp(s - lse_c)
        ds = p * (_dot_nt(dos, vcat[...]) - delta)
        dq_ref[...] = _fold_groups(_dot(ds, kcat[...]), TW)
        dvt = jnp.dot(_compact_t(dot_ref[...], MXU_DTYPE), p.astype(MXU_DTYPE), preferred_element_type=F32)
        dkt = jnp.dot(_compact_t(qt_ref[...], MXU_DTYPE), ds.astype(MXU_DTYPE), preferred_element_type=F32)
        lat = pl.ds(pl.multiple_of(i * TW, TW), WLAT)
        dkt_ref[0, :, lat] += dkt[:, :WLAT]
        dvt_ref[0, :, lat] += dvt[:, :WLAT]
        dkt_ref[0, :, ctx0:ctx0 + c_rows] += dkt[:, WLAT:]
        dvt_ref[0, :, ctx0:ctx0 + c_rows] += dvt[:, WLAT:]
        dsk_ref[0] += -(jnp.exp(sink_ref[0][:, 0:1] - lse_c) * delta)

    rows, _ = _win_specs(s_rows, c_rows)
    qspec = pl.BlockSpec((TW, KV_W), lambda h, i: (i, h))
    tspec = pl.BlockSpec((KV_W, TW), lambda h, i: (h, i))
    col = pl.BlockSpec((1, 1, WR, 1), lambda h, i: (h, i, 0, 0))
    kvt = pl.BlockSpec((1, HEAD_DIM, WIN + n), lambda h, i: (h, 0, 0))
    return pl.pallas_call(
        kern, name="win_bwd", grid=(N_KV, nt),
        in_specs=[qspec, tspec] + rows + rows + [pl.BlockSpec((1, WR, LANES), lambda h, i: (h, 0, 0)), qspec, qspec, tspec, col],
        out_specs=[qspec, kvt, kvt, pl.BlockSpec((1, WR, 1), lambda h, i: (h, 0, 0))],
        out_shape=[SDS((s_rows, Q_W), F32), SDS((N_KV, HEAD_DIM, WIN + n), F32), SDS((N_KV, HEAD_DIM, WIN + n), F32),
                   SDS((N_KV, WR, 1), F32)],
        scratch_shapes=[pltpu.VMEM((nkeys, KV_W), MXU_DTYPE), pltpu.VMEM((nkeys, KV_W), MXU_DTYPE)],
        compiler_params=_cparams(("arbitrary", "arbitrary")),
    )(q, q_t, kt, kt, kt, kt, vt, vt, vt, vt, sinkcol, o, do, do_t, lse)


def adamw(name, w, m, v, grads):
    r, wd = w.shape
    tr = _pick(r, [t for t in ELEMENTWISE_ROWS if t * wd * 4 <= ELEMENTWISE_BLOCK_BYTES])
    stacked = not isinstance(grads, (list, tuple))
    ng = grads.shape[0] if stacked else len(grads)

    def kern(*refs):
        w_ref, m_ref, v_ref = refs[:3]
        g_refs = refs[3:-4]
        g_out, d_out, m_out, v_out = refs[-4:]
        if stacked:
            g = g_refs[0][0]
            for k in range(1, ng):
                g = g + g_refs[0][k]
        else:
            g = g_refs[0][...]
            for gr in g_refs[1:]:
                g = g + gr[...]
        wv = w_ref[...]
        mn = ADAM_B1 * m_ref[...] + (1.0 - ADAM_B1) * g
        vn = ADAM_B2 * v_ref[...] + (1.0 - ADAM_B2) * (g * g)
        m_hat = mn / (1.0 - ADAM_B1 ** ADAM_STEP)
        v_hat = vn / (1.0 - ADAM_B2 ** ADAM_STEP)
        g_out[...] = g
        d_out[...] = -ADAM_LR * (m_hat / (jnp.sqrt(v_hat) + ADAM_EPS) + ADAM_WD * wv)
        m_out[...] = mn
        v_out[...] = vn

    spec = pl.BlockSpec((tr, wd), lambda i: (i, 0))
    gspecs = [pl.BlockSpec((ng, tr, wd), lambda i: (0, i, 0))] if stacked else [spec] * ng
    return pl.pallas_call(
        kern, name=name, grid=(r // tr,), in_specs=[spec] * 3 + gspecs, out_specs=[spec] * 4,
        out_shape=[SDS((r, wd), F32)] * 4, compiler_params=_cparams(("parallel",)),
    )(w, m, v, *([grads] if stacked else grads))


def add2(name, a, b):
    k, r, w = a.shape
    tr = _pick(r, [t for t in ELEMENTWISE_ROWS if t * w * 4 <= ELEMENTWISE_BLOCK_BYTES])

    def kern(a_ref, b_ref, o_ref):
        o_ref[...] = a_ref[...] + b_ref[...]

    spec = pl.BlockSpec((1, tr, w), lambda s, i: (s, i, 0))
    return pl.pallas_call(kern, name=name, grid=(k, r // tr), in_specs=[spec, spec], out_specs=spec,
                          out_shape=SDS(a.shape, a.dtype), compiler_params=_cparams(("parallel", "parallel")))(a, b)


def _rep4(a, off):
    return jnp.concatenate([a[:, off + HEAD_DIM * h: off + HEAD_DIM * (h + 1)] for h in range(N_KV) for _ in range(GROUPS)], axis=1)


def _extend_cols(a):
    return jnp.concatenate([a[:, 0:OFF_KA], _rep4(a, OFF_KA), _rep4(a, OFF_VA), a[:, OFF_QB:OFF_KB],
                            _rep4(a, OFF_KB), _rep4(a, OFF_VB), a[:, OFF_GA:]], axis=1)


def _fold4(a, off):
    r = a.shape[0]
    return a[:, off:off + Q_W].reshape(r, N_KV, GROUPS, HEAD_DIM).sum(axis=2).reshape(r, N_KV * HEAD_DIM)


def _fold_cols(a):
    return jnp.concatenate([a[:, X_QA:X_QA + Q_W], _fold4(a, X_KA), _fold4(a, X_VA), a[:, X_QB:X_QB + Q_W],
                            _fold4(a, X_KB), _fold4(a, X_VB), a[:, X_GL:]], axis=1)


def _rope_tables(s_rows, c_rows):
    n_rows = s_rows // GRID_W
    n_freq = HEAD_DIM // 4
    inv_freq = ROPE_THETA ** (-jnp.arange(n_freq, dtype=F32) / n_freq)
    ang_r = jnp.arange(n_rows, dtype=jnp.int32).astype(F32)[:, None] * inv_freq
    ang_c = jnp.arange(GRID_W, dtype=jnp.int32).astype(F32)[:, None] * inv_freq
    by_row = lambda t: jnp.repeat(t, GRID_W, axis=0)
    by_col = lambda t: jnp.tile(t, (n_rows, 1))
    cos = jnp.concatenate([by_row(jnp.cos(ang_r))] * 2 + [by_col(jnp.cos(ang_c))] * 2, axis=1)
    sin_r, sin_c = by_row(jnp.sin(ang_r)), by_col(jnp.sin(ang_c))
    sin = jnp.concatenate([-sin_r, sin_r, -sin_c, sin_c], axis=1)
    cos = jnp.concatenate([cos, jnp.ones((c_rows, HEAD_DIM), F32)], axis=0)
    sin = jnp.concatenate([sin, jnp.zeros((c_rows, HEAD_DIM), F32)], axis=0)
    return jnp.concatenate([cos, cos], axis=1), jnp.concatenate([sin, sin], axis=1)


def _ff_pad_cols(a):
    r = a.shape[0]
    a = jnp.pad(a.reshape(r, N_DEV, FF_SHARD), ((0, 0), (0, 0), (0, FF_SHARD_PAD - FF_SHARD)))
    return a.reshape(r, 2 * FF)


def _ff_unpad_cols(a):
    r = a.shape[0]
    return a.reshape(r, N_DEV, FF_SHARD_PAD)[:, :, :FF_SHARD].reshape(r, 2 * D_FF)


def _ff_pad_rows(a):
    c = a.shape[1]
    a = jnp.pad(a.reshape(N_DEV // 2, FF_SHARD, c), ((0, 0), (0, FF_SHARD_PAD - FF_SHARD), (0, 0)))
    return a.reshape(FF, c)


def _ff_unpad_rows(a):
    c = a.shape[1]
    return a.reshape(N_DEV // 2, FF_SHARD_PAD, c)[:, :FF_SHARD].reshape(D_FF, c)


BIG = (("w_in", (D, IN_COLS // N_DEV)), ("w_branch_a", (Q_W, D // N_DEV)), ("w_branch_b", (Q_W, D // N_DEV)),
       ("w_out", (D // N_DEV, D)), ("w_up", (D, FF_SHARD_PAD)), ("w_down", (D_FF // N_DEV, D)))
BIG_SIZES = tuple(int(np.prod(s)) for _, s in BIG)
BIG_ROWS = sum(BIG_SIZES) // LANES


def _pack_big(parts):
    return jnp.concatenate([p.reshape(-1) for p in parts]).reshape(BIG_ROWS, LANES)


def _unpack_big(flat):
    lead = flat.shape[:-2]
    f = flat.reshape(*lead, BIG_ROWS * LANES)
    out, off = [], 0
    for (_, shp), sz in zip(BIG, BIG_SIZES, strict=True):
        out.append(f[..., off:off + sz].reshape(*lead, *shp))
        off += sz
    return out


def _cols_to_full(g):
    return jnp.transpose(g, (1, 0, 2)).reshape(g.shape[1], -1)


def _full_to_cols(a):
    r, c = a.shape
    return jnp.transpose(a.reshape(r, N_DEV, c // N_DEV), (1, 0, 2))


SMALL = (("c_ctx", D), ("b_mod", N_MOD * D), ("b_in", IN_COLS), ("attn_sink", N_HEADS), ("q_norm_g", HEAD_DIM),
         ("k_norm_g", HEAD_DIM), ("ln1_g", D), ("ln1_b", D), ("conv_w", 3 * 2 * D_FF // N_DEV), ("conv_b", 2 * D_FF),
         ("ln2_g", D), ("ln2_b", D))
SMALL_TOTAL = sum(n for _, n in SMALL)
SMALL_ROWS = -(-SMALL_TOTAL // (8 * LANES)) * 8


def _pack_small(parts):
    flat = jnp.concatenate([p.reshape(-1).astype(F32) for p in parts])
    return jnp.pad(flat, (0, SMALL_ROWS * LANES - flat.shape[0])).reshape(SMALL_ROWS, LANES)


def _unpack_small(packed):
    f = packed.reshape(-1)
    out, off = {}, 0
    for name, n in SMALL:
        out[name] = f[off:off + n]
        off += n
    return out


RED = (("c_ctx", D), ("b_in", IN_COLS), ("attn_sink", N_HEADS), ("q_norm_g", HEAD_DIM), ("k_norm_g", HEAD_DIM),
       ("ln1_g", D), ("ln1_b", D), ("conv_w", 3 * 2 * FF), ("conv_b", 2 * FF), ("ln2_g", D), ("ln2_b", D))
RED_TOTAL = sum(n for _, n in RED)
RED_ROWS = -(-RED_TOTAL // (8 * LANES)) * 8


def sum8(name, g):
    _, r, w = g.shape

    def kern(g_ref, o_ref):
        acc = g_ref[0]
        for k in range(1, N_DEV):
            acc = acc + g_ref[k]
        o_ref[...] = acc

    return pl.pallas_call(kern, name=name, out_shape=SDS((r, w), F32))(g)


def _local_step(x, ctx, target, modrows, weights, small):
    s_rows, c_rows = x.shape[0], ctx.shape[0]
    n = s_rows + c_rows
    nl = s_rows // TM
    w_in, wba, wbb, w_out, w_up, w_down = weights
    f = lambda a: a.reshape(1, -1).astype(F32)
    b_in, ln1_g, ln1_b, ln2_g, ln2_b, conv_b = (f(small[k]) for k in ("b_in", "ln1_g", "ln1_b", "ln2_g", "ln2_b", "conv_b"))
    conv_w8 = jnp.pad(small["conv_w_full"], ((0, 5), (0, 0)))
    qg = jnp.tile(small["q_norm_g"].reshape(1, HEAD_DIM), (1, N_HEADS))
    kg = jnp.tile(small["k_norm_g"].reshape(1, HEAD_DIM), (1, N_HEADS))
    sink_rep = jnp.repeat(small["attn_sink"].reshape(N_KV, GROUPS), TW, axis=1)
    sinkrow = sink_rep.reshape(N_KV, 1, WR)
    sinkcol = jnp.broadcast_to(sink_rep[:, :, None], (N_KV, WR, LANES))
    bd = jnp.kron(jnp.eye(N_HEADS, dtype=F32), jnp.ones((HEAD_DIM, HEAD_DIM), F32)).astype(BF16)
    cos, sin = _rope_tables(s_rows, c_rows)
    w_ext = _extend_cols(w_in)
    b_ext = _extend_cols(b_in)
    xa = jnp.concatenate([x, ctx], axis=0)

    hb, qa, kat, vat, qb, kbt, vbt, tq, rq, tk, rk, gl = inproj_fwd(xa, cos, sin, modrows, w_ext, b_ext, qg, kg, bd, nl)
    compact_t = lambda t: jnp.stack([t[:, h * KV_W:h * KV_W + HEAD_DIM].T for h in range(N_KV)])
    oa, lse_a = win_fwd(qa, kat, compact_t(vat), sinkrow, s_rows, c_rows)
    vb_t = compact_t(vbt)
    ob, lse_b = glob_fwd(qb, kbt, vb_t, s_rows)
    tqb = _pick(s_rows, GLOB_BWD_TQ)
    lse_b = lse_b.reshape(N_KV, s_rows // tqb, GROUPS * tqb, 1)
    ya, yb, mrg, y, xhat1, rstd1 = merge_fwd(oa, ob, gl, x, modrows, wba, wbb, w_out, s_rows)
    h2, u0 = ffn_up_fwd(xhat1, modrows, ln1_g, ln1_b, w_up, s_rows)
    a = conv_swiglu_fwd(u0, conv_w8, conv_b, s_rows)
    (dr2, dy2), (loss, dln2_g, dln2_b, dgate2) = ffn_down_loss(a, xhat1, target, modrows, ln1_g, ln1_b, ln2_g, ln2_b, w_down, s_rows)

    da = ffn_down_bwd(dy2, w_down.T, s_rows)
    dw_down = mm_tn("dw_down", a, dy2, s_rows)
    du0, (dconv_b, dcw0, dcw1, dcw2) = swiglu_conv_bwd(u0, da, conv_w8, conv_b, s_rows)
    dw_up = mm_tn("dw_up", h2, du0, s_rows)
    (dy, dxp), (dscale2, dshift2, dln1_g, dln1_b, dgate1) = ffn_up_ln1_bwd(du0, dr2, xhat1, y, rstd1, modrows, ln1_g, ln1_b, w_up.T, s_rows)
    dya, dyb, dgl, doa, dob = merge_bwd(dy, ya, yb, gl, w_out.T, wba.T, wbb.T, s_rows)
    dw_out = mm_tn("dw_out", mrg, dy, s_rows)
    dwba = mm_tn("dw_branch_a", oa, dya, s_rows)
    dwbb = mm_tn("dw_branch_b", ob, dyb, s_rows)

    head_rows = lambda t: t.reshape(N_KV * HEAD_DIM, t.shape[-1])
    place = np.zeros((N_KV * HEAD_DIM, Q_W), np.float32)
    for h in range(N_KV):
        place[h * HEAD_DIM + np.arange(HEAD_DIM), h * KV_W + np.arange(HEAD_DIM)] = 1.0
    place = jnp.asarray(place, BF16)
    dqa, dka_t, dva_t, dsk = win_bwd(qa, qa[:s_rows].T, kat, vat, sinkcol, oa, doa, doa.astype(MXU_DTYPE).T, lse_a,
                                     s_rows, c_rows)
    dka_t, dva_t = head_rows(dka_t[:, :, WIN:]), head_rows(dva_t[:, :, WIN:])
    delta_b = attn_delta(ob, dob, s_rows)
    qb_t = qb[:s_rows].T
    dob_t = dob.astype(MXU_DTYPE).T
    heads = [glob_bwd(qb, qb_t, kbt, vbt, dob, dob_t, lse_b, delta_b, h, s_rows) for h in range(N_KV)]
    dqb = [hd[0] for hd in heads]
    dkb_t = jnp.concatenate([hd[1] for hd in heads], axis=0)
    dvb_t = jnp.concatenate([hd[2] for hd in heads], axis=0)
    dproj, (db_ext, dqg, dkg) = qk_bwd(dqa, dka_t, dva_t, dqb, dkb_t, dvb_t, dgl, tq, rq, tk, rk, cos, sin, qg, kg, bd,
                                       place, nl, n)
    w_ext_t = w_ext.T
    (grad_x,), (dscale1, dshift1) = inproj_bwd("inproj_bwd", dproj, xa, dxp, modrows, w_ext_t, ntiles=nl, tile_off=0,
                                               is_ctx=False, out_rows=s_rows)
    _, (dscale_c, dshift_c) = inproj_bwd("inproj_bwd_ctx", dproj, xa, None, modrows, w_ext_t, ntiles=c_rows // TM,
                                         tile_off=nl, is_ctx=True, out_rows=0)
    dw_in = _fold_cols(mm_tn("dw_in", hb, dproj, n))

    dmod = jnp.concatenate([dshift1, dscale1, dgate1, dshift2, dscale2, dgate2], axis=1)
    dmod_c = jnp.concatenate([dshift_c, dscale_c, jnp.zeros((1, (N_MOD - 2) * D), F32)], axis=1)
    fold_g = lambda t: t.reshape(N_HEADS, HEAD_DIM).sum(axis=0)
    red = {
        "b_in": _fold_cols(db_ext), "attn_sink": dsk.reshape(N_HEADS, TW).sum(axis=1), "q_norm_g": fold_g(dqg),
        "k_norm_g": fold_g(dkg), "ln1_g": dln1_g, "ln1_b": dln1_b, "conv_w": jnp.concatenate([dcw0, dcw1, dcw2], axis=0),
        "conv_b": dconv_b, "ln2_g": dln2_g, "ln2_b": dln2_b,
    }
    return loss[0, 0], grad_x, (dw_in, dwba, dwbb, dw_out, dw_up, dw_down), dmod, dmod_c, red


def kernel(x, c, ctx, c_ctx, w_mod, b_mod, w_in, b_in, attn_sink, q_norm_g, k_norm_g, w_branch_a, w_branch_b, w_out, ln1_g, ln1_b, w_up, conv_w, conv_b, w_down, ln2_g, ln2_b, loss_target, m_c_ctx, m_w_mod, m_b_mod, m_w_in, m_b_in, m_attn_sink, m_q_norm_g, m_k_norm_g, m_w_branch_a, m_w_branch_b, m_w_out, m_ln1_g, m_ln1_b, m_w_up, m_conv_w, m_conv_b, m_w_down, m_ln2_g, m_ln2_b, v_c_ctx, v_w_mod, v_b_mod, v_w_in, v_b_in, v_attn_sink, v_q_norm_g, v_k_norm_g, v_w_branch_a, v_w_branch_b, v_w_out, v_ln1_g, v_ln1_b, v_w_up, v_conv_w, v_conv_b, v_w_down, v_ln2_g, v_ln2_b):
    ax, ay, ac = (lax.axis_index(a) for a in AXES)
    me = 4 * ax + 2 * ay + ac
    chip = 2 * ax + ay
    mod_w = N_MOD * D // N_DEV
    params = dict(c_ctx=c_ctx, w_mod=w_mod, b_mod=b_mod, w_in=w_in, b_in=b_in, attn_sink=attn_sink, q_norm_g=q_norm_g,
                  k_norm_g=k_norm_g, w_branch_a=w_branch_a, w_branch_b=w_branch_b, w_out=w_out, ln1_g=ln1_g, ln1_b=ln1_b,
                  w_up=w_up, conv_w=conv_w, conv_b=conv_b, w_down=w_down, ln2_g=ln2_g, ln2_b=ln2_b)
    mom_m = dict(c_ctx=m_c_ctx, w_mod=m_w_mod, b_mod=m_b_mod, w_in=m_w_in, b_in=m_b_in, attn_sink=m_attn_sink,
                 q_norm_g=m_q_norm_g, k_norm_g=m_k_norm_g, w_branch_a=m_w_branch_a, w_branch_b=m_w_branch_b, w_out=m_w_out,
                 ln1_g=m_ln1_g, ln1_b=m_ln1_b, w_up=m_w_up, conv_w=m_conv_w, conv_b=m_conv_b, w_down=m_w_down,
                 ln2_g=m_ln2_g, ln2_b=m_ln2_b)
    mom_v = dict(c_ctx=v_c_ctx, w_mod=v_w_mod, b_mod=v_b_mod, w_in=v_w_in, b_in=v_b_in, attn_sink=v_attn_sink,
                 q_norm_g=v_q_norm_g, k_norm_g=v_k_norm_g, w_branch_a=v_w_branch_a, w_branch_b=v_w_branch_b, w_out=v_w_out,
                 ln1_g=v_ln1_g, ln1_b=v_ln1_b, w_up=v_w_up, conv_w=v_conv_w, conv_b=v_conv_b, w_down=v_w_down,
                 ln2_g=v_ln2_g, ln2_b=v_ln2_b)
    big_names = [nm for nm, _ in BIG]

    def shard(tree, nm):
        t = tree[nm][0]
        return jnp.pad(t, ((0, 0), (0, FF_SHARD_PAD - FF_SHARD))) if nm == "w_up" else t

    wg = all_gather("ag_weights", _pack_big([shard(params, nm).astype(MXU_DTYPE) for nm in big_names]))
    g_in, g_ba, g_bb, g_out, g_up, g_down = _unpack_big(wg)
    weights = (_cols_to_full(g_in), _cols_to_full(g_ba), _cols_to_full(g_bb), g_out.reshape(D, D), _cols_to_full(g_up),
               _ff_pad_rows(g_down.reshape(D_FF, D)))

    c_all = all_gather("ag_c", c.reshape(8, LANES)).reshape(N_DEV, D)
    cs = jnp.concatenate([c_all, c_ctx.reshape(1, D), jnp.zeros((7, D), F32)], axis=0)
    w_mod_sh = w_mod[0]
    b_mod_sh = lax.dynamic_slice(b_mod, (0, me * mod_w), (1, mod_w))
    mod_part = mod_fwd(cs, w_mod_sh, b_mod_sh)
    mg = all_gather("ag_mod", mod_part.reshape(16 * mod_w // LANES, LANES)).reshape(N_DEV, 16, mod_w)
    mod = lax.dynamic_index_in_dim(mg, me, axis=1, keepdims=False).reshape(N_MOD, D)
    mod_c = mg[:, 8, :].reshape(N_MOD, D)
    modrows = jnp.stack([mod[0], mod[1], mod_c[0], mod_c[1], mod[2], mod[3], mod[4], mod[5]], axis=0)

    conv_w_full = all_gather("ag_conv_w", jnp.pad(conv_w[0], ((0, 5), (0, FF_SHARD_PAD - FF_SHARD))))
    conv_w_full = _cols_to_full(conv_w_full[:, :3, :])
    small = dict(b_in=b_in, ln1_g=ln1_g, ln1_b=ln1_b, ln2_g=ln2_g, ln2_b=ln2_b, conv_b=_ff_pad_cols(conv_b),
                 conv_w_full=conv_w_full, q_norm_g=q_norm_g, k_norm_g=k_norm_g, attn_sink=attn_sink)
    loss, grad_x, big_grads, dmod, dmod_c, red = _local_step(x[0], ctx[0], loss_target[0], modrows, weights, small)
    loss = lax.psum(loss, AXES)

    dm = all_gather("ag_dmod", jnp.concatenate([dmod, dmod_c], axis=0).reshape(2 * N_MOD * D // LANES, LANES))
    dm = dm.reshape(N_DEV, 2, N_MOD * D)
    dm_all = jnp.concatenate([dm[:, 0], dm[:, 1]], axis=0)
    dm_sh = lax.dynamic_slice(dm_all, (0, me * mod_w), (16, mod_w))
    dw_mod, dcc, db_mod = mod_bwd(cs, w_mod_sh, dm_sh, dm_all)
    red["c_ctx"] = dcc[8]

    red_vec = jnp.concatenate([red[nm].reshape(-1) for nm, _ in RED])
    red_vec = jnp.pad(red_vec, (0, RED_ROWS * LANES - RED_TOTAL)).reshape(RED_ROWS, LANES)
    red_sum = sum8("sum_small", all_gather("ag_small", red_vec)).reshape(-1)
    gsm, off = {}, 0
    for nm, k in RED:
        gsm[nm] = red_sum[off:off + k]
        off += k
    gsm["b_mod"] = db_mod.reshape(-1)
    gsm["conv_b"] = _ff_unpad_cols(gsm["conv_b"].reshape(1, 2 * FF))
    gsm["conv_w"] = lax.dynamic_slice(gsm["conv_w"].reshape(3, 2 * FF), (0, me * FF_SHARD_PAD), (3, FF_SHARD_PAD))[:, :FF_SHARD]
    sm_names = [nm for nm, _ in SMALL]
    gs, ds, ms, vs = adamw("adamw_small", _pack_small([params[nm] for nm in sm_names]),
                           _pack_small([mom_m[nm] for nm in sm_names]), _pack_small([mom_v[nm] for nm in sm_names]),
                           [_pack_small([gsm[nm] for nm in sm_names])])
    sm_out = [_unpack_small(t) for t in (gs, ds, ms, vs)]

    dw_in, dwba, dwbb, dw_out, dw_up, dw_down = big_grads
    slabs = jnp.concatenate([t.reshape(N_DEV, -1) for t in (
        _full_to_cols(dw_in), _full_to_cols(dwba), _full_to_cols(dwbb), dw_out, _full_to_cols(dw_up),
        _ff_unpad_rows(dw_down))], axis=1)
    by_core = slabs.reshape(4, 2, BIG_ROWS, LANES)
    keep = lax.dynamic_index_in_dim(by_core, ac, axis=1, keepdims=False)
    give = lax.dynamic_index_in_dim(by_core, 1 - ac, axis=1, keepdims=False)
    got = exchange("rs_sibling", give.reshape(1, 4 * BIG_ROWS, LANES), to_chips=False).reshape(4, BIG_ROWS, LANES)
    pair = add2("rs_pair_sum", keep, got)
    outbox = jnp.stack([lax.dynamic_index_in_dim(pair, jnp.bitwise_xor(chip, m), axis=0, keepdims=False) for m in (1, 2, 3)])
    inbox = exchange("rs_chips", outbox.astype(MXU_DTYPE), to_chips=True)
    mine = lax.dynamic_index_in_dim(pair, chip, axis=0, keepdims=False)
    gb, db, mb, vb = adamw("adamw_big", _pack_big([shard(params, nm) for nm in big_names]),
                           _pack_big([shard(mom_m, nm) for nm in big_names]), _pack_big([shard(mom_v, nm) for nm in big_names]),
                           [mine, inbox[0], inbox[1], inbox[2]])
    big_out = [dict(zip(big_names, _unpack_big(t), strict=True)) for t in (gb, db, mb, vb)]
    for out in big_out:
        out["w_up"] = out["w_up"][:, :FF_SHARD]
    gm, dmo, mmo, vmo = adamw("adamw_mod", w_mod[0], m_w_mod[0], v_w_mod[0], [dw_mod])
    mod_out = (gm, dmo, mmo, vmo)

    order = ["c_ctx", "w_mod", "b_mod", "w_in", "b_in", "attn_sink", "q_norm_g", "k_norm_g", "w_branch_a", "w_branch_b",
             "w_out", "ln1_g", "ln1_b", "w_up", "conv_w", "conv_b", "w_down", "ln2_g", "ln2_b"]
    results = [loss, grad_x[None]]
    for kind in range(4):
        for nm in order:
            if nm == "w_mod":
                val = mod_out[kind]
            elif nm in big_out[kind]:
                val = big_out[kind][nm]
            else:
                val = sm_out[kind][nm]
            results.append(val.reshape(params[nm].shape))
    return tuple(results)
```

```python
import functools

import jax
import jax.numpy as jnp
import numpy as np
from jax import lax
from jax.experimental import pallas as pl
from jax.experimental.pallas import tpu as pltpu

F32 = jnp.float32
BF16 = jnp.bfloat16
MXU_DTYPE = BF16

AXES = ("x", "y", "c")
N_DEV = 8
D = 1024
HEAD_DIM = 64
N_HEADS = 8
N_KV = 2
GROUPS = 4
KV_W = GROUPS * HEAD_DIM
Q_W = N_HEADS * HEAD_DIM
GRID_W = 64
WIN = 128
ROPE_THETA = 10000.0
D_FF = 2816
FF_SHARD = 2 * D_FF // N_DEV
FF_SHARD_PAD = 768
FF = N_DEV // 2 * FF_SHARD_PAD
LN_EPS = 1e-5
QK_EPS = 1e-6
N_MOD = 6
ALPHA = 2.0 ** 0.25
Q_SCALE = HEAD_DIM ** -0.5
IN_COLS = 3584
OFF_KA, OFF_VA, OFF_QB, OFF_KB, OFF_VB, OFF_GA = 512, 640, 768, 1280, 1408, 1536
EXT_COLS = 6 * Q_W + 2 * D
X_QA, X_KA, X_VA, X_QB, X_KB, X_VB, X_GL = 0, 512, 1024, 1536, 2048, 2560, 3072
ADAM_LR, ADAM_B1, ADAM_B2, ADAM_EPS, ADAM_WD, ADAM_STEP = 0.001, 0.9, 0.999, 1e-08, 0.01, 10
LANES = 128
TM = 256
VMEM_LIMIT = 56 * 1024 * 1024
ELEMENTWISE_BLOCK_BYTES = 1 << 20
MM_TN_LHS_BYTES = 8 << 20
ELEMENTWISE_ROWS = (1824, 1408, 1024, 512, 256, 128, 64, 32, 16, 8)

ANY = pl.BlockSpec(memory_space=pl.ANY)
SDS = jax.ShapeDtypeStruct


def _pick(n, candidates):
    for t in candidates:
        if n % t == 0:
            return t
    raise ValueError(f"no tile for {n}")


def _full(a):
    nd = a.ndim
    return pl.BlockSpec(a.shape, lambda *_: (0,) * nd)


def _rows(tm, w, fn=lambda t: t):
    return pl.BlockSpec((tm, w), lambda i: (fn(i), 0))


def _dot(a, b):
    return jnp.dot(a.astype(MXU_DTYPE), b.astype(MXU_DTYPE), preferred_element_type=F32)


def _dot_nt(a, b):
    return lax.dot_general(a.astype(MXU_DTYPE), b.astype(MXU_DTYPE), (((1,), (1,)), ((), ())), preferred_element_type=F32)


def _dot_tn(a, b):
    return lax.dot_general(a.astype(MXU_DTYPE), b.astype(MXU_DTYPE), (((0,), (0,)), ((), ())), preferred_element_type=F32)


def _cparams(sem):
    return pltpu.CompilerParams(dimension_semantics=sem, vmem_limit_bytes=VMEM_LIMIT)


def all_gather(name, v):
    r, w = v.shape

    def body(x_ref, out_ref, send_sems, recv_sems, local_sem):
        x, y, c = (lax.axis_index(a) for a in AXES)
        me, sibling = (x, y, c), (x, y, 1 - c)
        chips = [(1 - x, y), (x, 1 - y), (1 - x, 1 - y)]

        def rows(px, py, pc):
            return out_ref.at[4 * px + 2 * py + pc]

        def copy(k, block, to, src=None):
            return pltpu.make_async_remote_copy(
                src_ref=rows(*block) if src is None else src, dst_ref=rows(*block),
                send_sem=send_sems.at[k], recv_sem=recv_sems.at[k],
                device_id=to, device_id_type=pl.DeviceIdType.MESH)

        mine = pltpu.make_async_copy(x_ref, rows(*me), local_sem)
        mine.start()
        first = [copy(0, me, sibling, src=x_ref)]
        first += [copy(1 + j, me, (*chip, c), src=x_ref) for j, chip in enumerate(chips)]
        for cp in first:
            cp.start()
        passed = [copy(4 + j, (*chip, c), sibling) for j, chip in enumerate(chips)]
        for j, chip in enumerate(chips):
            copy(1 + j, (*chip, c), me).wait_recv()
            passed[j].start()
        copy(0, sibling, me).wait_recv()
        for j, chip in enumerate(chips):
            copy(4 + j, (*chip, 1 - c), me).wait_recv()
        for cp in first + passed:
            cp.wait_send()
        mine.wait()

    return pl.pallas_call(
        body, name=name, out_shape=SDS((N_DEV, r, w), v.dtype), in_specs=[ANY], out_specs=ANY,
        scratch_shapes=[pltpu.SemaphoreType.DMA((7,)), pltpu.SemaphoreType.DMA((7,)), pltpu.SemaphoreType.DMA],
    )(v)


def exchange(name, outbox, to_chips):
    k = outbox.shape[0]
    assert k == (3 if to_chips else 1)

    def body(out_ref, in_ref, send_sems, recv_sems):
        x, y, c = (lax.axis_index(a) for a in AXES)
        peers = [(x, 1 - y, c), (1 - x, y, c), (1 - x, 1 - y, c)] if to_chips else [(x, y, 1 - c)]
        copies = [
            pltpu.make_async_remote_copy(
                src_ref=out_ref.at[m], dst_ref=in_ref.at[m], send_sem=send_sems.at[m], recv_sem=recv_sems.at[m],
                device_id=peer, device_id_type=pl.DeviceIdType.MESH)
            for m, peer in enumerate(peers)
        ]
        for cp in copies:
            cp.start()
        for cp in copies:
            cp.wait_recv()
        for cp in copies:
            cp.wait_send()

    return pl.pallas_call(
        body, name=name, out_shape=SDS(outbox.shape, outbox.dtype), in_specs=[ANY], out_specs=ANY,
        scratch_shapes=[pltpu.SemaphoreType.DMA((k,)), pltpu.SemaphoreType.DMA((k,))],
    )(outbox)


def rowwise(name, body, *, ntiles, tile_off=0, tiled, full, outs, accs=()):
    nt, nf, no = len(tiled), len(full), len(outs)

    def kern(*refs):
        i = pl.program_id(0)
        out_vals, incs = body(i + tile_off, refs[:nt], refs[nt:nt + nf])
        for r, v in zip(refs[nt + nf:nt + nf + no], out_vals, strict=True):
            r[...] = v.astype(r.dtype)
        acc_refs = refs[nt + nf + no:]

        @pl.when(i == 0)
        def _():
            for r in acc_refs:
                r[...] = jnp.zeros_like(r)

        for r, v in zip(acc_refs, incs, strict=True):
            r[...] += v

    res = pl.pallas_call(
        kern, name=name, grid=(ntiles,),
        in_specs=[s for _, s in tiled] + [_full(a) for a in full],
        out_specs=[s for _, _, s in outs] + [pl.BlockSpec(s, lambda i, n=len(s): (0,) * n) for s in accs],
        out_shape=[SDS(s, d) for s, d, _ in outs] + [SDS(s, F32) for s in accs],
        compiler_params=_cparams(("arbitrary",) if accs else ("parallel",)),
    )(*[a for a, _ in tiled], *full)
    return res[:no], res[no:]


def mm_tn(name, a, b, rows):
    ka, nb = a.shape[1], b.shape[1]
    tr = _pick(rows, [t for t in (2048, 1280, 1024, 768, 512, 256) if t * ka * a.dtype.itemsize <= MM_TN_LHS_BYTES])
    tn = _pick(nb, (1024, 512, 256, 128))

    def kern(a_ref, b_ref, o_ref):
        @pl.when(pl.program_id(1) == 0)
        def _():
            o_ref[...] = jnp.zeros_like(o_ref)

        o_ref[...] += _dot_tn(a_ref[...], b_ref[...])

    return pl.pallas_call(
        kern, name=name, grid=(nb // tn, rows // tr),
        in_specs=[pl.BlockSpec((tr, ka), lambda n, r: (r, 0)), pl.BlockSpec((tr, tn), lambda n, r: (r, n))],
        out_specs=pl.BlockSpec((ka, tn), lambda n, r: (0, n)), out_shape=SDS((ka, nb), F32),
        compiler_params=_cparams(("parallel", "arbitrary")),
    )(a, b)


def _swap16(t):
    w = t.shape[1]
    lane = lax.broadcasted_iota(jnp.int32, t.shape, 1)
    return jnp.where((lane & 16) == 0, pltpu.roll(t, w - 16, 1), pltpu.roll(t, 16, 1))


def _rope(t, cos, sin):
    return t * cos + _swap16(t) * sin


def _rope_t(d, cos, sin):
    return d * cos - _swap16(d) * sin


def _seg_sum64(a, bd_ref):
    bd = bd_ref[...]
    hi = a.astype(BF16)
    lo = (a - hi.astype(F32)).astype(BF16)
    return jnp.dot(hi, bd, preferred_element_type=F32) + jnp.dot(lo, bd, preferred_element_type=F32)


def _lane_block(shape):
    return jnp.right_shift(lax.broadcasted_iota(jnp.int32, shape, 1), 6)


def _stack_groups(t, dtype):
    blk = _lane_block(t.shape)
    return jnp.concatenate([jnp.where(blk == g, t, jnp.zeros_like(t)).astype(dtype) for g in range(GROUPS)], axis=0)


def _fold_groups(ts, tq):
    blk = _lane_block((tq, KV_W))
    out = jnp.zeros((tq, KV_W), ts.dtype)
    for g in range(GROUPS):
        out = jnp.where(blk == g, ts[g * tq:(g + 1) * tq], out)
    return out


def _row_to_col(row):
    hi = row.astype(BF16)
    r1 = row - hi.astype(F32)
    mid = r1.astype(BF16)
    lo = (r1 - mid.astype(F32)).astype(BF16)
    ones = jnp.ones((8, LANES), BF16)
    pad = jnp.zeros((7, row.shape[1]), BF16)
    acc = jnp.zeros((row.shape[1], LANES), F32)
    for term in (hi, mid, lo):
        acc = acc + lax.dot_general(jnp.concatenate([term, pad], axis=0), ones, (((0,), (0,)), ((), ())),
                                    preferred_element_type=F32)
    return acc[:, 0:1]


def _stack_tiles(t, dtype):
    return jnp.concatenate([_stack_groups(t[a:a + TM], dtype) for a in range(0, t.shape[0], TM)], axis=0)


def _fold_tiles(ts, tq):
    return jnp.concatenate([_fold_groups(ts[GROUPS * a:GROUPS * (a + TM)], TM) for a in range(0, tq, TM)], axis=0)


def _compact_tiles_t(tt, dtype):
    return jnp.concatenate([tt[g * HEAD_DIM:(g + 1) * HEAD_DIM, a:a + TM] for a in range(0, tt.shape[1], TM)
                            for g in range(GROUPS)], axis=1).astype(dtype)


def _layer_norm_bwd(dxh, xhat, rstd):
    m1 = jnp.mean(dxh, axis=1, keepdims=True)
    m2 = jnp.mean(dxh * xhat, axis=1, keepdims=True)
    return rstd * (dxh - m1 - xhat * m2)


def _colsum(a):
    return jnp.sum(a, axis=0, keepdims=True)


def _shifted_rows(t, prev_row, next_row):
    n = t.shape[0]
    row = lax.broadcasted_iota(jnp.int32, t.shape, 0)
    up = jnp.where(row == 0, prev_row, pltpu.roll(t, 1, 0))
    dn = jnp.where(row == n - 1, next_row, pltpu.roll(t, n - 1, 0))
    return up, dn


def mod_fwd(cs, w_sh, b_sh):
    def kern(c_ref, w_ref, b_ref, o_ref):
        o_ref[...] = _dot(jax.nn.silu(c_ref[...]), w_ref[...]) + b_ref[...]

    return pl.pallas_call(kern, name="mod_fwd", out_shape=SDS((16, w_sh.shape[1]), F32),
                          compiler_params=pltpu.CompilerParams(vmem_limit_bytes=VMEM_LIMIT))(cs, w_sh, b_sh)


def mod_bwd(cs, w_sh, dm_sh, dm_all):
    hp = lax.Precision.HIGHEST

    def kern(c_ref, w_ref, dm_ref, da_ref, dw_ref, dc_ref, db_ref):
        c = c_ref[...]
        sg = jax.nn.sigmoid(c)
        sc = c * sg
        dm = dm_ref[...]
        dmc = dm_ref[8:9, :]
        for i in range(9, 16):
            dmc = dmc + dm_ref[i:i + 1, :]
        row = lax.broadcasted_iota(jnp.int32, dm.shape, 0)
        a = jnp.where(row < 8, dm, jnp.where(row == 8, dmc, 0.0))
        dw_ref[...] = lax.dot_general(sc, a, (((0,), (0,)), ((), ())), precision=hp, preferred_element_type=F32)
        dsc = lax.dot_general(a, w_ref[...], (((1,), (1,)), ((), ())), precision=hp, preferred_element_type=F32)
        dc_ref[...] = dsc * (sg * (1.0 + c * (1.0 - sg)))
        db = da_ref[0:1, :]
        for i in range(1, 16):
            db = db + da_ref[i:i + 1, :]
        db_ref[...] = db

    return pl.pallas_call(
        kern, name="mod_bwd",
        out_shape=[SDS(w_sh.shape, F32), SDS((16, D), F32), SDS((1, dm_all.shape[1]), F32)],
        compiler_params=pltpu.CompilerParams(vmem_limit_bytes=VMEM_LIMIT))(cs, w_sh, dm_sh, dm_all)


M_SHIFT1, M_SCALE1, M_SHIFTC, M_SCALEC, M_GATE1, M_SHIFT2, M_SCALE2, M_GATE2 = range(8)


def _mrow(ref, k):
    return ref[k:k + 1, :]


def inproj_fwd(xa, cos, sin, modrows, w_ext, b_ext, qg, kg, bd, n_lat_tiles):
    n = xa.shape[0]

    def body(t, vals, fr):
        x, cs, sn = (v[...] for v in vals)
        mod, w, b, qg_r, kg_r, bd_r = fr
        is_ctx = t >= n_lat_tiles
        shift = jnp.where(is_ctx, _mrow(mod, M_SHIFTC), _mrow(mod, M_SHIFT1))
        scale = jnp.where(is_ctx, _mrow(mod, M_SCALEC), _mrow(mod, M_SCALE1))
        hb = (x * (1.0 + scale) + shift).astype(MXU_DTYPE)
        proj = jnp.dot(hb, w[...], preferred_element_type=F32) + b[...]
        cos4 = jnp.concatenate([cs] * 4, axis=1)
        sin4 = jnp.concatenate([sn] * 4, axis=1)
        qa = _rope(proj[:, X_QA:X_QA + Q_W], cos4, sin4) * Q_SCALE
        ka = _rope(proj[:, X_KA:X_KA + Q_W], cos4, sin4)
        va = proj[:, X_VA:X_VA + Q_W]
        tq = proj[:, X_QB:X_QB + Q_W]
        rq = lax.rsqrt(_seg_sum64(tq * tq, bd_r) * (1.0 / HEAD_DIM) + QK_EPS)
        qb = _rope(tq * rq * qg_r[...], cos4, sin4) * Q_SCALE
        tk = proj[:, X_KB:X_KB + Q_W]
        rk = lax.rsqrt(_seg_sum64(tk * tk, bd_r) * (1.0 / HEAD_DIM) + QK_EPS)
        kb = _rope(tk * rk * kg_r[...], cos4, sin4)
        vb = proj[:, X_VB:X_VB + Q_W]
        gl = proj[:, X_GL:]
        return [hb, qa, ka, va, qb, kb, vb, tq, rq, tk, rk, gl], []

    mx = MXU_DTYPE
    outs = [((n, D), mx, _rows(TM, D))] + [((n, Q_W), mx, _rows(TM, Q_W))] * 6 + \
           [((n, Q_W), F32, _rows(TM, Q_W))] * 4 + [((n, 2 * D), F32, _rows(TM, 2 * D))]
    res, _ = rowwise("inproj_fwd", body, ntiles=n // TM,
                     tiled=[(xa, _rows(TM, D)), (cos, _rows(TM, LANES)), (sin, _rows(TM, LANES))],
                     full=[modrows, w_ext, b_ext, qg, kg, bd], outs=outs)
    return res


def merge_fwd(oa, ob, gl, x, modrows, wba, wbb, w_out, s_rows):
    def body(t, vals, fr):
        oa_, ob_, gl_, x_ = (v[...] for v in vals)
        mod, wa, wb, wo = fr
        ya = _dot(oa_, wa[...])
        yb = _dot(ob_, wb[...])
        ga = jax.nn.sigmoid(gl_[:, :D])
        gb = jax.nn.sigmoid(gl_[:, D:])
        mrg = ga * ya + gb * yb
        y = _dot(mrg, wo[...])
        r1 = ALPHA * x_ + _mrow(mod, M_GATE1) * y
        mu = jnp.mean(r1, axis=1, keepdims=True)
        xc = r1 - mu
        var = jnp.mean(xc * xc, axis=1, keepdims=True)
        rstd = lax.rsqrt(var + LN_EPS)
        xhat = xc * rstd
        return [ya, yb, mrg, y, xhat, rstd], []

    outs = [((s_rows, D), F32, _rows(TM, D))] * 2 + [((s_rows, D), MXU_DTYPE, _rows(TM, D))] + \
           [((s_rows, D), F32, _rows(TM, D))] * 2 + [((s_rows, 1), F32, _rows(TM, 1))]
    res, _ = rowwise("merge_fwd", body, ntiles=s_rows // TM,
                     tiled=[(oa, _rows(TM, Q_W)), (ob, _rows(TM, Q_W)), (gl, _rows(TM, 2 * D)), (x, _rows(TM, D))],
                     full=[modrows, wba, wbb, w_out], outs=outs)
    return res


def ffn_up_fwd(xhat1, modrows, ln_g, ln_b, w_up, s_rows):
    def body(t, vals, fr):
        xh = vals[0][...]
        mod, g_r, b_r, w = fr
        x1 = xh * g_r[...] + b_r[...]
        h2 = (x1 * (1.0 + _mrow(mod, M_SCALE2)) + _mrow(mod, M_SHIFT2)).astype(MXU_DTYPE)
        return [h2, jnp.dot(h2, w[...], preferred_element_type=F32)], []

    res, _ = rowwise("ffn_up_fwd", body, ntiles=s_rows // TM, tiled=[(xhat1, _rows(TM, D))],
                     full=[modrows, ln_g, ln_b, w_up],
                     outs=[((s_rows, D), MXU_DTYPE, _rows(TM, D)), ((s_rows, 2 * FF), F32, _rows(TM, 2 * FF))])
    return res


TC = 128


def _halo_specs(tm, w, s_rows):
    per = tm // 8
    last = s_rows // 8 - 1
    return (pl.BlockSpec((8, w), lambda i: (jnp.maximum(i * per - 1, 0), 0)),
            pl.BlockSpec((8, w), lambda i: (jnp.minimum((i + 1) * per, last), 0)))


def _halo_rows(t, ntiles, prev_ref, next_ref):
    prev_row = jnp.where(t == 0, 0.0, prev_ref[7:8, :].astype(F32))
    next_row = jnp.where(t == ntiles - 1, 0.0, next_ref[0:1, :].astype(F32))
    return prev_row, next_row


def conv_swiglu_fwd(u0, conv_w8, conv_b, s_rows):
    w2 = 2 * FF
    tc = _pick(s_rows, (2 * TC, TC))
    nt = s_rows // tc

    def body(t, vals, fr):
        u_ref, pv, nx = vals
        cw, cb = fr
        u = u_ref[...]
        up, dn = _shifted_rows(u, *_halo_rows(t, nt, pv, nx))
        uc = cw[0:1, :] * up + cw[1:2, :] * u + cw[2:3, :] * dn + cb[...]
        gate, val = uc[:, :FF], uc[:, FF:]
        return [gate * jax.nn.sigmoid(gate) * val], []

    hp, hn = _halo_specs(tc, w2, s_rows)
    res, _ = rowwise("conv_swiglu_fwd", body, ntiles=nt,
                     tiled=[(u0, _rows(tc, w2)), (u0, hp), (u0, hn)], full=[conv_w8, conv_b],
                     outs=[((s_rows, FF), MXU_DTYPE, _rows(tc, FF))])
    return res[0]


def ffn_down_loss(a, xhat1, target, modrows, ln1_g, ln1_b, ln2_g, ln2_b, w_down, s_rows):
    def body(t, vals, fr):
        a_, xh1, tgt = (v[...] for v in vals)
        mod, g1, b1, g2, b2, wd = fr
        y2 = jnp.dot(a_, wd[...], preferred_element_type=F32)
        x1 = xh1 * g1[...] + b1[...]
        gate2 = _mrow(mod, M_GATE2)
        r2 = ALPHA * x1 + gate2 * y2
        mu = jnp.mean(r2, axis=1, keepdims=True)
        xc = r2 - mu
        var = jnp.mean(xc * xc, axis=1, keepdims=True)
        rstd = lax.rsqrt(var + LN_EPS)
        xhat = xc * rstd
        out = xhat * g2[...] + b2[...]
        diff = out - tgt
        loss = 0.5 * jnp.sum(jnp.mean(diff * diff, axis=1, keepdims=True), axis=0, keepdims=True)
        dout = diff * (1.0 / D)
        dr2 = _layer_norm_bwd(dout * g2[...], xhat, rstd)
        incs = [loss, _colsum(dout * xhat), _colsum(dout), _colsum(dr2 * y2)]
        return [dr2, dr2 * gate2], incs

    res, accs = rowwise("ffn_down_loss", body, ntiles=s_rows // TM,
                        tiled=[(a, _rows(TM, FF)), (xhat1, _rows(TM, D)), (target, _rows(TM, D))],
                        full=[modrows, ln1_g, ln1_b, ln2_g, ln2_b, w_down],
                        outs=[((s_rows, D), F32, _rows(TM, D)), ((s_rows, D), MXU_DTYPE, _rows(TM, D))],
                        accs=[(1, 1), (1, D), (1, D), (1, D)])
    return res, accs


def ffn_down_bwd(dy2, w_down_t, s_rows):
    def body(t, vals, fr):
        return [jnp.dot(vals[0][...], fr[0][...], preferred_element_type=F32)], []

    res, _ = rowwise("ffn_down_bwd", body, ntiles=s_rows // TM, tiled=[(dy2, _rows(TM, D))], full=[w_down_t],
                     outs=[((s_rows, FF), F32, _rows(TM, FF))])
    return res[0]


def swiglu_conv_bwd(u0, da, conv_w8, conv_b, s_rows):
    w2 = 2 * FF
    nt = s_rows // TC
    n = TC + 16

    def body(t, vals, fr):
        u_ref, upv, unx, da_ref, apv, anx = vals
        cw, cb = fr
        first, last = t == 0, t == nt - 1
        ue = jnp.concatenate([jnp.where(first, 0.0, upv[...]), u_ref[...], jnp.where(last, 0.0, unx[...])], axis=0)
        ae = jnp.concatenate([jnp.where(first, 0.0, apv[...]), da_ref[...], jnp.where(last, 0.0, anx[...])], axis=0)
        up = pltpu.roll(ue, 1, 0)
        dn = pltpu.roll(ue, n - 1, 0)
        uc = cw[0:1, :] * up + cw[1:2, :] * ue + cw[2:3, :] * dn + cb[...]
        gate, val = uc[:, :FF], uc[:, FF:]
        sg = jax.nn.sigmoid(gate)
        du = jnp.concatenate([ae * val * (sg * (1.0 + gate * (1.0 - sg))), ae * (gate * sg)], axis=1)
        du0 = cw[0:1, :] * pltpu.roll(du, n - 1, 0) + cw[1:2, :] * du + cw[2:3, :] * pltpu.roll(du, 1, 0)
        rows = slice(8, 8 + TC)
        dut = du[rows]
        return [du0[rows]], [_colsum(dut), _colsum(up[rows] * dut), _colsum(ue[rows] * dut), _colsum(dn[rows] * dut)]

    hp, hn = _halo_specs(TC, w2, s_rows)
    ap, an = _halo_specs(TC, FF, s_rows)
    res, accs = rowwise("swiglu_conv_bwd", body, ntiles=nt,
                        tiled=[(u0, _rows(TC, w2)), (u0, hp), (u0, hn), (da, _rows(TC, FF)), (da, ap), (da, an)],
                        full=[conv_w8, conv_b], outs=[((s_rows, w2), MXU_DTYPE, _rows(TC, w2))], accs=[(1, w2)] * 4)
    return res[0], accs


def ffn_up_ln1_bwd(du0, dr2, xhat1, y, rstd1, modrows, ln_g, ln_b, w_up_t, s_rows):
    def body(t, vals, fr):
        du0_, dr2_, xh, y_, rstd = (v[...] for v in vals)
        mod, g_r, b_r, wt = fr
        dh2 = jnp.dot(du0_, wt[...], preferred_element_type=F32)
        x1 = xh * g_r[...] + b_r[...]
        dx1 = ALPHA * dr2_ + dh2 * (1.0 + _mrow(mod, M_SCALE2))
        dr1 = _layer_norm_bwd(dx1 * g_r[...], xh, rstd)
        incs = [_colsum(dh2 * x1), _colsum(dh2), _colsum(dx1 * xh), _colsum(dx1), _colsum(dr1 * y_)]
        return [dr1 * _mrow(mod, M_GATE1), ALPHA * dr1], incs

    res, accs = rowwise("ffn_up_ln1_bwd", body, ntiles=s_rows // TM,
                        tiled=[(du0, _rows(TM, 2 * FF)), (dr2, _rows(TM, D)), (xhat1, _rows(TM, D)), (y, _rows(TM, D)),
                               (rstd1, _rows(TM, 1))],
                        full=[modrows, ln_g, ln_b, w_up_t],
                        outs=[((s_rows, D), MXU_DTYPE, _rows(TM, D)), ((s_rows, D), F32, _rows(TM, D))],
                        accs=[(1, D)] * 5)
    return res, accs


def merge_bwd(dy, ya, yb, gl, w_out_t, wba_t, wbb_t, s_rows):
    def body(t, vals, fr):
        dy_, ya_, yb_, gl_ = (v[...] for v in vals)
        wot, wat, wbt = fr
        dmrg = jnp.dot(dy_, wot[...], preferred_element_type=F32)
        ga = jax.nn.sigmoid(gl_[:, :D])
        gb = jax.nn.sigmoid(gl_[:, D:])
        dya = dmrg * ga
        dyb = dmrg * gb
        dgl = jnp.concatenate([dmrg * ya_ * ga * (1.0 - ga), dmrg * yb_ * gb * (1.0 - gb)], axis=1)
        return [dya, dyb, dgl, _dot(dya, wat[...]), _dot(dyb, wbt[...])], []

    mx = MXU_DTYPE
    res, _ = rowwise("merge_bwd", body, ntiles=s_rows // TM,
                     tiled=[(dy, _rows(TM, D)), (ya, _rows(TM, D)), (yb, _rows(TM, D)), (gl, _rows(TM, 2 * D))],
                     full=[w_out_t, wba_t, wbb_t],
                     outs=[((s_rows, D), mx, _rows(TM, D))] * 2 + [((s_rows, 2 * D), F32, _rows(TM, 2 * D))] +
                          [((s_rows, Q_W), F32, _rows(TM, Q_W))] * 2)
    return res


def qk_bwd(dqa, dka_t, dva_t, dqb_heads, dkb_t, dvb_t, dgl, tq, rq, tk, rk, cos, sin, qg, kg, bd, place, n_lat_tiles, n):
    def placed(xt_ref, place_ref):
        xt = xt_ref[...]
        hi = xt.astype(BF16)
        r1 = xt - hi.astype(F32)
        mid = r1.astype(BF16)
        lo = (r1 - mid.astype(F32)).astype(BF16)
        pm = place_ref[...]
        return sum(lax.dot_general(term, pm, (((0,), (0,)), ((), ())), preferred_element_type=F32) for term in (hi, mid, lo))

    def body(t, vals, fr):
        dqa_, dgl_, tq_, rq_, tk_, rk_, cs, sn = (v[...] for v in vals[:8])
        qg_r, kg_r, bd_r, pl_r = fr
        dka_, dva_, dkb_, dvb_ = (placed(v, pl_r) for v in vals[8:12])
        dqb_ = jnp.concatenate([v[...] for v in vals[12:]], axis=1)
        is_ctx = t >= n_lat_tiles
        cos4 = jnp.concatenate([cs] * 4, axis=1)
        sin4 = jnp.concatenate([sn] * 4, axis=1)
        zero = jnp.zeros_like(dqa_)
        dpqa = jnp.where(is_ctx, zero, _rope_t(dqa_, cos4, sin4) * Q_SCALE)
        dpka = _rope_t(dka_, cos4, sin4)
        dpva = dva_
        dnq = jnp.where(is_ctx, zero, _rope_t(dqb_, cos4, sin4) * Q_SCALE)
        gq = qg_r[...] * dnq
        dtq = rq_ * gq - tq_ * (rq_ * rq_ * rq_) * (_seg_sum64(gq * tq_, bd_r) * (1.0 / HEAD_DIM))
        dnk = _rope_t(dkb_, cos4, sin4)
        gk = kg_r[...] * dnk
        dtk = rk_ * gk - tk_ * (rk_ * rk_ * rk_) * (_seg_sum64(gk * tk_, bd_r) * (1.0 / HEAD_DIM))
        dgl32 = jnp.where(is_ctx, jnp.zeros_like(dgl_), dgl_)
        dproj = jnp.concatenate([dpqa, dpka, dpva, dtq, dtk, dvb_, dgl32], axis=1)
        return [dproj], [_colsum(dproj), _colsum(dnq * tq_ * rq_), _colsum(dnk * tk_ * rk_)]

    lat = lambda t: jnp.minimum(t, n_lat_tiles - 1)
    qs = _rows(TM, Q_W)
    ts = pl.BlockSpec((N_KV * HEAD_DIM, TM), lambda i: (0, i))
    res, accs = rowwise(
        "qk_bwd", body, ntiles=n // TM,
        tiled=[(dqa, _rows(TM, Q_W, lat)), (dgl, _rows(TM, 2 * D, lat)),
               (tq, qs), (rq, qs), (tk, qs), (rk, qs), (cos, _rows(TM, LANES)), (sin, _rows(TM, LANES)),
               (dka_t, ts), (dva_t, ts), (dkb_t, ts), (dvb_t, ts)] + [(d, _rows(TM, KV_W, lat)) for d in dqb_heads],
        full=[qg, kg, bd, place], outs=[((n, EXT_COLS), MXU_DTYPE, _rows(TM, EXT_COLS))],
        accs=[(1, EXT_COLS), (1, Q_W), (1, Q_W)])
    return res[0], accs


def inproj_bwd(name, dproj, xa, dxp, modrows, w_ext_t, *, ntiles, tile_off, is_ctx, out_rows):
    kc = M_SCALEC if is_ctx else M_SCALE1

    def body(t, vals, fr):
        dp, x_ = vals[0][...], vals[1][...]
        mod, wt = fr
        dh = jnp.dot(dp, wt[...], preferred_element_type=F32)
        incs = [_colsum(dh * x_), _colsum(dh)]
        if is_ctx:
            return [], incs
        return [vals[2][...] + dh * (1.0 + _mrow(mod, kc))], incs

    tiled = [(dproj, _rows(TM, EXT_COLS, lambda i: i + tile_off)), (xa, _rows(TM, D, lambda i: i + tile_off))]
    outs = []
    if not is_ctx:
        tiled.append((dxp, _rows(TM, D)))
        outs = [((out_rows, D), F32, _rows(TM, D))]
    return rowwise(name, body, ntiles=ntiles, tiled=tiled, full=[modrows, w_ext_t], outs=outs, accs=[(1, D)] * 2)


def _attn_semantics():
    return _cparams(("arbitrary", "arbitrary", "arbitrary"))


GLOB_TK = (1280, 1024, 768, 512, 256)
GLOB_TQ = (512, 256)
GLOB_BWD_TQ = GLOB_TQ
KEY_CHUNK = 256


def glob_fwd(q, kt, v_t, s_rows):
    n = kt.shape[0]
    tq = _pick(s_rows, GLOB_TQ)
    tk = _pick(n, GLOB_TK)
    nq, nk = s_rows // tq, n // tk
    r = GROUPS * tq
    nch = tk // KEY_CHUNK

    def produce(qs, k_ref, s_buf, c, mx):
        rows = slice(c * KEY_CHUNK, (c + 1) * KEY_CHUNK)
        sn = _dot_nt(k_ref[rows, :], qs[...])
        s_buf[rows, :] = sn
        return jnp.maximum(mx, jnp.max(sn, axis=0, keepdims=True))

    def kern(q_ref, k0_ref, kn_ref, vt_ref, o_ref, lse_ref, qs, s_buf, mx_buf, m_s, l_s, acc):
        j = pl.program_id(2)

        @pl.when(j == 0)
        def _():
            qs[...] = _stack_tiles(q_ref[...], qs.dtype)
            mx = jnp.full((1, r), -jnp.inf, F32)
            for c in range(nch):
                mx = produce(qs, k0_ref, s_buf, c, mx)
            mx_buf[...] = mx
            m_s[...] = jnp.full_like(m_s, -jnp.inf)
            l_s[...] = jnp.zeros_like(l_s)
            acc[...] = jnp.zeros_like(acc)

        m_prev = m_s[...]
        m_new = jnp.maximum(m_prev, mx_buf[...])
        alpha = jnp.exp(m_prev - m_new)
        a = alpha * acc[...]
        ls = alpha * l_s[...]
        mx = jnp.full((1, r), -jnp.inf, F32)
        for c in range(nch):
            rows = slice(c * KEY_CHUNK, (c + 1) * KEY_CHUNK)
            p = jnp.exp(s_buf[rows, :] - m_new)
            ls = ls + jnp.sum(p, axis=0, keepdims=True)
            a = a + jnp.dot(vt_ref[0, :, rows], p.astype(MXU_DTYPE), preferred_element_type=F32)
            mx = produce(qs, kn_ref, s_buf, c, mx)
        mx_buf[...] = mx
        l_s[...] = ls
        acc[...] = a
        m_s[...] = m_new

        @pl.when(j == nk - 1)
        def _():
            o_t = acc[...] / l_s[...]
            o_ref[...] = jnp.concatenate([_untranspose_groups(o_t[:, GROUPS * a:GROUPS * (a + TM)], TM)
                                          for a in range(0, tq, TM)], axis=0)
            lse_ref[0, 0] = _row_to_col(m_s[...] + jnp.log(l_s[...]))

    kspec = lambda f: pl.BlockSpec((tk, KV_W), lambda h, i, j: (f(j), h))
    return pl.pallas_call(
        kern, name="glob_fwd", grid=(N_KV, nq, nk),
        in_specs=[pl.BlockSpec((tq, KV_W), lambda h, i, j: (i, h)), kspec(lambda j: 0),
                  kspec(lambda j: jnp.minimum(j + 1, nk - 1)), pl.BlockSpec((1, HEAD_DIM, tk), lambda h, i, j: (h, 0, j))],
        out_specs=[pl.BlockSpec((tq, KV_W), lambda h, i, j: (i, h)),
                   pl.BlockSpec((1, 1, r, 1), lambda h, i, j: (h, i, 0, 0))],
        out_shape=[SDS((s_rows, Q_W), F32), SDS((N_KV, nq, r, 1), F32)],
        scratch_shapes=[pltpu.VMEM((r, KV_W), MXU_DTYPE), pltpu.VMEM((tk, r), F32), pltpu.VMEM((1, r), F32),
                        pltpu.VMEM((1, r), F32), pltpu.VMEM((1, r), F32), pltpu.VMEM((HEAD_DIM, r), F32)],
        compiler_params=_attn_semantics(),
    )(q, kt, kt, v_t)


def attn_delta(o, do, s_rows):
    tq = _pick(s_rows, GLOB_BWD_TQ)
    nq = s_rows // tq
    r = GROUPS * tq

    def kern(o_ref, do_ref, d_ref):
        d_ref[0, 0] = jnp.sum(_stack_tiles(do_ref[...] * o_ref[...], F32), axis=1, keepdims=True)

    qspec = pl.BlockSpec((tq, KV_W), lambda h, i: (i, h))
    return pl.pallas_call(
        kern, name="attn_delta", grid=(N_KV, nq), in_specs=[qspec, qspec],
        out_specs=pl.BlockSpec((1, 1, r, 1), lambda h, i: (h, i, 0, 0)), out_shape=SDS((N_KV, nq, r, 1), F32),
        compiler_params=_cparams(("parallel", "parallel")),
    )(o, do)


def _compact_t(tt, dtype):
    return jnp.concatenate([tt[g * HEAD_DIM:(g + 1) * HEAD_DIM, :] for g in range(GROUPS)], axis=1).astype(dtype)


def glob_bwd(q, q_t, kt, vt, do, do_t, lse, delta, h, s_rows):
    n = kt.shape[0]
    tq = _pick(s_rows, GLOB_BWD_TQ)
    tk = _pick(n, GLOB_TK)
    nq, nk = s_rows // tq, n // tk
    r = GROUPS * tq
    nch = tk // KEY_CHUNK

    def kern(q_ref, qt_ref, k_ref, v_ref, do_ref, dot_ref, lse_ref, dl_ref, dq_ref, dkt_ref, dvt_ref, p_buf, ds_buf):
        j = pl.program_id(0)
        i = pl.program_id(1)

        @pl.when(i == 0)
        def _():
            dkt_ref[...] = jnp.zeros_like(dkt_ref)
            dvt_ref[...] = jnp.zeros_like(dvt_ref)

        qs = _stack_tiles(q_ref[...], MXU_DTYPE)
        dos = _stack_tiles(do_ref[...], MXU_DTYPE)
        lse_b = jnp.broadcast_to(lse_ref[0, 0], (r, LANES))
        dl_b = jnp.broadcast_to(dl_ref[0, 0], (r, LANES))
        for c in range(nch):
            lo = c * KEY_CHUNK
            sc = _dot_nt(qs, k_ref[lo:lo + KEY_CHUNK, :])
            dpc = _dot_nt(dos, v_ref[lo:lo + KEY_CHUNK, :])
            for t in range(KEY_CHUNK // LANES):
                sl = slice(t * LANES, (t + 1) * LANES)
                pt = jnp.exp(sc[:, sl] - lse_b)
                p_buf[:, lo + t * LANES:lo + (t + 1) * LANES] = pt.astype(p_buf.dtype)
                ds_buf[:, lo + t * LANES:lo + (t + 1) * LANES] = (pt * (dpc[:, sl] - dl_b)).astype(ds_buf.dtype)
        dq_t = _fold_tiles(jnp.dot(ds_buf[...], k_ref[...], preferred_element_type=F32), tq)
        rows = pl.ds(pl.multiple_of(i * tq, tq), tq)

        @pl.when(j == 0)
        def _():
            dq_ref[rows, :] = dq_t

        @pl.when(j > 0)
        def _():
            dq_ref[rows, :] += dq_t

        dvt_ref[...] += jnp.dot(_compact_tiles_t(dot_ref[...], MXU_DTYPE), p_buf[...], preferred_element_type=F32)
        dkt_ref[...] += jnp.dot(_compact_tiles_t(qt_ref[...], MXU_DTYPE), ds_buf[...], preferred_element_type=F32)

    col = pl.BlockSpec((1, 1, r, 1), lambda j, i: (h, i, 0, 0))
    qspec = pl.BlockSpec((tq, KV_W), lambda j, i: (i, h))
    tspec = pl.BlockSpec((KV_W, tq), lambda j, i: (h, i))
    kspec = pl.BlockSpec((tk, KV_W), lambda j, i: (j, h))
    ospec = pl.BlockSpec((HEAD_DIM, tk), lambda j, i: (0, j))
    return pl.pallas_call(
        kern, name=f"glob_bwd_h{h}", grid=(nk, nq),
        in_specs=[qspec, tspec, kspec, kspec, qspec, tspec, col, col],
        out_specs=[pl.BlockSpec(memory_space=pltpu.VMEM), ospec, ospec],
        out_shape=[SDS((s_rows, KV_W), F32), SDS((HEAD_DIM, n), F32), SDS((HEAD_DIM, n), F32)],
        scratch_shapes=[pltpu.VMEM((r, tk), MXU_DTYPE), pltpu.VMEM((r, tk), MXU_DTYPE)],
        compiler_params=_cparams(("arbitrary", "arbitrary")),
    )(q, q_t, kt, vt, do, do_t, lse, delta)


TW = 2 * WIN
WR = GROUPS * TW
WLAT = 4 * WIN


def _win_cat(dst, parts):
    off = 0
    for p in parts:
        dst[off:off + p.shape[0], :] = p[...]
        off += p.shape[0]


def _win_specs(s_rows, c_rows):
    nb = s_rows // WIN
    prev = lambda i: jnp.maximum(2 * i - 1, 0)
    nxt = lambda i: jnp.minimum(2 * i + 2, nb - 1)
    rows = [pl.BlockSpec((WIN, KV_W), lambda h, i: (prev(i), h)), pl.BlockSpec((TW, KV_W), lambda h, i: (i, h)),
            pl.BlockSpec((WIN, KV_W), lambda h, i: (nxt(i), h)), pl.BlockSpec((c_rows, KV_W), lambda h, i: (s_rows // c_rows, h))]
    cols = [pl.BlockSpec((1, HEAD_DIM, WIN), lambda h, i: (h, 0, prev(i))), pl.BlockSpec((1, HEAD_DIM, TW), lambda h, i: (h, 0, i)),
            pl.BlockSpec((1, HEAD_DIM, WIN), lambda h, i: (h, 0, nxt(i))),
            pl.BlockSpec((1, HEAD_DIM, c_rows), lambda h, i: (h, 0, s_rows // c_rows))]
    return rows, cols


def _win_mask(i, s_rows, shape, keys_on_rows):
    a = lax.broadcasted_iota(jnp.int32, shape, 0)
    b = lax.broadcasted_iota(jnp.int32, shape, 1)
    kk, qq = (a, b) if keys_on_rows else (b, a)
    qpos = i * TW + (qq & (TW - 1))
    kpos = (2 * i - 1) * WIN + kk
    band = (jnp.abs(qpos - kpos) <= WIN) & (kpos >= 0) & (kpos < s_rows)
    return (kk >= WLAT) | band


def _untranspose_groups(o_t, tq):
    row = lax.broadcasted_iota(jnp.int32, (HEAD_DIM, KV_W), 0)
    col = lax.broadcasted_iota(jnp.int32, (HEAD_DIM, KV_W), 1)
    hi = o_t.astype(BF16)
    r1 = o_t - hi.astype(F32)
    mid = r1.astype(BF16)
    lo = (r1 - mid.astype(F32)).astype(BF16)
    o = jnp.zeros((tq, KV_W), F32)
    for g in range(GROUPS):
        sel = jnp.where(col == row + g * HEAD_DIM, 1.0, 0.0).astype(BF16)
        for term in (hi, mid, lo):
            o = o + lax.dot_general(term[:, g * tq:(g + 1) * tq], sel, (((0,), (0,)), ((), ())), preferred_element_type=F32)
    return o


def win_fwd(q, kt, v_t, sinkrow, s_rows, c_rows):
    nt = s_rows // TW
    nkeys = WLAT + c_rows

    def kern(q_ref, kp, kc, kn, kx, vp, vc, vn, vx, sink_ref, o_ref, lse_ref, kcat):
        i = pl.program_id(1)
        _win_cat(kcat, (kp, kc, kn, kx))
        qs = _stack_groups(q_ref[...], MXU_DTYPE)
        st = _dot_nt(kcat[...], qs)
        st = jnp.where(_win_mask(i, s_rows, st.shape, True), st, -jnp.inf)
        sink = sink_ref[0]
        m = jnp.maximum(jnp.max(st, axis=0, keepdims=True), sink)
        e = jnp.exp(st - m)
        den = jnp.sum(e, axis=0, keepdims=True) + jnp.exp(sink - m)
        v_cat = jnp.concatenate([vp[0], vc[0], vn[0], vx[0]], axis=1)
        o_t = jnp.dot(v_cat, e.astype(MXU_DTYPE), preferred_element_type=F32) / den
        o_ref[...] = _untranspose_groups(o_t, TW)
        lse_ref[0, 0] = _row_to_col(m + jnp.log(den))

    rows, cols = _win_specs(s_rows, c_rows)
    qspec = pl.BlockSpec((TW, KV_W), lambda h, i: (i, h))
    rowv = pl.BlockSpec((1, 1, WR, 1), lambda h, i: (h, i, 0, 0))
    return pl.pallas_call(
        kern, name="win_fwd", grid=(N_KV, nt),
        in_specs=[qspec] + rows + cols + [pl.BlockSpec((1, 1, WR), lambda h, i: (h, 0, 0))],
        out_specs=[qspec, rowv], out_shape=[SDS((s_rows, Q_W), F32), SDS((N_KV, nt, WR, 1), F32)],
        scratch_shapes=[pltpu.VMEM((nkeys, KV_W), MXU_DTYPE)],
        compiler_params=_cparams(("parallel", "parallel")),
    )(q, kt, kt, kt, kt, v_t, v_t, v_t, v_t, sinkrow)


def win_bwd(q, q_t, kt, vt, sinkcol, o, do, do_t, lse, s_rows, c_rows):
    nt = s_rows // TW
    nkeys = WLAT + c_rows
    n = s_rows + c_rows
    ctx0 = WIN + s_rows

    def kern(q_ref, qt_ref, kp, kc, kn, kx, vp, vc, vn, vx, sink_ref, o_ref, do_ref, dot_ref, lse_ref,
             dq_ref, dkt_ref, dvt_ref, dsk_ref, kcat, vcat):
        i = pl.program_id(1)

        @pl.when(i == 0)
        def _():
            dkt_ref[...] = jnp.zeros_like(dkt_ref)
            dvt_ref[...] = jnp.zeros_like(dvt_ref)
            dsk_ref[...] = jnp.zeros_like(dsk_ref)

        _win_cat(kcat, (kp, kc, kn, kx))
        _win_cat(vcat, (vp, vc, vn, vx))
        qs = _stack_groups(q_ref[...], MXU_DTYPE)
        do32 = _stack_groups(do_ref[...], F32)
        delta = jnp.sum(do32 * _stack_groups(o_ref[...], F32), axis=1, keepdims=True)
        dos = do32.astype(MXU_DTYPE)
        lse_c = lse_ref[0, 0]
        s = _dot_nt(qs, kcat[...])
        s = jnp.where(_win_mask(i, s_rows, s.shape, False), s, -jnp.inf)
        p = jnp.exp(s - lse_c)
        ds = p * (_dot_nt(dos, vcat[...]) - delta)
        dq_ref[...] = _fold_groups(_dot(ds, kcat[...]), TW)
        dvt = jnp.dot(_compact_t(dot_ref[...], MXU_DTYPE), p.astype(MXU_DTYPE), preferred_element_type=F32)
        dkt = jnp.dot(_compact_t(qt_ref[...], MXU_DTYPE), ds.astype(MXU_DTYPE), preferred_element_type=F32)
        lat = pl.ds(pl.multiple_of(i * TW, TW), WLAT)
        dkt_ref[0, :, lat] += dkt[:, :WLAT]
        dvt_ref[0, :, lat] += dvt[:, :WLAT]
        dkt_ref[0, :, ctx0:ctx0 + c_rows] += dkt[:, WLAT:]
        dvt_ref[0, :, ctx0:ctx0 + c_rows] += dvt[:, WLAT:]
        dsk_ref[0] += -(jnp.exp(sink_ref[0][:, 0:1] - lse_c) * delta)

    rows, _ = _win_specs(s_rows, c_rows)
    qspec = pl.BlockSpec((TW, KV_W), lambda h, i: (i, h))
    tspec = pl.BlockSpec((KV_W, TW), lambda h, i: (h, i))
    col = pl.BlockSpec((1, 1, WR, 1), lambda h, i: (h, i, 0, 0))
    kvt = pl.BlockSpec((1, HEAD_DIM, WIN + n), lambda h, i: (h, 0, 0))
    return pl.pallas_call(
        kern, name="win_bwd", grid=(N_KV, nt),
        in_specs=[qspec, tspec] + rows + rows + [pl.BlockSpec((1, WR, LANES), lambda h, i: (h, 0, 0)), qspec, qspec, tspec, col],
        out_specs=[qspec, kvt, kvt, pl.BlockSpec((1, WR, 1), lambda h, i: (h, 0, 0))],
        out_shape=[SDS((s_rows, Q_W), F32), SDS((N_KV, HEAD_DIM, WIN + n), F32), SDS((N_KV, HEAD_DIM, WIN + n), F32),
                   SDS((N_KV, WR, 1), F32)],
        scratch_shapes=[pltpu.VMEM((nkeys, KV_W), MXU_DTYPE), pltpu.VMEM((nkeys, KV_W), MXU_DTYPE)],
        compiler_params=_cparams(("arbitrary", "arbitrary")),
    )(q, q_t, kt, kt, kt, kt, vt, vt, vt, vt, sinkcol, o, do, do_t, lse)


def adamw(name, w, m, v, grads):
    r, wd = w.shape
    tr = _pick(r, [t for t in ELEMENTWISE_ROWS if t * wd * 4 <= ELEMENTWISE_BLOCK_BYTES])
    stacked = not isinstance(grads, (list, tuple))
    ng = grads.shape[0] if stacked else len(grads)

    def kern(*refs):
        w_ref, m_ref, v_ref = refs[:3]
        g_refs = refs[3:-4]
        g_out, d_out, m_out, v_out = refs[-4:]
        if stacked:
            g = g_refs[0][0]
            for k in range(1, ng):
                g = g + g_refs[0][k]
        else:
            g = g_refs[0][...]
            for gr in g_refs[1:]:
                g = g + gr[...]
        wv = w_ref[...]
        mn = ADAM_B1 * m_ref[...] + (1.0 - ADAM_B1) * g
        vn = ADAM_B2 * v_ref[...] + (1.0 - ADAM_B2) * (g * g)
        m_hat = mn / (1.0 - ADAM_B1 ** ADAM_STEP)
        v_hat = vn / (1.0 - ADAM_B2 ** ADAM_STEP)
        g_out[...] = g
        d_out[...] = -ADAM_LR * (m_hat / (jnp.sqrt(v_hat) + ADAM_EPS) + ADAM_WD * wv)
        m_out[...] = mn
        v_out[...] = vn

    spec = pl.BlockSpec((tr, wd), lambda i: (i, 0))
    gspecs = [pl.BlockSpec((ng, tr, wd), lambda i: (0, i, 0))] if stacked else [spec] * ng
    return pl.pallas_call(
        kern, name=name, grid=(r // tr,), in_specs=[spec] * 3 + gspecs, out_specs=[spec] * 4,
        out_shape=[SDS((r, wd), F32)] * 4, compiler_params=_cparams(("parallel",)),
    )(w, m, v, *([grads] if stacked else grads))


def add2(name, a, b):
    k, r, w = a.shape
    tr = _pick(r, [t for t in ELEMENTWISE_ROWS if t * w * 4 <= ELEMENTWISE_BLOCK_BYTES])

    def kern(a_ref, b_ref, o_ref):
        o_ref[...] = a_ref[...] + b_ref[...]

    spec = pl.BlockSpec((1, tr, w), lambda s, i: (s, i, 0))
    return pl.pallas_call(kern, name=name, grid=(k, r // tr), in_specs=[spec, spec], out_specs=spec,
                          out_shape=SDS(a.shape, a.dtype), compiler_params=_cparams(("parallel", "parallel")))(a, b)


def _rep4(a, off):
    return jnp.concatenate([a[:, off + HEAD_DIM * h: off + HEAD_DIM * (h + 1)] for h in range(N_KV) for _ in range(GROUPS)], axis=1)


def _extend_cols(a):
    return jnp.concatenate([a[:, 0:OFF_KA], _rep4(a, OFF_KA), _rep4(a, OFF_VA), a[:, OFF_QB:OFF_KB],
                            _rep4(a, OFF_KB), _rep4(a, OFF_VB), a[:, OFF_GA:]], axis=1)


def _fold4(a, off):
    r = a.shape[0]
    return a[:, off:off + Q_W].reshape(r, N_KV, GROUPS, HEAD_DIM).sum(axis=2).reshape(r, N_KV * HEAD_DIM)


def _fold_cols(a):
    return jnp.concatenate([a[:, X_QA:X_QA + Q_W], _fold4(a, X_KA), _fold4(a, X_VA), a[:, X_QB:X_QB + Q_W],
                            _fold4(a, X_KB), _fold4(a, X_VB), a[:, X_GL:]], axis=1)


def _rope_tables(s_rows, c_rows):
    n_rows = s_rows // GRID_W
    n_freq = HEAD_DIM // 4
    inv_freq = ROPE_THETA ** (-jnp.arange(n_freq, dtype=F32) / n_freq)
    ang_r = jnp.arange(n_rows, dtype=jnp.int32).astype(F32)[:, None] * inv_freq
    ang_c = jnp.arange(GRID_W, dtype=jnp.int32).astype(F32)[:, None] * inv_freq
    by_row = lambda t: jnp.repeat(t, GRID_W, axis=0)
    by_col = lambda t: jnp.tile(t, (n_rows, 1))
    cos = jnp.concatenate([by_row(jnp.cos(ang_r))] * 2 + [by_col(jnp.cos(ang_c))] * 2, axis=1)
    sin_r, sin_c = by_row(jnp.sin(ang_r)), by_col(jnp.sin(ang_c))
    sin = jnp.concatenate([-sin_r, sin_r, -sin_c, sin_c], axis=1)
    cos = jnp.concatenate([cos, jnp.ones((c_rows, HEAD_DIM), F32)], axis=0)
    sin = jnp.concatenate([sin, jnp.zeros((c_rows, HEAD_DIM), F32)], axis=0)
    return jnp.concatenate([cos, cos], axis=1), jnp.concatenate([sin, sin], axis=1)


def _ff_pad_cols(a):
    r = a.shape[0]
    a = jnp.pad(a.reshape(r, N_DEV, FF_SHARD), ((0, 0), (0, 0), (0, FF_SHARD_PAD - FF_SHARD)))
    return a.reshape(r, 2 * FF)


def _ff_unpad_cols(a):
    r = a.shape[0]
    return a.reshape(r, N_DEV, FF_SHARD_PAD)[:, :, :FF_SHARD].reshape(r, 2 * D_FF)


def _ff_pad_rows(a):
    c = a.shape[1]
    a = jnp.pad(a.reshape(N_DEV // 2, FF_SHARD, c), ((0, 0), (0, FF_SHARD_PAD - FF_SHARD), (0, 0)))
    return a.reshape(FF, c)


def _ff_unpad_rows(a):
    c = a.shape[1]
    return a.reshape(N_DEV // 2, FF_SHARD_PAD, c)[:, :FF_SHARD].reshape(D_FF, c)


BIG = (("w_in", (D, IN_COLS // N_DEV)), ("w_branch_a", (Q_W, D // N_DEV)), ("w_branch_b", (Q_W, D // N_DEV)),
       ("w_out", (D // N_DEV, D)), ("w_up", (D, FF_SHARD_PAD)), ("w_down", (D_FF // N_DEV, D)))
BIG_SIZES = tuple(int(np.prod(s)) for _, s in BIG)
BIG_ROWS = sum(BIG_SIZES) // LANES


def _pack_big(parts):
    return jnp.concatenate([p.reshape(-1) for p in parts]).reshape(BIG_ROWS, LANES)


def _unpack_big(flat):
    lead = flat.shape[:-2]
    f = flat.reshape(*lead, BIG_ROWS * LANES)
    out, off = [], 0
    for (_, shp), sz in zip(BIG, BIG_SIZES, strict=True):
        out.append(f[..., off:off + sz].reshape(*lead, *shp))
        off += sz
    return out


def _cols_to_full(g):
    return jnp.transpose(g, (1, 0, 2)).reshape(g.shape[1], -1)


def _full_to_cols(a):
    r, c = a.shape
    return jnp.transpose(a.reshape(r, N_DEV, c // N_DEV), (1, 0, 2))


SMALL = (("c_ctx", D), ("b_mod", N_MOD * D), ("b_in", IN_COLS), ("attn_sink", N_HEADS), ("q_norm_g", HEAD_DIM),
         ("k_norm_g", HEAD_DIM), ("ln1_g", D), ("ln1_b", D), ("conv_w", 3 * 2 * D_FF // N_DEV), ("conv_b", 2 * D_FF),
         ("ln2_g", D), ("ln2_b", D))
SMALL_TOTAL = sum(n for _, n in SMALL)
SMALL_ROWS = -(-SMALL_TOTAL // (8 * LANES)) * 8


def _pack_small(parts):
    flat = jnp.concatenate([p.reshape(-1).astype(F32) for p in parts])
    return jnp.pad(flat, (0, SMALL_ROWS * LANES - flat.shape[0])).reshape(SMALL_ROWS, LANES)


def _unpack_small(packed):
    f = packed.reshape(-1)
    out, off = {}, 0
    for name, n in SMALL:
        out[name] = f[off:off + n]
        off += n
    return out


RED = (("c_ctx", D), ("b_in", IN_COLS), ("attn_sink", N_HEADS), ("q_norm_g", HEAD_DIM), ("k_norm_g", HEAD_DIM),
       ("ln1_g", D), ("ln1_b", D), ("conv_w", 3 * 2 * FF), ("conv_b", 2 * FF), ("ln2_g", D), ("ln2_b", D))
RED_TOTAL = sum(n for _, n in RED)
RED_ROWS = -(-RED_TOTAL // (8 * LANES)) * 8


def sum8(name, g):
    _, r, w = g.shape

    def kern(g_ref, o_ref):
        acc = g_ref[0]
        for k in range(1, N_DEV):
            acc = acc + g_ref[k]
        o_ref[...] = acc

    return pl.pallas_call(kern, name=name, out_shape=SDS((r, w), F32))(g)


def _local_step(x, ctx, target, modrows, weights, small):
    s_rows, c_rows = x.shape[0], ctx.shape[0]
    n = s_rows + c_rows
    nl = s_rows // TM
    w_in, wba, wbb, w_out, w_up, w_down = weights
    f = lambda a: a.reshape(1, -1).astype(F32)
    b_in, ln1_g, ln1_b, ln2_g, ln2_b, conv_b = (f(small[k]) for k in ("b_in", "ln1_g", "ln1_b", "ln2_g", "ln2_b", "conv_b"))
    conv_w8 = jnp.pad(small["conv_w_full"], ((0, 5), (0, 0)))
    qg = jnp.tile(small["q_norm_g"].reshape(1, HEAD_DIM), (1, N_HEADS))
    kg = jnp.tile(small["k_norm_g"].reshape(1, HEAD_DIM), (1, N_HEADS))
    sink_rep = jnp.repeat(small["attn_sink"].reshape(N_KV, GROUPS), TW, axis=1)
    sinkrow = sink_rep.reshape(N_KV, 1, WR)
    sinkcol = jnp.broadcast_to(sink_rep[:, :, None], (N_KV, WR, LANES))
    bd = jnp.kron(jnp.eye(N_HEADS, dtype=F32), jnp.ones((HEAD_DIM, HEAD_DIM), F32)).astype(BF16)
    cos, sin = _rope_tables(s_rows, c_rows)
    w_ext = _extend_cols(w_in)
    b_ext = _extend_cols(b_in)
    xa = jnp.concatenate([x, ctx], axis=0)

    hb, qa, kat, vat, qb, kbt, vbt, tq, rq, tk, rk, gl = inproj_fwd(xa, cos, sin, modrows, w_ext, b_ext, qg, kg, bd, nl)
    compact_t = lambda t: jnp.stack([t[:, h * KV_W:h * KV_W + HEAD_DIM].T for h in range(N_KV)])
    oa, lse_a = win_fwd(qa, kat, compact_t(vat), sinkrow, s_rows, c_rows)
    vb_t = compact_t(vbt)
    ob, lse_b = glob_fwd(qb, kbt, vb_t, s_rows)
    tqb = _pick(s_rows, GLOB_BWD_TQ)
    lse_b = lse_b.reshape(N_KV, s_rows // tqb, GROUPS * tqb, 1)
    ya, yb, mrg, y, xhat1, rstd1 = merge_fwd(oa, ob, gl, x, modrows, wba, wbb, w_out, s_rows)
    h2, u0 = ffn_up_fwd(xhat1, modrows, ln1_g, ln1_b, w_up, s_rows)
    a = conv_swiglu_fwd(u0, conv_w8, conv_b, s_rows)
    (dr2, dy2), (loss, dln2_g, dln2_b, dgate2) = ffn_down_loss(a, xhat1, target, modrows, ln1_g, ln1_b, ln2_g, ln2_b, w_down, s_rows)

    da = ffn_down_bwd(dy2, w_down.T, s_rows)
    dw_down = mm_tn("dw_down", a, dy2, s_rows)
    du0, (dconv_b, dcw0, dcw1, dcw2) = swiglu_conv_bwd(u0, da, conv_w8, conv_b, s_rows)
    dw_up = mm_tn("dw_up", h2, du0, s_rows)
    (dy, dxp), (dscale2, dshift2, dln1_g, dln1_b, dgate1) = ffn_up_ln1_bwd(du0, dr2, xhat1, y, rstd1, modrows, ln1_g, ln1_b, w_up.T, s_rows)
    dya, dyb, dgl, doa, dob = merge_bwd(dy, ya, yb, gl, w_out.T, wba.T, wbb.T, s_rows)
    dw_out = mm_tn("dw_out", mrg, dy, s_rows)
    dwba = mm_tn("dw_branch_a", oa, dya, s_rows)
    dwbb = mm_tn("dw_branch_b", ob, dyb, s_rows)

    head_rows = lambda t: t.reshape(N_KV * HEAD_DIM, t.shape[-1])
    place = np.zeros((N_KV * HEAD_DIM, Q_W), np.float32)
    for h in range(N_KV):
        place[h * HEAD_DIM + np.arange(HEAD_DIM), h * KV_W + np.arange(HEAD_DIM)] = 1.0
    place = jnp.asarray(place, BF16)
    dqa, dka_t, dva_t, dsk = win_bwd(qa, qa[:s_rows].T, kat, vat, sinkcol, oa, doa, doa.astype(MXU_DTYPE).T, lse_a,
                                     s_rows, c_rows)
    dka_t, dva_t = head_rows(dka_t[:, :, WIN:]), head_rows(dva_t[:, :, WIN:])
    delta_b = attn_delta(ob, dob, s_rows)
    qb_t = qb[:s_rows].T
    dob_t = dob.astype(MXU_DTYPE).T
    heads = [glob_bwd(qb, qb_t, kbt, vbt, dob, dob_t, lse_b, delta_b, h, s_rows) for h in range(N_KV)]
    dqb = [hd[0] for hd in heads]
    dkb_t = jnp.concatenate([hd[1] for hd in heads], axis=0)
    dvb_t = jnp.concatenate([hd[2] for hd in heads], axis=0)
    dproj, (db_ext, dqg, dkg) = qk_bwd(dqa, dka_t, dva_t, dqb, dkb_t, dvb_t, dgl, tq, rq, tk, rk, cos, sin, qg, kg, bd,
                                       place, nl, n)
    w_ext_t = w_ext.T
    (grad_x,), (dscale1, dshift1) = inproj_bwd("inproj_bwd", dproj, xa, dxp, modrows, w_ext_t, ntiles=nl, tile_off=0,
                                               is_ctx=False, out_rows=s_rows)
    _, (dscale_c, dshift_c) = inproj_bwd("inproj_bwd_ctx", dproj, xa, None, modrows, w_ext_t, ntiles=c_rows // TM,
                                         tile_off=nl, is_ctx=True, out_rows=0)
    dw_in = _fold_cols(mm_tn("dw_in", hb, dproj, n))

    dmod = jnp.concatenate([dshift1, dscale1, dgate1, dshift2, dscale2, dgate2], axis=1)
    dmod_c = jnp.concatenate([dshift_c, dscale_c, jnp.zeros((1, (N_MOD - 2) * D), F32)], axis=1)
    fold_g = lambda t: t.reshape(N_HEADS, HEAD_DIM).sum(axis=0)
    red = {
        "b_in": _fold_cols(db_ext), "attn_sink": dsk.reshape(N_HEADS, TW).sum(axis=1), "q_norm_g": fold_g(dqg),
        "k_norm_g": fold_g(dkg), "ln1_g": dln1_g, "ln1_b": dln1_b, "conv_w": jnp.concatenate([dcw0, dcw1, dcw2], axis=0),
        "conv_b": dconv_b, "ln2_g": dln2_g, "ln2_b": dln2_b,
    }
    return loss[0, 0], grad_x, (dw_in, dwba, dwbb, dw_out, dw_up, dw_down), dmod, dmod_c, red


def kernel(x, c, ctx, c_ctx, w_mod, b_mod, w_in, b_in, attn_sink, q_norm_g, k_norm_g, w_branch_a, w_branch_b, w_out, ln1_g, ln1_b, w_up, conv_w, conv_b, w_down, ln2_g, ln2_b, loss_target, m_c_ctx, m_w_mod, m_b_mod, m_w_in, m_b_in, m_attn_sink, m_q_norm_g, m_k_norm_g, m_w_branch_a, m_w_branch_b, m_w_out, m_ln1_g, m_ln1_b, m_w_up, m_conv_w, m_conv_b, m_w_down, m_ln2_g, m_ln2_b, v_c_ctx, v_w_mod, v_b_mod, v_w_in, v_b_in, v_attn_sink, v_q_norm_g, v_k_norm_g, v_w_branch_a, v_w_branch_b, v_w_out, v_ln1_g, v_ln1_b, v_w_up, v_conv_w, v_conv_b, v_w_down, v_ln2_g, v_ln2_b):
    ax, ay, ac = (lax.axis_index(a) for a in AXES)
    me = 4 * ax + 2 * ay + ac
    chip = 2 * ax + ay
    mod_w = N_MOD * D // N_DEV
    params = dict(c_ctx=c_ctx, w_mod=w_mod, b_mod=b_mod, w_in=w_in, b_in=b_in, attn_sink=attn_sink, q_norm_g=q_norm_g,
                  k_norm_g=k_norm_g, w_branch_a=w_branch_a, w_branch_b=w_branch_b, w_out=w_out, ln1_g=ln1_g, ln1_b=ln1_b,
                  w_up=w_up, conv_w=conv_w, conv_b=conv_b, w_down=w_down, ln2_g=ln2_g, ln2_b=ln2_b)
    mom_m = dict(c_ctx=m_c_ctx, w_mod=m_w_mod, b_mod=m_b_mod, w_in=m_w_in, b_in=m_b_in, attn_sink=m_attn_sink,
                 q_norm_g=m_q_norm_g, k_norm_g=m_k_norm_g, w_branch_a=m_w_branch_a, w_branch_b=m_w_branch_b, w_out=m_w_out,
                 ln1_g=m_ln1_g, ln1_b=m_ln1_b, w_up=m_w_up, conv_w=m_conv_w, conv_b=m_conv_b, w_down=m_w_down,
                 ln2_g=m_ln2_g, ln2_b=m_ln2_b)
    mom_v = dict(c_ctx=v_c_ctx, w_mod=v_w_mod, b_mod=v_b_mod, w_in=v_w_in, b_in=v_b_in, attn_sink=v_attn_sink,
                 q_norm_g=v_q_norm_g, k_norm_g=v_k_norm_g, w_branch_a=v_w_branch_a, w_branch_b=v_w_branch_b, w_out=v_w_out,
                 ln1_g=v_ln1_g, ln1_b=v_ln1_b, w_up=v_w_up, conv_w=v_conv_w, conv_b=v_conv_b, w_down=v_w_down,
                 ln2_g=v_ln2_g, ln2_b=v_ln2_b)
    big_names = [nm for nm, _ in BIG]

    def shard(tree, nm):
        t = tree[nm][0]
        return jnp.pad(t, ((0, 0), (0, FF_SHARD_PAD - FF_SHARD))) if nm == "w_up" else t

    wg = all_gather("ag_weights", _pack_big([shard(params, nm).astype(MXU_DTYPE) for nm in big_names]))
    g_in, g_ba, g_bb, g_out, g_up, g_down = _unpack_big(wg)
    weights = (_cols_to_full(g_in), _cols_to_full(g_ba), _cols_to_full(g_bb), g_out.reshape(D, D), _cols_to_full(g_up),
               _ff_pad_rows(g_down.reshape(D_FF, D)))

    cw_pad = jnp.pad(conv_w[0], ((0, 5), (0, FF_SHARD_PAD - FF_SHARD)))
    c_cw = all_gather("ag_c_conv_w", jnp.concatenate([c.reshape(8, LANES), cw_pad], axis=1))
    c_all = c_cw[:, :, :LANES].reshape(N_DEV, D)
    cs = jnp.concatenate([c_all, c_ctx.reshape(1, D), jnp.zeros((7, D), F32)], axis=0)
    w_mod_sh = w_mod[0]
    b_mod_sh = lax.dynamic_slice(b_mod, (0, me * mod_w), (1, mod_w))
    mod_part = mod_fwd(cs, w_mod_sh, b_mod_sh)
    mg = all_gather("ag_mod", mod_part.reshape(16 * mod_w // LANES, LANES)).reshape(N_DEV, 16, mod_w)
    mod = lax.dynamic_index_in_dim(mg, me, axis=1, keepdims=False).reshape(N_MOD, D)
    mod_c = mg[:, 8, :].reshape(N_MOD, D)
    modrows = jnp.stack([mod[0], mod[1], mod_c[0], mod_c[1], mod[2], mod[3], mod[4], mod[5]], axis=0)

    conv_w_full = _cols_to_full(c_cw[:, :3, LANES:])
    small = dict(b_in=b_in, ln1_g=ln1_g, ln1_b=ln1_b, ln2_g=ln2_g, ln2_b=ln2_b, conv_b=_ff_pad_cols(conv_b),
                 conv_w_full=conv_w_full, q_norm_g=q_norm_g, k_norm_g=k_norm_g, attn_sink=attn_sink)
    loss, grad_x, big_grads, dmod, dmod_c, red = _local_step(x[0], ctx[0], loss_target[0], modrows, weights, small)
    loss = lax.psum(loss, AXES)

    dm = all_gather("ag_dmod", jnp.concatenate([dmod, dmod_c], axis=0).reshape(2 * N_MOD * D // LANES, LANES))
    dm = dm.reshape(N_DEV, 2, N_MOD * D)
    dm_all = jnp.concatenate([dm[:, 0], dm[:, 1]], axis=0)
    dm_sh = lax.dynamic_slice(dm_all, (0, me * mod_w), (16, mod_w))
    dw_mod, dcc, db_mod = mod_bwd(cs, w_mod_sh, dm_sh, dm_all)
    red["c_ctx"] = dcc[8]

    red_vec = jnp.concatenate([red[nm].reshape(-1) for nm, _ in RED])
    red_vec = jnp.pad(red_vec, (0, RED_ROWS * LANES - RED_TOTAL)).reshape(RED_ROWS, LANES)
    red_sum = sum8("sum_small", all_gather("ag_small", red_vec)).reshape(-1)
    gsm, off = {}, 0
    for nm, k in RED:
        gsm[nm] = red_sum[off:off + k]
        off += k
    gsm["b_mod"] = db_mod.reshape(-1)
    gsm["conv_b"] = _ff_unpad_cols(gsm["conv_b"].reshape(1, 2 * FF))
    gsm["conv_w"] = lax.dynamic_slice(gsm["conv_w"].reshape(3, 2 * FF), (0, me * FF_SHARD_PAD), (3, FF_SHARD_PAD))[:, :FF_SHARD]
    sm_names = [nm for nm, _ in SMALL]
    gs, ds, ms, vs = adamw("adamw_small", _pack_small([params[nm] for nm in sm_names]),
                           _pack_small([mom_m[nm] for nm in sm_names]), _pack_small([mom_v[nm] for nm in sm_names]),
                           [_pack_small([gsm[nm] for nm in sm_names])])
    sm_out = [_unpack_small(t) for t in (gs, ds, ms, vs)]

    dw_in, dwba, dwbb, dw_out, dw_up, dw_down = big_grads
    slabs = jnp.concatenate([t.reshape(N_DEV, -1) for t in (
        _full_to_cols(dw_in), _full_to_cols(dwba), _full_to_cols(dwbb), dw_out, _full_to_cols(dw_up),
        _ff_unpad_rows(dw_down))], axis=1)
    by_core = slabs.reshape(4, 2, BIG_ROWS, LANES)
    keep = lax.dynamic_index_in_dim(by_core, ac, axis=1, keepdims=False)
    give = lax.dynamic_index_in_dim(by_core, 1 - ac, axis=1, keepdims=False)
    got = exchange("rs_sibling", give.reshape(1, 4 * BIG_ROWS, LANES), to_chips=False).reshape(4, BIG_ROWS, LANES)
    pair = add2("rs_pair_sum", keep, got)
    outbox = jnp.stack([lax.dynamic_index_in_dim(pair, jnp.bitwise_xor(chip, m), axis=0, keepdims=False) for m in (1, 2, 3)])
    inbox = exchange("rs_chips", outbox.astype(MXU_DTYPE), to_chips=True)
    mine = lax.dynamic_index_in_dim(pair, chip, axis=0, keepdims=False)
    gb, db, mb, vb = adamw("adamw_big", _pack_big([shard(params, nm) for nm in big_names]),
                           _pack_big([shard(mom_m, nm) for nm in big_names]), _pack_big([shard(mom_v, nm) for nm in big_names]),
                           [mine, inbox[0], inbox[1], inbox[2]])
    big_out = [dict(zip(big_names, _unpack_big(t), strict=True)) for t in (gb, db, mb, vb)]
    for out in big_out:
        out["w_up"] = out["w_up"][:, :FF_SHARD]
    gm, dmo, mmo, vmo = adamw("adamw_mod", w_mod[0], m_w_mod[0], v_w_mod[0], [dw_mod])
    mod_out = (gm, dmo, mmo, vmo)

    order = ["c_ctx", "w_mod", "b_mod", "w_in", "b_in", "attn_sink", "q_norm_g", "k_norm_g", "w_branch_a", "w_branch_b",
             "w_out", "ln1_g", "ln1_b", "w_up", "conv_w", "conv_b", "w_down", "ln2_g", "ln2_b"]
    results = [loss, grad_x[None]]
    for kind in range(4):
        for nm in order:
            if nm == "w_mod":
                val = mod_out[kind]
            elif nm in big_out[kind]:
                val = big_out[kind][nm]
            else:
                val = sm_out[kind][nm]
            results.append(val.reshape(params[nm].shape))
    return tuple(results)
```
